```python
import jax, jax.numpy as jnp
from jax import lax
import numpy as np

D_MODEL = 1024
BATCH = 8
SEQ = 8192
DEPTH = 1

N_MEM = 256
HEAD_DIM = 128
HEADS_PER_GROUP = 4
DIL_GROUPS = ((128, 1), (512, 4), (2048, 16))
N_GROUPS = len(DIL_GROUPS)
ATTN_HEADS = N_GROUPS * HEADS_PER_GROUP
ATTN_WIDTH = ATTN_HEADS * HEAD_DIM
ATTN_OUT = HEADS_PER_GROUP * HEAD_DIM
ROT_DIM = HEAD_DIM // 4
ROPE_THETA = 500000.0
CONV_CH = 3 * D_MODEL // 4
CONV_K = 31
N_BRANCH = 2
IN_SPLITS = (ATTN_WIDTH, ATTN_WIDTH, ATTN_WIDTH, CONV_CH, CONV_CH, D_MODEL, D_MODEL)
IN_WIDTH = sum(IN_SPLITS)
CROSS_HEADS = 4
CROSS_HEAD_DIM = D_MODEL // CROSS_HEADS
D_FF = 4 * D_MODEL
EPS = 1e-6

kernel_name = 'hybrid_dilated_attn_conformer_conv_gated'


def rmsnorm(x, g):
    xf = x.astype(jnp.float32)
    y = xf * lax.rsqrt(jnp.mean(xf * xf, axis=-1, keepdims=True) + EPS) * g.astype(jnp.float32)
    return y.astype(x.dtype)


def layernorm(x, g, b):
    xf = x.astype(jnp.float32)
    mu = jnp.mean(xf, axis=-1, keepdims=True)
    var = jnp.mean(jnp.square(xf - mu), axis=-1, keepdims=True)
    y = (xf - mu) * lax.rsqrt(var + EPS) * g.astype(jnp.float32) + b.astype(jnp.float32)
    return y.astype(x.dtype)


def rope_partial(t, pos):
    half = ROT_DIM // 2
    inv_freq = ROPE_THETA ** (-jnp.arange(0, ROT_DIM, 2, dtype=jnp.float32) / ROT_DIM)
    ang = pos[:, None] * inv_freq[None, :]
    cos = jnp.cos(ang)[None, :, None, :]
    sin = jnp.sin(ang)[None, :, None, :]
    tf = t.astype(jnp.float32)
    x1, x2, rest = tf[..., :half], tf[..., half:ROT_DIM], tf[..., ROT_DIM:]
    out = jnp.concatenate([x1 * cos - x2 * sin, x2 * cos + x1 * sin, rest], axis=-1)
    return out.astype(t.dtype)


def dilated_window_attention(q, k, v, window, dilation):
    B, S, H, Dh = q.shape
    L = window // dilation
    span = dilation * L
    Sp = -(-S // span) * span
    M = Sp // dilation
    nb = M // L
    pad = ((0, 0), (0, Sp - S), (0, 0), (0, 0))

    def to_blocks(t):
        t = jnp.pad(t, pad).reshape(B, M, dilation, H, Dh)
        t = t.transpose(0, 3, 2, 1, 4)
        return t.reshape(B, H, dilation, nb, L, Dh)

    def with_prev(t):
        prev = jnp.pad(t, ((0, 0), (0, 0), (0, 0), (1, 0), (0, 0), (0, 0)))[:, :, :, :-1]
        return jnp.concatenate([prev, t], axis=-2)

    qb = to_blocks(q)
    kw = with_prev(to_blocks(k))
    vw = with_prev(to_blocks(v))
    s = jnp.einsum('bhrnqe,bhrnke->bhrnqk', qb, kw).astype(jnp.float32) * (Dh ** -0.5)
    qi = jnp.arange(L)[:, None]
    kj = jnp.arange(2 * L)[None, :]
    band = (kj >= qi) & (kj <= qi + L)
    first = (jnp.arange(nb)[:, None, None] == 0) & (kj[None] < L)
    mask = band[None] & jnp.logical_not(first)
    s = jnp.where(mask, s, -jnp.inf)
    lse = jax.nn.logsumexp(s, axis=-1)
    p = jnp.exp(s - lse[..., None])
    o = jnp.einsum('bhrnqk,bhrnke->bhrnqe', p.astype(v.dtype), vw)
    o = o.reshape(B, H, dilation, M, Dh).transpose(0, 3, 2, 1, 4).reshape(B, Sp, H, Dh)[:, :S]
    lse = lse.reshape(B, H, dilation, M).transpose(0, 3, 2, 1).reshape(B, Sp, H)[:, :S]
    return o, lse


def _fwd_setup_inputs(seed: int = 0) -> dict:
    key = jax.random.key(seed)
    ks = jax.random.split(key, 24)
    f32 = jnp.float32
    nrm = lambda k, shape, fan: jax.random.normal(k, shape, f32) * (fan ** -0.5)
    gain = lambda k, shape: 1.0 + 0.01 * jax.random.normal(k, shape, f32)
    small = lambda k, shape: 0.01 * jax.random.normal(k, shape, f32)
    return {
        'x': jax.random.normal(ks[0], (BATCH, SEQ, D_MODEL), f32),
        'mem': jax.random.normal(ks[1], (BATCH, N_MEM, D_MODEL), f32),
        'g_mix': gain(ks[2], (DEPTH, D_MODEL)),
        'w_in': nrm(ks[3], (DEPTH, D_MODEL, IN_WIDTH), D_MODEL),
        'b_gate': small(ks[4], (DEPTH, N_BRANCH * D_MODEL)),
        'conv_w': nrm(ks[5], (DEPTH, CONV_K, CONV_CH), CONV_K),
        'conv_b': small(ks[6], (DEPTH, CONV_CH)),
        'conv_ln_g': gain(ks[7], (DEPTH, CONV_CH)),
        'conv_ln_b': small(ks[8], (DEPTH, CONV_CH)),
        'w_attn_proj': nrm(ks[9], (DEPTH, ATTN_OUT, D_MODEL), ATTN_OUT),
        'w_conv_proj': nrm(ks[10], (DEPTH, CONV_CH, D_MODEL), CONV_CH),
        'w_out': nrm(ks[11], (DEPTH, D_MODEL, D_MODEL), D_MODEL),
        'g_cross': gain(ks[12], (DEPTH, D_MODEL)),
        'g_mem': gain(ks[13], (DEPTH, D_MODEL)),
        'w_cq': nrm(ks[14], (DEPTH, D_MODEL, D_MODEL), D_MODEL),
        'w_ckv': nrm(ks[15], (DEPTH, D_MODEL, 2 * D_MODEL), D_MODEL),
        'w_co': nrm(ks[16], (DEPTH, D_MODEL, D_MODEL), D_MODEL),
        'g_mlp': gain(ks[17], (DEPTH, D_MODEL)),
        'w_up': nrm(ks[18], (DEPTH, D_MODEL, D_FF), D_MODEL),
        'w_down': nrm(ks[19], (DEPTH, D_FF, D_MODEL), D_FF),
        'g_final': gain(ks[20], (D_MODEL,)),
    }


def _fwd_reference(x, mem, g_mix, w_in, b_gate, conv_w, conv_b, conv_ln_g, conv_ln_b, w_attn_proj,
              w_conv_proj, w_out, g_cross, g_mem, w_cq, w_ckv, w_co, g_mlp, w_up, w_down, g_final):
    B, S, _ = x.shape
    pos = jnp.arange(S, dtype=jnp.float32)
    split_pts = [int(v) for v in np.cumsum(IN_SPLITS)[:-1]]
    for l in range(DEPTH):
        u = rmsnorm(x, g_mix[l])
        z = u @ w_in[l]
        q, k, v, glu_a, glu_b, gate_a, gate_b = jnp.split(z, split_pts, axis=-1)
        q = rope_partial(q.reshape(B, S, ATTN_HEADS, HEAD_DIM), pos)
        k = rope_partial(k.reshape(B, S, ATTN_HEADS, HEAD_DIM), pos)
        v = v.reshape(B, S, ATTN_HEADS, HEAD_DIM)
        q = q.reshape(B, S, N_GROUPS, HEADS_PER_GROUP, HEAD_DIM)
        k = k.reshape(B, S, N_GROUPS, HEADS_PER_GROUP, HEAD_DIM)
        v = v.reshape(B, S, N_GROUPS, HEADS_PER_GROUP, HEAD_DIM)
        outs, lses = [], []
        for g, (win, dil) in enumerate(DIL_GROUPS):
            o_g, lse_g = dilated_window_attention(q[:, :, g], k[:, :, g], v[:, :, g], win, dil)
            outs.append(o_g)
            lses.append(lse_g)
        wts = jax.nn.softmax(jnp.stack(lses, axis=0), axis=0)
        attn = jnp.sum(wts[..., None] * jnp.stack(outs, axis=0).astype(jnp.float32), axis=0)
        y_attn = attn.astype(x.dtype).reshape(B, S, ATTN_OUT) @ w_attn_proj[l]

        c = glu_a * jax.nn.sigmoid(glu_b)
        c = lax.conv_general_dilated(c, conv_w[l].astype(c.dtype)[:, None, :], window_strides=(1,),
                                     padding=[(CONV_K - 1, 0)], dimension_numbers=('NWC', 'WIO', 'NWC'),
                                     feature_group_count=CONV_CH) + conv_b[l]
        c = jax.nn.silu(layernorm(c, conv_ln_g[l], conv_ln_b[l]))
        y_conv = c @ w_conv_proj[l]

        bg_a, bg_b = jnp.split(b_gate[l], 2)
        merged = jax.nn.sigmoid(gate_a + bg_a) * y_attn + jax.nn.sigmoid(gate_b + bg_b) * y_conv
        x = x + merged @ w_out[l]

        uq = rmsnorm(x, g_cross[l])
        m = rmsnorm(mem, g_mem[l])
        cq = (uq @ w_cq[l]).reshape(B, S, CROSS_HEADS, CROSS_HEAD_DIM)
        ck, cv = jnp.split(m @ w_ckv[l], 2, axis=-1)
        ck = ck.reshape(B, N_MEM, CROSS_HEADS, CROSS_HEAD_DIM)
        cv = cv.reshape(B, N_MEM, CROSS_HEADS, CROSS_HEAD_DIM)
        sc = jnp.einsum('bshe,bmhe->bhsm', cq, ck).astype(jnp.float32) * (CROSS_HEAD_DIM ** -0.5)
        pc = jax.nn.softmax(sc, axis=-1).astype(cv.dtype)
        co = jnp.einsum('bhsm,bmhe->bshe', pc, cv).reshape(B, S, D_MODEL)
        x = x + co @ w_co[l]

        h = jnp.square(jax.nn.relu(rmsnorm(x, g_mlp[l]) @ w_up[l]))
        x = x + h @ w_down[l]
    return rmsnorm(x, g_final)


import jax as _jax
import jax.numpy as _jnp

TWIN_FORMAT = 'train_step'
FWD_PARAMS = ['x', 'mem', 'g_mix', 'w_in', 'b_gate', 'conv_w', 'conv_b', 'conv_ln_g', 'conv_ln_b', 'w_attn_proj', 'w_conv_proj', 'w_out', 'g_cross', 'g_mem', 'w_cq', 'w_ckv', 'w_co', 'g_mlp', 'w_up', 'w_down', 'g_final']
TWIN_WEIGHTS = ['g_mix', 'w_in', 'b_gate', 'conv_w', 'conv_b', 'conv_ln_g', 'conv_ln_b', 'w_attn_proj', 'w_conv_proj', 'w_out', 'g_cross', 'g_mem', 'w_cq', 'w_ckv', 'w_co', 'g_mlp', 'w_up', 'w_down', 'g_final']
TWIN_DIFF_INPUT = 'x'
TWIN_INPUTS = ['x', 'mem', 'g_mix', 'w_in', 'b_gate', 'conv_w', 'conv_b', 'conv_ln_g', 'conv_ln_b', 'w_attn_proj', 'w_conv_proj', 'w_out', 'g_cross', 'g_mem', 'w_cq', 'w_ckv', 'w_co', 'g_mlp', 'w_up', 'w_down', 'g_final', 'loss_target', 'm_g_mix', 'm_w_in', 'm_b_gate', 'm_conv_w', 'm_conv_b', 'm_conv_ln_g', 'm_conv_ln_b', 'm_w_attn_proj', 'm_w_conv_proj', 'm_w_out', 'm_g_cross', 'm_g_mem', 'm_w_cq', 'm_w_ckv', 'm_w_co', 'm_g_mlp', 'm_w_up', 'm_w_down', 'm_g_final', 'v_g_mix', 'v_w_in', 'v_b_gate', 'v_conv_w', 'v_conv_b', 'v_conv_ln_g', 'v_conv_ln_b', 'v_w_attn_proj', 'v_w_conv_proj', 'v_w_out', 'v_g_cross', 'v_g_mem', 'v_w_cq', 'v_w_ckv', 'v_w_co', 'v_g_mlp', 'v_w_up', 'v_w_down', 'v_g_final']
TWIN_OUTPUTS = ['loss', 'grad_x', 'grad_g_mix', 'grad_w_in', 'grad_b_gate', 'grad_conv_w', 'grad_conv_b', 'grad_conv_ln_g', 'grad_conv_ln_b', 'grad_w_attn_proj', 'grad_w_conv_proj', 'grad_w_out', 'grad_g_cross', 'grad_g_mem', 'grad_w_cq', 'grad_w_ckv', 'grad_w_co', 'grad_g_mlp', 'grad_w_up', 'grad_w_down', 'grad_g_final', 'delta_g_mix', 'delta_w_in', 'delta_b_gate', 'delta_conv_w', 'delta_conv_b', 'delta_conv_ln_g', 'delta_conv_ln_b', 'delta_w_attn_proj', 'delta_w_conv_proj', 'delta_w_out', 'delta_g_cross', 'delta_g_mem', 'delta_w_cq', 'delta_w_ckv', 'delta_w_co', 'delta_g_mlp', 'delta_w_up', 'delta_w_down', 'delta_g_final', 'new_m_g_mix', 'new_m_w_in', 'new_m_b_gate', 'new_m_conv_w', 'new_m_conv_b', 'new_m_conv_ln_g', 'new_m_conv_ln_b', 'new_m_w_attn_proj', 'new_m_w_conv_proj', 'new_m_w_out', 'new_m_g_cross', 'new_m_g_mem', 'new_m_w_cq', 'new_m_w_ckv', 'new_m_w_co', 'new_m_g_mlp', 'new_m_w_up', 'new_m_w_down', 'new_m_g_final', 'new_v_g_mix', 'new_v_w_in', 'new_v_b_gate', 'new_v_conv_w', 'new_v_conv_b', 'new_v_conv_ln_g', 'new_v_conv_ln_b', 'new_v_w_attn_proj', 'new_v_w_conv_proj', 'new_v_w_out', 'new_v_g_cross', 'new_v_g_mem', 'new_v_w_cq', 'new_v_w_ckv', 'new_v_w_co', 'new_v_g_mlp', 'new_v_w_up', 'new_v_w_down', 'new_v_g_final']
TWIN_LEAF_KINDS = {'loss': 'loss', 'grad_x': 'grad_x', 'grad_g_mix': 'grad_w', 'grad_w_in': 'grad_w', 'grad_b_gate': 'grad_w', 'grad_conv_w': 'grad_w', 'grad_conv_b': 'grad_w', 'grad_conv_ln_g': 'grad_w', 'grad_conv_ln_b': 'grad_w', 'grad_w_attn_proj': 'grad_w', 'grad_w_conv_proj': 'grad_w', 'grad_w_out': 'grad_w', 'grad_g_cross': 'grad_w', 'grad_g_mem': 'grad_w', 'grad_w_cq': 'grad_w', 'grad_w_ckv': 'grad_w', 'grad_w_co': 'grad_w', 'grad_g_mlp': 'grad_w', 'grad_w_up': 'grad_w', 'grad_w_down': 'grad_w', 'grad_g_final': 'grad_w', 'delta_g_mix': 'delta_w', 'delta_w_in': 'delta_w', 'delta_b_gate': 'delta_w', 'delta_conv_w': 'delta_w', 'delta_conv_b': 'delta_w', 'delta_conv_ln_g': 'delta_w', 'delta_conv_ln_b': 'delta_w', 'delta_w_attn_proj': 'delta_w', 'delta_w_conv_proj': 'delta_w', 'delta_w_out': 'delta_w', 'delta_g_cross': 'delta_w', 'delta_g_mem': 'delta_w', 'delta_w_cq': 'delta_w', 'delta_w_ckv': 'delta_w', 'delta_w_co': 'delta_w', 'delta_g_mlp': 'delta_w', 'delta_w_up': 'delta_w', 'delta_w_down': 'delta_w', 'delta_g_final': 'delta_w', 'new_m_g_mix': 'new_m', 'new_m_w_in': 'new_m', 'new_m_b_gate': 'new_m', 'new_m_conv_w': 'new_m', 'new_m_conv_b': 'new_m', 'new_m_conv_ln_g': 'new_m', 'new_m_conv_ln_b': 'new_m', 'new_m_w_attn_proj': 'new_m', 'new_m_w_conv_proj': 'new_m', 'new_m_w_out': 'new_m', 'new_m_g_cross': 'new_m', 'new_m_g_mem': 'new_m', 'new_m_w_cq': 'new_m', 'new_m_w_ckv': 'new_m', 'new_m_w_co': 'new_m', 'new_m_g_mlp': 'new_m', 'new_m_w_up': 'new_m', 'new_m_w_down': 'new_m', 'new_m_g_final': 'new_m', 'new_v_g_mix': 'new_v', 'new_v_w_in': 'new_v', 'new_v_b_gate': 'new_v', 'new_v_conv_w': 'new_v', 'new_v_conv_b': 'new_v', 'new_v_conv_ln_g': 'new_v', 'new_v_conv_ln_b': 'new_v', 'new_v_w_attn_proj': 'new_v', 'new_v_w_conv_proj': 'new_v', 'new_v_w_out': 'new_v', 'new_v_g_cross': 'new_v', 'new_v_g_mem': 'new_v', 'new_v_w_cq': 'new_v', 'new_v_w_ckv': 'new_v', 'new_v_w_co': 'new_v', 'new_v_g_mlp': 'new_v', 'new_v_w_up': 'new_v', 'new_v_w_down': 'new_v', 'new_v_g_final': 'new_v'}


def _forward(args):
    return _fwd_reference(*[args[k] for k in FWD_PARAMS])


def _output_shape():
    def fwd():
        inp = _fwd_setup_inputs(0)
        return _fwd_reference(*[inp[k] for k in FWD_PARAMS])
    out = _jax.eval_shape(fwd)
    return out.shape, out.dtype

N_MICROBATCH = 1
ADAM_LR = 0.001
ADAM_B1 = 0.9
ADAM_B2 = 0.999
ADAM_EPS = 1e-08
ADAM_WD = 0.01
ADAM_STEP = 10
PER_EXAMPLE_BATCH_AXIS = {'x': 0, 'mem': 0, 'loss_target': 0}
SHARED_INPUTS = []
_WEIGHT_DTYPES = {'g_mix': _jnp.float32, 'w_in': _jnp.float32, 'b_gate': _jnp.float32, 'conv_w': _jnp.float32, 'conv_b': _jnp.float32, 'conv_ln_g': _jnp.float32, 'conv_ln_b': _jnp.float32, 'w_attn_proj': _jnp.float32, 'w_conv_proj': _jnp.float32, 'w_out': _jnp.float32, 'g_cross': _jnp.float32, 'g_mem': _jnp.float32, 'w_cq': _jnp.float32, 'w_ckv': _jnp.float32, 'w_co': _jnp.float32, 'g_mlp': _jnp.float32, 'w_up': _jnp.float32, 'w_down': _jnp.float32, 'g_final': _jnp.float32}
MOMENT_SCALE = {'g_mix': 1.091570e-01, 'w_in': 3.811205e-02, 'b_gate': 2.491368e-02, 'conv_w': 1.023987e-01, 'conv_b': 2.144413e-01, 'conv_ln_g': 1.281266e-01, 'conv_ln_b': 1.216033e-01, 'w_attn_proj': 2.603885e-02, 'w_conv_proj': 8.671880e-02, 'w_out': 8.797972e-02, 'g_cross': 3.110095e-02, 'g_mem': 4.150094e-02, 'w_cq': 2.719156e-02, 'w_ckv': 2.748849e-02, 'w_co': 2.802529e-02, 'g_mlp': 2.249781e-01, 'w_up': 1.100060e-01, 'w_down': 2.044241e-01, 'g_final': 6.464140e+01}


def _to_microbatches(a, axis):
    t = _jnp.moveaxis(a, axis, 0)
    t = t.reshape((N_MICROBATCH, t.shape[0] // N_MICROBATCH) + t.shape[1:])
    return _jnp.moveaxis(t, 1, axis + 1)


def setup_inputs(seed: int = 0) -> dict:
    inp = _fwd_setup_inputs(seed)
    key = _jax.random.fold_in(_jax.random.key(seed), 7919)
    shape, _ = _output_shape()
    out = dict(inp)
    out["loss_target"] = _jax.random.normal(_jax.random.fold_in(key, 0), shape, _jnp.float32)
    for i, name in enumerate(TWIN_WEIGHTS):
        w = inp[name].astype(_jnp.float32)
        if MOMENT_SCALE is None:
            s = _jnp.sqrt(_jnp.mean(_jnp.square(w)) + 1e-30)
        else:
            s = MOMENT_SCALE[name]
        km, kv = _jax.random.split(_jax.random.fold_in(key, i + 1))
        out[name] = w
        out["m_" + name] = s * _jax.random.normal(km, w.shape, _jnp.float32)
        out["v_" + name] = (s * s) * _jax.random.uniform(kv, w.shape, _jnp.float32, 0.5, 1.5)
    if N_MICROBATCH > 1:
        for name, axis in PER_EXAMPLE_BATCH_AXIS.items():
            out[name] = _to_microbatches(out[name], axis)
    return {'x': out['x'], 'mem': out['mem'], 'g_mix': out['g_mix'], 'w_in': out['w_in'], 'b_gate': out['b_gate'], 'conv_w': out['conv_w'], 'conv_b': out['conv_b'], 'conv_ln_g': out['conv_ln_g'], 'conv_ln_b': out['conv_ln_b'], 'w_attn_proj': out['w_attn_proj'], 'w_conv_proj': out['w_conv_proj'], 'w_out': out['w_out'], 'g_cross': out['g_cross'], 'g_mem': out['g_mem'], 'w_cq': out['w_cq'], 'w_ckv': out['w_ckv'], 'w_co': out['w_co'], 'g_mlp': out['g_mlp'], 'w_up': out['w_up'], 'w_down': out['w_down'], 'g_final': out['g_final'], 'loss_target': out['loss_target'], 'm_g_mix': out['m_g_mix'], 'm_w_in': out['m_w_in'], 'm_b_gate': out['m_b_gate'], 'm_conv_w': out['m_conv_w'], 'm_conv_b': out['m_conv_b'], 'm_conv_ln_g': out['m_conv_ln_g'], 'm_conv_ln_b': out['m_conv_ln_b'], 'm_w_attn_proj': out['m_w_attn_proj'], 'm_w_conv_proj': out['m_w_conv_proj'], 'm_w_out': out['m_w_out'], 'm_g_cross': out['m_g_cross'], 'm_g_mem': out['m_g_mem'], 'm_w_cq': out['m_w_cq'], 'm_w_ckv': out['m_w_ckv'], 'm_w_co': out['m_w_co'], 'm_g_mlp': out['m_g_mlp'], 'm_w_up': out['m_w_up'], 'm_w_down': out['m_w_down'], 'm_g_final': out['m_g_final'], 'v_g_mix': out['v_g_mix'], 'v_w_in': out['v_w_in'], 'v_b_gate': out['v_b_gate'], 'v_conv_w': out['v_conv_w'], 'v_conv_b': out['v_conv_b'], 'v_conv_ln_g': out['v_conv_ln_g'], 'v_conv_ln_b': out['v_conv_ln_b'], 'v_w_attn_proj': out['v_w_attn_proj'], 'v_w_conv_proj': out['v_w_conv_proj'], 'v_w_out': out['v_w_out'], 'v_g_cross': out['v_g_cross'], 'v_g_mem': out['v_g_mem'], 'v_w_cq': out['v_w_cq'], 'v_w_ckv': out['v_w_ckv'], 'v_w_co': out['v_w_co'], 'v_g_mlp': out['v_g_mlp'], 'v_w_up': out['v_w_up'], 'v_w_down': out['v_w_down'], 'v_g_final': out['v_g_final']}


def _loss(weights, diff, rest, loss_target):
    with _jax.named_scope("forward"):
        args = {**rest, TWIN_DIFF_INPUT: diff, **{k: w.astype(_WEIGHT_DTYPES[k]) for k, w in weights.items()}}
        y = _forward(args)
    with _jax.named_scope("loss_head"):
        err = _jnp.square(y.astype(_jnp.float32) - loss_target)
        return 0.5 * _jnp.sum(_jnp.mean(err, axis=-1)) if err.ndim else 0.5 * err


def _adamw(w, g, m, v):
    m = ADAM_B1 * m + (1.0 - ADAM_B1) * g
    v = ADAM_B2 * v + (1.0 - ADAM_B2) * _jnp.square(g)
    m_hat = m / (1.0 - ADAM_B1 ** ADAM_STEP)
    v_hat = v / (1.0 - ADAM_B2 ** ADAM_STEP)
    delta = -ADAM_LR * (m_hat / (_jnp.sqrt(v_hat) + ADAM_EPS) + ADAM_WD * w)
    return delta, m, v


def reference(x, mem, g_mix, w_in, b_gate, conv_w, conv_b, conv_ln_g, conv_ln_b, w_attn_proj, w_conv_proj, w_out, g_cross, g_mem, w_cq, w_ckv, w_co, g_mlp, w_up, w_down, g_final, loss_target, m_g_mix, m_w_in, m_b_gate, m_conv_w, m_conv_b, m_conv_ln_g, m_conv_ln_b, m_w_attn_proj, m_w_conv_proj, m_w_out, m_g_cross, m_g_mem, m_w_cq, m_w_ckv, m_w_co, m_g_mlp, m_w_up, m_w_down, m_g_final, v_g_mix, v_w_in, v_b_gate, v_conv_w, v_conv_b, v_conv_ln_g, v_conv_ln_b, v_w_attn_proj, v_w_conv_proj, v_w_out, v_g_cross, v_g_mem, v_w_cq, v_w_ckv, v_w_co, v_g_mlp, v_w_up, v_w_down, v_g_final):
    given = dict(x=x, mem=mem, g_mix=g_mix, w_in=w_in, b_gate=b_gate, conv_w=conv_w, conv_b=conv_b, conv_ln_g=conv_ln_g, conv_ln_b=conv_ln_b, w_attn_proj=w_attn_proj, w_conv_proj=w_conv_proj, w_out=w_out, g_cross=g_cross, g_mem=g_mem, w_cq=w_cq, w_ckv=w_ckv, w_co=w_co, g_mlp=g_mlp, w_up=w_up, w_down=w_down, g_final=g_final, loss_target=loss_target, m_g_mix=m_g_mix, m_w_in=m_w_in, m_b_gate=m_b_gate, m_conv_w=m_conv_w, m_conv_b=m_conv_b, m_conv_ln_g=m_conv_ln_g, m_conv_ln_b=m_conv_ln_b, m_w_attn_proj=m_w_attn_proj, m_w_conv_proj=m_w_conv_proj, m_w_out=m_w_out, m_g_cross=m_g_cross, m_g_mem=m_g_mem, m_w_cq=m_w_cq, m_w_ckv=m_w_ckv, m_w_co=m_w_co, m_g_mlp=m_g_mlp, m_w_up=m_w_up, m_w_down=m_w_down, m_g_final=m_g_final, v_g_mix=v_g_mix, v_w_in=v_w_in, v_b_gate=v_b_gate, v_conv_w=v_conv_w, v_conv_b=v_conv_b, v_conv_ln_g=v_conv_ln_g, v_conv_ln_b=v_conv_ln_b, v_w_attn_proj=v_w_attn_proj, v_w_conv_proj=v_w_conv_proj, v_w_out=v_w_out, v_g_cross=v_g_cross, v_g_mem=v_g_mem, v_w_cq=v_w_cq, v_w_ckv=v_w_ckv, v_w_co=v_w_co, v_g_mlp=v_g_mlp, v_w_up=v_w_up, v_w_down=v_w_down, v_g_final=v_g_final)
    weights = {n: given[n] for n in TWIN_WEIGHTS}
    shared = {n: given[n] for n in SHARED_INPUTS}
    per_example = {n: given[n] for n in ['x', 'mem']}
    grad_fn = _jax.value_and_grad(_loss, argnums=(0, 1))

    def one_microbatch(ex, loss_target):
        ex = dict(ex)
        diff = ex.pop(TWIN_DIFF_INPUT)
        return grad_fn(weights, diff, {**shared, **ex}, loss_target)

    if N_MICROBATCH == 1:
        loss, (grad_w, grad_x) = one_microbatch(per_example, given["loss_target"])
    else:
        def body(carry, xs):
            loss_sum, grad_sum = carry
            l_k, (gw_k, gx_k) = one_microbatch(xs[0], xs[1])
            with _jax.named_scope("update"):
                return (loss_sum + l_k, _jax.tree.map(_jnp.add, grad_sum, gw_k)), gx_k

        init = (_jnp.zeros((), _jnp.float32), _jax.tree.map(_jnp.zeros_like, weights))
        (loss, grad_w), grad_x = _jax.lax.scan(body, init, (per_example, given["loss_target"]))
    with _jax.named_scope("update"):
        delta_w, new_m, new_v = {}, {}, {}
        for n in TWIN_WEIGHTS:
            delta_w[n], new_m[n], new_v[n] = _adamw(weights[n], grad_w[n], given["m_" + n], given["v_" + n])
    return (loss, grad_x, *[grad_w[n] for n in TWIN_WEIGHTS], *[delta_w[n] for n in TWIN_WEIGHTS],
            *[new_m[n] for n in TWIN_WEIGHTS], *[new_v[n] for n in TWIN_WEIGHTS])
```

```python
import functools

import jax
import jax.numpy as jnp
from jax import lax
from jax.experimental import pallas as pl
from jax.experimental.pallas import tpu as pltpu

F32 = jnp.float32
BF16 = jnp.bfloat16

N_DEV = 8
D_MODEL = 1024
N_MEM = 256
HEAD_DIM = 128
HEADS_PER_GROUP = 4
GROUP_W = HEADS_PER_GROUP * HEAD_DIM
DILATIONS = (1, 4, 16)
BAND = 128
N_GROUPS = 3
ATTN_W = N_GROUPS * GROUP_W
QKV_W = 3 * ATTN_W
ROT_DIM = HEAD_DIM // 4
ROPE_THETA = 500000.0
CONV_CH = 768
CONV_K = 31
CONV_HALO = 32
IN_W = 8192
GLU_COL_BLK = QKV_W // (2 * CONV_CH)
GATE_COL_BLK = (QKV_W + 2 * CONV_CH) // (2 * D_MODEL)
CROSS_HEADS = 4
CROSS_HD = D_MODEL // CROSS_HEADS
D_FF = 4096
EPS = 1e-6
NEG = -1e30
QB = 4
ROW_BLK = QB * BAND

ADAM_LR = 0.001
ADAM_B1 = 0.9
ADAM_B2 = 0.999
ADAM_EPS = 1e-08
ADAM_WD = 0.01
ADAM_STEP = 10

VMEM_LIMIT = 56 * 1024 * 1024
MESH = pl.DeviceIdType.MESH


def _params(**kw):
    return pltpu.CompilerParams(vmem_limit_bytes=VMEM_LIMIT, **kw)


def _sigmoid(x):
    return 1.0 / (1.0 + jnp.exp(-x))


def _dot(a, b, kind):
    dims = {"nn": (((1,), (0,)), ((), ())), "nt": (((1,), (1,)), ((), ())), "tn": (((0,), (0,)), ((), ()))}[kind]
    if a.dtype != BF16:
        a = a.astype(BF16)
    if b.dtype != BF16:
        b = b.astype(BF16)
    return lax.dot_general(a, b, dims, preferred_element_type=F32)


def _mm(name, a, b, kind, grid, a_blk, b_blk, outs, extras=(), epi=None, acc_outs=(), j_outer=False, alias=None):
    gi, gj, gk = grid
    n_ex = len(extras)
    n_out = len(outs)

    def wrap(fn):
        if j_outer:
            return lambda j, i, k: fn(i, j, k)
        return fn

    def spec(blk, fn):
        return pl.BlockSpec(blk, wrap(fn))

    in_specs = [spec(*a_blk), spec(*b_blk)] + [spec(blk, fn) for _, blk, fn in extras]
    operands = [a, b] + [e for e, _, _ in extras]
    io_alias = {}
    if alias is not None:
        in_specs.append(pl.BlockSpec(memory_space=pl.ANY))
        operands.append(alias[0])
        io_alias = {len(operands) - 1: alias[1]}
    out_specs = [spec(blk, fn) for _, blk, fn in outs]
    out_shape = [s for s, _, _ in outs]

    def body(*refs):
        a_ref, b_ref = refs[0], refs[1]
        ex = refs[2:2 + n_ex]
        o0 = 2 + n_ex + (1 if alias is not None else 0)
        out_refs = refs[o0:o0 + n_out]
        acc_ref = refs[o0 + n_out] if gk > 1 else None
        i = pl.program_id(1 if j_outer else 0)
        k = pl.program_id(2)
        part = _dot(a_ref[...], b_ref[...], kind)

        def finish(acc):
            vals = epi(acc, *[e[...] for e in ex]) if epi is not None else (acc,)
            for idx, (o, v) in enumerate(zip(out_refs, vals)):
                if idx in acc_outs:
                    @pl.when(i == 0)
                    def _():
                        o[...] = v.astype(o.dtype)

                    @pl.when(i != 0)
                    def _():
                        o[...] += v.astype(o.dtype)
                else:
                    o[...] = v.astype(o.dtype)

        if gk == 1:
            finish(part)
        else:
            @pl.when(k == 0)
            def _():
                acc_ref[...] = part

            @pl.when(k != 0)
            def _():
                acc_ref[...] += part

            @pl.when(k == gk - 1)
            def _():
                finish(acc_ref[...])

    scratch = []
    if gk > 1:
        tm = a_blk[0][-1] if kind == "tn" else a_blk[0][-2]
        tn = b_blk[0][-2] if kind == "nt" else b_blk[0][-1]
        scratch = [pltpu.VMEM((tm, tn), F32)]
    res = pl.pallas_call(
        body, name=name, out_shape=out_shape, grid=(gj, gi, gk) if j_outer else (gi, gj, gk),
        in_specs=in_specs, out_specs=out_specs, scratch_shapes=scratch, input_output_aliases=io_alias,
        compiler_params=_params(dimension_semantics=("arbitrary", "arbitrary", "arbitrary")),
    )(*operands)
    return res


def _rms_fwd_vals(x, g):
    r = lax.rsqrt(jnp.mean(x * x, axis=-1, keepdims=True) + EPS)
    return x * r * g


def _rms_bwd_vals(x, g, du):
    r = lax.rsqrt(jnp.mean(x * x, axis=-1, keepdims=True) + EPS)
    xh = x * r
    dxh = du * g
    dx = r * (dxh - xh * jnp.mean(dxh * xh, axis=-1, keepdims=True))
    return dx, jnp.sum(du * xh, axis=0, keepdims=True)


def _rms_fwd(name, x, g, rows):
    n = x.shape[0]

    def body(x_ref, g_ref, o_ref):
        o_ref[...] = _rms_fwd_vals(x_ref[...], g_ref[...]).astype(BF16)

    return pl.pallas_call(
        body, name=name, out_shape=jax.ShapeDtypeStruct(x.shape, BF16), grid=(n // rows,),
        in_specs=[pl.BlockSpec((rows, D_MODEL), lambda i: (i, 0)), pl.BlockSpec((1, D_MODEL), lambda i: (0, 0))],
        out_specs=pl.BlockSpec((rows, D_MODEL), lambda i: (i, 0)),
        compiler_params=_params(dimension_semantics=("arbitrary",)),
    )(x, g)


def _rope_tables(seq):
    half = ROT_DIM // 2
    pos = jnp.arange(seq, dtype=F32)
    inv_freq = ROPE_THETA ** (-jnp.arange(0, ROT_DIM, 2, dtype=F32) / ROT_DIM)
    ang = pos[:, None] * inv_freq[None, :]
    cos, sin = jnp.cos(ang), jnp.sin(ang)
    rest = HEAD_DIM - ROT_DIM
    c = jnp.concatenate([cos, cos, jnp.ones((seq, rest), F32)], axis=1)
    s1 = jnp.concatenate([jnp.zeros((seq, half), F32), sin, jnp.zeros((seq, rest), F32)], axis=1)
    s2 = jnp.concatenate([-sin, jnp.zeros((seq, half + rest), F32)], axis=1)
    return c, s1, s2


def _group_shapes(seq, width, dtype):
    return [jax.ShapeDtypeStruct((d, seq // d, width), dtype) for d in DILATIONS]


def _group_specs(width):
    return [pl.BlockSpec((d, ROW_BLK // d, width), lambda i: (0, i, 0)) for d in DILATIONS]


def _qkv_prep(z, tabs):
    seq = z.shape[0]

    def body(z_ref, c_ref, s1_ref, s2_ref, a0, a1, a2, sc):
        outs = (a0, a1, a2)
        c, s1, s2 = c_ref[...], s1_ref[...], s2_ref[...]
        for part in range(3):
            for hh in range(N_GROUPS * HEADS_PER_GROUP):
                g, hl = divmod(hh, HEADS_PER_GROUP)
                col = part * ATTN_W + hh * HEAD_DIM
                ocol = part * GROUP_W + hl * HEAD_DIM
                x = z_ref[:, col:col + HEAD_DIM].astype(F32)
                if part < 2:
                    x = x * c + pltpu.roll(x, ROT_DIM // 2, 1) * s1 + pltpu.roll(x, HEAD_DIM - ROT_DIM // 2, 1) * s2
                d = DILATIONS[g]
                if d == 1:
                    outs[g][0, :, ocol:ocol + HEAD_DIM] = x.astype(BF16)
                else:
                    sc[...] = x
                    for r in range(d):
                        outs[g][r, :, ocol:ocol + HEAD_DIM] = sc[pl.ds(r, ROW_BLK // d, stride=d), :].astype(BF16)

    tab_spec = pl.BlockSpec((ROW_BLK, HEAD_DIM), lambda i: (i, 0))
    return pl.pallas_call(
        body, name="qkv_prep", out_shape=_group_shapes(seq, ATTN_W, BF16), grid=(seq // ROW_BLK,),
        in_specs=[pl.BlockSpec((ROW_BLK, QKV_W), lambda i: (i, 0)), tab_spec, tab_spec, tab_spec],
        out_specs=_group_specs(ATTN_W), scratch_shapes=[pltpu.VMEM((ROW_BLK, HEAD_DIM), F32)],
        compiler_params=_params(dimension_semantics=("arbitrary",)),
    )(z, *tabs)


def _band_masks():
    qi = lax.broadcasted_iota(jnp.int32, (BAND, BAND), 0)
    kj = lax.broadcasted_iota(jnp.int32, (BAND, BAND), 1)
    return kj >= qi, kj <= qi


def _attn_fwd(name, a_g):
    dil, m_len, _ = a_g.shape
    qb = min(QB, m_len // BAND)
    rows = qb * BAND
    steps = m_len // rows
    scale = HEAD_DIM ** -0.5

    def body(q_ref, kc_ref, vc_ref, kp_ref, vp_ref, o_ref, l_ref):
        t = pl.program_id(1)
        m_prev, m_cur = _band_masks()
        for sb in range(qb):
            rs = slice(sb * BAND, (sb + 1) * BAND)
            for h in range(HEADS_PER_GROUP):
                cs = slice(h * HEAD_DIM, (h + 1) * HEAD_DIM)
                q = q_ref[rs, cs]
                if sb == 0:
                    kp, vp = kp_ref[:, cs], vp_ref[:, cs]
                    pm = jnp.logical_and(m_prev, t > 0)
                else:
                    ps = slice((sb - 1) * BAND, sb * BAND)
                    kp, vp = kc_ref[ps, cs], vc_ref[ps, cs]
                    pm = m_prev
                kc, vc = kc_ref[rs, cs], vc_ref[rs, cs]
                sp = jnp.where(pm, _dot(q, kp, "nt") * scale, NEG)
                sc = jnp.where(m_cur, _dot(q, kc, "nt") * scale, NEG)
                mx = jnp.maximum(jnp.max(sp, axis=-1, keepdims=True), jnp.max(sc, axis=-1, keepdims=True))
                pp = jnp.exp(sp - mx)
                pc = jnp.exp(sc - mx)
                den = jnp.sum(pp, axis=-1, keepdims=True) + jnp.sum(pc, axis=-1, keepdims=True)
                o = (_dot(pp, vp, "nn") + _dot(pc, vc, "nn")) / den
                o_ref[rs, cs] = o
                l_ref[rs, cs] = jnp.broadcast_to(mx + jnp.log(den), (BAND, HEAD_DIM))

    def prev(r, t):
        return jnp.maximum(qb * t - 1, 0)

    cur = lambda c: pl.BlockSpec((None, rows, GROUP_W), lambda r, t, c=c: (r, t, c))
    prv = lambda c: pl.BlockSpec((None, BAND, GROUP_W), lambda r, t, c=c: (r, prev(r, t), c))
    out_spec = pl.BlockSpec((None, rows, GROUP_W), lambda r, t: (r, t, 0))
    shp = jax.ShapeDtypeStruct((dil, m_len, GROUP_W), F32)
    return pl.pallas_call(
        body, name=name, out_shape=[shp, shp], grid=(dil, steps),
        in_specs=[cur(0), cur(1), cur(2), prv(1), prv(2)], out_specs=[out_spec, out_spec],
        compiler_params=_params(dimension_semantics=("arbitrary", "arbitrary")),
    )(a_g, a_g, a_g, a_g, a_g)


def _attn_merge(os_, ls_, seq):
    def body(o0, l0, o1, l1, o2, l2, at_ref, lt_ref, sc):
        for h in range(HEADS_PER_GROUP):
            cs = slice(h * HEAD_DIM, (h + 1) * HEAD_DIM)
            for gi, (o_r, l_r) in enumerate(((o1, l1), (o2, l2))):
                d = DILATIONS[gi + 1]
                for r in range(d):
                    sc.at[2 * gi][pl.ds(r, ROW_BLK // d, stride=d), :] = o_r[r, :, cs]
                    sc.at[2 * gi + 1][pl.ds(r, ROW_BLK // d, stride=d), :] = l_r[r, :, cs]
            o0v, l0v = o0[0, :, cs], l0[0, :, cs]
            o1v, l1v, o2v, l2v = sc[0], sc[1], sc[2], sc[3]
            mx = jnp.maximum(jnp.maximum(l0v, l1v), l2v)
            e0, e1, e2 = jnp.exp(l0v - mx), jnp.exp(l1v - mx), jnp.exp(l2v - mx)
            tot = e0 + e1 + e2
            at_ref[:, cs] = ((e0 * o0v + e1 * o1v + e2 * o2v) / tot).astype(BF16)
            lt_ref[:, cs] = mx + jnp.log(tot)

    gs = _group_specs(GROUP_W)
    row = pl.BlockSpec((ROW_BLK, GROUP_W), lambda i: (i, 0))
    return pl.pallas_call(
        body, name="attn_merge",
        out_shape=[jax.ShapeDtypeStruct((seq, GROUP_W), BF16), jax.ShapeDtypeStruct((seq, GROUP_W), F32)],
        grid=(seq // ROW_BLK,), in_specs=[gs[0], gs[0], gs[1], gs[1], gs[2], gs[2]], out_specs=[row, row],
        scratch_shapes=[pltpu.VMEM((4, ROW_BLK, HEAD_DIM), F32)],
        compiler_params=_params(dimension_semantics=("arbitrary",)),
    )(os_[0], ls_[0], os_[1], ls_[1], os_[2], ls_[2])


def _attn_bwd_prep(dattn, attn, lt):
    seq = dattn.shape[0]

    def body(da_ref, at_ref, lt_ref, d0, c0, t0, d1, c1, t1, d2, c2, t2, sc):
        outs = ((d0, c0, t0), (d1, c1, t1), (d2, c2, t2))
        for h in range(HEADS_PER_GROUP):
            cs = slice(h * HEAD_DIM, (h + 1) * HEAD_DIM)
            da = da_ref[:, cs].astype(F32)
            cc = jnp.broadcast_to(jnp.sum(da * at_ref[:, cs].astype(F32), axis=-1, keepdims=True), (ROW_BLK, HEAD_DIM))
            ltv = lt_ref[:, cs]
            d0[0, :, cs] = da_ref[:, cs]
            c0[0, :, cs] = cc
            t0[0, :, cs] = ltv
            sc[0], sc[1], sc[2] = da, cc, ltv
            for g in (1, 2):
                d = DILATIONS[g]
                for r in range(d):
                    rows = pl.ds(r, ROW_BLK // d, stride=d)
                    outs[g][0][r, :, cs] = sc.at[0][rows, :].astype(BF16)
                    outs[g][1][r, :, cs] = sc.at[1][rows, :]
                    outs[g][2][r, :, cs] = sc.at[2][rows, :]

    gs = _group_specs(GROUP_W)
    row = pl.BlockSpec((ROW_BLK, GROUP_W), lambda i: (i, 0))
    shapes, specs = [], []
    for g, d in enumerate(DILATIONS):
        for dt in (BF16, F32, F32):
            shapes.append(jax.ShapeDtypeStruct((d, seq // d, GROUP_W), dt))
            specs.append(gs[g])
    res = pl.pallas_call(
        body, name="attn_bwd_prep", out_shape=shapes, grid=(seq // ROW_BLK,), in_specs=[row, row, row],
        out_specs=specs, scratch_shapes=[pltpu.VMEM((3, ROW_BLK, HEAD_DIM), F32)],
        compiler_params=_params(dimension_semantics=("arbitrary",)),
    )(dattn, attn, lt)
    return [res[3 * g:3 * g + 3] for g in range(N_GROUPS)]


def _attn_bwd(name, a_g, da_g, c_g, lt_g):
    dil, m_len, _ = a_g.shape
    qb = min(QB, m_len // BAND)
    rows = qb * BAND
    steps = m_len // rows
    scale = HEAD_DIM ** -0.5

    def body(q_ref, kc_ref, vc_ref, kp_ref, vp_ref, da_ref, c_ref, lt_ref, d_ref, dk_acc, dv_acc, car_k, car_v):
        tg = pl.program_id(1)
        t = steps - 1 - tg
        m_prev, m_cur = _band_masks()

        @pl.when(tg == 0)
        def _():
            car_k[...] = jnp.zeros_like(car_k)
            car_v[...] = jnp.zeros_like(car_v)

        zero = jnp.zeros((rows, GROUP_W), F32)
        dk_acc[0:rows, :] = zero
        dv_acc[0:rows, :] = zero
        dk_acc[rows:rows + BAND, :] = car_k[...]
        dv_acc[rows:rows + BAND, :] = car_v[...]
        for sb in range(qb):
            rs = slice(sb * BAND, (sb + 1) * BAND)
            ps = slice((sb - 1) * BAND, sb * BAND)
            acc_p = slice(sb * BAND, (sb + 1) * BAND)
            acc_c = slice((sb + 1) * BAND, (sb + 2) * BAND)
            for h in range(HEADS_PER_GROUP):
                cs = slice(h * HEAD_DIM, (h + 1) * HEAD_DIM)
                q, da = q_ref[rs, cs], da_ref[rs, cs]
                cc, ltv = c_ref[rs, cs], lt_ref[rs, cs]
                if sb == 0:
                    kp, vp = kp_ref[:, cs], vp_ref[:, cs]
                    pm = jnp.logical_and(m_prev, t > 0)
                else:
                    kp, vp = kc_ref[ps, cs], vc_ref[ps, cs]
                    pm = m_prev
                kc, vc = kc_ref[rs, cs], vc_ref[rs, cs]
                pp = jnp.exp(jnp.where(pm, _dot(q, kp, "nt") * scale - ltv, NEG))
                pc = jnp.exp(jnp.where(m_cur, _dot(q, kc, "nt") * scale - ltv, NEG))
                dsp = (pp * (_dot(da, vp, "nt") - cc) * scale).astype(BF16)
                dsc = (pc * (_dot(da, vc, "nt") - cc) * scale).astype(BF16)
                ppb, pcb = pp.astype(BF16), pc.astype(BF16)
                d_ref[rs, cs] = (_dot(dsp, kp, "nn") + _dot(dsc, kc, "nn")).astype(BF16)
                dk_acc[acc_p, cs] += _dot(dsp, q, "tn")
                dk_acc[acc_c, cs] += _dot(dsc, q, "tn")
                dv_acc[acc_p, cs] += _dot(ppb, da, "tn")
                dv_acc[acc_c, cs] += _dot(pcb, da, "tn")
        d_ref[:, GROUP_W:2 * GROUP_W] = dk_acc[BAND:rows + BAND, :].astype(BF16)
        d_ref[:, 2 * GROUP_W:3 * GROUP_W] = dv_acc[BAND:rows + BAND, :].astype(BF16)
        car_k[...] = dk_acc[0:BAND, :]
        car_v[...] = dv_acc[0:BAND, :]

    def rev(tg):
        return steps - 1 - tg

    def prev(tg):
        return jnp.maximum(qb * rev(tg) - 1, 0)

    cur = lambda c: pl.BlockSpec((None, rows, GROUP_W), lambda r, tg, c=c: (r, rev(tg), c))
    prv = lambda c: pl.BlockSpec((None, BAND, GROUP_W), lambda r, tg, c=c: (r, prev(tg), c))
    return pl.pallas_call(
        body, name=name, out_shape=jax.ShapeDtypeStruct((dil, m_len, ATTN_W), BF16), grid=(dil, steps),
        in_specs=[cur(0), cur(1), cur(2), prv(1), prv(2), cur(0), cur(0), cur(0)],
        out_specs=pl.BlockSpec((None, rows, ATTN_W), lambda r, tg: (r, rev(tg), 0)),
        scratch_shapes=[pltpu.VMEM((rows + BAND, GROUP_W), F32), pltpu.VMEM((rows + BAND, GROUP_W), F32),
                        pltpu.VMEM((BAND, GROUP_W), F32), pltpu.VMEM((BAND, GROUP_W), F32)],
        compiler_params=_params(dimension_semantics=("arbitrary", "arbitrary")),
    )(a_g, a_g, a_g, a_g, a_g, da_g, c_g, lt_g)


def _dqkv_post(d_gs, tabs, dz):
    seq = dz.shape[0]

    def body(g0, g1, g2, c_ref, s1_ref, s2_ref, dz_any, o_ref, sc):
        del dz_any
        ins = (g0, g1, g2)
        c, s1, s2 = c_ref[...], s1_ref[...], s2_ref[...]
        for part in range(3):
            for hh in range(N_GROUPS * HEADS_PER_GROUP):
                g, hl = divmod(hh, HEADS_PER_GROUP)
                icol = part * GROUP_W + hl * HEAD_DIM
                ocol = part * ATTN_W + hh * HEAD_DIM
                d = DILATIONS[g]
                if d == 1:
                    x = ins[g][0, :, icol:icol + HEAD_DIM].astype(F32)
                else:
                    for r in range(d):
                        sc[pl.ds(r, ROW_BLK // d, stride=d), :] = ins[g][r, :, icol:icol + HEAD_DIM].astype(F32)
                    x = sc[...]
                if part < 2:
                    x = x * c + pltpu.roll(x * s1, HEAD_DIM - ROT_DIM // 2, 1) + pltpu.roll(x * s2, ROT_DIM // 2, 1)
                o_ref[:, ocol:ocol + HEAD_DIM] = x.astype(BF16)

    tab_spec = pl.BlockSpec((ROW_BLK, HEAD_DIM), lambda i: (i, 0))
    return pl.pallas_call(
        body, name="dqkv_post", out_shape=jax.ShapeDtypeStruct(dz.shape, BF16), grid=(seq // ROW_BLK,),
        in_specs=_group_specs(ATTN_W) + [tab_spec, tab_spec, tab_spec, pl.BlockSpec(memory_space=pl.ANY)],
        out_specs=pl.BlockSpec((ROW_BLK, QKV_W), lambda i: (i, 0)),
        scratch_shapes=[pltpu.VMEM((ROW_BLK, HEAD_DIM), F32)], input_output_aliases={6: 0},
        compiler_params=_params(dimension_semantics=("arbitrary",)),
    )(*d_gs, *tabs, dz)


def _glu(zg):
    a = zg[:, :CONV_CH].astype(F32)
    s = _sigmoid(zg[:, CONV_CH:].astype(F32))
    return a, s, a * s


def _conv_fwd(z, cw, cb, lg, lb):
    seq = z.shape[0]
    halo_per_blk = ROW_BLK // CONV_HALO

    def body(zg_ref, zh_ref, cw_ref, cb_ref, lg_ref, lb_ref, c2_ref, c4_ref, xpad):
        i = pl.program_id(0)
        _, _, c1 = _glu(zg_ref[...])
        _, _, c1h = _glu(zh_ref[...])
        xpad[0:CONV_HALO, :] = jnp.where(i > 0, c1h, 0.0)
        xpad[CONV_HALO:, :] = c1
        acc = jnp.zeros((ROW_BLK, CONV_CH), F32)
        for j in range(CONV_K):
            acc = acc + cw_ref[j:j + 1, :] * xpad[pl.ds(CONV_HALO - (CONV_K - 1) + j, ROW_BLK), :]
        c2 = acc + cb_ref[...]
        c2_ref[...] = c2
        mu = jnp.mean(c2, axis=-1, keepdims=True)
        xc = c2 - mu
        rstd = lax.rsqrt(jnp.mean(xc * xc, axis=-1, keepdims=True) + EPS)
        c3 = xc * rstd * lg_ref[...] + lb_ref[...]
        c4_ref[...] = (c3 * _sigmoid(c3)).astype(BF16)

    vec = pl.BlockSpec((1, CONV_CH), lambda i: (0, 0))
    return pl.pallas_call(
        body, name="conv_fwd",
        out_shape=[jax.ShapeDtypeStruct((seq, CONV_CH), F32), jax.ShapeDtypeStruct((seq, CONV_CH), BF16)],
        grid=(seq // ROW_BLK,),
        in_specs=[pl.BlockSpec((ROW_BLK, 2 * CONV_CH), lambda i: (i, GLU_COL_BLK)),
                  pl.BlockSpec((CONV_HALO, 2 * CONV_CH), lambda i: (jnp.maximum(i * halo_per_blk - 1, 0), GLU_COL_BLK)),
                  pl.BlockSpec((CONV_HALO, CONV_CH), lambda i: (0, 0)), vec, vec, vec],
        out_specs=[pl.BlockSpec((ROW_BLK, CONV_CH), lambda i: (i, 0)), pl.BlockSpec((ROW_BLK, CONV_CH), lambda i: (i, 0))],
        scratch_shapes=[pltpu.VMEM((ROW_BLK + CONV_HALO, CONV_CH), F32)],
        compiler_params=_params(dimension_semantics=("arbitrary",)),
    )(z, z, cw, cb, lg, lb)


def _conv_bwd(dc2, z, cw, dz):
    seq = z.shape[0]
    halo_per_blk = ROW_BLK // CONV_HALO
    n_blk = seq // ROW_BLK
    last_halo = seq // CONV_HALO - 1

    def body(dc_ref, dn_ref, zg_ref, zh_ref, cw_ref, dz_any, o_ref, dcw_ref, xpad, ypad):
        del dz_any
        i = pl.program_id(0)
        a, s, c1 = _glu(zg_ref[...])
        _, _, c1h = _glu(zh_ref[...])
        xpad[0:CONV_HALO, :] = jnp.where(i > 0, c1h, 0.0)
        xpad[CONV_HALO:, :] = c1
        dc = dc_ref[...]
        ypad[0:ROW_BLK, :] = dc
        ypad[ROW_BLK:, :] = jnp.where(i < n_blk - 1, dn_ref[...], 0.0)

        @pl.when(i == 0)
        def _():
            dcw_ref[...] = jnp.zeros_like(dcw_ref)

        dc1 = jnp.zeros((ROW_BLK, CONV_CH), F32)
        for j in range(CONV_K):
            xs = xpad[pl.ds(CONV_HALO - (CONV_K - 1) + j, ROW_BLK), :]
            dcw_ref[j:j + 1, :] += jnp.sum(dc * xs, axis=0, keepdims=True)
            dc1 = dc1 + cw_ref[j:j + 1, :] * ypad[pl.ds(CONV_K - 1 - j, ROW_BLK), :]
        o_ref[:, :CONV_CH] = (dc1 * s).astype(BF16)
        o_ref[:, CONV_CH:] = (dc1 * a * s * (1.0 - s)).astype(BF16)

    return pl.pallas_call(
        body, name="conv_bwd",
        out_shape=[jax.ShapeDtypeStruct(dz.shape, BF16), jax.ShapeDtypeStruct((CONV_HALO, CONV_CH), F32)],
        grid=(n_blk,),
        in_specs=[pl.BlockSpec((ROW_BLK, CONV_CH), lambda i: (i, 0)),
                  pl.BlockSpec((CONV_HALO, CONV_CH), lambda i: (jnp.minimum((i + 1) * halo_per_blk, last_halo), 0)),
                  pl.BlockSpec((ROW_BLK, 2 * CONV_CH), lambda i: (i, GLU_COL_BLK)),
                  pl.BlockSpec((CONV_HALO, 2 * CONV_CH), lambda i: (jnp.maximum(i * halo_per_blk - 1, 0), GLU_COL_BLK)),
                  pl.BlockSpec((CONV_HALO, CONV_CH), lambda i: (0, 0)),
                  pl.BlockSpec(memory_space=pl.ANY)],
        out_specs=[pl.BlockSpec((ROW_BLK, 2 * CONV_CH), lambda i: (i, GLU_COL_BLK)),
                   pl.BlockSpec((CONV_HALO, CONV_CH), lambda i: (0, 0))],
        scratch_shapes=[pltpu.VMEM((ROW_BLK + CONV_HALO, CONV_CH), F32), pltpu.VMEM((ROW_BLK + CONV_HALO, CONV_CH), F32)],
        input_output_aliases={5: 0},
        compiler_params=_params(dimension_semantics=("arbitrary",)),
    )(dc2, dc2, z, z, cw, dz)


def _epi_mix(ya, c4, wcp, gates, bg):
    yc = _dot(c4, wcp, "nn")
    gv = _sigmoid(gates.astype(F32) + bg)
    merged = gv[:, :D_MODEL] * ya + gv[:, D_MODEL:] * yc
    return merged, ya, yc


def _epi_residual_rms(acc, xres, g):
    x = xres + acc
    return x, _rms_fwd_vals(x, g)


def _cross_scores(cq, ck):
    out = []
    for h in range(CROSS_HEADS):
        cs = slice(h * CROSS_HD, (h + 1) * CROSS_HD)
        s = _dot(cq[:, cs], ck[:, cs], "nt") * (CROSS_HD ** -0.5)
        e = jnp.exp(s - jnp.max(s, axis=-1, keepdims=True))
        out.append((cs, e, jnp.sum(e, axis=-1, keepdims=True)))
    return out


def _epi_cross_fwd(acc, ck, cv):
    cq = acc.astype(BF16)
    co = [_dot(e, cv[:, cs], "nn") / den for cs, e, den in _cross_scores(cq, ck)]
    return cq, jnp.concatenate(co, axis=1)


def _epi_cross_bwd(dco, cq, ck, cv):
    dco = dco.astype(BF16)
    dcq, dck, dcv = [], [], []
    for cs, e, den in _cross_scores(cq, ck):
        p = e / den
        dp = _dot(dco[:, cs], cv[:, cs], "nt")
        ds = (p * (dp - jnp.sum(dp * p, axis=-1, keepdims=True)) * (CROSS_HD ** -0.5)).astype(BF16)
        dcq.append(_dot(ds, ck[:, cs], "nn"))
        dck.append(_dot(ds, cq[:, cs], "tn"))
        dcv.append(_dot(p, dco[:, cs], "tn"))
    return jnp.concatenate(dcq, axis=1), jnp.concatenate(dck, axis=1), jnp.concatenate(dcv, axis=1)


def _epi_mlp_up(acc):
    return acc, jnp.square(jnp.maximum(acc, 0.0))


def _epi_final(acc, x2, tgt, g):
    x3 = x2 + acc
    err = _rms_fwd_vals(x3, g) - tgt
    loss = (0.5 / D_MODEL) * jnp.sum(err * err)
    dx3, dg = _rms_bwd_vals(x3, g, err * (1.0 / D_MODEL))
    return dx3, jnp.full((1, HEAD_DIM), loss, F32), dg


def _epi_mlp_down_bwd(dh, hpre):
    return (dh * 2.0 * jnp.maximum(hpre.astype(F32), 0.0),)


def _epi_rms_bwd(du, x, g, dres):
    dx, dg = _rms_bwd_vals(x, g, du)
    return dres + dx, dg


def _epi_rms_bwd_g(du, x, g):
    return (_rms_bwd_vals(x, g, du)[1],)


def _epi_mix_bwd(dm, ya, yc, gates, bg):
    gv = _sigmoid(gates.astype(F32) + bg)
    ga, gb = gv[:, :D_MODEL], gv[:, D_MODEL:]
    ya, yc = ya.astype(F32), yc.astype(F32)
    dgate = jnp.concatenate([dm * ya * ga * (1.0 - ga), dm * yc * gb * (1.0 - gb)], axis=1)
    return dm * ga, dm * gb, dgate, jnp.sum(dgate, axis=0, keepdims=True)


def _epi_ln_bwd(dc4, c2, lg, lb):
    mu = jnp.mean(c2, axis=-1, keepdims=True)
    xc = c2 - mu
    rstd = lax.rsqrt(jnp.mean(xc * xc, axis=-1, keepdims=True) + EPS)
    xh = xc * rstd
    c3 = xh * lg + lb
    sg = _sigmoid(c3)
    dc3 = dc4 * sg * (1.0 + c3 * (1.0 - sg))
    dxh = dc3 * lg
    dc2 = rstd * (dxh - jnp.mean(dxh, axis=-1, keepdims=True) - xh * jnp.mean(dxh * xh, axis=-1, keepdims=True))
    return (dc2, jnp.sum(dc3 * xh, axis=0, keepdims=True), jnp.sum(dc3, axis=0, keepdims=True),
            jnp.sum(dc2, axis=0, keepdims=True))


def _sds(shape, dtype):
    return jax.ShapeDtypeStruct(shape, dtype)


def _local_step(x, mem, tgt, sm, w):
    seq = x.shape[0]
    nr = seq // ROW_BLK
    big = min(1024, seq)
    nb = seq // big
    row = lambda n: ((ROW_BLK, n), lambda i, j, k: (i, 0))
    vec = lambda n: ((1, n), lambda i, j, k: (0, 0))
    full = lambda r, c: ((r, c), lambda i, j, k: (0, 0))
    gates_blk = ((ROW_BLK, 2 * D_MODEL), lambda i, j, k: (i, GATE_COL_BLK))
    tabs = _rope_tables(seq)

    u = _rms_fwd("rms_mix", x, sm["g_mix"], ROW_BLK)
    z = _mm("in_proj", u, w["w_in"], "nn", (nb, N_DEV, 1), ((big, D_MODEL), lambda i, j, k: (i, 0)),
            ((None, D_MODEL, D_MODEL), lambda i, j, k: (j, 0, 0)),
            [(_sds((seq, IN_W), BF16), (big, D_MODEL), lambda i, j, k: (i, j))], j_outer=True)[0]
    a_gs = _qkv_prep(z, tabs)
    os_, ls_ = [], []
    for g in range(N_GROUPS):
        o_g, l_g = _attn_fwd("attn_fwd_%d" % g, a_gs[g])
        os_.append(o_g)
        ls_.append(l_g)
    attn, lt = _attn_merge(os_, ls_, seq)
    c2, c4 = _conv_fwd(z, sm["conv_w"], sm["conv_b"], sm["conv_ln_g"], sm["conv_ln_b"])
    merged, ya, yc = _mm(
        "mix", attn, w["w_attn_proj"], "nn", (nr, 1, 1), row(GROUP_W), full(GROUP_W, D_MODEL),
        [(_sds((seq, D_MODEL), BF16), *row(D_MODEL))] * 3,
        extras=[(c4, *row(CONV_CH)), (w["w_conv_proj"], *full(CONV_CH, D_MODEL)), (z, *gates_blk), (sm["b_gate"], *vec(2 * D_MODEL))],
        epi=_epi_mix)
    x1, uq = _mm("out_proj", merged, w["w_out"], "nn", (nr, 1, 1), row(D_MODEL), full(D_MODEL, D_MODEL),
                 [(_sds((seq, D_MODEL), F32), *row(D_MODEL)), (_sds((seq, D_MODEL), BF16), *row(D_MODEL))],
                 extras=[(x, *row(D_MODEL)), (sm["g_cross"], *vec(D_MODEL))], epi=_epi_residual_rms)

    mn = _rms_fwd("rms_mem", mem, sm["g_mem"], N_MEM)
    ckv = _mm("ckv_proj", mn, w["w_ckv"], "nn", (1, N_DEV, 1), full(N_MEM, D_MODEL),
              ((None, D_MODEL, 2 * D_MODEL // N_DEV), lambda i, j, k: (j, 0, 0)),
              [(_sds((N_MEM, 2 * D_MODEL), BF16), (N_MEM, 2 * D_MODEL // N_DEV), lambda i, j, k: (0, j))])[0]
    ck, cv = ckv[:, :D_MODEL], ckv[:, D_MODEL:]
    kv_blk = full(N_MEM, D_MODEL)
    cq, co = _mm("cq_proj_cross", uq, w["w_cq"], "nn", (nr, 1, 1), row(D_MODEL), full(D_MODEL, D_MODEL),
                 [(_sds((seq, D_MODEL), BF16), *row(D_MODEL))] * 2,
                 extras=[(ck, *kv_blk), (cv, *kv_blk)], epi=_epi_cross_fwd)
    x2, um = _mm("co_proj", co, w["w_co"], "nn", (nr, 1, 1), row(D_MODEL), full(D_MODEL, D_MODEL),
                 [(_sds((seq, D_MODEL), F32), *row(D_MODEL)), (_sds((seq, D_MODEL), BF16), *row(D_MODEL))],
                 extras=[(x1, *row(D_MODEL)), (sm["g_mlp"], *vec(D_MODEL))], epi=_epi_residual_rms)

    ff_blk = D_FF // N_DEV
    hpre, h = _mm("mlp_up", um, w["w_up"], "nn", (nb, N_DEV, 1), ((big, D_MODEL), lambda i, j, k: (i, 0)),
                  ((None, D_MODEL, ff_blk), lambda i, j, k: (j, 0, 0)),
                  [(_sds((seq, D_FF), BF16), (big, ff_blk), lambda i, j, k: (i, j))] * 2, epi=_epi_mlp_up, j_outer=True)
    kt = D_FF // D_MODEL
    dx3, loss, dg_final = _mm(
        "mlp_down_loss", h, w["w_down"], "nn", (nr, 1, kt), ((ROW_BLK, D_MODEL), lambda i, j, k: (i, k)),
        ((D_MODEL, D_MODEL), lambda i, j, k: (k, 0)),
        [(_sds((seq, D_MODEL), F32), *row(D_MODEL)), (_sds((1, HEAD_DIM), F32), *vec(HEAD_DIM)), (_sds((1, D_MODEL), F32), *vec(D_MODEL))],
        extras=[(x2, *row(D_MODEL)), (tgt, *row(D_MODEL)), (sm["g_final"], *vec(D_MODEL))], epi=_epi_final, acc_outs=(1, 2))

    dhpre = _mm("mlp_down_bwd", dx3, w["w_down"], "nt", (nb, kt, 1), ((big, D_MODEL), lambda i, j, k: (i, 0)),
                ((D_MODEL, D_MODEL), lambda i, j, k: (j, 0)),
                [(_sds((seq, D_FF), BF16), (big, D_MODEL), lambda i, j, k: (i, j))],
                extras=[(hpre, (big, D_MODEL), lambda i, j, k: (i, j))], epi=_epi_mlp_down_bwd, j_outer=True)[0]
    dw = {}
    dw["w_down"] = _mm("dw_down", h, dx3, "tn", (kt, 1, nb), ((big, D_MODEL), lambda i, j, k: (k, i)),
                       ((big, D_MODEL), lambda i, j, k: (k, 0)),
                       [(_sds((D_FF, D_MODEL), BF16), (D_MODEL, D_MODEL), lambda i, j, k: (i, 0))])[0]
    dx2, dg_mlp = _mm("mlp_up_bwd", dhpre, w["w_up"], "nt", (nr, 1, N_DEV), ((ROW_BLK, ff_blk), lambda i, j, k: (i, k)),
                      ((None, D_MODEL, ff_blk), lambda i, j, k: (k, 0, 0)),
                      [(_sds((seq, D_MODEL), F32), *row(D_MODEL)), (_sds((1, D_MODEL), F32), *vec(D_MODEL))],
                      extras=[(x2, *row(D_MODEL)), (sm["g_mlp"], *vec(D_MODEL)), (dx3, *row(D_MODEL))],
                      epi=_epi_rms_bwd, acc_outs=(1,))
    dw["w_up"] = _mm("dw_up", um, dhpre, "tn", (1, N_DEV, nb), ((big, D_MODEL), lambda i, j, k: (k, 0)),
                     ((big, ff_blk), lambda i, j, k: (k, j)),
                     [(_sds((N_DEV, D_MODEL, ff_blk), BF16), (None, D_MODEL, ff_blk), lambda i, j, k: (j, 0, 0))])[0]

    acc_kv = (_sds((N_MEM, D_MODEL), F32), *kv_blk)
    dcq, dck, dcv = _mm("co_proj_bwd_cross", dx2, w["w_co"], "nt", (nr, 1, 1), row(D_MODEL), full(D_MODEL, D_MODEL),
                        [(_sds((seq, D_MODEL), BF16), *row(D_MODEL)), acc_kv, acc_kv],
                        extras=[(cq, *row(D_MODEL)), (ck, *kv_blk), (cv, *kv_blk)], epi=_epi_cross_bwd, acc_outs=(1, 2))

    def dw_square(name, act, grad):
        return _mm(name, act, grad, "tn", (1, 1, nb), ((big, D_MODEL), lambda i, j, k: (k, 0)),
                   ((big, D_MODEL), lambda i, j, k: (k, 0)), [(_sds((D_MODEL, D_MODEL), BF16), *full(D_MODEL, D_MODEL))])[0]

    dw["w_co"] = dw_square("dw_co", co, dx2)
    dx1, dg_cross = _mm("cq_proj_bwd", dcq, w["w_cq"], "nt", (nr, 1, 1), row(D_MODEL), full(D_MODEL, D_MODEL),
                        [(_sds((seq, D_MODEL), F32), *row(D_MODEL)), (_sds((1, D_MODEL), F32), *vec(D_MODEL))],
                        extras=[(x1, *row(D_MODEL)), (sm["g_cross"], *vec(D_MODEL)), (dx2, *row(D_MODEL))],
                        epi=_epi_rms_bwd, acc_outs=(1,))
    dw["w_cq"] = dw_square("dw_cq", uq, dcq)
    dckv = jnp.concatenate([dck, dcv], axis=1)
    kv_chunk = 2 * D_MODEL // N_DEV
    dw["w_ckv"] = _mm("dw_ckv", mn, dckv, "tn", (1, N_DEV, 1), full(N_MEM, D_MODEL), ((N_MEM, kv_chunk), lambda i, j, k: (0, j)),
                      [(_sds((N_DEV, D_MODEL, kv_chunk), BF16), (None, D_MODEL, kv_chunk), lambda i, j, k: (j, 0, 0))])[0]
    dg_mem = _mm("ckv_proj_bwd", dckv, w["w_ckv"], "nt", (1, 1, N_DEV), ((N_MEM, kv_chunk), lambda i, j, k: (0, k)),
                 ((None, D_MODEL, kv_chunk), lambda i, j, k: (k, 0, 0)), [(_sds((1, D_MODEL), F32), *vec(D_MODEL))],
                 extras=[(mem, *full(N_MEM, D_MODEL)), (sm["g_mem"], *vec(D_MODEL))], epi=_epi_rms_bwd_g, acc_outs=(0,))[0]

    dya, dyc, dz, db_gate = _mm(
        "out_proj_bwd_mix", dx1, w["w_out"], "nt", (nr, 1, 1), row(D_MODEL), full(D_MODEL, D_MODEL),
        [(_sds((seq, D_MODEL), BF16), *row(D_MODEL)), (_sds((seq, D_MODEL), BF16), *row(D_MODEL)),
         (_sds((seq, IN_W), BF16), *gates_blk), (_sds((1, 2 * D_MODEL), F32), *vec(2 * D_MODEL))],
        extras=[(ya, *row(D_MODEL)), (yc, *row(D_MODEL)), (z, *gates_blk), (sm["b_gate"], *vec(2 * D_MODEL))],
        epi=_epi_mix_bwd, acc_outs=(3,))
    dw["w_out"] = dw_square("dw_out", merged, dx1)
    dattn = _mm("attn_proj_bwd", dya, w["w_attn_proj"], "nt", (nr, 1, 1), row(D_MODEL), full(GROUP_W, D_MODEL),
                [(_sds((seq, GROUP_W), BF16), *row(GROUP_W))])[0]
    pc = D_MODEL // N_DEV
    dw["w_attn_proj"] = _mm("dw_attn_proj", attn, dya, "tn", (1, N_DEV, nb), ((big, GROUP_W), lambda i, j, k: (k, 0)),
                            ((big, pc), lambda i, j, k: (k, j)),
                            [(_sds((N_DEV, GROUP_W, pc), BF16), (None, GROUP_W, pc), lambda i, j, k: (j, 0, 0))])[0]
    cvec = (_sds((1, CONV_CH), F32), *vec(CONV_CH))
    dc2, dg_ln_g, dg_ln_b, dg_conv_b = _mm(
        "conv_proj_bwd_ln", dyc, w["w_conv_proj"], "nt", (nr, 1, 1), row(D_MODEL), full(CONV_CH, D_MODEL),
        [(_sds((seq, CONV_CH), F32), *row(CONV_CH)), cvec, cvec, cvec],
        extras=[(c2, *row(CONV_CH)), (sm["conv_ln_g"], *vec(CONV_CH)), (sm["conv_ln_b"], *vec(CONV_CH))],
        epi=_epi_ln_bwd, acc_outs=(1, 2, 3))
    dw["w_conv_proj"] = _mm("dw_conv_proj", c4, dyc, "tn", (1, N_DEV, nb), ((big, CONV_CH), lambda i, j, k: (k, 0)),
                            ((big, pc), lambda i, j, k: (k, j)),
                            [(_sds((N_DEV, CONV_CH, pc), BF16), (None, CONV_CH, pc), lambda i, j, k: (j, 0, 0))])[0]
    dz, dg_conv_w = _conv_bwd(dc2, z, sm["conv_w"], dz)
    preps = _attn_bwd_prep(dattn, attn, lt)
    d_gs = [_attn_bwd("attn_bwd_%d" % g, a_gs[g], *preps[g]) for g in range(N_GROUPS)]
    dz = _dqkv_post(d_gs, tabs, dz)
    grad_x, dg_mix = _mm("in_proj_bwd", dz, w["w_in"], "nt", (nr, 1, N_DEV), ((ROW_BLK, D_MODEL), lambda i, j, k: (i, k)),
                         ((None, D_MODEL, D_MODEL), lambda i, j, k: (k, 0, 0)),
                         [(_sds((seq, D_MODEL), F32), *row(D_MODEL)), (_sds((1, D_MODEL), F32), *vec(D_MODEL))],
                         extras=[(x, *row(D_MODEL)), (sm["g_mix"], *vec(D_MODEL)), (dx1, *row(D_MODEL))],
                         epi=_epi_rms_bwd, acc_outs=(1,))
    dw["w_in"] = _mm("dw_in", u, dz, "tn", (1, N_DEV, nb), ((big, D_MODEL), lambda i, j, k: (k, 0)),
                     ((big, D_MODEL), lambda i, j, k: (k, j)),
                     [(_sds((N_DEV, D_MODEL, D_MODEL), BF16), (None, D_MODEL, D_MODEL), lambda i, j, k: (j, 0, 0))])[0]
    small = dict(g_mix=dg_mix, b_gate=db_gate, conv_b=dg_conv_b, conv_ln_g=dg_ln_g, conv_ln_b=dg_ln_b, g_cross=dg_cross,
                 g_mem=dg_mem, g_mlp=dg_mlp, g_final=dg_final, loss=loss, conv_w=dg_conv_w)
    return grad_x, dw, small


def _peers():
    x, y, c = lax.axis_index("x"), lax.axis_index("y"), lax.axis_index("c")
    me = 4 * x + 2 * y + c
    peers = [(x, y, 1 - c), (1 - x, y, c), (x, 1 - y, c), (1 - x, 1 - y, c),
             (1 - x, y, 1 - c), (x, 1 - y, 1 - c), (1 - x, 1 - y, 1 - c)]
    return me, peers


def _wait_all(out_ref, send_sem, recv_sem, peer):
    seven = out_ref.at[pl.ds(0, N_DEV - 1)]
    cp = pltpu.make_async_remote_copy(src_ref=seven, dst_ref=seven, send_sem=send_sem, recv_sem=recv_sem,
                                      device_id=peer, device_id_type=MESH)
    cp.wait_send()
    cp.wait_recv()


def _all_gather(shards, out_dtypes):
    n = len(shards)

    def body(*refs):
        ins, outs, stage = refs[:n], refs[n:2 * n], refs[2 * n:3 * n]
        send_sems, recv_sems, local_sems = refs[3 * n:]
        me, peers = _peers()
        for a in range(n):
            stage[a][...] = ins[a][...].astype(stage[a].dtype)
        local = []
        for a in range(n):
            cp = pltpu.make_async_copy(stage[a], outs[a].at[me], local_sems.at[a])
            cp.start()
            local.append(cp)
            for p in peers:
                pltpu.make_async_remote_copy(src_ref=stage[a], dst_ref=outs[a].at[me], send_sem=send_sems.at[a],
                                             recv_sem=recv_sems.at[a], device_id=p, device_id_type=MESH).start()
        for a in range(n):
            _wait_all(outs[a], send_sems.at[a], recv_sems.at[a], peers[0])
            local[a].wait()

    return pl.pallas_call(
        body, name="gather_weights",
        out_shape=[jax.ShapeDtypeStruct((N_DEV,) + s.shape, dt) for s, dt in zip(shards, out_dtypes)],
        in_specs=[pl.BlockSpec(memory_space=pltpu.VMEM)] * n, out_specs=[pl.BlockSpec(memory_space=pl.ANY)] * n,
        scratch_shapes=[pltpu.VMEM(s.shape, dt) for s, dt in zip(shards, out_dtypes)]
        + [pltpu.SemaphoreType.DMA((n,)), pltpu.SemaphoreType.DMA((n,)), pltpu.SemaphoreType.DMA((n,))],
        compiler_params=_params(),
    )(*shards)


def _exchange(chunked, replicated):
    n_c, n_r = len(chunked), len(replicated)
    n = n_c + n_r

    def body(*refs):
        ins, outs = refs[:n], refs[n:2 * n]
        send_sems, recv_sems, local_sems = refs[2 * n:]
        me, peers = _peers()
        local = []
        for a in range(n):
            src_me = ins[a].at[me] if a < n_c else ins[a]
            cp = pltpu.make_async_copy(src_me, outs[a].at[me], local_sems.at[a])
            cp.start()
            local.append(cp)
            for (px, py, pc) in peers:
                src = ins[a].at[4 * px + 2 * py + pc] if a < n_c else ins[a]
                pltpu.make_async_remote_copy(src_ref=src, dst_ref=outs[a].at[me], send_sem=send_sems.at[a],
                                             recv_sem=recv_sems.at[a], device_id=(px, py, pc), device_id_type=MESH).start()
        for a in range(n):
            _wait_all(outs[a], send_sems.at[a], recv_sems.at[a], peers[0])
            local[a].wait()

    operands = list(chunked) + list(replicated)
    shapes = [jax.ShapeDtypeStruct(a.shape, a.dtype) for a in chunked]
    shapes += [jax.ShapeDtypeStruct((N_DEV,) + a.shape, a.dtype) for a in replicated]
    return pl.pallas_call(
        body, name="exchange_grads", out_shape=shapes,
        in_specs=[pl.BlockSpec(memory_space=pl.ANY)] * n, out_specs=[pl.BlockSpec(memory_space=pl.ANY)] * n,
        scratch_shapes=[pltpu.SemaphoreType.DMA((n,)), pltpu.SemaphoreType.DMA((n,)), pltpu.SemaphoreType.DMA((n,))],
        compiler_params=_params(),
    )(*operands)


def _adamw(name, w, m, v, parts):
    rows, cols = w.shape
    n_parts = parts.shape[0]
    rb = rows if rows <= 256 or rows % 256 else 256

    def body(w_ref, m_ref, v_ref, p_ref, g_ref, d_ref, nm_ref, nv_ref):
        g = p_ref[0].astype(F32)
        for q in range(1, n_parts):
            g = g + p_ref[q].astype(F32)
        wv = w_ref[...]
        nm = ADAM_B1 * m_ref[...] + (1.0 - ADAM_B1) * g
        nv = ADAM_B2 * v_ref[...] + (1.0 - ADAM_B2) * jnp.square(g)
        m_hat = nm / (1.0 - ADAM_B1 ** ADAM_STEP)
        v_hat = nv / (1.0 - ADAM_B2 ** ADAM_STEP)
        g_ref[...] = g
        d_ref[...] = -ADAM_LR * (m_hat / (jnp.sqrt(v_hat) + ADAM_EPS) + ADAM_WD * wv)
        nm_ref[...] = nm
        nv_ref[...] = nv

    blk = pl.BlockSpec((rb, cols), lambda i: (i, 0))
    return pl.pallas_call(
        body, name=name, out_shape=[jax.ShapeDtypeStruct((rows, cols), F32)] * 4, grid=(rows // rb,),
        in_specs=[blk, blk, blk, pl.BlockSpec((n_parts, rb, cols), lambda i: (0, i, 0))], out_specs=[blk] * 4,
        compiler_params=_params(dimension_semantics=("arbitrary",)),
    )(w, m, v, parts)


BIG = ("w_in", "w_attn_proj", "w_conv_proj", "w_out", "w_cq", "w_ckv", "w_co", "w_up", "w_down")
SMALL = ("g_mix", "b_gate", "conv_b", "conv_ln_g", "conv_ln_b", "g_cross", "g_mem", "g_mlp", "g_final")
PACK_ORDER = SMALL + ("loss", "conv_w")
PACK_ROWS = 272
WEIGHTS = ("g_mix", "w_in", "b_gate", "conv_w", "conv_b", "conv_ln_g", "conv_ln_b", "w_attn_proj", "w_conv_proj", "w_out",
           "g_cross", "g_mem", "w_cq", "w_ckv", "w_co", "g_mlp", "w_up", "w_down", "g_final")


def _pack(d):
    flat = jnp.concatenate([d[k].reshape(-1) for k in PACK_ORDER])
    return jnp.pad(flat, (0, PACK_ROWS * HEAD_DIM - flat.shape[0])).reshape(PACK_ROWS, HEAD_DIM)


def _unpack(p, sizes):
    flat, out, off = p.reshape(-1), {}, 0
    for k in PACK_ORDER:
        out[k] = flat[off:off + sizes[k]]
        off += sizes[k]
    return out


def kernel(x, mem, g_mix, w_in, b_gate, conv_w, conv_b, conv_ln_g, conv_ln_b, w_attn_proj, w_conv_proj, w_out, g_cross, g_mem, w_cq, w_ckv, w_co, g_mlp, w_up, w_down, g_final, loss_target, m_g_mix, m_w_in, m_b_gate, m_conv_w, m_conv_b, m_conv_ln_g, m_conv_ln_b, m_w_attn_proj, m_w_conv_proj, m_w_out, m_g_cross, m_g_mem, m_w_cq, m_w_ckv, m_w_co, m_g_mlp, m_w_up, m_w_down, m_g_final, v_g_mix, v_w_in, v_b_gate, v_conv_w, v_conv_b, v_conv_ln_g, v_conv_ln_b, v_w_attn_proj, v_w_conv_proj, v_w_out, v_g_cross, v_g_mem, v_w_cq, v_w_ckv, v_w_co, v_g_mlp, v_w_up, v_w_down, v_g_final):
    args = dict(locals())
    wts = {k: args[k] for k in WEIGHTS}
    mom = {k: args["m_" + k] for k in WEIGHTS}
    var = {k: args["v_" + k] for k in WEIGHTS}
    two_d = lambda a: a.reshape(a.shape[-2:]) if a.ndim == 3 else a.reshape(1, -1)

    taps = jnp.pad(two_d(conv_w), ((0, 1), (0, HEAD_DIM - conv_w.shape[-1])))
    gathered = _all_gather([two_d(wts[k]) for k in BIG] + [taps], [BF16] * len(BIG) + [F32])
    gw = dict(zip(BIG, gathered[:-1]))
    cols_to_2d = lambda a: a.transpose(1, 0, 2).reshape(a.shape[1], -1)
    w = dict(w_in=gw["w_in"], w_up=gw["w_up"], w_ckv=gw["w_ckv"],
             w_attn_proj=cols_to_2d(gw["w_attn_proj"]), w_conv_proj=cols_to_2d(gw["w_conv_proj"]),
             w_out=gw["w_out"].reshape(D_MODEL, D_MODEL), w_cq=gw["w_cq"].reshape(D_MODEL, D_MODEL),
             w_co=gw["w_co"].reshape(D_MODEL, D_MODEL), w_down=gw["w_down"].reshape(D_FF, D_MODEL))
    taps_full = cols_to_2d(gathered[-1][:, :CONV_K, :conv_w.shape[-1]])
    sm = {k: two_d(wts[k]) for k in SMALL}
    sm["conv_w"] = jnp.pad(taps_full, ((0, 1), (0, 0)))

    grad_x, dw, small = _local_step(x[0], mem[0], loss_target[0], sm, w)

    chunked = [dw[k].reshape((N_DEV,) + two_d(wts[k]).shape) for k in BIG]
    recv = _exchange(chunked, [_pack(small)])
    parts = dict(zip(BIG, recv[:-1]))

    out = {}
    for k in BIG:
        res = _adamw("adamw_" + k, two_d(wts[k]), two_d(mom[k]), two_d(var[k]), parts[k])
        out[k] = [r.reshape(wts[k].shape) for r in res]
    sizes = {k: small[k].size for k in PACK_ORDER}
    zeros = dict(loss=jnp.zeros_like(small["loss"]), conv_w=jnp.zeros_like(small["conv_w"]))
    packed = _adamw("adamw_small", _pack({**{k: wts[k] for k in SMALL}, **zeros}), _pack({**{k: mom[k] for k in SMALL}, **zeros}),
                    _pack({**{k: var[k] for k in SMALL}, **zeros}), recv[-1])
    unpacked = [_unpack(p, sizes) for p in packed]
    for k in SMALL:
        out[k] = [u[k].reshape(wts[k].shape) for u in unpacked]
    loss = unpacked[0]["loss"][0]
    me = 4 * lax.axis_index("x") + 2 * lax.axis_index("y") + lax.axis_index("c")
    n_tap_cols = conv_w.shape[-1]
    g_taps = lax.dynamic_slice(unpacked[0]["conv_w"].reshape(CONV_HALO, CONV_CH), (0, me * n_tap_cols), (CONV_K, n_tap_cols))
    res = _adamw("adamw_conv_w", two_d(conv_w), two_d(m_conv_w), two_d(v_conv_w), g_taps[None])
    out["conv_w"] = [r.reshape(conv_w.shape) for r in res]

    return (loss, grad_x[None], *[out[k][0] for k in WEIGHTS], *[out[k][1] for k in WEIGHTS],
            *[out[k][2] for k in WEIGHTS], *[out[k][3] for k in WEIGHTS])
```

```python
import functools

import jax
import jax.numpy as jnp
from jax import lax
from jax.experimental import pallas as pl
from jax.experimental.pallas import tpu as pltpu

F32 = jnp.float32
BF16 = jnp.bfloat16

N_DEV = 8
D_MODEL = 1024
N_MEM = 256
HEAD_DIM = 128
HEADS_PER_GROUP = 4
GROUP_W = HEADS_PER_GROUP * HEAD_DIM
DILATIONS = (1, 4, 16)
BAND = 128
N_GROUPS = 3
ATTN_W = N_GROUPS * GROUP_W
QKV_W = 3 * ATTN_W
ROT_DIM = HEAD_DIM // 4
ROPE_THETA = 500000.0
CONV_CH = 768
CONV_K = 31
CONV_HALO = 32
IN_W = 8192
GLU_COL_BLK = QKV_W // (2 * CONV_CH)
GATE_COL_BLK = (QKV_W + 2 * CONV_CH) // (2 * D_MODEL)
CROSS_HEADS = 4
CROSS_HD = D_MODEL // CROSS_HEADS
D_FF = 4096
EPS = 1e-6
NEG = -1e30
QB = 4
ROW_BLK = QB * BAND

ADAM_LR = 0.001
ADAM_B1 = 0.9
ADAM_B2 = 0.999
ADAM_EPS = 1e-08
ADAM_WD = 0.01
ADAM_STEP = 10

VMEM_LIMIT = 56 * 1024 * 1024
MESH = pl.DeviceIdType.MESH


def _params(**kw):
    return pltpu.CompilerParams(vmem_limit_bytes=VMEM_LIMIT, **kw)


def _sigmoid(x):
    return 1.0 / (1.0 + jnp.exp(-x))


def _dot(a, b, kind):
    dims = {"nn": (((1,), (0,)), ((), ())), "nt": (((1,), (1,)), ((), ())), "tn": (((0,), (0,)), ((), ()))}[kind]
    if a.dtype != BF16:
        a = a.astype(BF16)
    if b.dtype != BF16:
        b = b.astype(BF16)
    return lax.dot_general(a, b, dims, preferred_element_type=F32)


def _peers():
    x, y, c = lax.axis_index("x"), lax.axis_index("y"), lax.axis_index("c")
    me = 4 * x + 2 * y + c
    peers = [(x, y, 1 - c), (1 - x, y, c), (x, 1 - y, c), (1 - x, 1 - y, c),
             (1 - x, y, 1 - c), (x, 1 - y, 1 - c), (1 - x, 1 - y, 1 - c)]
    return me, peers


class _Comm:
    def __init__(self, chunked=(), replicated=()):
        self.arrays = list(chunked) + list(replicated)
        self.n_c = len(chunked)
        self.n = len(self.arrays)
        self.out_shape = [jax.ShapeDtypeStruct(a.shape, a.dtype) for a in chunked]
        self.out_shape += [jax.ShapeDtypeStruct((N_DEV,) + a.shape, a.dtype) for a in replicated]
        self.in_specs = [pl.BlockSpec(memory_space=pl.ANY)] * self.n
        self.out_specs = [pl.BlockSpec(memory_space=pl.ANY)] * self.n
        self.scratch = [pltpu.SemaphoreType.DMA((self.n,))] * 3 if self.n else []

    def _local(self, ins, outs, sems, a, me):
        src = ins[a].at[me] if a < self.n_c else ins[a]
        return pltpu.make_async_copy(src, outs[a].at[me], sems[2].at[a])

    def start(self, ins, outs, sems):
        me, peers = _peers()
        for a in range(self.n):
            self._local(ins, outs, sems, a, me).start()
            for (px, py, pc) in peers:
                src = ins[a].at[4 * px + 2 * py + pc] if a < self.n_c else ins[a]
                pltpu.make_async_remote_copy(src_ref=src, dst_ref=outs[a].at[me], send_sem=sems[0].at[a],
                                             recv_sem=sems[1].at[a], device_id=(px, py, pc), device_id_type=MESH).start()

    def wait(self, ins, outs, sems):
        me, peers = _peers()
        for a in range(self.n):
            seven = outs[a].at[pl.ds(0, N_DEV - 1)]
            cp = pltpu.make_async_remote_copy(src_ref=seven, dst_ref=seven, send_sem=sems[0].at[a], recv_sem=sems[1].at[a],
                                              device_id=peers[0], device_id_type=MESH)
            cp.wait_send()
            cp.wait_recv()
            self._local(ins, outs, sems, a, me).wait()


_NO_COMM = _Comm()


def _pcall(body, name, grid, operands, in_specs, out_shape, out_specs, scratch=(), aliases=None, comm=_NO_COMM):
    n_in, n_out, n_scr = len(operands), len(out_shape), len(scratch)
    grid = tuple(grid)

    def carried(*refs):
        ins, c_in = refs[:n_in], refs[n_in:n_in + comm.n]
        o0 = n_in + comm.n
        outs, c_out = refs[o0:o0 + n_out], refs[o0 + n_out:o0 + n_out + comm.n]
        s0 = o0 + n_out + comm.n
        scr, sems = refs[s0:s0 + n_scr], refs[s0 + n_scr:]
        ids = [pl.program_id(ax) for ax in range(len(grid))]

        @pl.when(functools.reduce(jnp.logical_and, [p == 0 for p in ids]))
        def _():
            comm.start(c_in, c_out, sems)

        body(*ins, *outs, *scr)

        @pl.when(functools.reduce(jnp.logical_and, [p == g - 1 for p, g in zip(ids, grid)]))
        def _():
            comm.wait(c_in, c_out, sems)

    return pl.pallas_call(
        carried if comm.n else body, name=name, grid=grid, in_specs=list(in_specs) + comm.in_specs,
        out_shape=list(out_shape) + comm.out_shape, out_specs=list(out_specs) + comm.out_specs,
        scratch_shapes=list(scratch) + comm.scratch, input_output_aliases=aliases or {},
        compiler_params=_params(dimension_semantics=("arbitrary",) * len(grid)),
    )(*operands, *comm.arrays)


def _mm(name, a, b, kind, grid, a_blk, b_blk, outs, extras=(), epi=None, acc_outs=(), j_outer=False, comm=_NO_COMM):
    gi, gj, gk = grid
    n_ex = len(extras)
    n_out = len(outs)

    def spec(blk, fn):
        return pl.BlockSpec(blk, (lambda j, i, k: fn(i, j, k)) if j_outer else fn)

    def body(*refs):
        a_ref, b_ref = refs[0], refs[1]
        ex = refs[2:2 + n_ex]
        out_refs = refs[2 + n_ex:2 + n_ex + n_out]
        acc_ref = refs[2 + n_ex + n_out] if gk > 1 else None
        i = pl.program_id(1 if j_outer else 0)
        k = pl.program_id(2)
        part = _dot(a_ref[...], b_ref[...], kind)

        def finish(acc):
            vals = epi(acc, *[e[...] for e in ex]) if epi is not None else (acc,)
            for idx, (o, v) in enumerate(zip(out_refs, vals)):
                if idx in acc_outs:
                    @pl.when(i == 0)
                    def _():
                        o[...] = v.astype(o.dtype)

                    @pl.when(i != 0)
                    def _():
                        o[...] += v.astype(o.dtype)
                else:
                    o[...] = v.astype(o.dtype)

        if gk == 1:
            finish(part)
        else:
            @pl.when(k == 0)
            def _():
                acc_ref[...] = part

            @pl.when(k != 0)
            def _():
                acc_ref[...] += part

            @pl.when(k == gk - 1)
            def _():
                finish(acc_ref[...])

    scratch = []
    if gk > 1:
        tm = a_blk[0][-1] if kind == "tn" else a_blk[0][-2]
        tn = b_blk[0][-2] if kind == "nt" else b_blk[0][-1]
        scratch = [pltpu.VMEM((tm, tn), F32)]
    return _pcall(body, name, (gj, gi, gk) if j_outer else (gi, gj, gk), [a, b] + [e for e, _, _ in extras],
                  [spec(*a_blk), spec(*b_blk)] + [spec(blk, fn) for _, blk, fn in extras],
                  [s for s, _, _ in outs], [spec(blk, fn) for _, blk, fn in outs], scratch, comm=comm)


def _rms_fwd_vals(x, g):
    r = lax.rsqrt(jnp.mean(x * x, axis=-1, keepdims=True) + EPS)
    return x * r * g


def _rms_bwd_vals(x, g, du):
    r = lax.rsqrt(jnp.mean(x * x, axis=-1, keepdims=True) + EPS)
    xh = x * r
    dxh = du * g
    dx = r * (dxh - xh * jnp.mean(dxh * xh, axis=-1, keepdims=True))
    return dx, jnp.sum(du * xh, axis=0, keepdims=True)


def _rms_fwd(name, x, g, rows):
    n = x.shape[0]

    def body(x_ref, g_ref, o_ref):
        o_ref[...] = _rms_fwd_vals(x_ref[...], g_ref[...]).astype(BF16)

    return pl.pallas_call(
        body, name=name, out_shape=jax.ShapeDtypeStruct(x.shape, BF16), grid=(n // rows,),
        in_specs=[pl.BlockSpec((rows, D_MODEL), lambda i: (i, 0)), pl.BlockSpec((1, D_MODEL), lambda i: (0, 0))],
        out_specs=pl.BlockSpec((rows, D_MODEL), lambda i: (i, 0)),
        compiler_params=_params(dimension_semantics=("arbitrary",)),
    )(x, g)


def _rope_tables(seq):
    half = ROT_DIM // 2
    pos = jnp.arange(seq, dtype=F32)
    inv_freq = ROPE_THETA ** (-jnp.arange(0, ROT_DIM, 2, dtype=F32) / ROT_DIM)
    ang = pos[:, None] * inv_freq[None, :]
    cos, sin = jnp.cos(ang), jnp.sin(ang)
    rest = HEAD_DIM - ROT_DIM
    c = jnp.concatenate([cos, cos, jnp.ones((seq, rest), F32)], axis=1)
    s1 = jnp.concatenate([jnp.zeros((seq, half), F32), sin, jnp.zeros((seq, rest), F32)], axis=1)
    s2 = jnp.concatenate([-sin, jnp.zeros((seq, half + rest), F32)], axis=1)
    return c, s1, s2


def _group_shapes(seq, width, dtype):
    return [jax.ShapeDtypeStruct((d, seq // d, width), dtype) for d in DILATIONS]


def _group_specs(width):
    return [pl.BlockSpec((d, ROW_BLK // d, width), lambda i: (0, i, 0)) for d in DILATIONS]


def _qkv_prep(z, tabs):
    seq = z.shape[0]

    def body(z_ref, c_ref, s1_ref, s2_ref, a0, a1, a2, sc):
        outs = (a0, a1, a2)
        c, s1, s2 = c_ref[...], s1_ref[...], s2_ref[...]
        for part in range(3):
            for hh in range(N_GROUPS * HEADS_PER_GROUP):
                g, hl = divmod(hh, HEADS_PER_GROUP)
                col = part * ATTN_W + hh * HEAD_DIM
                ocol = part * GROUP_W + hl * HEAD_DIM
                x = z_ref[:, col:col + HEAD_DIM].astype(F32)
                if part < 2:
                    x = x * c + pltpu.roll(x, ROT_DIM // 2, 1) * s1 + pltpu.roll(x, HEAD_DIM - ROT_DIM // 2, 1) * s2
                d = DILATIONS[g]
                if d == 1:
                    outs[g][0, :, ocol:ocol + HEAD_DIM] = x.astype(BF16)
                else:
                    sc[...] = x
                    for r in range(d):
                        outs[g][r, :, ocol:ocol + HEAD_DIM] = sc[pl.ds(r, ROW_BLK // d, stride=d), :].astype(BF16)

    tab_spec = pl.BlockSpec((ROW_BLK, HEAD_DIM), lambda i: (i, 0))
    return pl.pallas_call(
        body, name="qkv_prep", out_shape=_group_shapes(seq, ATTN_W, BF16), grid=(seq // ROW_BLK,),
        in_specs=[pl.BlockSpec((ROW_BLK, QKV_W), lambda i: (i, 0)), tab_spec, tab_spec, tab_spec],
        out_specs=_group_specs(ATTN_W), scratch_shapes=[pltpu.VMEM((ROW_BLK, HEAD_DIM), F32)],
        compiler_params=_params(dimension_semantics=("arbitrary",)),
    )(z, *tabs)


def _band_masks():
    qi = lax.broadcasted_iota(jnp.int32, (BAND, BAND), 0)
    kj = lax.broadcasted_iota(jnp.int32, (BAND, BAND), 1)
    return kj >= qi, kj <= qi


def _attn_fwd(name, a_g, comm=_NO_COMM):
    dil, m_len, _ = a_g.shape
    qb = min(QB, m_len // BAND)
    rows = qb * BAND
    steps = m_len // rows
    scale = HEAD_DIM ** -0.5

    def body(q_ref, kc_ref, vc_ref, kp_ref, vp_ref, o_ref, l_ref):
        t = pl.program_id(1)
        m_prev, m_cur = _band_masks()
        for sb in range(qb):
            rs = slice(sb * BAND, (sb + 1) * BAND)
            for h in range(HEADS_PER_GROUP):
                cs = slice(h * HEAD_DIM, (h + 1) * HEAD_DIM)
                q = q_ref[rs, cs]
                if sb == 0:
                    kp, vp = kp_ref[:, cs], vp_ref[:, cs]
                    pm = jnp.logical_and(m_prev, t > 0)
                else:
                    ps = slice((sb - 1) * BAND, sb * BAND)
                    kp, vp = kc_ref[ps, cs], vc_ref[ps, cs]
                    pm = m_prev
                kc, vc = kc_ref[rs, cs], vc_ref[rs, cs]
                sp = jnp.where(pm, _dot(q, kp, "nt") * scale, NEG)
                sc = jnp.where(m_cur, _dot(q, kc, "nt") * scale, NEG)
                mx = jnp.maximum(jnp.max(sp, axis=-1, keepdims=True), jnp.max(sc, axis=-1, keepdims=True))
                pp = jnp.exp(sp - mx)
                pc = jnp.exp(sc - mx)
                den = jnp.sum(pp, axis=-1, keepdims=True) + jnp.sum(pc, axis=-1, keepdims=True)
                o = (_dot(pp, vp, "nn") + _dot(pc, vc, "nn")) / den
                o_ref[rs, cs] = o
                l_ref[rs, cs] = jnp.broadcast_to(mx + jnp.log(den), (BAND, HEAD_DIM))

    def prev(r, t):
        return jnp.maximum(qb * t - 1, 0)

    cur = lambda c: pl.BlockSpec((None, rows, GROUP_W), lambda r, t, c=c: (r, t, c))
    prv = lambda c: pl.BlockSpec((None, BAND, GROUP_W), lambda r, t, c=c: (r, prev(r, t), c))
    out_spec = pl.BlockSpec((None, rows, GROUP_W), lambda r, t: (r, t, 0))
    shp = jax.ShapeDtypeStruct((dil, m_len, GROUP_W), F32)
    return _pcall(body, name, (dil, steps), [a_g] * 5, [cur(0), cur(1), cur(2), prv(1), prv(2)], [shp, shp],
                  [out_spec, out_spec], comm=comm)


def _attn_merge(os_, ls_, seq):
    def body(o0, l0, o1, l1, o2, l2, at_ref, lt_ref, sc):
        for h in range(HEADS_PER_GROUP):
            cs = slice(h * HEAD_DIM, (h + 1) * HEAD_DIM)
            for gi, (o_r, l_r) in enumerate(((o1, l1), (o2, l2))):
                d = DILATIONS[gi + 1]
                for r in range(d):
                    sc.at[2 * gi][pl.ds(r, ROW_BLK // d, stride=d), :] = o_r[r, :, cs]
                    sc.at[2 * gi + 1][pl.ds(r, ROW_BLK // d, stride=d), :] = l_r[r, :, cs]
            o0v, l0v = o0[0, :, cs], l0[0, :, cs]
            o1v, l1v, o2v, l2v = sc[0], sc[1], sc[2], sc[3]
            mx = jnp.maximum(jnp.maximum(l0v, l1v), l2v)
            e0, e1, e2 = jnp.exp(l0v - mx), jnp.exp(l1v - mx), jnp.exp(l2v - mx)
            tot = e0 + e1 + e2
            at_ref[:, cs] = ((e0 * o0v + e1 * o1v + e2 * o2v) / tot).astype(BF16)
            lt_ref[:, cs] = mx + jnp.log(tot)

    gs = _group_specs(GROUP_W)
    row = pl.BlockSpec((ROW_BLK, GROUP_W), lambda i: (i, 0))
    return pl.pallas_call(
        body, name="attn_merge",
        out_shape=[jax.ShapeDtypeStruct((seq, GROUP_W), BF16), jax.ShapeDtypeStruct((seq, GROUP_W), F32)],
        grid=(seq // ROW_BLK,), in_specs=[gs[0], gs[0], gs[1], gs[1], gs[2], gs[2]], out_specs=[row, row],
        scratch_shapes=[pltpu.VMEM((4, ROW_BLK, HEAD_DIM), F32)],
        compiler_params=_params(dimension_semantics=("arbitrary",)),
    )(os_[0], ls_[0], os_[1], ls_[1], os_[2], ls_[2])


def _attn_bwd_prep(dattn, attn, lt):
    seq = dattn.shape[0]

    def body(da_ref, at_ref, lt_ref, d0, c0, t0, d1, c1, t1, d2, c2, t2, sc):
        outs = ((d0, c0, t0), (d1, c1, t1), (d2, c2, t2))
        for h in range(HEADS_PER_GROUP):
            cs = slice(h * HEAD_DIM, (h + 1) * HEAD_DIM)
            da = da_ref[:, cs].astype(F32)
            cc = jnp.broadcast_to(jnp.sum(da * at_ref[:, cs].astype(F32), axis=-1, keepdims=True), (ROW_BLK, HEAD_DIM))
            ltv = lt_ref[:, cs]
            d0[0, :, cs] = da_ref[:, cs]
            c0[0, :, cs] = cc
            t0[0, :, cs] = ltv
            sc[0], sc[1], sc[2] = da, cc, ltv
            for g in (1, 2):
                d = DILATIONS[g]
                for r in range(d):
                    rows = pl.ds(r, ROW_BLK // d, stride=d)
                    outs[g][0][r, :, cs] = sc.at[0][rows, :].astype(BF16)
                    outs[g][1][r, :, cs] = sc.at[1][rows, :]
                    outs[g][2][r, :, cs] = sc.at[2][rows, :]

    gs = _group_specs(GROUP_W)
    row = pl.BlockSpec((ROW_BLK, GROUP_W), lambda i: (i, 0))
    shapes, specs = [], []
    for g, d in enumerate(DILATIONS):
        for dt in (BF16, F32, F32):
            shapes.append(jax.ShapeDtypeStruct((d, seq // d, GROUP_W), dt))
            specs.append(gs[g])
    res = pl.pallas_call(
        body, name="attn_bwd_prep", out_shape=shapes, grid=(seq // ROW_BLK,), in_specs=[row, row, row],
        out_specs=specs, scratch_shapes=[pltpu.VMEM((3, ROW_BLK, HEAD_DIM), F32)],
        compiler_params=_params(dimension_semantics=("arbitrary",)),
    )(dattn, attn, lt)
    return [res[3 * g:3 * g + 3] for g in range(N_GROUPS)]


def _attn_bwd(name, a_g, da_g, c_g, lt_g):
    dil, m_len, _ = a_g.shape
    qb = min(QB, m_len // BAND)
    rows = qb * BAND
    steps = m_len // rows
    scale = HEAD_DIM ** -0.5

    def body(q_ref, kc_ref, vc_ref, kp_ref, vp_ref, da_ref, c_ref, lt_ref, d_ref, dk_acc, dv_acc, car_k, car_v):
        tg = pl.program_id(1)
        t = steps - 1 - tg
        m_prev, m_cur = _band_masks()

        @pl.when(tg == 0)
        def _():
            car_k[...] = jnp.zeros_like(car_k)
            car_v[...] = jnp.zeros_like(car_v)

        zero = jnp.zeros((rows, GROUP_W), F32)
        dk_acc[0:rows, :] = zero
        dv_acc[0:rows, :] = zero
        dk_acc[rows:rows + BAND, :] = car_k[...]
        dv_acc[rows:rows + BAND, :] = car_v[...]
        for sb in range(qb):
            rs = slice(sb * BAND, (sb + 1) * BAND)
            ps = slice((sb - 1) * BAND, sb * BAND)
            acc_p = slice(sb * BAND, (sb + 1) * BAND)
            acc_c = slice((sb + 1) * BAND, (sb + 2) * BAND)
            for h in range(HEADS_PER_GROUP):
                cs = slice(h * HEAD_DIM, (h + 1) * HEAD_DIM)
                q, da = q_ref[rs, cs], da_ref[rs, cs]
                cc, ltv = c_ref[rs, cs], lt_ref[rs, cs]
                if sb == 0:
                    kp, vp = kp_ref[:, cs], vp_ref[:, cs]
                    pm = jnp.logical_and(m_prev, t > 0)
                else:
                    kp, vp = kc_ref[ps, cs], vc_ref[ps, cs]
                    pm = m_prev
                kc, vc = kc_ref[rs, cs], vc_ref[rs, cs]
                pp = jnp.exp(jnp.where(pm, _dot(q, kp, "nt") * scale - ltv, NEG))
                pc = jnp.exp(jnp.where(m_cur, _dot(q, kc, "nt") * scale - ltv, NEG))
                dsp = (pp * (_dot(da, vp, "nt") - cc) * scale).astype(BF16)
                dsc = (pc * (_dot(da, vc, "nt") - cc) * scale).astype(BF16)
                ppb, pcb = pp.astype(BF16), pc.astype(BF16)
                d_ref[rs, cs] = (_dot(dsp, kp, "nn") + _dot(dsc, kc, "nn")).astype(BF16)
                dk_acc[acc_p, cs] += _dot(dsp, q, "tn")
                dk_acc[acc_c, cs] += _dot(dsc, q, "tn")
                dv_acc[acc_p, cs] += _dot(ppb, da, "tn")
                dv_acc[acc_c, cs] += _dot(pcb, da, "tn")
        d_ref[:, GROUP_W:2 * GROUP_W] = dk_acc[BAND:rows + BAND, :].astype(BF16)
        d_ref[:, 2 * GROUP_W:3 * GROUP_W] = dv_acc[BAND:rows + BAND, :].astype(BF16)
        car_k[...] = dk_acc[0:BAND, :]
        car_v[...] = dv_acc[0:BAND, :]

    def rev(tg):
        return steps - 1 - tg

    def prev(tg):
        return jnp.maximum(qb * rev(tg) - 1, 0)

    cur = lambda c: pl.BlockSpec((None, rows, GROUP_W), lambda r, tg, c=c: (r, rev(tg), c))
    prv = lambda c: pl.BlockSpec((None, BAND, GROUP_W), lambda r, tg, c=c: (r, prev(tg), c))
    return pl.pallas_call(
        body, name=name, out_shape=jax.ShapeDtypeStruct((dil, m_len, ATTN_W), BF16), grid=(dil, steps),
        in_specs=[cur(0), cur(1), cur(2), prv(1), prv(2), cur(0), cur(0), cur(0)],
        out_specs=pl.BlockSpec((None, rows, ATTN_W), lambda r, tg: (r, rev(tg), 0)),
        scratch_shapes=[pltpu.VMEM((rows + BAND, GROUP_W), F32), pltpu.VMEM((rows + BAND, GROUP_W), F32),
                        pltpu.VMEM((BAND, GROUP_W), F32), pltpu.VMEM((BAND, GROUP_W), F32)],
        compiler_params=_params(dimension_semantics=("arbitrary", "arbitrary")),
    )(a_g, a_g, a_g, a_g, a_g, da_g, c_g, lt_g)


def _dqkv_post(d_gs, tabs, dz):
    seq = dz.shape[0]

    def body(g0, g1, g2, c_ref, s1_ref, s2_ref, dz_any, o_ref, sc):
        del dz_any
        ins = (g0, g1, g2)
        c, s1, s2 = c_ref[...], s1_ref[...], s2_ref[...]
        for part in range(3):
            for hh in range(N_GROUPS * HEADS_PER_GROUP):
                g, hl = divmod(hh, HEADS_PER_GROUP)
                icol = part * GROUP_W + hl * HEAD_DIM
                ocol = part * ATTN_W + hh * HEAD_DIM
                d = DILATIONS[g]
                if d == 1:
                    x = ins[g][0, :, icol:icol + HEAD_DIM].astype(F32)
                else:
                    for r in range(d):
                        sc[pl.ds(r, ROW_BLK // d, stride=d), :] = ins[g][r, :, icol:icol + HEAD_DIM].astype(F32)
                    x = sc[...]
                if part < 2:
                    x = x * c + pltpu.roll(x * s1, HEAD_DIM - ROT_DIM // 2, 1) + pltpu.roll(x * s2, ROT_DIM // 2, 1)
                o_ref[:, ocol:ocol + HEAD_DIM] = x.astype(BF16)

    tab_spec = pl.BlockSpec((ROW_BLK, HEAD_DIM), lambda i: (i, 0))
    return pl.pallas_call(
        body, name="dqkv_post", out_shape=jax.ShapeDtypeStruct(dz.shape, BF16), grid=(seq // ROW_BLK,),
        in_specs=_group_specs(ATTN_W) + [tab_spec, tab_spec, tab_spec, pl.BlockSpec(memory_space=pl.ANY)],
        out_specs=pl.BlockSpec((ROW_BLK, QKV_W), lambda i: (i, 0)),
        scratch_shapes=[pltpu.VMEM((ROW_BLK, HEAD_DIM), F32)], input_output_aliases={6: 0},
        compiler_params=_params(dimension_semantics=("arbitrary",)),
    )(*d_gs, *tabs, dz)


def _glu(zg):
    a = zg[:, :CONV_CH].astype(F32)
    s = _sigmoid(zg[:, CONV_CH:].astype(F32))
    return a, s, a * s


def _conv_fwd(z, cw, cb, lg, lb):
    seq = z.shape[0]
    halo_per_blk = ROW_BLK // CONV_HALO

    def body(zg_ref, zh_ref, cw_ref, cb_ref, lg_ref, lb_ref, c2_ref, c4_ref, xpad):
        i = pl.program_id(0)
        _, _, c1 = _glu(zg_ref[...])
        _, _, c1h = _glu(zh_ref[...])
        xpad[0:CONV_HALO, :] = jnp.where(i > 0, c1h, 0.0)
        xpad[CONV_HALO:, :] = c1
        acc = jnp.zeros((ROW_BLK, CONV_CH), F32)
        for j in range(CONV_K):
            acc = acc + cw_ref[j:j + 1, :] * xpad[pl.ds(CONV_HALO - (CONV_K - 1) + j, ROW_BLK), :]
        c2 = acc + cb_ref[...]
        c2_ref[...] = c2
        mu = jnp.mean(c2, axis=-1, keepdims=True)
        xc = c2 - mu
        rstd = lax.rsqrt(jnp.mean(xc * xc, axis=-1, keepdims=True) + EPS)
        c3 = xc * rstd * lg_ref[...] + lb_ref[...]
        c4_ref[...] = (c3 * _sigmoid(c3)).astype(BF16)

    vec = pl.BlockSpec((1, CONV_CH), lambda i: (0, 0))
    return pl.pallas_call(
        body, name="conv_fwd",
        out_shape=[jax.ShapeDtypeStruct((seq, CONV_CH), F32), jax.ShapeDtypeStruct((seq, CONV_CH), BF16)],
        grid=(seq // ROW_BLK,),
        in_specs=[pl.BlockSpec((ROW_BLK, 2 * CONV_CH), lambda i: (i, GLU_COL_BLK)),
                  pl.BlockSpec((CONV_HALO, 2 * CONV_CH), lambda i: (jnp.maximum(i * halo_per_blk - 1, 0), GLU_COL_BLK)),
                  pl.BlockSpec((CONV_HALO, CONV_CH), lambda i: (0, 0)), vec, vec, vec],
        out_specs=[pl.BlockSpec((ROW_BLK, CONV_CH), lambda i: (i, 0)), pl.BlockSpec((ROW_BLK, CONV_CH), lambda i: (i, 0))],
        scratch_shapes=[pltpu.VMEM((ROW_BLK + CONV_HALO, CONV_CH), F32)],
        compiler_params=_params(dimension_semantics=("arbitrary",)),
    )(z, z, cw, cb, lg, lb)


def _conv_bwd(dc2, z, cw, dz, comm=_NO_COMM):
    seq = z.shape[0]
    halo_per_blk = ROW_BLK // CONV_HALO
    n_blk = seq // ROW_BLK
    last_halo = seq // CONV_HALO - 1

    def body(dc_ref, dn_ref, zg_ref, zh_ref, cw_ref, dz_any, o_ref, dcw_ref, xpad, ypad):
        del dz_any
        i = pl.program_id(0)
        a, s, c1 = _glu(zg_ref[...])
        _, _, c1h = _glu(zh_ref[...])
        xpad[0:CONV_HALO, :] = jnp.where(i > 0, c1h, 0.0)
        xpad[CONV_HALO:, :] = c1
        dc = dc_ref[...]
        ypad[0:ROW_BLK, :] = dc
        ypad[ROW_BLK:, :] = jnp.where(i < n_blk - 1, dn_ref[...], 0.0)

        @pl.when(i == 0)
        def _():
            dcw_ref[...] = jnp.zeros_like(dcw_ref)

        dc1 = jnp.zeros((ROW_BLK, CONV_CH), F32)
        for j in range(CONV_K):
            xs = xpad[pl.ds(CONV_HALO - (CONV_K - 1) + j, ROW_BLK), :]
            dcw_ref[j:j + 1, :] += jnp.sum(dc * xs, axis=0, keepdims=True)
            dc1 = dc1 + cw_ref[j:j + 1, :] * ypad[pl.ds(CONV_K - 1 - j, ROW_BLK), :]
        o_ref[:, :CONV_CH] = (dc1 * s).astype(BF16)
        o_ref[:, CONV_CH:] = (dc1 * a * s * (1.0 - s)).astype(BF16)

    return _pcall(
        body, "conv_bwd", (n_blk,), [dc2, dc2, z, z, cw, dz],
        [pl.BlockSpec((ROW_BLK, CONV_CH), lambda i: (i, 0)),
         pl.BlockSpec((CONV_HALO, CONV_CH), lambda i: (jnp.minimum((i + 1) * halo_per_blk, last_halo), 0)),
         pl.BlockSpec((ROW_BLK, 2 * CONV_CH), lambda i: (i, GLU_COL_BLK)),
         pl.BlockSpec((CONV_HALO, 2 * CONV_CH), lambda i: (jnp.maximum(i * halo_per_blk - 1, 0), GLU_COL_BLK)),
         pl.BlockSpec((CONV_HALO, CONV_CH), lambda i: (0, 0)),
         pl.BlockSpec(memory_space=pl.ANY)],
        [jax.ShapeDtypeStruct(dz.shape, BF16), jax.ShapeDtypeStruct((CONV_HALO, CONV_CH), F32)],
        [pl.BlockSpec((ROW_BLK, 2 * CONV_CH), lambda i: (i, GLU_COL_BLK)), pl.BlockSpec((CONV_HALO, CONV_CH), lambda i: (0, 0))],
        [pltpu.VMEM((ROW_BLK + CONV_HALO, CONV_CH), F32), pltpu.VMEM((ROW_BLK + CONV_HALO, CONV_CH), F32)],
        aliases={5: 0}, comm=comm)


def _epi_mix(ya, c4, wcp, gates, bg):
    yc = _dot(c4, wcp, "nn")
    gv = _sigmoid(gates.astype(F32) + bg)
    merged = gv[:, :D_MODEL] * ya + gv[:, D_MODEL:] * yc
    return merged, ya, yc


def _epi_residual_rms(acc, xres, g):
    x = xres + acc
    return x, _rms_fwd_vals(x, g)


def _cross_scores(cq, ck):
    out = []
    for h in range(CROSS_HEADS):
        cs = slice(h * CROSS_HD, (h + 1) * CROSS_HD)
        s = _dot(cq[:, cs], ck[:, cs], "nt") * (CROSS_HD ** -0.5)
        e = jnp.exp(s - jnp.max(s, axis=-1, keepdims=True))
        out.append((cs, e, jnp.sum(e, axis=-1, keepdims=True)))
    return out


def _epi_cross_fwd(acc, ck, cv):
    cq = acc.astype(BF16)
    co = [_dot(e, cv[:, cs], "nn") / den for cs, e, den in _cross_scores(cq, ck)]
    return cq, jnp.concatenate(co, axis=1)


def _epi_cross_bwd(dco, cq, ck, cv):
    dco = dco.astype(BF16)
    dcq, dck, dcv = [], [], []
    for cs, e, den in _cross_scores(cq, ck):
        p = e / den
        dp = _dot(dco[:, cs], cv[:, cs], "nt")
        ds = (p * (dp - jnp.sum(dp * p, axis=-1, keepdims=True)) * (CROSS_HD ** -0.5)).astype(BF16)
        dcq.append(_dot(ds, ck[:, cs], "nn"))
        dck.append(_dot(ds, cq[:, cs], "tn"))
        dcv.append(_dot(p, dco[:, cs], "tn"))
    return jnp.concatenate(dcq, axis=1), jnp.concatenate(dck, axis=1), jnp.concatenate(dcv, axis=1)


def _epi_mlp_up(acc):
    return acc, jnp.square(jnp.maximum(acc, 0.0))


def _epi_final(acc, x2, tgt, g):
    x3 = x2 + acc
    err = _rms_fwd_vals(x3, g) - tgt
    loss = (0.5 / D_MODEL) * jnp.sum(err * err)
    dx3, dg = _rms_bwd_vals(x3, g, err * (1.0 / D_MODEL))
    return dx3, jnp.full((1, HEAD_DIM), loss, F32), dg


def _epi_mlp_down_bwd(dh, hpre):
    return (dh * 2.0 * jnp.maximum(hpre.astype(F32), 0.0),)


def _epi_rms_bwd(du, x, g, dres):
    dx, dg = _rms_bwd_vals(x, g, du)
    return dres + dx, dg


def _epi_rms_bwd_g(du, x, g):
    return (_rms_bwd_vals(x, g, du)[1],)


def _epi_mix_bwd(dm, ya, yc, gates, bg):
    gv = _sigmoid(gates.astype(F32) + bg)
    ga, gb = gv[:, :D_MODEL], gv[:, D_MODEL:]
    ya, yc = ya.astype(F32), yc.astype(F32)
    dgate = jnp.concatenate([dm * ya * ga * (1.0 - ga), dm * yc * gb * (1.0 - gb)], axis=1)
    return dm * ga, dm * gb, dgate, jnp.sum(dgate, axis=0, keepdims=True)


def _epi_ln_bwd(dc4, c2, lg, lb):
    mu = jnp.mean(c2, axis=-1, keepdims=True)
    xc = c2 - mu
    rstd = lax.rsqrt(jnp.mean(xc * xc, axis=-1, keepdims=True) + EPS)
    xh = xc * rstd
    c3 = xh * lg + lb
    sg = _sigmoid(c3)
    dc3 = dc4 * sg * (1.0 + c3 * (1.0 - sg))
    dxh = dc3 * lg
    dc2 = rstd * (dxh - jnp.mean(dxh, axis=-1, keepdims=True) - xh * jnp.mean(dxh * xh, axis=-1, keepdims=True))
    return (dc2, jnp.sum(dc3 * xh, axis=0, keepdims=True), jnp.sum(dc3, axis=0, keepdims=True),
            jnp.sum(dc2, axis=0, keepdims=True))


def _sds(shape, dtype):
    return jax.ShapeDtypeStruct(shape, dtype)


class _Lazy:
    def __init__(self, fn):
        self.fn = fn

    def __getitem__(self, key):
        return self.fn(key)


def _local_step(x, mem, tgt, sm, plan):
    w = _Lazy(plan.w)
    dw = {}

    def mm(name, *args, **kw):
        c = plan.comm(name, dw)
        res = _mm(name, *args, comm=c, **kw)
        plan.done(name, res[len(res) - c.n:])
        return res[:len(res) - c.n]

    seq = x.shape[0]
    nr = seq // ROW_BLK
    big = min(1024, seq)
    nb = seq // big
    row = lambda n: ((ROW_BLK, n), lambda i, j, k: (i, 0))
    vec = lambda n: ((1, n), lambda i, j, k: (0, 0))
    full = lambda r, c: ((r, c), lambda i, j, k: (0, 0))
    gates_blk = ((ROW_BLK, 2 * D_MODEL), lambda i, j, k: (i, GATE_COL_BLK))
    tabs = _rope_tables(seq)

    u = _rms_fwd("rms_mix", x, sm["g_mix"], ROW_BLK)
    z = mm("in_proj", u, w["w_in"], "nn", (nb, N_DEV, 1), ((big, D_MODEL), lambda i, j, k: (i, 0)),
            ((None, D_MODEL, D_MODEL), lambda i, j, k: (j, 0, 0)),
            [(_sds((seq, IN_W), BF16), (big, D_MODEL), lambda i, j, k: (i, j))], j_outer=True)[0]
    a_gs = _qkv_prep(z, tabs)
    os_, ls_ = [], []
    for g in range(N_GROUPS):
        name = "attn_fwd_%d" % g
        c = plan.comm(name, dw)
        res = _attn_fwd(name, a_gs[g], c)
        plan.done(name, res[2:])
        os_.append(res[0])
        ls_.append(res[1])
    attn, lt = _attn_merge(os_, ls_, seq)
    c2, c4 = _conv_fwd(z, w["taps"], sm["conv_b"], sm["conv_ln_g"], sm["conv_ln_b"])
    merged, ya, yc = mm(
        "mix", attn, w["w_attn_proj"], "nn", (nr, 1, 1), row(GROUP_W), full(GROUP_W, D_MODEL),
        [(_sds((seq, D_MODEL), BF16), *row(D_MODEL))] * 3,
        extras=[(c4, *row(CONV_CH)), (w["w_conv_proj"], *full(CONV_CH, D_MODEL)), (z, *gates_blk), (sm["b_gate"], *vec(2 * D_MODEL))],
        epi=_epi_mix)
    x1, uq = mm("out_proj", merged, w["w_out"], "nn", (nr, 1, 1), row(D_MODEL), full(D_MODEL, D_MODEL),
                 [(_sds((seq, D_MODEL), F32), *row(D_MODEL)), (_sds((seq, D_MODEL), BF16), *row(D_MODEL))],
                 extras=[(x, *row(D_MODEL)), (sm["g_cross"], *vec(D_MODEL))], epi=_epi_residual_rms)

    mn = _rms_fwd("rms_mem", mem, sm["g_mem"], N_MEM)
    ckv = mm("ckv_proj", mn, w["w_ckv"], "nn", (1, N_DEV, 1), full(N_MEM, D_MODEL),
              ((None, D_MODEL, 2 * D_MODEL // N_DEV), lambda i, j, k: (j, 0, 0)),
              [(_sds((N_MEM, 2 * D_MODEL), BF16), (N_MEM, 2 * D_MODEL // N_DEV), lambda i, j, k: (0, j))])[0]
    ck, cv = ckv[:, :D_MODEL], ckv[:, D_MODEL:]
    kv_blk = full(N_MEM, D_MODEL)
    cq, co = mm("cq_proj_cross", uq, w["w_cq"], "nn", (nr, 1, 1), row(D_MODEL), full(D_MODEL, D_MODEL),
                 [(_sds((seq, D_MODEL), BF16), *row(D_MODEL))] * 2,
                 extras=[(ck, *kv_blk), (cv, *kv_blk)], epi=_epi_cross_fwd)
    x2, um = mm("co_proj", co, w["w_co"], "nn", (nr, 1, 1), row(D_MODEL), full(D_MODEL, D_MODEL),
                 [(_sds((seq, D_MODEL), F32), *row(D_MODEL)), (_sds((seq, D_MODEL), BF16), *row(D_MODEL))],
                 extras=[(x1, *row(D_MODEL)), (sm["g_mlp"], *vec(D_MODEL))], epi=_epi_residual_rms)

    ff_blk = D_FF // N_DEV
    hpre, h = mm("mlp_up", um, w["w_up"], "nn", (nb, N_DEV, 1), ((big, D_MODEL), lambda i, j, k: (i, 0)),
                  ((None, D_MODEL, ff_blk), lambda i, j, k: (j, 0, 0)),
                  [(_sds((seq, D_FF), BF16), (big, ff_blk), lambda i, j, k: (i, j))] * 2, epi=_epi_mlp_up, j_outer=True)
    kt = D_FF // D_MODEL
    dx3, loss, dg_final = mm(
        "mlp_down_loss", h, w["w_down"], "nn", (nr, 1, kt), ((ROW_BLK, D_MODEL), lambda i, j, k: (i, k)),
        ((D_MODEL, D_MODEL), lambda i, j, k: (k, 0)),
        [(_sds((seq, D_MODEL), F32), *row(D_MODEL)), (_sds((1, HEAD_DIM), F32), *vec(HEAD_DIM)), (_sds((1, D_MODEL), F32), *vec(D_MODEL))],
        extras=[(x2, *row(D_MODEL)), (tgt, *row(D_MODEL)), (sm["g_final"], *vec(D_MODEL))], epi=_epi_final, acc_outs=(1, 2))

    dhpre = mm("mlp_down_bwd", dx3, w["w_down"], "nt", (nb, kt, 1), ((big, D_MODEL), lambda i, j, k: (i, 0)),
                ((D_MODEL, D_MODEL), lambda i, j, k: (j, 0)),
                [(_sds((seq, D_FF), BF16), (big, D_MODEL), lambda i, j, k: (i, j))],
                extras=[(hpre, (big, D_MODEL), lambda i, j, k: (i, j))], epi=_epi_mlp_down_bwd, j_outer=True)[0]
    dw["w_down"] = mm("dw_down", h, dx3, "tn", (kt, 1, nb), ((big, D_MODEL), lambda i, j, k: (k, i)),
                       ((big, D_MODEL), lambda i, j, k: (k, 0)),
                       [(_sds((D_FF, D_MODEL), BF16), (D_MODEL, D_MODEL), lambda i, j, k: (i, 0))])[0]
    dx2, dg_mlp = mm("mlp_up_bwd", dhpre, w["w_up"], "nt", (nr, 1, N_DEV), ((ROW_BLK, ff_blk), lambda i, j, k: (i, k)),
                      ((None, D_MODEL, ff_blk), lambda i, j, k: (k, 0, 0)),
                      [(_sds((seq, D_MODEL), F32), *row(D_MODEL)), (_sds((1, D_MODEL), F32), *vec(D_MODEL))],
                      extras=[(x2, *row(D_MODEL)), (sm["g_mlp"], *vec(D_MODEL)), (dx3, *row(D_MODEL))],
                      epi=_epi_rms_bwd, acc_outs=(1,))
    dw["w_up"] = mm("dw_up", um, dhpre, "tn", (1, N_DEV, nb), ((big, D_MODEL), lambda i, j, k: (k, 0)),
                     ((big, ff_blk), lambda i, j, k: (k, j)),
                     [(_sds((N_DEV, D_MODEL, ff_blk), BF16), (None, D_MODEL, ff_blk), lambda i, j, k: (j, 0, 0))])[0]

    acc_kv = (_sds((N_MEM, D_MODEL), F32), *kv_blk)
    dcq, dck, dcv = mm("co_proj_bwd_cross", dx2, w["w_co"], "nt", (nr, 1, 1), row(D_MODEL), full(D_MODEL, D_MODEL),
                        [(_sds((seq, D_MODEL), BF16), *row(D_MODEL)), acc_kv, acc_kv],
                        extras=[(cq, *row(D_MODEL)), (ck, *kv_blk), (cv, *kv_blk)], epi=_epi_cross_bwd, acc_outs=(1, 2))

    def dw_square(name, act, grad):
        return mm(name, act, grad, "tn", (1, 1, nb), ((big, D_MODEL), lambda i, j, k: (k, 0)),
                   ((big, D_MODEL), lambda i, j, k: (k, 0)), [(_sds((D_MODEL, D_MODEL), BF16), *full(D_MODEL, D_MODEL))])[0]

    dw["w_co"] = dw_square("dw_co", co, dx2)
    dx1, dg_cross = mm("cq_proj_bwd", dcq, w["w_cq"], "nt", (nr, 1, 1), row(D_MODEL), full(D_MODEL, D_MODEL),
                        [(_sds((seq, D_MODEL), F32), *row(D_MODEL)), (_sds((1, D_MODEL), F32), *vec(D_MODEL))],
                        extras=[(x1, *row(D_MODEL)), (sm["g_cross"], *vec(D_MODEL)), (dx2, *row(D_MODEL))],
                        epi=_epi_rms_bwd, acc_outs=(1,))
    dw["w_cq"] = dw_square("dw_cq", uq, dcq)
    dckv = jnp.concatenate([dck, dcv], axis=1)
    kv_chunk = 2 * D_MODEL // N_DEV
    dw["w_ckv"] = mm("dw_ckv", mn, dckv, "tn", (1, N_DEV, 1), full(N_MEM, D_MODEL), ((N_MEM, kv_chunk), lambda i, j, k: (0, j)),
                      [(_sds((N_DEV, D_MODEL, kv_chunk), BF16), (None, D_MODEL, kv_chunk), lambda i, j, k: (j, 0, 0))])[0]
    dg_mem = mm("ckv_proj_bwd", dckv, w["w_ckv"], "nt", (1, 1, N_DEV), ((N_MEM, kv_chunk), lambda i, j, k: (0, k)),
                 ((None, D_MODEL, kv_chunk), lambda i, j, k: (k, 0, 0)), [(_sds((1, D_MODEL), F32), *vec(D_MODEL))],
                 extras=[(mem, *full(N_MEM, D_MODEL)), (sm["g_mem"], *vec(D_MODEL))], epi=_epi_rms_bwd_g, acc_outs=(0,))[0]

    dya, dyc, dz, db_gate = mm(
        "out_proj_bwd_mix", dx1, w["w_out"], "nt", (nr, 1, 1), row(D_MODEL), full(D_MODEL, D_MODEL),
        [(_sds((seq, D_MODEL), BF16), *row(D_MODEL)), (_sds((seq, D_MODEL), BF16), *row(D_MODEL)),
         (_sds((seq, IN_W), BF16), *gates_blk), (_sds((1, 2 * D_MODEL), F32), *vec(2 * D_MODEL))],
        extras=[(ya, *row(D_MODEL)), (yc, *row(D_MODEL)), (z, *gates_blk), (sm["b_gate"], *vec(2 * D_MODEL))],
        epi=_epi_mix_bwd, acc_outs=(3,))
    dw["w_out"] = dw_square("dw_out", merged, dx1)
    dattn = mm("attn_proj_bwd", dya, w["w_attn_proj"], "nt", (nr, 1, 1), row(D_MODEL), full(GROUP_W, D_MODEL),
                [(_sds((seq, GROUP_W), BF16), *row(GROUP_W))])[0]
    pc = D_MODEL // N_DEV
    dw["w_attn_proj"] = mm("dw_attn_proj", attn, dya, "tn", (1, N_DEV, nb), ((big, GROUP_W), lambda i, j, k: (k, 0)),
                            ((big, pc), lambda i, j, k: (k, j)),
                            [(_sds((N_DEV, GROUP_W, pc), BF16), (None, GROUP_W, pc), lambda i, j, k: (j, 0, 0))])[0]
    cvec = (_sds((1, CONV_CH), F32), *vec(CONV_CH))
    dc2, dg_ln_g, dg_ln_b, dg_conv_b = mm(
        "conv_proj_bwd_ln", dyc, w["w_conv_proj"], "nt", (nr, 1, 1), row(D_MODEL), full(CONV_CH, D_MODEL),
        [(_sds((seq, CONV_CH), F32), *row(CONV_CH)), cvec, cvec, cvec],
        extras=[(c2, *row(CONV_CH)), (sm["conv_ln_g"], *vec(CONV_CH)), (sm["conv_ln_b"], *vec(CONV_CH))],
        epi=_epi_ln_bwd, acc_outs=(1, 2, 3))
    dw["w_conv_proj"] = mm("dw_conv_proj", c4, dyc, "tn", (1, N_DEV, nb), ((big, CONV_CH), lambda i, j, k: (k, 0)),
                            ((big, pc), lambda i, j, k: (k, j)),
                            [(_sds((N_DEV, CONV_CH, pc), BF16), (None, CONV_CH, pc), lambda i, j, k: (j, 0, 0))])[0]
    c = plan.comm("conv_bwd", dw)
    res = _conv_bwd(dc2, z, w["taps"], dz, c)
    plan.done("conv_bwd", res[2:])
    dz, dg_conv_w = res[:2]
    preps = _attn_bwd_prep(dattn, attn, lt)
    d_gs = [_attn_bwd("attn_bwd_%d" % g, a_gs[g], *preps[g]) for g in range(N_GROUPS)]
    dz = _dqkv_post(d_gs, tabs, dz)
    dw["w_in"] = mm("dw_in", u, dz, "tn", (1, N_DEV, nb), ((big, D_MODEL), lambda i, j, k: (k, 0)),
                    ((big, D_MODEL), lambda i, j, k: (k, j)),
                    [(_sds((N_DEV, D_MODEL, D_MODEL), BF16), (None, D_MODEL, D_MODEL), lambda i, j, k: (j, 0, 0))])[0]
    grad_x, dg_mix = mm("in_proj_bwd", dz, w["w_in"], "nt", (nr, 1, N_DEV), ((ROW_BLK, D_MODEL), lambda i, j, k: (i, k)),
                         ((None, D_MODEL, D_MODEL), lambda i, j, k: (k, 0, 0)),
                         [(_sds((seq, D_MODEL), F32), *row(D_MODEL)), (_sds((1, D_MODEL), F32), *vec(D_MODEL))],
                         extras=[(x, *row(D_MODEL)), (sm["g_mix"], *vec(D_MODEL)), (dx1, *row(D_MODEL))],
                         epi=_epi_rms_bwd, acc_outs=(1,))
    small = dict(g_mix=dg_mix, b_gate=db_gate, conv_b=dg_conv_b, conv_ln_g=dg_ln_g, conv_ln_b=dg_ln_b, g_cross=dg_cross,
                 g_mem=dg_mem, g_mlp=dg_mlp, g_final=dg_final, loss=loss, conv_w=dg_conv_w)
    return grad_x, dw, small


def _exchange(name, comm):
    return _pcall(lambda: None, name, (1,), [], [], [], [], comm=comm)


SHARD_SHAPE = dict(w_in=(1024, 1024), w_attn_proj=(512, 128), w_conv_proj=(768, 128), w_out=(128, 1024), w_cq=(128, 1024),
                   w_ckv=(1024, 256), w_co=(128, 1024), w_up=(1024, 512), w_down=(512, 1024))
FWD_CARRY = {"in_proj": ("w_attn_proj", "w_conv_proj", "w_out", "w_cq", "w_ckv", "w_co", "taps"),
             "attn_fwd_0": ("w_up",), "attn_fwd_1": ("w_down",)}
BWD_CARRY = {"mlp_up_bwd": ("w_down",),
             "conv_bwd": ("w_up", "w_co", "w_cq", "w_ckv", "w_out", "w_attn_proj", "w_conv_proj"),
             "in_proj_bwd": ("w_in",)}


def _cols_to_2d(a):
    return a.transpose(1, 0, 2).reshape(a.shape[1], -1)


class _Plan:
    def __init__(self, shards, w_in_gathered, n_tap_cols):
        self.shards = shards
        self.gathered = {"w_in": w_in_gathered}
        self.parts = {}
        self.n_tap_cols = n_tap_cols

    def comm(self, name, dw):
        if name in FWD_CARRY:
            return _Comm(replicated=[self.shards[k] for k in FWD_CARRY[name]])
        if name in BWD_CARRY:
            return _Comm(chunked=[dw[k].reshape((N_DEV,) + SHARD_SHAPE[k]) for k in BWD_CARRY[name]])
        return _NO_COMM

    def done(self, name, got):
        if name in FWD_CARRY:
            self.gathered.update(zip(FWD_CARRY[name], got))
        elif name in BWD_CARRY:
            self.parts.update(zip(BWD_CARRY[name], got))

    def w(self, key):
        g = self.gathered[key]
        if key in ("w_in", "w_up", "w_ckv"):
            return g
        if key in ("w_attn_proj", "w_conv_proj"):
            return _cols_to_2d(g)
        if key == "taps":
            return jnp.pad(_cols_to_2d(g[:, :CONV_K, :self.n_tap_cols]), ((0, 1), (0, 0)))
        return g.reshape(-1, g.shape[-1])


def _adamw(name, w, m, v, parts):
    rows, cols = w.shape
    n_parts = parts.shape[0]
    rb = rows if rows <= 256 or rows % 256 else 256

    def body(w_ref, m_ref, v_ref, p_ref, g_ref, d_ref, nm_ref, nv_ref):
        g = p_ref[0].astype(F32)
        for q in range(1, n_parts):
            g = g + p_ref[q].astype(F32)
        wv = w_ref[...]
        nm = ADAM_B1 * m_ref[...] + (1.0 - ADAM_B1) * g
        nv = ADAM_B2 * v_ref[...] + (1.0 - ADAM_B2) * jnp.square(g)
        m_hat = nm / (1.0 - ADAM_B1 ** ADAM_STEP)
        v_hat = nv / (1.0 - ADAM_B2 ** ADAM_STEP)
        g_ref[...] = g
        d_ref[...] = -ADAM_LR * (m_hat / (jnp.sqrt(v_hat) + ADAM_EPS) + ADAM_WD * wv)
        nm_ref[...] = nm
        nv_ref[...] = nv

    blk = pl.BlockSpec((rb, cols), lambda i: (i, 0))
    return pl.pallas_call(
        body, name=name, out_shape=[jax.ShapeDtypeStruct((rows, cols), F32)] * 4, grid=(rows // rb,),
        in_specs=[blk, blk, blk, pl.BlockSpec((n_parts, rb, cols), lambda i: (0, i, 0))], out_specs=[blk] * 4,
        compiler_params=_params(dimension_semantics=("arbitrary",)),
    )(w, m, v, parts)


BIG = ("w_in", "w_attn_proj", "w_conv_proj", "w_out", "w_cq", "w_ckv", "w_co", "w_up", "w_down")
SMALL = ("g_mix", "b_gate", "conv_b", "conv_ln_g", "conv_ln_b", "g_cross", "g_mem", "g_mlp", "g_final")
PACK_ORDER = SMALL + ("loss", "conv_w")
PACK_ROWS = 272
WEIGHTS = ("g_mix", "w_in", "b_gate", "conv_w", "conv_b", "conv_ln_g", "conv_ln_b", "w_attn_proj", "w_conv_proj", "w_out",
           "g_cross", "g_mem", "w_cq", "w_ckv", "w_co", "g_mlp", "w_up", "w_down", "g_final")


def _pack(d):
    flat = jnp.concatenate([d[k].reshape(-1) for k in PACK_ORDER])
    return jnp.pad(flat, (0, PACK_ROWS * HEAD_DIM - flat.shape[0])).reshape(PACK_ROWS, HEAD_DIM)


def _unpack(p, sizes):
    flat, out, off = p.reshape(-1), {}, 0
    for k in PACK_ORDER:
        out[k] = flat[off:off + sizes[k]]
        off += sizes[k]
    return out


def kernel(x, mem, g_mix, w_in, b_gate, conv_w, conv_b, conv_ln_g, conv_ln_b, w_attn_proj, w_conv_proj, w_out, g_cross, g_mem, w_cq, w_ckv, w_co, g_mlp, w_up, w_down, g_final, loss_target, m_g_mix, m_w_in, m_b_gate, m_conv_w, m_conv_b, m_conv_ln_g, m_conv_ln_b, m_w_attn_proj, m_w_conv_proj, m_w_out, m_g_cross, m_g_mem, m_w_cq, m_w_ckv, m_w_co, m_g_mlp, m_w_up, m_w_down, m_g_final, v_g_mix, v_w_in, v_b_gate, v_conv_w, v_conv_b, v_conv_ln_g, v_conv_ln_b, v_w_attn_proj, v_w_conv_proj, v_w_out, v_g_cross, v_g_mem, v_w_cq, v_w_ckv, v_w_co, v_g_mlp, v_w_up, v_w_down, v_g_final):
    args = dict(locals())
    wts = {k: args[k] for k in WEIGHTS}
    mom = {k: args["m_" + k] for k in WEIGHTS}
    var = {k: args["v_" + k] for k in WEIGHTS}
    two_d = lambda a: a.reshape(a.shape[-2:]) if a.ndim == 3 else a.reshape(1, -1)

    shards = {k: two_d(wts[k]).astype(BF16) for k in BIG}
    shards["taps"] = jnp.pad(two_d(conv_w), ((0, 1), (0, HEAD_DIM - conv_w.shape[-1])))
    plan = _Plan(shards, _exchange("gather_w_in", _Comm(replicated=[shards["w_in"]]))[0], conv_w.shape[-1])
    sm = {k: two_d(wts[k]) for k in SMALL}

    grad_x, _, small = _local_step(x[0], mem[0], loss_target[0], sm, plan)
    parts = plan.parts
    small_parts = _exchange("exchange_small", _Comm(replicated=[_pack(small)]))[0]

    out = {}
    for k in BIG:
        res = _adamw("adamw_" + k, two_d(wts[k]), two_d(mom[k]), two_d(var[k]), parts[k])
        out[k] = [r.reshape(wts[k].shape) for r in res]
    sizes = {k: small[k].size for k in PACK_ORDER}
    zeros = dict(loss=jnp.zeros_like(small["loss"]), conv_w=jnp.zeros_like(small["conv_w"]))
    packed = _adamw("adamw_small", _pack({**{k: wts[k] for k in SMALL}, **zeros}), _pack({**{k: mom[k] for k in SMALL}, **zeros}),
                    _pack({**{k: var[k] for k in SMALL}, **zeros}), small_parts)
    unpacked = [_unpack(p, sizes) for p in packed]
    for k in SMALL:
        out[k] = [u[k].reshape(wts[k].shape) for u in unpacked]
    loss = unpacked[0]["loss"][0]
    me = 4 * lax.axis_index("x") + 2 * lax.axis_index("y") + lax.axis_index("c")
    n_tap_cols = conv_w.shape[-1]
    g_taps = lax.dynamic_slice(unpacked[0]["conv_w"].reshape(CONV_HALO, CONV_CH), (0, me * n_tap_cols), (CONV_K, n_tap_cols))
    res = _adamw("adamw_conv_w", two_d(conv_w), two_d(m_conv_w), two_d(v_conv_w), g_taps[None])
    out["conv_w"] = [r.reshape(conv_w.shape) for r in res]

    return (loss, grad_x[None], *[out[k][0] for k in WEIGHTS], *[out[k][1] for k in WEIGHTS],
            *[out[k][2] for k in WEIGHTS], *[out[k][3] for k in WEIGHTS])
```

```python
import functools

import jax
import jax.numpy as jnp
from jax import lax
from jax.experimental import pallas as pl
from jax.experimental.pallas import tpu as pltpu

F32 = jnp.float32
BF16 = jnp.bfloat16

N_DEV = 8
D_MODEL = 1024
N_MEM = 256
HEAD_DIM = 128
HEADS_PER_GROUP = 4
GROUP_W = HEADS_PER_GROUP * HEAD_DIM
DILATIONS = (1, 4, 16)
BAND = 128
N_GROUPS = 3
ATTN_W = N_GROUPS * GROUP_W
QKV_W = 3 * ATTN_W
ROT_DIM = HEAD_DIM // 4
ROPE_THETA = 500000.0
CONV_CH = 768
CONV_K = 31
CONV_HALO = 32
SUBLANES = 8
CONV_ROWS = 64
IN_W = 8192
GLU_COL_BLK = QKV_W // (2 * CONV_CH)
GATE_COL_BLK = (QKV_W + 2 * CONV_CH) // (2 * D_MODEL)
CROSS_HEADS = 4
CROSS_HD = D_MODEL // CROSS_HEADS
D_FF = 4096
EPS = 1e-6
NEG = -1e30
QB = 4
ROW_BLK = QB * BAND

ADAM_LR = 0.001
ADAM_B1 = 0.9
ADAM_B2 = 0.999
ADAM_EPS = 1e-08
ADAM_WD = 0.01
ADAM_STEP = 10

VMEM_LIMIT = 56 * 1024 * 1024
MESH = pl.DeviceIdType.MESH


def _params(**kw):
    return pltpu.CompilerParams(vmem_limit_bytes=VMEM_LIMIT, **kw)


def _sigmoid(x):
    return 1.0 / (1.0 + jnp.exp(-x))


def _dot(a, b, kind):
    dims = {"nn": (((1,), (0,)), ((), ())), "nt": (((1,), (1,)), ((), ())), "tn": (((0,), (0,)), ((), ()))}[kind]
    if a.dtype != BF16:
        a = a.astype(BF16)
    if b.dtype != BF16:
        b = b.astype(BF16)
    return lax.dot_general(a, b, dims, preferred_element_type=F32)


def _peers():
    x, y, c = lax.axis_index("x"), lax.axis_index("y"), lax.axis_index("c")
    me = 4 * x + 2 * y + c
    peers = [(x, y, 1 - c), (1 - x, y, c), (x, 1 - y, c), (1 - x, 1 - y, c),
             (1 - x, y, 1 - c), (x, 1 - y, 1 - c), (1 - x, 1 - y, 1 - c)]
    return me, peers


class _Comm:
    def __init__(self, chunked=(), replicated=()):
        self.arrays = list(chunked) + list(replicated)
        self.n_c = len(chunked)
        self.n = len(self.arrays)
        self.out_shape = [jax.ShapeDtypeStruct(a.shape, a.dtype) for a in chunked]
        self.out_shape += [jax.ShapeDtypeStruct((N_DEV,) + a.shape, a.dtype) for a in replicated]
        self.in_specs = [pl.BlockSpec(memory_space=pl.ANY)] * self.n
        self.out_specs = [pl.BlockSpec(memory_space=pl.ANY)] * self.n
        self.scratch = [pltpu.SemaphoreType.DMA((self.n,))] * 3 if self.n else []

    def _local(self, ins, outs, sems, a, me):
        src = ins[a].at[me] if a < self.n_c else ins[a]
        return pltpu.make_async_copy(src, outs[a].at[me], sems[2].at[a])

    def start(self, ins, outs, sems):
        me, peers = _peers()
        for a in range(self.n):
            self._local(ins, outs, sems, a, me).start()
            for (px, py, pc) in peers:
                src = ins[a].at[4 * px + 2 * py + pc] if a < self.n_c else ins[a]
                pltpu.make_async_remote_copy(src_ref=src, dst_ref=outs[a].at[me], send_sem=sems[0].at[a],
                                             recv_sem=sems[1].at[a], device_id=(px, py, pc), device_id_type=MESH).start()

    def wait(self, ins, outs, sems):
        me, peers = _peers()
        for a in range(self.n):
            seven = outs[a].at[pl.ds(0, N_DEV - 1)]
            cp = pltpu.make_async_remote_copy(src_ref=seven, dst_ref=seven, send_sem=sems[0].at[a], recv_sem=sems[1].at[a],
                                              device_id=peers[0], device_id_type=MESH)
            cp.wait_send()
            cp.wait_recv()
            self._local(ins, outs, sems, a, me).wait()


_NO_COMM = _Comm()


def _pcall(body, name, grid, operands, in_specs, out_shape, out_specs, scratch=(), aliases=None, comm=_NO_COMM):
    n_in, n_out, n_scr = len(operands), len(out_shape), len(scratch)
    grid = tuple(grid)

    def carried(*refs):
        ins, c_in = refs[:n_in], refs[n_in:n_in + comm.n]
        o0 = n_in + comm.n
        outs, c_out = refs[o0:o0 + n_out], refs[o0 + n_out:o0 + n_out + comm.n]
        s0 = o0 + n_out + comm.n
        scr, sems = refs[s0:s0 + n_scr], refs[s0 + n_scr:]
        ids = [pl.program_id(ax) for ax in range(len(grid))]

        @pl.when(functools.reduce(jnp.logical_and, [p == 0 for p in ids]))
        def _():
            comm.start(c_in, c_out, sems)

        body(*ins, *outs, *scr)

        @pl.when(functools.reduce(jnp.logical_and, [p == g - 1 for p, g in zip(ids, grid)]))
        def _():
            comm.wait(c_in, c_out, sems)

    return pl.pallas_call(
        carried if comm.n else body, name=name, grid=grid, in_specs=list(in_specs) + comm.in_specs,
        out_shape=list(out_shape) + comm.out_shape, out_specs=list(out_specs) + comm.out_specs,
        scratch_shapes=list(scratch) + comm.scratch, input_output_aliases=aliases or {},
        compiler_params=_params(dimension_semantics=("arbitrary",) * len(grid)),
    )(*operands, *comm.arrays)


def _mm(name, a, b, kind, grid, a_blk, b_blk, outs, extras=(), epi=None, acc_outs=(), j_outer=False, comm=_NO_COMM,
        split=None, b_resident=False):
    gi, gj, gk = grid
    n_ex = len(extras)
    n_out = len(outs)
    mode, n_chunks = split if split is not None else (None, 1)

    def spec(blk, fn, **kw):
        return pl.BlockSpec(blk, (lambda j, i, k: fn(i, j, k)) if j_outer else fn, **kw)

    def b_chunk(b_ref, c):
        if len(b_ref.shape) == 3:
            return b_ref[c]
        rows, cols = b_ref.shape
        if (kind == "nn") == (mode == "cols"):
            return b_ref[:, c * (cols // n_chunks):(c + 1) * (cols // n_chunks)]
        return b_ref[c * (rows // n_chunks):(c + 1) * (rows // n_chunks), :]

    def col_chunk(ref, c):
        width = ref.shape[-1] // n_chunks
        return slice(c * width, (c + 1) * width)

    def body(*refs):
        a_ref, b_ref = refs[0], refs[1]
        ex = refs[2:2 + n_ex]
        out_refs = refs[2 + n_ex:2 + n_ex + n_out]
        acc_ref = refs[2 + n_ex + n_out] if gk > 1 else None
        i = pl.program_id(1 if j_outer else 0)
        k = pl.program_id(2)
        if mode == "cols":
            a_val = a_ref[...]
            for c in range(n_chunks):
                acc = _dot(a_val, b_chunk(b_ref, c), kind)
                vals = epi(acc, *[e[:, col_chunk(e, c)] for e in ex]) if epi is not None else (acc,)
                for o, v in zip(out_refs, vals):
                    o[:, col_chunk(o, c)] = v.astype(o.dtype)
            return
        if mode == "sum":
            part = _dot(a_ref[:, col_chunk(a_ref, 0)], b_chunk(b_ref, 0), kind)
            for c in range(1, n_chunks):
                part = part + _dot(a_ref[:, col_chunk(a_ref, c)], b_chunk(b_ref, c), kind)
        else:
            part = _dot(a_ref[...], b_ref[...], kind)

        def finish(acc):
            vals = epi(acc, *[e[...] for e in ex]) if epi is not None else (acc,)
            for idx, (o, v) in enumerate(zip(out_refs, vals)):
                if idx in acc_outs:
                    @pl.when(i == 0)
                    def _():
                        o[...] = v.astype(o.dtype)

                    @pl.when(i != 0)
                    def _():
                        o[...] += v.astype(o.dtype)
                else:
                    o[...] = v.astype(o.dtype)

        if gk == 1:
            finish(part)
        else:
            @pl.when(k == 0)
            def _():
                acc_ref[...] = part

            @pl.when(k != 0)
            def _():
                acc_ref[...] += part

            @pl.when(k == gk - 1)
            def _():
                finish(acc_ref[...])

    scratch = []
    if gk > 1:
        tm = a_blk[0][-1] if kind == "tn" else a_blk[0][-2]
        tn = b_blk[0][-2] if kind == "nt" else b_blk[0][-1]
        scratch = [pltpu.VMEM((tm, tn), F32)]
    b_kw = dict(pipeline_mode=pl.Buffered(1)) if b_resident else {}
    return _pcall(body, name, (gj, gi, gk) if j_outer else (gi, gj, gk), [a, b] + [e for e, _, _ in extras],
                  [spec(*a_blk), spec(*b_blk, **b_kw)] + [spec(blk, fn) for _, blk, fn in extras],
                  [s for s, _, _ in outs], [spec(blk, fn) for _, blk, fn in outs], scratch, comm=comm)


def _rms_fwd_vals(x, g):
    r = lax.rsqrt(jnp.mean(x * x, axis=-1, keepdims=True) + EPS)
    return x * r * g


def _rms_bwd_vals(x, g, du):
    r = lax.rsqrt(jnp.mean(x * x, axis=-1, keepdims=True) + EPS)
    xh = x * r
    dxh = du * g
    dx = r * (dxh - xh * jnp.mean(dxh * xh, axis=-1, keepdims=True))
    return dx, jnp.sum(du * xh, axis=0, keepdims=True)


def _rms_fwd(name, x, g, rows):
    n = x.shape[0]

    def body(x_ref, g_ref, o_ref):
        o_ref[...] = _rms_fwd_vals(x_ref[...], g_ref[...]).astype(BF16)

    return pl.pallas_call(
        body, name=name, out_shape=jax.ShapeDtypeStruct(x.shape, BF16), grid=(n // rows,),
        in_specs=[pl.BlockSpec((rows, D_MODEL), lambda i: (i, 0)), pl.BlockSpec((1, D_MODEL), lambda i: (0, 0))],
        out_specs=pl.BlockSpec((rows, D_MODEL), lambda i: (i, 0)),
        compiler_params=_params(dimension_semantics=("arbitrary",)),
    )(x, g)


def _rope_tables(seq):
    half = ROT_DIM // 2
    pos = jnp.arange(seq, dtype=F32)
    inv_freq = ROPE_THETA ** (-jnp.arange(0, ROT_DIM, 2, dtype=F32) / ROT_DIM)
    ang = pos[:, None] * inv_freq[None, :]
    cos, sin = jnp.cos(ang), jnp.sin(ang)
    rest = HEAD_DIM - ROT_DIM
    c = jnp.concatenate([cos, cos, jnp.ones((seq, rest), F32)], axis=1)
    s1 = jnp.concatenate([jnp.zeros((seq, half), F32), sin, jnp.zeros((seq, rest), F32)], axis=1)
    s2 = jnp.concatenate([-sin, jnp.zeros((seq, half + rest), F32)], axis=1)
    return c, s1, s2


def _group_shapes(seq, width, dtype):
    return [jax.ShapeDtypeStruct((d, seq // d, width), dtype) for d in DILATIONS]


def _group_specs(width):
    return [pl.BlockSpec((d, ROW_BLK // d, width), lambda i: (0, i, 0)) for d in DILATIONS]


def _qkv_prep(z, tabs):
    seq = z.shape[0]

    def body(z_ref, c_ref, s1_ref, s2_ref, a0, a1, a2, sc):
        outs = (a0, a1, a2)
        c, s1, s2 = c_ref[...], s1_ref[...], s2_ref[...]
        for part in range(3):
            for hh in range(N_GROUPS * HEADS_PER_GROUP):
                g, hl = divmod(hh, HEADS_PER_GROUP)
                col = part * ATTN_W + hh * HEAD_DIM
                ocol = part * GROUP_W + hl * HEAD_DIM
                x = z_ref[:, col:col + HEAD_DIM].astype(F32)
                if part < 2:
                    x = x * c + pltpu.roll(x, ROT_DIM // 2, 1) * s1 + pltpu.roll(x, HEAD_DIM - ROT_DIM // 2, 1) * s2
                d = DILATIONS[g]
                if d == 1:
                    outs[g][0, :, ocol:ocol + HEAD_DIM] = x.astype(BF16)
                else:
                    sc[...] = x
                    for r in range(d):
                        outs[g][r, :, ocol:ocol + HEAD_DIM] = sc[pl.ds(r, ROW_BLK // d, stride=d), :].astype(BF16)

    tab_spec = pl.BlockSpec((ROW_BLK, HEAD_DIM), lambda i: (i, 0))
    return pl.pallas_call(
        body, name="qkv_prep", out_shape=_group_shapes(seq, ATTN_W, BF16), grid=(seq // ROW_BLK,),
        in_specs=[pl.BlockSpec((ROW_BLK, QKV_W), lambda i: (i, 0)), tab_spec, tab_spec, tab_spec],
        out_specs=_group_specs(ATTN_W), scratch_shapes=[pltpu.VMEM((ROW_BLK, HEAD_DIM), F32)],
        compiler_params=_params(dimension_semantics=("arbitrary",)),
    )(z, *tabs)


def _band_masks():
    qi = lax.broadcasted_iota(jnp.int32, (BAND, BAND), 0)
    kj = lax.broadcasted_iota(jnp.int32, (BAND, BAND), 1)
    return kj >= qi, kj <= qi


def _attn_fwd(name, a_g, comm=_NO_COMM):
    dil, m_len, _ = a_g.shape
    qb = min(QB, m_len // BAND)
    rows = qb * BAND
    steps = m_len // rows
    scale = HEAD_DIM ** -0.5

    def body(q_ref, kc_ref, vc_ref, kp_ref, vp_ref, o_ref, l_ref):
        t = pl.program_id(1)
        m_prev, m_cur = _band_masks()
        for sb in range(qb):
            rs = slice(sb * BAND, (sb + 1) * BAND)
            for h in range(HEADS_PER_GROUP):
                cs = slice(h * HEAD_DIM, (h + 1) * HEAD_DIM)
                q = q_ref[rs, cs]
                if sb == 0:
                    kp, vp = kp_ref[:, cs], vp_ref[:, cs]
                    pm = jnp.logical_and(m_prev, t > 0)
                else:
                    ps = slice((sb - 1) * BAND, sb * BAND)
                    kp, vp = kc_ref[ps, cs], vc_ref[ps, cs]
                    pm = m_prev
                kc, vc = kc_ref[rs, cs], vc_ref[rs, cs]
                sp = jnp.where(pm, _dot(q, kp, "nt") * scale, NEG)
                sc = jnp.where(m_cur, _dot(q, kc, "nt") * scale, NEG)
                mx = jnp.maximum(jnp.max(sp, axis=-1, keepdims=True), jnp.max(sc, axis=-1, keepdims=True))
                pp = jnp.exp(sp - mx)
                pc = jnp.exp(sc - mx)
                den = jnp.sum(pp, axis=-1, keepdims=True) + jnp.sum(pc, axis=-1, keepdims=True)
                o = (_dot(pp, vp, "nn") + _dot(pc, vc, "nn")) / den
                o_ref[rs, cs] = o
                l_ref[rs, cs] = jnp.broadcast_to(mx + jnp.log(den), (BAND, HEAD_DIM))

    def prev(r, t):
        return jnp.maximum(qb * t - 1, 0)

    cur = lambda c: pl.BlockSpec((None, rows, GROUP_W), lambda r, t, c=c: (r, t, c))
    prv = lambda c: pl.BlockSpec((None, BAND, GROUP_W), lambda r, t, c=c: (r, prev(r, t), c))
    out_spec = pl.BlockSpec((None, rows, GROUP_W), lambda r, t: (r, t, 0))
    shp = jax.ShapeDtypeStruct((dil, m_len, GROUP_W), F32)
    return _pcall(body, name, (dil, steps), [a_g] * 5, [cur(0), cur(1), cur(2), prv(1), prv(2)], [shp, shp],
                  [out_spec, out_spec], comm=comm)


def _attn_merge(os_, ls_, seq):
    def body(o0, l0, o1, l1, o2, l2, at_ref, lt_ref, sc):
        for h in range(HEADS_PER_GROUP):
            cs = slice(h * HEAD_DIM, (h + 1) * HEAD_DIM)
            for gi, (o_r, l_r) in enumerate(((o1, l1), (o2, l2))):
                d = DILATIONS[gi + 1]
                for r in range(d):
                    sc.at[2 * gi][pl.ds(r, ROW_BLK // d, stride=d), :] = o_r[r, :, cs]
                    sc.at[2 * gi + 1][pl.ds(r, ROW_BLK // d, stride=d), :] = l_r[r, :, cs]
            o0v, l0v = o0[0, :, cs], l0[0, :, cs]
            o1v, l1v, o2v, l2v = sc[0], sc[1], sc[2], sc[3]
            mx = jnp.maximum(jnp.maximum(l0v, l1v), l2v)
            e0, e1, e2 = jnp.exp(l0v - mx), jnp.exp(l1v - mx), jnp.exp(l2v - mx)
            tot = e0 + e1 + e2
            at_ref[:, cs] = ((e0 * o0v + e1 * o1v + e2 * o2v) / tot).astype(BF16)
            lt_ref[:, cs] = mx + jnp.log(tot)

    gs = _group_specs(GROUP_W)
    row = pl.BlockSpec((ROW_BLK, GROUP_W), lambda i: (i, 0))
    return pl.pallas_call(
        body, name="attn_merge",
        out_shape=[jax.ShapeDtypeStruct((seq, GROUP_W), BF16), jax.ShapeDtypeStruct((seq, GROUP_W), F32)],
        grid=(seq // ROW_BLK,), in_specs=[gs[0], gs[0], gs[1], gs[1], gs[2], gs[2]], out_specs=[row, row],
        scratch_shapes=[pltpu.VMEM((4, ROW_BLK, HEAD_DIM), F32)],
        compiler_params=_params(dimension_semantics=("arbitrary",)),
    )(os_[0], ls_[0], os_[1], ls_[1], os_[2], ls_[2])


def _attn_bwd_prep(dattn, attn, lt):
    seq = dattn.shape[0]

    def body(da_ref, at_ref, lt_ref, d0, c0, t0, d1, c1, t1, d2, c2, t2, sc):
        outs = ((d0, c0, t0), (d1, c1, t1), (d2, c2, t2))
        for h in range(HEADS_PER_GROUP):
            cs = slice(h * HEAD_DIM, (h + 1) * HEAD_DIM)
            da = da_ref[:, cs].astype(F32)
            cc = jnp.broadcast_to(jnp.sum(da * at_ref[:, cs].astype(F32), axis=-1, keepdims=True), (ROW_BLK, HEAD_DIM))
            ltv = lt_ref[:, cs]
            d0[0, :, cs] = da_ref[:, cs]
            c0[0, :, cs] = cc
            t0[0, :, cs] = ltv
            sc[0], sc[1], sc[2] = da, cc, ltv
            for g in (1, 2):
                d = DILATIONS[g]
                for r in range(d):
                    rows = pl.ds(r, ROW_BLK // d, stride=d)
                    outs[g][0][r, :, cs] = sc.at[0][rows, :].astype(BF16)
                    outs[g][1][r, :, cs] = sc.at[1][rows, :]
                    outs[g][2][r, :, cs] = sc.at[2][rows, :]

    gs = _group_specs(GROUP_W)
    row = pl.BlockSpec((ROW_BLK, GROUP_W), lambda i: (i, 0))
    shapes, specs = [], []
    for g, d in enumerate(DILATIONS):
        for dt in (BF16, F32, F32):
            shapes.append(jax.ShapeDtypeStruct((d, seq // d, GROUP_W), dt))
            specs.append(gs[g])
    res = pl.pallas_call(
        body, name="attn_bwd_prep", out_shape=shapes, grid=(seq // ROW_BLK,), in_specs=[row, row, row],
        out_specs=specs, scratch_shapes=[pltpu.VMEM((3, ROW_BLK, HEAD_DIM), F32)],
        compiler_params=_params(dimension_semantics=("arbitrary",)),
    )(dattn, attn, lt)
    return [res[3 * g:3 * g + 3] for g in range(N_GROUPS)]


def _attn_bwd(name, a_g, da_g, c_g, lt_g):
    dil, m_len, _ = a_g.shape
    qb = min(QB, m_len // BAND)
    rows = qb * BAND
    steps = m_len // rows
    scale = HEAD_DIM ** -0.5

    def body(q_ref, kc_ref, vc_ref, kp_ref, vp_ref, da_ref, c_ref, lt_ref, d_ref, dk_acc, dv_acc, car_k, car_v):
        tg = pl.program_id(1)
        t = steps - 1 - tg
        m_prev, m_cur = _band_masks()

        @pl.when(tg == 0)
        def _():
            car_k[...] = jnp.zeros_like(car_k)
            car_v[...] = jnp.zeros_like(car_v)

        zero = jnp.zeros((rows, GROUP_W), F32)
        dk_acc[0:rows, :] = zero
        dv_acc[0:rows, :] = zero
        dk_acc[rows:rows + BAND, :] = car_k[...]
        dv_acc[rows:rows + BAND, :] = car_v[...]
        for sb in range(qb):
            rs = slice(sb * BAND, (sb + 1) * BAND)
            ps = slice((sb - 1) * BAND, sb * BAND)
            acc_p = slice(sb * BAND, (sb + 1) * BAND)
            acc_c = slice((sb + 1) * BAND, (sb + 2) * BAND)
            for h in range(HEADS_PER_GROUP):
                cs = slice(h * HEAD_DIM, (h + 1) * HEAD_DIM)
                q, da = q_ref[rs, cs], da_ref[rs, cs]
                cc, ltv = c_ref[rs, cs], lt_ref[rs, cs]
                if sb == 0:
                    kp, vp = kp_ref[:, cs], vp_ref[:, cs]
                    pm = jnp.logical_and(m_prev, t > 0)
                else:
                    kp, vp = kc_ref[ps, cs], vc_ref[ps, cs]
                    pm = m_prev
                kc, vc = kc_ref[rs, cs], vc_ref[rs, cs]
                pp = jnp.exp(jnp.where(pm, _dot(q, kp, "nt") * scale - ltv, NEG))
                pc = jnp.exp(jnp.where(m_cur, _dot(q, kc, "nt") * scale - ltv, NEG))
                dsp = (pp * (_dot(da, vp, "nt") - cc) * scale).astype(BF16)
                dsc = (pc * (_dot(da, vc, "nt") - cc) * scale).astype(BF16)
                ppb, pcb = pp.astype(BF16), pc.astype(BF16)
                d_ref[rs, cs] = (_dot(dsp, kp, "nn") + _dot(dsc, kc, "nn")).astype(BF16)
                dk_acc[acc_p, cs] += _dot(dsp, q, "tn")
                dk_acc[acc_c, cs] += _dot(dsc, q, "tn")
                dv_acc[acc_p, cs] += _dot(ppb, da, "tn")
                dv_acc[acc_c, cs] += _dot(pcb, da, "tn")
        d_ref[:, GROUP_W:2 * GROUP_W] = dk_acc[BAND:rows + BAND, :].astype(BF16)
        d_ref[:, 2 * GROUP_W:3 * GROUP_W] = dv_acc[BAND:rows + BAND, :].astype(BF16)
        car_k[...] = dk_acc[0:BAND, :]
        car_v[...] = dv_acc[0:BAND, :]

    def rev(tg):
        return steps - 1 - tg

    def prev(tg):
        return jnp.maximum(qb * rev(tg) - 1, 0)

    cur = lambda c: pl.BlockSpec((None, rows, GROUP_W), lambda r, tg, c=c: (r, rev(tg), c))
    prv = lambda c: pl.BlockSpec((None, BAND, GROUP_W), lambda r, tg, c=c: (r, prev(tg), c))
    return pl.pallas_call(
        body, name=name, out_shape=jax.ShapeDtypeStruct((dil, m_len, ATTN_W), BF16), grid=(dil, steps),
        in_specs=[cur(0), cur(1), cur(2), prv(1), prv(2), cur(0), cur(0), cur(0)],
        out_specs=pl.BlockSpec((None, rows, ATTN_W), lambda r, tg: (r, rev(tg), 0)),
        scratch_shapes=[pltpu.VMEM((rows + BAND, GROUP_W), F32), pltpu.VMEM((rows + BAND, GROUP_W), F32),
                        pltpu.VMEM((BAND, GROUP_W), F32), pltpu.VMEM((BAND, GROUP_W), F32)],
        compiler_params=_params(dimension_semantics=("arbitrary", "arbitrary")),
    )(a_g, a_g, a_g, a_g, a_g, da_g, c_g, lt_g)


def _dqkv_post(d_gs, tabs, dz):
    seq = dz.shape[0]

    def body(g0, g1, g2, c_ref, s1_ref, s2_ref, dz_any, o_ref, sc):
        del dz_any
        ins = (g0, g1, g2)
        c, s1, s2 = c_ref[...], s1_ref[...], s2_ref[...]
        for part in range(3):
            for hh in range(N_GROUPS * HEADS_PER_GROUP):
                g, hl = divmod(hh, HEADS_PER_GROUP)
                icol = part * GROUP_W + hl * HEAD_DIM
                ocol = part * ATTN_W + hh * HEAD_DIM
                d = DILATIONS[g]
                if d == 1:
                    x = ins[g][0, :, icol:icol + HEAD_DIM].astype(F32)
                else:
                    for r in range(d):
                        sc[pl.ds(r, ROW_BLK // d, stride=d), :] = ins[g][r, :, icol:icol + HEAD_DIM].astype(F32)
                    x = sc[...]
                if part < 2:
                    x = x * c + pltpu.roll(x * s1, HEAD_DIM - ROT_DIM // 2, 1) + pltpu.roll(x * s2, ROT_DIM // 2, 1)
                o_ref[:, ocol:ocol + HEAD_DIM] = x.astype(BF16)

    tab_spec = pl.BlockSpec((ROW_BLK, HEAD_DIM), lambda i: (i, 0))
    return pl.pallas_call(
        body, name="dqkv_post", out_shape=jax.ShapeDtypeStruct(dz.shape, BF16), grid=(seq // ROW_BLK,),
        in_specs=_group_specs(ATTN_W) + [tab_spec, tab_spec, tab_spec, pl.BlockSpec(memory_space=pl.ANY)],
        out_specs=pl.BlockSpec((ROW_BLK, QKV_W), lambda i: (i, 0)),
        scratch_shapes=[pltpu.VMEM((ROW_BLK, HEAD_DIM), F32)], input_output_aliases={6: 0},
        compiler_params=_params(dimension_semantics=("arbitrary",)),
    )(*d_gs, *tabs, dz)


def _glu(zg):
    a = zg[:, :CONV_CH].astype(F32)
    s = _sigmoid(zg[:, CONV_CH:].astype(F32))
    return a, s, a * s


def _shifted_copies(xs):
    n = xs.shape[1] - SUBLANES
    for b in range(1, SUBLANES):
        xs[b, 0:n, :] = xs[0, pl.ds(b, n), :]


def _shifted(xs, offset, r0, cs):
    a, b = divmod(offset, SUBLANES)
    return xs[b, pl.ds(SUBLANES * a + r0, CONV_ROWS), cs]


def _conv_fwd(z, cw, cb, lg, lb):
    seq = z.shape[0]
    halo_per_blk = ROW_BLK // CONV_HALO

    def body(zg_ref, zh_ref, cw_ref, cb_ref, lg_ref, lb_ref, c2_ref, c4_ref, xs):
        i = pl.program_id(0)
        _, _, c1 = _glu(zg_ref[...])
        _, _, c1h = _glu(zh_ref[...])
        xs[0, 0:CONV_HALO, :] = jnp.where(i > 0, c1h, 0.0)
        xs[0, CONV_HALO:, :] = c1
        _shifted_copies(xs)
        for s in range(CONV_CH // HEAD_DIM):
            cs = slice(s * HEAD_DIM, (s + 1) * HEAD_DIM)
            taps = [cw_ref[j:j + 1, cs] for j in range(CONV_K)]
            bias = cb_ref[:, cs]

            def chunk(rc, carry, cs=cs, taps=taps, bias=bias):
                r0 = pl.multiple_of(rc * CONV_ROWS, CONV_ROWS)
                acc = jnp.zeros((CONV_ROWS, HEAD_DIM), F32)
                for j in range(CONV_K):
                    acc = acc + taps[j] * _shifted(xs, CONV_HALO - (CONV_K - 1) + j, r0, cs)
                c2_ref[pl.ds(r0, CONV_ROWS), cs] = acc + bias
                return carry

            lax.fori_loop(0, ROW_BLK // CONV_ROWS, chunk, 0)
        c2 = c2_ref[...]
        mu = jnp.mean(c2, axis=-1, keepdims=True)
        xc = c2 - mu
        rstd = lax.rsqrt(jnp.mean(xc * xc, axis=-1, keepdims=True) + EPS)
        c3 = xc * rstd * lg_ref[...] + lb_ref[...]
        c4_ref[...] = (c3 * _sigmoid(c3)).astype(BF16)

    vec = pl.BlockSpec((1, CONV_CH), lambda i: (0, 0))
    return pl.pallas_call(
        body, name="conv_fwd",
        out_shape=[jax.ShapeDtypeStruct((seq, CONV_CH), F32), jax.ShapeDtypeStruct((seq, CONV_CH), BF16)],
        grid=(seq // ROW_BLK,),
        in_specs=[pl.BlockSpec((ROW_BLK, 2 * CONV_CH), lambda i: (i, GLU_COL_BLK)),
                  pl.BlockSpec((CONV_HALO, 2 * CONV_CH), lambda i: (jnp.maximum(i * halo_per_blk - 1, 0), GLU_COL_BLK)),
                  pl.BlockSpec((CONV_HALO, CONV_CH), lambda i: (0, 0)), vec, vec, vec],
        out_specs=[pl.BlockSpec((ROW_BLK, CONV_CH), lambda i: (i, 0)), pl.BlockSpec((ROW_BLK, CONV_CH), lambda i: (i, 0))],
        scratch_shapes=[pltpu.VMEM((SUBLANES, ROW_BLK + CONV_HALO, CONV_CH), F32)],
        compiler_params=_params(dimension_semantics=("arbitrary",)),
    )(z, z, cw, cb, lg, lb)


def _conv_bwd(dc2, z, cw, dz, comm=_NO_COMM):
    seq = z.shape[0]
    halo_per_blk = ROW_BLK // CONV_HALO
    n_blk = seq // ROW_BLK
    last_halo = seq // CONV_HALO - 1

    def body(dc_ref, dn_ref, zg_ref, zh_ref, cw_ref, dz_any, o_ref, dcw_ref, xs, ys, dc1_ref, dcw_acc):
        del dz_any
        i = pl.program_id(0)
        a, s, c1 = _glu(zg_ref[...])
        _, _, c1h = _glu(zh_ref[...])
        xs[0, 0:CONV_HALO, :] = jnp.where(i > 0, c1h, 0.0)
        xs[0, CONV_HALO:, :] = c1
        ys[0, 0:ROW_BLK, :] = dc_ref[...]
        ys[0, ROW_BLK:, :] = jnp.where(i < n_blk - 1, dn_ref[...], 0.0)
        _shifted_copies(xs)
        _shifted_copies(ys)

        @pl.when(i == 0)
        def _():
            dcw_acc[...] = jnp.zeros_like(dcw_acc)

        for sl in range(CONV_CH // HEAD_DIM):
            cs = slice(sl * HEAD_DIM, (sl + 1) * HEAD_DIM)
            taps = [cw_ref[j:j + 1, cs] for j in range(CONV_K)]

            def chunk(rc, carry, cs=cs, taps=taps):
                r0 = pl.multiple_of(rc * CONV_ROWS, CONV_ROWS)
                dc = ys[0, pl.ds(r0, CONV_ROWS), cs]
                acc = jnp.zeros((CONV_ROWS, HEAD_DIM), F32)
                for j in range(CONV_K):
                    prod = dc * _shifted(xs, CONV_HALO - (CONV_K - 1) + j, r0, cs)
                    dcw_acc[j, :, cs] += jnp.sum(prod.reshape(CONV_ROWS // SUBLANES, SUBLANES, HEAD_DIM), axis=0)
                    acc = acc + taps[j] * _shifted(ys, CONV_K - 1 - j, r0, cs)
                dc1_ref[pl.ds(r0, CONV_ROWS), cs] = acc
                return carry

            lax.fori_loop(0, ROW_BLK // CONV_ROWS, chunk, 0)
        dc1 = dc1_ref[...]
        o_ref[:, :CONV_CH] = (dc1 * s).astype(BF16)
        o_ref[:, CONV_CH:] = (dc1 * a * s * (1.0 - s)).astype(BF16)

        @pl.when(i == n_blk - 1)
        def _():
            dcw_ref[...] = jnp.sum(dcw_acc[...], axis=1)

    return _pcall(
        body, "conv_bwd", (n_blk,), [dc2, dc2, z, z, cw, dz],
        [pl.BlockSpec((ROW_BLK, CONV_CH), lambda i: (i, 0)),
         pl.BlockSpec((CONV_HALO, CONV_CH), lambda i: (jnp.minimum((i + 1) * halo_per_blk, last_halo), 0)),
         pl.BlockSpec((ROW_BLK, 2 * CONV_CH), lambda i: (i, GLU_COL_BLK)),
         pl.BlockSpec((CONV_HALO, 2 * CONV_CH), lambda i: (jnp.maximum(i * halo_per_blk - 1, 0), GLU_COL_BLK)),
         pl.BlockSpec((CONV_HALO, CONV_CH), lambda i: (0, 0)),
         pl.BlockSpec(memory_space=pl.ANY)],
        [jax.ShapeDtypeStruct(dz.shape, BF16), jax.ShapeDtypeStruct((CONV_HALO, CONV_CH), F32)],
        [pl.BlockSpec((ROW_BLK, 2 * CONV_CH), lambda i: (i, GLU_COL_BLK)), pl.BlockSpec((CONV_HALO, CONV_CH), lambda i: (0, 0))],
        [pltpu.VMEM((SUBLANES, ROW_BLK + CONV_HALO, CONV_CH), F32), pltpu.VMEM((SUBLANES, ROW_BLK + CONV_HALO, CONV_CH), F32),
         pltpu.VMEM((ROW_BLK, CONV_CH), F32), pltpu.VMEM((CONV_HALO, SUBLANES, CONV_CH), F32)],
        aliases={5: 0}, comm=comm)


def _epi_mix(ya, c4, wcp, gates, bg):
    yc = _dot(c4, wcp, "nn")
    gv = _sigmoid(gates.astype(F32) + bg)
    merged = gv[:, :D_MODEL] * ya + gv[:, D_MODEL:] * yc
    return merged, ya, yc


def _epi_residual_rms(acc, xres, g):
    x = xres + acc
    return x, _rms_fwd_vals(x, g)


def _cross_scores(cq, ck):
    out = []
    for h in range(CROSS_HEADS):
        cs = slice(h * CROSS_HD, (h + 1) * CROSS_HD)
        s = _dot(cq[:, cs], ck[:, cs], "nt") * (CROSS_HD ** -0.5)
        e = jnp.exp(s - jnp.max(s, axis=-1, keepdims=True))
        out.append((cs, e, jnp.sum(e, axis=-1, keepdims=True)))
    return out


def _epi_cross_fwd(acc, ck, cv):
    cq = acc.astype(BF16)
    co = [_dot(e, cv[:, cs], "nn") / den for cs, e, den in _cross_scores(cq, ck)]
    return cq, jnp.concatenate(co, axis=1)


def _epi_cross_bwd(dco, cq, ck, cv):
    dco = dco.astype(BF16)
    dcq, dck, dcv = [], [], []
    for cs, e, den in _cross_scores(cq, ck):
        p = e / den
        dp = _dot(dco[:, cs], cv[:, cs], "nt")
        ds = (p * (dp - jnp.sum(dp * p, axis=-1, keepdims=True)) * (CROSS_HD ** -0.5)).astype(BF16)
        dcq.append(_dot(ds, ck[:, cs], "nn"))
        dck.append(_dot(ds, cq[:, cs], "tn"))
        dcv.append(_dot(p, dco[:, cs], "tn"))
    return jnp.concatenate(dcq, axis=1), jnp.concatenate(dck, axis=1), jnp.concatenate(dcv, axis=1)


def _epi_mlp_up(acc):
    return acc, jnp.square(jnp.maximum(acc, 0.0))


def _epi_final(acc, x2, tgt, g):
    x3 = x2 + acc
    err = _rms_fwd_vals(x3, g) - tgt
    loss = (0.5 / D_MODEL) * jnp.sum(err * err)
    dx3, dg = _rms_bwd_vals(x3, g, err * (1.0 / D_MODEL))
    return dx3, dx3, jnp.full((1, HEAD_DIM), loss, F32), dg


def _epi_mlp_down_bwd(dh, hpre):
    return (dh * 2.0 * jnp.maximum(hpre.astype(F32), 0.0),)


def _epi_rms_bwd(du, x, g, dres):
    dx, dg = _rms_bwd_vals(x, g, du)
    return dres + dx, dg


def _epi_rms_bwd_2(du, x, g, dres):
    dx, dg = _epi_rms_bwd(du, x, g, dres)
    return dx, dx, dg


def _epi_rms_bwd_g(du, x, g):
    return (_rms_bwd_vals(x, g, du)[1],)


def _epi_mix_bwd(dm, ya, yc, gates, bg):
    gv = _sigmoid(gates.astype(F32) + bg)
    ga, gb = gv[:, :D_MODEL], gv[:, D_MODEL:]
    ya, yc = ya.astype(F32), yc.astype(F32)
    dgate = jnp.concatenate([dm * ya * ga * (1.0 - ga), dm * yc * gb * (1.0 - gb)], axis=1)
    return dm * ga, dm * gb, dgate, jnp.sum(dgate, axis=0, keepdims=True)


def _epi_ln_bwd(dc4, c2, lg, lb):
    mu = jnp.mean(c2, axis=-1, keepdims=True)
    xc = c2 - mu
    rstd = lax.rsqrt(jnp.mean(xc * xc, axis=-1, keepdims=True) + EPS)
    xh = xc * rstd
    c3 = xh * lg + lb
    sg = _sigmoid(c3)
    dc3 = dc4 * sg * (1.0 + c3 * (1.0 - sg))
    dxh = dc3 * lg
    dc2 = rstd * (dxh - jnp.mean(dxh, axis=-1, keepdims=True) - xh * jnp.mean(dxh * xh, axis=-1, keepdims=True))
    return (dc2, jnp.sum(dc3 * xh, axis=0, keepdims=True), jnp.sum(dc3, axis=0, keepdims=True),
            jnp.sum(dc2, axis=0, keepdims=True))


def _sds(shape, dtype):
    return jax.ShapeDtypeStruct(shape, dtype)


class _Lazy:
    def __init__(self, fn):
        self.fn = fn

    def __getitem__(self, key):
        return self.fn(key)


def _local_step(x, mem, tgt, sm, plan):
    w = _Lazy(plan.w)
    dw = {}

    def mm(name, *args, **kw):
        c = plan.comm(name, dw)
        res = _mm(name, *args, comm=c, **kw)
        plan.done(name, res[len(res) - c.n:])
        return res[:len(res) - c.n]

    seq = x.shape[0]
    nr = seq // ROW_BLK
    big = min(1024, seq)
    nb = seq // big
    row = lambda n: ((ROW_BLK, n), lambda i, j, k: (i, 0))
    vec = lambda n: ((1, n), lambda i, j, k: (0, 0))
    full = lambda r, c: ((r, c), lambda i, j, k: (0, 0))
    gates_blk = ((ROW_BLK, 2 * D_MODEL), lambda i, j, k: (i, GATE_COL_BLK))
    tabs = _rope_tables(seq)

    u = _rms_fwd("rms_mix", x, sm["g_mix"], ROW_BLK)
    whole3 = lambda a: (a.shape, lambda i, j, k: (0, 0, 0))
    z = mm("in_proj", u, w["w_in"], "nn", (nr, 1, 1), row(D_MODEL), whole3(w["w_in"]),
           [(_sds((seq, IN_W), BF16), *row(IN_W))], split=("cols", N_DEV), b_resident=True)[0]
    a_gs = _qkv_prep(z, tabs)
    os_, ls_ = [], []
    for g in range(N_GROUPS):
        name = "attn_fwd_%d" % g
        c = plan.comm(name, dw)
        res = _attn_fwd(name, a_gs[g], c)
        plan.done(name, res[2:])
        os_.append(res[0])
        ls_.append(res[1])
    attn, lt = _attn_merge(os_, ls_, seq)
    c2, c4 = _conv_fwd(z, w["taps"], sm["conv_b"], sm["conv_ln_g"], sm["conv_ln_b"])
    merged, ya, yc = mm(
        "mix", attn, w["w_attn_proj"], "nn", (nr, 1, 1), row(GROUP_W), full(GROUP_W, D_MODEL),
        [(_sds((seq, D_MODEL), BF16), *row(D_MODEL))] * 3,
        extras=[(c4, *row(CONV_CH)), (w["w_conv_proj"], *full(CONV_CH, D_MODEL)), (z, *gates_blk), (sm["b_gate"], *vec(2 * D_MODEL))],
        epi=_epi_mix)
    x1, uq = mm("out_proj", merged, w["w_out"], "nn", (nr, 1, 1), row(D_MODEL), full(D_MODEL, D_MODEL),
                 [(_sds((seq, D_MODEL), F32), *row(D_MODEL)), (_sds((seq, D_MODEL), BF16), *row(D_MODEL))],
                 extras=[(x, *row(D_MODEL)), (sm["g_cross"], *vec(D_MODEL))], epi=_epi_residual_rms)

    mn = _rms_fwd("rms_mem", mem, sm["g_mem"], N_MEM)
    ckv = mm("ckv_proj", mn, w["w_ckv"], "nn", (1, N_DEV, 1), full(N_MEM, D_MODEL),
              ((None, D_MODEL, 2 * D_MODEL // N_DEV), lambda i, j, k: (j, 0, 0)),
              [(_sds((N_MEM, 2 * D_MODEL), BF16), (N_MEM, 2 * D_MODEL // N_DEV), lambda i, j, k: (0, j))])[0]
    ck, cv = ckv[:, :D_MODEL], ckv[:, D_MODEL:]
    kv_blk = full(N_MEM, D_MODEL)
    cq, co = mm("cq_proj_cross", uq, w["w_cq"], "nn", (nr, 1, 1), row(D_MODEL), full(D_MODEL, D_MODEL),
                 [(_sds((seq, D_MODEL), BF16), *row(D_MODEL))] * 2,
                 extras=[(ck, *kv_blk), (cv, *kv_blk)], epi=_epi_cross_fwd)
    x2, um = mm("co_proj", co, w["w_co"], "nn", (nr, 1, 1), row(D_MODEL), full(D_MODEL, D_MODEL),
                 [(_sds((seq, D_MODEL), F32), *row(D_MODEL)), (_sds((seq, D_MODEL), BF16), *row(D_MODEL))],
                 extras=[(x1, *row(D_MODEL)), (sm["g_mlp"], *vec(D_MODEL))], epi=_epi_residual_rms)

    ff_blk = D_FF // N_DEV
    row_f32 = (_sds((seq, D_MODEL), F32), *row(D_MODEL))
    row_bf16 = (_sds((seq, D_MODEL), BF16), *row(D_MODEL))
    col_sum = (_sds((1, D_MODEL), F32), *vec(D_MODEL))
    hpre, h = mm("mlp_up", um, w["w_up"], "nn", (nr, 1, 1), row(D_MODEL), whole3(w["w_up"]),
                 [(_sds((seq, D_FF), BF16), *row(D_FF))] * 2, epi=_epi_mlp_up, split=("cols", N_DEV), b_resident=True)
    kt = D_FF // D_MODEL
    dx3, dx3b, loss, dg_final = mm(
        "mlp_down_loss", h, w["w_down"], "nn", (nr, 1, 1), row(D_FF), full(D_FF, D_MODEL),
        [row_f32, row_bf16, (_sds((1, HEAD_DIM), F32), *vec(HEAD_DIM)), col_sum],
        extras=[(x2, *row(D_MODEL)), (tgt, *row(D_MODEL)), (sm["g_final"], *vec(D_MODEL))], epi=_epi_final, acc_outs=(2, 3),
        b_resident=True)

    dhpre = mm("mlp_down_bwd", dx3b, w["w_down"], "nt", (nr, 1, 1), row(D_MODEL), full(D_FF, D_MODEL),
               [(_sds((seq, D_FF), BF16), *row(D_FF))], extras=[(hpre, *row(D_FF))], epi=_epi_mlp_down_bwd,
               split=("cols", kt), b_resident=True)[0]
    big2 = min(2 * big, seq)
    nb2 = seq // big2
    dw["w_down"] = mm("dw_down", h, dx3b, "tn", (kt, 1, nb2), ((big2, D_MODEL), lambda i, j, k: (k, i)),
                      ((big2, D_MODEL), lambda i, j, k: (k, 0)),
                      [(_sds((D_FF, D_MODEL), BF16), (D_MODEL, D_MODEL), lambda i, j, k: (i, 0))])[0]
    dx2, dx2b, dg_mlp = mm("mlp_up_bwd", dhpre, w["w_up"], "nt", (nr, 1, 1), row(D_FF), whole3(w["w_up"]),
                           [row_f32, row_bf16, col_sum],
                           extras=[(x2, *row(D_MODEL)), (sm["g_mlp"], *vec(D_MODEL)), (dx3, *row(D_MODEL))],
                           epi=_epi_rms_bwd_2, acc_outs=(2,), split=("sum", N_DEV), b_resident=True)
    dw["w_up"] = mm("dw_up", um, dhpre, "tn", (1, N_DEV, nb2), ((big2, D_MODEL), lambda i, j, k: (k, 0)),
                    ((big2, ff_blk), lambda i, j, k: (k, j)),
                    [(_sds((N_DEV, D_MODEL, ff_blk), BF16), (None, D_MODEL, ff_blk), lambda i, j, k: (j, 0, 0))])[0]

    acc_kv = (_sds((N_MEM, D_MODEL), F32), *kv_blk)
    dcq, dck, dcv = mm("co_proj_bwd_cross", dx2b, w["w_co"], "nt", (nr, 1, 1), row(D_MODEL), full(D_MODEL, D_MODEL),
                       [row_bf16, acc_kv, acc_kv],
                       extras=[(cq, *row(D_MODEL)), (ck, *kv_blk), (cv, *kv_blk)], epi=_epi_cross_bwd, acc_outs=(1, 2))

    def dw_square(name, act, grad):
        return mm(name, act, grad, "tn", (1, 1, nb2), ((big2, D_MODEL), lambda i, j, k: (k, 0)),
                  ((big2, D_MODEL), lambda i, j, k: (k, 0)), [(_sds((D_MODEL, D_MODEL), BF16), *full(D_MODEL, D_MODEL))])[0]

    dw["w_co"] = dw_square("dw_co", co, dx2b)
    dx1, dx1b, dg_cross = mm("cq_proj_bwd", dcq, w["w_cq"], "nt", (nr, 1, 1), row(D_MODEL), full(D_MODEL, D_MODEL),
                             [row_f32, row_bf16, col_sum],
                             extras=[(x1, *row(D_MODEL)), (sm["g_cross"], *vec(D_MODEL)), (dx2, *row(D_MODEL))],
                             epi=_epi_rms_bwd_2, acc_outs=(2,))
    dw["w_cq"] = dw_square("dw_cq", uq, dcq)
    dckv = jnp.concatenate([dck, dcv], axis=1)
    kv_chunk = 2 * D_MODEL // N_DEV
    dw["w_ckv"] = mm("dw_ckv", mn, dckv, "tn", (1, N_DEV, 1), full(N_MEM, D_MODEL), ((N_MEM, kv_chunk), lambda i, j, k: (0, j)),
                      [(_sds((N_DEV, D_MODEL, kv_chunk), BF16), (None, D_MODEL, kv_chunk), lambda i, j, k: (j, 0, 0))])[0]
    dg_mem = mm("ckv_proj_bwd", dckv, w["w_ckv"], "nt", (1, 1, N_DEV), ((N_MEM, kv_chunk), lambda i, j, k: (0, k)),
                 ((None, D_MODEL, kv_chunk), lambda i, j, k: (k, 0, 0)), [(_sds((1, D_MODEL), F32), *vec(D_MODEL))],
                 extras=[(mem, *full(N_MEM, D_MODEL)), (sm["g_mem"], *vec(D_MODEL))], epi=_epi_rms_bwd_g, acc_outs=(0,))[0]

    dya, dyc, dz, db_gate = mm(
        "out_proj_bwd_mix", dx1b, w["w_out"], "nt", (nr, 1, 1), row(D_MODEL), full(D_MODEL, D_MODEL),
        [(_sds((seq, D_MODEL), BF16), *row(D_MODEL)), (_sds((seq, D_MODEL), BF16), *row(D_MODEL)),
         (_sds((seq, IN_W), BF16), *gates_blk), (_sds((1, 2 * D_MODEL), F32), *vec(2 * D_MODEL))],
        extras=[(ya, *row(D_MODEL)), (yc, *row(D_MODEL)), (z, *gates_blk), (sm["b_gate"], *vec(2 * D_MODEL))],
        epi=_epi_mix_bwd, acc_outs=(3,))
    dw["w_out"] = dw_square("dw_out", merged, dx1b)
    dattn = mm("attn_proj_bwd", dya, w["w_attn_proj"], "nt", (nr, 1, 1), row(D_MODEL), full(GROUP_W, D_MODEL),
                [(_sds((seq, GROUP_W), BF16), *row(GROUP_W))])[0]
    pc = D_MODEL // N_DEV
    dw["w_attn_proj"] = mm("dw_attn_proj", attn, dya, "tn", (1, N_DEV, nb), ((big, GROUP_W), lambda i, j, k: (k, 0)),
                            ((big, pc), lambda i, j, k: (k, j)),
                            [(_sds((N_DEV, GROUP_W, pc), BF16), (None, GROUP_W, pc), lambda i, j, k: (j, 0, 0))])[0]
    cvec = (_sds((1, CONV_CH), F32), *vec(CONV_CH))
    dc2, dg_ln_g, dg_ln_b, dg_conv_b = mm(
        "conv_proj_bwd_ln", dyc, w["w_conv_proj"], "nt", (nr, 1, 1), row(D_MODEL), full(CONV_CH, D_MODEL),
        [(_sds((seq, CONV_CH), F32), *row(CONV_CH)), cvec, cvec, cvec],
        extras=[(c2, *row(CONV_CH)), (sm["conv_ln_g"], *vec(CONV_CH)), (sm["conv_ln_b"], *vec(CONV_CH))],
        epi=_epi_ln_bwd, acc_outs=(1, 2, 3))
    dw["w_conv_proj"] = mm("dw_conv_proj", c4, dyc, "tn", (1, N_DEV, nb), ((big, CONV_CH), lambda i, j, k: (k, 0)),
                            ((big, pc), lambda i, j, k: (k, j)),
                            [(_sds((N_DEV, CONV_CH, pc), BF16), (None, CONV_CH, pc), lambda i, j, k: (j, 0, 0))])[0]
    c = plan.comm("conv_bwd", dw)
    res = _conv_bwd(dc2, z, w["taps"], dz, c)
    plan.done("conv_bwd", res[2:])
    dz, dg_conv_w = res[:2]
    preps = _attn_bwd_prep(dattn, attn, lt)
    d_gs = [_attn_bwd("attn_bwd_%d" % g, a_gs[g], *preps[g]) for g in range(N_GROUPS)]
    dz = _dqkv_post(d_gs, tabs, dz)
    dw["w_in"] = mm("dw_in", u, dz, "tn", (1, N_DEV, nb2), ((big2, D_MODEL), lambda i, j, k: (k, 0)),
                    ((big2, D_MODEL), lambda i, j, k: (k, j)),
                    [(_sds((N_DEV, D_MODEL, D_MODEL), BF16), (None, D_MODEL, D_MODEL), lambda i, j, k: (j, 0, 0))])[0]
    grad_x, dg_mix = mm("in_proj_bwd", dz, w["w_in"], "nt", (nr, 1, 1), row(IN_W), whole3(w["w_in"]), [row_f32, col_sum],
                        extras=[(x, *row(D_MODEL)), (sm["g_mix"], *vec(D_MODEL)), (dx1, *row(D_MODEL))],
                        epi=_epi_rms_bwd, acc_outs=(1,), split=("sum", N_DEV), b_resident=True)
    small = dict(g_mix=dg_mix, b_gate=db_gate, conv_b=dg_conv_b, conv_ln_g=dg_ln_g, conv_ln_b=dg_ln_b, g_cross=dg_cross,
                 g_mem=dg_mem, g_mlp=dg_mlp, g_final=dg_final, loss=loss, conv_w=dg_conv_w)
    return grad_x, dw, small


def _exchange(name, comm):
    return _pcall(lambda: None, name, (1,), [], [], [], [], comm=comm)


SHARD_SHAPE = dict(w_in=(1024, 1024), w_attn_proj=(512, 128), w_conv_proj=(768, 128), w_out=(128, 1024), w_cq=(128, 1024),
                   w_ckv=(1024, 256), w_co=(128, 1024), w_up=(1024, 512), w_down=(512, 1024))
FWD_CARRY = {"in_proj": ("w_attn_proj", "w_conv_proj", "w_out", "w_cq", "w_ckv", "w_co", "taps"),
             "attn_fwd_0": ("w_up",), "attn_fwd_1": ("w_down",)}
BWD_CARRY = {"mlp_up_bwd": ("w_down",),
             "conv_bwd": ("w_up", "w_co", "w_cq", "w_ckv", "w_out", "w_attn_proj", "w_conv_proj"),
             "in_proj_bwd": ("w_in",)}


def _cols_to_2d(a):
    return a.transpose(1, 0, 2).reshape(a.shape[1], -1)


class _Plan:
    def __init__(self, shards, w_in_gathered, n_tap_cols):
        self.shards = shards
        self.gathered = {"w_in": w_in_gathered}
        self.parts = {}
        self.n_tap_cols = n_tap_cols

    def comm(self, name, dw):
        if name in FWD_CARRY:
            return _Comm(replicated=[self.shards[k] for k in FWD_CARRY[name]])
        if name in BWD_CARRY:
            return _Comm(chunked=[dw[k].reshape((N_DEV,) + SHARD_SHAPE[k]) for k in BWD_CARRY[name]])
        return _NO_COMM

    def done(self, name, got):
        if name in FWD_CARRY:
            self.gathered.update(zip(FWD_CARRY[name], got))
        elif name in BWD_CARRY:
            self.parts.update(zip(BWD_CARRY[name], got))

    def w(self, key):
        g = self.gathered[key]
        if key in ("w_in", "w_up", "w_ckv"):
            return g
        if key in ("w_attn_proj", "w_conv_proj"):
            return _cols_to_2d(g)
        if key == "taps":
            return jnp.pad(_cols_to_2d(g[:, :CONV_K, :self.n_tap_cols]), ((0, 1), (0, 0)))
        return g.reshape(-1, g.shape[-1])


def _adamw(name, w, m, v, parts):
    rows, cols = w.shape
    n_parts = parts.shape[0]
    rb = rows if rows <= 256 or rows % 256 else 256

    def body(w_ref, m_ref, v_ref, p_ref, g_ref, d_ref, nm_ref, nv_ref):
        g = p_ref[0].astype(F32)
        for q in range(1, n_parts):
            g = g + p_ref[q].astype(F32)
        wv = w_ref[...]
        nm = ADAM_B1 * m_ref[...] + (1.0 - ADAM_B1) * g
        nv = ADAM_B2 * v_ref[...] + (1.0 - ADAM_B2) * jnp.square(g)
        m_hat = nm / (1.0 - ADAM_B1 ** ADAM_STEP)
        v_hat = nv / (1.0 - ADAM_B2 ** ADAM_STEP)
        g_ref[...] = g
        d_ref[...] = -ADAM_LR * (m_hat / (jnp.sqrt(v_hat) + ADAM_EPS) + ADAM_WD * wv)
        nm_ref[...] = nm
        nv_ref[...] = nv

    blk = pl.BlockSpec((rb, cols), lambda i: (i, 0))
    return pl.pallas_call(
        body, name=name, out_shape=[jax.ShapeDtypeStruct((rows, cols), F32)] * 4, grid=(rows // rb,),
        in_specs=[blk, blk, blk, pl.BlockSpec((n_parts, rb, cols), lambda i: (0, i, 0))], out_specs=[blk] * 4,
        compiler_params=_params(dimension_semantics=("arbitrary",)),
    )(w, m, v, parts)


BIG = ("w_in", "w_attn_proj", "w_conv_proj", "w_out", "w_cq", "w_ckv", "w_co", "w_up", "w_down")
SMALL = ("g_mix", "b_gate", "conv_b", "conv_ln_g", "conv_ln_b", "g_cross", "g_mem", "g_mlp", "g_final")
PACK_ORDER = SMALL + ("loss", "conv_w")
PACK_ROWS = 272
WEIGHTS = ("g_mix", "w_in", "b_gate", "conv_w", "conv_b", "conv_ln_g", "conv_ln_b", "w_attn_proj", "w_conv_proj", "w_out",
           "g_cross", "g_mem", "w_cq", "w_ckv", "w_co", "g_mlp", "w_up", "w_down", "g_final")


def _pack(d):
    flat = jnp.concatenate([d[k].reshape(-1) for k in PACK_ORDER])
    return jnp.pad(flat, (0, PACK_ROWS * HEAD_DIM - flat.shape[0])).reshape(PACK_ROWS, HEAD_DIM)


def _unpack(p, sizes):
    flat, out, off = p.reshape(-1), {}, 0
    for k in PACK_ORDER:
        out[k] = flat[off:off + sizes[k]]
        off += sizes[k]
    return out


def kernel(x, mem, g_mix, w_in, b_gate, conv_w, conv_b, conv_ln_g, conv_ln_b, w_attn_proj, w_conv_proj, w_out, g_cross, g_mem, w_cq, w_ckv, w_co, g_mlp, w_up, w_down, g_final, loss_target, m_g_mix, m_w_in, m_b_gate, m_conv_w, m_conv_b, m_conv_ln_g, m_conv_ln_b, m_w_attn_proj, m_w_conv_proj, m_w_out, m_g_cross, m_g_mem, m_w_cq, m_w_ckv, m_w_co, m_g_mlp, m_w_up, m_w_down, m_g_final, v_g_mix, v_w_in, v_b_gate, v_conv_w, v_conv_b, v_conv_ln_g, v_conv_ln_b, v_w_attn_proj, v_w_conv_proj, v_w_out, v_g_cross, v_g_mem, v_w_cq, v_w_ckv, v_w_co, v_g_mlp, v_w_up, v_w_down, v_g_final):
    args = dict(locals())
    wts = {k: args[k] for k in WEIGHTS}
    mom = {k: args["m_" + k] for k in WEIGHTS}
    var = {k: args["v_" + k] for k in WEIGHTS}
    two_d = lambda a: a.reshape(a.shape[-2:]) if a.ndim == 3 else a.reshape(1, -1)

    shards = {k: two_d(wts[k]).astype(BF16) for k in BIG}
    shards["taps"] = jnp.pad(two_d(conv_w), ((0, 1), (0, HEAD_DIM - conv_w.shape[-1])))
    plan = _Plan(shards, _exchange("gather_w_in", _Comm(replicated=[shards["w_in"]]))[0], conv_w.shape[-1])
    sm = {k: two_d(wts[k]) for k in SMALL}

    grad_x, _, small = _local_step(x[0], mem[0], loss_target[0], sm, plan)
    parts = plan.parts
    small_parts = _exchange("exchange_small", _Comm(replicated=[_pack(small)]))[0]

    out = {}
    for k in BIG:
        res = _adamw("adamw_" + k, two_d(wts[k]), two_d(mom[k]), two_d(var[k]), parts[k])
        out[k] = [r.reshape(wts[k].shape) for r in res]
    sizes = {k: small[k].size for k in PACK_ORDER}
    zeros = dict(loss=jnp.zeros_like(small["loss"]), conv_w=jnp.zeros_like(small["conv_w"]))
    packed = _adamw("adamw_small", _pack({**{k: wts[k] for k in SMALL}, **zeros}), _pack({**{k: mom[k] for k in SMALL}, **zeros}),
                    _pack({**{k: var[k] for k in SMALL}, **zeros}), small_parts)
    unpacked = [_unpack(p, sizes) for p in packed]
    for k in SMALL:
        out[k] = [u[k].reshape(wts[k].shape) for u in unpacked]
    loss = unpacked[0]["loss"][0]
    me = 4 * lax.axis_index("x") + 2 * lax.axis_index("y") + lax.axis_index("c")
    n_tap_cols = conv_w.shape[-1]
    g_taps = lax.dynamic_slice(unpacked[0]["conv_w"].reshape(CONV_HALO, CONV_CH), (0, me * n_tap_cols), (CONV_K, n_tap_cols))
    res = _adamw("adamw_conv_w", two_d(conv_w), two_d(m_conv_w), two_d(v_conv_w), g_taps[None])
    out["conv_w"] = [r.reshape(conv_w.shape) for r in res]

    return (loss, grad_x[None], *[out[k][0] for k in WEIGHTS], *[out[k][1] for k in WEIGHTS],
            *[out[k][2] for k in WEIGHTS], *[out[k][3] for k in WEIGHTS])
```

```python
import functools

import jax
import jax.numpy as jnp
from jax import lax
from jax.experimental import pallas as pl
from jax.experimental.pallas import tpu as pltpu

F32 = jnp.float32
BF16 = jnp.bfloat16

N_DEV = 8
D_MODEL = 1024
N_MEM = 256
HEAD_DIM = 128
HEADS_PER_GROUP = 4
GROUP_W = HEADS_PER_GROUP * HEAD_DIM
DILATIONS = (1, 4, 16)
BAND = 128
N_GROUPS = 3
ATTN_W = N_GROUPS * GROUP_W
QKV_W = 3 * ATTN_W
ROT_DIM = HEAD_DIM // 4
ROPE_THETA = 500000.0
CONV_CH = 768
CONV_K = 31
CONV_HALO = 32
SUBLANES = 8
CONV_ROWS = 64
IN_W = 8192
GLU_COL_BLK = QKV_W // (2 * CONV_CH)
GATE_COL_BLK = (QKV_W + 2 * CONV_CH) // (2 * D_MODEL)
CROSS_HEADS = 4
CROSS_HD = D_MODEL // CROSS_HEADS
D_FF = 4096
EPS = 1e-6
NEG = -1e30
QB = 4
ROW_BLK = QB * BAND

ADAM_LR = 0.001
ADAM_B1 = 0.9
ADAM_B2 = 0.999
ADAM_EPS = 1e-08
ADAM_WD = 0.01
ADAM_STEP = 10

VMEM_LIMIT = 56 * 1024 * 1024
MESH = pl.DeviceIdType.MESH


def _params(**kw):
    return pltpu.CompilerParams(vmem_limit_bytes=VMEM_LIMIT, **kw)


def _sigmoid(x):
    return 1.0 / (1.0 + jnp.exp(-x))


def _dot(a, b, kind):
    dims = {"nn": (((1,), (0,)), ((), ())), "nt": (((1,), (1,)), ((), ())), "tn": (((0,), (0,)), ((), ()))}[kind]
    if a.dtype != BF16:
        a = a.astype(BF16)
    if b.dtype != BF16:
        b = b.astype(BF16)
    return lax.dot_general(a, b, dims, preferred_element_type=F32)


def _peers():
    x, y, c = lax.axis_index("x"), lax.axis_index("y"), lax.axis_index("c")
    me = 4 * x + 2 * y + c
    peers = [(x, y, 1 - c), (1 - x, y, c), (x, 1 - y, c), (1 - x, 1 - y, c),
             (1 - x, y, 1 - c), (x, 1 - y, 1 - c), (1 - x, 1 - y, 1 - c)]
    return me, peers


class _Comm:
    def __init__(self, chunked=(), replicated=()):
        self.arrays = list(chunked) + list(replicated)
        self.n_c = len(chunked)
        self.n = len(self.arrays)
        self.out_shape = [jax.ShapeDtypeStruct(a.shape, a.dtype) for a in chunked]
        self.out_shape += [jax.ShapeDtypeStruct((N_DEV,) + a.shape, a.dtype) for a in replicated]
        self.in_specs = [pl.BlockSpec(memory_space=pl.ANY)] * self.n
        self.out_specs = [pl.BlockSpec(memory_space=pl.ANY)] * self.n
        self.scratch = [pltpu.SemaphoreType.DMA((self.n,))] * 5 if self.n else []

    @staticmethod
    def _where():
        x, y, c = lax.axis_index("x"), lax.axis_index("y"), lax.axis_index("c")
        chips = [(1 - x, y), (x, 1 - y), (1 - x, 1 - y)]
        return (x, y, c), 4 * x + 2 * y + c, chips

    def _local(self, ins, outs, sems, a, me):
        src = ins[a].at[me] if a < self.n_c else ins[a]
        return pltpu.make_async_copy(src, outs[a].at[me], sems[2].at[a])

    @staticmethod
    def _remote(src, dst, send, recv, to):
        return pltpu.make_async_remote_copy(src_ref=src, dst_ref=dst, send_sem=send, recv_sem=recv, device_id=to,
                                            device_id_type=MESH)

    def start(self, ins, outs, sems):
        (x, y, c), me, chips = self._where()
        for a in range(self.n):
            self._local(ins, outs, sems, a, me).start()
            if a < self.n_c:
                for (px, py, pc) in _peers()[1]:
                    self._remote(ins[a].at[4 * px + 2 * py + pc], outs[a].at[me], sems[0].at[a], sems[1].at[a], (px, py, pc)).start()
            else:
                self._remote(ins[a], outs[a].at[me], sems[3].at[a], sems[4].at[a], (x, y, 1 - c)).start()
                for (px, py) in chips:
                    self._remote(ins[a], outs[a].at[me], sems[0].at[a], sems[1].at[a], (px, py, c)).start()

    def wait(self, ins, outs, sems):
        (x, y, c), me, chips = self._where()
        sibling = (x, y, 1 - c)

        def drain(a, pair, count):
            blocks = outs[a].at[pl.ds(0, count)]
            cp = self._remote(blocks, blocks, sems[pair].at[a], sems[pair + 1].at[a], sibling)
            cp.wait_send()
            cp.wait_recv()

        for a in range(self.n):
            if a < self.n_c:
                drain(a, 0, N_DEV - 1)
            else:
                drain(a, 0, len(chips))
                for (px, py) in chips:
                    blk = outs[a].at[4 * px + 2 * py + c]
                    self._remote(blk, blk, sems[3].at[a], sems[4].at[a], sibling).start()
        for a in range(self.n):
            if a >= self.n_c:
                drain(a, 3, len(chips) + 1)
            self._local(ins, outs, sems, a, me).wait()


_NO_COMM = _Comm()


def _pcall(body, name, grid, operands, in_specs, out_shape, out_specs, scratch=(), aliases=None, comm=_NO_COMM):
    n_in, n_out, n_scr = len(operands), len(out_shape), len(scratch)
    grid = tuple(grid)

    def carried(*refs):
        ins, c_in = refs[:n_in], refs[n_in:n_in + comm.n]
        o0 = n_in + comm.n
        outs, c_out = refs[o0:o0 + n_out], refs[o0 + n_out:o0 + n_out + comm.n]
        s0 = o0 + n_out + comm.n
        scr, sems = refs[s0:s0 + n_scr], refs[s0 + n_scr:]
        ids = [pl.program_id(ax) for ax in range(len(grid))]

        @pl.when(functools.reduce(jnp.logical_and, [p == 0 for p in ids]))
        def _():
            comm.start(c_in, c_out, sems)

        body(*ins, *outs, *scr)

        @pl.when(functools.reduce(jnp.logical_and, [p == g - 1 for p, g in zip(ids, grid)]))
        def _():
            comm.wait(c_in, c_out, sems)

    return pl.pallas_call(
        carried if comm.n else body, name=name, grid=grid, in_specs=list(in_specs) + comm.in_specs,
        out_shape=list(out_shape) + comm.out_shape, out_specs=list(out_specs) + comm.out_specs,
        scratch_shapes=list(scratch) + comm.scratch, input_output_aliases=aliases or {},
        compiler_params=_params(dimension_semantics=("arbitrary",) * len(grid)),
    )(*operands, *comm.arrays)


def _mm(name, a, b, kind, grid, a_blk, b_blk, outs, extras=(), epi=None, acc_outs=(), j_outer=False, comm=_NO_COMM,
        split=None, b_resident=False):
    gi, gj, gk = grid
    n_ex = len(extras)
    n_out = len(outs)
    mode, n_chunks = split if split is not None else (None, 1)

    def spec(blk, fn, **kw):
        return pl.BlockSpec(blk, (lambda j, i, k: fn(i, j, k)) if j_outer else fn, **kw)

    def b_chunk(b_ref, c):
        if len(b_ref.shape) == 3:
            return b_ref[c]
        rows, cols = b_ref.shape
        if (kind == "nn") == (mode == "cols"):
            return b_ref[:, c * (cols // n_chunks):(c + 1) * (cols // n_chunks)]
        return b_ref[c * (rows // n_chunks):(c + 1) * (rows // n_chunks), :]

    def col_chunk(ref, c):
        width = ref.shape[-1] // n_chunks
        return slice(c * width, (c + 1) * width)

    def body(*refs):
        a_ref, b_ref = refs[0], refs[1]
        ex = refs[2:2 + n_ex]
        out_refs = refs[2 + n_ex:2 + n_ex + n_out]
        acc_ref = refs[2 + n_ex + n_out] if gk > 1 else None
        i = pl.program_id(1 if j_outer else 0)
        k = pl.program_id(2)
        if mode == "cols":
            a_val = a_ref[...]
            for c in range(n_chunks):
                acc = _dot(a_val, b_chunk(b_ref, c), kind)
                vals = epi(acc, *[e[:, col_chunk(e, c)] for e in ex]) if epi is not None else (acc,)
                for o, v in zip(out_refs, vals):
                    o[:, col_chunk(o, c)] = v.astype(o.dtype)
            return
        if mode == "sum":
            part = _dot(a_ref[:, col_chunk(a_ref, 0)], b_chunk(b_ref, 0), kind)
            for c in range(1, n_chunks):
                part = part + _dot(a_ref[:, col_chunk(a_ref, c)], b_chunk(b_ref, c), kind)
        else:
            part = _dot(a_ref[...], b_ref[...], kind)

        def finish(acc):
            vals = epi(acc, *[e[...] for e in ex]) if epi is not None else (acc,)
            for idx, (o, v) in enumerate(zip(out_refs, vals)):
                if idx in acc_outs:
                    @pl.when(i == 0)
                    def _():
                        o[...] = v.astype(o.dtype)

                    @pl.when(i != 0)
                    def _():
                        o[...] += v.astype(o.dtype)
                else:
                    o[...] = v.astype(o.dtype)

        if gk == 1:
            finish(part)
        else:
            @pl.when(k == 0)
            def _():
                acc_ref[...] = part

            @pl.when(k != 0)
            def _():
                acc_ref[...] += part

            @pl.when(k == gk - 1)
            def _():
                finish(acc_ref[...])

    scratch = []
    if gk > 1:
        tm = a_blk[0][-1] if kind == "tn" else a_blk[0][-2]
        tn = b_blk[0][-2] if kind == "nt" else b_blk[0][-1]
        scratch = [pltpu.VMEM((tm, tn), F32)]
    b_kw = dict(pipeline_mode=pl.Buffered(1)) if b_resident else {}
    return _pcall(body, name, (gj, gi, gk) if j_outer else (gi, gj, gk), [a, b] + [e for e, _, _ in extras],
                  [spec(*a_blk), spec(*b_blk, **b_kw)] + [spec(blk, fn) for _, blk, fn in extras],
                  [s for s, _, _ in outs], [spec(blk, fn) for _, blk, fn in outs], scratch, comm=comm)


def _rms_fwd_vals(x, g):
    r = lax.rsqrt(jnp.mean(x * x, axis=-1, keepdims=True) + EPS)
    return x * r * g


def _rms_bwd_vals(x, g, du):
    r = lax.rsqrt(jnp.mean(x * x, axis=-1, keepdims=True) + EPS)
    xh = x * r
    dxh = du * g
    dx = r * (dxh - xh * jnp.mean(dxh * xh, axis=-1, keepdims=True))
    return dx, jnp.sum(du * xh, axis=0, keepdims=True)


def _rms_fwd(name, x, g, rows):
    n = x.shape[0]

    def body(x_ref, g_ref, o_ref):
        o_ref[...] = _rms_fwd_vals(x_ref[...], g_ref[...]).astype(BF16)

    return pl.pallas_call(
        body, name=name, out_shape=jax.ShapeDtypeStruct(x.shape, BF16), grid=(n // rows,),
        in_specs=[pl.BlockSpec((rows, D_MODEL), lambda i: (i, 0)), pl.BlockSpec((1, D_MODEL), lambda i: (0, 0))],
        out_specs=pl.BlockSpec((rows, D_MODEL), lambda i: (i, 0)),
        compiler_params=_params(dimension_semantics=("arbitrary",)),
    )(x, g)


def _rope_tables(seq):
    half = ROT_DIM // 2
    pos = jnp.arange(seq, dtype=F32)
    inv_freq = ROPE_THETA ** (-jnp.arange(0, ROT_DIM, 2, dtype=F32) / ROT_DIM)
    ang = pos[:, None] * inv_freq[None, :]
    cos, sin = jnp.cos(ang), jnp.sin(ang)
    rest = HEAD_DIM - ROT_DIM
    c = jnp.concatenate([cos, cos, jnp.ones((seq, rest), F32)], axis=1)
    s1 = jnp.concatenate([jnp.zeros((seq, half), F32), sin, jnp.zeros((seq, rest), F32)], axis=1)
    s2 = jnp.concatenate([-sin, jnp.zeros((seq, half + rest), F32)], axis=1)
    return c, s1, s2


def _group_shapes(seq, width, dtype):
    return [jax.ShapeDtypeStruct((d, seq // d, width), dtype) for d in DILATIONS]


def _group_specs(width):
    return [pl.BlockSpec((d, ROW_BLK // d, width), lambda i: (0, i, 0)) for d in DILATIONS]


def _qkv_prep(z, tabs):
    seq = z.shape[0]

    def body(z_ref, c_ref, s1_ref, s2_ref, a0, a1, a2, sc):
        outs = (a0, a1, a2)
        c, s1, s2 = c_ref[...], s1_ref[...], s2_ref[...]
        for part in range(3):
            for hh in range(N_GROUPS * HEADS_PER_GROUP):
                g, hl = divmod(hh, HEADS_PER_GROUP)
                col = part * ATTN_W + hh * HEAD_DIM
                ocol = part * GROUP_W + hl * HEAD_DIM
                x = z_ref[:, col:col + HEAD_DIM].astype(F32)
                if part < 2:
                    x = x * c + pltpu.roll(x, ROT_DIM // 2, 1) * s1 + pltpu.roll(x, HEAD_DIM - ROT_DIM // 2, 1) * s2
                d = DILATIONS[g]
                if d == 1:
                    outs[g][0, :, ocol:ocol + HEAD_DIM] = x.astype(BF16)
                else:
                    sc[...] = x
                    for r in range(d):
                        outs[g][r, :, ocol:ocol + HEAD_DIM] = sc[pl.ds(r, ROW_BLK // d, stride=d), :].astype(BF16)

    tab_spec = pl.BlockSpec((ROW_BLK, HEAD_DIM), lambda i: (i, 0))
    return pl.pallas_call(
        body, name="qkv_prep", out_shape=_group_shapes(seq, ATTN_W, BF16), grid=(seq // ROW_BLK,),
        in_specs=[pl.BlockSpec((ROW_BLK, QKV_W), lambda i: (i, 0)), tab_spec, tab_spec, tab_spec],
        out_specs=_group_specs(ATTN_W), scratch_shapes=[pltpu.VMEM((ROW_BLK, HEAD_DIM), F32)],
        compiler_params=_params(dimension_semantics=("arbitrary",)),
    )(z, *tabs)


def _band_masks():
    qi = lax.broadcasted_iota(jnp.int32, (BAND, BAND), 0)
    kj = lax.broadcasted_iota(jnp.int32, (BAND, BAND), 1)
    return kj >= qi, kj <= qi


def _attn_fwd(name, a_g, comm=_NO_COMM):
    dil, m_len, _ = a_g.shape
    qb = min(QB, m_len // BAND)
    rows = qb * BAND
    steps = m_len // rows
    scale = HEAD_DIM ** -0.5

    def body(q_ref, kc_ref, vc_ref, kp_ref, vp_ref, o_ref, l_ref):
        t = pl.program_id(1)
        m_prev, m_cur = _band_masks()
        for sb in range(qb):
            rs = slice(sb * BAND, (sb + 1) * BAND)
            for h in range(HEADS_PER_GROUP):
                cs = slice(h * HEAD_DIM, (h + 1) * HEAD_DIM)
                q = q_ref[rs, cs]
                if sb == 0:
                    kp, vp = kp_ref[:, cs], vp_ref[:, cs]
                    pm = jnp.logical_and(m_prev, t > 0)
                else:
                    ps = slice((sb - 1) * BAND, sb * BAND)
                    kp, vp = kc_ref[ps, cs], vc_ref[ps, cs]
                    pm = m_prev
                kc, vc = kc_ref[rs, cs], vc_ref[rs, cs]
                sp = jnp.where(pm, _dot(q, kp, "nt") * scale, NEG)
                sc = jnp.where(m_cur, _dot(q, kc, "nt") * scale, NEG)
                mx = jnp.maximum(jnp.max(sp, axis=-1, keepdims=True), jnp.max(sc, axis=-1, keepdims=True))
                pp = jnp.exp(sp - mx)
                pc = jnp.exp(sc - mx)
                den = jnp.sum(pp, axis=-1, keepdims=True) + jnp.sum(pc, axis=-1, keepdims=True)
                o = (_dot(pp, vp, "nn") + _dot(pc, vc, "nn")) / den
                o_ref[rs, cs] = o
                l_ref[rs, cs] = jnp.broadcast_to(mx + jnp.log(den), (BAND, HEAD_DIM))

    def prev(r, t):
        return jnp.maximum(qb * t - 1, 0)

    cur = lambda c: pl.BlockSpec((None, rows, GROUP_W), lambda r, t, c=c: (r, t, c))
    prv = lambda c: pl.BlockSpec((None, BAND, GROUP_W), lambda r, t, c=c: (r, prev(r, t), c))
    out_spec = pl.BlockSpec((None, rows, GROUP_W), lambda r, t: (r, t, 0))
    shp = jax.ShapeDtypeStruct((dil, m_len, GROUP_W), F32)
    return _pcall(body, name, (dil, steps), [a_g] * 5, [cur(0), cur(1), cur(2), prv(1), prv(2)], [shp, shp],
                  [out_spec, out_spec], comm=comm)


def _attn_merge(os_, ls_, seq):
    def body(o0, l0, o1, l1, o2, l2, at_ref, lt_ref, sc):
        for h in range(HEADS_PER_GROUP):
            cs = slice(h * HEAD_DIM, (h + 1) * HEAD_DIM)
            for gi, (o_r, l_r) in enumerate(((o1, l1), (o2, l2))):
                d = DILATIONS[gi + 1]
                for r in range(d):
                    sc.at[2 * gi][pl.ds(r, ROW_BLK // d, stride=d), :] = o_r[r, :, cs]
                    sc.at[2 * gi + 1][pl.ds(r, ROW_BLK // d, stride=d), :] = l_r[r, :, cs]
            o0v, l0v = o0[0, :, cs], l0[0, :, cs]
            o1v, l1v, o2v, l2v = sc[0], sc[1], sc[2], sc[3]
            mx = jnp.maximum(jnp.maximum(l0v, l1v), l2v)
            e0, e1, e2 = jnp.exp(l0v - mx), jnp.exp(l1v - mx), jnp.exp(l2v - mx)
            tot = e0 + e1 + e2
            at_ref[:, cs] = ((e0 * o0v + e1 * o1v + e2 * o2v) / tot).astype(BF16)
            lt_ref[:, cs] = mx + jnp.log(tot)

    gs = _group_specs(GROUP_W)
    row = pl.BlockSpec((ROW_BLK, GROUP_W), lambda i: (i, 0))
    return pl.pallas_call(
        body, name="attn_merge",
        out_shape=[jax.ShapeDtypeStruct((seq, GROUP_W), BF16), jax.ShapeDtypeStruct((seq, GROUP_W), F32)],
        grid=(seq // ROW_BLK,), in_specs=[gs[0], gs[0], gs[1], gs[1], gs[2], gs[2]], out_specs=[row, row],
        scratch_shapes=[pltpu.VMEM((4, ROW_BLK, HEAD_DIM), F32)],
        compiler_params=_params(dimension_semantics=("arbitrary",)),
    )(os_[0], ls_[0], os_[1], ls_[1], os_[2], ls_[2])


def _attn_bwd_prep(dattn, attn, lt):
    seq = dattn.shape[0]

    def body(da_ref, at_ref, lt_ref, d0, c0, t0, d1, c1, t1, d2, c2, t2, sc):
        outs = ((d0, c0, t0), (d1, c1, t1), (d2, c2, t2))
        for h in range(HEADS_PER_GROUP):
            cs = slice(h * HEAD_DIM, (h + 1) * HEAD_DIM)
            da = da_ref[:, cs].astype(F32)
            cc = jnp.broadcast_to(jnp.sum(da * at_ref[:, cs].astype(F32), axis=-1, keepdims=True), (ROW_BLK, HEAD_DIM))
            ltv = lt_ref[:, cs]
            d0[0, :, cs] = da_ref[:, cs]
            c0[0, :, cs] = cc
            t0[0, :, cs] = ltv
            sc[0], sc[1], sc[2] = da, cc, ltv
            for g in (1, 2):
                d = DILATIONS[g]
                for r in range(d):
                    rows = pl.ds(r, ROW_BLK // d, stride=d)
                    outs[g][0][r, :, cs] = sc.at[0][rows, :].astype(BF16)
                    outs[g][1][r, :, cs] = sc.at[1][rows, :]
                    outs[g][2][r, :, cs] = sc.at[2][rows, :]

    gs = _group_specs(GROUP_W)
    row = pl.BlockSpec((ROW_BLK, GROUP_W), lambda i: (i, 0))
    shapes, specs = [], []
    for g, d in enumerate(DILATIONS):
        for dt in (BF16, F32, F32):
            shapes.append(jax.ShapeDtypeStruct((d, seq // d, GROUP_W), dt))
            specs.append(gs[g])
    res = pl.pallas_call(
        body, name="attn_bwd_prep", out_shape=shapes, grid=(seq // ROW_BLK,), in_specs=[row, row, row],
        out_specs=specs, scratch_shapes=[pltpu.VMEM((3, ROW_BLK, HEAD_DIM), F32)],
        compiler_params=_params(dimension_semantics=("arbitrary",)),
    )(dattn, attn, lt)
    return [res[3 * g:3 * g + 3] for g in range(N_GROUPS)]


def _attn_bwd(name, a_g, da_g, c_g, lt_g, comm=_NO_COMM):
    dil, m_len, _ = a_g.shape
    qb = min(QB, m_len // BAND)
    rows = qb * BAND
    steps = m_len // rows
    scale = HEAD_DIM ** -0.5

    def body(q_ref, kc_ref, vc_ref, kp_ref, vp_ref, da_ref, c_ref, lt_ref, d_ref, dk_acc, dv_acc, car_k, car_v):
        tg = pl.program_id(1)
        t = steps - 1 - tg
        m_prev, m_cur = _band_masks()

        @pl.when(tg == 0)
        def _():
            car_k[...] = jnp.zeros_like(car_k)
            car_v[...] = jnp.zeros_like(car_v)

        zero = jnp.zeros((rows, GROUP_W), F32)
        dk_acc[0:rows, :] = zero
        dv_acc[0:rows, :] = zero
        dk_acc[rows:rows + BAND, :] = car_k[...]
        dv_acc[rows:rows + BAND, :] = car_v[...]
        for sb in range(qb):
            rs = slice(sb * BAND, (sb + 1) * BAND)
            ps = slice((sb - 1) * BAND, sb * BAND)
            acc_p = slice(sb * BAND, (sb + 1) * BAND)
            acc_c = slice((sb + 1) * BAND, (sb + 2) * BAND)
            for h in range(HEADS_PER_GROUP):
                cs = slice(h * HEAD_DIM, (h + 1) * HEAD_DIM)
                q, da = q_ref[rs, cs], da_ref[rs, cs]
                cc, ltv = c_ref[rs, cs], lt_ref[rs, cs]
                if sb == 0:
                    kp, vp = kp_ref[:, cs], vp_ref[:, cs]
                    pm = jnp.logical_and(m_prev, t > 0)
                else:
                    kp, vp = kc_ref[ps, cs], vc_ref[ps, cs]
                    pm = m_prev
                kc, vc = kc_ref[rs, cs], vc_ref[rs, cs]
                pp = jnp.exp(jnp.where(pm, _dot(q, kp, "nt") * scale - ltv, NEG))
                pc = jnp.exp(jnp.where(m_cur, _dot(q, kc, "nt") * scale - ltv, NEG))
                dsp = (pp * (_dot(da, vp, "nt") - cc) * scale).astype(BF16)
                dsc = (pc * (_dot(da, vc, "nt") - cc) * scale).astype(BF16)
                ppb, pcb = pp.astype(BF16), pc.astype(BF16)
                d_ref[rs, cs] = (_dot(dsp, kp, "nn") + _dot(dsc, kc, "nn")).astype(BF16)
                dk_acc[acc_p, cs] += _dot(dsp, q, "tn")
                dk_acc[acc_c, cs] += _dot(dsc, q, "tn")
                dv_acc[acc_p, cs] += _dot(ppb, da, "tn")
                dv_acc[acc_c, cs] += _dot(pcb, da, "tn")
        d_ref[:, GROUP_W:2 * GROUP_W] = dk_acc[BAND:rows + BAND, :].astype(BF16)
        d_ref[:, 2 * GROUP_W:3 * GROUP_W] = dv_acc[BAND:rows + BAND, :].astype(BF16)
        car_k[...] = dk_acc[0:BAND, :]
        car_v[...] = dv_acc[0:BAND, :]

    def rev(tg):
        return steps - 1 - tg

    def prev(tg):
        return jnp.maximum(qb * rev(tg) - 1, 0)

    cur = lambda c: pl.BlockSpec((None, rows, GROUP_W), lambda r, tg, c=c: (r, rev(tg), c))
    prv = lambda c: pl.BlockSpec((None, BAND, GROUP_W), lambda r, tg, c=c: (r, prev(tg), c))
    return _pcall(
        body, name, (dil, steps), [a_g, a_g, a_g, a_g, a_g, da_g, c_g, lt_g],
        [cur(0), cur(1), cur(2), prv(1), prv(2), cur(0), cur(0), cur(0)],
        [jax.ShapeDtypeStruct((dil, m_len, ATTN_W), BF16)], [pl.BlockSpec((None, rows, ATTN_W), lambda r, tg: (r, rev(tg), 0))],
        [pltpu.VMEM((rows + BAND, GROUP_W), F32), pltpu.VMEM((rows + BAND, GROUP_W), F32),
         pltpu.VMEM((BAND, GROUP_W), F32), pltpu.VMEM((BAND, GROUP_W), F32)], comm=comm)


def _dqkv_post(d_gs, tabs, dz):
    seq = dz.shape[0]

    def body(g0, g1, g2, c_ref, s1_ref, s2_ref, dz_any, o_ref, sc):
        del dz_any
        ins = (g0, g1, g2)
        c, s1, s2 = c_ref[...], s1_ref[...], s2_ref[...]
        for part in range(3):
            for hh in range(N_GROUPS * HEADS_PER_GROUP):
                g, hl = divmod(hh, HEADS_PER_GROUP)
                icol = part * GROUP_W + hl * HEAD_DIM
                ocol = part * ATTN_W + hh * HEAD_DIM
                d = DILATIONS[g]
                if d == 1:
                    x = ins[g][0, :, icol:icol + HEAD_DIM].astype(F32)
                else:
                    for r in range(d):
                        sc[pl.ds(r, ROW_BLK // d, stride=d), :] = ins[g][r, :, icol:icol + HEAD_DIM].astype(F32)
                    x = sc[...]
                if part < 2:
                    x = x * c + pltpu.roll(x * s1, HEAD_DIM - ROT_DIM // 2, 1) + pltpu.roll(x * s2, ROT_DIM // 2, 1)
                o_ref[:, ocol:ocol + HEAD_DIM] = x.astype(BF16)

    tab_spec = pl.BlockSpec((ROW_BLK, HEAD_DIM), lambda i: (i, 0))
    return pl.pallas_call(
        body, name="dqkv_post", out_shape=jax.ShapeDtypeStruct(dz.shape, BF16), grid=(seq // ROW_BLK,),
        in_specs=_group_specs(ATTN_W) + [tab_spec, tab_spec, tab_spec, pl.BlockSpec(memory_space=pl.ANY)],
        out_specs=pl.BlockSpec((ROW_BLK, QKV_W), lambda i: (i, 0)),
        scratch_shapes=[pltpu.VMEM((ROW_BLK, HEAD_DIM), F32)], input_output_aliases={6: 0},
        compiler_params=_params(dimension_semantics=("arbitrary",)),
    )(*d_gs, *tabs, dz)


def _glu(zg):
    a = zg[:, :CONV_CH].astype(F32)
    s = _sigmoid(zg[:, CONV_CH:].astype(F32))
    return a, s, a * s


def _shifted_copies(xs):
    n = xs.shape[1] - SUBLANES
    for b in range(1, SUBLANES):
        xs[b, 0:n, :] = xs[0, pl.ds(b, n), :]


def _shifted(xs, offset, r0, cs):
    a, b = divmod(offset, SUBLANES)
    return xs[b, pl.ds(SUBLANES * a + r0, CONV_ROWS), cs]


def _conv_fwd(z, cw, cb, lg, lb):
    seq = z.shape[0]
    halo_per_blk = ROW_BLK // CONV_HALO

    def body(zg_ref, zh_ref, cw_ref, cb_ref, lg_ref, lb_ref, c2_ref, c4_ref, xs):
        i = pl.program_id(0)
        _, _, c1 = _glu(zg_ref[...])
        _, _, c1h = _glu(zh_ref[...])
        xs[0, 0:CONV_HALO, :] = jnp.where(i > 0, c1h, 0.0)
        xs[0, CONV_HALO:, :] = c1
        _shifted_copies(xs)
        for s in range(CONV_CH // HEAD_DIM):
            cs = slice(s * HEAD_DIM, (s + 1) * HEAD_DIM)
            taps = [cw_ref[j:j + 1, cs] for j in range(CONV_K)]
            bias = cb_ref[:, cs]

            def chunk(rc, carry, cs=cs, taps=taps, bias=bias):
                r0 = pl.multiple_of(rc * CONV_ROWS, CONV_ROWS)
                acc = jnp.zeros((CONV_ROWS, HEAD_DIM), F32)
                for j in range(CONV_K):
                    acc = acc + taps[j] * _shifted(xs, CONV_HALO - (CONV_K - 1) + j, r0, cs)
                c2_ref[pl.ds(r0, CONV_ROWS), cs] = acc + bias
                return carry

            lax.fori_loop(0, ROW_BLK // CONV_ROWS, chunk, 0)
        c2 = c2_ref[...]
        mu = jnp.mean(c2, axis=-1, keepdims=True)
        xc = c2 - mu
        rstd = lax.rsqrt(jnp.mean(xc * xc, axis=-1, keepdims=True) + EPS)
        c3 = xc * rstd * lg_ref[...] + lb_ref[...]
        c4_ref[...] = (c3 * _sigmoid(c3)).astype(BF16)

    vec = pl.BlockSpec((1, CONV_CH), lambda i: (0, 0))
    return pl.pallas_call(
        body, name="conv_fwd",
        out_shape=[jax.ShapeDtypeStruct((seq, CONV_CH), F32), jax.ShapeDtypeStruct((seq, CONV_CH), BF16)],
        grid=(seq // ROW_BLK,),
        in_specs=[pl.BlockSpec((ROW_BLK, 2 * CONV_CH), lambda i: (i, GLU_COL_BLK)),
                  pl.BlockSpec((CONV_HALO, 2 * CONV_CH), lambda i: (jnp.maximum(i * halo_per_blk - 1, 0), GLU_COL_BLK)),
                  pl.BlockSpec((CONV_HALO, CONV_CH), lambda i: (0, 0)), vec, vec, vec],
        out_specs=[pl.BlockSpec((ROW_BLK, CONV_CH), lambda i: (i, 0)), pl.BlockSpec((ROW_BLK, CONV_CH), lambda i: (i, 0))],
        scratch_shapes=[pltpu.VMEM((SUBLANES, ROW_BLK + CONV_HALO, CONV_CH), F32)],
        compiler_params=_params(dimension_semantics=("arbitrary",)),
    )(z, z, cw, cb, lg, lb)


def _conv_bwd(dc2, z, cw, dz, comm=_NO_COMM):
    seq = z.shape[0]
    halo_per_blk = ROW_BLK // CONV_HALO
    n_blk = seq // ROW_BLK
    last_halo = seq // CONV_HALO - 1

    def body(dc_ref, dn_ref, zg_ref, zh_ref, cw_ref, dz_any, o_ref, dcw_ref, xs, ys, dc1_ref, dcw_acc):
        del dz_any
        i = pl.program_id(0)
        a, s, c1 = _glu(zg_ref[...])
        _, _, c1h = _glu(zh_ref[...])
        xs[0, 0:CONV_HALO, :] = jnp.where(i > 0, c1h, 0.0)
        xs[0, CONV_HALO:, :] = c1
        ys[0, 0:ROW_BLK, :] = dc_ref[...]
        ys[0, ROW_BLK:, :] = jnp.where(i < n_blk - 1, dn_ref[...], 0.0)
        _shifted_copies(xs)
        _shifted_copies(ys)

        @pl.when(i == 0)
        def _():
            dcw_acc[...] = jnp.zeros_like(dcw_acc)

        for sl in range(CONV_CH // HEAD_DIM):
            cs = slice(sl * HEAD_DIM, (sl + 1) * HEAD_DIM)
            taps = [cw_ref[j:j + 1, cs] for j in range(CONV_K)]

            def chunk(rc, carry, cs=cs, taps=taps):
                r0 = pl.multiple_of(rc * CONV_ROWS, CONV_ROWS)
                dc = ys[0, pl.ds(r0, CONV_ROWS), cs]
                acc = jnp.zeros((CONV_ROWS, HEAD_DIM), F32)
                for j in range(CONV_K):
                    prod = dc * _shifted(xs, CONV_HALO - (CONV_K - 1) + j, r0, cs)
                    dcw_acc[j, :, cs] += jnp.sum(prod.reshape(CONV_ROWS // SUBLANES, SUBLANES, HEAD_DIM), axis=0)
                    acc = acc + taps[j] * _shifted(ys, CONV_K - 1 - j, r0, cs)
                dc1_ref[pl.ds(r0, CONV_ROWS), cs] = acc
                return carry

            lax.fori_loop(0, ROW_BLK // CONV_ROWS, chunk, 0)
        dc1 = dc1_ref[...]
        o_ref[:, :CONV_CH] = (dc1 * s).astype(BF16)
        o_ref[:, CONV_CH:] = (dc1 * a * s * (1.0 - s)).astype(BF16)

        @pl.when(i == n_blk - 1)
        def _():
            dcw_ref[...] = jnp.sum(dcw_acc[...], axis=1)

    return _pcall(
        body, "conv_bwd", (n_blk,), [dc2, dc2, z, z, cw, dz],
        [pl.BlockSpec((ROW_BLK, CONV_CH), lambda i: (i, 0)),
         pl.BlockSpec((CONV_HALO, CONV_CH), lambda i: (jnp.minimum((i + 1) * halo_per_blk, last_halo), 0)),
         pl.BlockSpec((ROW_BLK, 2 * CONV_CH), lambda i: (i, GLU_COL_BLK)),
         pl.BlockSpec((CONV_HALO, 2 * CONV_CH), lambda i: (jnp.maximum(i * halo_per_blk - 1, 0), GLU_COL_BLK)),
         pl.BlockSpec((CONV_HALO, CONV_CH), lambda i: (0, 0)),
         pl.BlockSpec(memory_space=pl.ANY)],
        [jax.ShapeDtypeStruct(dz.shape, BF16), jax.ShapeDtypeStruct((CONV_HALO, CONV_CH), F32)],
        [pl.BlockSpec((ROW_BLK, 2 * CONV_CH), lambda i: (i, GLU_COL_BLK)), pl.BlockSpec((CONV_HALO, CONV_CH), lambda i: (0, 0))],
        [pltpu.VMEM((SUBLANES, ROW_BLK + CONV_HALO, CONV_CH), F32), pltpu.VMEM((SUBLANES, ROW_BLK + CONV_HALO, CONV_CH), F32),
         pltpu.VMEM((ROW_BLK, CONV_CH), F32), pltpu.VMEM((CONV_HALO, SUBLANES, CONV_CH), F32)],
        aliases={5: 0}, comm=comm)


def _epi_mix(ya, c4, wcp, gates, bg):
    yc = _dot(c4, wcp, "nn")
    gv = _sigmoid(gates.astype(F32) + bg)
    merged = gv[:, :D_MODEL] * ya + gv[:, D_MODEL:] * yc
    return merged, ya, yc


def _epi_residual_rms(acc, xres, g):
    x = xres + acc
    return x, _rms_fwd_vals(x, g)


def _cross_scores(cq, ck):
    out = []
    for h in range(CROSS_HEADS):
        cs = slice(h * CROSS_HD, (h + 1) * CROSS_HD)
        s = _dot(cq[:, cs], ck[:, cs], "nt") * (CROSS_HD ** -0.5)
        e = jnp.exp(s - jnp.max(s, axis=-1, keepdims=True))
        out.append((cs, e, jnp.sum(e, axis=-1, keepdims=True)))
    return out


def _epi_cross_fwd(acc, ck, cv):
    cq = acc.astype(BF16)
    co = [_dot(e, cv[:, cs], "nn") / den for cs, e, den in _cross_scores(cq, ck)]
    return cq, jnp.concatenate(co, axis=1)


def _epi_cross_bwd(dco, cq, ck, cv):
    dco = dco.astype(BF16)
    dcq, dck, dcv = [], [], []
    for cs, e, den in _cross_scores(cq, ck):
        p = e / den
        dp = _dot(dco[:, cs], cv[:, cs], "nt")
        ds = (p * (dp - jnp.sum(dp * p, axis=-1, keepdims=True)) * (CROSS_HD ** -0.5)).astype(BF16)
        dcq.append(_dot(ds, ck[:, cs], "nn"))
        dck.append(_dot(ds, cq[:, cs], "tn"))
        dcv.append(_dot(p, dco[:, cs], "tn"))
    return jnp.concatenate(dcq, axis=1), jnp.concatenate(dck, axis=1), jnp.concatenate(dcv, axis=1)


def _epi_mlp_up(acc):
    return acc, jnp.square(jnp.maximum(acc, 0.0))


def _epi_final(acc, x2, tgt, g):
    x3 = x2 + acc
    err = _rms_fwd_vals(x3, g) - tgt
    loss = (0.5 / D_MODEL) * jnp.sum(err * err)
    dx3, dg = _rms_bwd_vals(x3, g, err * (1.0 / D_MODEL))
    return dx3, dx3, jnp.full((1, HEAD_DIM), loss, F32), dg


def _epi_mlp_down_bwd(dh, hpre):
    return (dh * 2.0 * jnp.maximum(hpre.astype(F32), 0.0),)


def _epi_rms_bwd(du, x, g, dres):
    dx, dg = _rms_bwd_vals(x, g, du)
    return dres + dx, dg


def _epi_rms_bwd_2(du, x, g, dres):
    dx, dg = _epi_rms_bwd(du, x, g, dres)
    return dx, dx, dg


def _epi_rms_bwd_g(du, x, g):
    return (_rms_bwd_vals(x, g, du)[1],)


def _epi_mix_bwd(dm, ya, yc, gates, bg):
    gv = _sigmoid(gates.astype(F32) + bg)
    ga, gb = gv[:, :D_MODEL], gv[:, D_MODEL:]
    ya, yc = ya.astype(F32), yc.astype(F32)
    dgate = jnp.concatenate([dm * ya * ga * (1.0 - ga), dm * yc * gb * (1.0 - gb)], axis=1)
    return dm * ga, dm * gb, dgate, jnp.sum(dgate, axis=0, keepdims=True)


def _epi_ln_bwd(dc4, c2, lg, lb):
    mu = jnp.mean(c2, axis=-1, keepdims=True)
    xc = c2 - mu
    rstd = lax.rsqrt(jnp.mean(xc * xc, axis=-1, keepdims=True) + EPS)
    xh = xc * rstd
    c3 = xh * lg + lb
    sg = _sigmoid(c3)
    dc3 = dc4 * sg * (1.0 + c3 * (1.0 - sg))
    dxh = dc3 * lg
    dc2 = rstd * (dxh - jnp.mean(dxh, axis=-1, keepdims=True) - xh * jnp.mean(dxh * xh, axis=-1, keepdims=True))
    return (dc2, jnp.sum(dc3 * xh, axis=0, keepdims=True), jnp.sum(dc3, axis=0, keepdims=True),
            jnp.sum(dc2, axis=0, keepdims=True))


def _sds(shape, dtype):
    return jax.ShapeDtypeStruct(shape, dtype)


class _Lazy:
    def __init__(self, fn):
        self.fn = fn

    def __getitem__(self, key):
        return self.fn(key)


def _local_step(x, mem, tgt, sm, plan):
    w = _Lazy(plan.w)
    dw = {}

    def mm(name, *args, **kw):
        c = plan.comm(name, dw)
        res = _mm(name, *args, comm=c, **kw)
        plan.done(name, res[len(res) - c.n:])
        return res[:len(res) - c.n]

    seq = x.shape[0]
    nr = seq // ROW_BLK
    big = min(1024, seq)
    nb = seq // big
    row = lambda n: ((ROW_BLK, n), lambda i, j, k: (i, 0))
    vec = lambda n: ((1, n), lambda i, j, k: (0, 0))
    full = lambda r, c: ((r, c), lambda i, j, k: (0, 0))
    gates_blk = ((ROW_BLK, 2 * D_MODEL), lambda i, j, k: (i, GATE_COL_BLK))
    tabs = _rope_tables(seq)

    u = _rms_fwd("rms_mix", x, sm["g_mix"], ROW_BLK)
    whole3 = lambda a: (a.shape, lambda i, j, k: (0, 0, 0))
    z = mm("in_proj", u, w["w_in"], "nn", (nr, 1, 1), row(D_MODEL), whole3(w["w_in"]),
           [(_sds((seq, IN_W), BF16), *row(IN_W))], split=("cols", N_DEV), b_resident=True)[0]
    a_gs = _qkv_prep(z, tabs)
    os_, ls_ = [], []
    for g in range(N_GROUPS):
        name = "attn_fwd_%d" % g
        c = plan.comm(name, dw)
        res = _attn_fwd(name, a_gs[g], c)
        plan.done(name, res[2:])
        os_.append(res[0])
        ls_.append(res[1])
    attn, lt = _attn_merge(os_, ls_, seq)
    c2, c4 = _conv_fwd(z, w["taps"], sm["conv_b"], sm["conv_ln_g"], sm["conv_ln_b"])
    merged, ya, yc = mm(
        "mix", attn, w["w_attn_proj"], "nn", (nr, 1, 1), row(GROUP_W), full(GROUP_W, D_MODEL),
        [(_sds((seq, D_MODEL), BF16), *row(D_MODEL))] * 3,
        extras=[(c4, *row(CONV_CH)), (w["w_conv_proj"], *full(CONV_CH, D_MODEL)), (z, *gates_blk), (sm["b_gate"], *vec(2 * D_MODEL))],
        epi=_epi_mix)
    x1, uq = mm("out_proj", merged, w["w_out"], "nn", (nr, 1, 1), row(D_MODEL), full(D_MODEL, D_MODEL),
                 [(_sds((seq, D_MODEL), F32), *row(D_MODEL)), (_sds((seq, D_MODEL), BF16), *row(D_MODEL))],
                 extras=[(x, *row(D_MODEL)), (sm["g_cross"], *vec(D_MODEL))], epi=_epi_residual_rms)

    mn = _rms_fwd("rms_mem", mem, sm["g_mem"], N_MEM)
    ckv = mm("ckv_proj", mn, w["w_ckv"], "nn", (1, N_DEV, 1), full(N_MEM, D_MODEL),
              ((None, D_MODEL, 2 * D_MODEL // N_DEV), lambda i, j, k: (j, 0, 0)),
              [(_sds((N_MEM, 2 * D_MODEL), BF16), (N_MEM, 2 * D_MODEL // N_DEV), lambda i, j, k: (0, j))])[0]
    ck, cv = ckv[:, :D_MODEL], ckv[:, D_MODEL:]
    kv_blk = full(N_MEM, D_MODEL)
    cq, co = mm("cq_proj_cross", uq, w["w_cq"], "nn", (nr, 1, 1), row(D_MODEL), full(D_MODEL, D_MODEL),
                 [(_sds((seq, D_MODEL), BF16), *row(D_MODEL))] * 2,
                 extras=[(ck, *kv_blk), (cv, *kv_blk)], epi=_epi_cross_fwd)
    x2, um = mm("co_proj", co, w["w_co"], "nn", (nr, 1, 1), row(D_MODEL), full(D_MODEL, D_MODEL),
                 [(_sds((seq, D_MODEL), F32), *row(D_MODEL)), (_sds((seq, D_MODEL), BF16), *row(D_MODEL))],
                 extras=[(x1, *row(D_MODEL)), (sm["g_mlp"], *vec(D_MODEL))], epi=_epi_residual_rms)

    ff_blk = D_FF // N_DEV
    row_f32 = (_sds((seq, D_MODEL), F32), *row(D_MODEL))
    row_bf16 = (_sds((seq, D_MODEL), BF16), *row(D_MODEL))
    col_sum = (_sds((1, D_MODEL), F32), *vec(D_MODEL))
    hpre, h = mm("mlp_up", um, w["w_up"], "nn", (nr, 1, 1), row(D_MODEL), whole3(w["w_up"]),
                 [(_sds((seq, D_FF), BF16), *row(D_FF))] * 2, epi=_epi_mlp_up, split=("cols", N_DEV), b_resident=True)
    kt = D_FF // D_MODEL
    dx3, dx3b, loss, dg_final = mm(
        "mlp_down_loss", h, w["w_down"], "nn", (nr, 1, 1), row(D_FF), full(D_FF, D_MODEL),
        [row_f32, row_bf16, (_sds((1, HEAD_DIM), F32), *vec(HEAD_DIM)), col_sum],
        extras=[(x2, *row(D_MODEL)), (tgt, *row(D_MODEL)), (sm["g_final"], *vec(D_MODEL))], epi=_epi_final, acc_outs=(2, 3),
        b_resident=True)

    dhpre = mm("mlp_down_bwd", dx3b, w["w_down"], "nt", (nr, 1, 1), row(D_MODEL), full(D_FF, D_MODEL),
               [(_sds((seq, D_FF), BF16), *row(D_FF))], extras=[(hpre, *row(D_FF))], epi=_epi_mlp_down_bwd,
               split=("cols", kt), b_resident=True)[0]
    big2 = min(2 * big, seq)
    nb2 = seq // big2
    dw["w_down"] = mm("dw_down", h, dx3b, "tn", (kt, 1, nb2), ((big2, D_MODEL), lambda i, j, k: (k, i)),
                      ((big2, D_MODEL), lambda i, j, k: (k, 0)),
                      [(_sds((D_FF, D_MODEL), BF16), (D_MODEL, D_MODEL), lambda i, j, k: (i, 0))])[0]
    dx2, dx2b, dg_mlp = mm("mlp_up_bwd", dhpre, w["w_up"], "nt", (nr, 1, 1), row(D_FF), whole3(w["w_up"]),
                           [row_f32, row_bf16, col_sum],
                           extras=[(x2, *row(D_MODEL)), (sm["g_mlp"], *vec(D_MODEL)), (dx3, *row(D_MODEL))],
                           epi=_epi_rms_bwd_2, acc_outs=(2,), split=("sum", N_DEV), b_resident=True)
    dw["w_up"] = mm("dw_up", um, dhpre, "tn", (1, N_DEV, nb2), ((big2, D_MODEL), lambda i, j, k: (k, 0)),
                    ((big2, ff_blk), lambda i, j, k: (k, j)),
                    [(_sds((N_DEV, D_MODEL, ff_blk), BF16), (None, D_MODEL, ff_blk), lambda i, j, k: (j, 0, 0))])[0]

    acc_kv = (_sds((N_MEM, D_MODEL), F32), *kv_blk)
    dcq, dck, dcv = mm("co_proj_bwd_cross", dx2b, w["w_co"], "nt", (nr, 1, 1), row(D_MODEL), full(D_MODEL, D_MODEL),
                       [row_bf16, acc_kv, acc_kv],
                       extras=[(cq, *row(D_MODEL)), (ck, *kv_blk), (cv, *kv_blk)], epi=_epi_cross_bwd, acc_outs=(1, 2))

    def dw_square(name, act, grad):
        return mm(name, act, grad, "tn", (1, 1, nb2), ((big2, D_MODEL), lambda i, j, k: (k, 0)),
                  ((big2, D_MODEL), lambda i, j, k: (k, 0)), [(_sds((D_MODEL, D_MODEL), BF16), *full(D_MODEL, D_MODEL))])[0]

    dw["w_co"] = dw_square("dw_co", co, dx2b)
    dx1, dx1b, dg_cross = mm("cq_proj_bwd", dcq, w["w_cq"], "nt", (nr, 1, 1), row(D_MODEL), full(D_MODEL, D_MODEL),
                             [row_f32, row_bf16, col_sum],
                             extras=[(x1, *row(D_MODEL)), (sm["g_cross"], *vec(D_MODEL)), (dx2, *row(D_MODEL))],
                             epi=_epi_rms_bwd_2, acc_outs=(2,))
    dw["w_cq"] = dw_square("dw_cq", uq, dcq)
    dckv = jnp.concatenate([dck, dcv], axis=1)
    kv_chunk = 2 * D_MODEL // N_DEV
    dw["w_ckv"] = mm("dw_ckv", mn, dckv, "tn", (1, N_DEV, 1), full(N_MEM, D_MODEL), ((N_MEM, kv_chunk), lambda i, j, k: (0, j)),
                      [(_sds((N_DEV, D_MODEL, kv_chunk), BF16), (None, D_MODEL, kv_chunk), lambda i, j, k: (j, 0, 0))])[0]
    dg_mem = mm("ckv_proj_bwd", dckv, w["w_ckv"], "nt", (1, 1, N_DEV), ((N_MEM, kv_chunk), lambda i, j, k: (0, k)),
                 ((None, D_MODEL, kv_chunk), lambda i, j, k: (k, 0, 0)), [(_sds((1, D_MODEL), F32), *vec(D_MODEL))],
                 extras=[(mem, *full(N_MEM, D_MODEL)), (sm["g_mem"], *vec(D_MODEL))], epi=_epi_rms_bwd_g, acc_outs=(0,))[0]

    dya, dyc, dz, db_gate = mm(
        "out_proj_bwd_mix", dx1b, w["w_out"], "nt", (nr, 1, 1), row(D_MODEL), full(D_MODEL, D_MODEL),
        [(_sds((seq, D_MODEL), BF16), *row(D_MODEL)), (_sds((seq, D_MODEL), BF16), *row(D_MODEL)),
         (_sds((seq, IN_W), BF16), *gates_blk), (_sds((1, 2 * D_MODEL), F32), *vec(2 * D_MODEL))],
        extras=[(ya, *row(D_MODEL)), (yc, *row(D_MODEL)), (z, *gates_blk), (sm["b_gate"], *vec(2 * D_MODEL))],
        epi=_epi_mix_bwd, acc_outs=(3,))
    dw["w_out"] = dw_square("dw_out", merged, dx1b)
    dattn = mm("attn_proj_bwd", dya, w["w_attn_proj"], "nt", (nr, 1, 1), row(D_MODEL), full(GROUP_W, D_MODEL),
                [(_sds((seq, GROUP_W), BF16), *row(GROUP_W))])[0]
    pc = D_MODEL // N_DEV
    dw["w_attn_proj"] = mm("dw_attn_proj", attn, dya, "tn", (1, N_DEV, nb), ((big, GROUP_W), lambda i, j, k: (k, 0)),
                            ((big, pc), lambda i, j, k: (k, j)),
                            [(_sds((N_DEV, GROUP_W, pc), BF16), (None, GROUP_W, pc), lambda i, j, k: (j, 0, 0))])[0]
    cvec = (_sds((1, CONV_CH), F32), *vec(CONV_CH))
    dc2, dg_ln_g, dg_ln_b, dg_conv_b = mm(
        "conv_proj_bwd_ln", dyc, w["w_conv_proj"], "nt", (nr, 1, 1), row(D_MODEL), full(CONV_CH, D_MODEL),
        [(_sds((seq, CONV_CH), F32), *row(CONV_CH)), cvec, cvec, cvec],
        extras=[(c2, *row(CONV_CH)), (sm["conv_ln_g"], *vec(CONV_CH)), (sm["conv_ln_b"], *vec(CONV_CH))],
        epi=_epi_ln_bwd, acc_outs=(1, 2, 3))
    dw["w_conv_proj"] = mm("dw_conv_proj", c4, dyc, "tn", (1, N_DEV, nb), ((big, CONV_CH), lambda i, j, k: (k, 0)),
                            ((big, pc), lambda i, j, k: (k, j)),
                            [(_sds((N_DEV, CONV_CH, pc), BF16), (None, CONV_CH, pc), lambda i, j, k: (j, 0, 0))])[0]
    c = plan.comm("conv_bwd", dw)
    res = _conv_bwd(dc2, z, w["taps"], dz, c)
    plan.done("conv_bwd", res[2:])
    dz, dg_conv_w = res[:2]
    preps = _attn_bwd_prep(dattn, attn, lt)
    d_gs = []
    for g in range(N_GROUPS):
        name = "attn_bwd_%d" % g
        c = plan.comm(name, dw)
        res = _attn_bwd(name, a_gs[g], *preps[g], comm=c)
        plan.done(name, res[1:])
        d_gs.append(res[0])
    dz = _dqkv_post(d_gs, tabs, dz)
    dw["w_in"] = mm("dw_in", u, dz, "tn", (1, N_DEV, nb2), ((big2, D_MODEL), lambda i, j, k: (k, 0)),
                    ((big2, D_MODEL), lambda i, j, k: (k, j)),
                    [(_sds((N_DEV, D_MODEL, D_MODEL), BF16), (None, D_MODEL, D_MODEL), lambda i, j, k: (j, 0, 0))])[0]
    grad_x, dg_mix = mm("in_proj_bwd", dz, w["w_in"], "nt", (nr, 1, 1), row(IN_W), whole3(w["w_in"]), [row_f32, col_sum],
                        extras=[(x, *row(D_MODEL)), (sm["g_mix"], *vec(D_MODEL)), (dx1, *row(D_MODEL))],
                        epi=_epi_rms_bwd, acc_outs=(1,), split=("sum", N_DEV), b_resident=True)
    small = dict(g_mix=dg_mix, b_gate=db_gate, conv_b=dg_conv_b, conv_ln_g=dg_ln_g, conv_ln_b=dg_ln_b, g_cross=dg_cross,
                 g_mem=dg_mem, g_mlp=dg_mlp, g_final=dg_final, loss=loss, conv_w=dg_conv_w)
    return grad_x, dw, small


def _exchange(name, comm):
    return _pcall(lambda: None, name, (1,), [], [], [], [], comm=comm)


SHARD_SHAPE = dict(w_in=(1024, 1024), w_attn_proj=(512, 128), w_conv_proj=(768, 128), w_out=(128, 1024), w_cq=(128, 1024),
                   w_ckv=(1024, 256), w_co=(128, 1024), w_up=(1024, 512), w_down=(512, 1024))
FWD_CARRY = {"in_proj": ("w_attn_proj", "w_conv_proj", "w_out", "w_cq", "w_ckv", "w_co", "taps"),
             "attn_fwd_0": ("w_up",), "attn_fwd_1": ("w_down",)}
BWD_CARRY = {"dw_up": ("w_down",), "out_proj_bwd_mix": ("w_co", "w_cq"), "dw_attn_proj": ("w_ckv",), "conv_bwd": ("w_up",),
             "attn_bwd_0": ("w_out", "w_attn_proj", "w_conv_proj"), "in_proj_bwd": ("w_in",)}


def _cols_to_2d(a):
    return a.transpose(1, 0, 2).reshape(a.shape[1], -1)


class _Plan:
    def __init__(self, shards, w_in_gathered, n_tap_cols):
        self.shards = shards
        self.gathered = {"w_in": w_in_gathered}
        self.parts = {}
        self.n_tap_cols = n_tap_cols

    def comm(self, name, dw):
        if name in FWD_CARRY:
            return _Comm(replicated=[self.shards[k] for k in FWD_CARRY[name]])
        if name in BWD_CARRY:
            return _Comm(chunked=[dw[k].reshape((N_DEV,) + SHARD_SHAPE[k]) for k in BWD_CARRY[name]])
        return _NO_COMM

    def done(self, name, got):
        if name in FWD_CARRY:
            self.gathered.update(zip(FWD_CARRY[name], got))
        elif name in BWD_CARRY:
            self.parts.update(zip(BWD_CARRY[name], got))

    def w(self, key):
        g = self.gathered[key]
        if key in ("w_in", "w_up", "w_ckv"):
            return g
        if key in ("w_attn_proj", "w_conv_proj"):
            return _cols_to_2d(g)
        if key == "taps":
            return jnp.pad(_cols_to_2d(g[:, :CONV_K, :self.n_tap_cols]), ((0, 1), (0, 0)))
        return g.reshape(-1, g.shape[-1])


def _adamw(name, w, m, v, parts):
    rows, cols = w.shape
    n_parts = parts.shape[0]
    rb = rows if rows <= 256 or rows % 256 else 256

    def body(w_ref, m_ref, v_ref, p_ref, g_ref, d_ref, nm_ref, nv_ref):
        g = p_ref[0].astype(F32)
        for q in range(1, n_parts):
            g = g + p_ref[q].astype(F32)
        wv = w_ref[...]
        nm = ADAM_B1 * m_ref[...] + (1.0 - ADAM_B1) * g
        nv = ADAM_B2 * v_ref[...] + (1.0 - ADAM_B2) * jnp.square(g)
        m_hat = nm / (1.0 - ADAM_B1 ** ADAM_STEP)
        v_hat = nv / (1.0 - ADAM_B2 ** ADAM_STEP)
        g_ref[...] = g
        d_ref[...] = -ADAM_LR * (m_hat / (jnp.sqrt(v_hat) + ADAM_EPS) + ADAM_WD * wv)
        nm_ref[...] = nm
        nv_ref[...] = nv

    blk = pl.BlockSpec((rb, cols), lambda i: (i, 0))
    return pl.pallas_call(
        body, name=name, out_shape=[jax.ShapeDtypeStruct((rows, cols), F32)] * 4, grid=(rows // rb,),
        in_specs=[blk, blk, blk, pl.BlockSpec((n_parts, rb, cols), lambda i: (0, i, 0))], out_specs=[blk] * 4,
        compiler_params=_params(dimension_semantics=("arbitrary",)),
    )(w, m, v, parts)


BIG = ("w_in", "w_attn_proj", "w_conv_proj", "w_out", "w_cq", "w_ckv", "w_co", "w_up", "w_down")
SMALL = ("g_mix", "b_gate", "conv_b", "conv_ln_g", "conv_ln_b", "g_cross", "g_mem", "g_mlp", "g_final")
PACK_ORDER = SMALL + ("loss", "conv_w")
PACK_ROWS = 272
WEIGHTS = ("g_mix", "w_in", "b_gate", "conv_w", "conv_b", "conv_ln_g", "conv_ln_b", "w_attn_proj", "w_conv_proj", "w_out",
           "g_cross", "g_mem", "w_cq", "w_ckv", "w_co", "g_mlp", "w_up", "w_down", "g_final")


def _pack(d):
    flat = jnp.concatenate([d[k].reshape(-1) for k in PACK_ORDER])
    return jnp.pad(flat, (0, PACK_ROWS * HEAD_DIM - flat.shape[0])).reshape(PACK_ROWS, HEAD_DIM)


def _unpack(p, sizes):
    flat, out, off = p.reshape(-1), {}, 0
    for k in PACK_ORDER:
        out[k] = flat[off:off + sizes[k]]
        off += sizes[k]
    return out


def kernel(x, mem, g_mix, w_in, b_gate, conv_w, conv_b, conv_ln_g, conv_ln_b, w_attn_proj, w_conv_proj, w_out, g_cross, g_mem, w_cq, w_ckv, w_co, g_mlp, w_up, w_down, g_final, loss_target, m_g_mix, m_w_in, m_b_gate, m_conv_w, m_conv_b, m_conv_ln_g, m_conv_ln_b, m_w_attn_proj, m_w_conv_proj, m_w_out, m_g_cross, m_g_mem, m_w_cq, m_w_ckv, m_w_co, m_g_mlp, m_w_up, m_w_down, m_g_final, v_g_mix, v_w_in, v_b_gate, v_conv_w, v_conv_b, v_conv_ln_g, v_conv_ln_b, v_w_attn_proj, v_w_conv_proj, v_w_out, v_g_cross, v_g_mem, v_w_cq, v_w_ckv, v_w_co, v_g_mlp, v_w_up, v_w_down, v_g_final):
    args = dict(locals())
    wts = {k: args[k] for k in WEIGHTS}
    mom = {k: args["m_" + k] for k in WEIGHTS}
    var = {k: args["v_" + k] for k in WEIGHTS}
    two_d = lambda a: a.reshape(a.shape[-2:]) if a.ndim == 3 else a.reshape(1, -1)

    shards = {k: two_d(wts[k]).astype(BF16) for k in BIG}
    shards["taps"] = jnp.pad(two_d(conv_w), ((0, 1), (0, HEAD_DIM - conv_w.shape[-1])))
    plan = _Plan(shards, _exchange("gather_w_in", _Comm(replicated=[shards["w_in"]]))[0], conv_w.shape[-1])
    sm = {k: two_d(wts[k]) for k in SMALL}

    grad_x, _, small = _local_step(x[0], mem[0], loss_target[0], sm, plan)
    parts = plan.parts
    small_parts = _exchange("exchange_small", _Comm(replicated=[_pack(small)]))[0]

    out = {}
    for k in BIG:
        res = _adamw("adamw_" + k, two_d(wts[k]), two_d(mom[k]), two_d(var[k]), parts[k])
        out[k] = [r.reshape(wts[k].shape) for r in res]
    sizes = {k: small[k].size for k in PACK_ORDER}
    zeros = dict(loss=jnp.zeros_like(small["loss"]), conv_w=jnp.zeros_like(small["conv_w"]))
    packed = _adamw("adamw_small", _pack({**{k: wts[k] for k in SMALL}, **zeros}), _pack({**{k: mom[k] for k in SMALL}, **zeros}),
                    _pack({**{k: var[k] for k in SMALL}, **zeros}), small_parts)
    unpacked = [_unpack(p, sizes) for p in packed]
    for k in SMALL:
        out[k] = [u[k].reshape(wts[k].shape) for u in unpacked]
    loss = unpacked[0]["loss"][0]
    me = 4 * lax.axis_index("x") + 2 * lax.axis_index("y") + lax.axis_index("c")
    n_tap_cols = conv_w.shape[-1]
    g_taps = lax.dynamic_slice(unpacked[0]["conv_w"].reshape(CONV_HALO, CONV_CH), (0, me * n_tap_cols), (CONV_K, n_tap_cols))
    res = _adamw("adamw_conv_w", two_d(conv_w), two_d(m_conv_w), two_d(v_conv_w), g_taps[None])
    out["conv_w"] = [r.reshape(conv_w.shape) for r in res]

    return (loss, grad_x[None], *[out[k][0] for k in WEIGHTS], *[out[k][1] for k in WEIGHTS],
            *[out[k][2] for k in WEIGHTS], *[out[k][3] for k in WEIGHTS])
```

```python
import functools

import jax
import jax.numpy as jnp
from jax import lax
from jax.experimental import pallas as pl
from jax.experimental.pallas import tpu as pltpu

F32 = jnp.float32
BF16 = jnp.bfloat16

N_DEV = 8
D_MODEL = 1024
N_MEM = 256
HEAD_DIM = 128
HEADS_PER_GROUP = 4
GROUP_W = HEADS_PER_GROUP * HEAD_DIM
DILATIONS = (1, 4, 16)
BAND = 128
N_GROUPS = 3
ATTN_W = N_GROUPS * GROUP_W
QKV_W = 3 * ATTN_W
ROT_DIM = HEAD_DIM // 4
ROPE_THETA = 500000.0
CONV_CH = 768
CONV_K = 31
CONV_HALO = 32
SUBLANES = 8
CONV_ROWS = 64
IN_W = 8192
GLU_COL_BLK = QKV_W // (2 * CONV_CH)
GATE_COL_BLK = (QKV_W + 2 * CONV_CH) // (2 * D_MODEL)
CROSS_HEADS = 4
CROSS_HD = D_MODEL // CROSS_HEADS
D_FF = 4096
EPS = 1e-6
NEG = -1e30
QB = 4
ROW_BLK = QB * BAND

ADAM_LR = 0.001
ADAM_B1 = 0.9
ADAM_B2 = 0.999
ADAM_EPS = 1e-08
ADAM_WD = 0.01
ADAM_STEP = 10

VMEM_LIMIT = 56 * 1024 * 1024
MESH = pl.DeviceIdType.MESH


def _params(**kw):
    return pltpu.CompilerParams(vmem_limit_bytes=VMEM_LIMIT, **kw)


def _sigmoid(x):
    return 1.0 / (1.0 + jnp.exp(-x))


def _dot(a, b, kind):
    dims = {"nn": (((1,), (0,)), ((), ())), "nt": (((1,), (1,)), ((), ())), "tn": (((0,), (0,)), ((), ()))}[kind]
    if a.dtype != BF16:
        a = a.astype(BF16)
    if b.dtype != BF16:
        b = b.astype(BF16)
    return lax.dot_general(a, b, dims, preferred_element_type=F32)


def _peers():
    x, y, c = lax.axis_index("x"), lax.axis_index("y"), lax.axis_index("c")
    me = 4 * x + 2 * y + c
    peers = [(x, y, 1 - c), (1 - x, y, c), (x, 1 - y, c), (1 - x, 1 - y, c),
             (1 - x, y, 1 - c), (x, 1 - y, 1 - c), (1 - x, 1 - y, 1 - c)]
    return me, peers


class _Comm:
    def __init__(self, chunked=(), replicated=()):
        self.arrays = list(chunked) + list(replicated)
        self.n_c = len(chunked)
        self.n = len(self.arrays)
        self.out_shape = [jax.ShapeDtypeStruct(a.shape, a.dtype) for a in chunked]
        self.out_shape += [jax.ShapeDtypeStruct((N_DEV,) + a.shape, a.dtype) for a in replicated]
        self.in_specs = [pl.BlockSpec(memory_space=pl.ANY)] * self.n
        self.out_specs = [pl.BlockSpec(memory_space=pl.ANY)] * self.n
        self.scratch = [pltpu.SemaphoreType.DMA((self.n,))] * 5 if self.n else []

    @staticmethod
    def _where():
        x, y, c = lax.axis_index("x"), lax.axis_index("y"), lax.axis_index("c")
        chips = [(1 - x, y), (x, 1 - y), (1 - x, 1 - y)]
        return (x, y, c), 4 * x + 2 * y + c, chips

    def _local(self, ins, outs, sems, a, me):
        src = ins[a].at[me] if a < self.n_c else ins[a]
        return pltpu.make_async_copy(src, outs[a].at[me], sems[2].at[a])

    @staticmethod
    def _remote(src, dst, send, recv, to):
        return pltpu.make_async_remote_copy(src_ref=src, dst_ref=dst, send_sem=send, recv_sem=recv, device_id=to,
                                            device_id_type=MESH)

    def start(self, ins, outs, sems):
        (x, y, c), me, chips = self._where()
        for a in range(self.n):
            self._local(ins, outs, sems, a, me).start()
            if a < self.n_c:
                for (px, py, pc) in _peers()[1]:
                    self._remote(ins[a].at[4 * px + 2 * py + pc], outs[a].at[me], sems[0].at[a], sems[1].at[a], (px, py, pc)).start()
            else:
                self._remote(ins[a], outs[a].at[me], sems[3].at[a], sems[4].at[a], (x, y, 1 - c)).start()
                for (px, py) in chips:
                    self._remote(ins[a], outs[a].at[me], sems[0].at[a], sems[1].at[a], (px, py, c)).start()

    def wait(self, ins, outs, sems):
        (x, y, c), me, chips = self._where()
        sibling = (x, y, 1 - c)

        def drain(a, pair, count):
            blocks = outs[a].at[pl.ds(0, count)]
            cp = self._remote(blocks, blocks, sems[pair].at[a], sems[pair + 1].at[a], sibling)
            cp.wait_send()
            cp.wait_recv()

        for a in range(self.n):
            if a < self.n_c:
                drain(a, 0, N_DEV - 1)
            else:
                drain(a, 0, len(chips))
                for (px, py) in chips:
                    blk = outs[a].at[4 * px + 2 * py + c]
                    self._remote(blk, blk, sems[3].at[a], sems[4].at[a], sibling).start()
        for a in range(self.n):
            if a >= self.n_c:
                drain(a, 3, len(chips) + 1)
            self._local(ins, outs, sems, a, me).wait()


_NO_COMM = _Comm()


def _pcall(body, name, grid, operands, in_specs, out_shape, out_specs, scratch=(), aliases=None, comm=_NO_COMM):
    n_in, n_out, n_scr = len(operands), len(out_shape), len(scratch)
    grid = tuple(grid)

    def carried(*refs):
        ins, c_in = refs[:n_in], refs[n_in:n_in + comm.n]
        o0 = n_in + comm.n
        outs, c_out = refs[o0:o0 + n_out], refs[o0 + n_out:o0 + n_out + comm.n]
        s0 = o0 + n_out + comm.n
        scr, sems = refs[s0:s0 + n_scr], refs[s0 + n_scr:]
        ids = [pl.program_id(ax) for ax in range(len(grid))]

        @pl.when(functools.reduce(jnp.logical_and, [p == 0 for p in ids]))
        def _():
            comm.start(c_in, c_out, sems)

        body(*ins, *outs, *scr)

        @pl.when(functools.reduce(jnp.logical_and, [p == g - 1 for p, g in zip(ids, grid)]))
        def _():
            comm.wait(c_in, c_out, sems)

    return pl.pallas_call(
        carried if comm.n else body, name=name, grid=grid, in_specs=list(in_specs) + comm.in_specs,
        out_shape=list(out_shape) + comm.out_shape, out_specs=list(out_specs) + comm.out_specs,
        scratch_shapes=list(scratch) + comm.scratch, input_output_aliases=aliases or {},
        compiler_params=_params(dimension_semantics=("arbitrary",) * len(grid)),
    )(*operands, *comm.arrays)


def _mm(name, a, b, kind, grid, a_blk, b_blk, outs, extras=(), epi=None, acc_outs=(), j_outer=False, comm=_NO_COMM,
        split=None, b_resident=False, out_chunks=0):
    gi, gj, gk = grid
    n_ex = len(extras)
    n_out = len(outs)
    mode, n_chunks = split if split is not None else (None, 1)

    def spec(blk, fn, **kw):
        return pl.BlockSpec(blk, (lambda j, i, k: fn(i, j, k)) if j_outer else fn, **kw)

    def b_chunk(b_ref, c):
        if len(b_ref.shape) == 3:
            return b_ref[c]
        rows, cols = b_ref.shape
        if (kind == "nn") == (mode == "cols"):
            return b_ref[:, c * (cols // n_chunks):(c + 1) * (cols // n_chunks)]
        return b_ref[c * (rows // n_chunks):(c + 1) * (rows // n_chunks), :]

    def col_chunk(ref, c):
        width = ref.shape[-1] // n_chunks
        return slice(c * width, (c + 1) * width)

    def body(*refs):
        a_ref, b_ref = refs[0], refs[1]
        ex = refs[2:2 + n_ex]
        out_refs = refs[2 + n_ex:2 + n_ex + n_out]
        acc_ref = refs[2 + n_ex + n_out] if gk > 1 else None
        i = pl.program_id(1 if j_outer else 0)
        k = pl.program_id(2)
        if mode == "cols":
            a_val = a_ref[...]
            for c in range(n_chunks):
                acc = _dot(a_val, b_chunk(b_ref, c), kind)
                vals = epi(acc, *[e[:, col_chunk(e, c)] for e in ex]) if epi is not None else (acc,)
                for o, v in zip(out_refs, vals):
                    o[:, col_chunk(o, c)] = v.astype(o.dtype)
            return
        if mode == "sum":
            part = _dot(a_ref[:, col_chunk(a_ref, 0)], b_chunk(b_ref, 0), kind)
            for c in range(1, n_chunks):
                part = part + _dot(a_ref[:, col_chunk(a_ref, c)], b_chunk(b_ref, c), kind)
        else:
            part = _dot(a_ref[...], b_ref[...], kind)

        def finish(acc):
            if out_chunks:
                width = acc.shape[-1] // out_chunks
                for c in range(out_chunks):
                    out_refs[0][c] = acc[:, c * width:(c + 1) * width].astype(out_refs[0].dtype)
                return
            vals = epi(acc, *[e[...] for e in ex]) if epi is not None else (acc,)
            for idx, (o, v) in enumerate(zip(out_refs, vals)):
                if idx in acc_outs:
                    @pl.when(i == 0)
                    def _():
                        o[...] = v.astype(o.dtype)

                    @pl.when(i != 0)
                    def _():
                        o[...] += v.astype(o.dtype)
                else:
                    o[...] = v.astype(o.dtype)

        if gk == 1:
            finish(part)
        else:
            @pl.when(k == 0)
            def _():
                acc_ref[...] = part

            @pl.when(k != 0)
            def _():
                acc_ref[...] += part

            @pl.when(k == gk - 1)
            def _():
                finish(acc_ref[...])

    scratch = []
    if gk > 1:
        tm = a_blk[0][-1] if kind == "tn" else a_blk[0][-2]
        tn = b_blk[0][-2] if kind == "nt" else b_blk[0][-1]
        scratch = [pltpu.VMEM((tm, tn), F32)]
    b_kw = dict(pipeline_mode=pl.Buffered(1)) if b_resident else {}
    return _pcall(body, name, (gj, gi, gk) if j_outer else (gi, gj, gk), [a, b] + [e for e, _, _ in extras],
                  [spec(*a_blk), spec(*b_blk, **b_kw)] + [spec(blk, fn) for _, blk, fn in extras],
                  [s for s, _, _ in outs], [spec(blk, fn) for _, blk, fn in outs], scratch, comm=comm)


def _rms_fwd_vals(x, g):
    r = lax.rsqrt(jnp.mean(x * x, axis=-1, keepdims=True) + EPS)
    return x * r * g


def _rms_bwd_vals(x, g, du):
    r = lax.rsqrt(jnp.mean(x * x, axis=-1, keepdims=True) + EPS)
    xh = x * r
    dxh = du * g
    dx = r * (dxh - xh * jnp.mean(dxh * xh, axis=-1, keepdims=True))
    return dx, jnp.sum(du * xh, axis=0, keepdims=True)


def _rms_fwd(name, x, g, rows):
    n = x.shape[0]

    def body(x_ref, g_ref, o_ref):
        o_ref[...] = _rms_fwd_vals(x_ref[...], g_ref[...]).astype(BF16)

    return pl.pallas_call(
        body, name=name, out_shape=jax.ShapeDtypeStruct(x.shape, BF16), grid=(n // rows,),
        in_specs=[pl.BlockSpec((rows, D_MODEL), lambda i: (i, 0)), pl.BlockSpec((1, D_MODEL), lambda i: (0, 0))],
        out_specs=pl.BlockSpec((rows, D_MODEL), lambda i: (i, 0)),
        compiler_params=_params(dimension_semantics=("arbitrary",)),
    )(x, g)


def _rope_tables(seq):
    half = ROT_DIM // 2
    pos = jnp.arange(seq, dtype=F32)
    inv_freq = ROPE_THETA ** (-jnp.arange(0, ROT_DIM, 2, dtype=F32) / ROT_DIM)
    ang = pos[:, None] * inv_freq[None, :]
    cos, sin = jnp.cos(ang), jnp.sin(ang)
    rest = HEAD_DIM - ROT_DIM
    c = jnp.concatenate([cos, cos, jnp.ones((seq, rest), F32)], axis=1)
    s1 = jnp.concatenate([jnp.zeros((seq, half), F32), sin, jnp.zeros((seq, rest), F32)], axis=1)
    s2 = jnp.concatenate([-sin, jnp.zeros((seq, half + rest), F32)], axis=1)
    return c, s1, s2


def _group_shapes(seq, width, dtype):
    return [jax.ShapeDtypeStruct((d, seq // d, width), dtype) for d in DILATIONS]


def _group_specs(width):
    return [pl.BlockSpec((d, ROW_BLK // d, width), lambda i: (0, i, 0)) for d in DILATIONS]


def _qkv_prep(z, tabs, comm=_NO_COMM):
    seq = z.shape[0]

    def body(z_ref, c_ref, s1_ref, s2_ref, a0, a1, a2, sc):
        outs = (a0, a1, a2)
        c, s1, s2 = c_ref[...], s1_ref[...], s2_ref[...]
        for part in range(3):
            for hh in range(N_GROUPS * HEADS_PER_GROUP):
                g, hl = divmod(hh, HEADS_PER_GROUP)
                col = part * ATTN_W + hh * HEAD_DIM
                ocol = part * GROUP_W + hl * HEAD_DIM
                x = z_ref[:, col:col + HEAD_DIM].astype(F32)
                if part < 2:
                    x = x * c + pltpu.roll(x, ROT_DIM // 2, 1) * s1 + pltpu.roll(x, HEAD_DIM - ROT_DIM // 2, 1) * s2
                d = DILATIONS[g]
                if d == 1:
                    outs[g][0, :, ocol:ocol + HEAD_DIM] = x.astype(BF16)
                else:
                    sc[...] = x
                    for r in range(d):
                        outs[g][r, :, ocol:ocol + HEAD_DIM] = sc[pl.ds(r, ROW_BLK // d, stride=d), :].astype(BF16)

    tab_spec = pl.BlockSpec((ROW_BLK, HEAD_DIM), lambda i: (i, 0))
    return _pcall(body, "qkv_prep", (seq // ROW_BLK,), [z, *tabs],
                  [pl.BlockSpec((ROW_BLK, QKV_W), lambda i: (i, 0)), tab_spec, tab_spec, tab_spec],
                  _group_shapes(seq, ATTN_W, BF16), _group_specs(ATTN_W), [pltpu.VMEM((ROW_BLK, HEAD_DIM), F32)], comm=comm)


def _band_masks_2(t):
    qi = lax.broadcasted_iota(jnp.int32, (BAND, 2 * BAND), 0)
    kj = lax.broadcasted_iota(jnp.int32, (BAND, 2 * BAND), 1)
    band = jnp.logical_and(kj >= qi, kj <= qi + BAND)
    return band, jnp.logical_and(band, jnp.logical_or(kj >= BAND, t > 0))


def _attn_fwd(name, a_g, comm=_NO_COMM):
    dil, m_len, _ = a_g.shape
    qb = min(QB, m_len // BAND)
    rows = qb * BAND
    steps = m_len // rows
    scale = HEAD_DIM ** -0.5

    tiles = [(sb, h) for sb in range(qb) for h in range(HEADS_PER_GROUP)]

    def body(q_ref, kc_ref, vc_ref, kp_ref, vp_ref, o_ref, l_ref, k_all, v_all, s_scr, p_scr, r_scr):
        t = pl.program_id(1)
        k_all[0:BAND, :] = kp_ref[...]
        k_all[BAND:, :] = kc_ref[...]
        v_all[0:BAND, :] = vp_ref[...]
        v_all[BAND:, :] = vc_ref[...]
        band, band_first = _band_masks_2(t)
        for idx, (sb, h) in enumerate(tiles):
            cs = slice(h * HEAD_DIM, (h + 1) * HEAD_DIM)
            s = _dot(q_ref[sb * BAND:(sb + 1) * BAND, cs], k_all[sb * BAND:(sb + 2) * BAND, cs], "nt") * scale
            s_scr[idx] = jnp.where(band_first if sb == 0 else band, s, NEG)
        for idx, (sb, h) in enumerate(tiles):
            cs = slice(h * HEAD_DIM, (h + 1) * HEAD_DIM)
            s = s_scr[idx]
            mx = jnp.max(s, axis=-1, keepdims=True)
            p = jnp.exp(s - mx)
            den = jnp.sum(p, axis=-1, keepdims=True)
            p_scr[idx] = p.astype(BF16)
            r_scr[idx] = jnp.broadcast_to(1.0 / den, (BAND, HEAD_DIM))
            l_ref[sb * BAND:(sb + 1) * BAND, cs] = jnp.broadcast_to(mx + jnp.log(den), (BAND, HEAD_DIM))
        for idx, (sb, h) in enumerate(tiles):
            cs = slice(h * HEAD_DIM, (h + 1) * HEAD_DIM)
            o_ref[sb * BAND:(sb + 1) * BAND, cs] = _dot(p_scr[idx], v_all[sb * BAND:(sb + 2) * BAND, cs], "nn") * r_scr[idx]

    def prev(r, t):
        return jnp.maximum(qb * t - 1, 0)

    cur = lambda c: pl.BlockSpec((None, rows, GROUP_W), lambda r, t, c=c: (r, t, c))
    prv = lambda c: pl.BlockSpec((None, BAND, GROUP_W), lambda r, t, c=c: (r, prev(r, t), c))
    out_spec = pl.BlockSpec((None, rows, GROUP_W), lambda r, t: (r, t, 0))
    shp = jax.ShapeDtypeStruct((dil, m_len, GROUP_W), F32)
    n_t = len(tiles)
    return _pcall(body, name, (dil, steps), [a_g] * 5, [cur(0), cur(1), cur(2), prv(1), prv(2)], [shp, shp],
                  [out_spec, out_spec],
                  [pltpu.VMEM((rows + BAND, GROUP_W), BF16), pltpu.VMEM((rows + BAND, GROUP_W), BF16),
                   pltpu.VMEM((n_t, BAND, 2 * BAND), F32), pltpu.VMEM((n_t, BAND, 2 * BAND), BF16),
                   pltpu.VMEM((n_t, BAND, HEAD_DIM), F32)], comm=comm)


def _attn_merge(os_, ls_, seq):
    def body(o0, l0, o1, l1, o2, l2, at_ref, lt_ref, sc):
        for h in range(HEADS_PER_GROUP):
            cs = slice(h * HEAD_DIM, (h + 1) * HEAD_DIM)
            for gi, (o_r, l_r) in enumerate(((o1, l1), (o2, l2))):
                d = DILATIONS[gi + 1]
                for r in range(d):
                    sc.at[2 * gi][pl.ds(r, ROW_BLK // d, stride=d), :] = o_r[r, :, cs]
                    sc.at[2 * gi + 1][pl.ds(r, ROW_BLK // d, stride=d), :] = l_r[r, :, cs]
            o0v, l0v = o0[0, :, cs], l0[0, :, cs]
            o1v, l1v, o2v, l2v = sc[0], sc[1], sc[2], sc[3]
            mx = jnp.maximum(jnp.maximum(l0v, l1v), l2v)
            e0, e1, e2 = jnp.exp(l0v - mx), jnp.exp(l1v - mx), jnp.exp(l2v - mx)
            tot = e0 + e1 + e2
            at_ref[:, cs] = ((e0 * o0v + e1 * o1v + e2 * o2v) / tot).astype(BF16)
            lt_ref[:, cs] = mx + jnp.log(tot)

    gs = _group_specs(GROUP_W)
    row = pl.BlockSpec((ROW_BLK, GROUP_W), lambda i: (i, 0))
    return pl.pallas_call(
        body, name="attn_merge",
        out_shape=[jax.ShapeDtypeStruct((seq, GROUP_W), BF16), jax.ShapeDtypeStruct((seq, GROUP_W), F32)],
        grid=(seq // ROW_BLK,), in_specs=[gs[0], gs[0], gs[1], gs[1], gs[2], gs[2]], out_specs=[row, row],
        scratch_shapes=[pltpu.VMEM((4, ROW_BLK, HEAD_DIM), F32)],
        compiler_params=_params(dimension_semantics=("arbitrary",)),
    )(os_[0], ls_[0], os_[1], ls_[1], os_[2], ls_[2])


def _attn_bwd_prep(dattn, attn, lt):
    seq = dattn.shape[0]

    def body(da_ref, at_ref, lt_ref, d0, c0, t0, d1, c1, t1, d2, c2, t2, sc):
        outs = ((d0, c0, t0), (d1, c1, t1), (d2, c2, t2))
        for h in range(HEADS_PER_GROUP):
            cs = slice(h * HEAD_DIM, (h + 1) * HEAD_DIM)
            da = da_ref[:, cs].astype(F32)
            cc = jnp.broadcast_to(jnp.sum(da * at_ref[:, cs].astype(F32), axis=-1, keepdims=True), (ROW_BLK, HEAD_DIM))
            ltv = lt_ref[:, cs]
            d0[0, :, cs] = da_ref[:, cs]
            c0[0, :, cs] = cc
            t0[0, :, cs] = ltv
            sc[0], sc[1], sc[2] = da, cc, ltv
            for g in (1, 2):
                d = DILATIONS[g]
                for r in range(d):
                    rows = pl.ds(r, ROW_BLK // d, stride=d)
                    outs[g][0][r, :, cs] = sc.at[0][rows, :].astype(BF16)
                    outs[g][1][r, :, cs] = sc.at[1][rows, :]
                    outs[g][2][r, :, cs] = sc.at[2][rows, :]

    gs = _group_specs(GROUP_W)
    row = pl.BlockSpec((ROW_BLK, GROUP_W), lambda i: (i, 0))
    shapes, specs = [], []
    for g, d in enumerate(DILATIONS):
        for dt in (BF16, F32, F32):
            shapes.append(jax.ShapeDtypeStruct((d, seq // d, GROUP_W), dt))
            specs.append(gs[g])
    res = pl.pallas_call(
        body, name="attn_bwd_prep", out_shape=shapes, grid=(seq // ROW_BLK,), in_specs=[row, row, row],
        out_specs=specs, scratch_shapes=[pltpu.VMEM((3, ROW_BLK, HEAD_DIM), F32)],
        compiler_params=_params(dimension_semantics=("arbitrary",)),
    )(dattn, attn, lt)
    return [res[3 * g:3 * g + 3] for g in range(N_GROUPS)]


def _attn_bwd(name, a_g, da_g, c_g, lt_g, comm=_NO_COMM):
    dil, m_len, _ = a_g.shape
    qb = min(QB, m_len // BAND)
    rows = qb * BAND
    steps = m_len // rows
    scale = HEAD_DIM ** -0.5

    tiles = [(sb, h) for sb in range(qb) for h in range(HEADS_PER_GROUP)]

    def body(q_ref, kc_ref, vc_ref, kp_ref, vp_ref, da_ref, c_ref, lt_ref, d_ref, dk_acc, dv_acc, car_k, car_v,
             k_all, v_all, s_scr, dp_scr, p_scr, ds_scr):
        tg = pl.program_id(1)
        t = steps - 1 - tg

        @pl.when(tg == 0)
        def _():
            car_k[...] = jnp.zeros_like(car_k)
            car_v[...] = jnp.zeros_like(car_v)

        k_all[0:BAND, :] = kp_ref[...]
        k_all[BAND:, :] = kc_ref[...]
        v_all[0:BAND, :] = vp_ref[...]
        v_all[BAND:, :] = vc_ref[...]
        zero = jnp.zeros((rows, GROUP_W), F32)
        dk_acc[0:rows, :] = zero
        dv_acc[0:rows, :] = zero
        dk_acc[rows:rows + BAND, :] = car_k[...]
        dv_acc[rows:rows + BAND, :] = car_v[...]
        band, band_first = _band_masks_2(t)
        for idx, (sb, h) in enumerate(tiles):
            cs = slice(h * HEAD_DIM, (h + 1) * HEAD_DIM)
            rs, ks = slice(sb * BAND, (sb + 1) * BAND), slice(sb * BAND, (sb + 2) * BAND)
            s_scr[idx] = _dot(q_ref[rs, cs], k_all[ks, cs], "nt")
            dp_scr[idx] = _dot(da_ref[rs, cs], v_all[ks, cs], "nt")
        for idx, (sb, h) in enumerate(tiles):
            cs = slice(h * HEAD_DIM, (h + 1) * HEAD_DIM)
            rs = slice(sb * BAND, (sb + 1) * BAND)
            ltv = jnp.concatenate([lt_ref[rs, cs]] * 2, axis=1)
            cc = jnp.concatenate([c_ref[rs, cs]] * 2, axis=1)
            p = jnp.exp(jnp.where(band_first if sb == 0 else band, s_scr[idx] * scale - ltv, NEG))
            p_scr[idx] = p.astype(BF16)
            ds_scr[idx] = (p * (dp_scr[idx] - cc) * scale).astype(BF16)
        for idx, (sb, h) in enumerate(tiles):
            cs = slice(h * HEAD_DIM, (h + 1) * HEAD_DIM)
            rs, ks = slice(sb * BAND, (sb + 1) * BAND), slice(sb * BAND, (sb + 2) * BAND)
            d_ref[rs, cs] = _dot(ds_scr[idx], k_all[ks, cs], "nn").astype(BF16)
            dk_acc[ks, cs] += _dot(ds_scr[idx], q_ref[rs, cs], "tn")
            dv_acc[ks, cs] += _dot(p_scr[idx], da_ref[rs, cs], "tn")
        d_ref[:, GROUP_W:2 * GROUP_W] = dk_acc[BAND:rows + BAND, :].astype(BF16)
        d_ref[:, 2 * GROUP_W:3 * GROUP_W] = dv_acc[BAND:rows + BAND, :].astype(BF16)
        car_k[...] = dk_acc[0:BAND, :]
        car_v[...] = dv_acc[0:BAND, :]

    def rev(tg):
        return steps - 1 - tg

    def prev(tg):
        return jnp.maximum(qb * rev(tg) - 1, 0)

    cur = lambda c: pl.BlockSpec((None, rows, GROUP_W), lambda r, tg, c=c: (r, rev(tg), c))
    prv = lambda c: pl.BlockSpec((None, BAND, GROUP_W), lambda r, tg, c=c: (r, prev(tg), c))
    return _pcall(
        body, name, (dil, steps), [a_g, a_g, a_g, a_g, a_g, da_g, c_g, lt_g],
        [cur(0), cur(1), cur(2), prv(1), prv(2), cur(0), cur(0), cur(0)],
        [jax.ShapeDtypeStruct((dil, m_len, ATTN_W), BF16)], [pl.BlockSpec((None, rows, ATTN_W), lambda r, tg: (r, rev(tg), 0))],
        [pltpu.VMEM((rows + BAND, GROUP_W), F32), pltpu.VMEM((rows + BAND, GROUP_W), F32),
         pltpu.VMEM((BAND, GROUP_W), F32), pltpu.VMEM((BAND, GROUP_W), F32),
         pltpu.VMEM((rows + BAND, GROUP_W), BF16), pltpu.VMEM((rows + BAND, GROUP_W), BF16),
         pltpu.VMEM((len(tiles), BAND, 2 * BAND), F32), pltpu.VMEM((len(tiles), BAND, 2 * BAND), F32),
         pltpu.VMEM((len(tiles), BAND, 2 * BAND), BF16), pltpu.VMEM((len(tiles), BAND, 2 * BAND), BF16)], comm=comm)


def _dqkv_post(d_gs, tabs, dz):
    seq = dz.shape[0]

    def body(g0, g1, g2, c_ref, s1_ref, s2_ref, dz_any, o_ref, sc):
        del dz_any
        ins = (g0, g1, g2)
        c, s1, s2 = c_ref[...], s1_ref[...], s2_ref[...]
        for part in range(3):
            for hh in range(N_GROUPS * HEADS_PER_GROUP):
                g, hl = divmod(hh, HEADS_PER_GROUP)
                icol = part * GROUP_W + hl * HEAD_DIM
                ocol = part * ATTN_W + hh * HEAD_DIM
                d = DILATIONS[g]
                if d == 1:
                    x = ins[g][0, :, icol:icol + HEAD_DIM].astype(F32)
                else:
                    for r in range(d):
                        sc[pl.ds(r, ROW_BLK // d, stride=d), :] = ins[g][r, :, icol:icol + HEAD_DIM].astype(F32)
                    x = sc[...]
                if part < 2:
                    x = x * c + pltpu.roll(x * s1, HEAD_DIM - ROT_DIM // 2, 1) + pltpu.roll(x * s2, ROT_DIM // 2, 1)
                o_ref[:, ocol:ocol + HEAD_DIM] = x.astype(BF16)

    tab_spec = pl.BlockSpec((ROW_BLK, HEAD_DIM), lambda i: (i, 0))
    return pl.pallas_call(
        body, name="dqkv_post", out_shape=jax.ShapeDtypeStruct(dz.shape, BF16), grid=(seq // ROW_BLK,),
        in_specs=_group_specs(ATTN_W) + [tab_spec, tab_spec, tab_spec, pl.BlockSpec(memory_space=pl.ANY)],
        out_specs=pl.BlockSpec((ROW_BLK, QKV_W), lambda i: (i, 0)),
        scratch_shapes=[pltpu.VMEM((ROW_BLK, HEAD_DIM), F32)], input_output_aliases={6: 0},
        compiler_params=_params(dimension_semantics=("arbitrary",)),
    )(*d_gs, *tabs, dz)


def _glu(zg):
    a = zg[:, :CONV_CH].astype(F32)
    s = _sigmoid(zg[:, CONV_CH:].astype(F32))
    return a, s, a * s


def _shifted_copies(xs):
    n = xs.shape[1] - SUBLANES
    for b in range(1, SUBLANES):
        xs[b, 0:n, :] = xs[0, pl.ds(b, n), :]


def _shifted(xs, offset, r0, cs):
    a, b = divmod(offset, SUBLANES)
    return xs[b, pl.ds(SUBLANES * a + r0, CONV_ROWS), cs]


def _conv_fwd(z, cw, cb, lg, lb, comm=_NO_COMM):
    seq = z.shape[0]
    halo_per_blk = ROW_BLK // CONV_HALO

    def body(zg_ref, zh_ref, cw_ref, cb_ref, lg_ref, lb_ref, c2_ref, c4_ref, xs):
        i = pl.program_id(0)
        _, _, c1 = _glu(zg_ref[...])
        _, _, c1h = _glu(zh_ref[...])
        xs[0, 0:CONV_HALO, :] = jnp.where(i > 0, c1h, 0.0)
        xs[0, CONV_HALO:, :] = c1
        _shifted_copies(xs)
        for s in range(CONV_CH // HEAD_DIM):
            cs = slice(s * HEAD_DIM, (s + 1) * HEAD_DIM)
            taps = [cw_ref[j:j + 1, cs] for j in range(CONV_K)]
            bias = cb_ref[:, cs]

            def chunk(rc, carry, cs=cs, taps=taps, bias=bias):
                r0 = pl.multiple_of(rc * CONV_ROWS, CONV_ROWS)
                acc = jnp.zeros((CONV_ROWS, HEAD_DIM), F32)
                for j in range(CONV_K):
                    acc = acc + taps[j] * _shifted(xs, CONV_HALO - (CONV_K - 1) + j, r0, cs)
                c2_ref[pl.ds(r0, CONV_ROWS), cs] = acc + bias
                return carry

            lax.fori_loop(0, ROW_BLK // CONV_ROWS, chunk, 0)
        c2 = c2_ref[...]
        mu = jnp.mean(c2, axis=-1, keepdims=True)
        xc = c2 - mu
        rstd = lax.rsqrt(jnp.mean(xc * xc, axis=-1, keepdims=True) + EPS)
        c3 = xc * rstd * lg_ref[...] + lb_ref[...]
        c4_ref[...] = (c3 * _sigmoid(c3)).astype(BF16)

    vec = pl.BlockSpec((1, CONV_CH), lambda i: (0, 0))
    return _pcall(
        body, "conv_fwd", (seq // ROW_BLK,), [z, z, cw, cb, lg, lb],
        [pl.BlockSpec((ROW_BLK, 2 * CONV_CH), lambda i: (i, GLU_COL_BLK)),
         pl.BlockSpec((CONV_HALO, 2 * CONV_CH), lambda i: (jnp.maximum(i * halo_per_blk - 1, 0), GLU_COL_BLK)),
         pl.BlockSpec((CONV_HALO, CONV_CH), lambda i: (0, 0)), vec, vec, vec],
        [jax.ShapeDtypeStruct((seq, CONV_CH), F32), jax.ShapeDtypeStruct((seq, CONV_CH), BF16)],
        [pl.BlockSpec((ROW_BLK, CONV_CH), lambda i: (i, 0)), pl.BlockSpec((ROW_BLK, CONV_CH), lambda i: (i, 0))],
        [pltpu.VMEM((SUBLANES, ROW_BLK + CONV_HALO, CONV_CH), F32)], comm=comm)


def _conv_bwd(dc2, z, cw, dz, comm=_NO_COMM):
    seq = z.shape[0]
    halo_per_blk = ROW_BLK // CONV_HALO
    n_blk = seq // ROW_BLK
    last_halo = seq // CONV_HALO - 1

    def body(dc_ref, dn_ref, zg_ref, zh_ref, cw_ref, dz_any, o_ref, dcw_ref, xs, ys, dc1_ref, dcw_acc):
        del dz_any
        i = pl.program_id(0)
        a, s, c1 = _glu(zg_ref[...])
        _, _, c1h = _glu(zh_ref[...])
        xs[0, 0:CONV_HALO, :] = jnp.where(i > 0, c1h, 0.0)
        xs[0, CONV_HALO:, :] = c1
        ys[0, 0:ROW_BLK, :] = dc_ref[...]
        ys[0, ROW_BLK:, :] = jnp.where(i < n_blk - 1, dn_ref[...], 0.0)
        _shifted_copies(xs)
        _shifted_copies(ys)

        @pl.when(i == 0)
        def _():
            dcw_acc[...] = jnp.zeros_like(dcw_acc)

        for sl in range(CONV_CH // HEAD_DIM):
            cs = slice(sl * HEAD_DIM, (sl + 1) * HEAD_DIM)
            taps = [cw_ref[j:j + 1, cs] for j in range(CONV_K)]

            def chunk(rc, carry, cs=cs, taps=taps):
                r0 = pl.multiple_of(rc * CONV_ROWS, CONV_ROWS)
                dc = ys[0, pl.ds(r0, CONV_ROWS), cs]
                acc = jnp.zeros((CONV_ROWS, HEAD_DIM), F32)
                for j in range(CONV_K):
                    prod = dc * _shifted(xs, CONV_HALO - (CONV_K - 1) + j, r0, cs)
                    dcw_acc[j, :, cs] += jnp.sum(prod.reshape(CONV_ROWS // SUBLANES, SUBLANES, HEAD_DIM), axis=0)
                    acc = acc + taps[j] * _shifted(ys, CONV_K - 1 - j, r0, cs)
                dc1_ref[pl.ds(r0, CONV_ROWS), cs] = acc
                return carry

            lax.fori_loop(0, ROW_BLK // CONV_ROWS, chunk, 0)
        dc1 = dc1_ref[...]
        o_ref[:, :CONV_CH] = (dc1 * s).astype(BF16)
        o_ref[:, CONV_CH:] = (dc1 * a * s * (1.0 - s)).astype(BF16)

        @pl.when(i == n_blk - 1)
        def _():
            dcw_ref[...] = jnp.sum(dcw_acc[...], axis=1)

    return _pcall(
        body, "conv_bwd", (n_blk,), [dc2, dc2, z, z, cw, dz],
        [pl.BlockSpec((ROW_BLK, CONV_CH), lambda i: (i, 0)),
         pl.BlockSpec((CONV_HALO, CONV_CH), lambda i: (jnp.minimum((i + 1) * halo_per_blk, last_halo), 0)),
         pl.BlockSpec((ROW_BLK, 2 * CONV_CH), lambda i: (i, GLU_COL_BLK)),
         pl.BlockSpec((CONV_HALO, 2 * CONV_CH), lambda i: (jnp.maximum(i * halo_per_blk - 1, 0), GLU_COL_BLK)),
         pl.BlockSpec((CONV_HALO, CONV_CH), lambda i: (0, 0)),
         pl.BlockSpec(memory_space=pl.ANY)],
        [jax.ShapeDtypeStruct(dz.shape, BF16), jax.ShapeDtypeStruct((CONV_HALO, CONV_CH), F32)],
        [pl.BlockSpec((ROW_BLK, 2 * CONV_CH), lambda i: (i, GLU_COL_BLK)), pl.BlockSpec((CONV_HALO, CONV_CH), lambda i: (0, 0))],
        [pltpu.VMEM((SUBLANES, ROW_BLK + CONV_HALO, CONV_CH), F32), pltpu.VMEM((SUBLANES, ROW_BLK + CONV_HALO, CONV_CH), F32),
         pltpu.VMEM((ROW_BLK, CONV_CH), F32), pltpu.VMEM((CONV_HALO, SUBLANES, CONV_CH), F32)],
        aliases={5: 0}, comm=comm)


def _epi_mix(ya, c4, wcp, gates, bg):
    yc = _dot(c4, wcp, "nn")
    gv = _sigmoid(gates.astype(F32) + bg)
    merged = gv[:, :D_MODEL] * ya + gv[:, D_MODEL:] * yc
    return merged, ya, yc


def _epi_residual_rms(acc, xres, g):
    x = xres + acc
    return x, _rms_fwd_vals(x, g)


def _cross_scores(cq, ck):
    out = []
    for h in range(CROSS_HEADS):
        cs = slice(h * CROSS_HD, (h + 1) * CROSS_HD)
        s = _dot(cq[:, cs], ck[:, cs], "nt") * (CROSS_HD ** -0.5)
        e = jnp.exp(s - jnp.max(s, axis=-1, keepdims=True))
        out.append((cs, e, jnp.sum(e, axis=-1, keepdims=True)))
    return out


def _epi_cross_fwd(acc, ck, cv):
    cq = acc.astype(BF16)
    co = [_dot(e, cv[:, cs], "nn") / den for cs, e, den in _cross_scores(cq, ck)]
    return cq, jnp.concatenate(co, axis=1)


def _epi_cross_bwd(dco, cq, ck, cv):
    dco = dco.astype(BF16)
    dcq, dck, dcv = [], [], []
    for cs, e, den in _cross_scores(cq, ck):
        p = e / den
        dp = _dot(dco[:, cs], cv[:, cs], "nt")
        ds = (p * (dp - jnp.sum(dp * p, axis=-1, keepdims=True)) * (CROSS_HD ** -0.5)).astype(BF16)
        dcq.append(_dot(ds, ck[:, cs], "nn"))
        dck.append(_dot(ds, cq[:, cs], "tn"))
        dcv.append(_dot(p, dco[:, cs], "tn"))
    return jnp.concatenate(dcq, axis=1), jnp.concatenate(dck, axis=1), jnp.concatenate(dcv, axis=1)


def _epi_mlp_up(acc):
    return acc, jnp.square(jnp.maximum(acc, 0.0))


def _epi_final(acc, x2, tgt, g):
    x3 = x2 + acc
    err = _rms_fwd_vals(x3, g) - tgt
    loss = (0.5 / D_MODEL) * jnp.sum(err * err)
    dx3, dg = _rms_bwd_vals(x3, g, err * (1.0 / D_MODEL))
    return dx3, dx3, jnp.full((1, HEAD_DIM), loss, F32), dg


def _epi_mlp_down_bwd(dh, hpre):
    return (dh * 2.0 * jnp.maximum(hpre.astype(F32), 0.0),)


def _epi_rms_bwd(du, x, g, dres):
    dx, dg = _rms_bwd_vals(x, g, du)
    return dres + dx, dg


def _epi_rms_bwd_2(du, x, g, dres):
    dx, dg = _epi_rms_bwd(du, x, g, dres)
    return dx, dx, dg


def _epi_rms_bwd_g(du, x, g):
    return (_rms_bwd_vals(x, g, du)[1],)


def _epi_mix_bwd(dm, ya, yc, gates, bg):
    gv = _sigmoid(gates.astype(F32) + bg)
    ga, gb = gv[:, :D_MODEL], gv[:, D_MODEL:]
    ya, yc = ya.astype(F32), yc.astype(F32)
    dgate = jnp.concatenate([dm * ya * ga * (1.0 - ga), dm * yc * gb * (1.0 - gb)], axis=1)
    return dm * ga, dm * gb, dgate, jnp.sum(dgate, axis=0, keepdims=True)


def _epi_ln_bwd(dc4, c2, lg, lb):
    mu = jnp.mean(c2, axis=-1, keepdims=True)
    xc = c2 - mu
    rstd = lax.rsqrt(jnp.mean(xc * xc, axis=-1, keepdims=True) + EPS)
    xh = xc * rstd
    c3 = xh * lg + lb
    sg = _sigmoid(c3)
    dc3 = dc4 * sg * (1.0 + c3 * (1.0 - sg))
    dxh = dc3 * lg
    dc2 = rstd * (dxh - jnp.mean(dxh, axis=-1, keepdims=True) - xh * jnp.mean(dxh * xh, axis=-1, keepdims=True))
    return (dc2, jnp.sum(dc3 * xh, axis=0, keepdims=True), jnp.sum(dc3, axis=0, keepdims=True),
            jnp.sum(dc2, axis=0, keepdims=True))


def _sds(shape, dtype):
    return jax.ShapeDtypeStruct(shape, dtype)


class _Lazy:
    def __init__(self, fn):
        self.fn = fn

    def __getitem__(self, key):
        return self.fn(key)


def _local_step(x, mem, tgt, sm, plan):
    w = _Lazy(plan.w)
    dw = {}

    def carry(name, n_own, fn, *args, **kw):
        c = plan.comm(name, dw)
        res = fn(*args, comm=c, **kw)
        plan.done(name, res[n_own:])
        return res[:n_own]

    def mm(name, *args, **kw):
        return carry(name, len(args[6]), _mm, name, *args, **kw)

    seq = x.shape[0]
    nr = seq // ROW_BLK
    big = min(1024, seq)
    nb = seq // big
    row = lambda n: ((ROW_BLK, n), lambda i, j, k: (i, 0))
    vec = lambda n: ((1, n), lambda i, j, k: (0, 0))
    full = lambda r, c: ((r, c), lambda i, j, k: (0, 0))
    gates_blk = ((ROW_BLK, 2 * D_MODEL), lambda i, j, k: (i, GATE_COL_BLK))
    tabs = _rope_tables(seq)

    u = _rms_fwd("rms_mix", x, sm["g_mix"], ROW_BLK)
    whole3 = lambda a: (a.shape, lambda i, j, k: (0, 0, 0))
    z = mm("in_proj", u, w["w_in"], "nn", (nr, 1, 1), row(D_MODEL), whole3(w["w_in"]),
           [(_sds((seq, IN_W), BF16), *row(IN_W))], split=("cols", N_DEV), b_resident=True)[0]
    a_gs = carry("qkv_prep", 3, _qkv_prep, z, tabs)
    os_, ls_ = [], []
    for g in range(N_GROUPS):
        name = "attn_fwd_%d" % g
        o_g, l_g = carry(name, 2, _attn_fwd, name, a_gs[g])
        os_.append(o_g)
        ls_.append(l_g)
    attn, lt = _attn_merge(os_, ls_, seq)
    c2, c4 = carry("conv_fwd", 2, _conv_fwd, z, w["taps"], sm["conv_b"], sm["conv_ln_g"], sm["conv_ln_b"])
    merged, ya, yc = mm(
        "mix", attn, w["w_attn_proj"], "nn", (nr, 1, 1), row(GROUP_W), full(GROUP_W, D_MODEL),
        [(_sds((seq, D_MODEL), BF16), *row(D_MODEL))] * 3,
        extras=[(c4, *row(CONV_CH)), (w["w_conv_proj"], *full(CONV_CH, D_MODEL)), (z, *gates_blk), (sm["b_gate"], *vec(2 * D_MODEL))],
        epi=_epi_mix)
    x1, uq = mm("out_proj", merged, w["w_out"], "nn", (nr, 1, 1), row(D_MODEL), full(D_MODEL, D_MODEL),
                 [(_sds((seq, D_MODEL), F32), *row(D_MODEL)), (_sds((seq, D_MODEL), BF16), *row(D_MODEL))],
                 extras=[(x, *row(D_MODEL)), (sm["g_cross"], *vec(D_MODEL))], epi=_epi_residual_rms)

    mn = _rms_fwd("rms_mem", mem, sm["g_mem"], N_MEM)
    ckv = mm("ckv_proj", mn, w["w_ckv"], "nn", (1, N_DEV, 1), full(N_MEM, D_MODEL),
              ((None, D_MODEL, 2 * D_MODEL // N_DEV), lambda i, j, k: (j, 0, 0)),
              [(_sds((N_MEM, 2 * D_MODEL), BF16), (N_MEM, 2 * D_MODEL // N_DEV), lambda i, j, k: (0, j))])[0]
    ck, cv = ckv[:, :D_MODEL], ckv[:, D_MODEL:]
    kv_blk = full(N_MEM, D_MODEL)
    cq, co = mm("cq_proj_cross", uq, w["w_cq"], "nn", (nr, 1, 1), row(D_MODEL), full(D_MODEL, D_MODEL),
                 [(_sds((seq, D_MODEL), BF16), *row(D_MODEL))] * 2,
                 extras=[(ck, *kv_blk), (cv, *kv_blk)], epi=_epi_cross_fwd)
    x2, um = mm("co_proj", co, w["w_co"], "nn", (nr, 1, 1), row(D_MODEL), full(D_MODEL, D_MODEL),
                 [(_sds((seq, D_MODEL), F32), *row(D_MODEL)), (_sds((seq, D_MODEL), BF16), *row(D_MODEL))],
                 extras=[(x1, *row(D_MODEL)), (sm["g_mlp"], *vec(D_MODEL))], epi=_epi_residual_rms)

    ff_blk = D_FF // N_DEV
    row_f32 = (_sds((seq, D_MODEL), F32), *row(D_MODEL))
    row_bf16 = (_sds((seq, D_MODEL), BF16), *row(D_MODEL))
    col_sum = (_sds((1, D_MODEL), F32), *vec(D_MODEL))
    hpre, h = mm("mlp_up", um, w["w_up"], "nn", (nr, 1, 1), row(D_MODEL), whole3(w["w_up"]),
                 [(_sds((seq, D_FF), BF16), *row(D_FF))] * 2, epi=_epi_mlp_up, split=("cols", N_DEV), b_resident=True)
    kt = D_FF // D_MODEL
    dx3, dx3b, loss, dg_final = mm(
        "mlp_down_loss", h, w["w_down"], "nn", (nr, 1, 1), row(D_FF), full(D_FF, D_MODEL),
        [row_f32, row_bf16, (_sds((1, HEAD_DIM), F32), *vec(HEAD_DIM)), col_sum],
        extras=[(x2, *row(D_MODEL)), (tgt, *row(D_MODEL)), (sm["g_final"], *vec(D_MODEL))], epi=_epi_final, acc_outs=(2, 3),
        b_resident=True)

    dhpre = mm("mlp_down_bwd", dx3b, w["w_down"], "nt", (nr, 1, 1), row(D_MODEL), full(D_FF, D_MODEL),
               [(_sds((seq, D_FF), BF16), *row(D_FF))], extras=[(hpre, *row(D_FF))], epi=_epi_mlp_down_bwd,
               split=("cols", kt), b_resident=True)[0]
    big2 = min(2 * big, seq)
    nb2 = seq // big2
    dw["w_down"] = mm("dw_down", h, dx3b, "tn", (kt, 1, nb2), ((big2, D_MODEL), lambda i, j, k: (k, i)),
                      ((big2, D_MODEL), lambda i, j, k: (k, 0)),
                      [(_sds((D_FF, D_MODEL), BF16), (D_MODEL, D_MODEL), lambda i, j, k: (i, 0))])[0]
    dx2, dx2b, dg_mlp = mm("mlp_up_bwd", dhpre, w["w_up"], "nt", (nr, 1, 1), row(D_FF), whole3(w["w_up"]),
                           [row_f32, row_bf16, col_sum],
                           extras=[(x2, *row(D_MODEL)), (sm["g_mlp"], *vec(D_MODEL)), (dx3, *row(D_MODEL))],
                           epi=_epi_rms_bwd_2, acc_outs=(2,), split=("sum", N_DEV), b_resident=True)
    dw["w_up"] = mm("dw_up", um, dhpre, "tn", (1, N_DEV, nb2), ((big2, D_MODEL), lambda i, j, k: (k, 0)),
                    ((big2, ff_blk), lambda i, j, k: (k, j)),
                    [(_sds((N_DEV, D_MODEL, ff_blk), BF16), (None, D_MODEL, ff_blk), lambda i, j, k: (j, 0, 0))])[0]

    acc_kv = (_sds((N_MEM, D_MODEL), F32), *kv_blk)
    dcq, dck, dcv = mm("co_proj_bwd_cross", dx2b, w["w_co"], "nt", (nr, 1, 1), row(D_MODEL), full(D_MODEL, D_MODEL),
                       [row_bf16, acc_kv, acc_kv],
                       extras=[(cq, *row(D_MODEL)), (ck, *kv_blk), (cv, *kv_blk)], epi=_epi_cross_bwd, acc_outs=(1, 2))

    def dw_square(name, act, grad):
        return mm(name, act, grad, "tn", (1, 1, nb2), ((big2, D_MODEL), lambda i, j, k: (k, 0)),
                  ((big2, D_MODEL), lambda i, j, k: (k, 0)), [(_sds((D_MODEL, D_MODEL), BF16), *full(D_MODEL, D_MODEL))])[0]

    dw["w_co"] = dw_square("dw_co", co, dx2b)
    dx1, dx1b, dg_cross = mm("cq_proj_bwd", dcq, w["w_cq"], "nt", (nr, 1, 1), row(D_MODEL), full(D_MODEL, D_MODEL),
                             [row_f32, row_bf16, col_sum],
                             extras=[(x1, *row(D_MODEL)), (sm["g_cross"], *vec(D_MODEL)), (dx2, *row(D_MODEL))],
                             epi=_epi_rms_bwd_2, acc_outs=(2,))
    dw["w_cq"] = dw_square("dw_cq", uq, dcq)
    dckv = jnp.concatenate([dck, dcv], axis=1)
    kv_chunk = 2 * D_MODEL // N_DEV
    dw["w_ckv"] = mm("dw_ckv", mn, dckv, "tn", (1, N_DEV, 1), full(N_MEM, D_MODEL), ((N_MEM, kv_chunk), lambda i, j, k: (0, j)),
                      [(_sds((N_DEV, D_MODEL, kv_chunk), BF16), (None, D_MODEL, kv_chunk), lambda i, j, k: (j, 0, 0))])[0]
    dg_mem = mm("ckv_proj_bwd", dckv, w["w_ckv"], "nt", (1, 1, N_DEV), ((N_MEM, kv_chunk), lambda i, j, k: (0, k)),
                 ((None, D_MODEL, kv_chunk), lambda i, j, k: (k, 0, 0)), [(_sds((1, D_MODEL), F32), *vec(D_MODEL))],
                 extras=[(mem, *full(N_MEM, D_MODEL)), (sm["g_mem"], *vec(D_MODEL))], epi=_epi_rms_bwd_g, acc_outs=(0,))[0]

    dya, dyc, dz, db_gate = mm(
        "out_proj_bwd_mix", dx1b, w["w_out"], "nt", (nr, 1, 1), row(D_MODEL), full(D_MODEL, D_MODEL),
        [(_sds((seq, D_MODEL), BF16), *row(D_MODEL)), (_sds((seq, D_MODEL), BF16), *row(D_MODEL)),
         (_sds((seq, IN_W), BF16), *gates_blk), (_sds((1, 2 * D_MODEL), F32), *vec(2 * D_MODEL))],
        extras=[(ya, *row(D_MODEL)), (yc, *row(D_MODEL)), (z, *gates_blk), (sm["b_gate"], *vec(2 * D_MODEL))],
        epi=_epi_mix_bwd, acc_outs=(3,))
    dw["w_out"] = dw_square("dw_out", merged, dx1b)
    dattn = mm("attn_proj_bwd", dya, w["w_attn_proj"], "nt", (nr, 1, 1), row(D_MODEL), full(GROUP_W, D_MODEL),
                [(_sds((seq, GROUP_W), BF16), *row(GROUP_W))])[0]
    pc = D_MODEL // N_DEV
    dw["w_attn_proj"] = mm("dw_attn_proj", attn, dya, "tn", (1, 1, nb2), ((big2, GROUP_W), lambda i, j, k: (k, 0)),
                           ((big2, D_MODEL), lambda i, j, k: (k, 0)),
                           [(_sds((N_DEV, GROUP_W, pc), BF16), (N_DEV, GROUP_W, pc), lambda i, j, k: (0, 0, 0))],
                           out_chunks=N_DEV)[0]
    cvec = (_sds((1, CONV_CH), F32), *vec(CONV_CH))
    dc2, dg_ln_g, dg_ln_b, dg_conv_b = mm(
        "conv_proj_bwd_ln", dyc, w["w_conv_proj"], "nt", (nr, 1, 1), row(D_MODEL), full(CONV_CH, D_MODEL),
        [(_sds((seq, CONV_CH), F32), *row(CONV_CH)), cvec, cvec, cvec],
        extras=[(c2, *row(CONV_CH)), (sm["conv_ln_g"], *vec(CONV_CH)), (sm["conv_ln_b"], *vec(CONV_CH))],
        epi=_epi_ln_bwd, acc_outs=(1, 2, 3))
    dw["w_conv_proj"] = mm("dw_conv_proj", c4, dyc, "tn", (1, 1, nb2), ((big2, CONV_CH), lambda i, j, k: (k, 0)),
                           ((big2, D_MODEL), lambda i, j, k: (k, 0)),
                           [(_sds((N_DEV, CONV_CH, pc), BF16), (N_DEV, CONV_CH, pc), lambda i, j, k: (0, 0, 0))],
                           out_chunks=N_DEV)[0]
    dz, dg_conv_w = carry("conv_bwd", 2, _conv_bwd, dc2, z, w["taps"], dz)
    preps = _attn_bwd_prep(dattn, attn, lt)
    d_gs = []
    for g in range(N_GROUPS):
        name = "attn_bwd_%d" % g
        d_gs.append(carry(name, 1, _attn_bwd, name, a_gs[g], *preps[g])[0])
    dz = _dqkv_post(d_gs, tabs, dz)
    dw["w_in"] = mm("dw_in", u, dz, "tn", (1, N_DEV, nb2), ((big2, D_MODEL), lambda i, j, k: (k, 0)),
                    ((big2, D_MODEL), lambda i, j, k: (k, j)),
                    [(_sds((N_DEV, D_MODEL, D_MODEL), BF16), (None, D_MODEL, D_MODEL), lambda i, j, k: (j, 0, 0))])[0]
    grad_x, dg_mix = mm("in_proj_bwd", dz, w["w_in"], "nt", (nr, 1, 1), row(IN_W), whole3(w["w_in"]), [row_f32, col_sum],
                        extras=[(x, *row(D_MODEL)), (sm["g_mix"], *vec(D_MODEL)), (dx1, *row(D_MODEL))],
                        epi=_epi_rms_bwd, acc_outs=(1,), split=("sum", N_DEV), b_resident=True)
    small = dict(g_mix=dg_mix, b_gate=db_gate, conv_b=dg_conv_b, conv_ln_g=dg_ln_g, conv_ln_b=dg_ln_b, g_cross=dg_cross,
                 g_mem=dg_mem, g_mlp=dg_mlp, g_final=dg_final, loss=loss, conv_w=dg_conv_w)
    return grad_x, dw, small


def _exchange(name, comm):
    return _pcall(lambda: None, name, (1,), [], [], [], [], comm=comm)


SHARD_SHAPE = dict(w_in=(1024, 1024), w_attn_proj=(512, 128), w_conv_proj=(768, 128), w_out=(128, 1024), w_cq=(128, 1024),
                   w_ckv=(1024, 256), w_co=(128, 1024), w_up=(1024, 512), w_down=(512, 1024))
FWD_CARRY = {"in_proj": ("w_attn_proj", "w_conv_proj", "w_out", "w_cq", "w_ckv", "w_co", "taps"),
             "qkv_prep": ("w_up",), "conv_fwd": ("w_down",)}
BWD_CARRY = {"dw_up": ("w_down",), "out_proj_bwd_mix": ("w_co", "w_cq"), "conv_bwd": ("w_up", "w_ckv"),
             "attn_bwd_0": ("w_out",), "attn_bwd_1": ("w_attn_proj", "w_conv_proj"), "in_proj_bwd": ("w_in",)}


def _cols_to_2d(a):
    return a.transpose(1, 0, 2).reshape(a.shape[1], -1)


class _Plan:
    def __init__(self, shards, w_in_gathered, n_tap_cols):
        self.shards = shards
        self.gathered = {"w_in": w_in_gathered}
        self.parts = {}
        self.n_tap_cols = n_tap_cols

    def comm(self, name, dw):
        if name in FWD_CARRY:
            return _Comm(replicated=[self.shards[k] for k in FWD_CARRY[name]])
        if name in BWD_CARRY:
            return _Comm(chunked=[dw[k].reshape((N_DEV,) + SHARD_SHAPE[k]) for k in BWD_CARRY[name]])
        return _NO_COMM

    def done(self, name, got):
        if name in FWD_CARRY:
            self.gathered.update(zip(FWD_CARRY[name], got))
        elif name in BWD_CARRY:
            self.parts.update(zip(BWD_CARRY[name], got))

    def w(self, key):
        g = self.gathered[key]
        if key in ("w_in", "w_up", "w_ckv"):
            return g
        if key in ("w_attn_proj", "w_conv_proj"):
            return _cols_to_2d(g)
        if key == "taps":
            return jnp.pad(_cols_to_2d(g[:, :CONV_K, :self.n_tap_cols]), ((0, 1), (0, 0)))
        return g.reshape(-1, g.shape[-1])


def _adamw(name, w, m, v, parts):
    rows, cols = w.shape
    n_parts = parts.shape[0]
    rb = rows if rows <= 256 or rows % 256 else 256

    def body(w_ref, m_ref, v_ref, p_ref, g_ref, d_ref, nm_ref, nv_ref):
        g = p_ref[0].astype(F32)
        for q in range(1, n_parts):
            g = g + p_ref[q].astype(F32)
        wv = w_ref[...]
        nm = ADAM_B1 * m_ref[...] + (1.0 - ADAM_B1) * g
        nv = ADAM_B2 * v_ref[...] + (1.0 - ADAM_B2) * jnp.square(g)
        m_hat = nm / (1.0 - ADAM_B1 ** ADAM_STEP)
        v_hat = nv / (1.0 - ADAM_B2 ** ADAM_STEP)
        g_ref[...] = g
        d_ref[...] = -ADAM_LR * (m_hat / (jnp.sqrt(v_hat) + ADAM_EPS) + ADAM_WD * wv)
        nm_ref[...] = nm
        nv_ref[...] = nv

    blk = pl.BlockSpec((rb, cols), lambda i: (i, 0))
    return pl.pallas_call(
        body, name=name, out_shape=[jax.ShapeDtypeStruct((rows, cols), F32)] * 4, grid=(rows // rb,),
        in_specs=[blk, blk, blk, pl.BlockSpec((n_parts, rb, cols), lambda i: (0, i, 0))], out_specs=[blk] * 4,
        compiler_params=_params(dimension_semantics=("arbitrary",)),
    )(w, m, v, parts)


BIG = ("w_in", "w_attn_proj", "w_conv_proj", "w_out", "w_cq", "w_ckv", "w_co", "w_up", "w_down")
SMALL = ("g_mix", "b_gate", "conv_b", "conv_ln_g", "conv_ln_b", "g_cross", "g_mem", "g_mlp", "g_final")
PACK_ORDER = SMALL + ("loss", "conv_w")
PACK_ROWS = 272
WEIGHTS = ("g_mix", "w_in", "b_gate", "conv_w", "conv_b", "conv_ln_g", "conv_ln_b", "w_attn_proj", "w_conv_proj", "w_out",
           "g_cross", "g_mem", "w_cq", "w_ckv", "w_co", "g_mlp", "w_up", "w_down", "g_final")


def _pack(d):
    flat = jnp.concatenate([d[k].reshape(-1) for k in PACK_ORDER])
    return jnp.pad(flat, (0, PACK_ROWS * HEAD_DIM - flat.shape[0])).reshape(PACK_ROWS, HEAD_DIM)


def _unpack(p, sizes):
    flat, out, off = p.reshape(-1), {}, 0
    for k in PACK_ORDER:
        out[k] = flat[off:off + sizes[k]]
        off += sizes[k]
    return out


def kernel(x, mem, g_mix, w_in, b_gate, conv_w, conv_b, conv_ln_g, conv_ln_b, w_attn_proj, w_conv_proj, w_out, g_cross, g_mem, w_cq, w_ckv, w_co, g_mlp, w_up, w_down, g_final, loss_target, m_g_mix, m_w_in, m_b_gate, m_conv_w, m_conv_b, m_conv_ln_g, m_conv_ln_b, m_w_attn_proj, m_w_conv_proj, m_w_out, m_g_cross, m_g_mem, m_w_cq, m_w_ckv, m_w_co, m_g_mlp, m_w_up, m_w_down, m_g_final, v_g_mix, v_w_in, v_b_gate, v_conv_w, v_conv_b, v_conv_ln_g, v_conv_ln_b, v_w_attn_proj, v_w_conv_proj, v_w_out, v_g_cross, v_g_mem, v_w_cq, v_w_ckv, v_w_co, v_g_mlp, v_w_up, v_w_down, v_g_final):
    args = dict(locals())
    wts = {k: args[k] for k in WEIGHTS}
    mom = {k: args["m_" + k] for k in WEIGHTS}
    var = {k: args["v_" + k] for k in WEIGHTS}
    two_d = lambda a: a.reshape(a.shape[-2:]) if a.ndim == 3 else a.reshape(1, -1)

    shards = {k: two_d(wts[k]).astype(BF16) for k in BIG}
    shards["taps"] = jnp.pad(two_d(conv_w), ((0, 1), (0, HEAD_DIM - conv_w.shape[-1])))
    plan = _Plan(shards, _exchange("gather_w_in", _Comm(replicated=[shards["w_in"]]))[0], conv_w.shape[-1])
    sm = {k: two_d(wts[k]) for k in SMALL}

    grad_x, _, small = _local_step(x[0], mem[0], loss_target[0], sm, plan)
    parts = plan.parts
    small_parts = _exchange("exchange_small", _Comm(replicated=[_pack(small)]))[0]

    out = {}
    for k in BIG:
        res = _adamw("adamw_" + k, two_d(wts[k]), two_d(mom[k]), two_d(var[k]), parts[k])
        out[k] = [r.reshape(wts[k].shape) for r in res]
    sizes = {k: small[k].size for k in PACK_ORDER}
    zeros = dict(loss=jnp.zeros_like(small["loss"]), conv_w=jnp.zeros_like(small["conv_w"]))
    packed = _adamw("adamw_small", _pack({**{k: wts[k] for k in SMALL}, **zeros}), _pack({**{k: mom[k] for k in SMALL}, **zeros}),
                    _pack({**{k: var[k] for k in SMALL}, **zeros}), small_parts)
    unpacked = [_unpack(p, sizes) for p in packed]
    for k in SMALL:
        out[k] = [u[k].reshape(wts[k].shape) for u in unpacked]
    loss = unpacked[0]["loss"][0]
    me = 4 * lax.axis_index("x") + 2 * lax.axis_index("y") + lax.axis_index("c")
    n_tap_cols = conv_w.shape[-1]
    g_taps = lax.dynamic_slice(unpacked[0]["conv_w"].reshape(CONV_HALO, CONV_CH), (0, me * n_tap_cols), (CONV_K, n_tap_cols))
    res = _adamw("adamw_conv_w", two_d(conv_w), two_d(m_conv_w), two_d(v_conv_w), g_taps[None])
    out["conv_w"] = [r.reshape(conv_w.shape) for r in res]

    return (loss, grad_x[None], *[out[k][0] for k in WEIGHTS], *[out[k][1] for k in WEIGHTS],
            *[out[k][2] for k in WEIGHTS], *[out[k][3] for k in WEIGHTS])
```

```python
import functools

import jax
import jax.numpy as jnp
from jax import lax
from jax.experimental import pallas as pl
from jax.experimental.pallas import tpu as pltpu

F32 = jnp.float32
BF16 = jnp.bfloat16

N_DEV = 8
D_MODEL = 1024
N_MEM = 256
HEAD_DIM = 128
HEADS_PER_GROUP = 4
GROUP_W = HEADS_PER_GROUP * HEAD_DIM
DILATIONS = (1, 4, 16)
BAND = 128
N_GROUPS = 3
ATTN_W = N_GROUPS * GROUP_W
QKV_W = 3 * ATTN_W
ROT_DIM = HEAD_DIM // 4
ROPE_THETA = 500000.0
CONV_CH = 768
CONV_K = 31
CONV_HALO = 32
SUBLANES = 8
CONV_ROWS = 64
IN_W = 8192
GLU_COL_BLK = QKV_W // (2 * CONV_CH)
GATE_COL_BLK = (QKV_W + 2 * CONV_CH) // (2 * D_MODEL)
CROSS_HEADS = 4
CROSS_HD = D_MODEL // CROSS_HEADS
D_FF = 4096
EPS = 1e-6
NEG = -1e30
QB = 4
ROW_BLK = QB * BAND

ADAM_LR = 0.001
ADAM_B1 = 0.9
ADAM_B2 = 0.999
ADAM_EPS = 1e-08
ADAM_WD = 0.01
ADAM_STEP = 10

VMEM_LIMIT = 56 * 1024 * 1024
MESH = pl.DeviceIdType.MESH


def _params(**kw):
    return pltpu.CompilerParams(vmem_limit_bytes=VMEM_LIMIT, **kw)


def _sigmoid(x):
    return 1.0 / (1.0 + jnp.exp(-x))


def _dot(a, b, kind):
    dims = {"nn": (((1,), (0,)), ((), ())), "nt": (((1,), (1,)), ((), ())), "tn": (((0,), (0,)), ((), ()))}[kind]
    if a.dtype != BF16:
        a = a.astype(BF16)
    if b.dtype != BF16:
        b = b.astype(BF16)
    return lax.dot_general(a, b, dims, preferred_element_type=F32)


def _peers():
    x, y, c = lax.axis_index("x"), lax.axis_index("y"), lax.axis_index("c")
    me = 4 * x + 2 * y + c
    peers = [(x, y, 1 - c), (1 - x, y, c), (x, 1 - y, c), (1 - x, 1 - y, c),
             (1 - x, y, 1 - c), (x, 1 - y, 1 - c), (1 - x, 1 - y, 1 - c)]
    return me, peers


class _Comm:
    def __init__(self, chunked=(), replicated=()):
        self.arrays = list(chunked) + list(replicated)
        self.n_c = len(chunked)
        self.n = len(self.arrays)
        self.out_shape = [jax.ShapeDtypeStruct(a.shape, a.dtype) for a in chunked]
        self.out_shape += [jax.ShapeDtypeStruct((N_DEV,) + a.shape, a.dtype) for a in replicated]
        self.in_specs = [pl.BlockSpec(memory_space=pl.ANY)] * self.n
        self.out_specs = [pl.BlockSpec(memory_space=pl.ANY)] * self.n
        self.scratch = [pltpu.SemaphoreType.DMA((self.n,))] * 5 if self.n else []

    @staticmethod
    def _where():
        x, y, c = lax.axis_index("x"), lax.axis_index("y"), lax.axis_index("c")
        chips = [(1 - x, y), (x, 1 - y), (1 - x, 1 - y)]
        return (x, y, c), 4 * x + 2 * y + c, chips

    def _local(self, ins, outs, sems, a, me):
        src = ins[a].at[me] if a < self.n_c else ins[a]
        return pltpu.make_async_copy(src, outs[a].at[me], sems[2].at[a])

    @staticmethod
    def _remote(src, dst, send, recv, to):
        return pltpu.make_async_remote_copy(src_ref=src, dst_ref=dst, send_sem=send, recv_sem=recv, device_id=to,
                                            device_id_type=MESH)

    def start(self, ins, outs, sems):
        (x, y, c), me, chips = self._where()
        for a in range(self.n):
            self._local(ins, outs, sems, a, me).start()
            if a < self.n_c:
                for (px, py, pc) in _peers()[1]:
                    self._remote(ins[a].at[4 * px + 2 * py + pc], outs[a].at[me], sems[0].at[a], sems[1].at[a], (px, py, pc)).start()
            else:
                self._remote(ins[a], outs[a].at[me], sems[3].at[a], sems[4].at[a], (x, y, 1 - c)).start()
                for (px, py) in chips:
                    self._remote(ins[a], outs[a].at[me], sems[0].at[a], sems[1].at[a], (px, py, c)).start()

    def wait(self, ins, outs, sems):
        (x, y, c), me, chips = self._where()
        sibling = (x, y, 1 - c)

        def drain(a, pair, count):
            blocks = outs[a].at[pl.ds(0, count)]
            cp = self._remote(blocks, blocks, sems[pair].at[a], sems[pair + 1].at[a], sibling)
            cp.wait_send()
            cp.wait_recv()

        for a in range(self.n):
            if a < self.n_c:
                drain(a, 0, N_DEV - 1)
            else:
                drain(a, 0, len(chips))
                for (px, py) in chips:
                    blk = outs[a].at[4 * px + 2 * py + c]
                    self._remote(blk, blk, sems[3].at[a], sems[4].at[a], sibling).start()
        for a in range(self.n):
            if a >= self.n_c:
                drain(a, 3, len(chips) + 1)
            self._local(ins, outs, sems, a, me).wait()


_NO_COMM = _Comm()


def _pcall(body, name, grid, operands, in_specs, out_shape, out_specs, scratch=(), aliases=None, comm=_NO_COMM):
    n_in, n_out, n_scr = len(operands), len(out_shape), len(scratch)
    grid = tuple(grid)

    def carried(*refs):
        ins, c_in = refs[:n_in], refs[n_in:n_in + comm.n]
        o0 = n_in + comm.n
        outs, c_out = refs[o0:o0 + n_out], refs[o0 + n_out:o0 + n_out + comm.n]
        s0 = o0 + n_out + comm.n
        scr, sems = refs[s0:s0 + n_scr], refs[s0 + n_scr:]
        ids = [pl.program_id(ax) for ax in range(len(grid))]

        @pl.when(functools.reduce(jnp.logical_and, [p == 0 for p in ids]))
        def _():
            comm.start(c_in, c_out, sems)

        body(*ins, *outs, *scr)

        @pl.when(functools.reduce(jnp.logical_and, [p == g - 1 for p, g in zip(ids, grid)]))
        def _():
            comm.wait(c_in, c_out, sems)

    return pl.pallas_call(
        carried if comm.n else body, name=name, grid=grid, in_specs=list(in_specs) + comm.in_specs,
        out_shape=list(out_shape) + comm.out_shape, out_specs=list(out_specs) + comm.out_specs,
        scratch_shapes=list(scratch) + comm.scratch, input_output_aliases=aliases or {},
        compiler_params=_params(dimension_semantics=("arbitrary",) * len(grid)),
    )(*operands, *comm.arrays)


def _mm(name, a, b, kind, grid, a_blk, b_blk, outs, extras=(), epi=None, acc_outs=(), j_outer=False, comm=_NO_COMM,
        split=None, b_resident=False, out_chunks=0):
    gi, gj, gk = grid
    n_ex = len(extras)
    n_out = len(outs)
    mode, n_chunks = split if split is not None else (None, 1)

    def spec(blk, fn, **kw):
        return pl.BlockSpec(blk, (lambda j, i, k: fn(i, j, k)) if j_outer else fn, **kw)

    def b_chunk(b_ref, c):
        if len(b_ref.shape) == 3:
            return b_ref[c]
        rows, cols = b_ref.shape
        if (kind == "nn") == (mode == "cols"):
            return b_ref[:, c * (cols // n_chunks):(c + 1) * (cols // n_chunks)]
        return b_ref[c * (rows // n_chunks):(c + 1) * (rows // n_chunks), :]

    def col_chunk(ref, c):
        width = ref.shape[-1] // n_chunks
        return slice(c * width, (c + 1) * width)

    def body(*refs):
        a_ref, b_ref = refs[0], refs[1]
        ex = refs[2:2 + n_ex]
        out_refs = refs[2 + n_ex:2 + n_ex + n_out]
        acc_ref = refs[2 + n_ex + n_out] if gk > 1 else None
        i = pl.program_id(1 if j_outer else 0)
        k = pl.program_id(2)
        if mode == "cols":
            a_val = a_ref[...]
            for c in range(n_chunks):
                acc = _dot(a_val, b_chunk(b_ref, c), kind)
                vals = epi(acc, *[e[:, col_chunk(e, c)] for e in ex]) if epi is not None else (acc,)
                for o, v in zip(out_refs, vals):
                    o[:, col_chunk(o, c)] = v.astype(o.dtype)
            return
        if mode == "sum":
            part = _dot(a_ref[:, col_chunk(a_ref, 0)], b_chunk(b_ref, 0), kind)
            for c in range(1, n_chunks):
                part = part + _dot(a_ref[:, col_chunk(a_ref, c)], b_chunk(b_ref, c), kind)
        else:
            part = _dot(a_ref[...], b_ref[...], kind)

        def finish(acc):
            if out_chunks:
                width = acc.shape[-1] // out_chunks
                for c in range(out_chunks):
                    out_refs[0][c] = acc[:, c * width:(c + 1) * width].astype(out_refs[0].dtype)
                return
            vals = epi(acc, *[e[...] for e in ex]) if epi is not None else (acc,)
            for idx, (o, v) in enumerate(zip(out_refs, vals)):
                if idx in acc_outs:
                    @pl.when(i == 0)
                    def _():
                        o[...] = v.astype(o.dtype)

                    @pl.when(i != 0)
                    def _():
                        o[...] += v.astype(o.dtype)
                else:
                    o[...] = v.astype(o.dtype)

        if gk == 1:
            finish(part)
        else:
            @pl.when(k == 0)
            def _():
                acc_ref[...] = part

            @pl.when(k != 0)
            def _():
                acc_ref[...] += part

            @pl.when(k == gk - 1)
            def _():
                finish(acc_ref[...])

    scratch = []
    if gk > 1:
        tm = a_blk[0][-1] if kind == "tn" else a_blk[0][-2]
        tn = b_blk[0][-2] if kind == "nt" else b_blk[0][-1]
        scratch = [pltpu.VMEM((tm, tn), F32)]
    b_kw = dict(pipeline_mode=pl.Buffered(1)) if b_resident else {}
    return _pcall(body, name, (gj, gi, gk) if j_outer else (gi, gj, gk), [a, b] + [e for e, _, _ in extras],
                  [spec(*a_blk), spec(*b_blk, **b_kw)] + [spec(blk, fn) for _, blk, fn in extras],
                  [s for s, _, _ in outs], [spec(blk, fn) for _, blk, fn in outs], scratch, comm=comm)


def _rms_fwd_vals(x, g):
    r = lax.rsqrt(jnp.mean(x * x, axis=-1, keepdims=True) + EPS)
    return x * r * g


def _rms_bwd_vals(x, g, du):
    r = lax.rsqrt(jnp.mean(x * x, axis=-1, keepdims=True) + EPS)
    xh = x * r
    dxh = du * g
    dx = r * (dxh - xh * jnp.mean(dxh * xh, axis=-1, keepdims=True))
    return dx, jnp.sum(du * xh, axis=0, keepdims=True)


def _rms_fwd(name, x, g, rows, comm=_NO_COMM):
    n = x.shape[0]

    def body(x_ref, g_ref, o_ref):
        o_ref[...] = _rms_fwd_vals(x_ref[...], g_ref[...]).astype(BF16)

    return _pcall(body, name, (n // rows,), [x, g],
                  [pl.BlockSpec((rows, D_MODEL), lambda i: (i, 0)), pl.BlockSpec((1, D_MODEL), lambda i: (0, 0))],
                  [jax.ShapeDtypeStruct(x.shape, BF16)], [pl.BlockSpec((rows, D_MODEL), lambda i: (i, 0))], comm=comm)


def _rope_tables(seq):
    half = ROT_DIM // 2
    pos = jnp.arange(seq, dtype=F32)
    inv_freq = ROPE_THETA ** (-jnp.arange(0, ROT_DIM, 2, dtype=F32) / ROT_DIM)
    ang = pos[:, None] * inv_freq[None, :]
    cos, sin = jnp.cos(ang), jnp.sin(ang)
    rest = HEAD_DIM - ROT_DIM
    c = jnp.concatenate([cos, cos, jnp.ones((seq, rest), F32)], axis=1)
    s1 = jnp.concatenate([jnp.zeros((seq, half), F32), sin, jnp.zeros((seq, rest), F32)], axis=1)
    s2 = jnp.concatenate([-sin, jnp.zeros((seq, half + rest), F32)], axis=1)
    return c, s1, s2


def _group_shapes(seq, width, dtype):
    return [jax.ShapeDtypeStruct((d, seq // d, width), dtype) for d in DILATIONS]


def _group_specs(width):
    return [pl.BlockSpec((d, ROW_BLK // d, width), lambda i: (0, i, 0)) for d in DILATIONS]


def _qkv_prep(z, tabs, comm=_NO_COMM):
    seq = z.shape[0]

    def body(z_ref, c_ref, s1_ref, s2_ref, a0, a1, a2, sc):
        outs = (a0, a1, a2)
        c, s1, s2 = c_ref[...], s1_ref[...], s2_ref[...]
        for part in range(3):
            for hh in range(N_GROUPS * HEADS_PER_GROUP):
                g, hl = divmod(hh, HEADS_PER_GROUP)
                col = part * ATTN_W + hh * HEAD_DIM
                ocol = part * GROUP_W + hl * HEAD_DIM
                x = z_ref[:, col:col + HEAD_DIM].astype(F32)
                if part < 2:
                    x = x * c + pltpu.roll(x, ROT_DIM // 2, 1) * s1 + pltpu.roll(x, HEAD_DIM - ROT_DIM // 2, 1) * s2
                d = DILATIONS[g]
                if d == 1:
                    outs[g][0, :, ocol:ocol + HEAD_DIM] = x.astype(BF16)
                else:
                    sc[...] = x
                    for r in range(d):
                        outs[g][r, :, ocol:ocol + HEAD_DIM] = sc[pl.ds(r, ROW_BLK // d, stride=d), :].astype(BF16)

    tab_spec = pl.BlockSpec((ROW_BLK, HEAD_DIM), lambda i: (i, 0))
    return _pcall(body, "qkv_prep", (seq // ROW_BLK,), [z, *tabs],
                  [pl.BlockSpec((ROW_BLK, QKV_W), lambda i: (i, 0)), tab_spec, tab_spec, tab_spec],
                  _group_shapes(seq, ATTN_W, BF16), _group_specs(ATTN_W), [pltpu.VMEM((ROW_BLK, HEAD_DIM), F32)], comm=comm)


def _band_masks_2(t):
    qi = lax.broadcasted_iota(jnp.int32, (BAND, 2 * BAND), 0)
    kj = lax.broadcasted_iota(jnp.int32, (BAND, 2 * BAND), 1)
    band = jnp.logical_and(kj >= qi, kj <= qi + BAND)
    return band, jnp.logical_and(band, jnp.logical_or(kj >= BAND, t > 0))


def _attn_fwd(name, a_g, comm=_NO_COMM):
    dil, m_len, _ = a_g.shape
    qb = min(QB, m_len // BAND)
    rows = qb * BAND
    steps = m_len // rows
    scale = HEAD_DIM ** -0.5

    tiles = [(sb, h) for sb in range(qb) for h in range(HEADS_PER_GROUP)]

    def body(q_ref, kc_ref, vc_ref, kp_ref, vp_ref, o_ref, l_ref, k_all, v_all, s_scr, p_scr, r_scr):
        t = pl.program_id(1)
        k_all[0:BAND, :] = kp_ref[...]
        k_all[BAND:, :] = kc_ref[...]
        v_all[0:BAND, :] = vp_ref[...]
        v_all[BAND:, :] = vc_ref[...]
        band, band_first = _band_masks_2(t)
        for idx, (sb, h) in enumerate(tiles):
            cs = slice(h * HEAD_DIM, (h + 1) * HEAD_DIM)
            s = _dot(q_ref[sb * BAND:(sb + 1) * BAND, cs], k_all[sb * BAND:(sb + 2) * BAND, cs], "nt") * scale
            s_scr[idx] = jnp.where(band_first if sb == 0 else band, s, NEG)
        for idx, (sb, h) in enumerate(tiles):
            cs = slice(h * HEAD_DIM, (h + 1) * HEAD_DIM)
            s = s_scr[idx]
            mx = jnp.max(s, axis=-1, keepdims=True)
            p = jnp.exp(s - mx)
            den = jnp.sum(p, axis=-1, keepdims=True)
            p_scr[idx] = p.astype(BF16)
            r_scr[idx] = jnp.broadcast_to(1.0 / den, (BAND, HEAD_DIM))
            l_ref[sb * BAND:(sb + 1) * BAND, cs] = jnp.broadcast_to(mx + jnp.log(den), (BAND, HEAD_DIM))
        for idx, (sb, h) in enumerate(tiles):
            cs = slice(h * HEAD_DIM, (h + 1) * HEAD_DIM)
            o_ref[sb * BAND:(sb + 1) * BAND, cs] = _dot(p_scr[idx], v_all[sb * BAND:(sb + 2) * BAND, cs], "nn") * r_scr[idx]

    def prev(r, t):
        return jnp.maximum(qb * t - 1, 0)

    cur = lambda c: pl.BlockSpec((None, rows, GROUP_W), lambda r, t, c=c: (r, t, c))
    prv = lambda c: pl.BlockSpec((None, BAND, GROUP_W), lambda r, t, c=c: (r, prev(r, t), c))
    out_spec = pl.BlockSpec((None, rows, GROUP_W), lambda r, t: (r, t, 0))
    shp = jax.ShapeDtypeStruct((dil, m_len, GROUP_W), F32)
    n_t = len(tiles)
    return _pcall(body, name, (dil, steps), [a_g] * 5, [cur(0), cur(1), cur(2), prv(1), prv(2)], [shp, shp],
                  [out_spec, out_spec],
                  [pltpu.VMEM((rows + BAND, GROUP_W), BF16), pltpu.VMEM((rows + BAND, GROUP_W), BF16),
                   pltpu.VMEM((n_t, BAND, 2 * BAND), F32), pltpu.VMEM((n_t, BAND, 2 * BAND), BF16),
                   pltpu.VMEM((n_t, BAND, HEAD_DIM), F32)], comm=comm)


def _attn_merge(os_, ls_, seq):
    def body(o0, l0, o1, l1, o2, l2, at_ref, lt_ref, sc):
        for h in range(HEADS_PER_GROUP):
            cs = slice(h * HEAD_DIM, (h + 1) * HEAD_DIM)
            for gi, (o_r, l_r) in enumerate(((o1, l1), (o2, l2))):
                d = DILATIONS[gi + 1]
                for r in range(d):
                    sc.at[2 * gi][pl.ds(r, ROW_BLK // d, stride=d), :] = o_r[r, :, cs]
                    sc.at[2 * gi + 1][pl.ds(r, ROW_BLK // d, stride=d), :] = l_r[r, :, cs]
            o0v, l0v = o0[0, :, cs], l0[0, :, cs]
            o1v, l1v, o2v, l2v = sc[0], sc[1], sc[2], sc[3]
            mx = jnp.maximum(jnp.maximum(l0v, l1v), l2v)
            e0, e1, e2 = jnp.exp(l0v - mx), jnp.exp(l1v - mx), jnp.exp(l2v - mx)
            tot = e0 + e1 + e2
            at_ref[:, cs] = ((e0 * o0v + e1 * o1v + e2 * o2v) / tot).astype(BF16)
            lt_ref[:, cs] = mx + jnp.log(tot)

    gs = _group_specs(GROUP_W)
    row = pl.BlockSpec((ROW_BLK, GROUP_W), lambda i: (i, 0))
    return pl.pallas_call(
        body, name="attn_merge",
        out_shape=[jax.ShapeDtypeStruct((seq, GROUP_W), BF16), jax.ShapeDtypeStruct((seq, GROUP_W), F32)],
        grid=(seq // ROW_BLK,), in_specs=[gs[0], gs[0], gs[1], gs[1], gs[2], gs[2]], out_specs=[row, row],
        scratch_shapes=[pltpu.VMEM((4, ROW_BLK, HEAD_DIM), F32)],
        compiler_params=_params(dimension_semantics=("arbitrary",)),
    )(os_[0], ls_[0], os_[1], ls_[1], os_[2], ls_[2])


def _attn_bwd_prep(dattn, attn, lt):
    seq = dattn.shape[0]

    def body(da_ref, at_ref, lt_ref, d0, c0, t0, d1, c1, t1, d2, c2, t2, sc):
        outs = ((d0, c0, t0), (d1, c1, t1), (d2, c2, t2))
        for h in range(HEADS_PER_GROUP):
            cs = slice(h * HEAD_DIM, (h + 1) * HEAD_DIM)
            da = da_ref[:, cs].astype(F32)
            cc = jnp.broadcast_to(jnp.sum(da * at_ref[:, cs].astype(F32), axis=-1, keepdims=True), (ROW_BLK, HEAD_DIM))
            ltv = lt_ref[:, cs]
            d0[0, :, cs] = da_ref[:, cs]
            c0[0, :, cs] = cc
            t0[0, :, cs] = ltv
            sc[0], sc[1], sc[2] = da, cc, ltv
            for g in (1, 2):
                d = DILATIONS[g]
                for r in range(d):
                    rows = pl.ds(r, ROW_BLK // d, stride=d)
                    outs[g][0][r, :, cs] = sc.at[0][rows, :].astype(BF16)
                    outs[g][1][r, :, cs] = sc.at[1][rows, :]
                    outs[g][2][r, :, cs] = sc.at[2][rows, :]

    gs = _group_specs(GROUP_W)
    row = pl.BlockSpec((ROW_BLK, GROUP_W), lambda i: (i, 0))
    shapes, specs = [], []
    for g, d in enumerate(DILATIONS):
        for dt in (BF16, F32, F32):
            shapes.append(jax.ShapeDtypeStruct((d, seq // d, GROUP_W), dt))
            specs.append(gs[g])
    res = pl.pallas_call(
        body, name="attn_bwd_prep", out_shape=shapes, grid=(seq // ROW_BLK,), in_specs=[row, row, row],
        out_specs=specs, scratch_shapes=[pltpu.VMEM((3, ROW_BLK, HEAD_DIM), F32)],
        compiler_params=_params(dimension_semantics=("arbitrary",)),
    )(dattn, attn, lt)
    return [res[3 * g:3 * g + 3] for g in range(N_GROUPS)]


def _attn_bwd(name, a_g, da_g, c_g, lt_g, comm=_NO_COMM):
    dil, m_len, _ = a_g.shape
    qb = min(QB, m_len // BAND)
    rows = qb * BAND
    steps = m_len // rows
    scale = HEAD_DIM ** -0.5

    tiles = [(sb, h) for sb in range(qb) for h in range(HEADS_PER_GROUP)]

    def body(q_ref, kc_ref, vc_ref, kp_ref, vp_ref, da_ref, c_ref, lt_ref, d_ref, dk_acc, dv_acc, car_k, car_v,
             k_all, v_all, s_scr, dp_scr, p_scr, ds_scr):
        tg = pl.program_id(1)
        t = steps - 1 - tg

        @pl.when(tg == 0)
        def _():
            car_k[...] = jnp.zeros_like(car_k)
            car_v[...] = jnp.zeros_like(car_v)

        k_all[0:BAND, :] = kp_ref[...]
        k_all[BAND:, :] = kc_ref[...]
        v_all[0:BAND, :] = vp_ref[...]
        v_all[BAND:, :] = vc_ref[...]
        zero = jnp.zeros((rows, GROUP_W), F32)
        dk_acc[0:rows, :] = zero
        dv_acc[0:rows, :] = zero
        dk_acc[rows:rows + BAND, :] = car_k[...]
        dv_acc[rows:rows + BAND, :] = car_v[...]
        band, band_first = _band_masks_2(t)
        for idx, (sb, h) in enumerate(tiles):
            cs = slice(h * HEAD_DIM, (h + 1) * HEAD_DIM)
            rs, ks = slice(sb * BAND, (sb + 1) * BAND), slice(sb * BAND, (sb + 2) * BAND)
            s_scr[idx] = _dot(q_ref[rs, cs], k_all[ks, cs], "nt")
            dp_scr[idx] = _dot(da_ref[rs, cs], v_all[ks, cs], "nt")
        for idx, (sb, h) in enumerate(tiles):
            cs = slice(h * HEAD_DIM, (h + 1) * HEAD_DIM)
            rs = slice(sb * BAND, (sb + 1) * BAND)
            ltv = jnp.concatenate([lt_ref[rs, cs]] * 2, axis=1)
            cc = jnp.concatenate([c_ref[rs, cs]] * 2, axis=1)
            p = jnp.exp(jnp.where(band_first if sb == 0 else band, s_scr[idx] * scale - ltv, NEG))
            p_scr[idx] = p.astype(BF16)
            ds_scr[idx] = (p * (dp_scr[idx] - cc) * scale).astype(BF16)
        for idx, (sb, h) in enumerate(tiles):
            cs = slice(h * HEAD_DIM, (h + 1) * HEAD_DIM)
            rs, ks = slice(sb * BAND, (sb + 1) * BAND), slice(sb * BAND, (sb + 2) * BAND)
            d_ref[rs, cs] = _dot(ds_scr[idx], k_all[ks, cs], "nn").astype(BF16)
            dk_acc[ks, cs] += _dot(ds_scr[idx], q_ref[rs, cs], "tn")
            dv_acc[ks, cs] += _dot(p_scr[idx], da_ref[rs, cs], "tn")
        d_ref[:, GROUP_W:2 * GROUP_W] = dk_acc[BAND:rows + BAND, :].astype(BF16)
        d_ref[:, 2 * GROUP_W:3 * GROUP_W] = dv_acc[BAND:rows + BAND, :].astype(BF16)
        car_k[...] = dk_acc[0:BAND, :]
        car_v[...] = dv_acc[0:BAND, :]

    def rev(tg):
        return steps - 1 - tg

    def prev(tg):
        return jnp.maximum(qb * rev(tg) - 1, 0)

    cur = lambda c: pl.BlockSpec((None, rows, GROUP_W), lambda r, tg, c=c: (r, rev(tg), c))
    prv = lambda c: pl.BlockSpec((None, BAND, GROUP_W), lambda r, tg, c=c: (r, prev(tg), c))
    return _pcall(
        body, name, (dil, steps), [a_g, a_g, a_g, a_g, a_g, da_g, c_g, lt_g],
        [cur(0), cur(1), cur(2), prv(1), prv(2), cur(0), cur(0), cur(0)],
        [jax.ShapeDtypeStruct((dil, m_len, ATTN_W), BF16)], [pl.BlockSpec((None, rows, ATTN_W), lambda r, tg: (r, rev(tg), 0))],
        [pltpu.VMEM((rows + BAND, GROUP_W), F32), pltpu.VMEM((rows + BAND, GROUP_W), F32),
         pltpu.VMEM((BAND, GROUP_W), F32), pltpu.VMEM((BAND, GROUP_W), F32),
         pltpu.VMEM((rows + BAND, GROUP_W), BF16), pltpu.VMEM((rows + BAND, GROUP_W), BF16),
         pltpu.VMEM((len(tiles), BAND, 2 * BAND), F32), pltpu.VMEM((len(tiles), BAND, 2 * BAND), F32),
         pltpu.VMEM((len(tiles), BAND, 2 * BAND), BF16), pltpu.VMEM((len(tiles), BAND, 2 * BAND), BF16)], comm=comm)


def _dqkv_post(d_gs, tabs, dz):
    seq = dz.shape[0]

    def body(g0, g1, g2, c_ref, s1_ref, s2_ref, dz_any, o_ref, sc):
        del dz_any
        ins = (g0, g1, g2)
        c, s1, s2 = c_ref[...], s1_ref[...], s2_ref[...]
        for part in range(3):
            for hh in range(N_GROUPS * HEADS_PER_GROUP):
                g, hl = divmod(hh, HEADS_PER_GROUP)
                icol = part * GROUP_W + hl * HEAD_DIM
                ocol = part * ATTN_W + hh * HEAD_DIM
                d = DILATIONS[g]
                if d == 1:
                    x = ins[g][0, :, icol:icol + HEAD_DIM].astype(F32)
                else:
                    for r in range(d):
                        sc[pl.ds(r, ROW_BLK // d, stride=d), :] = ins[g][r, :, icol:icol + HEAD_DIM].astype(F32)
                    x = sc[...]
                if part < 2:
                    x = x * c + pltpu.roll(x * s1, HEAD_DIM - ROT_DIM // 2, 1) + pltpu.roll(x * s2, ROT_DIM // 2, 1)
                o_ref[:, ocol:ocol + HEAD_DIM] = x.astype(BF16)

    tab_spec = pl.BlockSpec((ROW_BLK, HEAD_DIM), lambda i: (i, 0))
    return pl.pallas_call(
        body, name="dqkv_post", out_shape=jax.ShapeDtypeStruct(dz.shape, BF16), grid=(seq // ROW_BLK,),
        in_specs=_group_specs(ATTN_W) + [tab_spec, tab_spec, tab_spec, pl.BlockSpec(memory_space=pl.ANY)],
        out_specs=pl.BlockSpec((ROW_BLK, QKV_W), lambda i: (i, 0)),
        scratch_shapes=[pltpu.VMEM((ROW_BLK, HEAD_DIM), F32)], input_output_aliases={6: 0},
        compiler_params=_params(dimension_semantics=("arbitrary",)),
    )(*d_gs, *tabs, dz)


def _glu(zg):
    a = zg[:, :CONV_CH].astype(F32)
    s = _sigmoid(zg[:, CONV_CH:].astype(F32))
    return a, s, a * s


def _shifted_copies(xs):
    n = xs.shape[1] - SUBLANES
    for b in range(1, SUBLANES):
        xs[b, 0:n, :] = xs[0, pl.ds(b, n), :]


def _shifted(xs, offset, r0, cs):
    a, b = divmod(offset, SUBLANES)
    return xs[b, pl.ds(SUBLANES * a + r0, CONV_ROWS), cs]


def _conv_fwd(z, cw, cb, lg, lb, comm=_NO_COMM):
    seq = z.shape[0]
    halo_per_blk = ROW_BLK // CONV_HALO

    def body(zg_ref, zh_ref, cw_ref, cb_ref, lg_ref, lb_ref, c2_ref, c4_ref, xs):
        i = pl.program_id(0)
        _, _, c1 = _glu(zg_ref[...])
        _, _, c1h = _glu(zh_ref[...])
        xs[0, 0:CONV_HALO, :] = jnp.where(i > 0, c1h, 0.0)
        xs[0, CONV_HALO:, :] = c1
        _shifted_copies(xs)
        for s in range(CONV_CH // HEAD_DIM):
            cs = slice(s * HEAD_DIM, (s + 1) * HEAD_DIM)
            taps = [cw_ref[j:j + 1, cs] for j in range(CONV_K)]
            bias = cb_ref[:, cs]

            def chunk(rc, carry, cs=cs, taps=taps, bias=bias):
                r0 = pl.multiple_of(rc * CONV_ROWS, CONV_ROWS)
                acc = jnp.zeros((CONV_ROWS, HEAD_DIM), F32)
                for j in range(CONV_K):
                    acc = acc + taps[j] * _shifted(xs, CONV_HALO - (CONV_K - 1) + j, r0, cs)
                c2_ref[pl.ds(r0, CONV_ROWS), cs] = acc + bias
                return carry

            lax.fori_loop(0, ROW_BLK // CONV_ROWS, chunk, 0)
        c2 = c2_ref[...]
        mu = jnp.mean(c2, axis=-1, keepdims=True)
        xc = c2 - mu
        rstd = lax.rsqrt(jnp.mean(xc * xc, axis=-1, keepdims=True) + EPS)
        c3 = xc * rstd * lg_ref[...] + lb_ref[...]
        c4_ref[...] = (c3 * _sigmoid(c3)).astype(BF16)

    vec = pl.BlockSpec((1, CONV_CH), lambda i: (0, 0))
    return _pcall(
        body, "conv_fwd", (seq // ROW_BLK,), [z, z, cw, cb, lg, lb],
        [pl.BlockSpec((ROW_BLK, 2 * CONV_CH), lambda i: (i, GLU_COL_BLK)),
         pl.BlockSpec((CONV_HALO, 2 * CONV_CH), lambda i: (jnp.maximum(i * halo_per_blk - 1, 0), GLU_COL_BLK)),
         pl.BlockSpec((CONV_HALO, CONV_CH), lambda i: (0, 0)), vec, vec, vec],
        [jax.ShapeDtypeStruct((seq, CONV_CH), F32), jax.ShapeDtypeStruct((seq, CONV_CH), BF16)],
        [pl.BlockSpec((ROW_BLK, CONV_CH), lambda i: (i, 0)), pl.BlockSpec((ROW_BLK, CONV_CH), lambda i: (i, 0))],
        [pltpu.VMEM((SUBLANES, ROW_BLK + CONV_HALO, CONV_CH), F32)], comm=comm)


def _conv_bwd(dc2, z, cw, dz, comm=_NO_COMM):
    seq = z.shape[0]
    halo_per_blk = ROW_BLK // CONV_HALO
    n_blk = seq // ROW_BLK
    last_halo = seq // CONV_HALO - 1

    def body(dc_ref, dn_ref, zg_ref, zh_ref, cw_ref, dz_any, o_ref, dcw_ref, xs, ys, dc1_ref, dcw_acc):
        del dz_any
        i = pl.program_id(0)
        a, s, c1 = _glu(zg_ref[...])
        _, _, c1h = _glu(zh_ref[...])
        xs[0, 0:CONV_HALO, :] = jnp.where(i > 0, c1h, 0.0)
        xs[0, CONV_HALO:, :] = c1
        ys[0, 0:ROW_BLK, :] = dc_ref[...]
        ys[0, ROW_BLK:, :] = jnp.where(i < n_blk - 1, dn_ref[...], 0.0)
        _shifted_copies(xs)
        _shifted_copies(ys)

        @pl.when(i == 0)
        def _():
            dcw_acc[...] = jnp.zeros_like(dcw_acc)

        for sl in range(CONV_CH // HEAD_DIM):
            cs = slice(sl * HEAD_DIM, (sl + 1) * HEAD_DIM)
            taps = [cw_ref[j:j + 1, cs] for j in range(CONV_K)]

            def chunk(rc, carry, cs=cs, taps=taps):
                r0 = pl.multiple_of(rc * CONV_ROWS, CONV_ROWS)
                dc = ys[0, pl.ds(r0, CONV_ROWS), cs]
                acc = jnp.zeros((CONV_ROWS, HEAD_DIM), F32)
                for j in range(CONV_K):
                    prod = dc * _shifted(xs, CONV_HALO - (CONV_K - 1) + j, r0, cs)
                    dcw_acc[j, :, cs] += jnp.sum(prod.reshape(CONV_ROWS // SUBLANES, SUBLANES, HEAD_DIM), axis=0)
                    acc = acc + taps[j] * _shifted(ys, CONV_K - 1 - j, r0, cs)
                dc1_ref[pl.ds(r0, CONV_ROWS), cs] = acc
                return carry

            lax.fori_loop(0, ROW_BLK // CONV_ROWS, chunk, 0)
        dc1 = dc1_ref[...]
        o_ref[:, :CONV_CH] = (dc1 * s).astype(BF16)
        o_ref[:, CONV_CH:] = (dc1 * a * s * (1.0 - s)).astype(BF16)

        @pl.when(i == n_blk - 1)
        def _():
            dcw_ref[...] = jnp.sum(dcw_acc[...], axis=1)

    return _pcall(
        body, "conv_bwd", (n_blk,), [dc2, dc2, z, z, cw, dz],
        [pl.BlockSpec((ROW_BLK, CONV_CH), lambda i: (i, 0)),
         pl.BlockSpec((CONV_HALO, CONV_CH), lambda i: (jnp.minimum((i + 1) * halo_per_blk, last_halo), 0)),
         pl.BlockSpec((ROW_BLK, 2 * CONV_CH), lambda i: (i, GLU_COL_BLK)),
         pl.BlockSpec((CONV_HALO, 2 * CONV_CH), lambda i: (jnp.maximum(i * halo_per_blk - 1, 0), GLU_COL_BLK)),
         pl.BlockSpec((CONV_HALO, CONV_CH), lambda i: (0, 0)),
         pl.BlockSpec(memory_space=pl.ANY)],
        [jax.ShapeDtypeStruct(dz.shape, BF16), jax.ShapeDtypeStruct((CONV_HALO, CONV_CH), F32)],
        [pl.BlockSpec((ROW_BLK, 2 * CONV_CH), lambda i: (i, GLU_COL_BLK)), pl.BlockSpec((CONV_HALO, CONV_CH), lambda i: (0, 0))],
        [pltpu.VMEM((SUBLANES, ROW_BLK + CONV_HALO, CONV_CH), F32), pltpu.VMEM((SUBLANES, ROW_BLK + CONV_HALO, CONV_CH), F32),
         pltpu.VMEM((ROW_BLK, CONV_CH), F32), pltpu.VMEM((CONV_HALO, SUBLANES, CONV_CH), F32)],
        aliases={5: 0}, comm=comm)


def _epi_mix(ya, c4, wcp, gates, bg):
    yc = _dot(c4, wcp, "nn")
    gv = _sigmoid(gates.astype(F32) + bg)
    merged = gv[:, :D_MODEL] * ya + gv[:, D_MODEL:] * yc
    return merged, ya, yc


def _epi_residual_rms(acc, xres, g):
    x = xres + acc
    return x, _rms_fwd_vals(x, g)


def _cross_scores(cq, ck):
    out = []
    for h in range(CROSS_HEADS):
        cs = slice(h * CROSS_HD, (h + 1) * CROSS_HD)
        s = _dot(cq[:, cs], ck[:, cs], "nt") * (CROSS_HD ** -0.5)
        e = jnp.exp(s - jnp.max(s, axis=-1, keepdims=True))
        out.append((cs, e, jnp.sum(e, axis=-1, keepdims=True)))
    return out


def _epi_cross_fwd(acc, ck, cv):
    cq = acc.astype(BF16)
    co = [_dot(e, cv[:, cs], "nn") / den for cs, e, den in _cross_scores(cq, ck)]
    return cq, jnp.concatenate(co, axis=1)


def _epi_cross_bwd(dco, cq, ck, cv):
    dco = dco.astype(BF16)
    dcq, dck, dcv = [], [], []
    for cs, e, den in _cross_scores(cq, ck):
        p = e / den
        dp = _dot(dco[:, cs], cv[:, cs], "nt")
        ds = (p * (dp - jnp.sum(dp * p, axis=-1, keepdims=True)) * (CROSS_HD ** -0.5)).astype(BF16)
        dcq.append(_dot(ds, ck[:, cs], "nn"))
        dck.append(_dot(ds, cq[:, cs], "tn"))
        dcv.append(_dot(p, dco[:, cs], "tn"))
    return jnp.concatenate(dcq, axis=1), jnp.concatenate(dck, axis=1), jnp.concatenate(dcv, axis=1)


def _epi_mlp_up(acc):
    return acc, jnp.square(jnp.maximum(acc, 0.0))


def _epi_final(acc, x2, tgt, g):
    x3 = x2 + acc
    err = _rms_fwd_vals(x3, g) - tgt
    loss = (0.5 / D_MODEL) * jnp.sum(err * err)
    dx3, dg = _rms_bwd_vals(x3, g, err * (1.0 / D_MODEL))
    return dx3, dx3, jnp.full((1, HEAD_DIM), loss, F32), dg


def _epi_mlp_down_bwd(dh, hpre):
    return (dh * 2.0 * jnp.maximum(hpre.astype(F32), 0.0),)


def _epi_rms_bwd(du, x, g, dres):
    dx, dg = _rms_bwd_vals(x, g, du)
    return dres + dx, dg


def _epi_rms_bwd_2(du, x, g, dres):
    dx, dg = _epi_rms_bwd(du, x, g, dres)
    return dx, dx, dg


def _epi_rms_bwd_g(du, x, g):
    return (_rms_bwd_vals(x, g, du)[1],)


def _epi_mix_bwd(dm, ya, yc, gates, bg):
    gv = _sigmoid(gates.astype(F32) + bg)
    ga, gb = gv[:, :D_MODEL], gv[:, D_MODEL:]
    ya, yc = ya.astype(F32), yc.astype(F32)
    dgate = jnp.concatenate([dm * ya * ga * (1.0 - ga), dm * yc * gb * (1.0 - gb)], axis=1)
    return dm * ga, dm * gb, dgate, jnp.sum(dgate, axis=0, keepdims=True)


def _epi_ln_bwd(dc4, c2, lg, lb):
    mu = jnp.mean(c2, axis=-1, keepdims=True)
    xc = c2 - mu
    rstd = lax.rsqrt(jnp.mean(xc * xc, axis=-1, keepdims=True) + EPS)
    xh = xc * rstd
    c3 = xh * lg + lb
    sg = _sigmoid(c3)
    dc3 = dc4 * sg * (1.0 + c3 * (1.0 - sg))
    dxh = dc3 * lg
    dc2 = rstd * (dxh - jnp.mean(dxh, axis=-1, keepdims=True) - xh * jnp.mean(dxh * xh, axis=-1, keepdims=True))
    return (dc2, jnp.sum(dc3 * xh, axis=0, keepdims=True), jnp.sum(dc3, axis=0, keepdims=True),
            jnp.sum(dc2, axis=0, keepdims=True))


def _sds(shape, dtype):
    return jax.ShapeDtypeStruct(shape, dtype)


class _Lazy:
    def __init__(self, fn):
        self.fn = fn

    def __getitem__(self, key):
        return self.fn(key)


def _local_step(x, mem, tgt, sm, plan):
    w = _Lazy(plan.w)
    dw = {}

    def carry(name, n_own, fn, *args, **kw):
        c = plan.comm(name, dw)
        res = fn(*args, comm=c, **kw)
        plan.done(name, res[n_own:])
        return res[:n_own]

    def mm(name, *args, **kw):
        return carry(name, len(args[6]), _mm, name, *args, **kw)

    seq = x.shape[0]
    nr = seq // ROW_BLK
    big = min(1024, seq)
    nb = seq // big
    row = lambda n: ((ROW_BLK, n), lambda i, j, k: (i, 0))
    vec = lambda n: ((1, n), lambda i, j, k: (0, 0))
    full = lambda r, c: ((r, c), lambda i, j, k: (0, 0))
    gates_blk = ((ROW_BLK, 2 * D_MODEL), lambda i, j, k: (i, GATE_COL_BLK))
    tabs = _rope_tables(seq)

    u = carry("rms_mix", 1, _rms_fwd, "rms_mix", x, sm["g_mix"], ROW_BLK)[0]
    whole3 = lambda a: (a.shape, lambda i, j, k: (0, 0, 0))
    z = mm("in_proj", u, w["w_in"], "nn", (nr, 1, 1), row(D_MODEL), whole3(w["w_in"]),
           [(_sds((seq, IN_W), BF16), *row(IN_W))], split=("cols", N_DEV), b_resident=True)[0]
    a_gs = carry("qkv_prep", 3, _qkv_prep, z, tabs)
    os_, ls_ = [], []
    for g in range(N_GROUPS):
        name = "attn_fwd_%d" % g
        o_g, l_g = carry(name, 2, _attn_fwd, name, a_gs[g])
        os_.append(o_g)
        ls_.append(l_g)
    attn, lt = _attn_merge(os_, ls_, seq)
    c2, c4 = carry("conv_fwd", 2, _conv_fwd, z, w["taps"], sm["conv_b"], sm["conv_ln_g"], sm["conv_ln_b"])
    merged, ya, yc = mm(
        "mix", attn, w["w_attn_proj"], "nn", (nr, 1, 1), row(GROUP_W), full(GROUP_W, D_MODEL),
        [(_sds((seq, D_MODEL), BF16), *row(D_MODEL))] * 3,
        extras=[(c4, *row(CONV_CH)), (w["w_conv_proj"], *full(CONV_CH, D_MODEL)), (z, *gates_blk), (sm["b_gate"], *vec(2 * D_MODEL))],
        epi=_epi_mix)
    x1, uq = mm("out_proj", merged, w["w_out"], "nn", (nr, 1, 1), row(D_MODEL), full(D_MODEL, D_MODEL),
                 [(_sds((seq, D_MODEL), F32), *row(D_MODEL)), (_sds((seq, D_MODEL), BF16), *row(D_MODEL))],
                 extras=[(x, *row(D_MODEL)), (sm["g_cross"], *vec(D_MODEL))], epi=_epi_residual_rms)

    mn = _rms_fwd("rms_mem", mem, sm["g_mem"], N_MEM)[0]
    ckv = mm("ckv_proj", mn, w["w_ckv"], "nn", (1, N_DEV, 1), full(N_MEM, D_MODEL),
              ((None, D_MODEL, 2 * D_MODEL // N_DEV), lambda i, j, k: (j, 0, 0)),
              [(_sds((N_MEM, 2 * D_MODEL), BF16), (N_MEM, 2 * D_MODEL // N_DEV), lambda i, j, k: (0, j))])[0]
    ck, cv = ckv[:, :D_MODEL], ckv[:, D_MODEL:]
    kv_blk = full(N_MEM, D_MODEL)
    cq, co = mm("cq_proj_cross", uq, w["w_cq"], "nn", (nr, 1, 1), row(D_MODEL), full(D_MODEL, D_MODEL),
                 [(_sds((seq, D_MODEL), BF16), *row(D_MODEL))] * 2,
                 extras=[(ck, *kv_blk), (cv, *kv_blk)], epi=_epi_cross_fwd)
    x2, um = mm("co_proj", co, w["w_co"], "nn", (nr, 1, 1), row(D_MODEL), full(D_MODEL, D_MODEL),
                 [(_sds((seq, D_MODEL), F32), *row(D_MODEL)), (_sds((seq, D_MODEL), BF16), *row(D_MODEL))],
                 extras=[(x1, *row(D_MODEL)), (sm["g_mlp"], *vec(D_MODEL))], epi=_epi_residual_rms)

    ff_blk = D_FF // N_DEV
    row_f32 = (_sds((seq, D_MODEL), F32), *row(D_MODEL))
    row_bf16 = (_sds((seq, D_MODEL), BF16), *row(D_MODEL))
    col_sum = (_sds((1, D_MODEL), F32), *vec(D_MODEL))
    hpre, h = mm("mlp_up", um, w["w_up"], "nn", (nr, 1, 1), row(D_MODEL), whole3(w["w_up"]),
                 [(_sds((seq, D_FF), BF16), *row(D_FF))] * 2, epi=_epi_mlp_up, split=("cols", N_DEV), b_resident=True)
    kt = D_FF // D_MODEL
    dx3, dx3b, loss, dg_final = mm(
        "mlp_down_loss", h, w["w_down"], "nn", (nr, 1, 1), row(D_FF), full(D_FF, D_MODEL),
        [row_f32, row_bf16, (_sds((1, HEAD_DIM), F32), *vec(HEAD_DIM)), col_sum],
        extras=[(x2, *row(D_MODEL)), (tgt, *row(D_MODEL)), (sm["g_final"], *vec(D_MODEL))], epi=_epi_final, acc_outs=(2, 3),
        b_resident=True)

    dhpre = mm("mlp_down_bwd", dx3b, w["w_down"], "nt", (nr, 1, 1), row(D_MODEL), full(D_FF, D_MODEL),
               [(_sds((seq, D_FF), BF16), *row(D_FF))], extras=[(hpre, *row(D_FF))], epi=_epi_mlp_down_bwd,
               split=("cols", kt), b_resident=True)[0]
    big2 = min(2 * big, seq)
    nb2 = seq // big2
    dw["w_down"] = mm("dw_down", h, dx3b, "tn", (kt, 1, nb2), ((big2, D_MODEL), lambda i, j, k: (k, i)),
                      ((big2, D_MODEL), lambda i, j, k: (k, 0)),
                      [(_sds((D_FF, D_MODEL), BF16), (D_MODEL, D_MODEL), lambda i, j, k: (i, 0))])[0]
    dx2, dx2b, dg_mlp = mm("mlp_up_bwd", dhpre, w["w_up"], "nt", (nr, 1, 1), row(D_FF), whole3(w["w_up"]),
                           [row_f32, row_bf16, col_sum],
                           extras=[(x2, *row(D_MODEL)), (sm["g_mlp"], *vec(D_MODEL)), (dx3, *row(D_MODEL))],
                           epi=_epi_rms_bwd_2, acc_outs=(2,), split=("sum", N_DEV), b_resident=True)
    dw["w_up"] = mm("dw_up", um, dhpre, "tn", (1, N_DEV, nb2), ((big2, D_MODEL), lambda i, j, k: (k, 0)),
                    ((big2, ff_blk), lambda i, j, k: (k, j)),
                    [(_sds((N_DEV, D_MODEL, ff_blk), BF16), (None, D_MODEL, ff_blk), lambda i, j, k: (j, 0, 0))])[0]

    acc_kv = (_sds((N_MEM, D_MODEL), F32), *kv_blk)
    dcq, dck, dcv = mm("co_proj_bwd_cross", dx2b, w["w_co"], "nt", (nr, 1, 1), row(D_MODEL), full(D_MODEL, D_MODEL),
                       [row_bf16, acc_kv, acc_kv],
                       extras=[(cq, *row(D_MODEL)), (ck, *kv_blk), (cv, *kv_blk)], epi=_epi_cross_bwd, acc_outs=(1, 2))

    def dw_square(name, act, grad):
        return mm(name, act, grad, "tn", (1, 1, nb2), ((big2, D_MODEL), lambda i, j, k: (k, 0)),
                  ((big2, D_MODEL), lambda i, j, k: (k, 0)), [(_sds((D_MODEL, D_MODEL), BF16), *full(D_MODEL, D_MODEL))])[0]

    dw["w_co"] = dw_square("dw_co", co, dx2b)
    dx1, dx1b, dg_cross = mm("cq_proj_bwd", dcq, w["w_cq"], "nt", (nr, 1, 1), row(D_MODEL), full(D_MODEL, D_MODEL),
                             [row_f32, row_bf16, col_sum],
                             extras=[(x1, *row(D_MODEL)), (sm["g_cross"], *vec(D_MODEL)), (dx2, *row(D_MODEL))],
                             epi=_epi_rms_bwd_2, acc_outs=(2,))
    dw["w_cq"] = dw_square("dw_cq", uq, dcq)
    dckv = jnp.concatenate([dck, dcv], axis=1)
    kv_chunk = 2 * D_MODEL // N_DEV
    dw["w_ckv"] = mm("dw_ckv", mn, dckv, "tn", (1, N_DEV, 1), full(N_MEM, D_MODEL), ((N_MEM, kv_chunk), lambda i, j, k: (0, j)),
                      [(_sds((N_DEV, D_MODEL, kv_chunk), BF16), (None, D_MODEL, kv_chunk), lambda i, j, k: (j, 0, 0))])[0]
    dg_mem = mm("ckv_proj_bwd", dckv, w["w_ckv"], "nt", (1, 1, N_DEV), ((N_MEM, kv_chunk), lambda i, j, k: (0, k)),
                 ((None, D_MODEL, kv_chunk), lambda i, j, k: (k, 0, 0)), [(_sds((1, D_MODEL), F32), *vec(D_MODEL))],
                 extras=[(mem, *full(N_MEM, D_MODEL)), (sm["g_mem"], *vec(D_MODEL))], epi=_epi_rms_bwd_g, acc_outs=(0,))[0]

    dya, dyc, dz, db_gate = mm(
        "out_proj_bwd_mix", dx1b, w["w_out"], "nt", (nr, 1, 1), row(D_MODEL), full(D_MODEL, D_MODEL),
        [(_sds((seq, D_MODEL), BF16), *row(D_MODEL)), (_sds((seq, D_MODEL), BF16), *row(D_MODEL)),
         (_sds((seq, IN_W), BF16), *gates_blk), (_sds((1, 2 * D_MODEL), F32), *vec(2 * D_MODEL))],
        extras=[(ya, *row(D_MODEL)), (yc, *row(D_MODEL)), (z, *gates_blk), (sm["b_gate"], *vec(2 * D_MODEL))],
        epi=_epi_mix_bwd, acc_outs=(3,))
    dw["w_out"] = dw_square("dw_out", merged, dx1b)
    dattn = mm("attn_proj_bwd", dya, w["w_attn_proj"], "nt", (nr, 1, 1), row(D_MODEL), full(GROUP_W, D_MODEL),
                [(_sds((seq, GROUP_W), BF16), *row(GROUP_W))])[0]
    pc = D_MODEL // N_DEV
    dw["w_attn_proj"] = mm("dw_attn_proj", attn, dya, "tn", (1, 1, nb2), ((big2, GROUP_W), lambda i, j, k: (k, 0)),
                           ((big2, D_MODEL), lambda i, j, k: (k, 0)),
                           [(_sds((N_DEV, GROUP_W, pc), BF16), (N_DEV, GROUP_W, pc), lambda i, j, k: (0, 0, 0))],
                           out_chunks=N_DEV)[0]
    cvec = (_sds((1, CONV_CH), F32), *vec(CONV_CH))
    dc2, dg_ln_g, dg_ln_b, dg_conv_b = mm(
        "conv_proj_bwd_ln", dyc, w["w_conv_proj"], "nt", (nr, 1, 1), row(D_MODEL), full(CONV_CH, D_MODEL),
        [(_sds((seq, CONV_CH), F32), *row(CONV_CH)), cvec, cvec, cvec],
        extras=[(c2, *row(CONV_CH)), (sm["conv_ln_g"], *vec(CONV_CH)), (sm["conv_ln_b"], *vec(CONV_CH))],
        epi=_epi_ln_bwd, acc_outs=(1, 2, 3))
    dw["w_conv_proj"] = mm("dw_conv_proj", c4, dyc, "tn", (1, 1, nb2), ((big2, CONV_CH), lambda i, j, k: (k, 0)),
                           ((big2, D_MODEL), lambda i, j, k: (k, 0)),
                           [(_sds((N_DEV, CONV_CH, pc), BF16), (N_DEV, CONV_CH, pc), lambda i, j, k: (0, 0, 0))],
                           out_chunks=N_DEV)[0]
    dz, dg_conv_w = carry("conv_bwd", 2, _conv_bwd, dc2, z, w["taps"], dz)
    preps = _attn_bwd_prep(dattn, attn, lt)
    d_gs = []
    for g in range(N_GROUPS):
        name = "attn_bwd_%d" % g
        d_gs.append(carry(name, 1, _attn_bwd, name, a_gs[g], *preps[g])[0])
    dz = _dqkv_post(d_gs, tabs, dz)
    dw["w_in"] = mm("dw_in", u, dz, "tn", (1, N_DEV, nb2), ((big2, D_MODEL), lambda i, j, k: (k, 0)),
                    ((big2, D_MODEL), lambda i, j, k: (k, j)),
                    [(_sds((N_DEV, D_MODEL, D_MODEL), BF16), (None, D_MODEL, D_MODEL), lambda i, j, k: (j, 0, 0))])[0]
    grad_x, dg_mix = mm("in_proj_bwd", dz, w["w_in"], "nt", (nr, 1, 1), row(IN_W), whole3(w["w_in"]), [row_f32, col_sum],
                        extras=[(x, *row(D_MODEL)), (sm["g_mix"], *vec(D_MODEL)), (dx1, *row(D_MODEL))],
                        epi=_epi_rms_bwd, acc_outs=(1,), split=("sum", N_DEV), b_resident=True)
    small = dict(g_mix=dg_mix, b_gate=db_gate, conv_b=dg_conv_b, conv_ln_g=dg_ln_g, conv_ln_b=dg_ln_b, g_cross=dg_cross,
                 g_mem=dg_mem, g_mlp=dg_mlp, g_final=dg_final, loss=loss, conv_w=dg_conv_w)
    return grad_x, dw, small


SHARD_SHAPE = dict(w_in=(1024, 1024), w_attn_proj=(512, 128), w_conv_proj=(768, 128), w_out=(128, 1024), w_cq=(128, 1024),
                   w_ckv=(1024, 256), w_co=(128, 1024), w_up=(1024, 512), w_down=(512, 1024))
FWD_CARRY = {"rms_mix": ("w_in",), "in_proj": ("w_attn_proj", "w_conv_proj", "w_out", "w_cq", "w_ckv", "w_co", "taps"),
             "qkv_prep": ("w_up",), "conv_fwd": ("w_down",)}
BWD_CARRY = {"dw_up": ("w_down",), "out_proj_bwd_mix": ("w_co", "w_cq"), "conv_bwd": ("w_up", "w_ckv"),
             "attn_bwd_0": ("w_out",), "attn_bwd_1": ("w_attn_proj", "w_conv_proj"), "in_proj_bwd": ("w_in",)}


def _cols_to_2d(a):
    return a.transpose(1, 0, 2).reshape(a.shape[1], -1)


class _Plan:
    def __init__(self, shards, n_tap_cols):
        self.shards = shards
        self.gathered = {}
        self.parts = {}
        self.n_tap_cols = n_tap_cols

    def comm(self, name, dw):
        if name in FWD_CARRY:
            return _Comm(replicated=[self.shards[k] for k in FWD_CARRY[name]])
        if name in BWD_CARRY:
            return _Comm(chunked=[dw[k].reshape((N_DEV,) + SHARD_SHAPE[k]) for k in BWD_CARRY[name]])
        return _NO_COMM

    def done(self, name, got):
        if name in FWD_CARRY:
            self.gathered.update(zip(FWD_CARRY[name], got))
        elif name in BWD_CARRY:
            self.parts.update(zip(BWD_CARRY[name], got))

    def w(self, key):
        g = self.gathered[key]
        if key in ("w_in", "w_up", "w_ckv"):
            return g
        if key in ("w_attn_proj", "w_conv_proj"):
            return _cols_to_2d(g)
        if key == "taps":
            return jnp.pad(_cols_to_2d(g[:, :CONV_K, :self.n_tap_cols]), ((0, 1), (0, 0)))
        return g.reshape(-1, g.shape[-1])


def _adamw(name, w, m, v, parts, comm=_NO_COMM):
    rows, cols = w.shape
    n_parts = parts.shape[0]
    rb = rows if rows <= 256 or rows % 256 else 256

    def body(w_ref, m_ref, v_ref, p_ref, g_ref, d_ref, nm_ref, nv_ref):
        g = p_ref[0].astype(F32)
        for q in range(1, n_parts):
            g = g + p_ref[q].astype(F32)
        wv = w_ref[...]
        nm = ADAM_B1 * m_ref[...] + (1.0 - ADAM_B1) * g
        nv = ADAM_B2 * v_ref[...] + (1.0 - ADAM_B2) * jnp.square(g)
        m_hat = nm / (1.0 - ADAM_B1 ** ADAM_STEP)
        v_hat = nv / (1.0 - ADAM_B2 ** ADAM_STEP)
        g_ref[...] = g
        d_ref[...] = -ADAM_LR * (m_hat / (jnp.sqrt(v_hat) + ADAM_EPS) + ADAM_WD * wv)
        nm_ref[...] = nm
        nv_ref[...] = nv

    blk = pl.BlockSpec((rb, cols), lambda i: (i, 0))
    return _pcall(body, name, (rows // rb,), [w, m, v, parts],
                  [blk, blk, blk, pl.BlockSpec((n_parts, rb, cols), lambda i: (0, i, 0))],
                  [jax.ShapeDtypeStruct((rows, cols), F32)] * 4, [blk] * 4, comm=comm)


def _sum_parts(name, parts):
    def body(p_ref, o_ref):
        acc = p_ref[0]
        for q in range(1, parts.shape[0]):
            acc = acc + p_ref[q]
        o_ref[...] = acc

    return _pcall(body, name, (1,), [parts], [pl.BlockSpec(parts.shape, lambda i: (0, 0, 0))],
                  [jax.ShapeDtypeStruct(parts.shape[1:], F32)], [pl.BlockSpec(parts.shape[1:], lambda i: (0, 0))])[0]


BIG = ("w_in", "w_attn_proj", "w_conv_proj", "w_out", "w_cq", "w_ckv", "w_co", "w_up", "w_down")
SMALL = ("g_mix", "b_gate", "conv_b", "conv_ln_g", "conv_ln_b", "g_cross", "g_mem", "g_mlp", "g_final")
SMALL_ORDER = SMALL + ("loss", "conv_w")
WEIGHTS = ("g_mix", "w_in", "b_gate", "conv_w", "conv_b", "conv_ln_g", "conv_ln_b", "w_attn_proj", "w_conv_proj", "w_out",
           "g_cross", "g_mem", "w_cq", "w_ckv", "w_co", "g_mlp", "w_up", "w_down", "g_final")


def kernel(x, mem, g_mix, w_in, b_gate, conv_w, conv_b, conv_ln_g, conv_ln_b, w_attn_proj, w_conv_proj, w_out, g_cross, g_mem, w_cq, w_ckv, w_co, g_mlp, w_up, w_down, g_final, loss_target, m_g_mix, m_w_in, m_b_gate, m_conv_w, m_conv_b, m_conv_ln_g, m_conv_ln_b, m_w_attn_proj, m_w_conv_proj, m_w_out, m_g_cross, m_g_mem, m_w_cq, m_w_ckv, m_w_co, m_g_mlp, m_w_up, m_w_down, m_g_final, v_g_mix, v_w_in, v_b_gate, v_conv_w, v_conv_b, v_conv_ln_g, v_conv_ln_b, v_w_attn_proj, v_w_conv_proj, v_w_out, v_g_cross, v_g_mem, v_w_cq, v_w_ckv, v_w_co, v_g_mlp, v_w_up, v_w_down, v_g_final):
    args = dict(locals())
    wts = {k: args[k] for k in WEIGHTS}
    mom = {k: args["m_" + k] for k in WEIGHTS}
    var = {k: args["v_" + k] for k in WEIGHTS}
    two_d = lambda a: a.reshape(a.shape[-2:]) if a.ndim == 3 else a.reshape(1, -1)

    shards = {k: two_d(wts[k]).astype(BF16) for k in BIG}
    shards["taps"] = jnp.pad(two_d(conv_w), ((0, 1), (0, HEAD_DIM - conv_w.shape[-1])))
    plan = _Plan(shards, conv_w.shape[-1])
    sm = {k: two_d(wts[k]) for k in SMALL}

    grad_x, _, small = _local_step(x[0], mem[0], loss_target[0], sm, plan)
    parts = plan.parts

    out = {}
    small_comm = _Comm(replicated=[small[k] for k in SMALL_ORDER])
    for k in BIG:
        res = _adamw("adamw_" + k, two_d(wts[k]), two_d(mom[k]), two_d(var[k]), parts[k],
                     comm=small_comm if k == BIG[0] else _NO_COMM)
        out[k] = [r.reshape(wts[k].shape) for r in res[:4]]
        if k == BIG[0]:
            small_parts = dict(zip(SMALL_ORDER, res[4:]))
    for k in SMALL:
        res = _adamw("adamw_" + k, two_d(wts[k]), two_d(mom[k]), two_d(var[k]), small_parts[k])
        out[k] = [r.reshape(wts[k].shape) for r in res]
    loss = _sum_parts("loss_sum", small_parts["loss"])[0, 0]
    me = 4 * lax.axis_index("x") + 2 * lax.axis_index("y") + lax.axis_index("c")
    n_tap_cols = conv_w.shape[-1]
    tap_parts = lax.dynamic_slice(small_parts["conv_w"], (0, 0, me * n_tap_cols), (N_DEV, CONV_K, n_tap_cols))
    res = _adamw("adamw_conv_w", two_d(conv_w), two_d(m_conv_w), two_d(v_conv_w), tap_parts)
    out["conv_w"] = [r.reshape(conv_w.shape) for r in res]

    return (loss, grad_x[None], *[out[k][0] for k in WEIGHTS], *[out[k][1] for k in WEIGHTS],
            *[out[k][2] for k in WEIGHTS], *[out[k][3] for k in WEIGHTS])
```

```python
import functools

import jax
import jax.numpy as jnp
import numpy as np
from jax import lax
from jax.experimental import pallas as pl
from jax.experimental.pallas import tpu as pltpu

F32 = jnp.float32
BF16 = jnp.bfloat16

N_DEV = 8
D_MODEL = 1024
N_MEM = 256
HEAD_DIM = 128
HEADS_PER_GROUP = 4
GROUP_W = HEADS_PER_GROUP * HEAD_DIM
DILATIONS = (1, 4, 16)
BAND = 128
N_GROUPS = 3
ATTN_W = N_GROUPS * GROUP_W
QKV_W = 3 * ATTN_W
ROT_DIM = HEAD_DIM // 4
ROPE_THETA = 500000.0
CONV_CH = 768
CONV_K = 31
CONV_HALO = 32
SUBLANES = 8
CONV_ROWS = 64
IN_W = 8192
GLU_COL_BLK = QKV_W // (2 * CONV_CH)
GATE_COL_BLK = (QKV_W + 2 * CONV_CH) // (2 * D_MODEL)
CROSS_HEADS = 4
CROSS_HD = D_MODEL // CROSS_HEADS
D_FF = 4096
EPS = 1e-6
NEG = -1e30
QB = 4
ROW_BLK = QB * BAND

ADAM_LR = 0.001
ADAM_B1 = 0.9
ADAM_B2 = 0.999
ADAM_EPS = 1e-08
ADAM_WD = 0.01
ADAM_STEP = 10

VMEM_LIMIT = 56 * 1024 * 1024
MESH = pl.DeviceIdType.MESH


def _params(**kw):
    return pltpu.CompilerParams(vmem_limit_bytes=VMEM_LIMIT, **kw)


def _sigmoid(x):
    return 1.0 / (1.0 + jnp.exp(-x))


def _dot(a, b, kind):
    dims = {"nn": (((1,), (0,)), ((), ())), "nt": (((1,), (1,)), ((), ())), "tn": (((0,), (0,)), ((), ()))}[kind]
    if a.dtype != BF16:
        a = a.astype(BF16)
    if b.dtype != BF16:
        b = b.astype(BF16)
    return lax.dot_general(a, b, dims, preferred_element_type=F32)


def _peers():
    x, y, c = lax.axis_index("x"), lax.axis_index("y"), lax.axis_index("c")
    me = 4 * x + 2 * y + c
    peers = [(x, y, 1 - c), (1 - x, y, c), (x, 1 - y, c), (1 - x, 1 - y, c),
             (1 - x, y, 1 - c), (x, 1 - y, 1 - c), (1 - x, 1 - y, 1 - c)]
    return me, peers


class _Comm:
    def __init__(self, chunked=(), replicated=()):
        self.arrays = list(chunked) + list(replicated)
        self.n_c = len(chunked)
        self.n = len(self.arrays)
        self.out_shape = [jax.ShapeDtypeStruct(a.shape, a.dtype) for a in chunked]
        self.out_shape += [jax.ShapeDtypeStruct((N_DEV,) + a.shape, a.dtype) for a in replicated]
        self.in_specs = [pl.BlockSpec(memory_space=pl.ANY)] * self.n
        self.out_specs = [pl.BlockSpec(memory_space=pl.ANY)] * self.n
        self.scratch = [pltpu.SemaphoreType.DMA((self.n,))] * 5 if self.n else []

    @staticmethod
    def _where():
        x, y, c = lax.axis_index("x"), lax.axis_index("y"), lax.axis_index("c")
        chips = [(1 - x, y), (x, 1 - y), (1 - x, 1 - y)]
        return (x, y, c), 4 * x + 2 * y + c, chips

    def _local(self, ins, outs, sems, a, me):
        src = ins[a].at[me] if a < self.n_c else ins[a]
        return pltpu.make_async_copy(src, outs[a].at[me], sems[2].at[a])

    @staticmethod
    def _remote(src, dst, send, recv, to):
        return pltpu.make_async_remote_copy(src_ref=src, dst_ref=dst, send_sem=send, recv_sem=recv, device_id=to,
                                            device_id_type=MESH)

    def start(self, ins, outs, sems):
        (x, y, c), me, chips = self._where()
        for a in range(self.n):
            self._local(ins, outs, sems, a, me).start()
            if a < self.n_c:
                for (px, py, pc) in _peers()[1]:
                    self._remote(ins[a].at[4 * px + 2 * py + pc], outs[a].at[me], sems[0].at[a], sems[1].at[a], (px, py, pc)).start()
            else:
                self._remote(ins[a], outs[a].at[me], sems[3].at[a], sems[4].at[a], (x, y, 1 - c)).start()
                for (px, py) in chips:
                    self._remote(ins[a], outs[a].at[me], sems[0].at[a], sems[1].at[a], (px, py, c)).start()

    def wait(self, ins, outs, sems):
        (x, y, c), me, chips = self._where()
        sibling = (x, y, 1 - c)

        def drain(a, pair, count):
            blocks = outs[a].at[pl.ds(0, count)]
            cp = self._remote(blocks, blocks, sems[pair].at[a], sems[pair + 1].at[a], sibling)
            cp.wait_send()
            cp.wait_recv()

        for a in range(self.n):
            if a < self.n_c:
                drain(a, 0, N_DEV - 1)
            else:
                drain(a, 0, len(chips))
                for (px, py) in chips:
                    blk = outs[a].at[4 * px + 2 * py + c]
                    self._remote(blk, blk, sems[3].at[a], sems[4].at[a], sibling).start()
        for a in range(self.n):
            if a >= self.n_c:
                drain(a, 3, len(chips) + 1)
            self._local(ins, outs, sems, a, me).wait()


_NO_COMM = _Comm()


def _pcall(body, name, grid, operands, in_specs, out_shape, out_specs, scratch=(), aliases=None, comm=_NO_COMM):
    n_in, n_out, n_scr = len(operands), len(out_shape), len(scratch)
    grid = tuple(grid)

    def carried(*refs):
        ins, c_in = refs[:n_in], refs[n_in:n_in + comm.n]
        o0 = n_in + comm.n
        outs, c_out = refs[o0:o0 + n_out], refs[o0 + n_out:o0 + n_out + comm.n]
        s0 = o0 + n_out + comm.n
        scr, sems = refs[s0:s0 + n_scr], refs[s0 + n_scr:]
        ids = [pl.program_id(ax) for ax in range(len(grid))]

        @pl.when(functools.reduce(jnp.logical_and, [p == 0 for p in ids]))
        def _():
            comm.start(c_in, c_out, sems)

        body(*ins, *outs, *scr)

        @pl.when(functools.reduce(jnp.logical_and, [p == g - 1 for p, g in zip(ids, grid)]))
        def _():
            comm.wait(c_in, c_out, sems)

    return pl.pallas_call(
        carried if comm.n else body, name=name, grid=grid, in_specs=list(in_specs) + comm.in_specs,
        out_shape=list(out_shape) + comm.out_shape, out_specs=list(out_specs) + comm.out_specs,
        scratch_shapes=list(scratch) + comm.scratch, input_output_aliases=aliases or {},
        compiler_params=_params(dimension_semantics=("arbitrary",) * len(grid)),
    )(*operands, *comm.arrays)


def _mm(name, a, b, kind, grid, a_blk, b_blk, outs, extras=(), epi=None, acc_outs=(), j_outer=False, comm=_NO_COMM,
        split=None, b_resident=False, out_chunks=0):
    gi, gj, gk = grid
    n_ex = len(extras)
    n_out = len(outs)
    mode, n_chunks = split if split is not None else (None, 1)

    def spec(blk, fn, **kw):
        return pl.BlockSpec(blk, (lambda j, i, k: fn(i, j, k)) if j_outer else fn, **kw)

    def b_chunk(b_ref, c):
        if len(b_ref.shape) == 3:
            return b_ref[c]
        rows, cols = b_ref.shape
        if (kind == "nn") == (mode == "cols"):
            return b_ref[:, c * (cols // n_chunks):(c + 1) * (cols // n_chunks)]
        return b_ref[c * (rows // n_chunks):(c + 1) * (rows // n_chunks), :]

    def col_chunk(ref, c):
        width = ref.shape[-1] // n_chunks
        return slice(c * width, (c + 1) * width)

    def body(*refs):
        a_ref, b_ref = refs[0], refs[1]
        ex = refs[2:2 + n_ex]
        out_refs = refs[2 + n_ex:2 + n_ex + n_out]
        acc_ref = refs[2 + n_ex + n_out] if gk > 1 else None
        i = pl.program_id(1 if j_outer else 0)
        k = pl.program_id(2)
        if mode == "cols":
            a_val = a_ref[...]
            for c in range(n_chunks):
                acc = _dot(a_val, b_chunk(b_ref, c), kind)
                vals = epi(acc, *[e[:, col_chunk(e, c)] for e in ex]) if epi is not None else (acc,)
                for o, v in zip(out_refs, vals):
                    o[:, col_chunk(o, c)] = v.astype(o.dtype)
            return
        if mode == "sum":
            part = _dot(a_ref[:, col_chunk(a_ref, 0)], b_chunk(b_ref, 0), kind)
            for c in range(1, n_chunks):
                part = part + _dot(a_ref[:, col_chunk(a_ref, c)], b_chunk(b_ref, c), kind)
        else:
            part = _dot(a_ref[...], b_ref[...], kind)

        def finish(acc):
            if out_chunks:
                width = acc.shape[-1] // out_chunks
                for c in range(out_chunks):
                    out_refs[0][c] = acc[:, c * width:(c + 1) * width].astype(out_refs[0].dtype)
                return
            vals = epi(acc, *[e[...] for e in ex]) if epi is not None else (acc,)
            for idx, (o, v) in enumerate(zip(out_refs, vals)):
                if idx in acc_outs:
                    @pl.when(i == 0)
                    def _():
                        o[...] = v.astype(o.dtype)

                    @pl.when(i != 0)
                    def _():
                        o[...] += v.astype(o.dtype)
                else:
                    o[...] = v.astype(o.dtype)

        if gk == 1:
            finish(part)
        else:
            @pl.when(k == 0)
            def _():
                acc_ref[...] = part

            @pl.when(k != 0)
            def _():
                acc_ref[...] += part

            @pl.when(k == gk - 1)
            def _():
                finish(acc_ref[...])

    scratch = []
    if gk > 1:
        tm = a_blk[0][-1] if kind == "tn" else a_blk[0][-2]
        tn = b_blk[0][-2] if kind == "nt" else b_blk[0][-1]
        scratch = [pltpu.VMEM((tm, tn), F32)]
    b_kw = dict(pipeline_mode=pl.Buffered(1)) if b_resident else {}
    return _pcall(body, name, (gj, gi, gk) if j_outer else (gi, gj, gk), [a, b] + [e for e, _, _ in extras],
                  [spec(*a_blk), spec(*b_blk, **b_kw)] + [spec(blk, fn) for _, blk, fn in extras],
                  [s for s, _, _ in outs], [spec(blk, fn) for _, blk, fn in outs], scratch, comm=comm)


def _rms_fwd_vals(x, g):
    r = lax.rsqrt(jnp.mean(x * x, axis=-1, keepdims=True) + EPS)
    return x * r * g


def _rms_bwd_vals(x, g, du):
    r = lax.rsqrt(jnp.mean(x * x, axis=-1, keepdims=True) + EPS)
    xh = x * r
    dxh = du * g
    dx = r * (dxh - xh * jnp.mean(dxh * xh, axis=-1, keepdims=True))
    return dx, jnp.sum(du * xh, axis=0, keepdims=True)


def _rms_fwd(name, x, g, rows, comm=_NO_COMM):
    n = x.shape[0]

    def body(x_ref, g_ref, o_ref):
        o_ref[...] = _rms_fwd_vals(x_ref[...], g_ref[...]).astype(BF16)

    return _pcall(body, name, (n // rows,), [x, g],
                  [pl.BlockSpec((rows, D_MODEL), lambda i: (i, 0)), pl.BlockSpec((1, D_MODEL), lambda i: (0, 0))],
                  [jax.ShapeDtypeStruct(x.shape, BF16)], [pl.BlockSpec((rows, D_MODEL), lambda i: (i, 0))], comm=comm)


def _rope_tables(seq):
    half = ROT_DIM // 2
    pos = np.arange(seq, dtype=np.float32)
    inv_freq = np.float32(ROPE_THETA) ** (-np.arange(0, ROT_DIM, 2, dtype=np.float32) / np.float32(ROT_DIM))
    ang = (pos[:, None] * inv_freq[None, :]).astype(np.float32)
    cos, sin = np.cos(ang), np.sin(ang)
    rest = HEAD_DIM - ROT_DIM
    c = np.concatenate([cos, cos, np.ones((seq, rest), np.float32)], axis=1)
    s1 = np.concatenate([np.zeros((seq, half), np.float32), sin, np.zeros((seq, rest), np.float32)], axis=1)
    s2 = np.concatenate([-sin, np.zeros((seq, half + rest), np.float32)], axis=1)
    return jnp.asarray(c), jnp.asarray(s1), jnp.asarray(s2)


def _group_shapes(seq, width, dtype):
    return [jax.ShapeDtypeStruct((d, seq // d, width), dtype) for d in DILATIONS]


def _group_specs(width):
    return [pl.BlockSpec((d, ROW_BLK // d, width), lambda i: (0, i, 0)) for d in DILATIONS]


def _qkv_prep(z, tabs, comm=_NO_COMM):
    seq = z.shape[0]

    def body(z_ref, c_ref, s1_ref, s2_ref, a0, a1, a2, sc):
        outs = (a0, a1, a2)
        c, s1, s2 = c_ref[...], s1_ref[...], s2_ref[...]
        for part in range(3):
            for hh in range(N_GROUPS * HEADS_PER_GROUP):
                g, hl = divmod(hh, HEADS_PER_GROUP)
                col = part * ATTN_W + hh * HEAD_DIM
                ocol = part * GROUP_W + hl * HEAD_DIM
                x = z_ref[:, col:col + HEAD_DIM].astype(F32)
                if part < 2:
                    x = x * c + pltpu.roll(x, ROT_DIM // 2, 1) * s1 + pltpu.roll(x, HEAD_DIM - ROT_DIM // 2, 1) * s2
                d = DILATIONS[g]
                if d == 1:
                    outs[g][0, :, ocol:ocol + HEAD_DIM] = x.astype(BF16)
                else:
                    sc[...] = x
                    for r in range(d):
                        outs[g][r, :, ocol:ocol + HEAD_DIM] = sc[pl.ds(r, ROW_BLK // d, stride=d), :].astype(BF16)

    tab_spec = pl.BlockSpec((ROW_BLK, HEAD_DIM), lambda i: (i, 0))
    return _pcall(body, "qkv_prep", (seq // ROW_BLK,), [z, *tabs],
                  [pl.BlockSpec((ROW_BLK, QKV_W), lambda i: (i, 0)), tab_spec, tab_spec, tab_spec],
                  _group_shapes(seq, ATTN_W, BF16), _group_specs(ATTN_W), [pltpu.VMEM((ROW_BLK, HEAD_DIM), F32)], comm=comm)


def _band_masks_2(t):
    qi = lax.broadcasted_iota(jnp.int32, (BAND, 2 * BAND), 0)
    kj = lax.broadcasted_iota(jnp.int32, (BAND, 2 * BAND), 1)
    band = jnp.logical_and(kj >= qi, kj <= qi + BAND)
    return band, jnp.logical_and(band, jnp.logical_or(kj >= BAND, t > 0))


def _attn_fwd(name, a_g, comm=_NO_COMM):
    dil, m_len, _ = a_g.shape
    qb = min(QB, m_len // BAND)
    rows = qb * BAND
    steps = m_len // rows
    scale = HEAD_DIM ** -0.5

    tiles = [(sb, h) for sb in range(qb) for h in range(HEADS_PER_GROUP)]

    def body(q_ref, kc_ref, vc_ref, kp_ref, vp_ref, o_ref, l_ref, k_all, v_all, s_scr, p_scr, r_scr):
        t = pl.program_id(1)
        k_all[0:BAND, :] = kp_ref[...]
        k_all[BAND:, :] = kc_ref[...]
        v_all[0:BAND, :] = vp_ref[...]
        v_all[BAND:, :] = vc_ref[...]
        band, band_first = _band_masks_2(t)
        for idx, (sb, h) in enumerate(tiles):
            cs = slice(h * HEAD_DIM, (h + 1) * HEAD_DIM)
            s = _dot(q_ref[sb * BAND:(sb + 1) * BAND, cs], k_all[sb * BAND:(sb + 2) * BAND, cs], "nt") * scale
            s_scr[idx] = jnp.where(band_first if sb == 0 else band, s, NEG)
        for idx, (sb, h) in enumerate(tiles):
            cs = slice(h * HEAD_DIM, (h + 1) * HEAD_DIM)
            s = s_scr[idx]
            mx = jnp.max(s, axis=-1, keepdims=True)
            p = jnp.exp(s - mx)
            den = jnp.sum(p, axis=-1, keepdims=True)
            p_scr[idx] = p.astype(BF16)
            r_scr[idx] = jnp.broadcast_to(1.0 / den, (BAND, HEAD_DIM))
            l_ref[sb * BAND:(sb + 1) * BAND, cs] = jnp.broadcast_to(mx + jnp.log(den), (BAND, HEAD_DIM))
        for idx, (sb, h) in enumerate(tiles):
            cs = slice(h * HEAD_DIM, (h + 1) * HEAD_DIM)
            o_ref[sb * BAND:(sb + 1) * BAND, cs] = _dot(p_scr[idx], v_all[sb * BAND:(sb + 2) * BAND, cs], "nn") * r_scr[idx]

    def prev(r, t):
        return jnp.maximum(qb * t - 1, 0)

    cur = lambda c: pl.BlockSpec((None, rows, GROUP_W), lambda r, t, c=c: (r, t, c))
    prv = lambda c: pl.BlockSpec((None, BAND, GROUP_W), lambda r, t, c=c: (r, prev(r, t), c))
    out_spec = pl.BlockSpec((None, rows, GROUP_W), lambda r, t: (r, t, 0))
    shp = jax.ShapeDtypeStruct((dil, m_len, GROUP_W), F32)
    n_t = len(tiles)
    return _pcall(body, name, (dil, steps), [a_g] * 5, [cur(0), cur(1), cur(2), prv(1), prv(2)], [shp, shp],
                  [out_spec, out_spec],
                  [pltpu.VMEM((rows + BAND, GROUP_W), BF16), pltpu.VMEM((rows + BAND, GROUP_W), BF16),
                   pltpu.VMEM((n_t, BAND, 2 * BAND), F32), pltpu.VMEM((n_t, BAND, 2 * BAND), BF16),
                   pltpu.VMEM((n_t, BAND, HEAD_DIM), F32)], comm=comm)


def _attn_merge(os_, ls_, seq):
    def body(o0, l0, o1, l1, o2, l2, at_ref, lt_ref, sc):
        for h in range(HEADS_PER_GROUP):
            cs = slice(h * HEAD_DIM, (h + 1) * HEAD_DIM)
            for gi, (o_r, l_r) in enumerate(((o1, l1), (o2, l2))):
                d = DILATIONS[gi + 1]
                for r in range(d):
                    sc.at[2 * gi][pl.ds(r, ROW_BLK // d, stride=d), :] = o_r[r, :, cs]
                    sc.at[2 * gi + 1][pl.ds(r, ROW_BLK // d, stride=d), :] = l_r[r, :, cs]
            o0v, l0v = o0[0, :, cs], l0[0, :, cs]
            o1v, l1v, o2v, l2v = sc[0], sc[1], sc[2], sc[3]
            mx = jnp.maximum(jnp.maximum(l0v, l1v), l2v)
            e0, e1, e2 = jnp.exp(l0v - mx), jnp.exp(l1v - mx), jnp.exp(l2v - mx)
            tot = e0 + e1 + e2
            at_ref[:, cs] = ((e0 * o0v + e1 * o1v + e2 * o2v) / tot).astype(BF16)
            lt_ref[:, cs] = mx + jnp.log(tot)

    gs = _group_specs(GROUP_W)
    row = pl.BlockSpec((ROW_BLK, GROUP_W), lambda i: (i, 0))
    return pl.pallas_call(
        body, name="attn_merge",
        out_shape=[jax.ShapeDtypeStruct((seq, GROUP_W), BF16), jax.ShapeDtypeStruct((seq, GROUP_W), F32)],
        grid=(seq // ROW_BLK,), in_specs=[gs[0], gs[0], gs[1], gs[1], gs[2], gs[2]], out_specs=[row, row],
        scratch_shapes=[pltpu.VMEM((4, ROW_BLK, HEAD_DIM), F32)],
        compiler_params=_params(dimension_semantics=("arbitrary",)),
    )(os_[0], ls_[0], os_[1], ls_[1], os_[2], ls_[2])


def _attn_bwd_prep(dattn, attn, lt):
    seq = dattn.shape[0]

    def body(da_ref, at_ref, lt_ref, d0, c0, t0, d1, c1, t1, d2, c2, t2, sc):
        outs = ((d0, c0, t0), (d1, c1, t1), (d2, c2, t2))
        for h in range(HEADS_PER_GROUP):
            cs = slice(h * HEAD_DIM, (h + 1) * HEAD_DIM)
            da = da_ref[:, cs].astype(F32)
            cc = jnp.broadcast_to(jnp.sum(da * at_ref[:, cs].astype(F32), axis=-1, keepdims=True), (ROW_BLK, HEAD_DIM))
            ltv = lt_ref[:, cs]
            d0[0, :, cs] = da_ref[:, cs]
            c0[0, :, cs] = cc
            t0[0, :, cs] = ltv
            sc[0], sc[1], sc[2] = da, cc, ltv
            for g in (1, 2):
                d = DILATIONS[g]
                for r in range(d):
                    rows = pl.ds(r, ROW_BLK // d, stride=d)
                    outs[g][0][r, :, cs] = sc.at[0][rows, :].astype(BF16)
                    outs[g][1][r, :, cs] = sc.at[1][rows, :]
                    outs[g][2][r, :, cs] = sc.at[2][rows, :]

    gs = _group_specs(GROUP_W)
    row = pl.BlockSpec((ROW_BLK, GROUP_W), lambda i: (i, 0))
    shapes, specs = [], []
    for g, d in enumerate(DILATIONS):
        for dt in (BF16, F32, F32):
            shapes.append(jax.ShapeDtypeStruct((d, seq // d, GROUP_W), dt))
            specs.append(gs[g])
    res = pl.pallas_call(
        body, name="attn_bwd_prep", out_shape=shapes, grid=(seq // ROW_BLK,), in_specs=[row, row, row],
        out_specs=specs, scratch_shapes=[pltpu.VMEM((3, ROW_BLK, HEAD_DIM), F32)],
        compiler_params=_params(dimension_semantics=("arbitrary",)),
    )(dattn, attn, lt)
    return [res[3 * g:3 * g + 3] for g in range(N_GROUPS)]


def _attn_bwd(name, a_g, da_g, c_g, lt_g, comm=_NO_COMM):
    dil, m_len, _ = a_g.shape
    qb = min(QB, m_len // BAND)
    rows = qb * BAND
    steps = m_len // rows
    scale = HEAD_DIM ** -0.5

    tiles = [(sb, h) for sb in range(qb) for h in range(HEADS_PER_GROUP)]

    def body(q_ref, kc_ref, vc_ref, kp_ref, vp_ref, da_ref, c_ref, lt_ref, d_ref, dk_acc, dv_acc, car_k, car_v,
             k_all, v_all, s_scr, dp_scr, p_scr, ds_scr):
        tg = pl.program_id(1)
        t = steps - 1 - tg

        @pl.when(tg == 0)
        def _():
            car_k[...] = jnp.zeros_like(car_k)
            car_v[...] = jnp.zeros_like(car_v)

        k_all[0:BAND, :] = kp_ref[...]
        k_all[BAND:, :] = kc_ref[...]
        v_all[0:BAND, :] = vp_ref[...]
        v_all[BAND:, :] = vc_ref[...]
        zero = jnp.zeros((rows, GROUP_W), F32)
        dk_acc[0:rows, :] = zero
        dv_acc[0:rows, :] = zero
        dk_acc[rows:rows + BAND, :] = car_k[...]
        dv_acc[rows:rows + BAND, :] = car_v[...]
        band, band_first = _band_masks_2(t)
        for idx, (sb, h) in enumerate(tiles):
            cs = slice(h * HEAD_DIM, (h + 1) * HEAD_DIM)
            rs, ks = slice(sb * BAND, (sb + 1) * BAND), slice(sb * BAND, (sb + 2) * BAND)
            s_scr[idx] = _dot(q_ref[rs, cs], k_all[ks, cs], "nt")
            dp_scr[idx] = _dot(da_ref[rs, cs], v_all[ks, cs], "nt")
        for idx, (sb, h) in enumerate(tiles):
            cs = slice(h * HEAD_DIM, (h + 1) * HEAD_DIM)
            rs = slice(sb * BAND, (sb + 1) * BAND)
            ltv = jnp.concatenate([lt_ref[rs, cs]] * 2, axis=1)
            cc = jnp.concatenate([c_ref[rs, cs]] * 2, axis=1)
            p = jnp.exp(jnp.where(band_first if sb == 0 else band, s_scr[idx] * scale - ltv, NEG))
            p_scr[idx] = p.astype(BF16)
            ds_scr[idx] = (p * (dp_scr[idx] - cc) * scale).astype(BF16)
        for idx, (sb, h) in enumerate(tiles):
            cs = slice(h * HEAD_DIM, (h + 1) * HEAD_DIM)
            rs, ks = slice(sb * BAND, (sb + 1) * BAND), slice(sb * BAND, (sb + 2) * BAND)
            d_ref[rs, cs] = _dot(ds_scr[idx], k_all[ks, cs], "nn").astype(BF16)
            dk_acc[ks, cs] += _dot(ds_scr[idx], q_ref[rs, cs], "tn")
            dv_acc[ks, cs] += _dot(p_scr[idx], da_ref[rs, cs], "tn")
        d_ref[:, GROUP_W:2 * GROUP_W] = dk_acc[BAND:rows + BAND, :].astype(BF16)
        d_ref[:, 2 * GROUP_W:3 * GROUP_W] = dv_acc[BAND:rows + BAND, :].astype(BF16)
        car_k[...] = dk_acc[0:BAND, :]
        car_v[...] = dv_acc[0:BAND, :]

    def rev(tg):
        return steps - 1 - tg

    def prev(tg):
        return jnp.maximum(qb * rev(tg) - 1, 0)

    cur = lambda c: pl.BlockSpec((None, rows, GROUP_W), lambda r, tg, c=c: (r, rev(tg), c))
    prv = lambda c: pl.BlockSpec((None, BAND, GROUP_W), lambda r, tg, c=c: (r, prev(tg), c))
    return _pcall(
        body, name, (dil, steps), [a_g, a_g, a_g, a_g, a_g, da_g, c_g, lt_g],
        [cur(0), cur(1), cur(2), prv(1), prv(2), cur(0), cur(0), cur(0)],
        [jax.ShapeDtypeStruct((dil, m_len, ATTN_W), BF16)], [pl.BlockSpec((None, rows, ATTN_W), lambda r, tg: (r, rev(tg), 0))],
        [pltpu.VMEM((rows + BAND, GROUP_W), F32), pltpu.VMEM((rows + BAND, GROUP_W), F32),
         pltpu.VMEM((BAND, GROUP_W), F32), pltpu.VMEM((BAND, GROUP_W), F32),
         pltpu.VMEM((rows + BAND, GROUP_W), BF16), pltpu.VMEM((rows + BAND, GROUP_W), BF16),
         pltpu.VMEM((len(tiles), BAND, 2 * BAND), F32), pltpu.VMEM((len(tiles), BAND, 2 * BAND), F32),
         pltpu.VMEM((len(tiles), BAND, 2 * BAND), BF16), pltpu.VMEM((len(tiles), BAND, 2 * BAND), BF16)], comm=comm)


def _dqkv_post(d_gs, tabs, dz):
    seq = dz.shape[0]

    def body(g0, g1, g2, c_ref, s1_ref, s2_ref, dz_any, o_ref, sc):
        del dz_any
        ins = (g0, g1, g2)
        c, s1, s2 = c_ref[...], s1_ref[...], s2_ref[...]
        for part in range(3):
            for hh in range(N_GROUPS * HEADS_PER_GROUP):
                g, hl = divmod(hh, HEADS_PER_GROUP)
                icol = part * GROUP_W + hl * HEAD_DIM
                ocol = part * ATTN_W + hh * HEAD_DIM
                d = DILATIONS[g]
                if d == 1:
                    x = ins[g][0, :, icol:icol + HEAD_DIM].astype(F32)
                else:
                    for r in range(d):
                        sc[pl.ds(r, ROW_BLK // d, stride=d), :] = ins[g][r, :, icol:icol + HEAD_DIM].astype(F32)
                    x = sc[...]
                if part < 2:
                    x = x * c + pltpu.roll(x * s1, HEAD_DIM - ROT_DIM // 2, 1) + pltpu.roll(x * s2, ROT_DIM // 2, 1)
                o_ref[:, ocol:ocol + HEAD_DIM] = x.astype(BF16)

    tab_spec = pl.BlockSpec((ROW_BLK, HEAD_DIM), lambda i: (i, 0))
    return pl.pallas_call(
        body, name="dqkv_post", out_shape=jax.ShapeDtypeStruct(dz.shape, BF16), grid=(seq // ROW_BLK,),
        in_specs=_group_specs(ATTN_W) + [tab_spec, tab_spec, tab_spec, pl.BlockSpec(memory_space=pl.ANY)],
        out_specs=pl.BlockSpec((ROW_BLK, QKV_W), lambda i: (i, 0)),
        scratch_shapes=[pltpu.VMEM((ROW_BLK, HEAD_DIM), F32)], input_output_aliases={6: 0},
        compiler_params=_params(dimension_semantics=("arbitrary",)),
    )(*d_gs, *tabs, dz)


def _glu(zg):
    a = zg[:, :CONV_CH].astype(F32)
    s = _sigmoid(zg[:, CONV_CH:].astype(F32))
    return a, s, a * s


def _shifted_copies(xs):
    n = xs.shape[1] - SUBLANES
    for b in range(1, SUBLANES):
        xs[b, 0:n, :] = xs[0, pl.ds(b, n), :]


def _shifted(xs, offset, r0, cs):
    a, b = divmod(offset, SUBLANES)
    return xs[b, pl.ds(SUBLANES * a + r0, CONV_ROWS), cs]


def _conv_fwd(z, cw, cb, lg, lb, comm=_NO_COMM):
    seq = z.shape[0]
    halo_per_blk = ROW_BLK // CONV_HALO

    def body(zg_ref, zh_ref, cw_ref, cb_ref, lg_ref, lb_ref, c2_ref, c4_ref, xs):
        i = pl.program_id(0)
        _, _, c1 = _glu(zg_ref[...])
        _, _, c1h = _glu(zh_ref[...])
        xs[0, 0:CONV_HALO, :] = jnp.where(i > 0, c1h, 0.0)
        xs[0, CONV_HALO:, :] = c1
        _shifted_copies(xs)
        for s in range(CONV_CH // HEAD_DIM):
            cs = slice(s * HEAD_DIM, (s + 1) * HEAD_DIM)
            taps = [cw_ref[j:j + 1, cs] for j in range(CONV_K)]
            bias = cb_ref[:, cs]

            def chunk(rc, carry, cs=cs, taps=taps, bias=bias):
                r0 = pl.multiple_of(rc * CONV_ROWS, CONV_ROWS)
                acc = jnp.zeros((CONV_ROWS, HEAD_DIM), F32)
                for j in range(CONV_K):
                    acc = acc + taps[j] * _shifted(xs, CONV_HALO - (CONV_K - 1) + j, r0, cs)
                c2_ref[pl.ds(r0, CONV_ROWS), cs] = acc + bias
                return carry

            lax.fori_loop(0, ROW_BLK // CONV_ROWS, chunk, 0)
        c2 = c2_ref[...]
        mu = jnp.mean(c2, axis=-1, keepdims=True)
        xc = c2 - mu
        rstd = lax.rsqrt(jnp.mean(xc * xc, axis=-1, keepdims=True) + EPS)
        c3 = xc * rstd * lg_ref[...] + lb_ref[...]
        c4_ref[...] = (c3 * _sigmoid(c3)).astype(BF16)

    vec = pl.BlockSpec((1, CONV_CH), lambda i: (0, 0))
    return _pcall(
        body, "conv_fwd", (seq // ROW_BLK,), [z, z, cw, cb, lg, lb],
        [pl.BlockSpec((ROW_BLK, 2 * CONV_CH), lambda i: (i, GLU_COL_BLK)),
         pl.BlockSpec((CONV_HALO, 2 * CONV_CH), lambda i: (jnp.maximum(i * halo_per_blk - 1, 0), GLU_COL_BLK)),
         pl.BlockSpec((CONV_HALO, CONV_CH), lambda i: (0, 0)), vec, vec, vec],
        [jax.ShapeDtypeStruct((seq, CONV_CH), F32), jax.ShapeDtypeStruct((seq, CONV_CH), BF16)],
        [pl.BlockSpec((ROW_BLK, CONV_CH), lambda i: (i, 0)), pl.BlockSpec((ROW_BLK, CONV_CH), lambda i: (i, 0))],
        [pltpu.VMEM((SUBLANES, ROW_BLK + CONV_HALO, CONV_CH), F32)], comm=comm)


def _conv_bwd(dc2, z, cw, dz, comm=_NO_COMM):
    seq = z.shape[0]
    halo_per_blk = ROW_BLK // CONV_HALO
    n_blk = seq // ROW_BLK
    last_halo = seq // CONV_HALO - 1

    def body(dc_ref, dn_ref, zg_ref, zh_ref, cw_ref, dz_any, o_ref, dcw_ref, xs, ys, dc1_ref, dcw_acc):
        del dz_any
        i = pl.program_id(0)
        a, s, c1 = _glu(zg_ref[...])
        _, _, c1h = _glu(zh_ref[...])
        xs[0, 0:CONV_HALO, :] = jnp.where(i > 0, c1h, 0.0)
        xs[0, CONV_HALO:, :] = c1
        ys[0, 0:ROW_BLK, :] = dc_ref[...]
        ys[0, ROW_BLK:, :] = jnp.where(i < n_blk - 1, dn_ref[...], 0.0)
        _shifted_copies(xs)
        _shifted_copies(ys)

        @pl.when(i == 0)
        def _():
            dcw_acc[...] = jnp.zeros_like(dcw_acc)

        for sl in range(CONV_CH // HEAD_DIM):
            cs = slice(sl * HEAD_DIM, (sl + 1) * HEAD_DIM)
            taps = [cw_ref[j:j + 1, cs] for j in range(CONV_K)]

            def chunk(rc, carry, cs=cs, taps=taps):
                r0 = pl.multiple_of(rc * CONV_ROWS, CONV_ROWS)
                dc = ys[0, pl.ds(r0, CONV_ROWS), cs]
                acc = jnp.zeros((CONV_ROWS, HEAD_DIM), F32)
                for j in range(CONV_K):
                    prod = dc * _shifted(xs, CONV_HALO - (CONV_K - 1) + j, r0, cs)
                    dcw_acc[j, :, cs] += jnp.sum(prod.reshape(CONV_ROWS // SUBLANES, SUBLANES, HEAD_DIM), axis=0)
                    acc = acc + taps[j] * _shifted(ys, CONV_K - 1 - j, r0, cs)
                dc1_ref[pl.ds(r0, CONV_ROWS), cs] = acc
                return carry

            lax.fori_loop(0, ROW_BLK // CONV_ROWS, chunk, 0)
        dc1 = dc1_ref[...]
        o_ref[:, :CONV_CH] = (dc1 * s).astype(BF16)
        o_ref[:, CONV_CH:] = (dc1 * a * s * (1.0 - s)).astype(BF16)

        @pl.when(i == n_blk - 1)
        def _():
            dcw_ref[...] = jnp.sum(dcw_acc[...], axis=1)

    return _pcall(
        body, "conv_bwd", (n_blk,), [dc2, dc2, z, z, cw, dz],
        [pl.BlockSpec((ROW_BLK, CONV_CH), lambda i: (i, 0)),
         pl.BlockSpec((CONV_HALO, CONV_CH), lambda i: (jnp.minimum((i + 1) * halo_per_blk, last_halo), 0)),
         pl.BlockSpec((ROW_BLK, 2 * CONV_CH), lambda i: (i, GLU_COL_BLK)),
         pl.BlockSpec((CONV_HALO, 2 * CONV_CH), lambda i: (jnp.maximum(i * halo_per_blk - 1, 0), GLU_COL_BLK)),
         pl.BlockSpec((CONV_HALO, CONV_CH), lambda i: (0, 0)),
         pl.BlockSpec(memory_space=pl.ANY)],
        [jax.ShapeDtypeStruct(dz.shape, BF16), jax.ShapeDtypeStruct((CONV_HALO, CONV_CH), F32)],
        [pl.BlockSpec((ROW_BLK, 2 * CONV_CH), lambda i: (i, GLU_COL_BLK)), pl.BlockSpec((CONV_HALO, CONV_CH), lambda i: (0, 0))],
        [pltpu.VMEM((SUBLANES, ROW_BLK + CONV_HALO, CONV_CH), F32), pltpu.VMEM((SUBLANES, ROW_BLK + CONV_HALO, CONV_CH), F32),
         pltpu.VMEM((ROW_BLK, CONV_CH), F32), pltpu.VMEM((CONV_HALO, SUBLANES, CONV_CH), F32)],
        aliases={5: 0}, comm=comm)


def _epi_mix(ya, c4, wcp, gates, bg):
    yc = _dot(c4, wcp, "nn")
    gv = _sigmoid(gates.astype(F32) + bg)
    merged = gv[:, :D_MODEL] * ya + gv[:, D_MODEL:] * yc
    return merged, ya, yc


def _epi_residual_rms(acc, xres, g):
    x = xres + acc
    return x, _rms_fwd_vals(x, g)


def _cross_scores(cq, ck):
    out = []
    for h in range(CROSS_HEADS):
        cs = slice(h * CROSS_HD, (h + 1) * CROSS_HD)
        s = _dot(cq[:, cs], ck[:, cs], "nt") * (CROSS_HD ** -0.5)
        e = jnp.exp(s - jnp.max(s, axis=-1, keepdims=True))
        out.append((cs, e, jnp.sum(e, axis=-1, keepdims=True)))
    return out


def _epi_cross_fwd(acc, ck, cv):
    cq = acc.astype(BF16)
    co = [_dot(e, cv[:, cs], "nn") / den for cs, e, den in _cross_scores(cq, ck)]
    return cq, jnp.concatenate(co, axis=1)


def _epi_cross_bwd(dco, cq, ck, cv):
    dco = dco.astype(BF16)
    dcq, dck, dcv = [], [], []
    for cs, e, den in _cross_scores(cq, ck):
        p = e / den
        dp = _dot(dco[:, cs], cv[:, cs], "nt")
        ds = (p * (dp - jnp.sum(dp * p, axis=-1, keepdims=True)) * (CROSS_HD ** -0.5)).astype(BF16)
        dcq.append(_dot(ds, ck[:, cs], "nn"))
        dck.append(_dot(ds, cq[:, cs], "tn"))
        dcv.append(_dot(p, dco[:, cs], "tn"))
    return jnp.concatenate(dcq, axis=1), jnp.concatenate(dck, axis=1), jnp.concatenate(dcv, axis=1)


def _epi_mlp_up(acc):
    return acc, jnp.square(jnp.maximum(acc, 0.0))


def _epi_final(acc, x2, tgt, g):
    x3 = x2 + acc
    err = _rms_fwd_vals(x3, g) - tgt
    loss = (0.5 / D_MODEL) * jnp.sum(err * err)
    dx3, dg = _rms_bwd_vals(x3, g, err * (1.0 / D_MODEL))
    return dx3, dx3, jnp.full((1, HEAD_DIM), loss, F32), dg


def _epi_mlp_down_bwd(dh, hpre):
    return (dh * 2.0 * jnp.maximum(hpre.astype(F32), 0.0),)


def _epi_rms_bwd(du, x, g, dres):
    dx, dg = _rms_bwd_vals(x, g, du)
    return dres + dx, dg


def _epi_rms_bwd_2(du, x, g, dres):
    dx, dg = _epi_rms_bwd(du, x, g, dres)
    return dx, dx, dg


def _epi_rms_bwd_g(du, x, g):
    return (_rms_bwd_vals(x, g, du)[1],)


def _epi_mix_bwd(dm, ya, yc, gates, bg):
    gv = _sigmoid(gates.astype(F32) + bg)
    ga, gb = gv[:, :D_MODEL], gv[:, D_MODEL:]
    ya, yc = ya.astype(F32), yc.astype(F32)
    dgate = jnp.concatenate([dm * ya * ga * (1.0 - ga), dm * yc * gb * (1.0 - gb)], axis=1)
    return dm * ga, dm * gb, dgate, jnp.sum(dgate, axis=0, keepdims=True)


def _epi_ln_bwd(dc4, c2, lg, lb):
    mu = jnp.mean(c2, axis=-1, keepdims=True)
    xc = c2 - mu
    rstd = lax.rsqrt(jnp.mean(xc * xc, axis=-1, keepdims=True) + EPS)
    xh = xc * rstd
    c3 = xh * lg + lb
    sg = _sigmoid(c3)
    dc3 = dc4 * sg * (1.0 + c3 * (1.0 - sg))
    dxh = dc3 * lg
    dc2 = rstd * (dxh - jnp.mean(dxh, axis=-1, keepdims=True) - xh * jnp.mean(dxh * xh, axis=-1, keepdims=True))
    return (dc2, jnp.sum(dc3 * xh, axis=0, keepdims=True), jnp.sum(dc3, axis=0, keepdims=True),
            jnp.sum(dc2, axis=0, keepdims=True))


def _sds(shape, dtype):
    return jax.ShapeDtypeStruct(shape, dtype)


class _Lazy:
    def __init__(self, fn):
        self.fn = fn

    def __getitem__(self, key):
        return self.fn(key)


def _local_step(x, mem, tgt, sm, plan):
    w = _Lazy(plan.w)
    dw = {}

    def carry(name, n_own, fn, *args, **kw):
        c = plan.comm(name, dw)
        res = fn(*args, comm=c, **kw)
        plan.done(name, res[n_own:])
        return res[:n_own]

    def mm(name, *args, **kw):
        return carry(name, len(args[6]), _mm, name, *args, **kw)

    seq = x.shape[0]
    nr = seq // ROW_BLK
    big = min(1024, seq)
    nb = seq // big
    row = lambda n: ((ROW_BLK, n), lambda i, j, k: (i, 0))
    vec = lambda n: ((1, n), lambda i, j, k: (0, 0))
    full = lambda r, c: ((r, c), lambda i, j, k: (0, 0))
    gates_blk = ((ROW_BLK, 2 * D_MODEL), lambda i, j, k: (i, GATE_COL_BLK))
    tabs = _rope_tables(seq)

    u = carry("rms_mix", 1, _rms_fwd, "rms_mix", x, sm["g_mix"], ROW_BLK)[0]
    whole3 = lambda a: (a.shape, lambda i, j, k: (0, 0, 0))
    z = plan.project_in(u)
    a_gs = carry("qkv_prep", 3, _qkv_prep, z, tabs)
    os_, ls_ = [], []
    for g in range(N_GROUPS):
        name = "attn_fwd_%d" % g
        o_g, l_g = carry(name, 2, _attn_fwd, name, a_gs[g])
        os_.append(o_g)
        ls_.append(l_g)
    attn, lt = _attn_merge(os_, ls_, seq)
    c2, c4 = carry("conv_fwd", 2, _conv_fwd, z, w["taps"], sm["conv_b"], sm["conv_ln_g"], sm["conv_ln_b"])
    merged, ya, yc = mm(
        "mix", attn, w["w_attn_proj"], "nn", (nr, 1, 1), row(GROUP_W), full(GROUP_W, D_MODEL),
        [(_sds((seq, D_MODEL), BF16), *row(D_MODEL))] * 3,
        extras=[(c4, *row(CONV_CH)), (w["w_conv_proj"], *full(CONV_CH, D_MODEL)), (z, *gates_blk), (sm["b_gate"], *vec(2 * D_MODEL))],
        epi=_epi_mix)
    x1, uq = mm("out_proj", merged, w["w_out"], "nn", (nr, 1, 1), row(D_MODEL), full(D_MODEL, D_MODEL),
                 [(_sds((seq, D_MODEL), F32), *row(D_MODEL)), (_sds((seq, D_MODEL), BF16), *row(D_MODEL))],
                 extras=[(x, *row(D_MODEL)), (sm["g_cross"], *vec(D_MODEL))], epi=_epi_residual_rms)

    mn = _rms_fwd("rms_mem", mem, sm["g_mem"], N_MEM)[0]
    ckv = mm("ckv_proj", mn, w["w_ckv"], "nn", (1, N_DEV, 1), full(N_MEM, D_MODEL),
              ((None, D_MODEL, 2 * D_MODEL // N_DEV), lambda i, j, k: (j, 0, 0)),
              [(_sds((N_MEM, 2 * D_MODEL), BF16), (N_MEM, 2 * D_MODEL // N_DEV), lambda i, j, k: (0, j))])[0]
    ck, cv = ckv[:, :D_MODEL], ckv[:, D_MODEL:]
    kv_blk = full(N_MEM, D_MODEL)
    cq, co = mm("cq_proj_cross", uq, w["w_cq"], "nn", (nr, 1, 1), row(D_MODEL), full(D_MODEL, D_MODEL),
                 [(_sds((seq, D_MODEL), BF16), *row(D_MODEL))] * 2,
                 extras=[(ck, *kv_blk), (cv, *kv_blk)], epi=_epi_cross_fwd)
    x2, um = mm("co_proj", co, w["w_co"], "nn", (nr, 1, 1), row(D_MODEL), full(D_MODEL, D_MODEL),
                 [(_sds((seq, D_MODEL), F32), *row(D_MODEL)), (_sds((seq, D_MODEL), BF16), *row(D_MODEL))],
                 extras=[(x1, *row(D_MODEL)), (sm["g_mlp"], *vec(D_MODEL))], epi=_epi_residual_rms)

    ff_blk = D_FF // N_DEV
    row_f32 = (_sds((seq, D_MODEL), F32), *row(D_MODEL))
    row_bf16 = (_sds((seq, D_MODEL), BF16), *row(D_MODEL))
    col_sum = (_sds((1, D_MODEL), F32), *vec(D_MODEL))
    hpre, h = mm("mlp_up", um, w["w_up"], "nn", (nr, 1, 1), row(D_MODEL), whole3(w["w_up"]),
                 [(_sds((seq, D_FF), BF16), *row(D_FF))] * 2, epi=_epi_mlp_up, split=("cols", N_DEV), b_resident=True)
    kt = D_FF // D_MODEL
    dx3, dx3b, loss, dg_final = mm(
        "mlp_down_loss", h, w["w_down"], "nn", (nr, 1, 1), row(D_FF), full(D_FF, D_MODEL),
        [row_f32, row_bf16, (_sds((1, HEAD_DIM), F32), *vec(HEAD_DIM)), col_sum],
        extras=[(x2, *row(D_MODEL)), (tgt, *row(D_MODEL)), (sm["g_final"], *vec(D_MODEL))], epi=_epi_final, acc_outs=(2, 3),
        b_resident=True)

    dhpre = mm("mlp_down_bwd", dx3b, w["w_down"], "nt", (nr, 1, 1), row(D_MODEL), full(D_FF, D_MODEL),
               [(_sds((seq, D_FF), BF16), *row(D_FF))], extras=[(hpre, *row(D_FF))], epi=_epi_mlp_down_bwd,
               split=("cols", kt), b_resident=True)[0]
    big2 = min(2 * big, seq)
    nb2 = seq // big2
    dw["w_down"] = mm("dw_down", h, dx3b, "tn", (kt, 1, nb2), ((big2, D_MODEL), lambda i, j, k: (k, i)),
                      ((big2, D_MODEL), lambda i, j, k: (k, 0)),
                      [(_sds((D_FF, D_MODEL), BF16), (D_MODEL, D_MODEL), lambda i, j, k: (i, 0))])[0]
    dx2, dx2b, dg_mlp = mm("mlp_up_bwd", dhpre, w["w_up"], "nt", (nr, 1, 1), row(D_FF), whole3(w["w_up"]),
                           [row_f32, row_bf16, col_sum],
                           extras=[(x2, *row(D_MODEL)), (sm["g_mlp"], *vec(D_MODEL)), (dx3, *row(D_MODEL))],
                           epi=_epi_rms_bwd_2, acc_outs=(2,), split=("sum", N_DEV), b_resident=True)
    dw["w_up"] = mm("dw_up", um, dhpre, "tn", (1, N_DEV, nb2), ((big2, D_MODEL), lambda i, j, k: (k, 0)),
                    ((big2, ff_blk), lambda i, j, k: (k, j)),
                    [(_sds((N_DEV, D_MODEL, ff_blk), BF16), (None, D_MODEL, ff_blk), lambda i, j, k: (j, 0, 0))])[0]

    acc_kv = (_sds((N_MEM, D_MODEL), F32), *kv_blk)
    dcq, dck, dcv = mm("co_proj_bwd_cross", dx2b, w["w_co"], "nt", (nr, 1, 1), row(D_MODEL), full(D_MODEL, D_MODEL),
                       [row_bf16, acc_kv, acc_kv],
                       extras=[(cq, *row(D_MODEL)), (ck, *kv_blk), (cv, *kv_blk)], epi=_epi_cross_bwd, acc_outs=(1, 2))

    def dw_square(name, act, grad):
        return mm(name, act, grad, "tn", (1, 1, nb2), ((big2, D_MODEL), lambda i, j, k: (k, 0)),
                  ((big2, D_MODEL), lambda i, j, k: (k, 0)), [(_sds((D_MODEL, D_MODEL), BF16), *full(D_MODEL, D_MODEL))])[0]

    dw["w_co"] = dw_square("dw_co", co, dx2b)
    dx1, dx1b, dg_cross = mm("cq_proj_bwd", dcq, w["w_cq"], "nt", (nr, 1, 1), row(D_MODEL), full(D_MODEL, D_MODEL),
                             [row_f32, row_bf16, col_sum],
                             extras=[(x1, *row(D_MODEL)), (sm["g_cross"], *vec(D_MODEL)), (dx2, *row(D_MODEL))],
                             epi=_epi_rms_bwd_2, acc_outs=(2,))
    dw["w_cq"] = dw_square("dw_cq", uq, dcq)
    dckv = jnp.concatenate([dck, dcv], axis=1)
    kv_chunk = 2 * D_MODEL // N_DEV
    dw["w_ckv"] = mm("dw_ckv", mn, dckv, "tn", (1, N_DEV, 1), full(N_MEM, D_MODEL), ((N_MEM, kv_chunk), lambda i, j, k: (0, j)),
                      [(_sds((N_DEV, D_MODEL, kv_chunk), BF16), (None, D_MODEL, kv_chunk), lambda i, j, k: (j, 0, 0))])[0]
    dg_mem = mm("ckv_proj_bwd", dckv, w["w_ckv"], "nt", (1, 1, N_DEV), ((N_MEM, kv_chunk), lambda i, j, k: (0, k)),
                 ((None, D_MODEL, kv_chunk), lambda i, j, k: (k, 0, 0)), [(_sds((1, D_MODEL), F32), *vec(D_MODEL))],
                 extras=[(mem, *full(N_MEM, D_MODEL)), (sm["g_mem"], *vec(D_MODEL))], epi=_epi_rms_bwd_g, acc_outs=(0,))[0]

    dya, dyc, dz, db_gate = mm(
        "out_proj_bwd_mix", dx1b, w["w_out"], "nt", (nr, 1, 1), row(D_MODEL), full(D_MODEL, D_MODEL),
        [(_sds((seq, D_MODEL), BF16), *row(D_MODEL)), (_sds((seq, D_MODEL), BF16), *row(D_MODEL)),
         (_sds((seq, IN_W), BF16), *gates_blk), (_sds((1, 2 * D_MODEL), F32), *vec(2 * D_MODEL))],
        extras=[(ya, *row(D_MODEL)), (yc, *row(D_MODEL)), (z, *gates_blk), (sm["b_gate"], *vec(2 * D_MODEL))],
        epi=_epi_mix_bwd, acc_outs=(3,))
    dw["w_out"] = dw_square("dw_out", merged, dx1b)
    dattn = mm("attn_proj_bwd", dya, w["w_attn_proj"], "nt", (nr, 1, 1), row(D_MODEL), full(GROUP_W, D_MODEL),
                [(_sds((seq, GROUP_W), BF16), *row(GROUP_W))])[0]
    pc = D_MODEL // N_DEV
    dw["w_attn_proj"] = mm("dw_attn_proj", attn, dya, "tn", (1, 1, nb2), ((big2, GROUP_W), lambda i, j, k: (k, 0)),
                           ((big2, D_MODEL), lambda i, j, k: (k, 0)),
                           [(_sds((N_DEV, GROUP_W, pc), BF16), (N_DEV, GROUP_W, pc), lambda i, j, k: (0, 0, 0))],
                           out_chunks=N_DEV)[0]
    cvec = (_sds((1, CONV_CH), F32), *vec(CONV_CH))
    dc2, dg_ln_g, dg_ln_b, dg_conv_b = mm(
        "conv_proj_bwd_ln", dyc, w["w_conv_proj"], "nt", (nr, 1, 1), row(D_MODEL), full(CONV_CH, D_MODEL),
        [(_sds((seq, CONV_CH), F32), *row(CONV_CH)), cvec, cvec, cvec],
        extras=[(c2, *row(CONV_CH)), (sm["conv_ln_g"], *vec(CONV_CH)), (sm["conv_ln_b"], *vec(CONV_CH))],
        epi=_epi_ln_bwd, acc_outs=(1, 2, 3))
    dw["w_conv_proj"] = mm("dw_conv_proj", c4, dyc, "tn", (1, 1, nb2), ((big2, CONV_CH), lambda i, j, k: (k, 0)),
                           ((big2, D_MODEL), lambda i, j, k: (k, 0)),
                           [(_sds((N_DEV, CONV_CH, pc), BF16), (N_DEV, CONV_CH, pc), lambda i, j, k: (0, 0, 0))],
                           out_chunks=N_DEV)[0]
    dz, dg_conv_w = carry("conv_bwd", 2, _conv_bwd, dc2, z, w["taps"], dz)
    preps = _attn_bwd_prep(dattn, attn, lt)
    d_gs = []
    for g in range(N_GROUPS):
        name = "attn_bwd_%d" % g
        d_gs.append(carry(name, 1, _attn_bwd, name, a_gs[g], *preps[g])[0])
    dz = _dqkv_post(d_gs, tabs, dz)
    dw["w_in"] = mm("dw_in", u, dz, "tn", (1, N_DEV, nb2), ((big2, D_MODEL), lambda i, j, k: (k, 0)),
                    ((big2, D_MODEL), lambda i, j, k: (k, j)),
                    [(_sds((N_DEV, D_MODEL, D_MODEL), BF16), (None, D_MODEL, D_MODEL), lambda i, j, k: (j, 0, 0))])[0]
    grad_x, dg_mix = mm("in_proj_bwd", dz, w["w_in"], "nt", (nr, 1, 1), row(IN_W), whole3(w["w_in"]), [row_f32, col_sum],
                        extras=[(x, *row(D_MODEL)), (sm["g_mix"], *vec(D_MODEL)), (dx1, *row(D_MODEL))],
                        epi=_epi_rms_bwd, acc_outs=(1,), split=("sum", N_DEV), b_resident=True)
    small = dict(g_mix=dg_mix, b_gate=db_gate, conv_b=dg_conv_b, conv_ln_g=dg_ln_g, conv_ln_b=dg_ln_b, g_cross=dg_cross,
                 g_mem=dg_mem, g_mlp=dg_mlp, g_final=dg_final, loss=loss, conv_w=dg_conv_w)
    return grad_x, dw, small


SHARD_SHAPE = dict(w_in=(1024, 1024), w_attn_proj=(512, 128), w_conv_proj=(768, 128), w_out=(128, 1024), w_cq=(128, 1024),
                   w_ckv=(1024, 256), w_co=(128, 1024), w_up=(1024, 512), w_down=(512, 1024))
FWD_CARRY = {"in_proj":("w_attn_proj", "w_conv_proj", "w_out", "w_cq", "w_ckv", "w_co", "taps"),
             "qkv_prep": ("w_up",), "conv_fwd": ("w_down",)}
BWD_CARRY = {"dw_up": ("w_down",), "out_proj_bwd_mix": ("w_co", "w_cq"), "conv_bwd": ("w_up", "w_ckv"),
             "attn_bwd_0": ("w_out",), "attn_bwd_1": ("w_attn_proj", "w_conv_proj"), "in_proj_bwd": ("w_in",)}


def _cols_to_2d(a):
    return a.transpose(1, 0, 2).reshape(a.shape[1], -1)


def _in_proj_gather(u, w_shard, comm):
    seq = u.shape[0]
    tm = min(1024, seq)
    x, y, c = lax.axis_index("x"), lax.axis_index("y"), lax.axis_index("c")
    ident = lambda px, py, pc: 4 * px + 2 * py + pc
    far = [(1 - x, y), (x, 1 - y), (1 - x, 1 - y)]
    order = jnp.stack([ident(x, y, c), ident(x, y, 1 - c)] + [ident(px, py, c) for px, py in far]
                      + [ident(px, py, 1 - c) for px, py in far]).astype(jnp.int32)
    n_far = len(far)

    def body(order_ref, u_ref, wsh_ref, *rest):
        c_in, z_ref, wg_ref = rest[:comm.n], rest[comm.n], rest[comm.n + 1]
        c_out = rest[comm.n + 2:2 * comm.n + 2]
        wbuf, load_sem, local_sem, recv_sems, ici_send, d2d_send = rest[2 * comm.n + 2:2 * comm.n + 8]
        sems = rest[2 * comm.n + 8:]
        jj, i = pl.program_id(0), pl.program_id(1)
        (kx, ky, kc), me, chips = _Comm._where()
        sibling = (kx, ky, 1 - kc)
        n = order_ref[jj]

        def push(src, blk, send, to):
            return pltpu.make_async_remote_copy(src_ref=src, dst_ref=wg_ref.at[blk], send_sem=send,
                                                recv_sem=recv_sems.at[blk], device_id=to, device_id_type=MESH)

        def load(src):
            cp = pltpu.make_async_copy(src, wbuf, load_sem)
            cp.start()
            cp.wait()

        @pl.when(jnp.logical_and(jj == 0, i == 0))
        def _():
            if comm.n:
                comm.start(c_in, c_out, sems)
            pltpu.make_async_copy(wsh_ref, wg_ref.at[me], local_sem).start()
            push(wsh_ref, me, d2d_send, sibling).start()
            for (px, py) in chips:
                push(wsh_ref, me, ici_send, (px, py, kc)).start()
            load(wsh_ref)

        @pl.when(jnp.logical_and(jj > 0, i == 0))
        def _():
            push(wg_ref.at[n], n, d2d_send, sibling).wait_recv()
            for idx, (px, py) in enumerate(chips):
                @pl.when(jj == 2 + idx)
                def _():
                    blk = 4 * px + 2 * py + kc
                    push(wg_ref.at[blk], blk, d2d_send, sibling).start()

            load(wg_ref.at[n])

        z_ref[...] = _dot(u_ref[...], wbuf[...], "nn").astype(BF16)

        @pl.when(jnp.logical_and(jj == N_DEV - 1, i == pl.num_programs(1) - 1))
        def _():
            def drain_sends(send, count):
                blocks = wg_ref.at[pl.ds(0, count)]
                pltpu.make_async_remote_copy(src_ref=blocks, dst_ref=blocks, send_sem=send, recv_sem=recv_sems.at[0],
                                             device_id=sibling, device_id_type=MESH).wait_send()

            drain_sends(ici_send, n_far)
            drain_sends(d2d_send, n_far + 1)
            pltpu.make_async_copy(wsh_ref, wg_ref.at[me], local_sem).wait()
            if comm.n:
                comm.wait(c_in, c_out, sems)

    any_spec = pl.BlockSpec(memory_space=pl.ANY)
    grid_spec = pltpu.PrefetchScalarGridSpec(
        num_scalar_prefetch=1, grid=(N_DEV, seq // tm),
        in_specs=[pl.BlockSpec((tm, D_MODEL), lambda jj, i, order_ref: (i, 0)), any_spec] + comm.in_specs,
        out_specs=[pl.BlockSpec((tm, D_MODEL), lambda jj, i, order_ref: (i, order_ref[jj])), any_spec] + comm.out_specs,
        scratch_shapes=[pltpu.VMEM((D_MODEL, D_MODEL), BF16), pltpu.SemaphoreType.DMA, pltpu.SemaphoreType.DMA,
                        pltpu.SemaphoreType.DMA((N_DEV,)), pltpu.SemaphoreType.DMA, pltpu.SemaphoreType.DMA] + comm.scratch)
    return pl.pallas_call(
        body, name="in_proj_gather", grid_spec=grid_spec,
        out_shape=[jax.ShapeDtypeStruct((seq, IN_W), BF16), jax.ShapeDtypeStruct((N_DEV, D_MODEL, D_MODEL), BF16)] + comm.out_shape,
        compiler_params=_params(dimension_semantics=("arbitrary", "arbitrary")),
    )(order, u, w_shard, *comm.arrays)


class _Plan:
    def __init__(self, shards, n_tap_cols):
        self.shards = shards
        self.gathered = {}
        self.parts = {}
        self.n_tap_cols = n_tap_cols

    def project_in(self, u):
        comm = self.comm("in_proj", None)
        res = _in_proj_gather(u, self.shards["w_in"], comm)
        self.gathered["w_in"] = res[1]
        self.done("in_proj", res[2:])
        return res[0]

    def comm(self, name, dw):
        if name in FWD_CARRY:
            return _Comm(replicated=[self.shards[k] for k in FWD_CARRY[name]])
        if name in BWD_CARRY:
            return _Comm(chunked=[dw[k].reshape((N_DEV,) + SHARD_SHAPE[k]) for k in BWD_CARRY[name]])
        return _NO_COMM

    def done(self, name, got):
        if name in FWD_CARRY:
            self.gathered.update(zip(FWD_CARRY[name], got))
        elif name in BWD_CARRY:
            self.parts.update(zip(BWD_CARRY[name], got))

    def w(self, key):
        g = self.gathered[key]
        if key in ("w_in", "w_up", "w_ckv"):
            return g
        if key in ("w_attn_proj", "w_conv_proj"):
            return _cols_to_2d(g)
        if key == "taps":
            return jnp.pad(_cols_to_2d(g[:, :CONV_K, :self.n_tap_cols]), ((0, 1), (0, 0)))
        return g.reshape(-1, g.shape[-1])


def _adamw(name, w, m, v, parts, comm=_NO_COMM):
    rows, cols = w.shape
    n_parts = parts.shape[0]
    rb = rows if rows <= 256 or rows % 256 else 256

    def body(w_ref, m_ref, v_ref, p_ref, g_ref, d_ref, nm_ref, nv_ref):
        g = p_ref[0].astype(F32)
        for q in range(1, n_parts):
            g = g + p_ref[q].astype(F32)
        wv = w_ref[...]
        nm = ADAM_B1 * m_ref[...] + (1.0 - ADAM_B1) * g
        nv = ADAM_B2 * v_ref[...] + (1.0 - ADAM_B2) * jnp.square(g)
        m_hat = nm / (1.0 - ADAM_B1 ** ADAM_STEP)
        v_hat = nv / (1.0 - ADAM_B2 ** ADAM_STEP)
        g_ref[...] = g
        d_ref[...] = -ADAM_LR * (m_hat / (jnp.sqrt(v_hat) + ADAM_EPS) + ADAM_WD * wv)
        nm_ref[...] = nm
        nv_ref[...] = nv

    blk = pl.BlockSpec((rb, cols), lambda i: (i, 0))
    return _pcall(body, name, (rows // rb,), [w, m, v, parts],
                  [blk, blk, blk, pl.BlockSpec((n_parts, rb, cols), lambda i: (0, i, 0))],
                  [jax.ShapeDtypeStruct((rows, cols), F32)] * 4, [blk] * 4, comm=comm)


def _sum_parts(name, parts):
    def body(p_ref, o_ref):
        acc = p_ref[0]
        for q in range(1, parts.shape[0]):
            acc = acc + p_ref[q]
        o_ref[...] = acc

    return _pcall(body, name, (1,), [parts], [pl.BlockSpec(parts.shape, lambda i: (0, 0, 0))],
                  [jax.ShapeDtypeStruct(parts.shape[1:], F32)], [pl.BlockSpec(parts.shape[1:], lambda i: (0, 0))])[0]


BIG = ("w_in", "w_attn_proj", "w_conv_proj", "w_out", "w_cq", "w_ckv", "w_co", "w_up", "w_down")
SMALL = ("g_mix", "b_gate", "conv_b", "conv_ln_g", "conv_ln_b", "g_cross", "g_mem", "g_mlp", "g_final")
SMALL_ORDER = SMALL + ("loss", "conv_w")
WEIGHTS = ("g_mix", "w_in", "b_gate", "conv_w", "conv_b", "conv_ln_g", "conv_ln_b", "w_attn_proj", "w_conv_proj", "w_out",
           "g_cross", "g_mem", "w_cq", "w_ckv", "w_co", "g_mlp", "w_up", "w_down", "g_final")


def kernel(x, mem, g_mix, w_in, b_gate, conv_w, conv_b, conv_ln_g, conv_ln_b, w_attn_proj, w_conv_proj, w_out, g_cross, g_mem, w_cq, w_ckv, w_co, g_mlp, w_up, w_down, g_final, loss_target, m_g_mix, m_w_in, m_b_gate, m_conv_w, m_conv_b, m_conv_ln_g, m_conv_ln_b, m_w_attn_proj, m_w_conv_proj, m_w_out, m_g_cross, m_g_mem, m_w_cq, m_w_ckv, m_w_co, m_g_mlp, m_w_up, m_w_down, m_g_final, v_g_mix, v_w_in, v_b_gate, v_conv_w, v_conv_b, v_conv_ln_g, v_conv_ln_b, v_w_attn_proj, v_w_conv_proj, v_w_out, v_g_cross, v_g_mem, v_w_cq, v_w_ckv, v_w_co, v_g_mlp, v_w_up, v_w_down, v_g_final):
    args = dict(locals())
    wts = {k: args[k] for k in WEIGHTS}
    mom = {k: args["m_" + k] for k in WEIGHTS}
    var = {k: args["v_" + k] for k in WEIGHTS}
    two_d = lambda a: a.reshape(a.shape[-2:]) if a.ndim == 3 else a.reshape(1, -1)

    shards = {k: two_d(wts[k]).astype(BF16) for k in BIG}
    shards["taps"] = jnp.pad(two_d(conv_w), ((0, 1), (0, HEAD_DIM - conv_w.shape[-1])))
    plan = _Plan(shards, conv_w.shape[-1])
    sm = {k: two_d(wts[k]) for k in SMALL}

    grad_x, _, small = _local_step(x[0], mem[0], loss_target[0], sm, plan)
    parts = plan.parts

    out = {}
    small_comm = _Comm(replicated=[small[k] for k in SMALL_ORDER])
    for k in BIG:
        res = _adamw("adamw_" + k, two_d(wts[k]), two_d(mom[k]), two_d(var[k]), parts[k],
                     comm=small_comm if k == BIG[0] else _NO_COMM)
        out[k] = [r.reshape(wts[k].shape) for r in res[:4]]
        if k == BIG[0]:
            small_parts = dict(zip(SMALL_ORDER, res[4:]))
    for k in SMALL:
        res = _adamw("adamw_" + k, two_d(wts[k]), two_d(mom[k]), two_d(var[k]), small_parts[k])
        out[k] = [r.reshape(wts[k].shape) for r in res]
    loss = _sum_parts("loss_sum", small_parts["loss"])[0, 0]
    me = 4 * lax.axis_index("x") + 2 * lax.axis_index("y") + lax.axis_index("c")
    n_tap_cols = conv_w.shape[-1]
    tap_parts = lax.dynamic_slice(small_parts["conv_w"], (0, 0, me * n_tap_cols), (N_DEV, CONV_K, n_tap_cols))
    res = _adamw("adamw_conv_w", two_d(conv_w), two_d(m_conv_w), two_d(v_conv_w), tap_parts)
    out["conv_w"] = [r.reshape(conv_w.shape) for r in res]

    return (loss, grad_x[None], *[out[k][0] for k in WEIGHTS], *[out[k][1] for k in WEIGHTS],
            *[out[k][2] for k in WEIGHTS], *[out[k][3] for k in WEIGHTS])
```

```python
import functools

import jax
import jax.numpy as jnp
import numpy as np
from jax import lax
from jax.experimental import pallas as pl
from jax.experimental.pallas import tpu as pltpu

F32 = jnp.float32
BF16 = jnp.bfloat16

N_DEV = 8
D_MODEL = 1024
N_MEM = 256
HEAD_DIM = 128
HEADS_PER_GROUP = 4
GROUP_W = HEADS_PER_GROUP * HEAD_DIM
DILATIONS = (1, 4, 16)
BAND = 128
N_GROUPS = 3
ATTN_W = N_GROUPS * GROUP_W
QKV_W = 3 * ATTN_W
ROT_DIM = HEAD_DIM // 4
ROPE_THETA = 500000.0
CONV_CH = 768
CONV_K = 31
CONV_HALO = 32
SUBLANES = 8
CONV_ROWS = 64
IN_W = 8192
GLU_COL_BLK = QKV_W // (2 * CONV_CH)
GATE_COL_BLK = (QKV_W + 2 * CONV_CH) // (2 * D_MODEL)
CROSS_HEADS = 4
CROSS_HD = D_MODEL // CROSS_HEADS
D_FF = 4096
EPS = 1e-6
NEG = -1e30
QB = 4
ROW_BLK = QB * BAND

ADAM_LR = 0.001
ADAM_B1 = 0.9
ADAM_B2 = 0.999
ADAM_EPS = 1e-08
ADAM_WD = 0.01
ADAM_STEP = 10

VMEM_LIMIT = 56 * 1024 * 1024
MESH = pl.DeviceIdType.MESH


def _params(**kw):
    return pltpu.CompilerParams(vmem_limit_bytes=VMEM_LIMIT, **kw)


def _sigmoid(x):
    return 1.0 / (1.0 + jnp.exp(-x))


def _dot(a, b, kind):
    dims = {"nn": (((1,), (0,)), ((), ())), "nt": (((1,), (1,)), ((), ())), "tn": (((0,), (0,)), ((), ()))}[kind]
    if a.dtype != BF16:
        a = a.astype(BF16)
    if b.dtype != BF16:
        b = b.astype(BF16)
    return lax.dot_general(a, b, dims, preferred_element_type=F32)


def _peers():
    x, y, c = lax.axis_index("x"), lax.axis_index("y"), lax.axis_index("c")
    me = 4 * x + 2 * y + c
    peers = [(x, y, 1 - c), (1 - x, y, c), (x, 1 - y, c), (1 - x, 1 - y, c),
             (1 - x, y, 1 - c), (x, 1 - y, 1 - c), (1 - x, 1 - y, 1 - c)]
    return me, peers


class _Comm:
    def __init__(self, chunked=(), replicated=()):
        self.arrays = list(chunked) + list(replicated)
        self.n_c = len(chunked)
        self.n = len(self.arrays)
        self.out_shape = [jax.ShapeDtypeStruct(a.shape, a.dtype) for a in chunked]
        self.out_shape += [jax.ShapeDtypeStruct((N_DEV,) + a.shape, a.dtype) for a in replicated]
        self.in_specs = [pl.BlockSpec(memory_space=pl.ANY)] * self.n
        self.out_specs = [pl.BlockSpec(memory_space=pl.ANY)] * self.n
        self.scratch = [pltpu.SemaphoreType.DMA((self.n,))] * 5 if self.n else []

    @staticmethod
    def _where():
        x, y, c = lax.axis_index("x"), lax.axis_index("y"), lax.axis_index("c")
        chips = [(1 - x, y), (x, 1 - y), (1 - x, 1 - y)]
        return (x, y, c), 4 * x + 2 * y + c, chips

    def _local(self, ins, outs, sems, a, me):
        src = ins[a].at[me] if a < self.n_c else ins[a]
        return pltpu.make_async_copy(src, outs[a].at[me], sems[2].at[a])

    @staticmethod
    def _remote(src, dst, send, recv, to):
        return pltpu.make_async_remote_copy(src_ref=src, dst_ref=dst, send_sem=send, recv_sem=recv, device_id=to,
                                            device_id_type=MESH)

    def start(self, ins, outs, sems):
        (x, y, c), me, chips = self._where()
        for a in range(self.n):
            self._local(ins, outs, sems, a, me).start()
            if a < self.n_c:
                for (px, py, pc) in _peers()[1]:
                    self._remote(ins[a].at[4 * px + 2 * py + pc], outs[a].at[me], sems[0].at[a], sems[1].at[a], (px, py, pc)).start()
            else:
                self._remote(ins[a], outs[a].at[me], sems[3].at[a], sems[4].at[a], (x, y, 1 - c)).start()
                for (px, py) in chips:
                    self._remote(ins[a], outs[a].at[me], sems[0].at[a], sems[1].at[a], (px, py, c)).start()

    def wait(self, ins, outs, sems):
        (x, y, c), me, chips = self._where()
        sibling = (x, y, 1 - c)

        def drain(a, pair, count):
            blocks = outs[a].at[pl.ds(0, count)]
            cp = self._remote(blocks, blocks, sems[pair].at[a], sems[pair + 1].at[a], sibling)
            cp.wait_send()
            cp.wait_recv()

        for a in range(self.n):
            if a < self.n_c:
                drain(a, 0, N_DEV - 1)
            else:
                drain(a, 0, len(chips))
                for (px, py) in chips:
                    blk = outs[a].at[4 * px + 2 * py + c]
                    self._remote(blk, blk, sems[3].at[a], sems[4].at[a], sibling).start()
        for a in range(self.n):
            if a >= self.n_c:
                drain(a, 3, len(chips) + 1)
            self._local(ins, outs, sems, a, me).wait()


_NO_COMM = _Comm()


def _pcall(body, name, grid, operands, in_specs, out_shape, out_specs, scratch=(), aliases=None, comm=_NO_COMM):
    n_in, n_out, n_scr = len(operands), len(out_shape), len(scratch)
    grid = tuple(grid)

    def carried(*refs):
        ins, c_in = refs[:n_in], refs[n_in:n_in + comm.n]
        o0 = n_in + comm.n
        outs, c_out = refs[o0:o0 + n_out], refs[o0 + n_out:o0 + n_out + comm.n]
        s0 = o0 + n_out + comm.n
        scr, sems = refs[s0:s0 + n_scr], refs[s0 + n_scr:]
        ids = [pl.program_id(ax) for ax in range(len(grid))]

        @pl.when(functools.reduce(jnp.logical_and, [p == 0 for p in ids]))
        def _():
            comm.start(c_in, c_out, sems)

        body(*ins, *outs, *scr)

        @pl.when(functools.reduce(jnp.logical_and, [p == g - 1 for p, g in zip(ids, grid)]))
        def _():
            comm.wait(c_in, c_out, sems)

    return pl.pallas_call(
        carried if comm.n else body, name=name, grid=grid, in_specs=list(in_specs) + comm.in_specs,
        out_shape=list(out_shape) + comm.out_shape, out_specs=list(out_specs) + comm.out_specs,
        scratch_shapes=list(scratch) + comm.scratch, input_output_aliases=aliases or {},
        compiler_params=_params(dimension_semantics=("arbitrary",) * len(grid)),
    )(*operands, *comm.arrays)


def _mm(name, a, b, kind, grid, a_blk, b_blk, outs, extras=(), epi=None, acc_outs=(), j_outer=False, comm=_NO_COMM,
        split=None, b_resident=False, out_chunks=0):
    gi, gj, gk = grid
    n_ex = len(extras)
    n_out = len(outs)
    mode, n_chunks = split if split is not None else (None, 1)

    def spec(blk, fn, **kw):
        return pl.BlockSpec(blk, (lambda j, i, k: fn(i, j, k)) if j_outer else fn, **kw)

    def b_chunk(b_ref, c):
        if len(b_ref.shape) == 3:
            return b_ref[c]
        rows, cols = b_ref.shape
        if (kind == "nn") == (mode == "cols"):
            return b_ref[:, c * (cols // n_chunks):(c + 1) * (cols // n_chunks)]
        return b_ref[c * (rows // n_chunks):(c + 1) * (rows // n_chunks), :]

    def col_chunk(ref, c):
        width = ref.shape[-1] // n_chunks
        return slice(c * width, (c + 1) * width)

    def body(*refs):
        a_ref, b_ref = refs[0], refs[1]
        ex = refs[2:2 + n_ex]
        out_refs = refs[2 + n_ex:2 + n_ex + n_out]
        acc_ref = refs[2 + n_ex + n_out] if gk > 1 else None
        i = pl.program_id(1 if j_outer else 0)
        k = pl.program_id(2)
        if mode == "cols":
            a_val = a_ref[...]
            for c in range(n_chunks):
                acc = _dot(a_val, b_chunk(b_ref, c), kind)
                vals = epi(acc, *[e[:, col_chunk(e, c)] for e in ex]) if epi is not None else (acc,)
                for o, v in zip(out_refs, vals):
                    o[:, col_chunk(o, c)] = v.astype(o.dtype)
            return
        if mode == "sum":
            part = _dot(a_ref[:, col_chunk(a_ref, 0)], b_chunk(b_ref, 0), kind)
            for c in range(1, n_chunks):
                part = part + _dot(a_ref[:, col_chunk(a_ref, c)], b_chunk(b_ref, c), kind)
        else:
            part = _dot(a_ref[...], b_ref[...], kind)

        def finish(acc):
            if out_chunks:
                width = acc.shape[-1] // out_chunks
                for c in range(out_chunks):
                    out_refs[0][c] = acc[:, c * width:(c + 1) * width].astype(out_refs[0].dtype)
                return
            vals = epi(acc, *[e[...] for e in ex]) if epi is not None else (acc,)
            for idx, (o, v) in enumerate(zip(out_refs, vals)):
                if idx in acc_outs:
                    @pl.when(i == 0)
                    def _():
                        o[...] = v.astype(o.dtype)

                    @pl.when(i != 0)
                    def _():
                        o[...] += v.astype(o.dtype)
                else:
                    o[...] = v.astype(o.dtype)

        if gk == 1:
            finish(part)
        else:
            @pl.when(k == 0)
            def _():
                acc_ref[...] = part

            @pl.when(k != 0)
            def _():
                acc_ref[...] += part

            @pl.when(k == gk - 1)
            def _():
                finish(acc_ref[...])

    scratch = []
    if gk > 1:
        tm = a_blk[0][-1] if kind == "tn" else a_blk[0][-2]
        tn = b_blk[0][-2] if kind == "nt" else b_blk[0][-1]
        scratch = [pltpu.VMEM((tm, tn), F32)]
    b_kw = dict(pipeline_mode=pl.Buffered(1)) if b_resident else {}
    return _pcall(body, name, (gj, gi, gk) if j_outer else (gi, gj, gk), [a, b] + [e for e, _, _ in extras],
                  [spec(*a_blk), spec(*b_blk, **b_kw)] + [spec(blk, fn) for _, blk, fn in extras],
                  [s for s, _, _ in outs], [spec(blk, fn) for _, blk, fn in outs], scratch, comm=comm)


def _rms_fwd_vals(x, g):
    r = lax.rsqrt(jnp.mean(x * x, axis=-1, keepdims=True) + EPS)
    return x * r * g


def _rms_bwd_vals(x, g, du):
    r = lax.rsqrt(jnp.mean(x * x, axis=-1, keepdims=True) + EPS)
    xh = x * r
    dxh = du * g
    dx = r * (dxh - xh * jnp.mean(dxh * xh, axis=-1, keepdims=True))
    return dx, jnp.sum(du * xh, axis=0, keepdims=True)


def _rms_fwd(name, x, g, rows, comm=_NO_COMM):
    n = x.shape[0]

    def body(x_ref, g_ref, o_ref):
        o_ref[...] = _rms_fwd_vals(x_ref[...], g_ref[...]).astype(BF16)

    return _pcall(body, name, (n // rows,), [x, g],
                  [pl.BlockSpec((rows, D_MODEL), lambda i: (i, 0)), pl.BlockSpec((1, D_MODEL), lambda i: (0, 0))],
                  [jax.ShapeDtypeStruct(x.shape, BF16)], [pl.BlockSpec((rows, D_MODEL), lambda i: (i, 0))], comm=comm)


def _rope_tables(seq):
    half = ROT_DIM // 2
    pos = np.arange(seq, dtype=np.float32)
    inv_freq = np.float32(ROPE_THETA) ** (-np.arange(0, ROT_DIM, 2, dtype=np.float32) / np.float32(ROT_DIM))
    ang = (pos[:, None] * inv_freq[None, :]).astype(np.float32)
    cos, sin = np.cos(ang), np.sin(ang)
    rest = HEAD_DIM - ROT_DIM
    c = np.concatenate([cos, cos, np.ones((seq, rest), np.float32)], axis=1)
    s1 = np.concatenate([np.zeros((seq, half), np.float32), sin, np.zeros((seq, rest), np.float32)], axis=1)
    s2 = np.concatenate([-sin, np.zeros((seq, half + rest), np.float32)], axis=1)
    return jnp.asarray(c), jnp.asarray(s1), jnp.asarray(s2)


def _group_shapes(seq, width, dtype):
    return [jax.ShapeDtypeStruct((d, seq // d, width), dtype) for d in DILATIONS]


def _group_specs(width):
    return [pl.BlockSpec((d, ROW_BLK // d, width), lambda i: (0, i, 0)) for d in DILATIONS]


def _qkv_prep(z, tabs, comm=_NO_COMM):
    seq = z.shape[0]

    def body(z_ref, c_ref, s1_ref, s2_ref, a0, a1, a2, sc):
        outs = (a0, a1, a2)
        c, s1, s2 = c_ref[...], s1_ref[...], s2_ref[...]
        for part in range(3):
            for hh in range(N_GROUPS * HEADS_PER_GROUP):
                g, hl = divmod(hh, HEADS_PER_GROUP)
                col = part * ATTN_W + hh * HEAD_DIM
                ocol = part * GROUP_W + hl * HEAD_DIM
                x = z_ref[:, col:col + HEAD_DIM].astype(F32)
                if part < 2:
                    x = x * c + pltpu.roll(x, ROT_DIM // 2, 1) * s1 + pltpu.roll(x, HEAD_DIM - ROT_DIM // 2, 1) * s2
                d = DILATIONS[g]
                if d == 1:
                    outs[g][0, :, ocol:ocol + HEAD_DIM] = x.astype(BF16)
                else:
                    sc[...] = x
                    for r in range(d):
                        outs[g][r, :, ocol:ocol + HEAD_DIM] = sc[pl.ds(r, ROW_BLK // d, stride=d), :].astype(BF16)

    tab_spec = pl.BlockSpec((ROW_BLK, HEAD_DIM), lambda i: (i, 0))
    return _pcall(body, "qkv_prep", (seq // ROW_BLK,), [z, *tabs],
                  [pl.BlockSpec((ROW_BLK, QKV_W), lambda i: (i, 0)), tab_spec, tab_spec, tab_spec],
                  _group_shapes(seq, ATTN_W, BF16), _group_specs(ATTN_W), [pltpu.VMEM((ROW_BLK, HEAD_DIM), F32)], comm=comm)


def _band_masks_2(t):
    qi = lax.broadcasted_iota(jnp.int32, (BAND, 2 * BAND), 0)
    kj = lax.broadcasted_iota(jnp.int32, (BAND, 2 * BAND), 1)
    band = jnp.logical_and(kj >= qi, kj <= qi + BAND)
    return band, jnp.logical_and(band, jnp.logical_or(kj >= BAND, t > 0))


def _attn_fwd(name, a_g, comm=_NO_COMM):
    dil, m_len, _ = a_g.shape
    qb = min(QB, m_len // BAND)
    rows = qb * BAND
    steps = m_len // rows
    scale = HEAD_DIM ** -0.5

    tiles = [(sb, h) for sb in range(qb) for h in range(HEADS_PER_GROUP)]

    def body(q_ref, kc_ref, vc_ref, kp_ref, vp_ref, o_ref, l_ref, k_all, v_all, s_scr, p_scr, r_scr):
        t = pl.program_id(1)
        k_all[0:BAND, :] = kp_ref[...]
        k_all[BAND:, :] = kc_ref[...]
        v_all[0:BAND, :] = vp_ref[...]
        v_all[BAND:, :] = vc_ref[...]
        band, band_first = _band_masks_2(t)
        for idx, (sb, h) in enumerate(tiles):
            cs = slice(h * HEAD_DIM, (h + 1) * HEAD_DIM)
            s = _dot(q_ref[sb * BAND:(sb + 1) * BAND, cs], k_all[sb * BAND:(sb + 2) * BAND, cs], "nt") * scale
            s_scr[idx] = jnp.where(band_first if sb == 0 else band, s, NEG)
        for idx, (sb, h) in enumerate(tiles):
            cs = slice(h * HEAD_DIM, (h + 1) * HEAD_DIM)
            s = s_scr[idx]
            mx = jnp.max(s, axis=-1, keepdims=True)
            p = jnp.exp(s - mx)
            den = jnp.sum(p, axis=-1, keepdims=True)
            p_scr[idx] = p.astype(BF16)
            r_scr[idx] = jnp.broadcast_to(1.0 / den, (BAND, HEAD_DIM))
            l_ref[sb * BAND:(sb + 1) * BAND, cs] = jnp.broadcast_to(mx + jnp.log(den), (BAND, HEAD_DIM))
        for idx, (sb, h) in enumerate(tiles):
            cs = slice(h * HEAD_DIM, (h + 1) * HEAD_DIM)
            o_ref[sb * BAND:(sb + 1) * BAND, cs] = _dot(p_scr[idx], v_all[sb * BAND:(sb + 2) * BAND, cs], "nn") * r_scr[idx]

    def prev(r, t):
        return jnp.maximum(qb * t - 1, 0)

    cur = lambda c: pl.BlockSpec((None, rows, GROUP_W), lambda r, t, c=c: (r, t, c))
    prv = lambda c: pl.BlockSpec((None, BAND, GROUP_W), lambda r, t, c=c: (r, prev(r, t), c))
    out_spec = pl.BlockSpec((None, rows, GROUP_W), lambda r, t: (r, t, 0))
    shp = jax.ShapeDtypeStruct((dil, m_len, GROUP_W), F32)
    n_t = len(tiles)
    return _pcall(body, name, (dil, steps), [a_g] * 5, [cur(0), cur(1), cur(2), prv(1), prv(2)], [shp, shp],
                  [out_spec, out_spec],
                  [pltpu.VMEM((rows + BAND, GROUP_W), BF16), pltpu.VMEM((rows + BAND, GROUP_W), BF16),
                   pltpu.VMEM((n_t, BAND, 2 * BAND), F32), pltpu.VMEM((n_t, BAND, 2 * BAND), BF16),
                   pltpu.VMEM((n_t, BAND, HEAD_DIM), F32)], comm=comm)


def _attn_merge(os_, ls_, seq):
    def body(o0, l0, o1, l1, o2, l2, at_ref, lt_ref, sc):
        for h in range(HEADS_PER_GROUP):
            cs = slice(h * HEAD_DIM, (h + 1) * HEAD_DIM)
            for gi, (o_r, l_r) in enumerate(((o1, l1), (o2, l2))):
                d = DILATIONS[gi + 1]
                for r in range(d):
                    sc.at[2 * gi][pl.ds(r, ROW_BLK // d, stride=d), :] = o_r[r, :, cs]
                    sc.at[2 * gi + 1][pl.ds(r, ROW_BLK // d, stride=d), :] = l_r[r, :, cs]
            o0v, l0v = o0[0, :, cs], l0[0, :, cs]
            o1v, l1v, o2v, l2v = sc[0], sc[1], sc[2], sc[3]
            mx = jnp.maximum(jnp.maximum(l0v, l1v), l2v)
            e0, e1, e2 = jnp.exp(l0v - mx), jnp.exp(l1v - mx), jnp.exp(l2v - mx)
            tot = e0 + e1 + e2
            at_ref[:, cs] = ((e0 * o0v + e1 * o1v + e2 * o2v) / tot).astype(BF16)
            lt_ref[:, cs] = mx + jnp.log(tot)

    gs = _group_specs(GROUP_W)
    row = pl.BlockSpec((ROW_BLK, GROUP_W), lambda i: (i, 0))
    return pl.pallas_call(
        body, name="attn_merge",
        out_shape=[jax.ShapeDtypeStruct((seq, GROUP_W), BF16), jax.ShapeDtypeStruct((seq, GROUP_W), F32)],
        grid=(seq // ROW_BLK,), in_specs=[gs[0], gs[0], gs[1], gs[1], gs[2], gs[2]], out_specs=[row, row],
        scratch_shapes=[pltpu.VMEM((4, ROW_BLK, HEAD_DIM), F32)],
        compiler_params=_params(dimension_semantics=("arbitrary",)),
    )(os_[0], ls_[0], os_[1], ls_[1], os_[2], ls_[2])


def _attn_bwd_prep(dattn, attn, lt):
    seq = dattn.shape[0]

    def body(da_ref, at_ref, lt_ref, d0, c0, t0, d1, c1, t1, d2, c2, t2, sc):
        outs = ((d0, c0, t0), (d1, c1, t1), (d2, c2, t2))
        for h in range(HEADS_PER_GROUP):
            cs = slice(h * HEAD_DIM, (h + 1) * HEAD_DIM)
            da = da_ref[:, cs].astype(F32)
            cc = jnp.broadcast_to(jnp.sum(da * at_ref[:, cs].astype(F32), axis=-1, keepdims=True), (ROW_BLK, HEAD_DIM))
            ltv = lt_ref[:, cs]
            d0[0, :, cs] = da_ref[:, cs]
            c0[0, :, cs] = cc
            t0[0, :, cs] = ltv
            sc[0], sc[1], sc[2] = da, cc, ltv
            for g in (1, 2):
                d = DILATIONS[g]
                for r in range(d):
                    rows = pl.ds(r, ROW_BLK // d, stride=d)
                    outs[g][0][r, :, cs] = sc.at[0][rows, :].astype(BF16)
                    outs[g][1][r, :, cs] = sc.at[1][rows, :]
                    outs[g][2][r, :, cs] = sc.at[2][rows, :]

    gs = _group_specs(GROUP_W)
    row = pl.BlockSpec((ROW_BLK, GROUP_W), lambda i: (i, 0))
    shapes, specs = [], []
    for g, d in enumerate(DILATIONS):
        for dt in (BF16, F32, F32):
            shapes.append(jax.ShapeDtypeStruct((d, seq // d, GROUP_W), dt))
            specs.append(gs[g])
    res = pl.pallas_call(
        body, name="attn_bwd_prep", out_shape=shapes, grid=(seq // ROW_BLK,), in_specs=[row, row, row],
        out_specs=specs, scratch_shapes=[pltpu.VMEM((3, ROW_BLK, HEAD_DIM), F32)],
        compiler_params=_params(dimension_semantics=("arbitrary",)),
    )(dattn, attn, lt)
    return [res[3 * g:3 * g + 3] for g in range(N_GROUPS)]


def _attn_bwd(name, a_g, da_g, c_g, lt_g, comm=_NO_COMM):
    dil, m_len, _ = a_g.shape
    qb = min(QB, m_len // BAND)
    rows = qb * BAND
    steps = m_len // rows
    scale = HEAD_DIM ** -0.5

    tiles = [(sb, h) for sb in range(qb) for h in range(HEADS_PER_GROUP)]

    def body(q_ref, kc_ref, vc_ref, kp_ref, vp_ref, da_ref, c_ref, lt_ref, d_ref, dk_acc, dv_acc, car_k, car_v,
             k_all, v_all, s_scr, dp_scr, p_scr, ds_scr):
        tg = pl.program_id(1)
        t = steps - 1 - tg

        @pl.when(tg == 0)
        def _():
            car_k[...] = jnp.zeros_like(car_k)
            car_v[...] = jnp.zeros_like(car_v)

        k_all[0:BAND, :] = kp_ref[...]
        k_all[BAND:, :] = kc_ref[...]
        v_all[0:BAND, :] = vp_ref[...]
        v_all[BAND:, :] = vc_ref[...]
        zero = jnp.zeros((rows, GROUP_W), F32)
        dk_acc[0:rows, :] = zero
        dv_acc[0:rows, :] = zero
        dk_acc[rows:rows + BAND, :] = car_k[...]
        dv_acc[rows:rows + BAND, :] = car_v[...]
        band, band_first = _band_masks_2(t)
        for idx, (sb, h) in enumerate(tiles):
            cs = slice(h * HEAD_DIM, (h + 1) * HEAD_DIM)
            rs, ks = slice(sb * BAND, (sb + 1) * BAND), slice(sb * BAND, (sb + 2) * BAND)
            s_scr[idx] = _dot(q_ref[rs, cs], k_all[ks, cs], "nt")
            dp_scr[idx] = _dot(da_ref[rs, cs], v_all[ks, cs], "nt")
        for idx, (sb, h) in enumerate(tiles):
            cs = slice(h * HEAD_DIM, (h + 1) * HEAD_DIM)
            rs = slice(sb * BAND, (sb + 1) * BAND)
            ltv = jnp.concatenate([lt_ref[rs, cs]] * 2, axis=1)
            cc = jnp.concatenate([c_ref[rs, cs]] * 2, axis=1)
            p = jnp.exp(jnp.where(band_first if sb == 0 else band, s_scr[idx] * scale - ltv, NEG))
            p_scr[idx] = p.astype(BF16)
            ds_scr[idx] = (p * (dp_scr[idx] - cc) * scale).astype(BF16)
        for idx, (sb, h) in enumerate(tiles):
            cs = slice(h * HEAD_DIM, (h + 1) * HEAD_DIM)
            rs, ks = slice(sb * BAND, (sb + 1) * BAND), slice(sb * BAND, (sb + 2) * BAND)
            d_ref[rs, cs] = _dot(ds_scr[idx], k_all[ks, cs], "nn").astype(BF16)
            dk_acc[ks, cs] += _dot(ds_scr[idx], q_ref[rs, cs], "tn")
            dv_acc[ks, cs] += _dot(p_scr[idx], da_ref[rs, cs], "tn")
        d_ref[:, GROUP_W:2 * GROUP_W] = dk_acc[BAND:rows + BAND, :].astype(BF16)
        d_ref[:, 2 * GROUP_W:3 * GROUP_W] = dv_acc[BAND:rows + BAND, :].astype(BF16)
        car_k[...] = dk_acc[0:BAND, :]
        car_v[...] = dv_acc[0:BAND, :]

    def rev(tg):
        return steps - 1 - tg

    def prev(tg):
        return jnp.maximum(qb * rev(tg) - 1, 0)

    cur = lambda c: pl.BlockSpec((None, rows, GROUP_W), lambda r, tg, c=c: (r, rev(tg), c))
    prv = lambda c: pl.BlockSpec((None, BAND, GROUP_W), lambda r, tg, c=c: (r, prev(tg), c))
    return _pcall(
        body, name, (dil, steps), [a_g, a_g, a_g, a_g, a_g, da_g, c_g, lt_g],
        [cur(0), cur(1), cur(2), prv(1), prv(2), cur(0), cur(0), cur(0)],
        [jax.ShapeDtypeStruct((dil, m_len, ATTN_W), BF16)], [pl.BlockSpec((None, rows, ATTN_W), lambda r, tg: (r, rev(tg), 0))],
        [pltpu.VMEM((rows + BAND, GROUP_W), F32), pltpu.VMEM((rows + BAND, GROUP_W), F32),
         pltpu.VMEM((BAND, GROUP_W), F32), pltpu.VMEM((BAND, GROUP_W), F32),
         pltpu.VMEM((rows + BAND, GROUP_W), BF16), pltpu.VMEM((rows + BAND, GROUP_W), BF16),
         pltpu.VMEM((len(tiles), BAND, 2 * BAND), F32), pltpu.VMEM((len(tiles), BAND, 2 * BAND), F32),
         pltpu.VMEM((len(tiles), BAND, 2 * BAND), BF16), pltpu.VMEM((len(tiles), BAND, 2 * BAND), BF16)], comm=comm)


def _dqkv_post(d_gs, tabs, dz):
    seq = dz.shape[0]

    def body(g0, g1, g2, c_ref, s1_ref, s2_ref, dz_any, o_ref, sc):
        del dz_any
        ins = (g0, g1, g2)
        c, s1, s2 = c_ref[...], s1_ref[...], s2_ref[...]
        for part in range(3):
            for hh in range(N_GROUPS * HEADS_PER_GROUP):
                g, hl = divmod(hh, HEADS_PER_GROUP)
                icol = part * GROUP_W + hl * HEAD_DIM
                ocol = part * ATTN_W + hh * HEAD_DIM
                d = DILATIONS[g]
                if d == 1:
                    x = ins[g][0, :, icol:icol + HEAD_DIM].astype(F32)
                else:
                    for r in range(d):
                        sc[pl.ds(r, ROW_BLK // d, stride=d), :] = ins[g][r, :, icol:icol + HEAD_DIM].astype(F32)
                    x = sc[...]
                if part < 2:
                    x = x * c + pltpu.roll(x * s1, HEAD_DIM - ROT_DIM // 2, 1) + pltpu.roll(x * s2, ROT_DIM // 2, 1)
                o_ref[:, ocol:ocol + HEAD_DIM] = x.astype(BF16)

    tab_spec = pl.BlockSpec((ROW_BLK, HEAD_DIM), lambda i: (i, 0))
    return pl.pallas_call(
        body, name="dqkv_post", out_shape=jax.ShapeDtypeStruct(dz.shape, BF16), grid=(seq // ROW_BLK,),
        in_specs=_group_specs(ATTN_W) + [tab_spec, tab_spec, tab_spec, pl.BlockSpec(memory_space=pl.ANY)],
        out_specs=pl.BlockSpec((ROW_BLK, QKV_W), lambda i: (i, 0)),
        scratch_shapes=[pltpu.VMEM((ROW_BLK, HEAD_DIM), F32)], input_output_aliases={6: 0},
        compiler_params=_params(dimension_semantics=("arbitrary",)),
    )(*d_gs, *tabs, dz)


def _glu(zg):
    a = zg[:, :CONV_CH].astype(F32)
    s = _sigmoid(zg[:, CONV_CH:].astype(F32))
    return a, s, a * s


def _shifted_copies(xs):
    n = xs.shape[1] - SUBLANES
    for b in range(1, SUBLANES):
        xs[b, 0:n, :] = xs[0, pl.ds(b, n), :]


def _shifted(xs, offset, r0, cs):
    a, b = divmod(offset, SUBLANES)
    return xs[b, pl.ds(SUBLANES * a + r0, CONV_ROWS), cs]


def _conv_fwd(z, cw, cb, lg, lb, comm=_NO_COMM):
    seq = z.shape[0]
    halo_per_blk = ROW_BLK // CONV_HALO

    def body(zg_ref, zh_ref, cw_ref, cb_ref, lg_ref, lb_ref, c2_ref, c4_ref, xs):
        i = pl.program_id(0)
        _, _, c1 = _glu(zg_ref[...])
        _, _, c1h = _glu(zh_ref[...])
        xs[0, 0:CONV_HALO, :] = jnp.where(i > 0, c1h, 0.0)
        xs[0, CONV_HALO:, :] = c1
        _shifted_copies(xs)
        for s in range(CONV_CH // HEAD_DIM):
            cs = slice(s * HEAD_DIM, (s + 1) * HEAD_DIM)
            taps = [cw_ref[j:j + 1, cs] for j in range(CONV_K)]
            bias = cb_ref[:, cs]

            def chunk(rc, carry, cs=cs, taps=taps, bias=bias):
                r0 = pl.multiple_of(rc * CONV_ROWS, CONV_ROWS)
                acc = jnp.zeros((CONV_ROWS, HEAD_DIM), F32)
                for j in range(CONV_K):
                    acc = acc + taps[j] * _shifted(xs, CONV_HALO - (CONV_K - 1) + j, r0, cs)
                c2_ref[pl.ds(r0, CONV_ROWS), cs] = acc + bias
                return carry

            lax.fori_loop(0, ROW_BLK // CONV_ROWS, chunk, 0)
        c2 = c2_ref[...]
        mu = jnp.mean(c2, axis=-1, keepdims=True)
        xc = c2 - mu
        rstd = lax.rsqrt(jnp.mean(xc * xc, axis=-1, keepdims=True) + EPS)
        c3 = xc * rstd * lg_ref[...] + lb_ref[...]
        c4_ref[...] = (c3 * _sigmoid(c3)).astype(BF16)

    vec = pl.BlockSpec((1, CONV_CH), lambda i: (0, 0))
    return _pcall(
        body, "conv_fwd", (seq // ROW_BLK,), [z, z, cw, cb, lg, lb],
        [pl.BlockSpec((ROW_BLK, 2 * CONV_CH), lambda i: (i, GLU_COL_BLK)),
         pl.BlockSpec((CONV_HALO, 2 * CONV_CH), lambda i: (jnp.maximum(i * halo_per_blk - 1, 0), GLU_COL_BLK)),
         pl.BlockSpec((CONV_HALO, CONV_CH), lambda i: (0, 0)), vec, vec, vec],
        [jax.ShapeDtypeStruct((seq, CONV_CH), F32), jax.ShapeDtypeStruct((seq, CONV_CH), BF16)],
        [pl.BlockSpec((ROW_BLK, CONV_CH), lambda i: (i, 0)), pl.BlockSpec((ROW_BLK, CONV_CH), lambda i: (i, 0))],
        [pltpu.VMEM((SUBLANES, ROW_BLK + CONV_HALO, CONV_CH), F32)], comm=comm)


def _conv_bwd(dc2, z, cw, dz, comm=_NO_COMM):
    seq = z.shape[0]
    halo_per_blk = ROW_BLK // CONV_HALO
    n_blk = seq // ROW_BLK
    last_halo = seq // CONV_HALO - 1

    def body(dc_ref, dn_ref, zg_ref, zh_ref, cw_ref, dz_any, o_ref, dcw_ref, xs, ys, dc1_ref, dcw_acc):
        del dz_any
        i = pl.program_id(0)
        a, s, c1 = _glu(zg_ref[...])
        _, _, c1h = _glu(zh_ref[...])
        xs[0, 0:CONV_HALO, :] = jnp.where(i > 0, c1h, 0.0)
        xs[0, CONV_HALO:, :] = c1
        ys[0, 0:ROW_BLK, :] = dc_ref[...]
        ys[0, ROW_BLK:, :] = jnp.where(i < n_blk - 1, dn_ref[...], 0.0)
        _shifted_copies(xs)
        _shifted_copies(ys)

        @pl.when(i == 0)
        def _():
            dcw_acc[...] = jnp.zeros_like(dcw_acc)

        for sl in range(CONV_CH // HEAD_DIM):
            cs = slice(sl * HEAD_DIM, (sl + 1) * HEAD_DIM)
            taps = [cw_ref[j:j + 1, cs] for j in range(CONV_K)]

            def chunk(rc, carry, cs=cs, taps=taps):
                r0 = pl.multiple_of(rc * CONV_ROWS, CONV_ROWS)
                dc = ys[0, pl.ds(r0, CONV_ROWS), cs]
                acc = jnp.zeros((CONV_ROWS, HEAD_DIM), F32)
                for j in range(CONV_K):
                    prod = dc * _shifted(xs, CONV_HALO - (CONV_K - 1) + j, r0, cs)
                    dcw_acc[j, :, cs] += jnp.sum(prod.reshape(CONV_ROWS // SUBLANES, SUBLANES, HEAD_DIM), axis=0)
                    acc = acc + taps[j] * _shifted(ys, CONV_K - 1 - j, r0, cs)
                dc1_ref[pl.ds(r0, CONV_ROWS), cs] = acc
                return carry

            lax.fori_loop(0, ROW_BLK // CONV_ROWS, chunk, 0)
        dc1 = dc1_ref[...]
        o_ref[:, :CONV_CH] = (dc1 * s).astype(BF16)
        o_ref[:, CONV_CH:] = (dc1 * a * s * (1.0 - s)).astype(BF16)

        @pl.when(i == n_blk - 1)
        def _():
            dcw_ref[...] = jnp.sum(dcw_acc[...], axis=1)

    return _pcall(
        body, "conv_bwd", (n_blk,), [dc2, dc2, z, z, cw, dz],
        [pl.BlockSpec((ROW_BLK, CONV_CH), lambda i: (i, 0)),
         pl.BlockSpec((CONV_HALO, CONV_CH), lambda i: (jnp.minimum((i + 1) * halo_per_blk, last_halo), 0)),
         pl.BlockSpec((ROW_BLK, 2 * CONV_CH), lambda i: (i, GLU_COL_BLK)),
         pl.BlockSpec((CONV_HALO, 2 * CONV_CH), lambda i: (jnp.maximum(i * halo_per_blk - 1, 0), GLU_COL_BLK)),
         pl.BlockSpec((CONV_HALO, CONV_CH), lambda i: (0, 0)),
         pl.BlockSpec(memory_space=pl.ANY)],
        [jax.ShapeDtypeStruct(dz.shape, BF16), jax.ShapeDtypeStruct((CONV_HALO, CONV_CH), F32)],
        [pl.BlockSpec((ROW_BLK, 2 * CONV_CH), lambda i: (i, GLU_COL_BLK)), pl.BlockSpec((CONV_HALO, CONV_CH), lambda i: (0, 0))],
        [pltpu.VMEM((SUBLANES, ROW_BLK + CONV_HALO, CONV_CH), F32), pltpu.VMEM((SUBLANES, ROW_BLK + CONV_HALO, CONV_CH), F32),
         pltpu.VMEM((ROW_BLK, CONV_CH), F32), pltpu.VMEM((CONV_HALO, SUBLANES, CONV_CH), F32)],
        aliases={5: 0}, comm=comm)


def _epi_mix(ya, c4, wcp, gates, bg):
    yc = _dot(c4, wcp, "nn")
    gv = _sigmoid(gates.astype(F32) + bg)
    merged = gv[:, :D_MODEL] * ya + gv[:, D_MODEL:] * yc
    return merged, ya, yc


def _epi_residual_rms(acc, xres, g):
    x = xres + acc
    return x, _rms_fwd_vals(x, g)


def _cross_scores(cq, ck):
    out = []
    for h in range(CROSS_HEADS):
        cs = slice(h * CROSS_HD, (h + 1) * CROSS_HD)
        s = _dot(cq[:, cs], ck[:, cs], "nt") * (CROSS_HD ** -0.5)
        e = jnp.exp(s - jnp.max(s, axis=-1, keepdims=True))
        out.append((cs, e, jnp.sum(e, axis=-1, keepdims=True)))
    return out


def _epi_cross_fwd(acc, ck, cv):
    cq = acc.astype(BF16)
    co = [_dot(e, cv[:, cs], "nn") / den for cs, e, den in _cross_scores(cq, ck)]
    return cq, jnp.concatenate(co, axis=1)


def _epi_cross_bwd(dco, cq, ck, cv):
    dco = dco.astype(BF16)
    dcq, dck, dcv = [], [], []
    for cs, e, den in _cross_scores(cq, ck):
        p = e / den
        dp = _dot(dco[:, cs], cv[:, cs], "nt")
        ds = (p * (dp - jnp.sum(dp * p, axis=-1, keepdims=True)) * (CROSS_HD ** -0.5)).astype(BF16)
        dcq.append(_dot(ds, ck[:, cs], "nn"))
        dck.append(_dot(ds, cq[:, cs], "tn"))
        dcv.append(_dot(p, dco[:, cs], "tn"))
    return jnp.concatenate(dcq, axis=1), jnp.concatenate(dck, axis=1), jnp.concatenate(dcv, axis=1)


def _epi_mlp_up(acc):
    return acc, jnp.square(jnp.maximum(acc, 0.0))


def _epi_final(acc, x2, tgt, g):
    x3 = x2 + acc
    err = _rms_fwd_vals(x3, g) - tgt
    loss = (0.5 / D_MODEL) * jnp.sum(err * err)
    dx3, dg = _rms_bwd_vals(x3, g, err * (1.0 / D_MODEL))
    return dx3, dx3, jnp.full((1, HEAD_DIM), loss, F32), dg


def _epi_mlp_down_bwd(dh, hpre):
    return (dh * 2.0 * jnp.maximum(hpre.astype(F32), 0.0),)


def _epi_rms_bwd(du, x, g, dres):
    dx, dg = _rms_bwd_vals(x, g, du)
    return dres + dx, dg


def _epi_rms_bwd_2(du, x, g, dres):
    dx, dg = _epi_rms_bwd(du, x, g, dres)
    return dx, dx, dg


def _epi_rms_bwd_g(du, x, g):
    return (_rms_bwd_vals(x, g, du)[1],)


def _epi_mix_bwd(dm, ya, yc, gates, bg):
    gv = _sigmoid(gates.astype(F32) + bg)
    ga, gb = gv[:, :D_MODEL], gv[:, D_MODEL:]
    ya, yc = ya.astype(F32), yc.astype(F32)
    dgate = jnp.concatenate([dm * ya * ga * (1.0 - ga), dm * yc * gb * (1.0 - gb)], axis=1)
    return dm * ga, dm * gb, dgate, jnp.sum(dgate, axis=0, keepdims=True)


def _epi_ln_bwd(dc4, c2, lg, lb):
    mu = jnp.mean(c2, axis=-1, keepdims=True)
    xc = c2 - mu
    rstd = lax.rsqrt(jnp.mean(xc * xc, axis=-1, keepdims=True) + EPS)
    xh = xc * rstd
    c3 = xh * lg + lb
    sg = _sigmoid(c3)
    dc3 = dc4 * sg * (1.0 + c3 * (1.0 - sg))
    dxh = dc3 * lg
    dc2 = rstd * (dxh - jnp.mean(dxh, axis=-1, keepdims=True) - xh * jnp.mean(dxh * xh, axis=-1, keepdims=True))
    return (dc2, jnp.sum(dc3 * xh, axis=0, keepdims=True), jnp.sum(dc3, axis=0, keepdims=True),
            jnp.sum(dc2, axis=0, keepdims=True))


def _sds(shape, dtype):
    return jax.ShapeDtypeStruct(shape, dtype)


class _Lazy:
    def __init__(self, fn):
        self.fn = fn

    def __getitem__(self, key):
        return self.fn(key)


def _local_step(x, mem, tgt, sm, plan):
    w = _Lazy(plan.w)
    dw = {}

    def carry(name, n_own, fn, *args, **kw):
        c = plan.comm(name, dw)
        res = fn(*args, comm=c, **kw)
        plan.done(name, res[n_own:])
        return res[:n_own]

    def mm(name, *args, **kw):
        return carry(name, len(args[6]), _mm, name, *args, **kw)

    seq = x.shape[0]
    nr = seq // ROW_BLK
    big = min(1024, seq)
    nb = seq // big
    row = lambda n: ((ROW_BLK, n), lambda i, j, k: (i, 0))
    vec = lambda n: ((1, n), lambda i, j, k: (0, 0))
    full = lambda r, c: ((r, c), lambda i, j, k: (0, 0))
    gates_blk = ((ROW_BLK, 2 * D_MODEL), lambda i, j, k: (i, GATE_COL_BLK))
    tabs = _rope_tables(seq)

    u = carry("rms_mix", 1, _rms_fwd, "rms_mix", x, sm["g_mix"], ROW_BLK)[0]
    whole3 = lambda a: (a.shape, lambda i, j, k: (0, 0, 0))
    z = plan.project_in(u)
    a_gs = carry("qkv_prep", 3, _qkv_prep, z, tabs)
    os_, ls_ = [], []
    for g in range(N_GROUPS):
        name = "attn_fwd_%d" % g
        o_g, l_g = carry(name, 2, _attn_fwd, name, a_gs[g])
        os_.append(o_g)
        ls_.append(l_g)
    attn, lt = _attn_merge(os_, ls_, seq)
    c2, c4 = carry("conv_fwd", 2, _conv_fwd, z, w["taps"], sm["conv_b"], sm["conv_ln_g"], sm["conv_ln_b"])
    merged, ya, yc = mm(
        "mix", attn, w["w_attn_proj"], "nn", (nr, 1, 1), row(GROUP_W), full(GROUP_W, D_MODEL),
        [(_sds((seq, D_MODEL), BF16), *row(D_MODEL))] * 3,
        extras=[(c4, *row(CONV_CH)), (w["w_conv_proj"], *full(CONV_CH, D_MODEL)), (z, *gates_blk), (sm["b_gate"], *vec(2 * D_MODEL))],
        epi=_epi_mix)
    x1, uq = mm("out_proj", merged, w["w_out"], "nn", (nr, 1, 1), row(D_MODEL), full(D_MODEL, D_MODEL),
                 [(_sds((seq, D_MODEL), F32), *row(D_MODEL)), (_sds((seq, D_MODEL), BF16), *row(D_MODEL))],
                 extras=[(x, *row(D_MODEL)), (sm["g_cross"], *vec(D_MODEL))], epi=_epi_residual_rms)

    mn = _rms_fwd("rms_mem", mem, sm["g_mem"], N_MEM)[0]
    ckv = mm("ckv_proj", mn, w["w_ckv"], "nn", (1, N_DEV, 1), full(N_MEM, D_MODEL),
              ((None, D_MODEL, 2 * D_MODEL // N_DEV), lambda i, j, k: (j, 0, 0)),
              [(_sds((N_MEM, 2 * D_MODEL), BF16), (N_MEM, 2 * D_MODEL // N_DEV), lambda i, j, k: (0, j))])[0]
    ck, cv = ckv[:, :D_MODEL], ckv[:, D_MODEL:]
    kv_blk = full(N_MEM, D_MODEL)
    cq, co = mm("cq_proj_cross", uq, w["w_cq"], "nn", (nr, 1, 1), row(D_MODEL), full(D_MODEL, D_MODEL),
                 [(_sds((seq, D_MODEL), BF16), *row(D_MODEL))] * 2,
                 extras=[(ck, *kv_blk), (cv, *kv_blk)], epi=_epi_cross_fwd)
    x2, um = mm("co_proj", co, w["w_co"], "nn", (nr, 1, 1), row(D_MODEL), full(D_MODEL, D_MODEL),
                 [(_sds((seq, D_MODEL), F32), *row(D_MODEL)), (_sds((seq, D_MODEL), BF16), *row(D_MODEL))],
                 extras=[(x1, *row(D_MODEL)), (sm["g_mlp"], *vec(D_MODEL))], epi=_epi_residual_rms)

    ff_blk = D_FF // N_DEV
    row_f32 = (_sds((seq, D_MODEL), F32), *row(D_MODEL))
    row_bf16 = (_sds((seq, D_MODEL), BF16), *row(D_MODEL))
    col_sum = (_sds((1, D_MODEL), F32), *vec(D_MODEL))
    hpre, h = mm("mlp_up", um, w["w_up"], "nn", (nr, 1, 1), row(D_MODEL), whole3(w["w_up"]),
                 [(_sds((seq, D_FF), BF16), *row(D_FF))] * 2, epi=_epi_mlp_up, split=("cols", N_DEV), b_resident=True)
    kt = D_FF // D_MODEL
    dx3, dx3b, loss, dg_final = mm(
        "mlp_down_loss", h, w["w_down"], "nn", (nr, 1, 1), row(D_FF), full(D_FF, D_MODEL),
        [row_f32, row_bf16, (_sds((1, HEAD_DIM), F32), *vec(HEAD_DIM)), col_sum],
        extras=[(x2, *row(D_MODEL)), (tgt, *row(D_MODEL)), (sm["g_final"], *vec(D_MODEL))], epi=_epi_final, acc_outs=(2, 3),
        b_resident=True)

    dhpre = mm("mlp_down_bwd", dx3b, w["w_down"], "nt", (nr, 1, 1), row(D_MODEL), full(D_FF, D_MODEL),
               [(_sds((seq, D_FF), BF16), *row(D_FF))], extras=[(hpre, *row(D_FF))], epi=_epi_mlp_down_bwd,
               split=("cols", kt), b_resident=True)[0]
    big2 = min(2 * big, seq)
    nb2 = seq // big2
    dw["w_down"] = mm("dw_down", h, dx3b, "tn", (kt, 1, nb2), ((big2, D_MODEL), lambda i, j, k: (k, i)),
                      ((big2, D_MODEL), lambda i, j, k: (k, 0)),
                      [(_sds((D_FF, D_MODEL), BF16), (D_MODEL, D_MODEL), lambda i, j, k: (i, 0))])[0]
    dx2, dx2b, dg_mlp = mm("mlp_up_bwd", dhpre, w["w_up"], "nt", (nr, 1, 1), row(D_FF), whole3(w["w_up"]),
                           [row_f32, row_bf16, col_sum],
                           extras=[(x2, *row(D_MODEL)), (sm["g_mlp"], *vec(D_MODEL)), (dx3, *row(D_MODEL))],
                           epi=_epi_rms_bwd_2, acc_outs=(2,), split=("sum", N_DEV), b_resident=True)
    dw["w_up"] = mm("dw_up", um, dhpre, "tn", (1, N_DEV, nb2), ((big2, D_MODEL), lambda i, j, k: (k, 0)),
                    ((big2, ff_blk), lambda i, j, k: (k, j)),
                    [(_sds((N_DEV, D_MODEL, ff_blk), BF16), (None, D_MODEL, ff_blk), lambda i, j, k: (j, 0, 0))])[0]

    acc_kv = (_sds((N_MEM, D_MODEL), F32), *kv_blk)
    dcq, dck, dcv = mm("co_proj_bwd_cross", dx2b, w["w_co"], "nt", (nr, 1, 1), row(D_MODEL), full(D_MODEL, D_MODEL),
                       [row_bf16, acc_kv, acc_kv],
                       extras=[(cq, *row(D_MODEL)), (ck, *kv_blk), (cv, *kv_blk)], epi=_epi_cross_bwd, acc_outs=(1, 2))

    def dw_square(name, act, grad):
        return mm(name, act, grad, "tn", (1, 1, nb2), ((big2, D_MODEL), lambda i, j, k: (k, 0)),
                  ((big2, D_MODEL), lambda i, j, k: (k, 0)), [(_sds((D_MODEL, D_MODEL), BF16), *full(D_MODEL, D_MODEL))])[0]

    dw["w_co"] = dw_square("dw_co", co, dx2b)
    dx1, dx1b, dg_cross = mm("cq_proj_bwd", dcq, w["w_cq"], "nt", (nr, 1, 1), row(D_MODEL), full(D_MODEL, D_MODEL),
                             [row_f32, row_bf16, col_sum],
                             extras=[(x1, *row(D_MODEL)), (sm["g_cross"], *vec(D_MODEL)), (dx2, *row(D_MODEL))],
                             epi=_epi_rms_bwd_2, acc_outs=(2,))
    dw["w_cq"] = dw_square("dw_cq", uq, dcq)
    dckv = jnp.concatenate([dck, dcv], axis=1)
    kv_chunk = 2 * D_MODEL // N_DEV
    dw["w_ckv"] = mm("dw_ckv", mn, dckv, "tn", (1, N_DEV, 1), full(N_MEM, D_MODEL), ((N_MEM, kv_chunk), lambda i, j, k: (0, j)),
                      [(_sds((N_DEV, D_MODEL, kv_chunk), BF16), (None, D_MODEL, kv_chunk), lambda i, j, k: (j, 0, 0))])[0]
    dg_mem = mm("ckv_proj_bwd", dckv, w["w_ckv"], "nt", (1, 1, N_DEV), ((N_MEM, kv_chunk), lambda i, j, k: (0, k)),
                 ((None, D_MODEL, kv_chunk), lambda i, j, k: (k, 0, 0)), [(_sds((1, D_MODEL), F32), *vec(D_MODEL))],
                 extras=[(mem, *full(N_MEM, D_MODEL)), (sm["g_mem"], *vec(D_MODEL))], epi=_epi_rms_bwd_g, acc_outs=(0,))[0]

    dya, dyc, dz, db_gate = mm(
        "out_proj_bwd_mix", dx1b, w["w_out"], "nt", (nr, 1, 1), row(D_MODEL), full(D_MODEL, D_MODEL),
        [(_sds((seq, D_MODEL), BF16), *row(D_MODEL)), (_sds((seq, D_MODEL), BF16), *row(D_MODEL)),
         (_sds((seq, IN_W), BF16), *gates_blk), (_sds((1, 2 * D_MODEL), F32), *vec(2 * D_MODEL))],
        extras=[(ya, *row(D_MODEL)), (yc, *row(D_MODEL)), (z, *gates_blk), (sm["b_gate"], *vec(2 * D_MODEL))],
        epi=_epi_mix_bwd, acc_outs=(3,))
    dw["w_out"] = dw_square("dw_out", merged, dx1b)
    dattn = mm("attn_proj_bwd", dya, w["w_attn_proj"], "nt", (nr, 1, 1), row(D_MODEL), full(GROUP_W, D_MODEL),
                [(_sds((seq, GROUP_W), BF16), *row(GROUP_W))])[0]
    pc = D_MODEL // N_DEV
    dw["w_attn_proj"] = mm("dw_attn_proj", attn, dya, "tn", (1, 1, nb2), ((big2, GROUP_W), lambda i, j, k: (k, 0)),
                           ((big2, D_MODEL), lambda i, j, k: (k, 0)),
                           [(_sds((N_DEV, GROUP_W, pc), BF16), (N_DEV, GROUP_W, pc), lambda i, j, k: (0, 0, 0))],
                           out_chunks=N_DEV)[0]
    cvec = (_sds((1, CONV_CH), F32), *vec(CONV_CH))
    dc2, dg_ln_g, dg_ln_b, dg_conv_b = mm(
        "conv_proj_bwd_ln", dyc, w["w_conv_proj"], "nt", (nr, 1, 1), row(D_MODEL), full(CONV_CH, D_MODEL),
        [(_sds((seq, CONV_CH), F32), *row(CONV_CH)), cvec, cvec, cvec],
        extras=[(c2, *row(CONV_CH)), (sm["conv_ln_g"], *vec(CONV_CH)), (sm["conv_ln_b"], *vec(CONV_CH))],
        epi=_epi_ln_bwd, acc_outs=(1, 2, 3))
    dw["w_conv_proj"] = mm("dw_conv_proj", c4, dyc, "tn", (1, 1, nb2), ((big2, CONV_CH), lambda i, j, k: (k, 0)),
                           ((big2, D_MODEL), lambda i, j, k: (k, 0)),
                           [(_sds((N_DEV, CONV_CH, pc), BF16), (N_DEV, CONV_CH, pc), lambda i, j, k: (0, 0, 0))],
                           out_chunks=N_DEV)[0]
    dz, dg_conv_w = carry("conv_bwd", 2, _conv_bwd, dc2, z, w["taps"], dz)
    preps = _attn_bwd_prep(dattn, attn, lt)
    d_gs = []
    for g in range(N_GROUPS):
        name = "attn_bwd_%d" % g
        d_gs.append(carry(name, 1, _attn_bwd, name, a_gs[g], *preps[g])[0])
    dz = _dqkv_post(d_gs, tabs, dz)
    dw["w_in"] = mm("dw_in", u, dz, "tn", (1, N_DEV, nb2), ((big2, D_MODEL), lambda i, j, k: (k, 0)),
                    ((big2, D_MODEL), lambda i, j, k: (k, j)),
                    [(_sds((N_DEV, D_MODEL, D_MODEL), BF16), (None, D_MODEL, D_MODEL), lambda i, j, k: (j, 0, 0))])[0]
    grad_x, dg_mix = mm("in_proj_bwd", dz, w["w_in"], "nt", (nr, 1, 1), row(IN_W), whole3(w["w_in"]), [row_f32, col_sum],
                        extras=[(x, *row(D_MODEL)), (sm["g_mix"], *vec(D_MODEL)), (dx1, *row(D_MODEL))],
                        epi=_epi_rms_bwd, acc_outs=(1,), split=("sum", N_DEV), b_resident=True)
    small = dict(g_mix=dg_mix, b_gate=db_gate, conv_b=dg_conv_b, conv_ln_g=dg_ln_g, conv_ln_b=dg_ln_b, g_cross=dg_cross,
                 g_mem=dg_mem, g_mlp=dg_mlp, g_final=dg_final, loss=loss, conv_w=dg_conv_w)
    return grad_x, dw, small


SHARD_SHAPE = dict(w_in=(1024, 1024), w_attn_proj=(512, 128), w_conv_proj=(768, 128), w_out=(128, 1024), w_cq=(128, 1024),
                   w_ckv=(1024, 256), w_co=(128, 1024), w_up=(1024, 512), w_down=(512, 1024))
FWD_CARRY = {"in_proj":("w_attn_proj", "w_conv_proj", "w_out", "w_cq", "w_ckv", "w_co", "taps"),
             "qkv_prep": ("w_up",), "conv_fwd": ("w_down",)}
BWD_CARRY = {"dw_up": ("w_down",), "out_proj_bwd_mix": ("w_co", "w_cq"), "conv_bwd": ("w_up", "w_ckv"),
             "attn_bwd_0": ("w_out",), "attn_bwd_1": ("w_attn_proj", "w_conv_proj"), "in_proj_bwd": ("w_in",)}


def _cols_to_2d(a):
    return a.transpose(1, 0, 2).reshape(a.shape[1], -1)


def _in_proj_gather(u, w_shard, comm):
    seq = u.shape[0]
    tm = min(1024, seq)
    x, y, c = lax.axis_index("x"), lax.axis_index("y"), lax.axis_index("c")
    ident = lambda px, py, pc: 4 * px + 2 * py + pc
    far = [(1 - x, y), (x, 1 - y), (1 - x, 1 - y)]
    order = jnp.stack([ident(x, y, c), ident(x, y, 1 - c)] + [ident(px, py, c) for px, py in far]
                      + [ident(px, py, 1 - c) for px, py in far]).astype(jnp.int32)
    n_far = len(far)

    def body(order_ref, u_ref, wsh_ref, *rest):
        c_in, z_ref, wg_ref = rest[:comm.n], rest[comm.n], rest[comm.n + 1]
        c_out = rest[comm.n + 2:2 * comm.n + 2]
        wbuf, load_sem, local_sem, recv_sems, ici_send, d2d_send = rest[2 * comm.n + 2:2 * comm.n + 8]
        sems = rest[2 * comm.n + 8:]
        jj, i = pl.program_id(0), pl.program_id(1)
        (kx, ky, kc), me, chips = _Comm._where()
        sibling = (kx, ky, 1 - kc)
        n = order_ref[jj]

        def push(src, blk, send, to):
            return pltpu.make_async_remote_copy(src_ref=src, dst_ref=wg_ref.at[blk], send_sem=send,
                                                recv_sem=recv_sems.at[blk], device_id=to, device_id_type=MESH)

        def load(src):
            cp = pltpu.make_async_copy(src, wbuf, load_sem)
            cp.start()
            cp.wait()

        @pl.when(jnp.logical_and(jj == 0, i == 0))
        def _():
            push(wsh_ref, me, d2d_send, sibling).start()
            for (px, py) in chips:
                push(wsh_ref, me, ici_send, (px, py, kc)).start()
            pltpu.make_async_copy(wsh_ref, wg_ref.at[me], local_sem).start()
            if comm.n:
                comm.start(c_in, c_out, sems)
            load(wsh_ref)

        @pl.when(jnp.logical_and(jj > 0, i == 0))
        def _():
            push(wg_ref.at[n], n, d2d_send, sibling).wait_recv()
            for idx, (px, py) in enumerate(chips):
                @pl.when(jj == 2 + idx)
                def _():
                    blk = 4 * px + 2 * py + kc
                    push(wg_ref.at[blk], blk, d2d_send, sibling).start()

            load(wg_ref.at[n])

        z_ref[...] = _dot(u_ref[...], wbuf[...], "nn").astype(BF16)

        @pl.when(jnp.logical_and(jj == N_DEV - 1, i == pl.num_programs(1) - 1))
        def _():
            def drain_sends(send, count):
                blocks = wg_ref.at[pl.ds(0, count)]
                pltpu.make_async_remote_copy(src_ref=blocks, dst_ref=blocks, send_sem=send, recv_sem=recv_sems.at[0],
                                             device_id=sibling, device_id_type=MESH).wait_send()

            drain_sends(ici_send, n_far)
            drain_sends(d2d_send, n_far + 1)
            pltpu.make_async_copy(wsh_ref, wg_ref.at[me], local_sem).wait()
            if comm.n:
                comm.wait(c_in, c_out, sems)

    any_spec = pl.BlockSpec(memory_space=pl.ANY)
    grid_spec = pltpu.PrefetchScalarGridSpec(
        num_scalar_prefetch=1, grid=(N_DEV, seq // tm),
        in_specs=[pl.BlockSpec((tm, D_MODEL), lambda jj, i, order_ref: (i, 0)), any_spec] + comm.in_specs,
        out_specs=[pl.BlockSpec((tm, D_MODEL), lambda jj, i, order_ref: (i, order_ref[jj])), any_spec] + comm.out_specs,
        scratch_shapes=[pltpu.VMEM((D_MODEL, D_MODEL), BF16), pltpu.SemaphoreType.DMA, pltpu.SemaphoreType.DMA,
                        pltpu.SemaphoreType.DMA((N_DEV,)), pltpu.SemaphoreType.DMA, pltpu.SemaphoreType.DMA] + comm.scratch)
    return pl.pallas_call(
        body, name="in_proj_gather", grid_spec=grid_spec,
        out_shape=[jax.ShapeDtypeStruct((seq, IN_W), BF16), jax.ShapeDtypeStruct((N_DEV, D_MODEL, D_MODEL), BF16)] + comm.out_shape,
        compiler_params=_params(dimension_semantics=("arbitrary", "arbitrary")),
    )(order, u, w_shard, *comm.arrays)


class _Plan:
    def __init__(self, shards, n_tap_cols):
        self.shards = shards
        self.gathered = {}
        self.parts = {}
        self.n_tap_cols = n_tap_cols

    def project_in(self, u):
        comm = self.comm("in_proj", None)
        res = _in_proj_gather(u, self.shards["w_in"], comm)
        self.gathered["w_in"] = res[1]
        self.done("in_proj", res[2:])
        return res[0]

    def comm(self, name, dw):
        if name in FWD_CARRY:
            return _Comm(replicated=[self.shards[k] for k in FWD_CARRY[name]])
        if name in BWD_CARRY:
            return _Comm(chunked=[dw[k].reshape((N_DEV,) + SHARD_SHAPE[k]) for k in BWD_CARRY[name]])
        return _NO_COMM

    def done(self, name, got):
        if name in FWD_CARRY:
            self.gathered.update(zip(FWD_CARRY[name], got))
        elif name in BWD_CARRY:
            self.parts.update(zip(BWD_CARRY[name], got))

    def w(self, key):
        g = self.gathered[key]
        if key in ("w_in", "w_up", "w_ckv"):
            return g
        if key in ("w_attn_proj", "w_conv_proj"):
            return _cols_to_2d(g)
        if key == "taps":
            return jnp.pad(_cols_to_2d(g[:, :CONV_K, :self.n_tap_cols]), ((0, 1), (0, 0)))
        return g.reshape(-1, g.shape[-1])


def _adamw(name, w, m, v, parts, comm=_NO_COMM):
    rows, cols = w.shape
    n_parts = parts.shape[0]
    rb = rows if rows <= 256 or rows % 256 else 256

    def body(w_ref, m_ref, v_ref, p_ref, g_ref, d_ref, nm_ref, nv_ref):
        g = p_ref[0].astype(F32)
        for q in range(1, n_parts):
            g = g + p_ref[q].astype(F32)
        wv = w_ref[...]
        nm = ADAM_B1 * m_ref[...] + (1.0 - ADAM_B1) * g
        nv = ADAM_B2 * v_ref[...] + (1.0 - ADAM_B2) * jnp.square(g)
        m_hat = nm / (1.0 - ADAM_B1 ** ADAM_STEP)
        v_hat = nv / (1.0 - ADAM_B2 ** ADAM_STEP)
        g_ref[...] = g
        d_ref[...] = -ADAM_LR * (m_hat / (jnp.sqrt(v_hat) + ADAM_EPS) + ADAM_WD * wv)
        nm_ref[...] = nm
        nv_ref[...] = nv

    blk = pl.BlockSpec((rb, cols), lambda i: (i, 0))
    return _pcall(body, name, (rows // rb,), [w, m, v, parts],
                  [blk, blk, blk, pl.BlockSpec((n_parts, rb, cols), lambda i: (0, i, 0))],
                  [jax.ShapeDtypeStruct((rows, cols), F32)] * 4, [blk] * 4, comm=comm)


def _sum_parts(name, parts):
    def body(p_ref, o_ref):
        acc = p_ref[0]
        for q in range(1, parts.shape[0]):
            acc = acc + p_ref[q]
        o_ref[...] = acc

    return _pcall(body, name, (1,), [parts], [pl.BlockSpec(parts.shape, lambda i: (0, 0, 0))],
                  [jax.ShapeDtypeStruct(parts.shape[1:], F32)], [pl.BlockSpec(parts.shape[1:], lambda i: (0, 0))])[0]


BIG = ("w_in", "w_attn_proj", "w_conv_proj", "w_out", "w_cq", "w_ckv", "w_co", "w_up", "w_down")
SMALL = ("g_mix", "b_gate", "conv_b", "conv_ln_g", "conv_ln_b", "g_cross", "g_mem", "g_mlp", "g_final")
SMALL_ORDER = SMALL + ("loss", "conv_w")
WEIGHTS = ("g_mix", "w_in", "b_gate", "conv_w", "conv_b", "conv_ln_g", "conv_ln_b", "w_attn_proj", "w_conv_proj", "w_out",
           "g_cross", "g_mem", "w_cq", "w_ckv", "w_co", "g_mlp", "w_up", "w_down", "g_final")


def kernel(x, mem, g_mix, w_in, b_gate, conv_w, conv_b, conv_ln_g, conv_ln_b, w_attn_proj, w_conv_proj, w_out, g_cross, g_mem, w_cq, w_ckv, w_co, g_mlp, w_up, w_down, g_final, loss_target, m_g_mix, m_w_in, m_b_gate, m_conv_w, m_conv_b, m_conv_ln_g, m_conv_ln_b, m_w_attn_proj, m_w_conv_proj, m_w_out, m_g_cross, m_g_mem, m_w_cq, m_w_ckv, m_w_co, m_g_mlp, m_w_up, m_w_down, m_g_final, v_g_mix, v_w_in, v_b_gate, v_conv_w, v_conv_b, v_conv_ln_g, v_conv_ln_b, v_w_attn_proj, v_w_conv_proj, v_w_out, v_g_cross, v_g_mem, v_w_cq, v_w_ckv, v_w_co, v_g_mlp, v_w_up, v_w_down, v_g_final):
    args = dict(locals())
    wts = {k: args[k] for k in WEIGHTS}
    mom = {k: args["m_" + k] for k in WEIGHTS}
    var = {k: args["v_" + k] for k in WEIGHTS}
    two_d = lambda a: a.reshape(a.shape[-2:]) if a.ndim == 3 else a.reshape(1, -1)

    shards = {k: two_d(wts[k]).astype(BF16) for k in BIG}
    shards["taps"] = jnp.pad(two_d(conv_w), ((0, 1), (0, HEAD_DIM - conv_w.shape[-1])))
    plan = _Plan(shards, conv_w.shape[-1])
    sm = {k: two_d(wts[k]) for k in SMALL}

    grad_x, _, small = _local_step(x[0], mem[0], loss_target[0], sm, plan)
    parts = plan.parts

    out = {}
    small_comm = _Comm(replicated=[small[k] for k in SMALL_ORDER])
    for k in BIG:
        res = _adamw("adamw_" + k, two_d(wts[k]), two_d(mom[k]), two_d(var[k]), parts[k],
                     comm=small_comm if k == BIG[0] else _NO_COMM)
        out[k] = [r.reshape(wts[k].shape) for r in res[:4]]
        if k == BIG[0]:
            small_parts = dict(zip(SMALL_ORDER, res[4:]))
    for k in SMALL:
        res = _adamw("adamw_" + k, two_d(wts[k]), two_d(mom[k]), two_d(var[k]), small_parts[k])
        out[k] = [r.reshape(wts[k].shape) for r in res]
    loss = _sum_parts("loss_sum", small_parts["loss"])[0, 0]
    me = 4 * lax.axis_index("x") + 2 * lax.axis_index("y") + lax.axis_index("c")
    n_tap_cols = conv_w.shape[-1]
    tap_parts = lax.dynamic_slice(small_parts["conv_w"], (0, 0, me * n_tap_cols), (N_DEV, CONV_K, n_tap_cols))
    res = _adamw("adamw_conv_w", two_d(conv_w), two_d(m_conv_w), two_d(v_conv_w), tap_parts)
    out["conv_w"] = [r.reshape(conv_w.shape) for r in res]

    return (loss, grad_x[None], *[out[k][0] for k in WEIGHTS], *[out[k][1] for k in WEIGHTS],
            *[out[k][2] for k in WEIGHTS], *[out[k][3] for k in WEIGHTS])
```

```python
import functools

import jax
import jax.numpy as jnp
import numpy as np
from jax import lax
from jax.experimental import pallas as pl
from jax.experimental.pallas import tpu as pltpu

F32 = jnp.float32
BF16 = jnp.bfloat16

N_DEV = 8
D_MODEL = 1024
N_MEM = 256
HEAD_DIM = 128
HEADS_PER_GROUP = 4
GROUP_W = HEADS_PER_GROUP * HEAD_DIM
DILATIONS = (1, 4, 16)
BAND = 128
N_GROUPS = 3
ATTN_W = N_GROUPS * GROUP_W
QKV_W = 3 * ATTN_W
ROT_DIM = HEAD_DIM // 4
ROPE_THETA = 500000.0
CONV_CH = 768
CONV_K = 31
CONV_HALO = 32
SUBLANES = 8
CONV_ROWS = 64
IN_W = 8192
GLU_COL_BLK = QKV_W // (2 * CONV_CH)
GATE_COL_BLK = (QKV_W + 2 * CONV_CH) // (2 * D_MODEL)
CROSS_HEADS = 4
CROSS_HD = D_MODEL // CROSS_HEADS
D_FF = 4096
EPS = 1e-6
NEG = -1e30
QB = 4
ROW_BLK = QB * BAND

ADAM_LR = 0.001
ADAM_B1 = 0.9
ADAM_B2 = 0.999
ADAM_EPS = 1e-08
ADAM_WD = 0.01
ADAM_STEP = 10

VMEM_LIMIT = 56 * 1024 * 1024
MESH = pl.DeviceIdType.MESH


def _params(**kw):
    return pltpu.CompilerParams(vmem_limit_bytes=VMEM_LIMIT, **kw)


def _sigmoid(x):
    return 1.0 / (1.0 + jnp.exp(-x))


def _dot(a, b, kind):
    dims = {"nn": (((1,), (0,)), ((), ())), "nt": (((1,), (1,)), ((), ())), "tn": (((0,), (0,)), ((), ()))}[kind]
    if a.dtype != BF16:
        a = a.astype(BF16)
    if b.dtype != BF16:
        b = b.astype(BF16)
    return lax.dot_general(a, b, dims, preferred_element_type=F32)


def _peers():
    x, y, c = lax.axis_index("x"), lax.axis_index("y"), lax.axis_index("c")
    me = 4 * x + 2 * y + c
    peers = [(x, y, 1 - c), (1 - x, y, c), (x, 1 - y, c), (1 - x, 1 - y, c),
             (1 - x, y, 1 - c), (x, 1 - y, 1 - c), (1 - x, 1 - y, 1 - c)]
    return me, peers


class _Comm:
    def __init__(self, chunked=(), replicated=()):
        self.arrays = list(chunked) + list(replicated)
        self.n_c = len(chunked)
        self.n = len(self.arrays)
        self.out_shape = [jax.ShapeDtypeStruct(a.shape, a.dtype) for a in chunked]
        self.out_shape += [jax.ShapeDtypeStruct((N_DEV,) + a.shape, a.dtype) for a in replicated]
        self.in_specs = [pl.BlockSpec(memory_space=pl.ANY)] * self.n
        self.out_specs = [pl.BlockSpec(memory_space=pl.ANY)] * self.n
        self.scratch = [pltpu.SemaphoreType.DMA((self.n,))] * 5 if self.n else []

    @staticmethod
    def _where():
        x, y, c = lax.axis_index("x"), lax.axis_index("y"), lax.axis_index("c")
        chips = [(1 - x, y), (x, 1 - y), (1 - x, 1 - y)]
        return (x, y, c), 4 * x + 2 * y + c, chips

    def _local(self, ins, outs, sems, a, me):
        src = ins[a].at[me] if a < self.n_c else ins[a]
        return pltpu.make_async_copy(src, outs[a].at[me], sems[2].at[a])

    @staticmethod
    def _remote(src, dst, send, recv, to):
        return pltpu.make_async_remote_copy(src_ref=src, dst_ref=dst, send_sem=send, recv_sem=recv, device_id=to,
                                            device_id_type=MESH)

    def start(self, ins, outs, sems):
        (x, y, c), me, chips = self._where()
        for a in range(self.n):
            self._local(ins, outs, sems, a, me).start()
            if a < self.n_c:
                for (px, py, pc) in _peers()[1]:
                    self._remote(ins[a].at[4 * px + 2 * py + pc], outs[a].at[me], sems[0].at[a], sems[1].at[a], (px, py, pc)).start()
            else:
                self._remote(ins[a], outs[a].at[me], sems[3].at[a], sems[4].at[a], (x, y, 1 - c)).start()
                for (px, py) in chips:
                    self._remote(ins[a], outs[a].at[me], sems[0].at[a], sems[1].at[a], (px, py, c)).start()

    def wait(self, ins, outs, sems):
        (x, y, c), me, chips = self._where()
        sibling = (x, y, 1 - c)

        def drain(a, pair, count):
            blocks = outs[a].at[pl.ds(0, count)]
            cp = self._remote(blocks, blocks, sems[pair].at[a], sems[pair + 1].at[a], sibling)
            cp.wait_send()
            cp.wait_recv()

        for a in range(self.n):
            if a < self.n_c:
                drain(a, 0, N_DEV - 1)
            else:
                drain(a, 0, len(chips))
                for (px, py) in chips:
                    blk = outs[a].at[4 * px + 2 * py + c]
                    self._remote(blk, blk, sems[3].at[a], sems[4].at[a], sibling).start()
        for a in range(self.n):
            if a >= self.n_c:
                drain(a, 3, len(chips) + 1)
            self._local(ins, outs, sems, a, me).wait()


_NO_COMM = _Comm()


def _pcall(body, name, grid, operands, in_specs, out_shape, out_specs, scratch=(), aliases=None, comm=_NO_COMM):
    n_in, n_out, n_scr = len(operands), len(out_shape), len(scratch)
    grid = tuple(grid)

    def carried(*refs):
        ins, c_in = refs[:n_in], refs[n_in:n_in + comm.n]
        o0 = n_in + comm.n
        outs, c_out = refs[o0:o0 + n_out], refs[o0 + n_out:o0 + n_out + comm.n]
        s0 = o0 + n_out + comm.n
        scr, sems = refs[s0:s0 + n_scr], refs[s0 + n_scr:]
        ids = [pl.program_id(ax) for ax in range(len(grid))]

        @pl.when(functools.reduce(jnp.logical_and, [p == 0 for p in ids]))
        def _():
            comm.start(c_in, c_out, sems)

        body(*ins, *outs, *scr)

        @pl.when(functools.reduce(jnp.logical_and, [p == g - 1 for p, g in zip(ids, grid)]))
        def _():
            comm.wait(c_in, c_out, sems)

    return pl.pallas_call(
        carried if comm.n else body, name=name, grid=grid, in_specs=list(in_specs) + comm.in_specs,
        out_shape=list(out_shape) + comm.out_shape, out_specs=list(out_specs) + comm.out_specs,
        scratch_shapes=list(scratch) + comm.scratch, input_output_aliases=aliases or {},
        compiler_params=_params(dimension_semantics=("arbitrary",) * len(grid)),
    )(*operands, *comm.arrays)


def _mm(name, a, b, kind, grid, a_blk, b_blk, outs, extras=(), epi=None, acc_outs=(), j_outer=False, comm=_NO_COMM,
        split=None, b_resident=False, out_chunks=0):
    gi, gj, gk = grid
    n_ex = len(extras)
    n_out = len(outs)
    mode, n_chunks = split if split is not None else (None, 1)

    def spec(blk, fn, **kw):
        return pl.BlockSpec(blk, (lambda j, i, k: fn(i, j, k)) if j_outer else fn, **kw)

    def b_chunk(b_ref, c):
        if len(b_ref.shape) == 3:
            return b_ref[c]
        rows, cols = b_ref.shape
        if (kind == "nn") == (mode == "cols"):
            return b_ref[:, c * (cols // n_chunks):(c + 1) * (cols // n_chunks)]
        return b_ref[c * (rows // n_chunks):(c + 1) * (rows // n_chunks), :]

    def col_chunk(ref, c):
        width = ref.shape[-1] // n_chunks
        return slice(c * width, (c + 1) * width)

    def body(*refs):
        a_ref, b_ref = refs[0], refs[1]
        ex = refs[2:2 + n_ex]
        out_refs = refs[2 + n_ex:2 + n_ex + n_out]
        acc_ref = refs[2 + n_ex + n_out] if gk > 1 else None
        i = pl.program_id(1 if j_outer else 0)
        k = pl.program_id(2)
        if mode == "cols":
            a_val = a_ref[...]
            for c in range(n_chunks):
                acc = _dot(a_val, b_chunk(b_ref, c), kind)
                vals = epi(acc, *[e[:, col_chunk(e, c)] for e in ex]) if epi is not None else (acc,)
                for o, v in zip(out_refs, vals):
                    o[:, col_chunk(o, c)] = v.astype(o.dtype)
            return
        if mode == "sum":
            part = _dot(a_ref[:, col_chunk(a_ref, 0)], b_chunk(b_ref, 0), kind)
            for c in range(1, n_chunks):
                part = part + _dot(a_ref[:, col_chunk(a_ref, c)], b_chunk(b_ref, c), kind)
        else:
            part = _dot(a_ref[...], b_ref[...], kind)

        def finish(acc):
            if out_chunks:
                width = acc.shape[-1] // out_chunks
                for c in range(out_chunks):
                    out_refs[0][c] = acc[:, c * width:(c + 1) * width].astype(out_refs[0].dtype)
                return
            vals = epi(acc, *[e[...] for e in ex]) if epi is not None else (acc,)
            for idx, (o, v) in enumerate(zip(out_refs, vals)):
                if idx in acc_outs:
                    @pl.when(i == 0)
                    def _():
                        o[...] = v.astype(o.dtype)

                    @pl.when(i != 0)
                    def _():
                        o[...] += v.astype(o.dtype)
                else:
                    o[...] = v.astype(o.dtype)

        if gk == 1:
            finish(part)
        else:
            @pl.when(k == 0)
            def _():
                acc_ref[...] = part

            @pl.when(k != 0)
            def _():
                acc_ref[...] += part

            @pl.when(k == gk - 1)
            def _():
                finish(acc_ref[...])

    scratch = []
    if gk > 1:
        tm = a_blk[0][-1] if kind == "tn" else a_blk[0][-2]
        tn = b_blk[0][-2] if kind == "nt" else b_blk[0][-1]
        scratch = [pltpu.VMEM((tm, tn), F32)]
    b_kw = dict(pipeline_mode=pl.Buffered(1)) if b_resident else {}
    return _pcall(body, name, (gj, gi, gk) if j_outer else (gi, gj, gk), [a, b] + [e for e, _, _ in extras],
                  [spec(*a_blk), spec(*b_blk, **b_kw)] + [spec(blk, fn) for _, blk, fn in extras],
                  [s for s, _, _ in outs], [spec(blk, fn) for _, blk, fn in outs], scratch, comm=comm)


def _rms_fwd_vals(x, g):
    r = lax.rsqrt(jnp.mean(x * x, axis=-1, keepdims=True) + EPS)
    return x * r * g


def _rms_bwd_vals(x, g, du):
    r = lax.rsqrt(jnp.mean(x * x, axis=-1, keepdims=True) + EPS)
    xh = x * r
    dxh = du * g
    dx = r * (dxh - xh * jnp.mean(dxh * xh, axis=-1, keepdims=True))
    return dx, jnp.sum(du * xh, axis=0, keepdims=True)


def _rms_fwd(name, x, g, rows, comm=_NO_COMM):
    n = x.shape[0]

    def body(x_ref, g_ref, o_ref):
        o_ref[...] = _rms_fwd_vals(x_ref[...], g_ref[...]).astype(BF16)

    return _pcall(body, name, (n // rows,), [x, g],
                  [pl.BlockSpec((rows, D_MODEL), lambda i: (i, 0)), pl.BlockSpec((1, D_MODEL), lambda i: (0, 0))],
                  [jax.ShapeDtypeStruct(x.shape, BF16)], [pl.BlockSpec((rows, D_MODEL), lambda i: (i, 0))], comm=comm)


def _rope_tables(seq):
    half = ROT_DIM // 2
    pos = np.arange(seq, dtype=np.float32)
    inv_freq = np.float32(ROPE_THETA) ** (-np.arange(0, ROT_DIM, 2, dtype=np.float32) / np.float32(ROT_DIM))
    ang = (pos[:, None] * inv_freq[None, :]).astype(np.float32)
    cos, sin = np.cos(ang), np.sin(ang)
    rest = HEAD_DIM - ROT_DIM
    c = np.concatenate([cos, cos, np.ones((seq, rest), np.float32)], axis=1)
    s1 = np.concatenate([np.zeros((seq, half), np.float32), sin, np.zeros((seq, rest), np.float32)], axis=1)
    s2 = np.concatenate([-sin, np.zeros((seq, half + rest), np.float32)], axis=1)
    return jnp.asarray(c), jnp.asarray(s1), jnp.asarray(s2)


def _group_shapes(seq, width, dtype):
    return [jax.ShapeDtypeStruct((d, seq // d, width), dtype) for d in DILATIONS]


def _group_specs(width):
    return [pl.BlockSpec((d, ROW_BLK // d, width), lambda i: (0, i, 0)) for d in DILATIONS]


def _qkv_prep(z, tabs, comm=_NO_COMM):
    seq = z.shape[0]

    def body(z_ref, c_ref, s1_ref, s2_ref, a0, a1, a2, sc):
        outs = (a0, a1, a2)
        c, s1, s2 = c_ref[...], s1_ref[...], s2_ref[...]
        for part in range(3):
            for hh in range(N_GROUPS * HEADS_PER_GROUP):
                g, hl = divmod(hh, HEADS_PER_GROUP)
                col = part * ATTN_W + hh * HEAD_DIM
                ocol = part * GROUP_W + hl * HEAD_DIM
                x = z_ref[:, col:col + HEAD_DIM].astype(F32)
                if part < 2:
                    x = x * c + pltpu.roll(x, ROT_DIM // 2, 1) * s1 + pltpu.roll(x, HEAD_DIM - ROT_DIM // 2, 1) * s2
                d = DILATIONS[g]
                if d == 1:
                    outs[g][0, :, ocol:ocol + HEAD_DIM] = x.astype(BF16)
                else:
                    sc[...] = x
                    for r in range(d):
                        outs[g][r, :, ocol:ocol + HEAD_DIM] = sc[pl.ds(r, ROW_BLK // d, stride=d), :].astype(BF16)

    tab_spec = pl.BlockSpec((ROW_BLK, HEAD_DIM), lambda i: (i, 0))
    return _pcall(body, "qkv_prep", (seq // ROW_BLK,), [z, *tabs],
                  [pl.BlockSpec((ROW_BLK, QKV_W), lambda i: (i, 0)), tab_spec, tab_spec, tab_spec],
                  _group_shapes(seq, ATTN_W, BF16), _group_specs(ATTN_W), [pltpu.VMEM((ROW_BLK, HEAD_DIM), F32)], comm=comm)


def _band_masks_2(t):
    qi = lax.broadcasted_iota(jnp.int32, (BAND, 2 * BAND), 0)
    kj = lax.broadcasted_iota(jnp.int32, (BAND, 2 * BAND), 1)
    band = jnp.logical_and(kj >= qi, kj <= qi + BAND)
    return band, jnp.logical_and(band, jnp.logical_or(kj >= BAND, t > 0))


def _attn_fwd(name, a_g, comm=_NO_COMM):
    dil, m_len, _ = a_g.shape
    qb = min(QB, m_len // BAND)
    rows = qb * BAND
    steps = m_len // rows
    scale = HEAD_DIM ** -0.5

    tiles = [(sb, h) for sb in range(qb) for h in range(HEADS_PER_GROUP)]

    def body(q_ref, kc_ref, vc_ref, kp_ref, vp_ref, o_ref, l_ref, k_all, v_all, s_scr, p_scr, r_scr):
        t = pl.program_id(1)
        k_all[0:BAND, :] = kp_ref[...]
        k_all[BAND:, :] = kc_ref[...]
        v_all[0:BAND, :] = vp_ref[...]
        v_all[BAND:, :] = vc_ref[...]
        band, band_first = _band_masks_2(t)
        for idx, (sb, h) in enumerate(tiles):
            cs = slice(h * HEAD_DIM, (h + 1) * HEAD_DIM)
            s = _dot(q_ref[sb * BAND:(sb + 1) * BAND, cs], k_all[sb * BAND:(sb + 2) * BAND, cs], "nt") * scale
            s_scr[idx] = jnp.where(band_first if sb == 0 else band, s, NEG)
        for idx, (sb, h) in enumerate(tiles):
            cs = slice(h * HEAD_DIM, (h + 1) * HEAD_DIM)
            s = s_scr[idx]
            mx = jnp.max(s, axis=-1, keepdims=True)
            p = jnp.exp(s - mx)
            den = jnp.sum(p, axis=-1, keepdims=True)
            p_scr[idx] = p.astype(BF16)
            r_scr[idx] = jnp.broadcast_to(1.0 / den, (BAND, HEAD_DIM))
            l_ref[sb * BAND:(sb + 1) * BAND, cs] = jnp.broadcast_to(mx + jnp.log(den), (BAND, HEAD_DIM))
        for idx, (sb, h) in enumerate(tiles):
            cs = slice(h * HEAD_DIM, (h + 1) * HEAD_DIM)
            o_ref[sb * BAND:(sb + 1) * BAND, cs] = _dot(p_scr[idx], v_all[sb * BAND:(sb + 2) * BAND, cs], "nn") * r_scr[idx]

    def prev(r, t):
        return jnp.maximum(qb * t - 1, 0)

    cur = lambda c: pl.BlockSpec((None, rows, GROUP_W), lambda r, t, c=c: (r, t, c))
    prv = lambda c: pl.BlockSpec((None, BAND, GROUP_W), lambda r, t, c=c: (r, prev(r, t), c))
    out_spec = pl.BlockSpec((None, rows, GROUP_W), lambda r, t: (r, t, 0))
    shp = jax.ShapeDtypeStruct((dil, m_len, GROUP_W), F32)
    n_t = len(tiles)
    return _pcall(body, name, (dil, steps), [a_g] * 5, [cur(0), cur(1), cur(2), prv(1), prv(2)], [shp, shp],
                  [out_spec, out_spec],
                  [pltpu.VMEM((rows + BAND, GROUP_W), BF16), pltpu.VMEM((rows + BAND, GROUP_W), BF16),
                   pltpu.VMEM((n_t, BAND, 2 * BAND), F32), pltpu.VMEM((n_t, BAND, 2 * BAND), BF16),
                   pltpu.VMEM((n_t, BAND, HEAD_DIM), F32)], comm=comm)


def _attn_merge(os_, ls_, seq):
    def body(o0, l0, o1, l1, o2, l2, at_ref, lt_ref, sc):
        for h in range(HEADS_PER_GROUP):
            cs = slice(h * HEAD_DIM, (h + 1) * HEAD_DIM)
            for gi, (o_r, l_r) in enumerate(((o1, l1), (o2, l2))):
                d = DILATIONS[gi + 1]
                for r in range(d):
                    sc.at[2 * gi][pl.ds(r, ROW_BLK // d, stride=d), :] = o_r[r, :, cs]
                    sc.at[2 * gi + 1][pl.ds(r, ROW_BLK // d, stride=d), :] = l_r[r, :, cs]
            o0v, l0v = o0[0, :, cs], l0[0, :, cs]
            o1v, l1v, o2v, l2v = sc[0], sc[1], sc[2], sc[3]
            mx = jnp.maximum(jnp.maximum(l0v, l1v), l2v)
            e0, e1, e2 = jnp.exp(l0v - mx), jnp.exp(l1v - mx), jnp.exp(l2v - mx)
            tot = e0 + e1 + e2
            at_ref[:, cs] = ((e0 * o0v + e1 * o1v + e2 * o2v) / tot).astype(BF16)
            lt_ref[:, cs] = mx + jnp.log(tot)

    gs = _group_specs(GROUP_W)
    row = pl.BlockSpec((ROW_BLK, GROUP_W), lambda i: (i, 0))
    return pl.pallas_call(
        body, name="attn_merge",
        out_shape=[jax.ShapeDtypeStruct((seq, GROUP_W), BF16), jax.ShapeDtypeStruct((seq, GROUP_W), F32)],
        grid=(seq // ROW_BLK,), in_specs=[gs[0], gs[0], gs[1], gs[1], gs[2], gs[2]], out_specs=[row, row],
        scratch_shapes=[pltpu.VMEM((4, ROW_BLK, HEAD_DIM), F32)],
        compiler_params=_params(dimension_semantics=("arbitrary",)),
    )(os_[0], ls_[0], os_[1], ls_[1], os_[2], ls_[2])


def _attn_bwd_prep(dattn, attn, lt):
    seq = dattn.shape[0]

    def body(da_ref, at_ref, lt_ref, d0, c0, t0, d1, c1, t1, d2, c2, t2, sc):
        outs = ((d0, c0, t0), (d1, c1, t1), (d2, c2, t2))
        for h in range(HEADS_PER_GROUP):
            cs = slice(h * HEAD_DIM, (h + 1) * HEAD_DIM)
            da = da_ref[:, cs].astype(F32)
            cc = jnp.broadcast_to(jnp.sum(da * at_ref[:, cs].astype(F32), axis=-1, keepdims=True), (ROW_BLK, HEAD_DIM))
            ltv = lt_ref[:, cs]
            d0[0, :, cs] = da_ref[:, cs]
            c0[0, :, cs] = cc
            t0[0, :, cs] = ltv
            sc[0], sc[1], sc[2] = da, cc, ltv
            for g in (1, 2):
                d = DILATIONS[g]
                for r in range(d):
                    rows = pl.ds(r, ROW_BLK // d, stride=d)
                    outs[g][0][r, :, cs] = sc.at[0][rows, :].astype(BF16)
                    outs[g][1][r, :, cs] = sc.at[1][rows, :]
                    outs[g][2][r, :, cs] = sc.at[2][rows, :]

    gs = _group_specs(GROUP_W)
    row = pl.BlockSpec((ROW_BLK, GROUP_W), lambda i: (i, 0))
    shapes, specs = [], []
    for g, d in enumerate(DILATIONS):
        for dt in (BF16, F32, F32):
            shapes.append(jax.ShapeDtypeStruct((d, seq // d, GROUP_W), dt))
            specs.append(gs[g])
    res = pl.pallas_call(
        body, name="attn_bwd_prep", out_shape=shapes, grid=(seq // ROW_BLK,), in_specs=[row, row, row],
        out_specs=specs, scratch_shapes=[pltpu.VMEM((3, ROW_BLK, HEAD_DIM), F32)],
        compiler_params=_params(dimension_semantics=("arbitrary",)),
    )(dattn, attn, lt)
    return [res[3 * g:3 * g + 3] for g in range(N_GROUPS)]


def _attn_bwd(name, a_g, da_g, c_g, lt_g, comm=_NO_COMM):
    dil, m_len, _ = a_g.shape
    qb = min(QB, m_len // BAND)
    rows = qb * BAND
    steps = m_len // rows
    scale = HEAD_DIM ** -0.5

    tiles = [(sb, h) for sb in range(qb) for h in range(HEADS_PER_GROUP)]

    def body(q_ref, kc_ref, vc_ref, kp_ref, vp_ref, da_ref, c_ref, lt_ref, d_ref, dk_acc, dv_acc, car_k, car_v,
             k_all, v_all, s_scr, dp_scr, p_scr, ds_scr):
        tg = pl.program_id(1)
        t = steps - 1 - tg

        @pl.when(tg == 0)
        def _():
            car_k[...] = jnp.zeros_like(car_k)
            car_v[...] = jnp.zeros_like(car_v)

        k_all[0:BAND, :] = kp_ref[...]
        k_all[BAND:, :] = kc_ref[...]
        v_all[0:BAND, :] = vp_ref[...]
        v_all[BAND:, :] = vc_ref[...]
        zero = jnp.zeros((rows, GROUP_W), F32)
        dk_acc[0:rows, :] = zero
        dv_acc[0:rows, :] = zero
        dk_acc[rows:rows + BAND, :] = car_k[...]
        dv_acc[rows:rows + BAND, :] = car_v[...]
        band, band_first = _band_masks_2(t)
        for idx, (sb, h) in enumerate(tiles):
            cs = slice(h * HEAD_DIM, (h + 1) * HEAD_DIM)
            rs, ks = slice(sb * BAND, (sb + 1) * BAND), slice(sb * BAND, (sb + 2) * BAND)
            s_scr[idx] = _dot(q_ref[rs, cs], k_all[ks, cs], "nt")
            dp_scr[idx] = _dot(da_ref[rs, cs], v_all[ks, cs], "nt")
        for idx, (sb, h) in enumerate(tiles):
            cs = slice(h * HEAD_DIM, (h + 1) * HEAD_DIM)
            rs = slice(sb * BAND, (sb + 1) * BAND)
            ltv = jnp.concatenate([lt_ref[rs, cs]] * 2, axis=1)
            cc = jnp.concatenate([c_ref[rs, cs]] * 2, axis=1)
            p = jnp.exp(jnp.where(band_first if sb == 0 else band, s_scr[idx] * scale - ltv, NEG))
            p_scr[idx] = p.astype(BF16)
            ds_scr[idx] = (p * (dp_scr[idx] - cc) * scale).astype(BF16)
        for idx, (sb, h) in enumerate(tiles):
            cs = slice(h * HEAD_DIM, (h + 1) * HEAD_DIM)
            rs, ks = slice(sb * BAND, (sb + 1) * BAND), slice(sb * BAND, (sb + 2) * BAND)
            d_ref[rs, cs] = _dot(ds_scr[idx], k_all[ks, cs], "nn").astype(BF16)
            dk_acc[ks, cs] += _dot(ds_scr[idx], q_ref[rs, cs], "tn")
            dv_acc[ks, cs] += _dot(p_scr[idx], da_ref[rs, cs], "tn")
        d_ref[:, GROUP_W:2 * GROUP_W] = dk_acc[BAND:rows + BAND, :].astype(BF16)
        d_ref[:, 2 * GROUP_W:3 * GROUP_W] = dv_acc[BAND:rows + BAND, :].astype(BF16)
        car_k[...] = dk_acc[0:BAND, :]
        car_v[...] = dv_acc[0:BAND, :]

    def rev(tg):
        return steps - 1 - tg

    def prev(tg):
        return jnp.maximum(qb * rev(tg) - 1, 0)

    cur = lambda c: pl.BlockSpec((None, rows, GROUP_W), lambda r, tg, c=c: (r, rev(tg), c))
    prv = lambda c: pl.BlockSpec((None, BAND, GROUP_W), lambda r, tg, c=c: (r, prev(tg), c))
    return _pcall(
        body, name, (dil, steps), [a_g, a_g, a_g, a_g, a_g, da_g, c_g, lt_g],
        [cur(0), cur(1), cur(2), prv(1), prv(2), cur(0), cur(0), cur(0)],
        [jax.ShapeDtypeStruct((dil, m_len, ATTN_W), BF16)], [pl.BlockSpec((None, rows, ATTN_W), lambda r, tg: (r, rev(tg), 0))],
        [pltpu.VMEM((rows + BAND, GROUP_W), F32), pltpu.VMEM((rows + BAND, GROUP_W), F32),
         pltpu.VMEM((BAND, GROUP_W), F32), pltpu.VMEM((BAND, GROUP_W), F32),
         pltpu.VMEM((rows + BAND, GROUP_W), BF16), pltpu.VMEM((rows + BAND, GROUP_W), BF16),
         pltpu.VMEM((len(tiles), BAND, 2 * BAND), F32), pltpu.VMEM((len(tiles), BAND, 2 * BAND), F32),
         pltpu.VMEM((len(tiles), BAND, 2 * BAND), BF16), pltpu.VMEM((len(tiles), BAND, 2 * BAND), BF16)], comm=comm)


def _dqkv_post(d_gs, tabs, dz):
    seq = dz.shape[0]

    def body(g0, g1, g2, c_ref, s1_ref, s2_ref, dz_any, o_ref, sc):
        del dz_any
        ins = (g0, g1, g2)
        c, s1, s2 = c_ref[...], s1_ref[...], s2_ref[...]
        for part in range(3):
            for hh in range(N_GROUPS * HEADS_PER_GROUP):
                g, hl = divmod(hh, HEADS_PER_GROUP)
                icol = part * GROUP_W + hl * HEAD_DIM
                ocol = part * ATTN_W + hh * HEAD_DIM
                d = DILATIONS[g]
                if d == 1:
                    x = ins[g][0, :, icol:icol + HEAD_DIM].astype(F32)
                else:
                    for r in range(d):
                        sc[pl.ds(r, ROW_BLK // d, stride=d), :] = ins[g][r, :, icol:icol + HEAD_DIM].astype(F32)
                    x = sc[...]
                if part < 2:
                    x = x * c + pltpu.roll(x * s1, HEAD_DIM - ROT_DIM // 2, 1) + pltpu.roll(x * s2, ROT_DIM // 2, 1)
                o_ref[:, ocol:ocol + HEAD_DIM] = x.astype(BF16)

    tab_spec = pl.BlockSpec((ROW_BLK, HEAD_DIM), lambda i: (i, 0))
    return pl.pallas_call(
        body, name="dqkv_post", out_shape=jax.ShapeDtypeStruct(dz.shape, BF16), grid=(seq // ROW_BLK,),
        in_specs=_group_specs(ATTN_W) + [tab_spec, tab_spec, tab_spec, pl.BlockSpec(memory_space=pl.ANY)],
        out_specs=pl.BlockSpec((ROW_BLK, QKV_W), lambda i: (i, 0)),
        scratch_shapes=[pltpu.VMEM((ROW_BLK, HEAD_DIM), F32)], input_output_aliases={6: 0},
        compiler_params=_params(dimension_semantics=("arbitrary",)),
    )(*d_gs, *tabs, dz)


def _glu(zg):
    a = zg[:, :CONV_CH].astype(F32)
    s = _sigmoid(zg[:, CONV_CH:].astype(F32))
    return a, s, a * s


def _shifted_copies(xs):
    n = xs.shape[1] - SUBLANES
    for b in range(1, SUBLANES):
        xs[b, 0:n, :] = xs[0, pl.ds(b, n), :]


def _shifted(xs, offset, r0, cs):
    a, b = divmod(offset, SUBLANES)
    return xs[b, pl.ds(SUBLANES * a + r0, CONV_ROWS), cs]


def _conv_fwd(z, cw, cb, lg, lb, comm=_NO_COMM):
    seq = z.shape[0]
    halo_per_blk = ROW_BLK // CONV_HALO

    def body(zg_ref, zh_ref, cw_ref, cb_ref, lg_ref, lb_ref, c2_ref, c4_ref, xs):
        i = pl.program_id(0)
        _, _, c1 = _glu(zg_ref[...])
        _, _, c1h = _glu(zh_ref[...])
        xs[0, 0:CONV_HALO, :] = jnp.where(i > 0, c1h, 0.0)
        xs[0, CONV_HALO:, :] = c1
        _shifted_copies(xs)
        for s in range(CONV_CH // HEAD_DIM):
            cs = slice(s * HEAD_DIM, (s + 1) * HEAD_DIM)
            taps = [cw_ref[j:j + 1, cs] for j in range(CONV_K)]
            bias = cb_ref[:, cs]

            def chunk(rc, carry, cs=cs, taps=taps, bias=bias):
                r0 = pl.multiple_of(rc * CONV_ROWS, CONV_ROWS)
                acc = jnp.zeros((CONV_ROWS, HEAD_DIM), F32)
                for j in range(CONV_K):
                    acc = acc + taps[j] * _shifted(xs, CONV_HALO - (CONV_K - 1) + j, r0, cs)
                c2_ref[pl.ds(r0, CONV_ROWS), cs] = acc + bias
                return carry

            lax.fori_loop(0, ROW_BLK // CONV_ROWS, chunk, 0)
        c2 = c2_ref[...]
        mu = jnp.mean(c2, axis=-1, keepdims=True)
        xc = c2 - mu
        rstd = lax.rsqrt(jnp.mean(xc * xc, axis=-1, keepdims=True) + EPS)
        c3 = xc * rstd * lg_ref[...] + lb_ref[...]
        c4_ref[...] = (c3 * _sigmoid(c3)).astype(BF16)

    vec = pl.BlockSpec((1, CONV_CH), lambda i: (0, 0))
    return _pcall(
        body, "conv_fwd", (seq // ROW_BLK,), [z, z, cw, cb, lg, lb],
        [pl.BlockSpec((ROW_BLK, 2 * CONV_CH), lambda i: (i, GLU_COL_BLK)),
         pl.BlockSpec((CONV_HALO, 2 * CONV_CH), lambda i: (jnp.maximum(i * halo_per_blk - 1, 0), GLU_COL_BLK)),
         pl.BlockSpec((CONV_HALO, CONV_CH), lambda i: (0, 0)), vec, vec, vec],
        [jax.ShapeDtypeStruct((seq, CONV_CH), F32), jax.ShapeDtypeStruct((seq, CONV_CH), BF16)],
        [pl.BlockSpec((ROW_BLK, CONV_CH), lambda i: (i, 0)), pl.BlockSpec((ROW_BLK, CONV_CH), lambda i: (i, 0))],
        [pltpu.VMEM((SUBLANES, ROW_BLK + CONV_HALO, CONV_CH), F32)], comm=comm)


def _conv_bwd(dc2, z, cw, dz, comm=_NO_COMM):
    seq = z.shape[0]
    halo_per_blk = ROW_BLK // CONV_HALO
    n_blk = seq // ROW_BLK
    last_halo = seq // CONV_HALO - 1

    def body(dc_ref, dn_ref, zg_ref, zh_ref, cw_ref, dz_any, o_ref, dcw_ref, xs, ys, dc1_ref, dcw_acc):
        del dz_any
        i = pl.program_id(0)
        a, s, c1 = _glu(zg_ref[...])
        _, _, c1h = _glu(zh_ref[...])
        xs[0, 0:CONV_HALO, :] = jnp.where(i > 0, c1h, 0.0)
        xs[0, CONV_HALO:, :] = c1
        ys[0, 0:ROW_BLK, :] = dc_ref[...]
        ys[0, ROW_BLK:, :] = jnp.where(i < n_blk - 1, dn_ref[...], 0.0)
        _shifted_copies(xs)
        _shifted_copies(ys)

        @pl.when(i == 0)
        def _():
            dcw_acc[...] = jnp.zeros_like(dcw_acc)

        for sl in range(CONV_CH // HEAD_DIM):
            cs = slice(sl * HEAD_DIM, (sl + 1) * HEAD_DIM)
            taps = [cw_ref[j:j + 1, cs] for j in range(CONV_K)]

            def chunk(rc, carry, cs=cs, taps=taps):
                r0 = pl.multiple_of(rc * CONV_ROWS, CONV_ROWS)
                dc = ys[0, pl.ds(r0, CONV_ROWS), cs]
                acc = jnp.zeros((CONV_ROWS, HEAD_DIM), F32)
                for j in range(CONV_K):
                    prod = dc * _shifted(xs, CONV_HALO - (CONV_K - 1) + j, r0, cs)
                    dcw_acc[j, :, cs] += jnp.sum(prod.reshape(CONV_ROWS // SUBLANES, SUBLANES, HEAD_DIM), axis=0)
                    acc = acc + taps[j] * _shifted(ys, CONV_K - 1 - j, r0, cs)
                dc1_ref[pl.ds(r0, CONV_ROWS), cs] = acc
                return carry

            lax.fori_loop(0, ROW_BLK // CONV_ROWS, chunk, 0)
        dc1 = dc1_ref[...]
        o_ref[:, :CONV_CH] = (dc1 * s).astype(BF16)
        o_ref[:, CONV_CH:] = (dc1 * a * s * (1.0 - s)).astype(BF16)

        @pl.when(i == n_blk - 1)
        def _():
            dcw_ref[...] = jnp.sum(dcw_acc[...], axis=1)

    return _pcall(
        body, "conv_bwd", (n_blk,), [dc2, dc2, z, z, cw, dz],
        [pl.BlockSpec((ROW_BLK, CONV_CH), lambda i: (i, 0)),
         pl.BlockSpec((CONV_HALO, CONV_CH), lambda i: (jnp.minimum((i + 1) * halo_per_blk, last_halo), 0)),
         pl.BlockSpec((ROW_BLK, 2 * CONV_CH), lambda i: (i, GLU_COL_BLK)),
         pl.BlockSpec((CONV_HALO, 2 * CONV_CH), lambda i: (jnp.maximum(i * halo_per_blk - 1, 0), GLU_COL_BLK)),
         pl.BlockSpec((CONV_HALO, CONV_CH), lambda i: (0, 0)),
         pl.BlockSpec(memory_space=pl.ANY)],
        [jax.ShapeDtypeStruct(dz.shape, BF16), jax.ShapeDtypeStruct((CONV_HALO, CONV_CH), F32)],
        [pl.BlockSpec((ROW_BLK, 2 * CONV_CH), lambda i: (i, GLU_COL_BLK)), pl.BlockSpec((CONV_HALO, CONV_CH), lambda i: (0, 0))],
        [pltpu.VMEM((SUBLANES, ROW_BLK + CONV_HALO, CONV_CH), F32), pltpu.VMEM((SUBLANES, ROW_BLK + CONV_HALO, CONV_CH), F32),
         pltpu.VMEM((ROW_BLK, CONV_CH), F32), pltpu.VMEM((CONV_HALO, SUBLANES, CONV_CH), F32)],
        aliases={5: 0}, comm=comm)


def _epi_mix(ya, c4, wcp, gates, bg):
    yc = _dot(c4, wcp, "nn")
    gv = _sigmoid(gates.astype(F32) + bg)
    merged = gv[:, :D_MODEL] * ya + gv[:, D_MODEL:] * yc
    return merged, ya, yc


def _epi_residual_rms(acc, xres, g):
    x = xres + acc
    return x, _rms_fwd_vals(x, g)


def _cross_scores(cq, ck):
    out = []
    for h in range(CROSS_HEADS):
        cs = slice(h * CROSS_HD, (h + 1) * CROSS_HD)
        s = _dot(cq[:, cs], ck[:, cs], "nt") * (CROSS_HD ** -0.5)
        e = jnp.exp(s - jnp.max(s, axis=-1, keepdims=True))
        out.append((cs, e, jnp.sum(e, axis=-1, keepdims=True)))
    return out


def _epi_cross_fwd(acc, ck, cv):
    cq = acc.astype(BF16)
    co = [_dot(e, cv[:, cs], "nn") / den for cs, e, den in _cross_scores(cq, ck)]
    return cq, jnp.concatenate(co, axis=1)


def _epi_cross_bwd(dco, cq, ck, cv):
    dco = dco.astype(BF16)
    dcq, dck, dcv = [], [], []
    for cs, e, den in _cross_scores(cq, ck):
        p = e / den
        dp = _dot(dco[:, cs], cv[:, cs], "nt")
        ds = (p * (dp - jnp.sum(dp * p, axis=-1, keepdims=True)) * (CROSS_HD ** -0.5)).astype(BF16)
        dcq.append(_dot(ds, ck[:, cs], "nn"))
        dck.append(_dot(ds, cq[:, cs], "tn"))
        dcv.append(_dot(p, dco[:, cs], "tn"))
    return jnp.concatenate(dcq, axis=1), jnp.concatenate(dck, axis=1), jnp.concatenate(dcv, axis=1)


def _epi_mlp_up(acc):
    return (jnp.square(jnp.maximum(acc, 0.0)),)


def _epi_final(acc, x2, tgt, g):
    x3 = x2 + acc
    err = _rms_fwd_vals(x3, g) - tgt
    loss = (0.5 / D_MODEL) * jnp.sum(err * err)
    dx3, dg = _rms_bwd_vals(x3, g, err * (1.0 / D_MODEL))
    return dx3, jnp.full((1, HEAD_DIM), loss, F32), dg


def _epi_mlp_down_bwd(dh, h):
    return (dh * 2.0 * jnp.sqrt(h.astype(F32)),)


def _epi_rms_bwd(du, x, g, dres):
    dx, dg = _rms_bwd_vals(x, g, du)
    return dres.astype(F32) + dx, dg


def _epi_rms_bwd_g(du, x, g):
    return (_rms_bwd_vals(x, g, du)[1],)


def _epi_mix_bwd(dm, ya, yc, gates, bg):
    gv = _sigmoid(gates.astype(F32) + bg)
    ga, gb = gv[:, :D_MODEL], gv[:, D_MODEL:]
    ya, yc = ya.astype(F32), yc.astype(F32)
    dgate = jnp.concatenate([dm * ya * ga * (1.0 - ga), dm * yc * gb * (1.0 - gb)], axis=1)
    return dm * ga, dm * gb, dgate, jnp.sum(dgate, axis=0, keepdims=True)


def _epi_ln_bwd(dc4, c2, lg, lb):
    mu = jnp.mean(c2, axis=-1, keepdims=True)
    xc = c2 - mu
    rstd = lax.rsqrt(jnp.mean(xc * xc, axis=-1, keepdims=True) + EPS)
    xh = xc * rstd
    c3 = xh * lg + lb
    sg = _sigmoid(c3)
    dc3 = dc4 * sg * (1.0 + c3 * (1.0 - sg))
    dxh = dc3 * lg
    dc2 = rstd * (dxh - jnp.mean(dxh, axis=-1, keepdims=True) - xh * jnp.mean(dxh * xh, axis=-1, keepdims=True))
    return (dc2, jnp.sum(dc3 * xh, axis=0, keepdims=True), jnp.sum(dc3, axis=0, keepdims=True),
            jnp.sum(dc2, axis=0, keepdims=True))


def _sds(shape, dtype):
    return jax.ShapeDtypeStruct(shape, dtype)


class _Lazy:
    def __init__(self, fn):
        self.fn = fn

    def __getitem__(self, key):
        return self.fn(key)


def _local_step(x, mem, tgt, sm, plan):
    w = _Lazy(plan.w)
    dw = {}

    def carry(name, n_own, fn, *args, **kw):
        c = plan.comm(name, dw)
        res = fn(*args, comm=c, **kw)
        plan.done(name, res[n_own:])
        return res[:n_own]

    def mm(name, *args, **kw):
        return carry(name, len(args[6]), _mm, name, *args, **kw)

    seq = x.shape[0]
    nr = seq // ROW_BLK
    big = min(1024, seq)
    nb = seq // big
    row = lambda n: ((ROW_BLK, n), lambda i, j, k: (i, 0))
    vec = lambda n: ((1, n), lambda i, j, k: (0, 0))
    full = lambda r, c: ((r, c), lambda i, j, k: (0, 0))
    gates_blk = ((ROW_BLK, 2 * D_MODEL), lambda i, j, k: (i, GATE_COL_BLK))
    tabs = _rope_tables(seq)

    u = carry("rms_mix", 1, _rms_fwd, "rms_mix", x, sm["g_mix"], ROW_BLK)[0]
    whole3 = lambda a: (a.shape, lambda i, j, k: (0, 0, 0))
    z = plan.project_in(u)
    a_gs = carry("qkv_prep", 3, _qkv_prep, z, tabs)
    os_, ls_ = [], []
    for g in range(N_GROUPS):
        name = "attn_fwd_%d" % g
        o_g, l_g = carry(name, 2, _attn_fwd, name, a_gs[g])
        os_.append(o_g)
        ls_.append(l_g)
    attn, lt = _attn_merge(os_, ls_, seq)
    c2, c4 = carry("conv_fwd", 2, _conv_fwd, z, w["taps"], sm["conv_b"], sm["conv_ln_g"], sm["conv_ln_b"])
    merged, ya, yc = mm(
        "mix", attn, w["w_attn_proj"], "nn", (nr, 1, 1), row(GROUP_W), full(GROUP_W, D_MODEL),
        [(_sds((seq, D_MODEL), BF16), *row(D_MODEL))] * 3,
        extras=[(c4, *row(CONV_CH)), (w["w_conv_proj"], *full(CONV_CH, D_MODEL)), (z, *gates_blk), (sm["b_gate"], *vec(2 * D_MODEL))],
        epi=_epi_mix)
    x1, uq = mm("out_proj", merged, w["w_out"], "nn", (nr, 1, 1), row(D_MODEL), full(D_MODEL, D_MODEL),
                 [(_sds((seq, D_MODEL), F32), *row(D_MODEL)), (_sds((seq, D_MODEL), BF16), *row(D_MODEL))],
                 extras=[(x, *row(D_MODEL)), (sm["g_cross"], *vec(D_MODEL))], epi=_epi_residual_rms)

    mn = _rms_fwd("rms_mem", mem, sm["g_mem"], N_MEM)[0]
    ckv = mm("ckv_proj", mn, w["w_ckv"], "nn", (1, N_DEV, 1), full(N_MEM, D_MODEL),
              ((None, D_MODEL, 2 * D_MODEL // N_DEV), lambda i, j, k: (j, 0, 0)),
              [(_sds((N_MEM, 2 * D_MODEL), BF16), (N_MEM, 2 * D_MODEL // N_DEV), lambda i, j, k: (0, j))])[0]
    ck, cv = ckv[:, :D_MODEL], ckv[:, D_MODEL:]
    kv_blk = full(N_MEM, D_MODEL)
    cq, co = mm("cq_proj_cross", uq, w["w_cq"], "nn", (nr, 1, 1), row(D_MODEL), full(D_MODEL, D_MODEL),
                 [(_sds((seq, D_MODEL), BF16), *row(D_MODEL))] * 2,
                 extras=[(ck, *kv_blk), (cv, *kv_blk)], epi=_epi_cross_fwd)
    x2, um = mm("co_proj", co, w["w_co"], "nn", (nr, 1, 1), row(D_MODEL), full(D_MODEL, D_MODEL),
                 [(_sds((seq, D_MODEL), F32), *row(D_MODEL)), (_sds((seq, D_MODEL), BF16), *row(D_MODEL))],
                 extras=[(x1, *row(D_MODEL)), (sm["g_mlp"], *vec(D_MODEL))], epi=_epi_residual_rms)

    ff_blk = D_FF // N_DEV
    row_f32 = (_sds((seq, D_MODEL), F32), *row(D_MODEL))
    row_bf16 = (_sds((seq, D_MODEL), BF16), *row(D_MODEL))
    col_sum = (_sds((1, D_MODEL), F32), *vec(D_MODEL))
    h = mm("mlp_up", um, w["w_up"], "nn", (nr, 1, 1), row(D_MODEL), whole3(w["w_up"]),
           [(_sds((seq, D_FF), BF16), *row(D_FF))], epi=_epi_mlp_up, split=("cols", N_DEV), b_resident=True)[0]
    kt = D_FF // D_MODEL
    dx3, loss, dg_final = mm(
        "mlp_down_loss", h, w["w_down"], "nn", (nr, 1, 1), row(D_FF), full(D_FF, D_MODEL),
        [row_bf16, (_sds((1, HEAD_DIM), F32), *vec(HEAD_DIM)), col_sum],
        extras=[(x2, *row(D_MODEL)), (tgt, *row(D_MODEL)), (sm["g_final"], *vec(D_MODEL))], epi=_epi_final, acc_outs=(1, 2),
        b_resident=True)

    dhpre = mm("mlp_down_bwd", dx3, w["w_down"], "nt", (nr, 1, 1), row(D_MODEL), full(D_FF, D_MODEL),
               [(_sds((seq, D_FF), BF16), *row(D_FF))], extras=[(h, *row(D_FF))], epi=_epi_mlp_down_bwd,
               split=("cols", kt), b_resident=True)[0]
    big2 = min(2 * big, seq)
    nb2 = seq // big2
    dw["w_down"] = mm("dw_down", h, dx3, "tn", (kt, 1, nb2), ((big2, D_MODEL), lambda i, j, k: (k, i)),
                      ((big2, D_MODEL), lambda i, j, k: (k, 0)),
                      [(_sds((D_FF, D_MODEL), BF16), (D_MODEL, D_MODEL), lambda i, j, k: (i, 0))])[0]
    dx2, dg_mlp = mm("mlp_up_bwd", dhpre, w["w_up"], "nt", (nr, 1, 1), row(D_FF), whole3(w["w_up"]),
                     [row_bf16, col_sum],
                     extras=[(x2, *row(D_MODEL)), (sm["g_mlp"], *vec(D_MODEL)), (dx3, *row(D_MODEL))],
                     epi=_epi_rms_bwd, acc_outs=(1,), split=("sum", N_DEV), b_resident=True)
    dw["w_up"] = mm("dw_up", um, dhpre, "tn", (1, N_DEV, nb2), ((big2, D_MODEL), lambda i, j, k: (k, 0)),
                    ((big2, ff_blk), lambda i, j, k: (k, j)),
                    [(_sds((N_DEV, D_MODEL, ff_blk), BF16), (None, D_MODEL, ff_blk), lambda i, j, k: (j, 0, 0))])[0]

    acc_kv = (_sds((N_MEM, D_MODEL), F32), *kv_blk)
    dcq, dck, dcv = mm("co_proj_bwd_cross", dx2, w["w_co"], "nt", (nr, 1, 1), row(D_MODEL), full(D_MODEL, D_MODEL),
                       [row_bf16, acc_kv, acc_kv],
                       extras=[(cq, *row(D_MODEL)), (ck, *kv_blk), (cv, *kv_blk)], epi=_epi_cross_bwd, acc_outs=(1, 2))

    def dw_square(name, act, grad):
        return mm(name, act, grad, "tn", (1, 1, nb2), ((big2, D_MODEL), lambda i, j, k: (k, 0)),
                  ((big2, D_MODEL), lambda i, j, k: (k, 0)), [(_sds((D_MODEL, D_MODEL), BF16), *full(D_MODEL, D_MODEL))])[0]

    dw["w_co"] = dw_square("dw_co", co, dx2)
    dx1, dg_cross = mm("cq_proj_bwd", dcq, w["w_cq"], "nt", (nr, 1, 1), row(D_MODEL), full(D_MODEL, D_MODEL),
                       [row_bf16, col_sum],
                       extras=[(x1, *row(D_MODEL)), (sm["g_cross"], *vec(D_MODEL)), (dx2, *row(D_MODEL))],
                       epi=_epi_rms_bwd, acc_outs=(1,))
    dw["w_cq"] = dw_square("dw_cq", uq, dcq)
    dckv = jnp.concatenate([dck, dcv], axis=1)
    kv_chunk = 2 * D_MODEL // N_DEV
    dw["w_ckv"] = mm("dw_ckv", mn, dckv, "tn", (1, N_DEV, 1), full(N_MEM, D_MODEL), ((N_MEM, kv_chunk), lambda i, j, k: (0, j)),
                      [(_sds((N_DEV, D_MODEL, kv_chunk), BF16), (None, D_MODEL, kv_chunk), lambda i, j, k: (j, 0, 0))])[0]
    dg_mem = mm("ckv_proj_bwd", dckv, w["w_ckv"], "nt", (1, 1, N_DEV), ((N_MEM, kv_chunk), lambda i, j, k: (0, k)),
                 ((None, D_MODEL, kv_chunk), lambda i, j, k: (k, 0, 0)), [(_sds((1, D_MODEL), F32), *vec(D_MODEL))],
                 extras=[(mem, *full(N_MEM, D_MODEL)), (sm["g_mem"], *vec(D_MODEL))], epi=_epi_rms_bwd_g, acc_outs=(0,))[0]

    dya, dyc, dz, db_gate = mm(
        "out_proj_bwd_mix", dx1, w["w_out"], "nt", (nr, 1, 1), row(D_MODEL), full(D_MODEL, D_MODEL),
        [(_sds((seq, D_MODEL), BF16), *row(D_MODEL)), (_sds((seq, D_MODEL), BF16), *row(D_MODEL)),
         (_sds((seq, IN_W), BF16), *gates_blk), (_sds((1, 2 * D_MODEL), F32), *vec(2 * D_MODEL))],
        extras=[(ya, *row(D_MODEL)), (yc, *row(D_MODEL)), (z, *gates_blk), (sm["b_gate"], *vec(2 * D_MODEL))],
        epi=_epi_mix_bwd, acc_outs=(3,))
    dw["w_out"] = dw_square("dw_out", merged, dx1)
    dattn = mm("attn_proj_bwd", dya, w["w_attn_proj"], "nt", (nr, 1, 1), row(D_MODEL), full(GROUP_W, D_MODEL),
                [(_sds((seq, GROUP_W), BF16), *row(GROUP_W))])[0]
    pc = D_MODEL // N_DEV
    dw["w_attn_proj"] = mm("dw_attn_proj", attn, dya, "tn", (1, 1, nb2), ((big2, GROUP_W), lambda i, j, k: (k, 0)),
                           ((big2, D_MODEL), lambda i, j, k: (k, 0)),
                           [(_sds((N_DEV, GROUP_W, pc), BF16), (N_DEV, GROUP_W, pc), lambda i, j, k: (0, 0, 0))],
                           out_chunks=N_DEV)[0]
    cvec = (_sds((1, CONV_CH), F32), *vec(CONV_CH))
    dc2, dg_ln_g, dg_ln_b, dg_conv_b = mm(
        "conv_proj_bwd_ln", dyc, w["w_conv_proj"], "nt", (nr, 1, 1), row(D_MODEL), full(CONV_CH, D_MODEL),
        [(_sds((seq, CONV_CH), F32), *row(CONV_CH)), cvec, cvec, cvec],
        extras=[(c2, *row(CONV_CH)), (sm["conv_ln_g"], *vec(CONV_CH)), (sm["conv_ln_b"], *vec(CONV_CH))],
        epi=_epi_ln_bwd, acc_outs=(1, 2, 3))
    dw["w_conv_proj"] = mm("dw_conv_proj", c4, dyc, "tn", (1, 1, nb2), ((big2, CONV_CH), lambda i, j, k: (k, 0)),
                           ((big2, D_MODEL), lambda i, j, k: (k, 0)),
                           [(_sds((N_DEV, CONV_CH, pc), BF16), (N_DEV, CONV_CH, pc), lambda i, j, k: (0, 0, 0))],
                           out_chunks=N_DEV)[0]
    dz, dg_conv_w = carry("conv_bwd", 2, _conv_bwd, dc2, z, w["taps"], dz)
    preps = _attn_bwd_prep(dattn, attn, lt)
    d_gs = []
    for g in range(N_GROUPS):
        name = "attn_bwd_%d" % g
        d_gs.append(carry(name, 1, _attn_bwd, name, a_gs[g], *preps[g])[0])
    dz = _dqkv_post(d_gs, tabs, dz)
    dw["w_in"] = mm("dw_in", u, dz, "tn", (1, N_DEV, nb2), ((big2, D_MODEL), lambda i, j, k: (k, 0)),
                    ((big2, D_MODEL), lambda i, j, k: (k, j)),
                    [(_sds((N_DEV, D_MODEL, D_MODEL), BF16), (None, D_MODEL, D_MODEL), lambda i, j, k: (j, 0, 0))])[0]
    grad_x, dg_mix = mm("in_proj_bwd", dz, w["w_in"], "nt", (nr, 1, 1), row(IN_W), whole3(w["w_in"]), [row_f32, col_sum],
                        extras=[(x, *row(D_MODEL)), (sm["g_mix"], *vec(D_MODEL)), (dx1, *row(D_MODEL))],
                        epi=_epi_rms_bwd, acc_outs=(1,), split=("sum", N_DEV), b_resident=True)
    small = dict(g_mix=dg_mix, b_gate=db_gate, conv_b=dg_conv_b, conv_ln_g=dg_ln_g, conv_ln_b=dg_ln_b, g_cross=dg_cross,
                 g_mem=dg_mem, g_mlp=dg_mlp, g_final=dg_final, loss=loss, conv_w=dg_conv_w)
    return grad_x, dw, small


SHARD_SHAPE = dict(w_in=(1024, 1024), w_attn_proj=(512, 128), w_conv_proj=(768, 128), w_out=(128, 1024), w_cq=(128, 1024),
                   w_ckv=(1024, 256), w_co=(128, 1024), w_up=(1024, 512), w_down=(512, 1024))
FWD_CARRY = {"in_proj":("w_attn_proj", "w_conv_proj", "w_out", "w_cq", "w_ckv", "w_co", "taps"),
             "qkv_prep": ("w_up",), "conv_fwd": ("w_down",)}
BWD_CARRY = {"dw_up": ("w_down",), "out_proj_bwd_mix": ("w_co", "w_cq"), "conv_bwd": ("w_up", "w_ckv"),
             "attn_bwd_0": ("w_out",), "attn_bwd_1": ("w_attn_proj", "w_conv_proj"), "in_proj_bwd": ("w_in",)}


def _cols_to_2d(a):
    return a.transpose(1, 0, 2).reshape(a.shape[1], -1)


def _in_proj_gather(u, w_shard, comm):
    seq = u.shape[0]
    tm = min(1024, seq)
    x, y, c = lax.axis_index("x"), lax.axis_index("y"), lax.axis_index("c")
    ident = lambda px, py, pc: 4 * px + 2 * py + pc
    far = [(1 - x, y), (x, 1 - y), (1 - x, 1 - y)]
    order = jnp.stack([ident(x, y, c), ident(x, y, 1 - c)] + [ident(px, py, c) for px, py in far]
                      + [ident(px, py, 1 - c) for px, py in far]).astype(jnp.int32)
    n_far = len(far)

    def body(order_ref, u_ref, wsh_ref, *rest):
        c_in, z_ref, wg_ref = rest[:comm.n], rest[comm.n], rest[comm.n + 1]
        c_out = rest[comm.n + 2:2 * comm.n + 2]
        wbuf, load_sem, local_sem, recv_sems, ici_send, d2d_send = rest[2 * comm.n + 2:2 * comm.n + 8]
        sems = rest[2 * comm.n + 8:]
        jj, i = pl.program_id(0), pl.program_id(1)
        (kx, ky, kc), me, chips = _Comm._where()
        sibling = (kx, ky, 1 - kc)
        n = order_ref[jj]

        def push(src, blk, send, to):
            return pltpu.make_async_remote_copy(src_ref=src, dst_ref=wg_ref.at[blk], send_sem=send,
                                                recv_sem=recv_sems.at[blk], device_id=to, device_id_type=MESH)

        def load(src):
            cp = pltpu.make_async_copy(src, wbuf, load_sem)
            cp.start()
            cp.wait()

        @pl.when(jnp.logical_and(jj == 0, i == 0))
        def _():
            push(wsh_ref, me, d2d_send, sibling).start()
            for (px, py) in chips:
                push(wsh_ref, me, ici_send, (px, py, kc)).start()
            pltpu.make_async_copy(wsh_ref, wg_ref.at[me], local_sem).start()
            if comm.n:
                comm.start(c_in, c_out, sems)
            load(wsh_ref)

        @pl.when(jnp.logical_and(jj > 0, i == 0))
        def _():
            push(wg_ref.at[n], n, d2d_send, sibling).wait_recv()
            for idx, (px, py) in enumerate(chips):
                @pl.when(jj == 2 + idx)
                def _():
                    blk = 4 * px + 2 * py + kc
                    push(wg_ref.at[blk], blk, d2d_send, sibling).start()

            load(wg_ref.at[n])

        z_ref[...] = _dot(u_ref[...], wbuf[...], "nn").astype(BF16)

        @pl.when(jnp.logical_and(jj == N_DEV - 1, i == pl.num_programs(1) - 1))
        def _():
            def drain_sends(send, count):
                blocks = wg_ref.at[pl.ds(0, count)]
                pltpu.make_async_remote_copy(src_ref=blocks, dst_ref=blocks, send_sem=send, recv_sem=recv_sems.at[0],
                                             device_id=sibling, device_id_type=MESH).wait_send()

            drain_sends(ici_send, n_far)
            drain_sends(d2d_send, n_far + 1)
            pltpu.make_async_copy(wsh_ref, wg_ref.at[me], local_sem).wait()
            if comm.n:
                comm.wait(c_in, c_out, sems)

    any_spec = pl.BlockSpec(memory_space=pl.ANY)
    grid_spec = pltpu.PrefetchScalarGridSpec(
        num_scalar_prefetch=1, grid=(N_DEV, seq // tm),
        in_specs=[pl.BlockSpec((tm, D_MODEL), lambda jj, i, order_ref: (i, 0)), any_spec] + comm.in_specs,
        out_specs=[pl.BlockSpec((tm, D_MODEL), lambda jj, i, order_ref: (i, order_ref[jj])), any_spec] + comm.out_specs,
        scratch_shapes=[pltpu.VMEM((D_MODEL, D_MODEL), BF16), pltpu.SemaphoreType.DMA, pltpu.SemaphoreType.DMA,
                        pltpu.SemaphoreType.DMA((N_DEV,)), pltpu.SemaphoreType.DMA, pltpu.SemaphoreType.DMA] + comm.scratch)
    return pl.pallas_call(
        body, name="in_proj_gather", grid_spec=grid_spec,
        out_shape=[jax.ShapeDtypeStruct((seq, IN_W), BF16), jax.ShapeDtypeStruct((N_DEV, D_MODEL, D_MODEL), BF16)] + comm.out_shape,
        compiler_params=_params(dimension_semantics=("arbitrary", "arbitrary")),
    )(order, u, w_shard, *comm.arrays)


class _Plan:
    def __init__(self, shards, n_tap_cols):
        self.shards = shards
        self.gathered = {}
        self.parts = {}
        self.n_tap_cols = n_tap_cols

    def project_in(self, u):
        comm = self.comm("in_proj", None)
        res = _in_proj_gather(u, self.shards["w_in"], comm)
        self.gathered["w_in"] = res[1]
        self.done("in_proj", res[2:])
        return res[0]

    def comm(self, name, dw):
        if name in FWD_CARRY:
            return _Comm(replicated=[self.shards[k] for k in FWD_CARRY[name]])
        if name in BWD_CARRY:
            return _Comm(chunked=[dw[k].reshape((N_DEV,) + SHARD_SHAPE[k]) for k in BWD_CARRY[name]])
        return _NO_COMM

    def done(self, name, got):
        if name in FWD_CARRY:
            self.gathered.update(zip(FWD_CARRY[name], got))
        elif name in BWD_CARRY:
            self.parts.update(zip(BWD_CARRY[name], got))

    def w(self, key):
        g = self.gathered[key]
        if key in ("w_in", "w_up", "w_ckv"):
            return g
        if key in ("w_attn_proj", "w_conv_proj"):
            return _cols_to_2d(g)
        if key == "taps":
            return jnp.pad(_cols_to_2d(g[:, :CONV_K, :self.n_tap_cols]), ((0, 1), (0, 0)))
        return g.reshape(-1, g.shape[-1])


def _adamw(name, w, m, v, parts, comm=_NO_COMM):
    rows, cols = w.shape
    n_parts = parts.shape[0]
    rb = rows if rows <= 256 or rows % 256 else 256

    def body(w_ref, m_ref, v_ref, p_ref, g_ref, d_ref, nm_ref, nv_ref):
        g = p_ref[0].astype(F32)
        for q in range(1, n_parts):
            g = g + p_ref[q].astype(F32)
        wv = w_ref[...]
        nm = ADAM_B1 * m_ref[...] + (1.0 - ADAM_B1) * g
        nv = ADAM_B2 * v_ref[...] + (1.0 - ADAM_B2) * jnp.square(g)
        m_hat = nm / (1.0 - ADAM_B1 ** ADAM_STEP)
        v_hat = nv / (1.0 - ADAM_B2 ** ADAM_STEP)
        g_ref[...] = g
        d_ref[...] = -ADAM_LR * (m_hat / (jnp.sqrt(v_hat) + ADAM_EPS) + ADAM_WD * wv)
        nm_ref[...] = nm
        nv_ref[...] = nv

    blk = pl.BlockSpec((rb, cols), lambda i: (i, 0))
    return _pcall(body, name, (rows // rb,), [w, m, v, parts],
                  [blk, blk, blk, pl.BlockSpec((n_parts, rb, cols), lambda i: (0, i, 0))],
                  [jax.ShapeDtypeStruct((rows, cols), F32)] * 4, [blk] * 4, comm=comm)


def _sum_parts(name, parts):
    def body(p_ref, o_ref):
        acc = p_ref[0]
        for q in range(1, parts.shape[0]):
            acc = acc + p_ref[q]
        o_ref[...] = acc

    return _pcall(body, name, (1,), [parts], [pl.BlockSpec(parts.shape, lambda i: (0, 0, 0))],
                  [jax.ShapeDtypeStruct(parts.shape[1:], F32)], [pl.BlockSpec(parts.shape[1:], lambda i: (0, 0))])[0]


BIG = ("w_in", "w_attn_proj", "w_conv_proj", "w_out", "w_cq", "w_ckv", "w_co", "w_up", "w_down")
SMALL = ("g_mix", "b_gate", "conv_b", "conv_ln_g", "conv_ln_b", "g_cross", "g_mem", "g_mlp", "g_final")
SMALL_ORDER = SMALL + ("loss", "conv_w")
WEIGHTS = ("g_mix", "w_in", "b_gate", "conv_w", "conv_b", "conv_ln_g", "conv_ln_b", "w_attn_proj", "w_conv_proj", "w_out",
           "g_cross", "g_mem", "w_cq", "w_ckv", "w_co", "g_mlp", "w_up", "w_down", "g_final")


def kernel(x, mem, g_mix, w_in, b_gate, conv_w, conv_b, conv_ln_g, conv_ln_b, w_attn_proj, w_conv_proj, w_out, g_cross, g_mem, w_cq, w_ckv, w_co, g_mlp, w_up, w_down, g_final, loss_target, m_g_mix, m_w_in, m_b_gate, m_conv_w, m_conv_b, m_conv_ln_g, m_conv_ln_b, m_w_attn_proj, m_w_conv_proj, m_w_out, m_g_cross, m_g_mem, m_w_cq, m_w_ckv, m_w_co, m_g_mlp, m_w_up, m_w_down, m_g_final, v_g_mix, v_w_in, v_b_gate, v_conv_w, v_conv_b, v_conv_ln_g, v_conv_ln_b, v_w_attn_proj, v_w_conv_proj, v_w_out, v_g_cross, v_g_mem, v_w_cq, v_w_ckv, v_w_co, v_g_mlp, v_w_up, v_w_down, v_g_final):
    args = dict(locals())
    wts = {k: args[k] for k in WEIGHTS}
    mom = {k: args["m_" + k] for k in WEIGHTS}
    var = {k: args["v_" + k] for k in WEIGHTS}
    two_d = lambda a: a.reshape(a.shape[-2:]) if a.ndim == 3 else a.reshape(1, -1)

    shards = {k: two_d(wts[k]).astype(BF16) for k in BIG}
    shards["taps"] = jnp.pad(two_d(conv_w), ((0, 1), (0, HEAD_DIM - conv_w.shape[-1])))
    plan = _Plan(shards, conv_w.shape[-1])
    sm = {k: two_d(wts[k]) for k in SMALL}

    grad_x, _, small = _local_step(x[0], mem[0], loss_target[0], sm, plan)
    parts = plan.parts

    out = {}
    small_comm = _Comm(replicated=[small[k] for k in SMALL_ORDER])
    for k in BIG:
        res = _adamw("adamw_" + k, two_d(wts[k]), two_d(mom[k]), two_d(var[k]), parts[k],
                     comm=small_comm if k == BIG[0] else _NO_COMM)
        out[k] = [r.reshape(wts[k].shape) for r in res[:4]]
        if k == BIG[0]:
            small_parts = dict(zip(SMALL_ORDER, res[4:]))
    for k in SMALL:
        res = _adamw("adamw_" + k, two_d(wts[k]), two_d(mom[k]), two_d(var[k]), small_parts[k])
        out[k] = [r.reshape(wts[k].shape) for r in res]
    loss = _sum_parts("loss_sum", small_parts["loss"])[0, 0]
    me = 4 * lax.axis_index("x") + 2 * lax.axis_index("y") + lax.axis_index("c")
    n_tap_cols = conv_w.shape[-1]
    tap_parts = lax.dynamic_slice(small_parts["conv_w"], (0, 0, me * n_tap_cols), (N_DEV, CONV_K, n_tap_cols))
    res = _adamw("adamw_conv_w", two_d(conv_w), two_d(m_conv_w), two_d(v_conv_w), tap_parts)
    out["conv_w"] = [r.reshape(conv_w.shape) for r in res]

    return (loss, grad_x[None], *[out[k][0] for k in WEIGHTS], *[out[k][1] for k in WEIGHTS],
            *[out[k][2] for k in WEIGHTS], *[out[k][3] for k in WEIGHTS])
```

```python
import functools

import jax
import jax.numpy as jnp
import numpy as np
from jax import lax
from jax.experimental import pallas as pl
from jax.experimental.pallas import tpu as pltpu

F32 = jnp.float32
BF16 = jnp.bfloat16

N_DEV = 8
D_MODEL = 1024
N_MEM = 256
HEAD_DIM = 128
HEADS_PER_GROUP = 4
GROUP_W = HEADS_PER_GROUP * HEAD_DIM
DILATIONS = (1, 4, 16)
BAND = 128
N_GROUPS = 3
ATTN_W = N_GROUPS * GROUP_W
QKV_W = 3 * ATTN_W
ROT_DIM = HEAD_DIM // 4
ROPE_THETA = 500000.0
CONV_CH = 768
CONV_K = 31
CONV_HALO = 32
SUBLANES = 8
CONV_ROWS = 64
IN_W = 8192
GLU_COL_BLK = QKV_W // (2 * CONV_CH)
GATE_COL_BLK = (QKV_W + 2 * CONV_CH) // (2 * D_MODEL)
CROSS_HEADS = 4
CROSS_HD = D_MODEL // CROSS_HEADS
D_FF = 4096
EPS = 1e-6
NEG = -1e30
QB = 4
ROW_BLK = QB * BAND

ADAM_LR = 0.001
ADAM_B1 = 0.9
ADAM_B2 = 0.999
ADAM_EPS = 1e-08
ADAM_WD = 0.01
ADAM_STEP = 10

VMEM_LIMIT = 56 * 1024 * 1024
MESH = pl.DeviceIdType.MESH


def _params(**kw):
    return pltpu.CompilerParams(vmem_limit_bytes=VMEM_LIMIT, **kw)


def _sigmoid(x):
    return 1.0 / (1.0 + jnp.exp(-x))


def _dot(a, b, kind):
    dims = {"nn": (((1,), (0,)), ((), ())), "nt": (((1,), (1,)), ((), ())), "tn": (((0,), (0,)), ((), ()))}[kind]
    if a.dtype != BF16:
        a = a.astype(BF16)
    if b.dtype != BF16:
        b = b.astype(BF16)
    return lax.dot_general(a, b, dims, preferred_element_type=F32)


def _peers():
    x, y, c = lax.axis_index("x"), lax.axis_index("y"), lax.axis_index("c")
    me = 4 * x + 2 * y + c
    peers = [(x, y, 1 - c), (1 - x, y, c), (x, 1 - y, c), (1 - x, 1 - y, c),
             (1 - x, y, 1 - c), (x, 1 - y, 1 - c), (1 - x, 1 - y, 1 - c)]
    return me, peers


class _Comm:
    def __init__(self, chunked=(), replicated=()):
        self.arrays = list(chunked) + list(replicated)
        self.n_c = len(chunked)
        self.n = len(self.arrays)
        self.out_shape = [jax.ShapeDtypeStruct(a.shape, a.dtype) for a in chunked]
        self.out_shape += [jax.ShapeDtypeStruct((N_DEV,) + a.shape, a.dtype) for a in replicated]
        self.in_specs = [pl.BlockSpec(memory_space=pl.ANY)] * self.n
        self.out_specs = [pl.BlockSpec(memory_space=pl.ANY)] * self.n
        self.scratch = [pltpu.SemaphoreType.DMA((self.n,))] * 5 if self.n else []

    @staticmethod
    def _where():
        x, y, c = lax.axis_index("x"), lax.axis_index("y"), lax.axis_index("c")
        chips = [(1 - x, y), (x, 1 - y), (1 - x, 1 - y)]
        return (x, y, c), 4 * x + 2 * y + c, chips

    def _local(self, ins, outs, sems, a, me):
        src = ins[a].at[me] if a < self.n_c else ins[a]
        return pltpu.make_async_copy(src, outs[a].at[me], sems[2].at[a])

    @staticmethod
    def _remote(src, dst, send, recv, to):
        return pltpu.make_async_remote_copy(src_ref=src, dst_ref=dst, send_sem=send, recv_sem=recv, device_id=to,
                                            device_id_type=MESH)

    def start(self, ins, outs, sems):
        (x, y, c), me, chips = self._where()
        for a in range(self.n):
            self._local(ins, outs, sems, a, me).start()
            if a < self.n_c:
                for (px, py, pc) in _peers()[1]:
                    self._remote(ins[a].at[4 * px + 2 * py + pc], outs[a].at[me], sems[0].at[a], sems[1].at[a], (px, py, pc)).start()
            else:
                self._remote(ins[a], outs[a].at[me], sems[3].at[a], sems[4].at[a], (x, y, 1 - c)).start()
                for (px, py) in chips:
                    self._remote(ins[a], outs[a].at[me], sems[0].at[a], sems[1].at[a], (px, py, c)).start()

    def wait(self, ins, outs, sems):
        (x, y, c), me, chips = self._where()
        sibling = (x, y, 1 - c)

        def drain(a, pair, count):
            blocks = outs[a].at[pl.ds(0, count)]
            cp = self._remote(blocks, blocks, sems[pair].at[a], sems[pair + 1].at[a], sibling)
            cp.wait_send()
            cp.wait_recv()

        for a in range(self.n):
            if a < self.n_c:
                drain(a, 0, N_DEV - 1)
            else:
                drain(a, 0, len(chips))
                for (px, py) in chips:
                    blk = outs[a].at[4 * px + 2 * py + c]
                    self._remote(blk, blk, sems[3].at[a], sems[4].at[a], sibling).start()
        for a in range(self.n):
            if a >= self.n_c:
                drain(a, 3, len(chips) + 1)
            self._local(ins, outs, sems, a, me).wait()


_NO_COMM = _Comm()


def _pcall(body, name, grid, operands, in_specs, out_shape, out_specs, scratch=(), aliases=None, comm=_NO_COMM, **params):
    n_in, n_out, n_scr = len(operands), len(out_shape), len(scratch)
    grid = tuple(grid)

    def carried(*refs):
        ins, c_in = refs[:n_in], refs[n_in:n_in + comm.n]
        o0 = n_in + comm.n
        outs, c_out = refs[o0:o0 + n_out], refs[o0 + n_out:o0 + n_out + comm.n]
        s0 = o0 + n_out + comm.n
        scr, sems = refs[s0:s0 + n_scr], refs[s0 + n_scr:]
        ids = [pl.program_id(ax) for ax in range(len(grid))]

        @pl.when(functools.reduce(jnp.logical_and, [p == 0 for p in ids]))
        def _():
            comm.start(c_in, c_out, sems)

        body(*ins, *outs, *scr)

        @pl.when(functools.reduce(jnp.logical_and, [p == g - 1 for p, g in zip(ids, grid)]))
        def _():
            comm.wait(c_in, c_out, sems)

    return pl.pallas_call(
        carried if comm.n else body, name=name, grid=grid, in_specs=list(in_specs) + comm.in_specs,
        out_shape=list(out_shape) + comm.out_shape, out_specs=list(out_specs) + comm.out_specs,
        scratch_shapes=list(scratch) + comm.scratch, input_output_aliases=aliases or {},
        compiler_params=_params(dimension_semantics=("arbitrary",) * len(grid), **params),
    )(*operands, *comm.arrays)


def _mm(name, a, b, kind, grid, a_blk, b_blk, outs, extras=(), epi=None, acc_outs=(), j_outer=False, comm=_NO_COMM,
        split=None, b_resident=False, out_chunks=0):
    gi, gj, gk = grid
    n_ex = len(extras)
    n_out = len(outs)
    mode, n_chunks = split if split is not None else (None, 1)

    def spec(blk, fn, **kw):
        return pl.BlockSpec(blk, (lambda j, i, k: fn(i, j, k)) if j_outer else fn, **kw)

    def b_chunk(b_ref, c):
        if len(b_ref.shape) == 3:
            return b_ref[c]
        rows, cols = b_ref.shape
        if (kind == "nn") == (mode == "cols"):
            return b_ref[:, c * (cols // n_chunks):(c + 1) * (cols // n_chunks)]
        return b_ref[c * (rows // n_chunks):(c + 1) * (rows // n_chunks), :]

    def col_chunk(ref, c):
        width = ref.shape[-1] // n_chunks
        return slice(c * width, (c + 1) * width)

    def body(*refs):
        a_ref, b_ref = refs[0], refs[1]
        ex = refs[2:2 + n_ex]
        out_refs = refs[2 + n_ex:2 + n_ex + n_out]
        acc_ref = refs[2 + n_ex + n_out] if gk > 1 else None
        i = pl.program_id(1 if j_outer else 0)
        k = pl.program_id(2)
        if mode == "cols":
            a_val = a_ref[...]
            for c in range(n_chunks):
                acc = _dot(a_val, b_chunk(b_ref, c), kind)
                vals = epi(acc, *[e[:, col_chunk(e, c)] for e in ex]) if epi is not None else (acc,)
                for o, v in zip(out_refs, vals):
                    o[:, col_chunk(o, c)] = v.astype(o.dtype)
            return
        if mode == "sum":
            part = _dot(a_ref[:, col_chunk(a_ref, 0)], b_chunk(b_ref, 0), kind)
            for c in range(1, n_chunks):
                part = part + _dot(a_ref[:, col_chunk(a_ref, c)], b_chunk(b_ref, c), kind)
        else:
            part = _dot(a_ref[...], b_ref[...], kind)

        def finish(acc):
            if out_chunks:
                width = acc.shape[-1] // out_chunks
                for c in range(out_chunks):
                    out_refs[0][c] = acc[:, c * width:(c + 1) * width].astype(out_refs[0].dtype)
                return
            vals = epi(acc, *[e[...] for e in ex]) if epi is not None else (acc,)
            for idx, (o, v) in enumerate(zip(out_refs, vals)):
                if idx in acc_outs:
                    @pl.when(i == 0)
                    def _():
                        o[...] = v.astype(o.dtype)

                    @pl.when(i != 0)
                    def _():
                        o[...] += v.astype(o.dtype)
                else:
                    o[...] = v.astype(o.dtype)

        if gk == 1:
            finish(part)
        else:
            @pl.when(k == 0)
            def _():
                acc_ref[...] = part

            @pl.when(k != 0)
            def _():
                acc_ref[...] += part

            @pl.when(k == gk - 1)
            def _():
                finish(acc_ref[...])

    scratch = []
    if gk > 1:
        tm = a_blk[0][-1] if kind == "tn" else a_blk[0][-2]
        tn = b_blk[0][-2] if kind == "nt" else b_blk[0][-1]
        scratch = [pltpu.VMEM((tm, tn), F32)]
    b_kw = dict(pipeline_mode=pl.Buffered(1)) if b_resident else {}
    return _pcall(body, name, (gj, gi, gk) if j_outer else (gi, gj, gk), [a, b] + [e for e, _, _ in extras],
                  [spec(*a_blk), spec(*b_blk, **b_kw)] + [spec(blk, fn) for _, blk, fn in extras],
                  [s for s, _, _ in outs], [spec(blk, fn) for _, blk, fn in outs], scratch, comm=comm)


def _rms_fwd_vals(x, g):
    r = lax.rsqrt(jnp.mean(x * x, axis=-1, keepdims=True) + EPS)
    return x * r * g


def _rms_bwd_vals(x, g, du):
    r = lax.rsqrt(jnp.mean(x * x, axis=-1, keepdims=True) + EPS)
    xh = x * r
    dxh = du * g
    dx = r * (dxh - xh * jnp.mean(dxh * xh, axis=-1, keepdims=True))
    return dx, jnp.sum(du * xh, axis=0, keepdims=True)


def _rms_fwd(name, x, g, rows, comm=_NO_COMM):
    n = x.shape[0]

    def body(x_ref, g_ref, o_ref):
        o_ref[...] = _rms_fwd_vals(x_ref[...], g_ref[...]).astype(BF16)

    return _pcall(body, name, (n // rows,), [x, g],
                  [pl.BlockSpec((rows, D_MODEL), lambda i: (i, 0)), pl.BlockSpec((1, D_MODEL), lambda i: (0, 0))],
                  [jax.ShapeDtypeStruct(x.shape, BF16)], [pl.BlockSpec((rows, D_MODEL), lambda i: (i, 0))], comm=comm)


def _rope_tables(seq):
    half = ROT_DIM // 2
    pos = np.arange(seq, dtype=np.float32)
    inv_freq = np.float32(ROPE_THETA) ** (-np.arange(0, ROT_DIM, 2, dtype=np.float32) / np.float32(ROT_DIM))
    ang = (pos[:, None] * inv_freq[None, :]).astype(np.float32)
    cos, sin = np.cos(ang), np.sin(ang)
    rest = HEAD_DIM - ROT_DIM
    c = np.concatenate([cos, cos, np.ones((seq, rest), np.float32)], axis=1)
    s1 = np.concatenate([np.zeros((seq, half), np.float32), sin, np.zeros((seq, rest), np.float32)], axis=1)
    s2 = np.concatenate([-sin, np.zeros((seq, half + rest), np.float32)], axis=1)
    return jnp.asarray(c), jnp.asarray(s1), jnp.asarray(s2)


def _group_shapes(seq, width, dtype):
    return [jax.ShapeDtypeStruct((d, seq // d, width), dtype) for d in DILATIONS]


def _group_specs(width):
    return [pl.BlockSpec((d, ROW_BLK // d, width), lambda i: (0, i, 0)) for d in DILATIONS]


def _qkv_prep(z, tabs, comm=_NO_COMM):
    seq = z.shape[0]

    def body(z_ref, c_ref, s1_ref, s2_ref, a0, a1, a2, sc):
        outs = (a0, a1, a2)
        c, s1, s2 = c_ref[...], s1_ref[...], s2_ref[...]
        for part in range(3):
            for hh in range(N_GROUPS * HEADS_PER_GROUP):
                g, hl = divmod(hh, HEADS_PER_GROUP)
                col = part * ATTN_W + hh * HEAD_DIM
                ocol = part * GROUP_W + hl * HEAD_DIM
                x = z_ref[:, col:col + HEAD_DIM].astype(F32)
                if part < 2:
                    x = x * c + pltpu.roll(x, ROT_DIM // 2, 1) * s1 + pltpu.roll(x, HEAD_DIM - ROT_DIM // 2, 1) * s2
                d = DILATIONS[g]
                if d == 1:
                    outs[g][0, :, ocol:ocol + HEAD_DIM] = x.astype(BF16)
                else:
                    sc[...] = x
                    for r in range(d):
                        outs[g][r, :, ocol:ocol + HEAD_DIM] = sc[pl.ds(r, ROW_BLK // d, stride=d), :].astype(BF16)

    tab_spec = pl.BlockSpec((ROW_BLK, HEAD_DIM), lambda i: (i, 0))
    return _pcall(body, "qkv_prep", (seq // ROW_BLK,), [z, *tabs],
                  [pl.BlockSpec((ROW_BLK, QKV_W), lambda i: (i, 0)), tab_spec, tab_spec, tab_spec],
                  _group_shapes(seq, ATTN_W, BF16), _group_specs(ATTN_W), [pltpu.VMEM((ROW_BLK, HEAD_DIM), F32)], comm=comm)


def _band_masks_2(t):
    qi = lax.broadcasted_iota(jnp.int32, (BAND, 2 * BAND), 0)
    kj = lax.broadcasted_iota(jnp.int32, (BAND, 2 * BAND), 1)
    band = jnp.logical_and(kj >= qi, kj <= qi + BAND)
    return band, jnp.logical_and(band, jnp.logical_or(kj >= BAND, t > 0))


def _attn_fwd(name, a_g, comm=_NO_COMM):
    dil, m_len, _ = a_g.shape
    qb = min(QB, m_len // BAND)
    rows = qb * BAND
    steps = m_len // rows
    scale = HEAD_DIM ** -0.5

    tiles = [(sb, h) for sb in range(qb) for h in range(HEADS_PER_GROUP)]

    def body(q_ref, kc_ref, vc_ref, kp_ref, vp_ref, o_ref, l_ref, k_all, v_all, s_scr, p_scr, r_scr):
        t = pl.program_id(1)
        k_all[0:BAND, :] = kp_ref[...]
        k_all[BAND:, :] = kc_ref[...]
        v_all[0:BAND, :] = vp_ref[...]
        v_all[BAND:, :] = vc_ref[...]
        band, band_first = _band_masks_2(t)
        for idx, (sb, h) in enumerate(tiles):
            cs = slice(h * HEAD_DIM, (h + 1) * HEAD_DIM)
            s = _dot(q_ref[sb * BAND:(sb + 1) * BAND, cs], k_all[sb * BAND:(sb + 2) * BAND, cs], "nt") * scale
            s_scr[idx] = jnp.where(band_first if sb == 0 else band, s, NEG)
        lane = lax.broadcasted_iota(jnp.int32, (BAND, HEAD_DIM), 1)
        lse_rows = [jnp.zeros((BAND, HEAD_DIM), F32)] * qb
        for idx, (sb, h) in enumerate(tiles):
            s = s_scr[idx]
            mx = jnp.max(s, axis=-1, keepdims=True)
            p = jnp.exp(s - mx)
            den = jnp.sum(p, axis=-1, keepdims=True)
            p_scr[idx] = p.astype(BF16)
            r_scr[idx] = jnp.broadcast_to(1.0 / den, (BAND, HEAD_DIM))
            lse_rows[sb] = jnp.where(lane == h, jnp.broadcast_to(mx + jnp.log(den), (BAND, HEAD_DIM)), lse_rows[sb])
        for sb in range(qb):
            l_ref[sb * BAND:(sb + 1) * BAND, :] = lse_rows[sb]
        for idx, (sb, h) in enumerate(tiles):
            cs = slice(h * HEAD_DIM, (h + 1) * HEAD_DIM)
            o_ref[sb * BAND:(sb + 1) * BAND, cs] = _dot(p_scr[idx], v_all[sb * BAND:(sb + 2) * BAND, cs], "nn") * r_scr[idx]

    def prev(r, t):
        return jnp.maximum(qb * t - 1, 0)

    cur = lambda c: pl.BlockSpec((None, rows, GROUP_W), lambda r, t, c=c: (r, t, c))
    prv = lambda c: pl.BlockSpec((None, BAND, GROUP_W), lambda r, t, c=c: (r, prev(r, t), c))
    out_spec = lambda width: pl.BlockSpec((None, rows, width), lambda r, t: (r, t, 0))
    shp = lambda width: jax.ShapeDtypeStruct((dil, m_len, width), F32)
    n_t = len(tiles)
    return _pcall(body, name, (dil, steps), [a_g] * 5, [cur(0), cur(1), cur(2), prv(1), prv(2)],
                  [shp(GROUP_W), shp(HEAD_DIM)], [out_spec(GROUP_W), out_spec(HEAD_DIM)],
                  [pltpu.VMEM((rows + BAND, GROUP_W), BF16), pltpu.VMEM((rows + BAND, GROUP_W), BF16),
                   pltpu.VMEM((n_t, BAND, 2 * BAND), F32), pltpu.VMEM((n_t, BAND, 2 * BAND), BF16),
                   pltpu.VMEM((n_t, BAND, HEAD_DIM), F32)], comm=comm)


def _attn_merge(os_, ls_, seq):
    def body(o0, l0, o1, l1, o2, l2, at_ref, lt_ref, sc, lsc):
        for gi, l_r in enumerate((l1, l2)):
            d = DILATIONS[gi + 1]
            for r in range(d):
                lsc.at[gi][pl.ds(r, ROW_BLK // d, stride=d), :] = l_r[r]
        lse = (l0.at[0], lsc.at[0], lsc.at[1])
        lane = lax.broadcasted_iota(jnp.int32, (ROW_BLK, HEAD_DIM), 1)
        lt_rows = jnp.zeros((ROW_BLK, HEAD_DIM), F32)
        for h in range(HEADS_PER_GROUP):
            cs = slice(h * HEAD_DIM, (h + 1) * HEAD_DIM)
            for gi, o_r in enumerate((o1, o2)):
                d = DILATIONS[gi + 1]
                for r in range(d):
                    sc.at[gi][pl.ds(r, ROW_BLK // d, stride=d), :] = o_r[r, :, cs]
            l_h = [v[:, h:h + 1] for v in lse]
            mx = jnp.maximum(jnp.maximum(l_h[0], l_h[1]), l_h[2])
            e = [jnp.exp(v - mx) for v in l_h]
            tot = e[0] + e[1] + e[2]
            inv = 1.0 / tot
            at_ref[:, cs] = ((e[0] * inv) * o0[0, :, cs] + (e[1] * inv) * sc[0] + (e[2] * inv) * sc[1]).astype(BF16)
            lt_rows = jnp.where(lane == h, jnp.broadcast_to(mx + jnp.log(tot), (ROW_BLK, HEAD_DIM)), lt_rows)
        lt_ref[...] = lt_rows

    go, gl = _group_specs(GROUP_W), _group_specs(HEAD_DIM)
    return pl.pallas_call(
        body, name="attn_merge",
        out_shape=[jax.ShapeDtypeStruct((seq, GROUP_W), BF16), jax.ShapeDtypeStruct((seq, HEAD_DIM), F32)],
        grid=(seq // ROW_BLK,), in_specs=[go[0], gl[0], go[1], gl[1], go[2], gl[2]],
        out_specs=[pl.BlockSpec((ROW_BLK, GROUP_W), lambda i: (i, 0)), pl.BlockSpec((ROW_BLK, HEAD_DIM), lambda i: (i, 0))],
        scratch_shapes=[pltpu.VMEM((2, ROW_BLK, HEAD_DIM), F32), pltpu.VMEM((2, ROW_BLK, HEAD_DIM), F32)],
        compiler_params=_params(dimension_semantics=("arbitrary",)),
    )(os_[0], ls_[0], os_[1], ls_[1], os_[2], ls_[2])


def _attn_bwd_prep(dattn, attn, lt):
    seq = dattn.shape[0]

    def body(da_ref, at_ref, lt_ref, cl0, d1, cl1, d2, cl2, sc, csc):
        lane = lax.broadcasted_iota(jnp.int32, (ROW_BLK, HEAD_DIM), 1)
        cl = pltpu.roll(lt_ref[...], HEADS_PER_GROUP, 1)
        for h in range(HEADS_PER_GROUP):
            cs = slice(h * HEAD_DIM, (h + 1) * HEAD_DIM)
            da = da_ref[:, cs].astype(F32)
            cc = jnp.sum(da * at_ref[:, cs].astype(F32), axis=-1, keepdims=True)
            cl = jnp.where(lane == h, jnp.broadcast_to(cc, (ROW_BLK, HEAD_DIM)), cl)
            sc[...] = da
            for g, d_ref in ((1, d1), (2, d2)):
                d = DILATIONS[g]
                for r in range(d):
                    d_ref[r, :, cs] = sc[pl.ds(r, ROW_BLK // d, stride=d), :].astype(BF16)
        cl0[0] = cl
        csc[...] = cl
        for g, c_ref in ((1, cl1), (2, cl2)):
            d = DILATIONS[g]
            for r in range(d):
                c_ref[r] = csc[pl.ds(r, ROW_BLK // d, stride=d), :]

    go, gl = _group_specs(GROUP_W), _group_specs(HEAD_DIM)
    row = lambda width: pl.BlockSpec((ROW_BLK, width), lambda i: (i, 0))
    shape = lambda g, width, dt: jax.ShapeDtypeStruct((DILATIONS[g], seq // DILATIONS[g], width), dt)
    cl0, d1, cl1, d2, cl2 = pl.pallas_call(
        body, name="attn_bwd_prep",
        out_shape=[shape(0, HEAD_DIM, F32), shape(1, GROUP_W, BF16), shape(1, HEAD_DIM, F32), shape(2, GROUP_W, BF16),
                   shape(2, HEAD_DIM, F32)],
        grid=(seq // ROW_BLK,), in_specs=[row(GROUP_W), row(GROUP_W), row(HEAD_DIM)],
        out_specs=[gl[0], go[1], gl[1], go[2], gl[2]],
        scratch_shapes=[pltpu.VMEM((ROW_BLK, HEAD_DIM), F32), pltpu.VMEM((ROW_BLK, HEAD_DIM), F32)],
        compiler_params=_params(dimension_semantics=("arbitrary",)),
    )(dattn, attn, lt)
    return [(dattn[None], cl0), (d1, cl1), (d2, cl2)]


def _attn_bwd(name, a_g, da_g, cl_g, comm=_NO_COMM):
    dil, m_len, _ = a_g.shape
    qb = min(QB, m_len // BAND)
    rows = qb * BAND
    steps = m_len // rows
    scale = HEAD_DIM ** -0.5

    tiles = [(sb, h) for sb in range(qb) for h in range(HEADS_PER_GROUP)]

    def body(q_ref, kc_ref, vc_ref, kp_ref, vp_ref, da_ref, cl_ref, d_ref, dk_acc, dv_acc, car_k, car_v,
             k_all, v_all, s_scr, dp_scr, p_scr, ds_scr):
        tg = pl.program_id(1)
        t = steps - 1 - tg

        @pl.when(tg == 0)
        def _():
            car_k[...] = jnp.zeros_like(car_k)
            car_v[...] = jnp.zeros_like(car_v)

        k_all[0:BAND, :] = kp_ref[...]
        k_all[BAND:, :] = kc_ref[...]
        v_all[0:BAND, :] = vp_ref[...]
        v_all[BAND:, :] = vc_ref[...]
        zero = jnp.zeros((rows, GROUP_W), F32)
        dk_acc[0:rows, :] = zero
        dv_acc[0:rows, :] = zero
        dk_acc[rows:rows + BAND, :] = car_k[...]
        dv_acc[rows:rows + BAND, :] = car_v[...]
        band, band_first = _band_masks_2(t)
        for idx, (sb, h) in enumerate(tiles):
            cs = slice(h * HEAD_DIM, (h + 1) * HEAD_DIM)
            rs, ks = slice(sb * BAND, (sb + 1) * BAND), slice(sb * BAND, (sb + 2) * BAND)
            s_scr[idx] = _dot(q_ref[rs, cs], k_all[ks, cs], "nt")
            dp_scr[idx] = _dot(da_ref[rs, cs], v_all[ks, cs], "nt")
        for idx, (sb, h) in enumerate(tiles):
            cs = slice(h * HEAD_DIM, (h + 1) * HEAD_DIM)
            rs = slice(sb * BAND, (sb + 1) * BAND)
            cc = jnp.broadcast_to(cl_ref[rs, h:h + 1], (BAND, 2 * BAND))
            ltv = jnp.broadcast_to(cl_ref[rs, HEADS_PER_GROUP + h:HEADS_PER_GROUP + h + 1], (BAND, 2 * BAND))
            p = jnp.exp(jnp.where(band_first if sb == 0 else band, s_scr[idx] * scale - ltv, NEG))
            p_scr[idx] = p.astype(BF16)
            ds_scr[idx] = (p * (dp_scr[idx] - cc) * scale).astype(BF16)
        for idx, (sb, h) in enumerate(tiles):
            cs = slice(h * HEAD_DIM, (h + 1) * HEAD_DIM)
            rs, ks = slice(sb * BAND, (sb + 1) * BAND), slice(sb * BAND, (sb + 2) * BAND)
            d_ref[rs, cs] = _dot(ds_scr[idx], k_all[ks, cs], "nn").astype(BF16)
            dk_acc[ks, cs] += _dot(ds_scr[idx], q_ref[rs, cs], "tn")
            dv_acc[ks, cs] += _dot(p_scr[idx], da_ref[rs, cs], "tn")
        d_ref[:, GROUP_W:2 * GROUP_W] = dk_acc[BAND:rows + BAND, :].astype(BF16)
        d_ref[:, 2 * GROUP_W:3 * GROUP_W] = dv_acc[BAND:rows + BAND, :].astype(BF16)
        car_k[...] = dk_acc[0:BAND, :]
        car_v[...] = dv_acc[0:BAND, :]

    def rev(tg):
        return steps - 1 - tg

    def prev(tg):
        return jnp.maximum(qb * rev(tg) - 1, 0)

    cur = lambda c: pl.BlockSpec((None, rows, GROUP_W), lambda r, tg, c=c: (r, rev(tg), c))
    prv = lambda c: pl.BlockSpec((None, BAND, GROUP_W), lambda r, tg, c=c: (r, prev(tg), c))
    return _pcall(
        body, name, (dil, steps), [a_g, a_g, a_g, a_g, a_g, da_g, cl_g],
        [cur(0), cur(1), cur(2), prv(1), prv(2), cur(0), pl.BlockSpec((None, rows, HEAD_DIM), lambda r, tg: (r, rev(tg), 0))],
        [jax.ShapeDtypeStruct((dil, m_len, ATTN_W), BF16)], [pl.BlockSpec((None, rows, ATTN_W), lambda r, tg: (r, rev(tg), 0))],
        [pltpu.VMEM((rows + BAND, GROUP_W), F32), pltpu.VMEM((rows + BAND, GROUP_W), F32),
         pltpu.VMEM((BAND, GROUP_W), F32), pltpu.VMEM((BAND, GROUP_W), F32),
         pltpu.VMEM((rows + BAND, GROUP_W), BF16), pltpu.VMEM((rows + BAND, GROUP_W), BF16),
         pltpu.VMEM((len(tiles), BAND, 2 * BAND), F32), pltpu.VMEM((len(tiles), BAND, 2 * BAND), F32),
         pltpu.VMEM((len(tiles), BAND, 2 * BAND), BF16), pltpu.VMEM((len(tiles), BAND, 2 * BAND), BF16)], comm=comm)


def _dqkv_post(d_gs, tabs, dz):
    seq = dz.shape[0]

    def body(g0, g1, g2, c_ref, s1_ref, s2_ref, dz_any, o_ref, sc):
        del dz_any
        ins = (g0, g1, g2)
        c, s1, s2 = c_ref[...], s1_ref[...], s2_ref[...]
        for part in range(3):
            for hh in range(N_GROUPS * HEADS_PER_GROUP):
                g, hl = divmod(hh, HEADS_PER_GROUP)
                icol = part * GROUP_W + hl * HEAD_DIM
                ocol = part * ATTN_W + hh * HEAD_DIM
                d = DILATIONS[g]
                if d == 1:
                    x = ins[g][0, :, icol:icol + HEAD_DIM].astype(F32)
                else:
                    for r in range(d):
                        sc[pl.ds(r, ROW_BLK // d, stride=d), :] = ins[g][r, :, icol:icol + HEAD_DIM].astype(F32)
                    x = sc[...]
                if part < 2:
                    x = x * c + pltpu.roll(x * s1, HEAD_DIM - ROT_DIM // 2, 1) + pltpu.roll(x * s2, ROT_DIM // 2, 1)
                o_ref[:, ocol:ocol + HEAD_DIM] = x.astype(BF16)

    tab_spec = pl.BlockSpec((ROW_BLK, HEAD_DIM), lambda i: (i, 0))
    return pl.pallas_call(
        body, name="dqkv_post", out_shape=jax.ShapeDtypeStruct(dz.shape, BF16), grid=(seq // ROW_BLK,),
        in_specs=_group_specs(ATTN_W) + [tab_spec, tab_spec, tab_spec, pl.BlockSpec(memory_space=pl.ANY)],
        out_specs=pl.BlockSpec((ROW_BLK, QKV_W), lambda i: (i, 0)),
        scratch_shapes=[pltpu.VMEM((ROW_BLK, HEAD_DIM), F32)], input_output_aliases={6: 0},
        compiler_params=_params(dimension_semantics=("arbitrary",)),
    )(*d_gs, *tabs, dz)


def _glu(zg):
    a = zg[:, :CONV_CH].astype(F32)
    s = _sigmoid(zg[:, CONV_CH:].astype(F32))
    return a, s, a * s


def _shifted_copies(xs):
    n = xs.shape[1] - SUBLANES
    for b in range(1, SUBLANES):
        xs[b, 0:n, :] = xs[0, pl.ds(b, n), :]


def _shifted(xs, offset, r0, cs):
    a, b = divmod(offset, SUBLANES)
    return xs[b, pl.ds(SUBLANES * a + r0, CONV_ROWS), cs]


def _conv_fwd(z, cw, cb, lg, lb, comm=_NO_COMM):
    seq = z.shape[0]
    halo_per_blk = ROW_BLK // CONV_HALO

    def body(zg_ref, zh_ref, cw_ref, cb_ref, lg_ref, lb_ref, c2_ref, c4_ref, xs):
        i = pl.program_id(0)
        _, _, c1 = _glu(zg_ref[...])
        _, _, c1h = _glu(zh_ref[...])
        xs[0, 0:CONV_HALO, :] = jnp.where(i > 0, c1h, 0.0)
        xs[0, CONV_HALO:, :] = c1
        _shifted_copies(xs)
        for s in range(CONV_CH // HEAD_DIM):
            cs = slice(s * HEAD_DIM, (s + 1) * HEAD_DIM)
            taps = [cw_ref[j:j + 1, cs] for j in range(CONV_K)]
            bias = cb_ref[:, cs]

            def chunk(rc, carry, cs=cs, taps=taps, bias=bias):
                r0 = pl.multiple_of(rc * CONV_ROWS, CONV_ROWS)
                acc = [jnp.zeros((CONV_ROWS, HEAD_DIM), F32)] * 2
                for j in range(CONV_K):
                    acc[j % 2] = acc[j % 2] + taps[j] * _shifted(xs, CONV_HALO - (CONV_K - 1) + j, r0, cs)
                c2_ref[pl.ds(r0, CONV_ROWS), cs] = acc[0] + acc[1] + bias
                return carry

            lax.fori_loop(0, ROW_BLK // CONV_ROWS, chunk, 0)
        c2 = c2_ref[...]
        mu = jnp.mean(c2, axis=-1, keepdims=True)
        xc = c2 - mu
        rstd = lax.rsqrt(jnp.mean(xc * xc, axis=-1, keepdims=True) + EPS)
        c3 = xc * rstd * lg_ref[...] + lb_ref[...]
        c4_ref[...] = (c3 * _sigmoid(c3)).astype(BF16)

    vec = pl.BlockSpec((1, CONV_CH), lambda i: (0, 0))
    return _pcall(
        body, "conv_fwd", (seq // ROW_BLK,), [z, z, cw, cb, lg, lb],
        [pl.BlockSpec((ROW_BLK, 2 * CONV_CH), lambda i: (i, GLU_COL_BLK)),
         pl.BlockSpec((CONV_HALO, 2 * CONV_CH), lambda i: (jnp.maximum(i * halo_per_blk - 1, 0), GLU_COL_BLK)),
         pl.BlockSpec((CONV_HALO, CONV_CH), lambda i: (0, 0)), vec, vec, vec],
        [jax.ShapeDtypeStruct((seq, CONV_CH), F32), jax.ShapeDtypeStruct((seq, CONV_CH), BF16)],
        [pl.BlockSpec((ROW_BLK, CONV_CH), lambda i: (i, 0)), pl.BlockSpec((ROW_BLK, CONV_CH), lambda i: (i, 0))],
        [pltpu.VMEM((SUBLANES, ROW_BLK + CONV_HALO, CONV_CH), F32)], comm=comm)


def _conv_bwd(dc2, z, cw, dz, comm=_NO_COMM):
    seq = z.shape[0]
    halo_per_blk = ROW_BLK // CONV_HALO
    n_blk = seq // ROW_BLK
    last_halo = seq // CONV_HALO - 1

    def body(dc_ref, dn_ref, zg_ref, zh_ref, cw_ref, dz_any, o_ref, dcw_ref, xs, ys, dc1_ref, dcw_acc):
        del dz_any
        i = pl.program_id(0)
        a, s, c1 = _glu(zg_ref[...])
        _, _, c1h = _glu(zh_ref[...])
        xs[0, 0:CONV_HALO, :] = jnp.where(i > 0, c1h, 0.0)
        xs[0, CONV_HALO:, :] = c1
        ys[0, 0:ROW_BLK, :] = dc_ref[...]
        ys[0, ROW_BLK:, :] = jnp.where(i < n_blk - 1, dn_ref[...], 0.0)
        _shifted_copies(xs)
        _shifted_copies(ys)

        @pl.when(i == 0)
        def _():
            dcw_acc[...] = jnp.zeros_like(dcw_acc)

        for sl in range(CONV_CH // HEAD_DIM):
            cs = slice(sl * HEAD_DIM, (sl + 1) * HEAD_DIM)
            taps = [cw_ref[j:j + 1, cs] for j in range(CONV_K)]

            def chunk(rc, carry, cs=cs, taps=taps):
                r0 = pl.multiple_of(rc * CONV_ROWS, CONV_ROWS)
                dc = ys[0, pl.ds(r0, CONV_ROWS), cs]
                acc = jnp.zeros((CONV_ROWS, HEAD_DIM), F32)
                for j in range(CONV_K):
                    prod = dc * _shifted(xs, CONV_HALO - (CONV_K - 1) + j, r0, cs)
                    dcw_acc[j, :, cs] += jnp.sum(prod.reshape(CONV_ROWS // SUBLANES, SUBLANES, HEAD_DIM), axis=0)
                    acc = acc + taps[j] * _shifted(ys, CONV_K - 1 - j, r0, cs)
                dc1_ref[pl.ds(r0, CONV_ROWS), cs] = acc
                return carry

            lax.fori_loop(0, ROW_BLK // CONV_ROWS, chunk, 0)
        dc1 = dc1_ref[...]
        o_ref[:, :CONV_CH] = (dc1 * s).astype(BF16)
        o_ref[:, CONV_CH:] = (dc1 * a * s * (1.0 - s)).astype(BF16)

        @pl.when(i == n_blk - 1)
        def _():
            dcw_ref[...] = jnp.sum(dcw_acc[...], axis=1)

    return _pcall(
        body, "conv_bwd", (n_blk,), [dc2, dc2, z, z, cw, dz],
        [pl.BlockSpec((ROW_BLK, CONV_CH), lambda i: (i, 0)),
         pl.BlockSpec((CONV_HALO, CONV_CH), lambda i: (jnp.minimum((i + 1) * halo_per_blk, last_halo), 0)),
         pl.BlockSpec((ROW_BLK, 2 * CONV_CH), lambda i: (i, GLU_COL_BLK)),
         pl.BlockSpec((CONV_HALO, 2 * CONV_CH), lambda i: (jnp.maximum(i * halo_per_blk - 1, 0), GLU_COL_BLK)),
         pl.BlockSpec((CONV_HALO, CONV_CH), lambda i: (0, 0)),
         pl.BlockSpec(memory_space=pl.ANY)],
        [jax.ShapeDtypeStruct(dz.shape, BF16), jax.ShapeDtypeStruct((CONV_HALO, CONV_CH), F32)],
        [pl.BlockSpec((ROW_BLK, 2 * CONV_CH), lambda i: (i, GLU_COL_BLK)), pl.BlockSpec((CONV_HALO, CONV_CH), lambda i: (0, 0))],
        [pltpu.VMEM((SUBLANES, ROW_BLK + CONV_HALO, CONV_CH), F32), pltpu.VMEM((SUBLANES, ROW_BLK + CONV_HALO, CONV_CH), F32),
         pltpu.VMEM((ROW_BLK, CONV_CH), F32), pltpu.VMEM((CONV_HALO, SUBLANES, CONV_CH), F32)],
        aliases={5: 0}, comm=comm)


def _epi_mix(ya, c4, wcp, gates, bg):
    yc = _dot(c4, wcp, "nn")
    gv = _sigmoid(gates.astype(F32) + bg)
    merged = gv[:, :D_MODEL] * ya + gv[:, D_MODEL:] * yc
    return merged, ya, yc


def _epi_residual_rms(acc, xres, g):
    x = xres + acc
    return x, _rms_fwd_vals(x, g)


def _cross_scores(cq, ck):
    out = []
    for h in range(CROSS_HEADS):
        cs = slice(h * CROSS_HD, (h + 1) * CROSS_HD)
        s = _dot(cq[:, cs], ck[:, cs], "nt") * (CROSS_HD ** -0.5)
        e = jnp.exp(s - jnp.max(s, axis=-1, keepdims=True))
        out.append((cs, e, jnp.sum(e, axis=-1, keepdims=True)))
    return out


def _epi_cross_fwd(acc, ck, cv):
    cq = acc.astype(BF16)
    co = [_dot(e, cv[:, cs], "nn") / den for cs, e, den in _cross_scores(cq, ck)]
    return cq, jnp.concatenate(co, axis=1)


def _epi_cross_bwd(dco, cq, ck, cv):
    dco = dco.astype(BF16)
    dcq, dck, dcv = [], [], []
    for cs, e, den in _cross_scores(cq, ck):
        p = e / den
        dp = _dot(dco[:, cs], cv[:, cs], "nt")
        ds = (p * (dp - jnp.sum(dp * p, axis=-1, keepdims=True)) * (CROSS_HD ** -0.5)).astype(BF16)
        dcq.append(_dot(ds, ck[:, cs], "nn"))
        dck.append(_dot(ds, cq[:, cs], "tn"))
        dcv.append(_dot(p, dco[:, cs], "tn"))
    return jnp.concatenate(dcq, axis=1), jnp.concatenate(dck, axis=1), jnp.concatenate(dcv, axis=1)


def _epi_mlp_up(acc):
    return acc, jnp.square(jnp.maximum(acc, 0.0))


def _epi_final(acc, x2, tgt, g):
    x3 = x2 + acc
    err = _rms_fwd_vals(x3, g) - tgt
    loss = (0.5 / D_MODEL) * jnp.sum(err * err)
    dx3, dg = _rms_bwd_vals(x3, g, err * (1.0 / D_MODEL))
    return dx3, jnp.full((1, HEAD_DIM), loss, F32), dg


def _epi_mlp_down_bwd(dh, hpre):
    return (dh * 2.0 * jnp.maximum(hpre.astype(F32), 0.0),)


def _epi_rms_bwd(du, x, g, dres):
    dx, dg = _rms_bwd_vals(x, g, du)
    return dres.astype(F32) + dx, dg


def _epi_rms_bwd_g(du, x, g):
    return (_rms_bwd_vals(x, g, du)[1],)


def _epi_mix_bwd(dm, ya, yc, gates, bg):
    gv = _sigmoid(gates.astype(F32) + bg)
    ga, gb = gv[:, :D_MODEL], gv[:, D_MODEL:]
    ya, yc = ya.astype(F32), yc.astype(F32)
    dgate = jnp.concatenate([dm * ya * ga * (1.0 - ga), dm * yc * gb * (1.0 - gb)], axis=1)
    return dm * ga, dm * gb, dgate, jnp.sum(dgate, axis=0, keepdims=True)


def _epi_ln_bwd(dc4, c2, lg, lb):
    mu = jnp.mean(c2, axis=-1, keepdims=True)
    xc = c2 - mu
    rstd = lax.rsqrt(jnp.mean(xc * xc, axis=-1, keepdims=True) + EPS)
    xh = xc * rstd
    c3 = xh * lg + lb
    sg = _sigmoid(c3)
    dc3 = dc4 * sg * (1.0 + c3 * (1.0 - sg))
    dxh = dc3 * lg
    dc2 = rstd * (dxh - jnp.mean(dxh, axis=-1, keepdims=True) - xh * jnp.mean(dxh * xh, axis=-1, keepdims=True))
    return (dc2, jnp.sum(dc3 * xh, axis=0, keepdims=True), jnp.sum(dc3, axis=0, keepdims=True),
            jnp.sum(dc2, axis=0, keepdims=True))


def _sds(shape, dtype):
    return jax.ShapeDtypeStruct(shape, dtype)


class _Lazy:
    def __init__(self, fn):
        self.fn = fn

    def __getitem__(self, key):
        return self.fn(key)


def _local_step(x, mem, tgt, sm, plan):
    w = _Lazy(plan.w)
    dw = {}

    def carry(name, n_own, fn, *args, **kw):
        c = plan.comm(name, dw)
        res = fn(*args, comm=c, **kw)
        plan.done(name, res[n_own:])
        return res[:n_own]

    def mm(name, *args, **kw):
        return carry(name, len(args[6]), _mm, name, *args, **kw)

    seq = x.shape[0]
    nr = seq // ROW_BLK
    big = min(1024, seq)
    nb = seq // big
    row = lambda n: ((ROW_BLK, n), lambda i, j, k: (i, 0))
    vec = lambda n: ((1, n), lambda i, j, k: (0, 0))
    full = lambda r, c: ((r, c), lambda i, j, k: (0, 0))
    gates_blk = ((ROW_BLK, 2 * D_MODEL), lambda i, j, k: (i, GATE_COL_BLK))
    tabs = _rope_tables(seq)

    u = carry("rms_mix", 1, _rms_fwd, "rms_mix", x, sm["g_mix"], ROW_BLK)[0]
    whole3 = lambda a: (a.shape, lambda i, j, k: (0, 0, 0))
    z = plan.project_in(u)
    a_gs = carry("qkv_prep", 3, _qkv_prep, z, tabs)
    os_, ls_ = [], []
    for g in range(N_GROUPS):
        name = "attn_fwd_%d" % g
        o_g, l_g = carry(name, 2, _attn_fwd, name, a_gs[g])
        os_.append(o_g)
        ls_.append(l_g)
    attn, lt = _attn_merge(os_, ls_, seq)
    c2, c4 = carry("conv_fwd", 2, _conv_fwd, z, w["taps"], sm["conv_b"], sm["conv_ln_g"], sm["conv_ln_b"])
    merged, ya, yc = mm(
        "mix", attn, w["w_attn_proj"], "nn", (nr, 1, 1), row(GROUP_W), full(GROUP_W, D_MODEL),
        [(_sds((seq, D_MODEL), BF16), *row(D_MODEL))] * 3,
        extras=[(c4, *row(CONV_CH)), (w["w_conv_proj"], *full(CONV_CH, D_MODEL)), (z, *gates_blk), (sm["b_gate"], *vec(2 * D_MODEL))],
        epi=_epi_mix)
    x1, uq = mm("out_proj", merged, w["w_out"], "nn", (nr, 1, 1), row(D_MODEL), full(D_MODEL, D_MODEL),
                 [(_sds((seq, D_MODEL), F32), *row(D_MODEL)), (_sds((seq, D_MODEL), BF16), *row(D_MODEL))],
                 extras=[(x, *row(D_MODEL)), (sm["g_cross"], *vec(D_MODEL))], epi=_epi_residual_rms)

    mn = _rms_fwd("rms_mem", mem, sm["g_mem"], N_MEM)[0]
    ckv = mm("ckv_proj", mn, w["w_ckv"], "nn", (1, N_DEV, 1), full(N_MEM, D_MODEL),
              ((None, D_MODEL, 2 * D_MODEL // N_DEV), lambda i, j, k: (j, 0, 0)),
              [(_sds((N_MEM, 2 * D_MODEL), BF16), (N_MEM, 2 * D_MODEL // N_DEV), lambda i, j, k: (0, j))])[0]
    ck, cv = ckv[:, :D_MODEL], ckv[:, D_MODEL:]
    kv_blk = full(N_MEM, D_MODEL)
    cq, co = mm("cq_proj_cross", uq, w["w_cq"], "nn", (nr, 1, 1), row(D_MODEL), full(D_MODEL, D_MODEL),
                 [(_sds((seq, D_MODEL), BF16), *row(D_MODEL))] * 2,
                 extras=[(ck, *kv_blk), (cv, *kv_blk)], epi=_epi_cross_fwd)
    x2, um = mm("co_proj", co, w["w_co"], "nn", (nr, 1, 1), row(D_MODEL), full(D_MODEL, D_MODEL),
                 [(_sds((seq, D_MODEL), F32), *row(D_MODEL)), (_sds((seq, D_MODEL), BF16), *row(D_MODEL))],
                 extras=[(x1, *row(D_MODEL)), (sm["g_mlp"], *vec(D_MODEL))], epi=_epi_residual_rms)

    ff_blk = D_FF // N_DEV
    row_f32 = (_sds((seq, D_MODEL), F32), *row(D_MODEL))
    row_bf16 = (_sds((seq, D_MODEL), BF16), *row(D_MODEL))
    col_sum = (_sds((1, D_MODEL), F32), *vec(D_MODEL))
    hpre, h = mm("mlp_up", um, w["w_up"], "nn", (nr, 1, 1), row(D_MODEL), whole3(w["w_up"]),
                 [(_sds((seq, D_FF), BF16), *row(D_FF))] * 2, epi=_epi_mlp_up, split=("cols", N_DEV), b_resident=True)
    kt = D_FF // D_MODEL
    dx3, loss, dg_final = mm(
        "mlp_down_loss", h, w["w_down"], "nn", (nr, 1, 1), row(D_FF), full(D_FF, D_MODEL),
        [row_bf16, (_sds((1, HEAD_DIM), F32), *vec(HEAD_DIM)), col_sum],
        extras=[(x2, *row(D_MODEL)), (tgt, *row(D_MODEL)), (sm["g_final"], *vec(D_MODEL))], epi=_epi_final, acc_outs=(1, 2),
        b_resident=True)

    dhpre = mm("mlp_down_bwd", dx3, w["w_down"], "nt", (nr, 1, 1), row(D_MODEL), full(D_FF, D_MODEL),
               [(_sds((seq, D_FF), BF16), *row(D_FF))], extras=[(hpre, *row(D_FF))], epi=_epi_mlp_down_bwd,
               split=("cols", kt), b_resident=True)[0]
    big2 = min(2 * big, seq)
    nb2 = seq // big2
    dw["w_down"] = mm("dw_down", h, dx3, "tn", (kt, 1, nb2), ((big2, D_MODEL), lambda i, j, k: (k, i)),
                      ((big2, D_MODEL), lambda i, j, k: (k, 0)),
                      [(_sds((D_FF, D_MODEL), BF16), (D_MODEL, D_MODEL), lambda i, j, k: (i, 0))])[0]
    dx2, dg_mlp = mm("mlp_up_bwd", dhpre, w["w_up"], "nt", (nr, 1, 1), row(D_FF), whole3(w["w_up"]),
                     [row_bf16, col_sum],
                     extras=[(x2, *row(D_MODEL)), (sm["g_mlp"], *vec(D_MODEL)), (dx3, *row(D_MODEL))],
                     epi=_epi_rms_bwd, acc_outs=(1,), split=("sum", N_DEV), b_resident=True)
    dw["w_up"] = mm("dw_up", um, dhpre, "tn", (1, N_DEV, nb2), ((big2, D_MODEL), lambda i, j, k: (k, 0)),
                    ((big2, ff_blk), lambda i, j, k: (k, j)),
                    [(_sds((N_DEV, D_MODEL, ff_blk), BF16), (None, D_MODEL, ff_blk), lambda i, j, k: (j, 0, 0))])[0]

    acc_kv = (_sds((N_MEM, D_MODEL), F32), *kv_blk)
    dcq, dck, dcv = mm("co_proj_bwd_cross", dx2, w["w_co"], "nt", (nr, 1, 1), row(D_MODEL), full(D_MODEL, D_MODEL),
                       [row_bf16, acc_kv, acc_kv],
                       extras=[(cq, *row(D_MODEL)), (ck, *kv_blk), (cv, *kv_blk)], epi=_epi_cross_bwd, acc_outs=(1, 2))

    def dw_square(name, act, grad):
        return mm(name, act, grad, "tn", (1, 1, nb2), ((big2, D_MODEL), lambda i, j, k: (k, 0)),
                  ((big2, D_MODEL), lambda i, j, k: (k, 0)), [(_sds((D_MODEL, D_MODEL), BF16), *full(D_MODEL, D_MODEL))])[0]

    dw["w_co"] = dw_square("dw_co", co, dx2)
    dx1, dg_cross = mm("cq_proj_bwd", dcq, w["w_cq"], "nt", (nr, 1, 1), row(D_MODEL), full(D_MODEL, D_MODEL),
                       [row_bf16, col_sum],
                       extras=[(x1, *row(D_MODEL)), (sm["g_cross"], *vec(D_MODEL)), (dx2, *row(D_MODEL))],
                       epi=_epi_rms_bwd, acc_outs=(1,))
    dw["w_cq"] = dw_square("dw_cq", uq, dcq)
    dckv = jnp.concatenate([dck, dcv], axis=1)
    kv_chunk = 2 * D_MODEL // N_DEV
    dw["w_ckv"] = mm("dw_ckv", mn, dckv, "tn", (1, N_DEV, 1), full(N_MEM, D_MODEL), ((N_MEM, kv_chunk), lambda i, j, k: (0, j)),
                      [(_sds((N_DEV, D_MODEL, kv_chunk), BF16), (None, D_MODEL, kv_chunk), lambda i, j, k: (j, 0, 0))])[0]
    dg_mem = mm("ckv_proj_bwd", dckv, w["w_ckv"], "nt", (1, 1, N_DEV), ((N_MEM, kv_chunk), lambda i, j, k: (0, k)),
                 ((None, D_MODEL, kv_chunk), lambda i, j, k: (k, 0, 0)), [(_sds((1, D_MODEL), F32), *vec(D_MODEL))],
                 extras=[(mem, *full(N_MEM, D_MODEL)), (sm["g_mem"], *vec(D_MODEL))], epi=_epi_rms_bwd_g, acc_outs=(0,))[0]

    dya, dyc, dz, db_gate = mm(
        "out_proj_bwd_mix", dx1, w["w_out"], "nt", (nr, 1, 1), row(D_MODEL), full(D_MODEL, D_MODEL),
        [(_sds((seq, D_MODEL), BF16), *row(D_MODEL)), (_sds((seq, D_MODEL), BF16), *row(D_MODEL)),
         (_sds((seq, IN_W), BF16), *gates_blk), (_sds((1, 2 * D_MODEL), F32), *vec(2 * D_MODEL))],
        extras=[(ya, *row(D_MODEL)), (yc, *row(D_MODEL)), (z, *gates_blk), (sm["b_gate"], *vec(2 * D_MODEL))],
        epi=_epi_mix_bwd, acc_outs=(3,))
    dw["w_out"] = dw_square("dw_out", merged, dx1)
    dattn = mm("attn_proj_bwd", dya, w["w_attn_proj"], "nt", (nr, 1, 1), row(D_MODEL), full(GROUP_W, D_MODEL),
                [(_sds((seq, GROUP_W), BF16), *row(GROUP_W))])[0]
    pc = D_MODEL // N_DEV
    dw["w_attn_proj"] = mm("dw_attn_proj", attn, dya, "tn", (1, 1, nb2), ((big2, GROUP_W), lambda i, j, k: (k, 0)),
                           ((big2, D_MODEL), lambda i, j, k: (k, 0)),
                           [(_sds((N_DEV, GROUP_W, pc), BF16), (N_DEV, GROUP_W, pc), lambda i, j, k: (0, 0, 0))],
                           out_chunks=N_DEV)[0]
    cvec = (_sds((1, CONV_CH), F32), *vec(CONV_CH))
    dc2, dg_ln_g, dg_ln_b, dg_conv_b = mm(
        "conv_proj_bwd_ln", dyc, w["w_conv_proj"], "nt", (nr, 1, 1), row(D_MODEL), full(CONV_CH, D_MODEL),
        [(_sds((seq, CONV_CH), F32), *row(CONV_CH)), cvec, cvec, cvec],
        extras=[(c2, *row(CONV_CH)), (sm["conv_ln_g"], *vec(CONV_CH)), (sm["conv_ln_b"], *vec(CONV_CH))],
        epi=_epi_ln_bwd, acc_outs=(1, 2, 3))
    dw["w_conv_proj"] = mm("dw_conv_proj", c4, dyc, "tn", (1, 1, nb2), ((big2, CONV_CH), lambda i, j, k: (k, 0)),
                           ((big2, D_MODEL), lambda i, j, k: (k, 0)),
                           [(_sds((N_DEV, CONV_CH, pc), BF16), (N_DEV, CONV_CH, pc), lambda i, j, k: (0, 0, 0))],
                           out_chunks=N_DEV)[0]
    dz, dg_conv_w = carry("conv_bwd", 2, _conv_bwd, dc2, z, w["taps"], dz)
    preps = _attn_bwd_prep(dattn, attn, lt)
    d_gs = []
    for g in range(N_GROUPS):
        name = "attn_bwd_%d" % g
        d_gs.append(carry(name, 1, _attn_bwd, name, a_gs[g], *preps[g])[0])
    dz = _dqkv_post(d_gs, tabs, dz)
    dw["w_in"] = mm("dw_in", u, dz, "tn", (1, N_DEV, nb2), ((big2, D_MODEL), lambda i, j, k: (k, 0)),
                    ((big2, D_MODEL), lambda i, j, k: (k, j)),
                    [(_sds((N_DEV, D_MODEL, D_MODEL), BF16), (None, D_MODEL, D_MODEL), lambda i, j, k: (j, 0, 0))])[0]
    grad_x, dg_mix = mm("in_proj_bwd", dz, w["w_in"], "nt", (nr, 1, 1), row(IN_W), whole3(w["w_in"]), [row_f32, col_sum],
                        extras=[(x, *row(D_MODEL)), (sm["g_mix"], *vec(D_MODEL)), (dx1, *row(D_MODEL))],
                        epi=_epi_rms_bwd, acc_outs=(1,), split=("sum", N_DEV), b_resident=True)
    small = dict(g_mix=dg_mix, b_gate=db_gate, conv_b=dg_conv_b, conv_ln_g=dg_ln_g, conv_ln_b=dg_ln_b, g_cross=dg_cross,
                 g_mem=dg_mem, g_mlp=dg_mlp, g_final=dg_final, loss=loss, conv_w=dg_conv_w)
    return grad_x, dw, small


SHARD_SHAPE = dict(w_in=(1024, 1024), w_attn_proj=(512, 128), w_conv_proj=(768, 128), w_out=(128, 1024), w_cq=(128, 1024),
                   w_ckv=(1024, 256), w_co=(128, 1024), w_up=(1024, 512), w_down=(512, 1024))
FWD_CARRY = {"in_proj":("w_attn_proj", "w_conv_proj", "w_out", "w_cq", "w_ckv", "w_co", "taps"),
             "qkv_prep": ("w_up",), "conv_fwd": ("w_down",)}
BWD_CARRY = {"dw_up": ("w_down",), "out_proj_bwd_mix": ("w_co", "w_cq"), "conv_bwd": ("w_up", "w_ckv"),
             "attn_bwd_0": ("w_out",), "attn_bwd_1": ("w_attn_proj", "w_conv_proj"), "in_proj_bwd": ("w_in",)}


def _cols_to_2d(a):
    return a.transpose(1, 0, 2).reshape(a.shape[1], -1)


def _in_proj_gather(u, w_shard, comm):
    seq = u.shape[0]
    tm = min(1024, seq)
    x, y, c = lax.axis_index("x"), lax.axis_index("y"), lax.axis_index("c")
    ident = lambda px, py, pc: 4 * px + 2 * py + pc
    far = [(1 - x, y), (x, 1 - y), (1 - x, 1 - y)]
    order = jnp.stack([ident(x, y, c), ident(x, y, 1 - c)] + [ident(px, py, c) for px, py in far]
                      + [ident(px, py, 1 - c) for px, py in far]).astype(jnp.int32)
    n_far = len(far)

    def body(order_ref, u_ref, wsh_ref, *rest):
        c_in, z_ref, wg_ref = rest[:comm.n], rest[comm.n], rest[comm.n + 1]
        c_out = rest[comm.n + 2:2 * comm.n + 2]
        wbuf, load_sem, local_sem, recv_sems, ici_send, d2d_send = rest[2 * comm.n + 2:2 * comm.n + 8]
        sems = rest[2 * comm.n + 8:]
        jj, i = pl.program_id(0), pl.program_id(1)
        (kx, ky, kc), me, chips = _Comm._where()
        sibling = (kx, ky, 1 - kc)
        n = order_ref[jj]

        def push(src, blk, send, to):
            return pltpu.make_async_remote_copy(src_ref=src, dst_ref=wg_ref.at[blk], send_sem=send,
                                                recv_sem=recv_sems.at[blk], device_id=to, device_id_type=MESH)

        def load(src):
            cp = pltpu.make_async_copy(src, wbuf, load_sem)
            cp.start()
            cp.wait()

        @pl.when(jnp.logical_and(jj == 0, i == 0))
        def _():
            push(wsh_ref, me, d2d_send, sibling).start()
            for (px, py) in chips:
                push(wsh_ref, me, ici_send, (px, py, kc)).start()
            pltpu.make_async_copy(wsh_ref, wg_ref.at[me], local_sem).start()
            if comm.n:
                comm.start(c_in, c_out, sems)
            load(wsh_ref)

        @pl.when(jnp.logical_and(jj > 0, i == 0))
        def _():
            push(wg_ref.at[n], n, d2d_send, sibling).wait_recv()
            for idx, (px, py) in enumerate(chips):
                @pl.when(jj == 2 + idx)
                def _():
                    blk = 4 * px + 2 * py + kc
                    push(wg_ref.at[blk], blk, d2d_send, sibling).start()

            load(wg_ref.at[n])

        z_ref[...] = _dot(u_ref[...], wbuf[...], "nn").astype(BF16)

        @pl.when(jnp.logical_and(jj == N_DEV - 1, i == pl.num_programs(1) - 1))
        def _():
            def drain_sends(send, count):
                blocks = wg_ref.at[pl.ds(0, count)]
                pltpu.make_async_remote_copy(src_ref=blocks, dst_ref=blocks, send_sem=send, recv_sem=recv_sems.at[0],
                                             device_id=sibling, device_id_type=MESH).wait_send()

            drain_sends(ici_send, n_far)
            drain_sends(d2d_send, n_far + 1)
            pltpu.make_async_copy(wsh_ref, wg_ref.at[me], local_sem).wait()
            if comm.n:
                comm.wait(c_in, c_out, sems)

    any_spec = pl.BlockSpec(memory_space=pl.ANY)
    grid_spec = pltpu.PrefetchScalarGridSpec(
        num_scalar_prefetch=1, grid=(N_DEV, seq // tm),
        in_specs=[pl.BlockSpec((tm, D_MODEL), lambda jj, i, order_ref: (i, 0)), any_spec] + comm.in_specs,
        out_specs=[pl.BlockSpec((tm, D_MODEL), lambda jj, i, order_ref: (i, order_ref[jj])), any_spec] + comm.out_specs,
        scratch_shapes=[pltpu.VMEM((D_MODEL, D_MODEL), BF16), pltpu.SemaphoreType.DMA, pltpu.SemaphoreType.DMA,
                        pltpu.SemaphoreType.DMA((N_DEV,)), pltpu.SemaphoreType.DMA, pltpu.SemaphoreType.DMA] + comm.scratch)
    return pl.pallas_call(
        body, name="in_proj_gather", grid_spec=grid_spec,
        out_shape=[jax.ShapeDtypeStruct((seq, IN_W), BF16), jax.ShapeDtypeStruct((N_DEV, D_MODEL, D_MODEL), BF16)] + comm.out_shape,
        compiler_params=_params(dimension_semantics=("arbitrary", "arbitrary")),
    )(order, u, w_shard, *comm.arrays)


class _Plan:
    def __init__(self, shards, n_tap_cols):
        self.shards = shards
        self.gathered = {}
        self.parts = {}
        self.n_tap_cols = n_tap_cols

    def project_in(self, u):
        comm = self.comm("in_proj", None)
        res = _in_proj_gather(u, self.shards["w_in"], comm)
        self.gathered["w_in"] = res[1]
        self.done("in_proj", res[2:])
        return res[0]

    def comm(self, name, dw):
        if name in FWD_CARRY:
            return _Comm(replicated=[self.shards[k] for k in FWD_CARRY[name]])
        if name in BWD_CARRY:
            return _Comm(chunked=[dw[k].reshape((N_DEV,) + SHARD_SHAPE[k]) for k in BWD_CARRY[name]])
        return _NO_COMM

    def done(self, name, got):
        if name in FWD_CARRY:
            self.gathered.update(zip(FWD_CARRY[name], got))
        elif name in BWD_CARRY:
            self.parts.update(zip(BWD_CARRY[name], got))

    def w(self, key):
        g = self.gathered[key]
        if key in ("w_in", "w_up", "w_ckv"):
            return g
        if key in ("w_attn_proj", "w_conv_proj"):
            return _cols_to_2d(g)
        if key == "taps":
            return jnp.pad(_cols_to_2d(g[:, :CONV_K, :self.n_tap_cols]), ((0, 1), (0, 0)))
        return g.reshape(-1, g.shape[-1])


def _adamw(name, w, m, v, parts, comm=_NO_COMM):
    rows, cols = w.shape
    n_parts = parts.shape[0]
    rb = rows if rows <= 256 or rows % 256 else 256

    def body(w_ref, m_ref, v_ref, p_ref, g_ref, d_ref, nm_ref, nv_ref):
        g = p_ref[0].astype(F32)
        for q in range(1, n_parts):
            g = g + p_ref[q].astype(F32)
        wv = w_ref[...]
        nm = ADAM_B1 * m_ref[...] + (1.0 - ADAM_B1) * g
        nv = ADAM_B2 * v_ref[...] + (1.0 - ADAM_B2) * jnp.square(g)
        m_hat = nm / (1.0 - ADAM_B1 ** ADAM_STEP)
        v_hat = nv / (1.0 - ADAM_B2 ** ADAM_STEP)
        g_ref[...] = g
        d_ref[...] = -ADAM_LR * (m_hat / (jnp.sqrt(v_hat) + ADAM_EPS) + ADAM_WD * wv)
        nm_ref[...] = nm
        nv_ref[...] = nv

    blk = pl.BlockSpec((rb, cols), lambda i: (i, 0))
    return _pcall(body, name, (rows // rb,), [w, m, v, parts],
                  [blk, blk, blk, pl.BlockSpec((n_parts, rb, cols), lambda i: (0, i, 0))],
                  [jax.ShapeDtypeStruct((rows, cols), F32)] * 4, [blk] * 4, comm=comm)


def _sum_parts(name, parts):
    def body(p_ref, o_ref):
        acc = p_ref[0]
        for q in range(1, parts.shape[0]):
            acc = acc + p_ref[q]
        o_ref[...] = acc

    return _pcall(body, name, (1,), [parts], [pl.BlockSpec(parts.shape, lambda i: (0, 0, 0))],
                  [jax.ShapeDtypeStruct(parts.shape[1:], F32)], [pl.BlockSpec(parts.shape[1:], lambda i: (0, 0))])[0]


BIG = ("w_in", "w_attn_proj", "w_conv_proj", "w_out", "w_cq", "w_ckv", "w_co", "w_up", "w_down")
SMALL = ("g_mix", "b_gate", "conv_b", "conv_ln_g", "conv_ln_b", "g_cross", "g_mem", "g_mlp", "g_final")
SMALL_ORDER = SMALL + ("loss", "conv_w")
WEIGHTS = ("g_mix", "w_in", "b_gate", "conv_w", "conv_b", "conv_ln_g", "conv_ln_b", "w_attn_proj", "w_conv_proj", "w_out",
           "g_cross", "g_mem", "w_cq", "w_ckv", "w_co", "g_mlp", "w_up", "w_down", "g_final")


def kernel(x, mem, g_mix, w_in, b_gate, conv_w, conv_b, conv_ln_g, conv_ln_b, w_attn_proj, w_conv_proj, w_out, g_cross, g_mem, w_cq, w_ckv, w_co, g_mlp, w_up, w_down, g_final, loss_target, m_g_mix, m_w_in, m_b_gate, m_conv_w, m_conv_b, m_conv_ln_g, m_conv_ln_b, m_w_attn_proj, m_w_conv_proj, m_w_out, m_g_cross, m_g_mem, m_w_cq, m_w_ckv, m_w_co, m_g_mlp, m_w_up, m_w_down, m_g_final, v_g_mix, v_w_in, v_b_gate, v_conv_w, v_conv_b, v_conv_ln_g, v_conv_ln_b, v_w_attn_proj, v_w_conv_proj, v_w_out, v_g_cross, v_g_mem, v_w_cq, v_w_ckv, v_w_co, v_g_mlp, v_w_up, v_w_down, v_g_final):
    args = dict(locals())
    wts = {k: args[k] for k in WEIGHTS}
    mom = {k: args["m_" + k] for k in WEIGHTS}
    var = {k: args["v_" + k] for k in WEIGHTS}
    two_d = lambda a: a.reshape(a.shape[-2:]) if a.ndim == 3 else a.reshape(1, -1)

    shards = {k: two_d(wts[k]).astype(BF16) for k in BIG}
    shards["taps"] = jnp.pad(two_d(conv_w), ((0, 1), (0, HEAD_DIM - conv_w.shape[-1])))
    plan = _Plan(shards, conv_w.shape[-1])
    sm = {k: two_d(wts[k]) for k in SMALL}

    grad_x, _, small = _local_step(x[0], mem[0], loss_target[0], sm, plan)
    parts = plan.parts

    out = {}
    small_comm = _Comm(replicated=[small[k] for k in SMALL_ORDER])
    for k in BIG:
        res = _adamw("adamw_" + k, two_d(wts[k]), two_d(mom[k]), two_d(var[k]), parts[k],
                     comm=small_comm if k == BIG[0] else _NO_COMM)
        out[k] = [r.reshape(wts[k].shape) for r in res[:4]]
        if k == BIG[0]:
            small_parts = dict(zip(SMALL_ORDER, res[4:]))
    for k in SMALL:
        res = _adamw("adamw_" + k, two_d(wts[k]), two_d(mom[k]), two_d(var[k]), small_parts[k])
        out[k] = [r.reshape(wts[k].shape) for r in res]
    loss = _sum_parts("loss_sum", small_parts["loss"])[0, 0]
    me = 4 * lax.axis_index("x") + 2 * lax.axis_index("y") + lax.axis_index("c")
    n_tap_cols = conv_w.shape[-1]
    tap_parts = lax.dynamic_slice(small_parts["conv_w"], (0, 0, me * n_tap_cols), (N_DEV, CONV_K, n_tap_cols))
    res = _adamw("adamw_conv_w", two_d(conv_w), two_d(m_conv_w), two_d(v_conv_w), tap_parts)
    out["conv_w"] = [r.reshape(conv_w.shape) for r in res]

    return (loss, grad_x[None], *[out[k][0] for k in WEIGHTS], *[out[k][1] for k in WEIGHTS],
            *[out[k][2] for k in WEIGHTS], *[out[k][3] for k in WEIGHTS])
```

```python
import functools

import jax
import jax.numpy as jnp
import numpy as np
from jax import lax
from jax.experimental import pallas as pl
from jax.experimental.pallas import tpu as pltpu

F32 = jnp.float32
BF16 = jnp.bfloat16

N_DEV = 8
D_MODEL = 1024
N_MEM = 256
HEAD_DIM = 128
HEADS_PER_GROUP = 4
GROUP_W = HEADS_PER_GROUP * HEAD_DIM
DILATIONS = (1, 4, 16)
BAND = 128
N_GROUPS = 3
ATTN_W = N_GROUPS * GROUP_W
QKV_W = 3 * ATTN_W
ROT_DIM = HEAD_DIM // 4
ROPE_THETA = 500000.0
CONV_CH = 768
CONV_K = 31
CONV_HALO = 32
SUBLANES = 8
CONV_ROWS = 64
IN_W = 8192
GLU_COL_BLK = QKV_W // (2 * CONV_CH)
GATE_COL_BLK = (QKV_W + 2 * CONV_CH) // (2 * D_MODEL)
CROSS_HEADS = 4
CROSS_HD = D_MODEL // CROSS_HEADS
D_FF = 4096
EPS = 1e-6
NEG = -1e30
QB = 4
ROW_BLK = QB * BAND

ADAM_LR = 0.001
ADAM_B1 = 0.9
ADAM_B2 = 0.999
ADAM_EPS = 1e-08
ADAM_WD = 0.01
ADAM_STEP = 10

VMEM_LIMIT = 56 * 1024 * 1024
MESH = pl.DeviceIdType.MESH


def _params(**kw):
    return pltpu.CompilerParams(vmem_limit_bytes=VMEM_LIMIT, **kw)


def _sigmoid(x):
    return 1.0 / (1.0 + jnp.exp(-x))


def _dot(a, b, kind):
    dims = {"nn": (((1,), (0,)), ((), ())), "nt": (((1,), (1,)), ((), ())), "tn": (((0,), (0,)), ((), ()))}[kind]
    if a.dtype != BF16:
        a = a.astype(BF16)
    if b.dtype != BF16:
        b = b.astype(BF16)
    return lax.dot_general(a, b, dims, preferred_element_type=F32)


def _peers():
    x, y, c = lax.axis_index("x"), lax.axis_index("y"), lax.axis_index("c")
    me = 4 * x + 2 * y + c
    peers = [(x, y, 1 - c), (1 - x, y, c), (x, 1 - y, c), (1 - x, 1 - y, c),
             (1 - x, y, 1 - c), (x, 1 - y, 1 - c), (1 - x, 1 - y, 1 - c)]
    return me, peers


class _Comm:
    def __init__(self, chunked=(), replicated=()):
        self.arrays = list(chunked) + list(replicated)
        self.n_c = len(chunked)
        self.n = len(self.arrays)
        self.out_shape = [jax.ShapeDtypeStruct(a.shape, a.dtype) for a in chunked]
        self.out_shape += [jax.ShapeDtypeStruct((N_DEV,) + a.shape, a.dtype) for a in replicated]
        self.in_specs = [pl.BlockSpec(memory_space=pl.ANY)] * self.n
        self.out_specs = [pl.BlockSpec(memory_space=pl.ANY)] * self.n
        self.scratch = [pltpu.SemaphoreType.DMA((self.n,))] * 5 if self.n else []

    @staticmethod
    def _where():
        x, y, c = lax.axis_index("x"), lax.axis_index("y"), lax.axis_index("c")
        chips = [(1 - x, y), (x, 1 - y), (1 - x, 1 - y)]
        return (x, y, c), 4 * x + 2 * y + c, chips

    def _local(self, ins, outs, sems, a, me):
        src = ins[a].at[me] if a < self.n_c else ins[a]
        return pltpu.make_async_copy(src, outs[a].at[me], sems[2].at[a])

    @staticmethod
    def _remote(src, dst, send, recv, to):
        return pltpu.make_async_remote_copy(src_ref=src, dst_ref=dst, send_sem=send, recv_sem=recv, device_id=to,
                                            device_id_type=MESH)

    def start(self, ins, outs, sems):
        (x, y, c), me, chips = self._where()
        for a in range(self.n):
            self._local(ins, outs, sems, a, me).start()
            if a < self.n_c:
                for (px, py, pc) in _peers()[1]:
                    self._remote(ins[a].at[4 * px + 2 * py + pc], outs[a].at[me], sems[0].at[a], sems[1].at[a], (px, py, pc)).start()
            else:
                self._remote(ins[a], outs[a].at[me], sems[3].at[a], sems[4].at[a], (x, y, 1 - c)).start()
                for (px, py) in chips:
                    self._remote(ins[a], outs[a].at[me], sems[0].at[a], sems[1].at[a], (px, py, c)).start()

    def wait(self, ins, outs, sems):
        (x, y, c), me, chips = self._where()
        sibling = (x, y, 1 - c)

        def drain(a, pair, count):
            blocks = outs[a].at[pl.ds(0, count)]
            cp = self._remote(blocks, blocks, sems[pair].at[a], sems[pair + 1].at[a], sibling)
            cp.wait_send()
            cp.wait_recv()

        for a in range(self.n):
            if a < self.n_c:
                drain(a, 0, N_DEV - 1)
            else:
                drain(a, 0, len(chips))
                for (px, py) in chips:
                    blk = outs[a].at[4 * px + 2 * py + c]
                    self._remote(blk, blk, sems[3].at[a], sems[4].at[a], sibling).start()
        for a in range(self.n):
            if a >= self.n_c:
                drain(a, 3, len(chips) + 1)
            self._local(ins, outs, sems, a, me).wait()


_NO_COMM = _Comm()


def _pcall(body, name, grid, operands, in_specs, out_shape, out_specs, scratch=(), aliases=None, comm=_NO_COMM, **params):
    n_in, n_out, n_scr = len(operands), len(out_shape), len(scratch)
    grid = tuple(grid)

    def carried(*refs):
        ins, c_in = refs[:n_in], refs[n_in:n_in + comm.n]
        o0 = n_in + comm.n
        outs, c_out = refs[o0:o0 + n_out], refs[o0 + n_out:o0 + n_out + comm.n]
        s0 = o0 + n_out + comm.n
        scr, sems = refs[s0:s0 + n_scr], refs[s0 + n_scr:]
        ids = [pl.program_id(ax) for ax in range(len(grid))]

        @pl.when(functools.reduce(jnp.logical_and, [p == 0 for p in ids]))
        def _():
            comm.start(c_in, c_out, sems)

        body(*ins, *outs, *scr)

        @pl.when(functools.reduce(jnp.logical_and, [p == g - 1 for p, g in zip(ids, grid)]))
        def _():
            comm.wait(c_in, c_out, sems)

    return pl.pallas_call(
        carried if comm.n else body, name=name, grid=grid, in_specs=list(in_specs) + comm.in_specs,
        out_shape=list(out_shape) + comm.out_shape, out_specs=list(out_specs) + comm.out_specs,
        scratch_shapes=list(scratch) + comm.scratch, input_output_aliases=aliases or {},
        compiler_params=_params(dimension_semantics=("arbitrary",) * len(grid), **params),
    )(*operands, *comm.arrays)


def _mm(name, a, b, kind, grid, a_blk, b_blk, outs, extras=(), epi=None, acc_outs=(), j_outer=False, comm=_NO_COMM,
        split=None, b_resident=False, out_chunks=0):
    gi, gj, gk = grid
    n_ex = len(extras)
    n_out = len(outs)
    mode, n_chunks = split if split is not None else (None, 1)

    def spec(blk, fn, **kw):
        return pl.BlockSpec(blk, (lambda j, i, k: fn(i, j, k)) if j_outer else fn, **kw)

    def b_chunk(b_ref, c):
        if len(b_ref.shape) == 3:
            return b_ref[c]
        rows, cols = b_ref.shape
        if (kind == "nn") == (mode == "cols"):
            return b_ref[:, c * (cols // n_chunks):(c + 1) * (cols // n_chunks)]
        return b_ref[c * (rows // n_chunks):(c + 1) * (rows // n_chunks), :]

    def col_chunk(ref, c):
        width = ref.shape[-1] // n_chunks
        return slice(c * width, (c + 1) * width)

    def body(*refs):
        a_ref, b_ref = refs[0], refs[1]
        ex = refs[2:2 + n_ex]
        out_refs = refs[2 + n_ex:2 + n_ex + n_out]
        acc_ref = refs[2 + n_ex + n_out] if gk > 1 else None
        i = pl.program_id(1 if j_outer else 0)
        k = pl.program_id(2)
        if mode == "cols":
            a_val = a_ref[...]
            for c in range(n_chunks):
                acc = _dot(a_val, b_chunk(b_ref, c), kind)
                vals = epi(acc, *[e[:, col_chunk(e, c)] for e in ex]) if epi is not None else (acc,)
                for o, v in zip(out_refs, vals):
                    o[:, col_chunk(o, c)] = v.astype(o.dtype)
            return
        if mode == "sum":
            part = _dot(a_ref[:, col_chunk(a_ref, 0)], b_chunk(b_ref, 0), kind)
            for c in range(1, n_chunks):
                part = part + _dot(a_ref[:, col_chunk(a_ref, c)], b_chunk(b_ref, c), kind)
        else:
            part = _dot(a_ref[...], b_ref[...], kind)

        def finish(acc):
            if out_chunks:
                width = acc.shape[-1] // out_chunks
                for c in range(out_chunks):
                    out_refs[0][c] = acc[:, c * width:(c + 1) * width].astype(out_refs[0].dtype)
                return
            vals = epi(acc, *[e[...] for e in ex]) if epi is not None else (acc,)
            for idx, (o, v) in enumerate(zip(out_refs, vals)):
                if idx in acc_outs:
                    @pl.when(i == 0)
                    def _():
                        o[...] = v.astype(o.dtype)

                    @pl.when(i != 0)
                    def _():
                        o[...] += v.astype(o.dtype)
                else:
                    o[...] = v.astype(o.dtype)

        if gk == 1:
            finish(part)
        else:
            @pl.when(k == 0)
            def _():
                acc_ref[...] = part

            @pl.when(k != 0)
            def _():
                acc_ref[...] += part

            @pl.when(k == gk - 1)
            def _():
                finish(acc_ref[...])

    scratch = []
    if gk > 1:
        tm = a_blk[0][-1] if kind == "tn" else a_blk[0][-2]
        tn = b_blk[0][-2] if kind == "nt" else b_blk[0][-1]
        scratch = [pltpu.VMEM((tm, tn), F32)]
    b_kw = dict(pipeline_mode=pl.Buffered(1)) if b_resident else {}
    return _pcall(body, name, (gj, gi, gk) if j_outer else (gi, gj, gk), [a, b] + [e for e, _, _ in extras],
                  [spec(*a_blk), spec(*b_blk, **b_kw)] + [spec(blk, fn) for _, blk, fn in extras],
                  [s for s, _, _ in outs], [spec(blk, fn) for _, blk, fn in outs], scratch, comm=comm)


def _rms_fwd_vals(x, g):
    r = lax.rsqrt(jnp.mean(x * x, axis=-1, keepdims=True) + EPS)
    return x * r * g


def _rms_bwd_vals(x, g, du):
    r = lax.rsqrt(jnp.mean(x * x, axis=-1, keepdims=True) + EPS)
    xh = x * r
    dxh = du * g
    dx = r * (dxh - xh * jnp.mean(dxh * xh, axis=-1, keepdims=True))
    return dx, jnp.sum(du * xh, axis=0, keepdims=True)


def _rms_fwd(name, x, g, rows, comm=_NO_COMM):
    n = x.shape[0]

    def body(x_ref, g_ref, o_ref):
        o_ref[...] = _rms_fwd_vals(x_ref[...], g_ref[...]).astype(BF16)

    return _pcall(body, name, (n // rows,), [x, g],
                  [pl.BlockSpec((rows, D_MODEL), lambda i: (i, 0)), pl.BlockSpec((1, D_MODEL), lambda i: (0, 0))],
                  [jax.ShapeDtypeStruct(x.shape, BF16)], [pl.BlockSpec((rows, D_MODEL), lambda i: (i, 0))], comm=comm)


def _rope_tables(seq):
    half = ROT_DIM // 2
    pos = np.arange(seq, dtype=np.float32)
    inv_freq = np.float32(ROPE_THETA) ** (-np.arange(0, ROT_DIM, 2, dtype=np.float32) / np.float32(ROT_DIM))
    ang = (pos[:, None] * inv_freq[None, :]).astype(np.float32)
    cos, sin = np.cos(ang), np.sin(ang)
    rest = HEAD_DIM - ROT_DIM
    c = np.concatenate([cos, cos, np.ones((seq, rest), np.float32)], axis=1)
    s1 = np.concatenate([np.zeros((seq, half), np.float32), sin, np.zeros((seq, rest), np.float32)], axis=1)
    s2 = np.concatenate([-sin, np.zeros((seq, half + rest), np.float32)], axis=1)
    return jnp.asarray(c), jnp.asarray(s1), jnp.asarray(s2)


def _group_shapes(seq, width, dtype):
    return [jax.ShapeDtypeStruct((d, seq // d, width), dtype) for d in DILATIONS]


def _group_specs(width):
    return [pl.BlockSpec((d, ROW_BLK // d, width), lambda i: (0, i, 0)) for d in DILATIONS]


def _qkv_prep(z, tabs, comm=_NO_COMM):
    seq = z.shape[0]

    def body(z_ref, c_ref, s1_ref, s2_ref, a0, a1, a2, sc):
        outs = (a0, a1, a2)
        c, s1, s2 = c_ref[...], s1_ref[...], s2_ref[...]
        for part in range(3):
            for hh in range(N_GROUPS * HEADS_PER_GROUP):
                g, hl = divmod(hh, HEADS_PER_GROUP)
                col = part * ATTN_W + hh * HEAD_DIM
                ocol = part * GROUP_W + hl * HEAD_DIM
                x = z_ref[:, col:col + HEAD_DIM].astype(F32)
                if part < 2:
                    x = x * c + pltpu.roll(x, ROT_DIM // 2, 1) * s1 + pltpu.roll(x, HEAD_DIM - ROT_DIM // 2, 1) * s2
                d = DILATIONS[g]
                if d == 1:
                    outs[g][0, :, ocol:ocol + HEAD_DIM] = x.astype(BF16)
                else:
                    sc[...] = x
                    for r in range(d):
                        outs[g][r, :, ocol:ocol + HEAD_DIM] = sc[pl.ds(r, ROW_BLK // d, stride=d), :].astype(BF16)

    tab_spec = pl.BlockSpec((ROW_BLK, HEAD_DIM), lambda i: (i, 0))
    return _pcall(body, "qkv_prep", (seq // ROW_BLK,), [z, *tabs],
                  [pl.BlockSpec((ROW_BLK, QKV_W), lambda i: (i, 0)), tab_spec, tab_spec, tab_spec],
                  _group_shapes(seq, ATTN_W, BF16), _group_specs(ATTN_W), [pltpu.VMEM((ROW_BLK, HEAD_DIM), F32)], comm=comm)


def _band_masks_2(t):
    qi = lax.broadcasted_iota(jnp.int32, (BAND, 2 * BAND), 0)
    kj = lax.broadcasted_iota(jnp.int32, (BAND, 2 * BAND), 1)
    band = jnp.logical_and(kj >= qi, kj <= qi + BAND)
    return band, jnp.logical_and(band, jnp.logical_or(kj >= BAND, t > 0))


def _attn_fwd(name, a_g, comm=_NO_COMM):
    dil, m_len, _ = a_g.shape
    qb = min(QB, m_len // BAND)
    rows = qb * BAND
    steps = m_len // rows
    scale = HEAD_DIM ** -0.5

    tiles = [(sb, h) for sb in range(qb) for h in range(HEADS_PER_GROUP)]

    def body(q_ref, kc_ref, vc_ref, kp_ref, vp_ref, o_ref, l_ref, k_all, v_all, s_scr, p_scr, r_scr):
        t = pl.program_id(1)
        k_all[0:BAND, :] = kp_ref[...]
        k_all[BAND:, :] = kc_ref[...]
        v_all[0:BAND, :] = vp_ref[...]
        v_all[BAND:, :] = vc_ref[...]
        band, band_first = _band_masks_2(t)
        for idx, (sb, h) in enumerate(tiles):
            cs = slice(h * HEAD_DIM, (h + 1) * HEAD_DIM)
            s = _dot(q_ref[sb * BAND:(sb + 1) * BAND, cs], k_all[sb * BAND:(sb + 2) * BAND, cs], "nt") * scale
            s_scr[idx] = jnp.where(band_first if sb == 0 else band, s, NEG)
        lane = lax.broadcasted_iota(jnp.int32, (BAND, HEAD_DIM), 1)
        lse_rows = [jnp.zeros((BAND, HEAD_DIM), F32)] * qb
        for idx, (sb, h) in enumerate(tiles):
            s = s_scr[idx]
            mx = jnp.max(s, axis=-1, keepdims=True)
            p = jnp.exp(s - mx)
            den = jnp.sum(p, axis=-1, keepdims=True)
            p_scr[idx] = p.astype(BF16)
            r_scr[idx] = jnp.broadcast_to(1.0 / den, (BAND, HEAD_DIM))
            lse_rows[sb] = jnp.where(lane == h, jnp.broadcast_to(mx + jnp.log(den), (BAND, HEAD_DIM)), lse_rows[sb])
        for sb in range(qb):
            l_ref[sb * BAND:(sb + 1) * BAND, :] = lse_rows[sb]
        for idx, (sb, h) in enumerate(tiles):
            cs = slice(h * HEAD_DIM, (h + 1) * HEAD_DIM)
            o_ref[sb * BAND:(sb + 1) * BAND, cs] = _dot(p_scr[idx], v_all[sb * BAND:(sb + 2) * BAND, cs], "nn") * r_scr[idx]

    def prev(r, t):
        return jnp.maximum(qb * t - 1, 0)

    cur = lambda c: pl.BlockSpec((None, rows, GROUP_W), lambda r, t, c=c: (r, t, c))
    prv = lambda c: pl.BlockSpec((None, BAND, GROUP_W), lambda r, t, c=c: (r, prev(r, t), c))
    out_spec = lambda width: pl.BlockSpec((None, rows, width), lambda r, t: (r, t, 0))
    shp = lambda width: jax.ShapeDtypeStruct((dil, m_len, width), F32)
    n_t = len(tiles)
    return _pcall(body, name, (dil, steps), [a_g] * 5, [cur(0), cur(1), cur(2), prv(1), prv(2)],
                  [shp(GROUP_W), shp(HEAD_DIM)], [out_spec(GROUP_W), out_spec(HEAD_DIM)],
                  [pltpu.VMEM((rows + BAND, GROUP_W), BF16), pltpu.VMEM((rows + BAND, GROUP_W), BF16),
                   pltpu.VMEM((n_t, BAND, 2 * BAND), F32), pltpu.VMEM((n_t, BAND, 2 * BAND), BF16),
                   pltpu.VMEM((n_t, BAND, HEAD_DIM), F32)], comm=comm)


def _attn_merge(os_, ls_, seq):
    def body(o0, l0, o1, l1, o2, l2, at_ref, lt_ref, sc, lsc):
        for gi, l_r in enumerate((l1, l2)):
            d = DILATIONS[gi + 1]
            for r in range(d):
                lsc.at[gi][pl.ds(r, ROW_BLK // d, stride=d), :] = l_r[r]
        lse = (l0.at[0], lsc.at[0], lsc.at[1])
        lane = lax.broadcasted_iota(jnp.int32, (ROW_BLK, HEAD_DIM), 1)
        lt_rows = jnp.zeros((ROW_BLK, HEAD_DIM), F32)
        for h in range(HEADS_PER_GROUP):
            cs = slice(h * HEAD_DIM, (h + 1) * HEAD_DIM)
            for gi, o_r in enumerate((o1, o2)):
                d = DILATIONS[gi + 1]
                for r in range(d):
                    sc.at[gi][pl.ds(r, ROW_BLK // d, stride=d), :] = o_r[r, :, cs]
            l_h = [v[:, h:h + 1] for v in lse]
            mx = jnp.maximum(jnp.maximum(l_h[0], l_h[1]), l_h[2])
            e = [jnp.exp(v - mx) for v in l_h]
            tot = e[0] + e[1] + e[2]
            inv = 1.0 / tot
            at_ref[:, cs] = ((e[0] * inv) * o0[0, :, cs] + (e[1] * inv) * sc[0] + (e[2] * inv) * sc[1]).astype(BF16)
            lt_rows = jnp.where(lane == h, jnp.broadcast_to(mx + jnp.log(tot), (ROW_BLK, HEAD_DIM)), lt_rows)
        lt_ref[...] = lt_rows

    go, gl = _group_specs(GROUP_W), _group_specs(HEAD_DIM)
    return pl.pallas_call(
        body, name="attn_merge",
        out_shape=[jax.ShapeDtypeStruct((seq, GROUP_W), BF16), jax.ShapeDtypeStruct((seq, HEAD_DIM), F32)],
        grid=(seq // ROW_BLK,), in_specs=[go[0], gl[0], go[1], gl[1], go[2], gl[2]],
        out_specs=[pl.BlockSpec((ROW_BLK, GROUP_W), lambda i: (i, 0)), pl.BlockSpec((ROW_BLK, HEAD_DIM), lambda i: (i, 0))],
        scratch_shapes=[pltpu.VMEM((2, ROW_BLK, HEAD_DIM), F32), pltpu.VMEM((2, ROW_BLK, HEAD_DIM), F32)],
        compiler_params=_params(dimension_semantics=("arbitrary",)),
    )(os_[0], ls_[0], os_[1], ls_[1], os_[2], ls_[2])


def _attn_bwd_prep(dattn, attn, lt):
    seq = dattn.shape[0]

    def body(da_ref, at_ref, lt_ref, cl0, d1, cl1, d2, cl2, sc, csc):
        lane = lax.broadcasted_iota(jnp.int32, (ROW_BLK, HEAD_DIM), 1)
        cl = pltpu.roll(lt_ref[...], HEADS_PER_GROUP, 1)
        for h in range(HEADS_PER_GROUP):
            cs = slice(h * HEAD_DIM, (h + 1) * HEAD_DIM)
            da = da_ref[:, cs].astype(F32)
            cc = jnp.sum(da * at_ref[:, cs].astype(F32), axis=-1, keepdims=True)
            cl = jnp.where(lane == h, jnp.broadcast_to(cc, (ROW_BLK, HEAD_DIM)), cl)
            sc[...] = da
            for g, d_ref in ((1, d1), (2, d2)):
                d = DILATIONS[g]
                for r in range(d):
                    d_ref[r, :, cs] = sc[pl.ds(r, ROW_BLK // d, stride=d), :].astype(BF16)
        cl0[0] = cl
        csc[...] = cl
        for g, c_ref in ((1, cl1), (2, cl2)):
            d = DILATIONS[g]
            for r in range(d):
                c_ref[r] = csc[pl.ds(r, ROW_BLK // d, stride=d), :]

    go, gl = _group_specs(GROUP_W), _group_specs(HEAD_DIM)
    row = lambda width: pl.BlockSpec((ROW_BLK, width), lambda i: (i, 0))
    shape = lambda g, width, dt: jax.ShapeDtypeStruct((DILATIONS[g], seq // DILATIONS[g], width), dt)
    cl0, d1, cl1, d2, cl2 = pl.pallas_call(
        body, name="attn_bwd_prep",
        out_shape=[shape(0, HEAD_DIM, F32), shape(1, GROUP_W, BF16), shape(1, HEAD_DIM, F32), shape(2, GROUP_W, BF16),
                   shape(2, HEAD_DIM, F32)],
        grid=(seq // ROW_BLK,), in_specs=[row(GROUP_W), row(GROUP_W), row(HEAD_DIM)],
        out_specs=[gl[0], go[1], gl[1], go[2], gl[2]],
        scratch_shapes=[pltpu.VMEM((ROW_BLK, HEAD_DIM), F32), pltpu.VMEM((ROW_BLK, HEAD_DIM), F32)],
        compiler_params=_params(dimension_semantics=("arbitrary",)),
    )(dattn, attn, lt)
    return [(dattn[None], cl0), (d1, cl1), (d2, cl2)]


def _attn_bwd(name, a_g, da_g, cl_g, comm=_NO_COMM):
    dil, m_len, _ = a_g.shape
    qb = min(QB, m_len // BAND)
    rows = qb * BAND
    steps = m_len // rows
    scale = HEAD_DIM ** -0.5

    tiles = [(sb, h) for sb in range(qb) for h in range(HEADS_PER_GROUP)]

    def body(q_ref, kc_ref, vc_ref, kp_ref, vp_ref, da_ref, cl_ref, d_ref, dk_acc, dv_acc, car_k, car_v,
             k_all, v_all, s_scr, dp_scr, p_scr, ds_scr):
        tg = pl.program_id(1)
        t = steps - 1 - tg

        @pl.when(tg == 0)
        def _():
            car_k[...] = jnp.zeros_like(car_k)
            car_v[...] = jnp.zeros_like(car_v)

        k_all[0:BAND, :] = kp_ref[...]
        k_all[BAND:, :] = kc_ref[...]
        v_all[0:BAND, :] = vp_ref[...]
        v_all[BAND:, :] = vc_ref[...]
        zero = jnp.zeros((rows, GROUP_W), F32)
        dk_acc[0:rows, :] = zero
        dv_acc[0:rows, :] = zero
        dk_acc[rows:rows + BAND, :] = car_k[...]
        dv_acc[rows:rows + BAND, :] = car_v[...]
        band, band_first = _band_masks_2(t)
        for idx, (sb, h) in enumerate(tiles):
            cs = slice(h * HEAD_DIM, (h + 1) * HEAD_DIM)
            rs, ks = slice(sb * BAND, (sb + 1) * BAND), slice(sb * BAND, (sb + 2) * BAND)
            s_scr[idx] = _dot(q_ref[rs, cs], k_all[ks, cs], "nt")
            dp_scr[idx] = _dot(da_ref[rs, cs], v_all[ks, cs], "nt")
        for idx, (sb, h) in enumerate(tiles):
            cs = slice(h * HEAD_DIM, (h + 1) * HEAD_DIM)
            rs = slice(sb * BAND, (sb + 1) * BAND)
            cc = jnp.broadcast_to(cl_ref[rs, h:h + 1], (BAND, 2 * BAND))
            ltv = jnp.broadcast_to(cl_ref[rs, HEADS_PER_GROUP + h:HEADS_PER_GROUP + h + 1], (BAND, 2 * BAND))
            p = jnp.exp(jnp.where(band_first if sb == 0 else band, s_scr[idx] * scale - ltv, NEG))
            p_scr[idx] = p.astype(BF16)
            ds_scr[idx] = (p * (dp_scr[idx] - cc) * scale).astype(BF16)
        for idx, (sb, h) in enumerate(tiles):
            cs = slice(h * HEAD_DIM, (h + 1) * HEAD_DIM)
            rs, ks = slice(sb * BAND, (sb + 1) * BAND), slice(sb * BAND, (sb + 2) * BAND)
            d_ref[rs, cs] = _dot(ds_scr[idx], k_all[ks, cs], "nn").astype(BF16)
            dk_acc[ks, cs] += _dot(ds_scr[idx], q_ref[rs, cs], "tn")
            dv_acc[ks, cs] += _dot(p_scr[idx], da_ref[rs, cs], "tn")
        d_ref[:, GROUP_W:2 * GROUP_W] = dk_acc[BAND:rows + BAND, :].astype(BF16)
        d_ref[:, 2 * GROUP_W:3 * GROUP_W] = dv_acc[BAND:rows + BAND, :].astype(BF16)
        car_k[...] = dk_acc[0:BAND, :]
        car_v[...] = dv_acc[0:BAND, :]

    def rev(tg):
        return steps - 1 - tg

    def prev(tg):
        return jnp.maximum(qb * rev(tg) - 1, 0)

    cur = lambda c: pl.BlockSpec((None, rows, GROUP_W), lambda r, tg, c=c: (r, rev(tg), c))
    prv = lambda c: pl.BlockSpec((None, BAND, GROUP_W), lambda r, tg, c=c: (r, prev(tg), c))
    return _pcall(
        body, name, (dil, steps), [a_g, a_g, a_g, a_g, a_g, da_g, cl_g],
        [cur(0), cur(1), cur(2), prv(1), prv(2), cur(0), pl.BlockSpec((None, rows, HEAD_DIM), lambda r, tg: (r, rev(tg), 0))],
        [jax.ShapeDtypeStruct((dil, m_len, ATTN_W), BF16)], [pl.BlockSpec((None, rows, ATTN_W), lambda r, tg: (r, rev(tg), 0))],
        [pltpu.VMEM((rows + BAND, GROUP_W), F32), pltpu.VMEM((rows + BAND, GROUP_W), F32),
         pltpu.VMEM((BAND, GROUP_W), F32), pltpu.VMEM((BAND, GROUP_W), F32),
         pltpu.VMEM((rows + BAND, GROUP_W), BF16), pltpu.VMEM((rows + BAND, GROUP_W), BF16),
         pltpu.VMEM((len(tiles), BAND, 2 * BAND), F32), pltpu.VMEM((len(tiles), BAND, 2 * BAND), F32),
         pltpu.VMEM((len(tiles), BAND, 2 * BAND), BF16), pltpu.VMEM((len(tiles), BAND, 2 * BAND), BF16)], comm=comm)


def _dqkv_post(d_gs, tabs, dz):
    seq = dz.shape[0]

    def body(g0, g1, g2, c_ref, s1_ref, s2_ref, dz_any, o_ref, sc):
        del dz_any
        ins = (g0, g1, g2)
        c, s1, s2 = c_ref[...], s1_ref[...], s2_ref[...]
        for part in range(3):
            for hh in range(N_GROUPS * HEADS_PER_GROUP):
                g, hl = divmod(hh, HEADS_PER_GROUP)
                icol = part * GROUP_W + hl * HEAD_DIM
                ocol = part * ATTN_W + hh * HEAD_DIM
                d = DILATIONS[g]
                if d == 1:
                    x = ins[g][0, :, icol:icol + HEAD_DIM].astype(F32)
                else:
                    for r in range(d):
                        sc[pl.ds(r, ROW_BLK // d, stride=d), :] = ins[g][r, :, icol:icol + HEAD_DIM].astype(F32)
                    x = sc[...]
                if part < 2:
                    x = x * c + pltpu.roll(x * s1, HEAD_DIM - ROT_DIM // 2, 1) + pltpu.roll(x * s2, ROT_DIM // 2, 1)
                o_ref[:, ocol:ocol + HEAD_DIM] = x.astype(BF16)

    tab_spec = pl.BlockSpec((ROW_BLK, HEAD_DIM), lambda i: (i, 0))
    return pl.pallas_call(
        body, name="dqkv_post", out_shape=jax.ShapeDtypeStruct(dz.shape, BF16), grid=(seq // ROW_BLK,),
        in_specs=_group_specs(ATTN_W) + [tab_spec, tab_spec, tab_spec, pl.BlockSpec(memory_space=pl.ANY)],
        out_specs=pl.BlockSpec((ROW_BLK, QKV_W), lambda i: (i, 0)),
        scratch_shapes=[pltpu.VMEM((ROW_BLK, HEAD_DIM), F32)], input_output_aliases={6: 0},
        compiler_params=_params(dimension_semantics=("arbitrary",)),
    )(*d_gs, *tabs, dz)


def _glu(zg):
    a = zg[:, :CONV_CH].astype(F32)
    s = _sigmoid(zg[:, CONV_CH:].astype(F32))
    return a, s, a * s


def _shifted_copies(xs):
    n = xs.shape[1] - SUBLANES
    for b in range(1, SUBLANES):
        xs[b, 0:n, :] = xs[0, pl.ds(b, n), :]


def _shifted(xs, offset, r0, cs):
    a, b = divmod(offset, SUBLANES)
    return xs[b, pl.ds(SUBLANES * a + r0, CONV_ROWS), cs]


def _conv_fwd(z, cw, cb, lg, lb, comm=_NO_COMM):
    seq = z.shape[0]
    halo_per_blk = ROW_BLK // CONV_HALO

    def body(zg_ref, zh_ref, cw_ref, cb_ref, lg_ref, lb_ref, c2_ref, c4_ref, xs):
        i = pl.program_id(0)
        _, _, c1 = _glu(zg_ref[...])
        _, _, c1h = _glu(zh_ref[...])
        xs[0, 0:CONV_HALO, :] = jnp.where(i > 0, c1h, 0.0)
        xs[0, CONV_HALO:, :] = c1
        _shifted_copies(xs)
        for s in range(CONV_CH // HEAD_DIM):
            cs = slice(s * HEAD_DIM, (s + 1) * HEAD_DIM)
            taps = [cw_ref[j:j + 1, cs] for j in range(CONV_K)]
            bias = cb_ref[:, cs]

            def chunk(rc, carry, cs=cs, taps=taps, bias=bias):
                r0 = pl.multiple_of(rc * CONV_ROWS, CONV_ROWS)
                acc = [jnp.zeros((CONV_ROWS, HEAD_DIM), F32)] * 2
                for j in range(CONV_K):
                    acc[j % 2] = acc[j % 2] + taps[j] * _shifted(xs, CONV_HALO - (CONV_K - 1) + j, r0, cs)
                c2_ref[pl.ds(r0, CONV_ROWS), cs] = acc[0] + acc[1] + bias
                return carry

            lax.fori_loop(0, ROW_BLK // CONV_ROWS, chunk, 0)
        c2 = c2_ref[...]
        mu = jnp.mean(c2, axis=-1, keepdims=True)
        xc = c2 - mu
        rstd = lax.rsqrt(jnp.mean(xc * xc, axis=-1, keepdims=True) + EPS)
        c3 = xc * rstd * lg_ref[...] + lb_ref[...]
        c4_ref[...] = (c3 * _sigmoid(c3)).astype(BF16)

    vec = pl.BlockSpec((1, CONV_CH), lambda i: (0, 0))
    return _pcall(
        body, "conv_fwd", (seq // ROW_BLK,), [z, z, cw, cb, lg, lb],
        [pl.BlockSpec((ROW_BLK, 2 * CONV_CH), lambda i: (i, GLU_COL_BLK)),
         pl.BlockSpec((CONV_HALO, 2 * CONV_CH), lambda i: (jnp.maximum(i * halo_per_blk - 1, 0), GLU_COL_BLK)),
         pl.BlockSpec((CONV_HALO, CONV_CH), lambda i: (0, 0)), vec, vec, vec],
        [jax.ShapeDtypeStruct((seq, CONV_CH), F32), jax.ShapeDtypeStruct((seq, CONV_CH), BF16)],
        [pl.BlockSpec((ROW_BLK, CONV_CH), lambda i: (i, 0)), pl.BlockSpec((ROW_BLK, CONV_CH), lambda i: (i, 0))],
        [pltpu.VMEM((SUBLANES, ROW_BLK + CONV_HALO, CONV_CH), F32)], comm=comm)


def _conv_bwd(dc2, z, cw, dz, comm=_NO_COMM):
    seq = z.shape[0]
    halo_per_blk = ROW_BLK // CONV_HALO
    n_blk = seq // ROW_BLK
    last_halo = seq // CONV_HALO - 1

    def body(dc_ref, dn_ref, zg_ref, zh_ref, cw_ref, dz_any, o_ref, dcw_ref, xs, ys, dc1_ref, dcw_acc):
        del dz_any
        i = pl.program_id(0)
        a, s, c1 = _glu(zg_ref[...])
        _, _, c1h = _glu(zh_ref[...])
        xs[0, 0:CONV_HALO, :] = jnp.where(i > 0, c1h, 0.0)
        xs[0, CONV_HALO:, :] = c1
        ys[0, 0:ROW_BLK, :] = dc_ref[...]
        ys[0, ROW_BLK:, :] = jnp.where(i < n_blk - 1, dn_ref[...], 0.0)
        _shifted_copies(xs)
        _shifted_copies(ys)

        @pl.when(i == 0)
        def _():
            dcw_acc[...] = jnp.zeros_like(dcw_acc)

        for sl in range(CONV_CH // HEAD_DIM):
            cs = slice(sl * HEAD_DIM, (sl + 1) * HEAD_DIM)
            taps = [cw_ref[j:j + 1, cs] for j in range(CONV_K)]

            def chunk(rc, carry, cs=cs, taps=taps):
                r0 = pl.multiple_of(rc * CONV_ROWS, CONV_ROWS)
                dc = ys[0, pl.ds(r0, CONV_ROWS), cs]
                acc = [jnp.zeros((CONV_ROWS, HEAD_DIM), F32)] * 2
                for j in range(CONV_K):
                    prod = dc * _shifted(xs, CONV_HALO - (CONV_K - 1) + j, r0, cs)
                    dcw_acc[j, :, cs] += jnp.sum(prod.reshape(CONV_ROWS // SUBLANES, SUBLANES, HEAD_DIM), axis=0)
                    acc[j % 2] = acc[j % 2] + taps[j] * _shifted(ys, CONV_K - 1 - j, r0, cs)
                dc1_ref[pl.ds(r0, CONV_ROWS), cs] = acc[0] + acc[1]
                return carry

            lax.fori_loop(0, ROW_BLK // CONV_ROWS, chunk, 0)
        dc1 = dc1_ref[...]
        o_ref[:, :CONV_CH] = (dc1 * s).astype(BF16)
        o_ref[:, CONV_CH:] = (dc1 * a * s * (1.0 - s)).astype(BF16)

        @pl.when(i == n_blk - 1)
        def _():
            dcw_ref[...] = jnp.sum(dcw_acc[...], axis=1)

    return _pcall(
        body, "conv_bwd", (n_blk,), [dc2, dc2, z, z, cw, dz],
        [pl.BlockSpec((ROW_BLK, CONV_CH), lambda i: (i, 0)),
         pl.BlockSpec((CONV_HALO, CONV_CH), lambda i: (jnp.minimum((i + 1) * halo_per_blk, last_halo), 0)),
         pl.BlockSpec((ROW_BLK, 2 * CONV_CH), lambda i: (i, GLU_COL_BLK)),
         pl.BlockSpec((CONV_HALO, 2 * CONV_CH), lambda i: (jnp.maximum(i * halo_per_blk - 1, 0), GLU_COL_BLK)),
         pl.BlockSpec((CONV_HALO, CONV_CH), lambda i: (0, 0)),
         pl.BlockSpec(memory_space=pl.ANY)],
        [jax.ShapeDtypeStruct(dz.shape, BF16), jax.ShapeDtypeStruct((CONV_HALO, CONV_CH), F32)],
        [pl.BlockSpec((ROW_BLK, 2 * CONV_CH), lambda i: (i, GLU_COL_BLK)), pl.BlockSpec((CONV_HALO, CONV_CH), lambda i: (0, 0))],
        [pltpu.VMEM((SUBLANES, ROW_BLK + CONV_HALO, CONV_CH), F32), pltpu.VMEM((SUBLANES, ROW_BLK + CONV_HALO, CONV_CH), F32),
         pltpu.VMEM((ROW_BLK, CONV_CH), F32), pltpu.VMEM((CONV_HALO, SUBLANES, CONV_CH), F32)],
        aliases={5: 0}, comm=comm)


def _epi_mix(ya, c4, wcp, gates, bg):
    yc = _dot(c4, wcp, "nn")
    gv = _sigmoid(gates.astype(F32) + bg)
    merged = gv[:, :D_MODEL] * ya + gv[:, D_MODEL:] * yc
    return merged, ya, yc


def _epi_residual_rms(acc, xres, g):
    x = xres + acc
    return x, _rms_fwd_vals(x, g)


def _cross_scores(cq, ck):
    out = []
    for h in range(CROSS_HEADS):
        cs = slice(h * CROSS_HD, (h + 1) * CROSS_HD)
        s = _dot(cq[:, cs], ck[:, cs], "nt") * (CROSS_HD ** -0.5)
        e = jnp.exp(s - jnp.max(s, axis=-1, keepdims=True))
        out.append((cs, e, jnp.sum(e, axis=-1, keepdims=True)))
    return out


def _epi_cross_fwd(acc, ck, cv):
    cq = acc.astype(BF16)
    co = [_dot(e, cv[:, cs], "nn") / den for cs, e, den in _cross_scores(cq, ck)]
    return cq, jnp.concatenate(co, axis=1)


def _epi_cross_bwd(dco, cq, ck, cv):
    dco = dco.astype(BF16)
    dcq, dck, dcv = [], [], []
    for cs, e, den in _cross_scores(cq, ck):
        p = e / den
        dp = _dot(dco[:, cs], cv[:, cs], "nt")
        ds = (p * (dp - jnp.sum(dp * p, axis=-1, keepdims=True)) * (CROSS_HD ** -0.5)).astype(BF16)
        dcq.append(_dot(ds, ck[:, cs], "nn"))
        dck.append(_dot(ds, cq[:, cs], "tn"))
        dcv.append(_dot(p, dco[:, cs], "tn"))
    return jnp.concatenate(dcq, axis=1), jnp.concatenate(dck, axis=1), jnp.concatenate(dcv, axis=1)


def _epi_mlp_up(acc):
    return acc, jnp.square(jnp.maximum(acc, 0.0))


def _epi_final(acc, x2, tgt, g):
    x3 = x2 + acc
    err = _rms_fwd_vals(x3, g) - tgt
    loss = (0.5 / D_MODEL) * jnp.sum(err * err)
    dx3, dg = _rms_bwd_vals(x3, g, err * (1.0 / D_MODEL))
    return dx3, jnp.full((1, HEAD_DIM), loss, F32), dg


def _epi_mlp_down_bwd(dh, hpre):
    return (dh * 2.0 * jnp.maximum(hpre.astype(F32), 0.0),)


def _epi_rms_bwd(du, x, g, dres):
    dx, dg = _rms_bwd_vals(x, g, du)
    return dres.astype(F32) + dx, dg


def _epi_rms_bwd_g(du, x, g):
    return (_rms_bwd_vals(x, g, du)[1],)


def _epi_mix_bwd(dm, ya, yc, gates, bg):
    gv = _sigmoid(gates.astype(F32) + bg)
    ga, gb = gv[:, :D_MODEL], gv[:, D_MODEL:]
    ya, yc = ya.astype(F32), yc.astype(F32)
    dgate = jnp.concatenate([dm * ya * ga * (1.0 - ga), dm * yc * gb * (1.0 - gb)], axis=1)
    return dm * ga, dm * gb, dgate, jnp.sum(dgate, axis=0, keepdims=True)


def _epi_ln_bwd(dc4, c2, lg, lb):
    mu = jnp.mean(c2, axis=-1, keepdims=True)
    xc = c2 - mu
    rstd = lax.rsqrt(jnp.mean(xc * xc, axis=-1, keepdims=True) + EPS)
    xh = xc * rstd
    c3 = xh * lg + lb
    sg = _sigmoid(c3)
    dc3 = dc4 * sg * (1.0 + c3 * (1.0 - sg))
    dxh = dc3 * lg
    dc2 = rstd * (dxh - jnp.mean(dxh, axis=-1, keepdims=True) - xh * jnp.mean(dxh * xh, axis=-1, keepdims=True))
    return (dc2, jnp.sum(dc3 * xh, axis=0, keepdims=True), jnp.sum(dc3, axis=0, keepdims=True),
            jnp.sum(dc2, axis=0, keepdims=True))


def _sds(shape, dtype):
    return jax.ShapeDtypeStruct(shape, dtype)


class _Lazy:
    def __init__(self, fn):
        self.fn = fn

    def __getitem__(self, key):
        return self.fn(key)


def _local_step(x, mem, tgt, sm, plan):
    w = _Lazy(plan.w)
    dw = {}

    def carry(name, n_own, fn, *args, **kw):
        c = plan.comm(name, dw)
        res = fn(*args, comm=c, **kw)
        plan.done(name, res[n_own:])
        return res[:n_own]

    def mm(name, *args, **kw):
        return carry(name, len(args[6]), _mm, name, *args, **kw)

    seq = x.shape[0]
    nr = seq // ROW_BLK
    big = min(1024, seq)
    nb = seq // big
    row = lambda n: ((ROW_BLK, n), lambda i, j, k: (i, 0))
    vec = lambda n: ((1, n), lambda i, j, k: (0, 0))
    full = lambda r, c: ((r, c), lambda i, j, k: (0, 0))
    gates_blk = ((ROW_BLK, 2 * D_MODEL), lambda i, j, k: (i, GATE_COL_BLK))
    tabs = _rope_tables(seq)

    u = carry("rms_mix", 1, _rms_fwd, "rms_mix", x, sm["g_mix"], ROW_BLK)[0]
    whole3 = lambda a: (a.shape, lambda i, j, k: (0, 0, 0))
    z = plan.project_in(u)
    a_gs = carry("qkv_prep", 3, _qkv_prep, z, tabs)
    os_, ls_ = [], []
    for g in range(N_GROUPS):
        name = "attn_fwd_%d" % g
        o_g, l_g = carry(name, 2, _attn_fwd, name, a_gs[g])
        os_.append(o_g)
        ls_.append(l_g)
    attn, lt = _attn_merge(os_, ls_, seq)
    c2, c4 = carry("conv_fwd", 2, _conv_fwd, z, w["taps"], sm["conv_b"], sm["conv_ln_g"], sm["conv_ln_b"])
    merged, ya, yc = mm(
        "mix", attn, w["w_attn_proj"], "nn", (nr, 1, 1), row(GROUP_W), full(GROUP_W, D_MODEL),
        [(_sds((seq, D_MODEL), BF16), *row(D_MODEL))] * 3,
        extras=[(c4, *row(CONV_CH)), (w["w_conv_proj"], *full(CONV_CH, D_MODEL)), (z, *gates_blk), (sm["b_gate"], *vec(2 * D_MODEL))],
        epi=_epi_mix)
    x1, uq = mm("out_proj", merged, w["w_out"], "nn", (nr, 1, 1), row(D_MODEL), full(D_MODEL, D_MODEL),
                 [(_sds((seq, D_MODEL), F32), *row(D_MODEL)), (_sds((seq, D_MODEL), BF16), *row(D_MODEL))],
                 extras=[(x, *row(D_MODEL)), (sm["g_cross"], *vec(D_MODEL))], epi=_epi_residual_rms)

    mn = _rms_fwd("rms_mem", mem, sm["g_mem"], N_MEM)[0]
    ckv = mm("ckv_proj", mn, w["w_ckv"], "nn", (1, N_DEV, 1), full(N_MEM, D_MODEL),
              ((None, D_MODEL, 2 * D_MODEL // N_DEV), lambda i, j, k: (j, 0, 0)),
              [(_sds((N_MEM, 2 * D_MODEL), BF16), (N_MEM, 2 * D_MODEL // N_DEV), lambda i, j, k: (0, j))])[0]
    ck, cv = ckv[:, :D_MODEL], ckv[:, D_MODEL:]
    kv_blk = full(N_MEM, D_MODEL)
    cq, co = mm("cq_proj_cross", uq, w["w_cq"], "nn", (nr, 1, 1), row(D_MODEL), full(D_MODEL, D_MODEL),
                 [(_sds((seq, D_MODEL), BF16), *row(D_MODEL))] * 2,
                 extras=[(ck, *kv_blk), (cv, *kv_blk)], epi=_epi_cross_fwd)
    x2, um = mm("co_proj", co, w["w_co"], "nn", (nr, 1, 1), row(D_MODEL), full(D_MODEL, D_MODEL),
                 [(_sds((seq, D_MODEL), F32), *row(D_MODEL)), (_sds((seq, D_MODEL), BF16), *row(D_MODEL))],
                 extras=[(x1, *row(D_MODEL)), (sm["g_mlp"], *vec(D_MODEL))], epi=_epi_residual_rms)

    ff_blk = D_FF // N_DEV
    row_f32 = (_sds((seq, D_MODEL), F32), *row(D_MODEL))
    row_bf16 = (_sds((seq, D_MODEL), BF16), *row(D_MODEL))
    col_sum = (_sds((1, D_MODEL), F32), *vec(D_MODEL))
    hpre, h = mm("mlp_up", um, w["w_up"], "nn", (nr, 1, 1), row(D_MODEL), whole3(w["w_up"]),
                 [(_sds((seq, D_FF), BF16), *row(D_FF))] * 2, epi=_epi_mlp_up, split=("cols", N_DEV), b_resident=True)
    kt = D_FF // D_MODEL
    dx3, loss, dg_final = mm(
        "mlp_down_loss", h, w["w_down"], "nn", (nr, 1, 1), row(D_FF), full(D_FF, D_MODEL),
        [row_bf16, (_sds((1, HEAD_DIM), F32), *vec(HEAD_DIM)), col_sum],
        extras=[(x2, *row(D_MODEL)), (tgt, *row(D_MODEL)), (sm["g_final"], *vec(D_MODEL))], epi=_epi_final, acc_outs=(1, 2),
        b_resident=True)

    dhpre = mm("mlp_down_bwd", dx3, w["w_down"], "nt", (nr, 1, 1), row(D_MODEL), full(D_FF, D_MODEL),
               [(_sds((seq, D_FF), BF16), *row(D_FF))], extras=[(hpre, *row(D_FF))], epi=_epi_mlp_down_bwd,
               split=("cols", kt), b_resident=True)[0]
    big2 = min(2 * big, seq)
    nb2 = seq // big2
    dw["w_down"] = mm("dw_down", h, dx3, "tn", (kt, 1, nb2), ((big2, D_MODEL), lambda i, j, k: (k, i)),
                      ((big2, D_MODEL), lambda i, j, k: (k, 0)),
                      [(_sds((D_FF, D_MODEL), BF16), (D_MODEL, D_MODEL), lambda i, j, k: (i, 0))])[0]
    dx2, dg_mlp = mm("mlp_up_bwd", dhpre, w["w_up"], "nt", (nr, 1, 1), row(D_FF), whole3(w["w_up"]),
                     [row_bf16, col_sum],
                     extras=[(x2, *row(D_MODEL)), (sm["g_mlp"], *vec(D_MODEL)), (dx3, *row(D_MODEL))],
                     epi=_epi_rms_bwd, acc_outs=(1,), split=("sum", N_DEV), b_resident=True)
    dw["w_up"] = mm("dw_up", um, dhpre, "tn", (1, N_DEV, nb2), ((big2, D_MODEL), lambda i, j, k: (k, 0)),
                    ((big2, ff_blk), lambda i, j, k: (k, j)),
                    [(_sds((N_DEV, D_MODEL, ff_blk), BF16), (None, D_MODEL, ff_blk), lambda i, j, k: (j, 0, 0))])[0]

    acc_kv = (_sds((N_MEM, D_MODEL), F32), *kv_blk)
    dcq, dck, dcv = mm("co_proj_bwd_cross", dx2, w["w_co"], "nt", (nr, 1, 1), row(D_MODEL), full(D_MODEL, D_MODEL),
                       [row_bf16, acc_kv, acc_kv],
                       extras=[(cq, *row(D_MODEL)), (ck, *kv_blk), (cv, *kv_blk)], epi=_epi_cross_bwd, acc_outs=(1, 2))

    def dw_square(name, act, grad):
        return mm(name, act, grad, "tn", (1, 1, nb2), ((big2, D_MODEL), lambda i, j, k: (k, 0)),
                  ((big2, D_MODEL), lambda i, j, k: (k, 0)), [(_sds((D_MODEL, D_MODEL), BF16), *full(D_MODEL, D_MODEL))])[0]

    dw["w_co"] = dw_square("dw_co", co, dx2)
    dx1, dg_cross = mm("cq_proj_bwd", dcq, w["w_cq"], "nt", (nr, 1, 1), row(D_MODEL), full(D_MODEL, D_MODEL),
                       [row_bf16, col_sum],
                       extras=[(x1, *row(D_MODEL)), (sm["g_cross"], *vec(D_MODEL)), (dx2, *row(D_MODEL))],
                       epi=_epi_rms_bwd, acc_outs=(1,))
    dw["w_cq"] = dw_square("dw_cq", uq, dcq)
    dckv = jnp.concatenate([dck, dcv], axis=1)
    kv_chunk = 2 * D_MODEL // N_DEV
    dw["w_ckv"] = mm("dw_ckv", mn, dckv, "tn", (1, N_DEV, 1), full(N_MEM, D_MODEL), ((N_MEM, kv_chunk), lambda i, j, k: (0, j)),
                      [(_sds((N_DEV, D_MODEL, kv_chunk), BF16), (None, D_MODEL, kv_chunk), lambda i, j, k: (j, 0, 0))])[0]
    dg_mem = mm("ckv_proj_bwd", dckv, w["w_ckv"], "nt", (1, 1, N_DEV), ((N_MEM, kv_chunk), lambda i, j, k: (0, k)),
                 ((None, D_MODEL, kv_chunk), lambda i, j, k: (k, 0, 0)), [(_sds((1, D_MODEL), F32), *vec(D_MODEL))],
                 extras=[(mem, *full(N_MEM, D_MODEL)), (sm["g_mem"], *vec(D_MODEL))], epi=_epi_rms_bwd_g, acc_outs=(0,))[0]

    dya, dyc, dz, db_gate = mm(
        "out_proj_bwd_mix", dx1, w["w_out"], "nt", (nr, 1, 1), row(D_MODEL), full(D_MODEL, D_MODEL),
        [(_sds((seq, D_MODEL), BF16), *row(D_MODEL)), (_sds((seq, D_MODEL), BF16), *row(D_MODEL)),
         (_sds((seq, IN_W), BF16), *gates_blk), (_sds((1, 2 * D_MODEL), F32), *vec(2 * D_MODEL))],
        extras=[(ya, *row(D_MODEL)), (yc, *row(D_MODEL)), (z, *gates_blk), (sm["b_gate"], *vec(2 * D_MODEL))],
        epi=_epi_mix_bwd, acc_outs=(3,))
    dw["w_out"] = dw_square("dw_out", merged, dx1)
    dattn = mm("attn_proj_bwd", dya, w["w_attn_proj"], "nt", (nr, 1, 1), row(D_MODEL), full(GROUP_W, D_MODEL),
                [(_sds((seq, GROUP_W), BF16), *row(GROUP_W))])[0]
    pc = D_MODEL // N_DEV
    dw["w_attn_proj"] = mm("dw_attn_proj", attn, dya, "tn", (1, 1, nb2), ((big2, GROUP_W), lambda i, j, k: (k, 0)),
                           ((big2, D_MODEL), lambda i, j, k: (k, 0)),
                           [(_sds((N_DEV, GROUP_W, pc), BF16), (N_DEV, GROUP_W, pc), lambda i, j, k: (0, 0, 0))],
                           out_chunks=N_DEV)[0]
    cvec = (_sds((1, CONV_CH), F32), *vec(CONV_CH))
    dc2, dg_ln_g, dg_ln_b, dg_conv_b = mm(
        "conv_proj_bwd_ln", dyc, w["w_conv_proj"], "nt", (nr, 1, 1), row(D_MODEL), full(CONV_CH, D_MODEL),
        [(_sds((seq, CONV_CH), F32), *row(CONV_CH)), cvec, cvec, cvec],
        extras=[(c2, *row(CONV_CH)), (sm["conv_ln_g"], *vec(CONV_CH)), (sm["conv_ln_b"], *vec(CONV_CH))],
        epi=_epi_ln_bwd, acc_outs=(1, 2, 3))
    dw["w_conv_proj"] = mm("dw_conv_proj", c4, dyc, "tn", (1, 1, nb2), ((big2, CONV_CH), lambda i, j, k: (k, 0)),
                           ((big2, D_MODEL), lambda i, j, k: (k, 0)),
                           [(_sds((N_DEV, CONV_CH, pc), BF16), (N_DEV, CONV_CH, pc), lambda i, j, k: (0, 0, 0))],
                           out_chunks=N_DEV)[0]
    dz, dg_conv_w = carry("conv_bwd", 2, _conv_bwd, dc2, z, w["taps"], dz)
    preps = _attn_bwd_prep(dattn, attn, lt)
    d_gs = []
    for g in range(N_GROUPS):
        name = "attn_bwd_%d" % g
        d_gs.append(carry(name, 1, _attn_bwd, name, a_gs[g], *preps[g])[0])
    dz = _dqkv_post(d_gs, tabs, dz)
    dw["w_in"] = mm("dw_in", u, dz, "tn", (1, N_DEV, nb2), ((big2, D_MODEL), lambda i, j, k: (k, 0)),
                    ((big2, D_MODEL), lambda i, j, k: (k, j)),
                    [(_sds((N_DEV, D_MODEL, D_MODEL), BF16), (None, D_MODEL, D_MODEL), lambda i, j, k: (j, 0, 0))])[0]
    grad_x, dg_mix = mm("in_proj_bwd", dz, w["w_in"], "nt", (nr, 1, 1), row(IN_W), whole3(w["w_in"]), [row_f32, col_sum],
                        extras=[(x, *row(D_MODEL)), (sm["g_mix"], *vec(D_MODEL)), (dx1, *row(D_MODEL))],
                        epi=_epi_rms_bwd, acc_outs=(1,), split=("sum", N_DEV), b_resident=True)
    small = dict(g_mix=dg_mix, b_gate=db_gate, conv_b=dg_conv_b, conv_ln_g=dg_ln_g, conv_ln_b=dg_ln_b, g_cross=dg_cross,
                 g_mem=dg_mem, g_mlp=dg_mlp, g_final=dg_final, loss=loss, conv_w=dg_conv_w)
    return grad_x, dw, small


SHARD_SHAPE = dict(w_in=(1024, 1024), w_attn_proj=(512, 128), w_conv_proj=(768, 128), w_out=(128, 1024), w_cq=(128, 1024),
                   w_ckv=(1024, 256), w_co=(128, 1024), w_up=(1024, 512), w_down=(512, 1024))
FWD_CARRY = {"in_proj":("w_attn_proj", "w_conv_proj", "w_out", "w_cq", "w_ckv", "w_co", "taps"),
             "qkv_prep": ("w_up",), "conv_fwd": ("w_down",)}
BWD_CARRY = {"dw_up": ("w_down",), "out_proj_bwd_mix": ("w_co", "w_cq"), "conv_bwd": ("w_up", "w_ckv"),
             "attn_bwd_0": ("w_out",), "attn_bwd_1": ("w_attn_proj", "w_conv_proj"), "in_proj_bwd": ("w_in",)}


def _cols_to_2d(a):
    return a.transpose(1, 0, 2).reshape(a.shape[1], -1)


def _in_proj_gather(u, w_shard, comm):
    seq = u.shape[0]
    tm = min(1024, seq)
    x, y, c = lax.axis_index("x"), lax.axis_index("y"), lax.axis_index("c")
    ident = lambda px, py, pc: 4 * px + 2 * py + pc
    far = [(1 - x, y), (x, 1 - y), (1 - x, 1 - y)]
    order = jnp.stack([ident(x, y, c), ident(x, y, 1 - c), ident(*far[0], c), ident(*far[1], c), ident(*far[0], 1 - c),
                       ident(*far[1], 1 - c), ident(*far[2], c), ident(*far[2], 1 - c)]).astype(jnp.int32)
    forward_at = {2: 0, 3: 1, 6: 2}
    n_far = len(far)

    def body(order_ref, u_ref, wsh_ref, *rest):
        c_in, z_ref, wg_ref = rest[:comm.n], rest[comm.n], rest[comm.n + 1]
        c_out = rest[comm.n + 2:2 * comm.n + 2]
        wbuf, load_sem, local_sem, recv_sems, ici_send, d2d_send = rest[2 * comm.n + 2:2 * comm.n + 8]
        sems = rest[2 * comm.n + 8:]
        jj, i = pl.program_id(0), pl.program_id(1)
        (kx, ky, kc), me, chips = _Comm._where()
        sibling = (kx, ky, 1 - kc)
        n = order_ref[jj]

        def push(src, blk, send, to):
            return pltpu.make_async_remote_copy(src_ref=src, dst_ref=wg_ref.at[blk], send_sem=send,
                                                recv_sem=recv_sems.at[blk], device_id=to, device_id_type=MESH)

        def load(src):
            cp = pltpu.make_async_copy(src, wbuf, load_sem)
            cp.start()
            cp.wait()

        @pl.when(jnp.logical_and(jj == 0, i == 0))
        def _():
            push(wsh_ref, me, d2d_send, sibling).start()
            for (px, py) in chips[:2]:
                push(wsh_ref, me, ici_send, (px, py, kc)).start()
            pltpu.make_async_copy(wsh_ref, wg_ref.at[me], local_sem).start()
            load(wsh_ref)

        @pl.when(jnp.logical_and(jj > 0, i == 0))
        def _():
            push(wg_ref.at[n], n, d2d_send, sibling).wait_recv()
            for step, k in forward_at.items():
                @pl.when(jj == step)
                def _(k=k):
                    blk = 4 * chips[k][0] + 2 * chips[k][1] + kc
                    push(wg_ref.at[blk], blk, d2d_send, sibling).start()

            @pl.when(jj == 2)
            def _():
                push(wsh_ref, me, ici_send, (*chips[2], kc)).start()

            if comm.n:
                @pl.when(jj == 3)
                def _():
                    comm.start(c_in, c_out, sems)

            load(wg_ref.at[n])

        z_ref[...] = _dot(u_ref[...], wbuf[...], "nn").astype(BF16)

        @pl.when(jnp.logical_and(jj == N_DEV - 1, i == pl.num_programs(1) - 1))
        def _():
            def drain_sends(send, count):
                blocks = wg_ref.at[pl.ds(0, count)]
                pltpu.make_async_remote_copy(src_ref=blocks, dst_ref=blocks, send_sem=send, recv_sem=recv_sems.at[0],
                                             device_id=sibling, device_id_type=MESH).wait_send()

            drain_sends(ici_send, n_far)
            drain_sends(d2d_send, n_far + 1)
            pltpu.make_async_copy(wsh_ref, wg_ref.at[me], local_sem).wait()
            if comm.n:
                comm.wait(c_in, c_out, sems)

    any_spec = pl.BlockSpec(memory_space=pl.ANY)
    grid_spec = pltpu.PrefetchScalarGridSpec(
        num_scalar_prefetch=1, grid=(N_DEV, seq // tm),
        in_specs=[pl.BlockSpec((tm, D_MODEL), lambda jj, i, order_ref: (i, 0)), any_spec] + comm.in_specs,
        out_specs=[pl.BlockSpec((tm, D_MODEL), lambda jj, i, order_ref: (i, order_ref[jj])), any_spec] + comm.out_specs,
        scratch_shapes=[pltpu.VMEM((D_MODEL, D_MODEL), BF16), pltpu.SemaphoreType.DMA, pltpu.SemaphoreType.DMA,
                        pltpu.SemaphoreType.DMA((N_DEV,)), pltpu.SemaphoreType.DMA, pltpu.SemaphoreType.DMA] + comm.scratch)
    return pl.pallas_call(
        body, name="in_proj_gather", grid_spec=grid_spec,
        out_shape=[jax.ShapeDtypeStruct((seq, IN_W), BF16), jax.ShapeDtypeStruct((N_DEV, D_MODEL, D_MODEL), BF16)] + comm.out_shape,
        compiler_params=_params(dimension_semantics=("arbitrary", "arbitrary")),
    )(order, u, w_shard, *comm.arrays)


class _Plan:
    def __init__(self, shards, n_tap_cols):
        self.shards = shards
        self.gathered = {}
        self.parts = {}
        self.n_tap_cols = n_tap_cols

    def project_in(self, u):
        comm = self.comm("in_proj", None)
        res = _in_proj_gather(u, self.shards["w_in"], comm)
        self.gathered["w_in"] = res[1]
        self.done("in_proj", res[2:])
        return res[0]

    def comm(self, name, dw):
        if name in FWD_CARRY:
            return _Comm(replicated=[self.shards[k] for k in FWD_CARRY[name]])
        if name in BWD_CARRY:
            return _Comm(chunked=[dw[k].reshape((N_DEV,) + SHARD_SHAPE[k]) for k in BWD_CARRY[name]])
        return _NO_COMM

    def done(self, name, got):
        if name in FWD_CARRY:
            self.gathered.update(zip(FWD_CARRY[name], got))
        elif name in BWD_CARRY:
            self.parts.update(zip(BWD_CARRY[name], got))

    def w(self, key):
        g = self.gathered[key]
        if key in ("w_in", "w_up", "w_ckv"):
            return g
        if key in ("w_attn_proj", "w_conv_proj"):
            return _cols_to_2d(g)
        if key == "taps":
            return jnp.pad(_cols_to_2d(g[:, :CONV_K, :self.n_tap_cols]), ((0, 1), (0, 0)))
        return g.reshape(-1, g.shape[-1])


def _adamw(name, w, m, v, parts, comm=_NO_COMM):
    rows, cols = w.shape
    n_parts = parts.shape[0]
    rb = rows if rows <= 256 or rows % 256 else 256

    def body(w_ref, m_ref, v_ref, p_ref, g_ref, d_ref, nm_ref, nv_ref):
        g = p_ref[0].astype(F32)
        for q in range(1, n_parts):
            g = g + p_ref[q].astype(F32)
        wv = w_ref[...]
        nm = ADAM_B1 * m_ref[...] + (1.0 - ADAM_B1) * g
        nv = ADAM_B2 * v_ref[...] + (1.0 - ADAM_B2) * jnp.square(g)
        m_hat = nm / (1.0 - ADAM_B1 ** ADAM_STEP)
        v_hat = nv / (1.0 - ADAM_B2 ** ADAM_STEP)
        g_ref[...] = g
        d_ref[...] = -ADAM_LR * (m_hat / (jnp.sqrt(v_hat) + ADAM_EPS) + ADAM_WD * wv)
        nm_ref[...] = nm
        nv_ref[...] = nv

    blk = pl.BlockSpec((rb, cols), lambda i: (i, 0))
    return _pcall(body, name, (rows // rb,), [w, m, v, parts],
                  [blk, blk, blk, pl.BlockSpec((n_parts, rb, cols), lambda i: (0, i, 0))],
                  [jax.ShapeDtypeStruct((rows, cols), F32)] * 4, [blk] * 4, comm=comm)


def _sum_parts(name, parts):
    def body(p_ref, o_ref):
        acc = p_ref[0]
        for q in range(1, parts.shape[0]):
            acc = acc + p_ref[q]
        o_ref[...] = acc

    return _pcall(body, name, (1,), [parts], [pl.BlockSpec(parts.shape, lambda i: (0, 0, 0))],
                  [jax.ShapeDtypeStruct(parts.shape[1:], F32)], [pl.BlockSpec(parts.shape[1:], lambda i: (0, 0))])[0]


BIG = ("w_in", "w_attn_proj", "w_conv_proj", "w_out", "w_cq", "w_ckv", "w_co", "w_up", "w_down")
SMALL = ("g_mix", "b_gate", "conv_b", "conv_ln_g", "conv_ln_b", "g_cross", "g_mem", "g_mlp", "g_final")
SMALL_ORDER = SMALL + ("loss", "conv_w")
WEIGHTS = ("g_mix", "w_in", "b_gate", "conv_w", "conv_b", "conv_ln_g", "conv_ln_b", "w_attn_proj", "w_conv_proj", "w_out",
           "g_cross", "g_mem", "w_cq", "w_ckv", "w_co", "g_mlp", "w_up", "w_down", "g_final")


def kernel(x, mem, g_mix, w_in, b_gate, conv_w, conv_b, conv_ln_g, conv_ln_b, w_attn_proj, w_conv_proj, w_out, g_cross, g_mem, w_cq, w_ckv, w_co, g_mlp, w_up, w_down, g_final, loss_target, m_g_mix, m_w_in, m_b_gate, m_conv_w, m_conv_b, m_conv_ln_g, m_conv_ln_b, m_w_attn_proj, m_w_conv_proj, m_w_out, m_g_cross, m_g_mem, m_w_cq, m_w_ckv, m_w_co, m_g_mlp, m_w_up, m_w_down, m_g_final, v_g_mix, v_w_in, v_b_gate, v_conv_w, v_conv_b, v_conv_ln_g, v_conv_ln_b, v_w_attn_proj, v_w_conv_proj, v_w_out, v_g_cross, v_g_mem, v_w_cq, v_w_ckv, v_w_co, v_g_mlp, v_w_up, v_w_down, v_g_final):
    args = dict(locals())
    wts = {k: args[k] for k in WEIGHTS}
    mom = {k: args["m_" + k] for k in WEIGHTS}
    var = {k: args["v_" + k] for k in WEIGHTS}
    two_d = lambda a: a.reshape(a.shape[-2:]) if a.ndim == 3 else a.reshape(1, -1)

    shards = {k: two_d(wts[k]).astype(BF16) for k in BIG}
    shards["taps"] = jnp.pad(two_d(conv_w), ((0, 1), (0, HEAD_DIM - conv_w.shape[-1])))
    plan = _Plan(shards, conv_w.shape[-1])
    sm = {k: two_d(wts[k]) for k in SMALL}

    grad_x, _, small = _local_step(x[0], mem[0], loss_target[0], sm, plan)
    parts = plan.parts

    out = {}
    small_comm = _Comm(replicated=[small[k] for k in SMALL_ORDER])
    for k in BIG:
        res = _adamw("adamw_" + k, two_d(wts[k]), two_d(mom[k]), two_d(var[k]), parts[k],
                     comm=small_comm if k == BIG[0] else _NO_COMM)
        out[k] = [r.reshape(wts[k].shape) for r in res[:4]]
        if k == BIG[0]:
            small_parts = dict(zip(SMALL_ORDER, res[4:]))
    for k in SMALL:
        res = _adamw("adamw_" + k, two_d(wts[k]), two_d(mom[k]), two_d(var[k]), small_parts[k])
        out[k] = [r.reshape(wts[k].shape) for r in res]
    loss = _sum_parts("loss_sum", small_parts["loss"])[0, 0]
    me = 4 * lax.axis_index("x") + 2 * lax.axis_index("y") + lax.axis_index("c")
    n_tap_cols = conv_w.shape[-1]
    tap_parts = lax.dynamic_slice(small_parts["conv_w"], (0, 0, me * n_tap_cols), (N_DEV, CONV_K, n_tap_cols))
    res = _adamw("adamw_conv_w", two_d(conv_w), two_d(m_conv_w), two_d(v_conv_w), tap_parts)
    out["conv_w"] = [r.reshape(conv_w.shape) for r in res]

    return (loss, grad_x[None], *[out[k][0] for k in WEIGHTS], *[out[k][1] for k in WEIGHTS],
            *[out[k][2] for k in WEIGHTS], *[out[k][3] for k in WEIGHTS])
```

```python
import functools

import jax
import jax.numpy as jnp
import numpy as np
from jax import lax
from jax.experimental import pallas as pl
from jax.experimental.pallas import tpu as pltpu

F32 = jnp.float32
BF16 = jnp.bfloat16

N_DEV = 8
D_MODEL = 1024
N_MEM = 256
HEAD_DIM = 128
HEADS_PER_GROUP = 4
GROUP_W = HEADS_PER_GROUP * HEAD_DIM
DILATIONS = (1, 4, 16)
BAND = 128
N_GROUPS = 3
ATTN_W = N_GROUPS * GROUP_W
QKV_W = 3 * ATTN_W
ROT_DIM = HEAD_DIM // 4
ROPE_THETA = 500000.0
CONV_CH = 768
CONV_K = 31
CONV_HALO = 32
SUBLANES = 8
CONV_ROWS = 64
IN_W = 8192
GLU_COL_BLK = QKV_W // (2 * CONV_CH)
GATE_COL_BLK = (QKV_W + 2 * CONV_CH) // (2 * D_MODEL)
CROSS_HEADS = 4
CROSS_HD = D_MODEL // CROSS_HEADS
D_FF = 4096
EPS = 1e-6
NEG = -1e30
QB = 4
ROW_BLK = QB * BAND

ADAM_LR = 0.001
ADAM_B1 = 0.9
ADAM_B2 = 0.999
ADAM_EPS = 1e-08
ADAM_WD = 0.01
ADAM_STEP = 10

VMEM_LIMIT = 56 * 1024 * 1024
MESH = pl.DeviceIdType.MESH


def _params(**kw):
    return pltpu.CompilerParams(vmem_limit_bytes=VMEM_LIMIT, **kw)


def _sigmoid(x):
    return 1.0 / (1.0 + jnp.exp(-x))


def _dot(a, b, kind):
    dims = {"nn": (((1,), (0,)), ((), ())), "nt": (((1,), (1,)), ((), ())), "tn": (((0,), (0,)), ((), ()))}[kind]
    if a.dtype != BF16:
        a = a.astype(BF16)
    if b.dtype != BF16:
        b = b.astype(BF16)
    return lax.dot_general(a, b, dims, preferred_element_type=F32)


def _peers():
    x, y, c = lax.axis_index("x"), lax.axis_index("y"), lax.axis_index("c")
    me = 4 * x + 2 * y + c
    peers = [(x, y, 1 - c), (1 - x, y, c), (x, 1 - y, c), (1 - x, 1 - y, c),
             (1 - x, y, 1 - c), (x, 1 - y, 1 - c), (1 - x, 1 - y, 1 - c)]
    return me, peers


class _Comm:
    def __init__(self, chunked=(), replicated=()):
        self.arrays = list(chunked) + list(replicated)
        self.n_c = len(chunked)
        self.n = len(self.arrays)
        self.out_shape = [jax.ShapeDtypeStruct(a.shape, a.dtype) for a in chunked]
        self.out_shape += [jax.ShapeDtypeStruct((N_DEV,) + a.shape, a.dtype) for a in replicated]
        self.in_specs = [pl.BlockSpec(memory_space=pl.ANY)] * self.n
        self.out_specs = [pl.BlockSpec(memory_space=pl.ANY)] * self.n
        self.scratch = [pltpu.SemaphoreType.DMA((self.n,))] * 5 if self.n else []

    @staticmethod
    def _where():
        x, y, c = lax.axis_index("x"), lax.axis_index("y"), lax.axis_index("c")
        chips = [(1 - x, y), (x, 1 - y), (1 - x, 1 - y)]
        return (x, y, c), 4 * x + 2 * y + c, chips

    def _local(self, ins, outs, sems, a, me):
        src = ins[a].at[me] if a < self.n_c else ins[a]
        return pltpu.make_async_copy(src, outs[a].at[me], sems[2].at[a])

    @staticmethod
    def _remote(src, dst, send, recv, to):
        return pltpu.make_async_remote_copy(src_ref=src, dst_ref=dst, send_sem=send, recv_sem=recv, device_id=to,
                                            device_id_type=MESH)

    def start(self, ins, outs, sems):
        (x, y, c), me, chips = self._where()
        for a in range(self.n):
            self._local(ins, outs, sems, a, me).start()
            if a < self.n_c:
                for (px, py, pc) in _peers()[1]:
                    self._remote(ins[a].at[4 * px + 2 * py + pc], outs[a].at[me], sems[0].at[a], sems[1].at[a], (px, py, pc)).start()
            else:
                self._remote(ins[a], outs[a].at[me], sems[3].at[a], sems[4].at[a], (x, y, 1 - c)).start()
                for (px, py) in chips:
                    self._remote(ins[a], outs[a].at[me], sems[0].at[a], sems[1].at[a], (px, py, c)).start()

    def wait(self, ins, outs, sems):
        (x, y, c), me, chips = self._where()
        sibling = (x, y, 1 - c)

        def drain(a, pair, count):
            blocks = outs[a].at[pl.ds(0, count)]
            cp = self._remote(blocks, blocks, sems[pair].at[a], sems[pair + 1].at[a], sibling)
            cp.wait_send()
            cp.wait_recv()

        for a in range(self.n):
            if a < self.n_c:
                drain(a, 0, N_DEV - 1)
            else:
                drain(a, 0, len(chips))
                for (px, py) in chips:
                    blk = outs[a].at[4 * px + 2 * py + c]
                    self._remote(blk, blk, sems[3].at[a], sems[4].at[a], sibling).start()
        for a in range(self.n):
            if a >= self.n_c:
                drain(a, 3, len(chips) + 1)
            self._local(ins, outs, sems, a, me).wait()


_NO_COMM = _Comm()


def _pcall(body, name, grid, operands, in_specs, out_shape, out_specs, scratch=(), aliases=None, comm=_NO_COMM, **params):
    n_in, n_out, n_scr = len(operands), len(out_shape), len(scratch)
    grid = tuple(grid)

    def carried(*refs):
        ins, c_in = refs[:n_in], refs[n_in:n_in + comm.n]
        o0 = n_in + comm.n
        outs, c_out = refs[o0:o0 + n_out], refs[o0 + n_out:o0 + n_out + comm.n]
        s0 = o0 + n_out + comm.n
        scr, sems = refs[s0:s0 + n_scr], refs[s0 + n_scr:]
        ids = [pl.program_id(ax) for ax in range(len(grid))]

        @pl.when(functools.reduce(jnp.logical_and, [p == 0 for p in ids]))
        def _():
            comm.start(c_in, c_out, sems)

        body(*ins, *outs, *scr)

        @pl.when(functools.reduce(jnp.logical_and, [p == g - 1 for p, g in zip(ids, grid)]))
        def _():
            comm.wait(c_in, c_out, sems)

    return pl.pallas_call(
        carried if comm.n else body, name=name, grid=grid, in_specs=list(in_specs) + comm.in_specs,
        out_shape=list(out_shape) + comm.out_shape, out_specs=list(out_specs) + comm.out_specs,
        scratch_shapes=list(scratch) + comm.scratch, input_output_aliases=aliases or {},
        compiler_params=_params(dimension_semantics=("arbitrary",) * len(grid), **params),
    )(*operands, *comm.arrays)


def _mm(name, a, b, kind, grid, a_blk, b_blk, outs, extras=(), epi=None, acc_outs=(), j_outer=False, comm=_NO_COMM,
        split=None, b_resident=False, out_chunks=0):
    gi, gj, gk = grid
    n_ex = len(extras)
    n_out = len(outs)
    mode, n_chunks = split if split is not None else (None, 1)

    def spec(blk, fn, **kw):
        return pl.BlockSpec(blk, (lambda j, i, k: fn(i, j, k)) if j_outer else fn, **kw)

    def b_chunk(b_ref, c):
        if len(b_ref.shape) == 3:
            return b_ref[c]
        rows, cols = b_ref.shape
        if (kind == "nn") == (mode == "cols"):
            return b_ref[:, c * (cols // n_chunks):(c + 1) * (cols // n_chunks)]
        return b_ref[c * (rows // n_chunks):(c + 1) * (rows // n_chunks), :]

    def col_chunk(ref, c):
        width = ref.shape[-1] // n_chunks
        return slice(c * width, (c + 1) * width)

    def body(*refs):
        a_ref, b_ref = refs[0], refs[1]
        ex = refs[2:2 + n_ex]
        out_refs = refs[2 + n_ex:2 + n_ex + n_out]
        acc_ref = refs[2 + n_ex + n_out] if gk > 1 else None
        i = pl.program_id(1 if j_outer else 0)
        k = pl.program_id(2)
        if mode == "cols":
            a_val = a_ref[...]
            for c in range(n_chunks):
                acc = _dot(a_val, b_chunk(b_ref, c), kind)
                vals = epi(acc, *[e[:, col_chunk(e, c)] for e in ex]) if epi is not None else (acc,)
                for o, v in zip(out_refs, vals):
                    o[:, col_chunk(o, c)] = v.astype(o.dtype)
            return
        if mode == "sum":
            part = _dot(a_ref[:, col_chunk(a_ref, 0)], b_chunk(b_ref, 0), kind)
            for c in range(1, n_chunks):
                part = part + _dot(a_ref[:, col_chunk(a_ref, c)], b_chunk(b_ref, c), kind)
        else:
            part = _dot(a_ref[...], b_ref[...], kind)

        def finish(acc):
            if out_chunks:
                width = acc.shape[-1] // out_chunks
                for c in range(out_chunks):
                    out_refs[0][c] = acc[:, c * width:(c + 1) * width].astype(out_refs[0].dtype)
                return
            vals = epi(acc, *[e[...] for e in ex]) if epi is not None else (acc,)
            for idx, (o, v) in enumerate(zip(out_refs, vals)):
                if idx in acc_outs:
                    @pl.when(i == 0)
                    def _():
                        o[...] = v.astype(o.dtype)

                    @pl.when(i != 0)
                    def _():
                        o[...] += v.astype(o.dtype)
                else:
                    o[...] = v.astype(o.dtype)

        if gk == 1:
            finish(part)
        else:
            @pl.when(k == 0)
            def _():
                acc_ref[...] = part

            @pl.when(k != 0)
            def _():
                acc_ref[...] += part

            @pl.when(k == gk - 1)
            def _():
                finish(acc_ref[...])

    scratch = []
    if gk > 1:
        tm = a_blk[0][-1] if kind == "tn" else a_blk[0][-2]
        tn = b_blk[0][-2] if kind == "nt" else b_blk[0][-1]
        scratch = [pltpu.VMEM((tm, tn), F32)]
    b_kw = dict(pipeline_mode=pl.Buffered(1)) if b_resident else {}
    return _pcall(body, name, (gj, gi, gk) if j_outer else (gi, gj, gk), [a, b] + [e for e, _, _ in extras],
                  [spec(*a_blk), spec(*b_blk, **b_kw)] + [spec(blk, fn) for _, blk, fn in extras],
                  [s for s, _, _ in outs], [spec(blk, fn) for _, blk, fn in outs], scratch, comm=comm)


def _rms_fwd_vals(x, g):
    r = lax.rsqrt(jnp.mean(x * x, axis=-1, keepdims=True) + EPS)
    return x * r * g


def _rms_bwd_vals(x, g, du):
    r = lax.rsqrt(jnp.mean(x * x, axis=-1, keepdims=True) + EPS)
    xh = x * r
    dxh = du * g
    dx = r * (dxh - xh * jnp.mean(dxh * xh, axis=-1, keepdims=True))
    return dx, jnp.sum(du * xh, axis=0, keepdims=True)


def _rms_fwd(name, x, g, rows, comm=_NO_COMM):
    n = x.shape[0]

    def body(x_ref, g_ref, o_ref):
        o_ref[...] = _rms_fwd_vals(x_ref[...], g_ref[...]).astype(BF16)

    return _pcall(body, name, (n // rows,), [x, g],
                  [pl.BlockSpec((rows, D_MODEL), lambda i: (i, 0)), pl.BlockSpec((1, D_MODEL), lambda i: (0, 0))],
                  [jax.ShapeDtypeStruct(x.shape, BF16)], [pl.BlockSpec((rows, D_MODEL), lambda i: (i, 0))], comm=comm)


def _rope_tables(seq):
    half = ROT_DIM // 2
    pos = np.arange(seq, dtype=np.float32)
    inv_freq = np.float32(ROPE_THETA) ** (-np.arange(0, ROT_DIM, 2, dtype=np.float32) / np.float32(ROT_DIM))
    ang = (pos[:, None] * inv_freq[None, :]).astype(np.float32)
    cos, sin = np.cos(ang), np.sin(ang)
    rest = HEAD_DIM - ROT_DIM
    c = np.concatenate([cos, cos, np.ones((seq, rest), np.float32)], axis=1)
    s1 = np.concatenate([np.zeros((seq, half), np.float32), sin, np.zeros((seq, rest), np.float32)], axis=1)
    s2 = np.concatenate([-sin, np.zeros((seq, half + rest), np.float32)], axis=1)
    return jnp.asarray(c), jnp.asarray(s1), jnp.asarray(s2)


def _group_shapes(seq, width, dtype):
    return [jax.ShapeDtypeStruct((d, seq // d, width), dtype) for d in DILATIONS]


def _group_specs(width):
    return [pl.BlockSpec((d, ROW_BLK // d, width), lambda i: (0, i, 0)) for d in DILATIONS]


def _qkv_prep(z, tabs, comm=_NO_COMM):
    seq = z.shape[0]

    def body(z_ref, c_ref, s1_ref, s2_ref, a0, a1, a2, sc):
        outs = (a0, a1, a2)
        c, s1, s2 = c_ref[...], s1_ref[...], s2_ref[...]
        for part in range(3):
            for hh in range(N_GROUPS * HEADS_PER_GROUP):
                g, hl = divmod(hh, HEADS_PER_GROUP)
                col = part * ATTN_W + hh * HEAD_DIM
                ocol = part * GROUP_W + hl * HEAD_DIM
                x = z_ref[:, col:col + HEAD_DIM].astype(F32)
                if part < 2:
                    x = x * c + pltpu.roll(x, ROT_DIM // 2, 1) * s1 + pltpu.roll(x, HEAD_DIM - ROT_DIM // 2, 1) * s2
                d = DILATIONS[g]
                if d == 1:
                    outs[g][0, :, ocol:ocol + HEAD_DIM] = x.astype(BF16)
                else:
                    sc[...] = x
                    for r in range(d):
                        outs[g][r, :, ocol:ocol + HEAD_DIM] = sc[pl.ds(r, ROW_BLK // d, stride=d), :].astype(BF16)

    tab_spec = pl.BlockSpec((ROW_BLK, HEAD_DIM), lambda i: (i, 0))
    return _pcall(body, "qkv_prep", (seq // ROW_BLK,), [z, *tabs],
                  [pl.BlockSpec((ROW_BLK, QKV_W), lambda i: (i, 0)), tab_spec, tab_spec, tab_spec],
                  _group_shapes(seq, ATTN_W, BF16), _group_specs(ATTN_W), [pltpu.VMEM((ROW_BLK, HEAD_DIM), F32)], comm=comm)


def _band_masks_2(t):
    qi = lax.broadcasted_iota(jnp.int32, (BAND, 2 * BAND), 0)
    kj = lax.broadcasted_iota(jnp.int32, (BAND, 2 * BAND), 1)
    band = jnp.logical_and(kj >= qi, kj <= qi + BAND)
    return band, jnp.logical_and(band, jnp.logical_or(kj >= BAND, t > 0))


def _attn_fwd(name, a_g, comm=_NO_COMM):
    dil, m_len, _ = a_g.shape
    qb = min(QB, m_len // BAND)
    rows = qb * BAND
    steps = m_len // rows
    scale = HEAD_DIM ** -0.5

    tiles = [(sb, h) for sb in range(qb) for h in range(HEADS_PER_GROUP)]

    def body(q_ref, kc_ref, vc_ref, kp_ref, vp_ref, o_ref, l_ref, k_all, v_all, s_scr, p_scr, r_scr):
        t = pl.program_id(1)
        k_all[0:BAND, :] = kp_ref[...]
        k_all[BAND:, :] = kc_ref[...]
        v_all[0:BAND, :] = vp_ref[...]
        v_all[BAND:, :] = vc_ref[...]
        band, band_first = _band_masks_2(t)
        for idx, (sb, h) in enumerate(tiles):
            cs = slice(h * HEAD_DIM, (h + 1) * HEAD_DIM)
            s = _dot(q_ref[sb * BAND:(sb + 1) * BAND, cs], k_all[sb * BAND:(sb + 2) * BAND, cs], "nt") * scale
            s_scr[idx] = jnp.where(band_first if sb == 0 else band, s, NEG)
        lane = lax.broadcasted_iota(jnp.int32, (BAND, HEAD_DIM), 1)
        lse_rows = [jnp.zeros((BAND, HEAD_DIM), F32)] * qb
        for idx, (sb, h) in enumerate(tiles):
            s = s_scr[idx]
            mx = jnp.max(s, axis=-1, keepdims=True)
            p = jnp.exp(s - mx)
            den = jnp.sum(p, axis=-1, keepdims=True)
            p_scr[idx] = p.astype(BF16)
            r_scr[idx] = jnp.broadcast_to(1.0 / den, (BAND, HEAD_DIM))
            lse_rows[sb] = jnp.where(lane == h, jnp.broadcast_to(mx + jnp.log(den), (BAND, HEAD_DIM)), lse_rows[sb])
        for sb in range(qb):
            l_ref[sb * BAND:(sb + 1) * BAND, :] = lse_rows[sb]
        for idx, (sb, h) in enumerate(tiles):
            cs = slice(h * HEAD_DIM, (h + 1) * HEAD_DIM)
            o_ref[sb * BAND:(sb + 1) * BAND, cs] = _dot(p_scr[idx], v_all[sb * BAND:(sb + 2) * BAND, cs], "nn") * r_scr[idx]

    def prev(r, t):
        return jnp.maximum(qb * t - 1, 0)

    cur = lambda c: pl.BlockSpec((None, rows, GROUP_W), lambda r, t, c=c: (r, t, c))
    prv = lambda c: pl.BlockSpec((None, BAND, GROUP_W), lambda r, t, c=c: (r, prev(r, t), c))
    out_spec = lambda width: pl.BlockSpec((None, rows, width), lambda r, t: (r, t, 0))
    shp = lambda width: jax.ShapeDtypeStruct((dil, m_len, width), F32)
    n_t = len(tiles)
    return _pcall(body, name, (dil, steps), [a_g] * 5, [cur(0), cur(1), cur(2), prv(1), prv(2)],
                  [shp(GROUP_W), shp(HEAD_DIM)], [out_spec(GROUP_W), out_spec(HEAD_DIM)],
                  [pltpu.VMEM((rows + BAND, GROUP_W), BF16), pltpu.VMEM((rows + BAND, GROUP_W), BF16),
                   pltpu.VMEM((n_t, BAND, 2 * BAND), F32), pltpu.VMEM((n_t, BAND, 2 * BAND), BF16),
                   pltpu.VMEM((n_t, BAND, HEAD_DIM), F32)], comm=comm)


def _attn_merge(os_, ls_, seq):
    def body(o0, l0, o1, l1, o2, l2, at_ref, lt_ref, sc, lsc):
        for gi, l_r in enumerate((l1, l2)):
            d = DILATIONS[gi + 1]
            for r in range(d):
                lsc.at[gi][pl.ds(r, ROW_BLK // d, stride=d), :] = l_r[r]
        lse = (l0.at[0], lsc.at[0], lsc.at[1])
        lane = lax.broadcasted_iota(jnp.int32, (ROW_BLK, HEAD_DIM), 1)
        lt_rows = jnp.zeros((ROW_BLK, HEAD_DIM), F32)
        for h in range(HEADS_PER_GROUP):
            cs = slice(h * HEAD_DIM, (h + 1) * HEAD_DIM)
            for gi, o_r in enumerate((o1, o2)):
                d = DILATIONS[gi + 1]
                for r in range(d):
                    sc.at[gi][pl.ds(r, ROW_BLK // d, stride=d), :] = o_r[r, :, cs]
            l_h = [v[:, h:h + 1] for v in lse]
            mx = jnp.maximum(jnp.maximum(l_h[0], l_h[1]), l_h[2])
            e = [jnp.exp(v - mx) for v in l_h]
            tot = e[0] + e[1] + e[2]
            inv = 1.0 / tot
            at_ref[:, cs] = ((e[0] * inv) * o0[0, :, cs] + (e[1] * inv) * sc[0] + (e[2] * inv) * sc[1]).astype(BF16)
            lt_rows = jnp.where(lane == h, jnp.broadcast_to(mx + jnp.log(tot), (ROW_BLK, HEAD_DIM)), lt_rows)
        lt_ref[...] = lt_rows

    go, gl = _group_specs(GROUP_W), _group_specs(HEAD_DIM)
    return pl.pallas_call(
        body, name="attn_merge",
        out_shape=[jax.ShapeDtypeStruct((seq, GROUP_W), BF16), jax.ShapeDtypeStruct((seq, HEAD_DIM), F32)],
        grid=(seq // ROW_BLK,), in_specs=[go[0], gl[0], go[1], gl[1], go[2], gl[2]],
        out_specs=[pl.BlockSpec((ROW_BLK, GROUP_W), lambda i: (i, 0)), pl.BlockSpec((ROW_BLK, HEAD_DIM), lambda i: (i, 0))],
        scratch_shapes=[pltpu.VMEM((2, ROW_BLK, HEAD_DIM), F32), pltpu.VMEM((2, ROW_BLK, HEAD_DIM), F32)],
        compiler_params=_params(dimension_semantics=("arbitrary",)),
    )(os_[0], ls_[0], os_[1], ls_[1], os_[2], ls_[2])


def _attn_bwd_prep(dattn, attn, lt):
    seq = dattn.shape[0]

    def body(da_ref, at_ref, lt_ref, cl0, d1, cl1, d2, cl2, sc, csc):
        lane = lax.broadcasted_iota(jnp.int32, (ROW_BLK, HEAD_DIM), 1)
        cl = pltpu.roll(lt_ref[...], HEADS_PER_GROUP, 1)
        for h in range(HEADS_PER_GROUP):
            cs = slice(h * HEAD_DIM, (h + 1) * HEAD_DIM)
            da = da_ref[:, cs].astype(F32)
            cc = jnp.sum(da * at_ref[:, cs].astype(F32), axis=-1, keepdims=True)
            cl = jnp.where(lane == h, jnp.broadcast_to(cc, (ROW_BLK, HEAD_DIM)), cl)
            sc[...] = da
            for g, d_ref in ((1, d1), (2, d2)):
                d = DILATIONS[g]
                for r in range(d):
                    d_ref[r, :, cs] = sc[pl.ds(r, ROW_BLK // d, stride=d), :].astype(BF16)
        cl0[0] = cl
        csc[...] = cl
        for g, c_ref in ((1, cl1), (2, cl2)):
            d = DILATIONS[g]
            for r in range(d):
                c_ref[r] = csc[pl.ds(r, ROW_BLK // d, stride=d), :]

    go, gl = _group_specs(GROUP_W), _group_specs(HEAD_DIM)
    row = lambda width: pl.BlockSpec((ROW_BLK, width), lambda i: (i, 0))
    shape = lambda g, width, dt: jax.ShapeDtypeStruct((DILATIONS[g], seq // DILATIONS[g], width), dt)
    cl0, d1, cl1, d2, cl2 = pl.pallas_call(
        body, name="attn_bwd_prep",
        out_shape=[shape(0, HEAD_DIM, F32), shape(1, GROUP_W, BF16), shape(1, HEAD_DIM, F32), shape(2, GROUP_W, BF16),
                   shape(2, HEAD_DIM, F32)],
        grid=(seq // ROW_BLK,), in_specs=[row(GROUP_W), row(GROUP_W), row(HEAD_DIM)],
        out_specs=[gl[0], go[1], gl[1], go[2], gl[2]],
        scratch_shapes=[pltpu.VMEM((ROW_BLK, HEAD_DIM), F32), pltpu.VMEM((ROW_BLK, HEAD_DIM), F32)],
        compiler_params=_params(dimension_semantics=("arbitrary",)),
    )(dattn, attn, lt)
    return [(dattn[None], cl0), (d1, cl1), (d2, cl2)]


def _attn_bwd(name, a_g, da_g, cl_g, comm=_NO_COMM):
    dil, m_len, _ = a_g.shape
    qb = min(QB, m_len // BAND)
    rows = qb * BAND
    steps = m_len // rows
    scale = HEAD_DIM ** -0.5

    tiles = [(sb, h) for sb in range(qb) for h in range(HEADS_PER_GROUP)]

    def body(q_ref, kc_ref, vc_ref, kp_ref, vp_ref, da_ref, cl_ref, d_ref, dk_acc, dv_acc, car_k, car_v,
             k_all, v_all, s_scr, dp_scr, p_scr, ds_scr):
        tg = pl.program_id(1)
        t = steps - 1 - tg

        @pl.when(tg == 0)
        def _():
            car_k[...] = jnp.zeros_like(car_k)
            car_v[...] = jnp.zeros_like(car_v)

        k_all[0:BAND, :] = kp_ref[...]
        k_all[BAND:, :] = kc_ref[...]
        v_all[0:BAND, :] = vp_ref[...]
        v_all[BAND:, :] = vc_ref[...]
        zero = jnp.zeros((rows, GROUP_W), F32)
        dk_acc[0:rows, :] = zero
        dv_acc[0:rows, :] = zero
        dk_acc[rows:rows + BAND, :] = car_k[...]
        dv_acc[rows:rows + BAND, :] = car_v[...]
        band, band_first = _band_masks_2(t)
        for idx, (sb, h) in enumerate(tiles):
            cs = slice(h * HEAD_DIM, (h + 1) * HEAD_DIM)
            rs, ks = slice(sb * BAND, (sb + 1) * BAND), slice(sb * BAND, (sb + 2) * BAND)
            s_scr[idx] = _dot(q_ref[rs, cs], k_all[ks, cs], "nt")
            dp_scr[idx] = _dot(da_ref[rs, cs], v_all[ks, cs], "nt")
        for idx, (sb, h) in enumerate(tiles):
            cs = slice(h * HEAD_DIM, (h + 1) * HEAD_DIM)
            rs = slice(sb * BAND, (sb + 1) * BAND)
            cc = jnp.broadcast_to(cl_ref[rs, h:h + 1], (BAND, 2 * BAND))
            ltv = jnp.broadcast_to(cl_ref[rs, HEADS_PER_GROUP + h:HEADS_PER_GROUP + h + 1], (BAND, 2 * BAND))
            p = jnp.exp(jnp.where(band_first if sb == 0 else band, s_scr[idx] * scale - ltv, NEG))
            p_scr[idx] = p.astype(BF16)
            ds_scr[idx] = (p * (dp_scr[idx] - cc) * scale).astype(BF16)
        for idx, (sb, h) in enumerate(tiles):
            cs = slice(h * HEAD_DIM, (h + 1) * HEAD_DIM)
            rs, ks = slice(sb * BAND, (sb + 1) * BAND), slice(sb * BAND, (sb + 2) * BAND)
            d_ref[rs, cs] = _dot(ds_scr[idx], k_all[ks, cs], "nn").astype(BF16)
            dk_acc[ks, cs] += _dot(ds_scr[idx], q_ref[rs, cs], "tn")
            dv_acc[ks, cs] += _dot(p_scr[idx], da_ref[rs, cs], "tn")
        d_ref[:, GROUP_W:2 * GROUP_W] = dk_acc[BAND:rows + BAND, :].astype(BF16)
        d_ref[:, 2 * GROUP_W:3 * GROUP_W] = dv_acc[BAND:rows + BAND, :].astype(BF16)
        car_k[...] = dk_acc[0:BAND, :]
        car_v[...] = dv_acc[0:BAND, :]

    def rev(tg):
        return steps - 1 - tg

    def prev(tg):
        return jnp.maximum(qb * rev(tg) - 1, 0)

    cur = lambda c: pl.BlockSpec((None, rows, GROUP_W), lambda r, tg, c=c: (r, rev(tg), c))
    prv = lambda c: pl.BlockSpec((None, BAND, GROUP_W), lambda r, tg, c=c: (r, prev(tg), c))
    return _pcall(
        body, name, (dil, steps), [a_g, a_g, a_g, a_g, a_g, da_g, cl_g],
        [cur(0), cur(1), cur(2), prv(1), prv(2), cur(0), pl.BlockSpec((None, rows, HEAD_DIM), lambda r, tg: (r, rev(tg), 0))],
        [jax.ShapeDtypeStruct((dil, m_len, ATTN_W), BF16)], [pl.BlockSpec((None, rows, ATTN_W), lambda r, tg: (r, rev(tg), 0))],
        [pltpu.VMEM((rows + BAND, GROUP_W), F32), pltpu.VMEM((rows + BAND, GROUP_W), F32),
         pltpu.VMEM((BAND, GROUP_W), F32), pltpu.VMEM((BAND, GROUP_W), F32),
         pltpu.VMEM((rows + BAND, GROUP_W), BF16), pltpu.VMEM((rows + BAND, GROUP_W), BF16),
         pltpu.VMEM((len(tiles), BAND, 2 * BAND), F32), pltpu.VMEM((len(tiles), BAND, 2 * BAND), F32),
         pltpu.VMEM((len(tiles), BAND, 2 * BAND), BF16), pltpu.VMEM((len(tiles), BAND, 2 * BAND), BF16)], comm=comm)


def _dqkv_post(d_gs, tabs, dz):
    seq = dz.shape[0]

    def body(g0, g1, g2, c_ref, s1_ref, s2_ref, dz_any, o_ref, sc):
        del dz_any
        ins = (g0, g1, g2)
        c, s1, s2 = c_ref[...], s1_ref[...], s2_ref[...]
        for part in range(3):
            for hh in range(N_GROUPS * HEADS_PER_GROUP):
                g, hl = divmod(hh, HEADS_PER_GROUP)
                icol = part * GROUP_W + hl * HEAD_DIM
                ocol = part * ATTN_W + hh * HEAD_DIM
                d = DILATIONS[g]
                if d == 1:
                    x = ins[g][0, :, icol:icol + HEAD_DIM].astype(F32)
                else:
                    for r in range(d):
                        sc[pl.ds(r, ROW_BLK // d, stride=d), :] = ins[g][r, :, icol:icol + HEAD_DIM].astype(F32)
                    x = sc[...]
                if part < 2:
                    x = x * c + pltpu.roll(x * s1, HEAD_DIM - ROT_DIM // 2, 1) + pltpu.roll(x * s2, ROT_DIM // 2, 1)
                o_ref[:, ocol:ocol + HEAD_DIM] = x.astype(BF16)

    tab_spec = pl.BlockSpec((ROW_BLK, HEAD_DIM), lambda i: (i, 0))
    return pl.pallas_call(
        body, name="dqkv_post", out_shape=jax.ShapeDtypeStruct(dz.shape, BF16), grid=(seq // ROW_BLK,),
        in_specs=_group_specs(ATTN_W) + [tab_spec, tab_spec, tab_spec, pl.BlockSpec(memory_space=pl.ANY)],
        out_specs=pl.BlockSpec((ROW_BLK, QKV_W), lambda i: (i, 0)),
        scratch_shapes=[pltpu.VMEM((ROW_BLK, HEAD_DIM), F32)], input_output_aliases={6: 0},
        compiler_params=_params(dimension_semantics=("arbitrary",)),
    )(*d_gs, *tabs, dz)


def _glu(zg):
    a = zg[:, :CONV_CH].astype(F32)
    s = _sigmoid(zg[:, CONV_CH:].astype(F32))
    return a, s, a * s


def _shifted_copies(xs):
    n = xs.shape[1] - SUBLANES
    for b in range(1, SUBLANES):
        xs[b, 0:n, :] = xs[0, pl.ds(b, n), :]


def _shifted(xs, offset, r0, cs):
    a, b = divmod(offset, SUBLANES)
    return xs[b, pl.ds(SUBLANES * a + r0, CONV_ROWS), cs]


def _conv_fwd(z, cw, cb, lg, lb, comm=_NO_COMM):
    seq = z.shape[0]
    halo_per_blk = ROW_BLK // CONV_HALO

    def body(zg_ref, zh_ref, cw_ref, cb_ref, lg_ref, lb_ref, c2_ref, c4_ref, xs):
        i = pl.program_id(0)
        _, _, c1 = _glu(zg_ref[...])
        _, _, c1h = _glu(zh_ref[...])
        xs[0, 0:CONV_HALO, :] = jnp.where(i > 0, c1h, 0.0)
        xs[0, CONV_HALO:, :] = c1
        _shifted_copies(xs)
        for s in range(CONV_CH // HEAD_DIM):
            cs = slice(s * HEAD_DIM, (s + 1) * HEAD_DIM)
            taps = [cw_ref[j:j + 1, cs] for j in range(CONV_K)]
            bias = cb_ref[:, cs]

            def chunk(rc, carry, cs=cs, taps=taps, bias=bias):
                r0 = pl.multiple_of(rc * CONV_ROWS, CONV_ROWS)
                acc = [jnp.zeros((CONV_ROWS, HEAD_DIM), F32)] * 2
                for j in range(CONV_K):
                    acc[j % 2] = acc[j % 2] + taps[j] * _shifted(xs, CONV_HALO - (CONV_K - 1) + j, r0, cs)
                c2_ref[pl.ds(r0, CONV_ROWS), cs] = acc[0] + acc[1] + bias
                return carry

            lax.fori_loop(0, ROW_BLK // CONV_ROWS, chunk, 0)
        c2 = c2_ref[...]
        mu = jnp.mean(c2, axis=-1, keepdims=True)
        xc = c2 - mu
        rstd = lax.rsqrt(jnp.mean(xc * xc, axis=-1, keepdims=True) + EPS)
        c3 = xc * rstd * lg_ref[...] + lb_ref[...]
        c4_ref[...] = (c3 * _sigmoid(c3)).astype(BF16)

    vec = pl.BlockSpec((1, CONV_CH), lambda i: (0, 0))
    return _pcall(
        body, "conv_fwd", (seq // ROW_BLK,), [z, z, cw, cb, lg, lb],
        [pl.BlockSpec((ROW_BLK, 2 * CONV_CH), lambda i: (i, GLU_COL_BLK)),
         pl.BlockSpec((CONV_HALO, 2 * CONV_CH), lambda i: (jnp.maximum(i * halo_per_blk - 1, 0), GLU_COL_BLK)),
         pl.BlockSpec((CONV_HALO, CONV_CH), lambda i: (0, 0)), vec, vec, vec],
        [jax.ShapeDtypeStruct((seq, CONV_CH), F32), jax.ShapeDtypeStruct((seq, CONV_CH), BF16)],
        [pl.BlockSpec((ROW_BLK, CONV_CH), lambda i: (i, 0)), pl.BlockSpec((ROW_BLK, CONV_CH), lambda i: (i, 0))],
        [pltpu.VMEM((SUBLANES, ROW_BLK + CONV_HALO, CONV_CH), F32)], comm=comm)


def _conv_bwd(dc2, z, cw, dz, comm=_NO_COMM):
    seq = z.shape[0]
    halo_per_blk = ROW_BLK // CONV_HALO
    n_blk = seq // ROW_BLK
    last_halo = seq // CONV_HALO - 1

    def body(dc_ref, dn_ref, zg_ref, zh_ref, cw_ref, dz_any, o_ref, dcw_ref, xs, ys, dc1_ref, dcw_acc):
        del dz_any
        i = pl.program_id(0)
        a, s, c1 = _glu(zg_ref[...])
        _, _, c1h = _glu(zh_ref[...])
        xs[0, 0:CONV_HALO, :] = jnp.where(i > 0, c1h, 0.0)
        xs[0, CONV_HALO:, :] = c1
        ys[0, 0:ROW_BLK, :] = dc_ref[...]
        ys[0, ROW_BLK:, :] = jnp.where(i < n_blk - 1, dn_ref[...], 0.0)
        _shifted_copies(xs)
        _shifted_copies(ys)

        @pl.when(i == 0)
        def _():
            dcw_acc[...] = jnp.zeros_like(dcw_acc)

        for sl in range(CONV_CH // HEAD_DIM):
            cs = slice(sl * HEAD_DIM, (sl + 1) * HEAD_DIM)
            taps = [cw_ref[j:j + 1, cs] for j in range(CONV_K)]

            def chunk(rc, carry, cs=cs, taps=taps):
                r0 = pl.multiple_of(rc * CONV_ROWS, CONV_ROWS)
                dc = ys[0, pl.ds(r0, CONV_ROWS), cs]
                acc = [jnp.zeros((CONV_ROWS, HEAD_DIM), F32)] * 2
                for j in range(CONV_K):
                    prod = dc * _shifted(xs, CONV_HALO - (CONV_K - 1) + j, r0, cs)
                    dcw_acc[j, :, cs] += jnp.sum(prod.reshape(CONV_ROWS // SUBLANES, SUBLANES, HEAD_DIM), axis=0)
                    acc[j % 2] = acc[j % 2] + taps[j] * _shifted(ys, CONV_K - 1 - j, r0, cs)
                dc1_ref[pl.ds(r0, CONV_ROWS), cs] = acc[0] + acc[1]
                return carry

            lax.fori_loop(0, ROW_BLK // CONV_ROWS, chunk, 0)
        dc1 = dc1_ref[...]
        o_ref[:, :CONV_CH] = (dc1 * s).astype(BF16)
        o_ref[:, CONV_CH:] = (dc1 * a * s * (1.0 - s)).astype(BF16)

        @pl.when(i == n_blk - 1)
        def _():
            dcw_ref[...] = jnp.sum(dcw_acc[...], axis=1)

    return _pcall(
        body, "conv_bwd", (n_blk,), [dc2, dc2, z, z, cw, dz],
        [pl.BlockSpec((ROW_BLK, CONV_CH), lambda i: (i, 0)),
         pl.BlockSpec((CONV_HALO, CONV_CH), lambda i: (jnp.minimum((i + 1) * halo_per_blk, last_halo), 0)),
         pl.BlockSpec((ROW_BLK, 2 * CONV_CH), lambda i: (i, GLU_COL_BLK)),
         pl.BlockSpec((CONV_HALO, 2 * CONV_CH), lambda i: (jnp.maximum(i * halo_per_blk - 1, 0), GLU_COL_BLK)),
         pl.BlockSpec((CONV_HALO, CONV_CH), lambda i: (0, 0)),
         pl.BlockSpec(memory_space=pl.ANY)],
        [jax.ShapeDtypeStruct(dz.shape, BF16), jax.ShapeDtypeStruct((CONV_HALO, CONV_CH), F32)],
        [pl.BlockSpec((ROW_BLK, 2 * CONV_CH), lambda i: (i, GLU_COL_BLK)), pl.BlockSpec((CONV_HALO, CONV_CH), lambda i: (0, 0))],
        [pltpu.VMEM((SUBLANES, ROW_BLK + CONV_HALO, CONV_CH), F32), pltpu.VMEM((SUBLANES, ROW_BLK + CONV_HALO, CONV_CH), F32),
         pltpu.VMEM((ROW_BLK, CONV_CH), F32), pltpu.VMEM((CONV_HALO, SUBLANES, CONV_CH), F32)],
        aliases={5: 0}, comm=comm)


def _epi_mix(ya, c4, wcp, gates, bg):
    yc = _dot(c4, wcp, "nn")
    gv = _sigmoid(gates.astype(F32) + bg)
    merged = gv[:, :D_MODEL] * ya + gv[:, D_MODEL:] * yc
    return merged, ya, yc


def _epi_residual_rms(acc, xres, g):
    x = xres + acc
    return x, _rms_fwd_vals(x, g)


def _cross_scores(cq, ck):
    out = []
    for h in range(CROSS_HEADS):
        cs = slice(h * CROSS_HD, (h + 1) * CROSS_HD)
        s = _dot(cq[:, cs], ck[:, cs], "nt") * (CROSS_HD ** -0.5)
        e = jnp.exp(s - jnp.max(s, axis=-1, keepdims=True))
        out.append((cs, e, jnp.sum(e, axis=-1, keepdims=True)))
    return out


def _epi_cross_fwd(acc, ck, cv):
    cq = acc.astype(BF16)
    co = [_dot(e, cv[:, cs], "nn") / den for cs, e, den in _cross_scores(cq, ck)]
    return cq, jnp.concatenate(co, axis=1)


def _epi_cross_bwd(dco, cq, ck, cv):
    dco = dco.astype(BF16)
    dcq, dck, dcv = [], [], []
    for cs, e, den in _cross_scores(cq, ck):
        p = e / den
        dp = _dot(dco[:, cs], cv[:, cs], "nt")
        ds = (p * (dp - jnp.sum(dp * p, axis=-1, keepdims=True)) * (CROSS_HD ** -0.5)).astype(BF16)
        dcq.append(_dot(ds, ck[:, cs], "nn"))
        dck.append(_dot(ds, cq[:, cs], "tn"))
        dcv.append(_dot(p, dco[:, cs], "tn"))
    return jnp.concatenate(dcq, axis=1), jnp.concatenate(dck, axis=1), jnp.concatenate(dcv, axis=1)


def _epi_mlp_up(acc):
    return acc, jnp.square(jnp.maximum(acc, 0.0))


def _epi_final(acc, x2, tgt, g):
    x3 = x2 + acc
    err = _rms_fwd_vals(x3, g) - tgt
    loss = (0.5 / D_MODEL) * jnp.sum(err * err)
    dx3, dg = _rms_bwd_vals(x3, g, err * (1.0 / D_MODEL))
    return dx3, jnp.full((1, HEAD_DIM), loss, F32), dg


def _epi_mlp_down_bwd(dh, hpre):
    return (dh * 2.0 * jnp.maximum(hpre.astype(F32), 0.0),)


def _epi_rms_bwd(du, x, g, dres):
    dx, dg = _rms_bwd_vals(x, g, du)
    return dres.astype(F32) + dx, dg


def _epi_rms_bwd_g(du, x, g):
    return (_rms_bwd_vals(x, g, du)[1],)


def _epi_mix_bwd(dm, ya, yc, gates, bg):
    gv = _sigmoid(gates.astype(F32) + bg)
    ga, gb = gv[:, :D_MODEL], gv[:, D_MODEL:]
    ya, yc = ya.astype(F32), yc.astype(F32)
    dgate = jnp.concatenate([dm * ya * ga * (1.0 - ga), dm * yc * gb * (1.0 - gb)], axis=1)
    return dm * ga, dm * gb, dgate, jnp.sum(dgate, axis=0, keepdims=True)


def _epi_ln_bwd(dc4, c2, lg, lb):
    mu = jnp.mean(c2, axis=-1, keepdims=True)
    xc = c2 - mu
    rstd = lax.rsqrt(jnp.mean(xc * xc, axis=-1, keepdims=True) + EPS)
    xh = xc * rstd
    c3 = xh * lg + lb
    sg = _sigmoid(c3)
    dc3 = dc4 * sg * (1.0 + c3 * (1.0 - sg))
    dxh = dc3 * lg
    dc2 = rstd * (dxh - jnp.mean(dxh, axis=-1, keepdims=True) - xh * jnp.mean(dxh * xh, axis=-1, keepdims=True))
    return (dc2, jnp.sum(dc3 * xh, axis=0, keepdims=True), jnp.sum(dc3, axis=0, keepdims=True),
            jnp.sum(dc2, axis=0, keepdims=True))


def _sds(shape, dtype):
    return jax.ShapeDtypeStruct(shape, dtype)


class _Lazy:
    def __init__(self, fn):
        self.fn = fn

    def __getitem__(self, key):
        return self.fn(key)


def _local_step(x, mem, tgt, sm, plan):
    w = _Lazy(plan.w)
    dw = {}

    def carry(name, n_own, fn, *args, **kw):
        c = plan.comm(name, dw)
        res = fn(*args, comm=c, **kw)
        plan.done(name, res[n_own:])
        return res[:n_own]

    def mm(name, *args, **kw):
        return carry(name, len(args[6]), _mm, name, *args, **kw)

    seq = x.shape[0]
    nr = seq // ROW_BLK
    big = min(1024, seq)
    nb = seq // big
    row = lambda n: ((ROW_BLK, n), lambda i, j, k: (i, 0))
    vec = lambda n: ((1, n), lambda i, j, k: (0, 0))
    full = lambda r, c: ((r, c), lambda i, j, k: (0, 0))
    gates_blk = ((ROW_BLK, 2 * D_MODEL), lambda i, j, k: (i, GATE_COL_BLK))
    tabs = _rope_tables(seq)

    u = carry("rms_mix", 1, _rms_fwd, "rms_mix", x, sm["g_mix"], ROW_BLK)[0]
    whole3 = lambda a: (a.shape, lambda i, j, k: (0, 0, 0))
    z = plan.project_in(u)
    a_gs = carry("qkv_prep", 3, _qkv_prep, z, tabs)
    os_, ls_ = [], []
    for g in range(N_GROUPS):
        name = "attn_fwd_%d" % g
        o_g, l_g = carry(name, 2, _attn_fwd, name, a_gs[g])
        os_.append(o_g)
        ls_.append(l_g)
    attn, lt = _attn_merge(os_, ls_, seq)
    c2, c4 = carry("conv_fwd", 2, _conv_fwd, z, w["taps"], sm["conv_b"], sm["conv_ln_g"], sm["conv_ln_b"])
    merged, ya, yc = mm(
        "mix", attn, w["w_attn_proj"], "nn", (nr, 1, 1), row(GROUP_W), full(GROUP_W, D_MODEL),
        [(_sds((seq, D_MODEL), BF16), *row(D_MODEL))] * 3,
        extras=[(c4, *row(CONV_CH)), (w["w_conv_proj"], *full(CONV_CH, D_MODEL)), (z, *gates_blk), (sm["b_gate"], *vec(2 * D_MODEL))],
        epi=_epi_mix)
    x1, uq = mm("out_proj", merged, w["w_out"], "nn", (nr, 1, 1), row(D_MODEL), full(D_MODEL, D_MODEL),
                 [(_sds((seq, D_MODEL), F32), *row(D_MODEL)), (_sds((seq, D_MODEL), BF16), *row(D_MODEL))],
                 extras=[(x, *row(D_MODEL)), (sm["g_cross"], *vec(D_MODEL))], epi=_epi_residual_rms)

    mn = _rms_fwd("rms_mem", mem, sm["g_mem"], N_MEM)[0]
    ckv = mm("ckv_proj", mn, w["w_ckv"], "nn", (1, N_DEV, 1), full(N_MEM, D_MODEL),
              ((None, D_MODEL, 2 * D_MODEL // N_DEV), lambda i, j, k: (j, 0, 0)),
              [(_sds((N_MEM, 2 * D_MODEL), BF16), (N_MEM, 2 * D_MODEL // N_DEV), lambda i, j, k: (0, j))])[0]
    ck, cv = ckv[:, :D_MODEL], ckv[:, D_MODEL:]
    kv_blk = full(N_MEM, D_MODEL)
    cq, co = mm("cq_proj_cross", uq, w["w_cq"], "nn", (nr, 1, 1), row(D_MODEL), full(D_MODEL, D_MODEL),
                 [(_sds((seq, D_MODEL), BF16), *row(D_MODEL))] * 2,
                 extras=[(ck, *kv_blk), (cv, *kv_blk)], epi=_epi_cross_fwd)
    x2, um = mm("co_proj", co, w["w_co"], "nn", (nr, 1, 1), row(D_MODEL), full(D_MODEL, D_MODEL),
                 [(_sds((seq, D_MODEL), F32), *row(D_MODEL)), (_sds((seq, D_MODEL), BF16), *row(D_MODEL))],
                 extras=[(x1, *row(D_MODEL)), (sm["g_mlp"], *vec(D_MODEL))], epi=_epi_residual_rms)

    ff_blk = D_FF // N_DEV
    row_f32 = (_sds((seq, D_MODEL), F32), *row(D_MODEL))
    row_bf16 = (_sds((seq, D_MODEL), BF16), *row(D_MODEL))
    col_sum = (_sds((1, D_MODEL), F32), *vec(D_MODEL))
    hpre, h = mm("mlp_up", um, w["w_up"], "nn", (nr, 1, 1), row(D_MODEL), whole3(w["w_up"]),
                 [(_sds((seq, D_FF), BF16), *row(D_FF))] * 2, epi=_epi_mlp_up, split=("cols", N_DEV), b_resident=True)
    kt = D_FF // D_MODEL
    dx3, loss, dg_final = mm(
        "mlp_down_loss", h, w["w_down"], "nn", (nr, 1, 1), row(D_FF), full(D_FF, D_MODEL),
        [row_bf16, (_sds((1, HEAD_DIM), F32), *vec(HEAD_DIM)), col_sum],
        extras=[(x2, *row(D_MODEL)), (tgt, *row(D_MODEL)), (sm["g_final"], *vec(D_MODEL))], epi=_epi_final, acc_outs=(1, 2),
        b_resident=True)

    dhpre = mm("mlp_down_bwd", dx3, w["w_down"], "nt", (nr, 1, 1), row(D_MODEL), full(D_FF, D_MODEL),
               [(_sds((seq, D_FF), BF16), *row(D_FF))], extras=[(hpre, *row(D_FF))], epi=_epi_mlp_down_bwd,
               split=("cols", kt), b_resident=True)[0]
    big2 = min(2 * big, seq)
    nb2 = seq // big2
    dw["w_down"] = mm("dw_down", h, dx3, "tn", (kt, 1, nb2), ((big2, D_MODEL), lambda i, j, k: (k, i)),
                      ((big2, D_MODEL), lambda i, j, k: (k, 0)),
                      [(_sds((D_FF, D_MODEL), BF16), (D_MODEL, D_MODEL), lambda i, j, k: (i, 0))])[0]
    dx2, dg_mlp = mm("mlp_up_bwd", dhpre, w["w_up"], "nt", (nr, 1, 1), row(D_FF), whole3(w["w_up"]),
                     [row_bf16, col_sum],
                     extras=[(x2, *row(D_MODEL)), (sm["g_mlp"], *vec(D_MODEL)), (dx3, *row(D_MODEL))],
                     epi=_epi_rms_bwd, acc_outs=(1,), split=("sum", N_DEV), b_resident=True)
    dw["w_up"] = mm("dw_up", um, dhpre, "tn", (1, N_DEV, nb2), ((big2, D_MODEL), lambda i, j, k: (k, 0)),
                    ((big2, ff_blk), lambda i, j, k: (k, j)),
                    [(_sds((N_DEV, D_MODEL, ff_blk), BF16), (None, D_MODEL, ff_blk), lambda i, j, k: (j, 0, 0))])[0]

    acc_kv = (_sds((N_MEM, D_MODEL), F32), *kv_blk)
    dcq, dck, dcv = mm("co_proj_bwd_cross", dx2, w["w_co"], "nt", (nr, 1, 1), row(D_MODEL), full(D_MODEL, D_MODEL),
                       [row_bf16, acc_kv, acc_kv],
                       extras=[(cq, *row(D_MODEL)), (ck, *kv_blk), (cv, *kv_blk)], epi=_epi_cross_bwd, acc_outs=(1, 2))

    def dw_square(name, act, grad):
        return mm(name, act, grad, "tn", (1, 1, nb2), ((big2, D_MODEL), lambda i, j, k: (k, 0)),
                  ((big2, D_MODEL), lambda i, j, k: (k, 0)), [(_sds((D_MODEL, D_MODEL), BF16), *full(D_MODEL, D_MODEL))])[0]

    dw["w_co"] = dw_square("dw_co", co, dx2)
    dx1, dg_cross = mm("cq_proj_bwd", dcq, w["w_cq"], "nt", (nr, 1, 1), row(D_MODEL), full(D_MODEL, D_MODEL),
                       [row_bf16, col_sum],
                       extras=[(x1, *row(D_MODEL)), (sm["g_cross"], *vec(D_MODEL)), (dx2, *row(D_MODEL))],
                       epi=_epi_rms_bwd, acc_outs=(1,))
    dw["w_cq"] = dw_square("dw_cq", uq, dcq)
    dckv = jnp.concatenate([dck, dcv], axis=1)
    kv_chunk = 2 * D_MODEL // N_DEV
    dw["w_ckv"] = mm("dw_ckv", mn, dckv, "tn", (1, N_DEV, 1), full(N_MEM, D_MODEL), ((N_MEM, kv_chunk), lambda i, j, k: (0, j)),
                      [(_sds((N_DEV, D_MODEL, kv_chunk), BF16), (None, D_MODEL, kv_chunk), lambda i, j, k: (j, 0, 0))])[0]
    dg_mem = mm("ckv_proj_bwd", dckv, w["w_ckv"], "nt", (1, 1, N_DEV), ((N_MEM, kv_chunk), lambda i, j, k: (0, k)),
                 ((None, D_MODEL, kv_chunk), lambda i, j, k: (k, 0, 0)), [(_sds((1, D_MODEL), F32), *vec(D_MODEL))],
                 extras=[(mem, *full(N_MEM, D_MODEL)), (sm["g_mem"], *vec(D_MODEL))], epi=_epi_rms_bwd_g, acc_outs=(0,))[0]

    dya, dyc, dz, db_gate = mm(
        "out_proj_bwd_mix", dx1, w["w_out"], "nt", (nr, 1, 1), row(D_MODEL), full(D_MODEL, D_MODEL),
        [(_sds((seq, D_MODEL), BF16), *row(D_MODEL)), (_sds((seq, D_MODEL), BF16), *row(D_MODEL)),
         (_sds((seq, IN_W), BF16), *gates_blk), (_sds((1, 2 * D_MODEL), F32), *vec(2 * D_MODEL))],
        extras=[(ya, *row(D_MODEL)), (yc, *row(D_MODEL)), (z, *gates_blk), (sm["b_gate"], *vec(2 * D_MODEL))],
        epi=_epi_mix_bwd, acc_outs=(3,))
    dw["w_out"] = dw_square("dw_out", merged, dx1)
    dattn = mm("attn_proj_bwd", dya, w["w_attn_proj"], "nt", (nr, 1, 1), row(D_MODEL), full(GROUP_W, D_MODEL),
                [(_sds((seq, GROUP_W), BF16), *row(GROUP_W))])[0]
    pc = D_MODEL // N_DEV
    dw["w_attn_proj"] = mm("dw_attn_proj", attn, dya, "tn", (1, 1, nb2), ((big2, GROUP_W), lambda i, j, k: (k, 0)),
                           ((big2, D_MODEL), lambda i, j, k: (k, 0)),
                           [(_sds((N_DEV, GROUP_W, pc), BF16), (N_DEV, GROUP_W, pc), lambda i, j, k: (0, 0, 0))],
                           out_chunks=N_DEV)[0]
    cvec = (_sds((1, CONV_CH), F32), *vec(CONV_CH))
    dc2, dg_ln_g, dg_ln_b, dg_conv_b = mm(
        "conv_proj_bwd_ln", dyc, w["w_conv_proj"], "nt", (nr, 1, 1), row(D_MODEL), full(CONV_CH, D_MODEL),
        [(_sds((seq, CONV_CH), F32), *row(CONV_CH)), cvec, cvec, cvec],
        extras=[(c2, *row(CONV_CH)), (sm["conv_ln_g"], *vec(CONV_CH)), (sm["conv_ln_b"], *vec(CONV_CH))],
        epi=_epi_ln_bwd, acc_outs=(1, 2, 3))
    dw["w_conv_proj"] = mm("dw_conv_proj", c4, dyc, "tn", (1, 1, nb2), ((big2, CONV_CH), lambda i, j, k: (k, 0)),
                           ((big2, D_MODEL), lambda i, j, k: (k, 0)),
                           [(_sds((N_DEV, CONV_CH, pc), BF16), (N_DEV, CONV_CH, pc), lambda i, j, k: (0, 0, 0))],
                           out_chunks=N_DEV)[0]
    dz, dg_conv_w = carry("conv_bwd", 2, _conv_bwd, dc2, z, w["taps"], dz)
    preps = _attn_bwd_prep(dattn, attn, lt)
    d_gs = []
    for g in range(N_GROUPS):
        name = "attn_bwd_%d" % g
        d_gs.append(carry(name, 1, _attn_bwd, name, a_gs[g], *preps[g])[0])
    dz = _dqkv_post(d_gs, tabs, dz)
    half = D_MODEL // 2
    for part, name in enumerate(("w_in_a", "w_in_b")):
        dw[name] = mm("d" + name, u, dz, "tn", (1, N_DEV, nb2), ((big2, half), lambda i, j, k, part=part: (k, part)),
                      ((big2, D_MODEL), lambda i, j, k: (k, j)),
                      [(_sds((N_DEV, half, D_MODEL), BF16), (None, half, D_MODEL), lambda i, j, k: (j, 0, 0))])[0]
    grad_x, dg_mix = mm("in_proj_bwd", dz, w["w_in"], "nt", (nr, 1, 1), row(IN_W), whole3(w["w_in"]), [row_f32, col_sum],
                        extras=[(x, *row(D_MODEL)), (sm["g_mix"], *vec(D_MODEL)), (dx1, *row(D_MODEL))],
                        epi=_epi_rms_bwd, acc_outs=(1,), split=("sum", N_DEV), b_resident=True)
    small = dict(g_mix=dg_mix, b_gate=db_gate, conv_b=dg_conv_b, conv_ln_g=dg_ln_g, conv_ln_b=dg_ln_b, g_cross=dg_cross,
                 g_mem=dg_mem, g_mlp=dg_mlp, g_final=dg_final, loss=loss, conv_w=dg_conv_w)
    return grad_x, dw, small


SHARD_SHAPE = dict(w_in_a=(512, 1024), w_in_b=(512, 1024), w_attn_proj=(512, 128), w_conv_proj=(768, 128), w_out=(128, 1024),
                   w_cq=(128, 1024), w_ckv=(1024, 256), w_co=(128, 1024), w_up=(1024, 512), w_down=(512, 1024))
FWD_CARRY = {"in_proj": ("taps",), "qkv_prep": ("w_attn_proj", "w_conv_proj", "w_out", "w_cq", "w_ckv", "w_co"),
             "conv_fwd": ("w_up", "w_down")}
BWD_CARRY = {"dw_up": ("w_down",), "out_proj_bwd_mix": ("w_co", "w_cq"), "conv_bwd": ("w_up", "w_ckv"),
             "attn_bwd_0": ("w_out",), "attn_bwd_1": ("w_attn_proj", "w_conv_proj"), "dw_in_b": ("w_in_a",),
             "in_proj_bwd": ("w_in_b",)}


def _cols_to_2d(a):
    return a.transpose(1, 0, 2).reshape(a.shape[1], -1)


def _in_proj_gather(u, w_shard, comm):
    seq = u.shape[0]
    tm = min(1024, seq)
    x, y, c = lax.axis_index("x"), lax.axis_index("y"), lax.axis_index("c")
    ident = lambda px, py, pc: 4 * px + 2 * py + pc
    far = [(1 - x, y), (x, 1 - y), (1 - x, 1 - y)]
    order = jnp.stack([ident(x, y, c), ident(x, y, 1 - c), ident(*far[0], c), ident(*far[1], c), ident(*far[0], 1 - c),
                       ident(*far[1], 1 - c), ident(*far[2], c), ident(*far[2], 1 - c)]).astype(jnp.int32)
    forward_at = {2: 0, 3: 1, 6: 2}
    n_far = len(far)

    def body(order_ref, u_ref, wsh_ref, *rest):
        c_in, z_ref, wg_ref = rest[:comm.n], rest[comm.n], rest[comm.n + 1]
        c_out = rest[comm.n + 2:2 * comm.n + 2]
        wbuf, load_sem, local_sem, recv_sems, ici_send, d2d_send = rest[2 * comm.n + 2:2 * comm.n + 8]
        sems = rest[2 * comm.n + 8:]
        jj, i = pl.program_id(0), pl.program_id(1)
        (kx, ky, kc), me, chips = _Comm._where()
        sibling = (kx, ky, 1 - kc)
        n = order_ref[jj]

        def push(src, blk, send, to):
            return pltpu.make_async_remote_copy(src_ref=src, dst_ref=wg_ref.at[blk], send_sem=send,
                                                recv_sem=recv_sems.at[blk], device_id=to, device_id_type=MESH)

        def load(src):
            cp = pltpu.make_async_copy(src, wbuf, load_sem)
            cp.start()
            cp.wait()

        @pl.when(jnp.logical_and(jj == 0, i == 0))
        def _():
            push(wsh_ref, me, d2d_send, sibling).start()
            for (px, py) in chips[:2]:
                push(wsh_ref, me, ici_send, (px, py, kc)).start()
            pltpu.make_async_copy(wsh_ref, wg_ref.at[me], local_sem).start()
            load(wsh_ref)

        @pl.when(jnp.logical_and(jj > 0, i == 0))
        def _():
            push(wg_ref.at[n], n, d2d_send, sibling).wait_recv()
            for step, k in forward_at.items():
                @pl.when(jj == step)
                def _(k=k):
                    blk = 4 * chips[k][0] + 2 * chips[k][1] + kc
                    push(wg_ref.at[blk], blk, d2d_send, sibling).start()

            @pl.when(jj == 2)
            def _():
                push(wsh_ref, me, ici_send, (*chips[2], kc)).start()

            if comm.n:
                @pl.when(jj == 3)
                def _():
                    comm.start(c_in, c_out, sems)

            load(wg_ref.at[n])

        z_ref[...] = _dot(u_ref[...], wbuf[...], "nn").astype(BF16)

        @pl.when(jnp.logical_and(jj == N_DEV - 1, i == pl.num_programs(1) - 1))
        def _():
            def drain_sends(send, count):
                blocks = wg_ref.at[pl.ds(0, count)]
                pltpu.make_async_remote_copy(src_ref=blocks, dst_ref=blocks, send_sem=send, recv_sem=recv_sems.at[0],
                                             device_id=sibling, device_id_type=MESH).wait_send()

            drain_sends(ici_send, n_far)
            drain_sends(d2d_send, n_far + 1)
            pltpu.make_async_copy(wsh_ref, wg_ref.at[me], local_sem).wait()
            if comm.n:
                comm.wait(c_in, c_out, sems)

    any_spec = pl.BlockSpec(memory_space=pl.ANY)
    grid_spec = pltpu.PrefetchScalarGridSpec(
        num_scalar_prefetch=1, grid=(N_DEV, seq // tm),
        in_specs=[pl.BlockSpec((tm, D_MODEL), lambda jj, i, order_ref: (i, 0)), any_spec] + comm.in_specs,
        out_specs=[pl.BlockSpec((tm, D_MODEL), lambda jj, i, order_ref: (i, order_ref[jj])), any_spec] + comm.out_specs,
        scratch_shapes=[pltpu.VMEM((D_MODEL, D_MODEL), BF16), pltpu.SemaphoreType.DMA, pltpu.SemaphoreType.DMA,
                        pltpu.SemaphoreType.DMA((N_DEV,)), pltpu.SemaphoreType.DMA, pltpu.SemaphoreType.DMA] + comm.scratch)
    return pl.pallas_call(
        body, name="in_proj_gather", grid_spec=grid_spec,
        out_shape=[jax.ShapeDtypeStruct((seq, IN_W), BF16), jax.ShapeDtypeStruct((N_DEV, D_MODEL, D_MODEL), BF16)] + comm.out_shape,
        compiler_params=_params(dimension_semantics=("arbitrary", "arbitrary")),
    )(order, u, w_shard, *comm.arrays)


class _Plan:
    def __init__(self, shards, n_tap_cols):
        self.shards = shards
        self.gathered = {}
        self.parts = {}
        self.n_tap_cols = n_tap_cols

    def project_in(self, u):
        comm = self.comm("in_proj", None)
        res = _in_proj_gather(u, self.shards["w_in"], comm)
        self.gathered["w_in"] = res[1]
        self.done("in_proj", res[2:])
        return res[0]

    def comm(self, name, dw):
        if name in FWD_CARRY:
            return _Comm(replicated=[self.shards[k] for k in FWD_CARRY[name]])
        if name in BWD_CARRY:
            return _Comm(chunked=[dw[k].reshape((N_DEV,) + SHARD_SHAPE[k]) for k in BWD_CARRY[name]])
        return _NO_COMM

    def done(self, name, got):
        if name in FWD_CARRY:
            self.gathered.update(zip(FWD_CARRY[name], got))
        elif name in BWD_CARRY:
            self.parts.update(zip(BWD_CARRY[name], got))

    def w(self, key):
        g = self.gathered[key]
        if key in ("w_in", "w_up", "w_ckv"):
            return g
        if key in ("w_attn_proj", "w_conv_proj"):
            return _cols_to_2d(g)
        if key == "taps":
            return jnp.pad(_cols_to_2d(g[:, :CONV_K, :self.n_tap_cols]), ((0, 1), (0, 0)))
        return g.reshape(-1, g.shape[-1])


def _adamw(name, w, m, v, parts, comm=_NO_COMM):
    stack = list(parts) if isinstance(parts, (list, tuple)) else [parts]
    rows, cols = w.shape
    n_parts = stack[0].shape[0]
    stack_rows = rows // len(stack)
    rb = stack_rows if stack_rows <= 256 or stack_rows % 256 else 256
    per_stack = stack_rows // rb

    def body(w_ref, m_ref, v_ref, *rest):
        p_refs, (g_ref, d_ref, nm_ref, nv_ref) = rest[:len(stack)], rest[len(stack):]
        i = pl.program_id(0)

        def total(p_ref):
            acc = p_ref[0].astype(F32)
            for q in range(1, n_parts):
                acc = acc + p_ref[q].astype(F32)
            return acc

        g = total(p_refs[0])
        for s in range(1, len(stack)):
            g = jnp.where(i >= s * per_stack, total(p_refs[s]), g)
        wv = w_ref[...]
        nm = ADAM_B1 * m_ref[...] + (1.0 - ADAM_B1) * g
        nv = ADAM_B2 * v_ref[...] + (1.0 - ADAM_B2) * jnp.square(g)
        m_hat = nm / (1.0 - ADAM_B1 ** ADAM_STEP)
        v_hat = nv / (1.0 - ADAM_B2 ** ADAM_STEP)
        g_ref[...] = g
        d_ref[...] = -ADAM_LR * (m_hat / (jnp.sqrt(v_hat) + ADAM_EPS) + ADAM_WD * wv)
        nm_ref[...] = nm
        nv_ref[...] = nv

    blk = pl.BlockSpec((rb, cols), lambda i: (i, 0))
    part_specs = [pl.BlockSpec((n_parts, rb, cols), lambda i, s=s: (0, jnp.clip(i - s * per_stack, 0, per_stack - 1), 0))
                  for s in range(len(stack))]
    return _pcall(body, name, (rows // rb,), [w, m, v] + stack, [blk, blk, blk] + part_specs,
                  [jax.ShapeDtypeStruct((rows, cols), F32)] * 4, [blk] * 4, comm=comm)


def _sum_parts(name, parts):
    def body(p_ref, o_ref):
        acc = p_ref[0]
        for q in range(1, parts.shape[0]):
            acc = acc + p_ref[q]
        o_ref[...] = acc

    return _pcall(body, name, (1,), [parts], [pl.BlockSpec(parts.shape, lambda i: (0, 0, 0))],
                  [jax.ShapeDtypeStruct(parts.shape[1:], F32)], [pl.BlockSpec(parts.shape[1:], lambda i: (0, 0))])[0]


BIG = ("w_in", "w_attn_proj", "w_conv_proj", "w_out", "w_cq", "w_ckv", "w_co", "w_up", "w_down")
SMALL = ("g_mix", "b_gate", "conv_b", "conv_ln_g", "conv_ln_b", "g_cross", "g_mem", "g_mlp", "g_final")
SMALL_ORDER = SMALL + ("loss", "conv_w")
WEIGHTS = ("g_mix", "w_in", "b_gate", "conv_w", "conv_b", "conv_ln_g", "conv_ln_b", "w_attn_proj", "w_conv_proj", "w_out",
           "g_cross", "g_mem", "w_cq", "w_ckv", "w_co", "g_mlp", "w_up", "w_down", "g_final")


def kernel(x, mem, g_mix, w_in, b_gate, conv_w, conv_b, conv_ln_g, conv_ln_b, w_attn_proj, w_conv_proj, w_out, g_cross, g_mem, w_cq, w_ckv, w_co, g_mlp, w_up, w_down, g_final, loss_target, m_g_mix, m_w_in, m_b_gate, m_conv_w, m_conv_b, m_conv_ln_g, m_conv_ln_b, m_w_attn_proj, m_w_conv_proj, m_w_out, m_g_cross, m_g_mem, m_w_cq, m_w_ckv, m_w_co, m_g_mlp, m_w_up, m_w_down, m_g_final, v_g_mix, v_w_in, v_b_gate, v_conv_w, v_conv_b, v_conv_ln_g, v_conv_ln_b, v_w_attn_proj, v_w_conv_proj, v_w_out, v_g_cross, v_g_mem, v_w_cq, v_w_ckv, v_w_co, v_g_mlp, v_w_up, v_w_down, v_g_final):
    args = dict(locals())
    wts = {k: args[k] for k in WEIGHTS}
    mom = {k: args["m_" + k] for k in WEIGHTS}
    var = {k: args["v_" + k] for k in WEIGHTS}
    two_d = lambda a: a.reshape(a.shape[-2:]) if a.ndim == 3 else a.reshape(1, -1)

    shards = {k: two_d(wts[k]).astype(BF16) for k in BIG}
    shards["taps"] = jnp.pad(two_d(conv_w), ((0, 1), (0, HEAD_DIM - conv_w.shape[-1])))
    plan = _Plan(shards, conv_w.shape[-1])
    sm = {k: two_d(wts[k]) for k in SMALL}

    grad_x, _, small = _local_step(x[0], mem[0], loss_target[0], sm, plan)
    parts = plan.parts

    out = {}
    small_comm = _Comm(replicated=[small[k] for k in SMALL_ORDER])
    parts["w_in"] = [parts["w_in_a"], parts["w_in_b"]]
    for k in BIG:
        res = _adamw("adamw_" + k, two_d(wts[k]), two_d(mom[k]), two_d(var[k]), parts[k],
                     comm=small_comm if k == BIG[0] else _NO_COMM)
        out[k] = [r.reshape(wts[k].shape) for r in res[:4]]
        if k == BIG[0]:
            small_parts = dict(zip(SMALL_ORDER, res[4:]))
    for k in SMALL:
        res = _adamw("adamw_" + k, two_d(wts[k]), two_d(mom[k]), two_d(var[k]), small_parts[k])
        out[k] = [r.reshape(wts[k].shape) for r in res]
    loss = _sum_parts("loss_sum", small_parts["loss"])[0, 0]
    me = 4 * lax.axis_index("x") + 2 * lax.axis_index("y") + lax.axis_index("c")
    n_tap_cols = conv_w.shape[-1]
    tap_parts = lax.dynamic_slice(small_parts["conv_w"], (0, 0, me * n_tap_cols), (N_DEV, CONV_K, n_tap_cols))
    res = _adamw("adamw_conv_w", two_d(conv_w), two_d(m_conv_w), two_d(v_conv_w), tap_parts)
    out["conv_w"] = [r.reshape(conv_w.shape) for r in res]

    return (loss, grad_x[None], *[out[k][0] for k in WEIGHTS], *[out[k][1] for k in WEIGHTS],
            *[out[k][2] for k in WEIGHTS], *[out[k][3] for k in WEIGHTS])
```

```python
import functools

import jax
import jax.numpy as jnp
import numpy as np
from jax import lax
from jax.experimental import pallas as pl
from jax.experimental.pallas import tpu as pltpu

F32 = jnp.float32
BF16 = jnp.bfloat16

N_DEV = 8
D_MODEL = 1024
N_MEM = 256
HEAD_DIM = 128
HEADS_PER_GROUP = 4
GROUP_W = HEADS_PER_GROUP * HEAD_DIM
DILATIONS = (1, 4, 16)
BAND = 128
N_GROUPS = 3
ATTN_W = N_GROUPS * GROUP_W
QKV_W = 3 * ATTN_W
ROT_DIM = HEAD_DIM // 4
ROPE_THETA = 500000.0
CONV_CH = 768
CONV_K = 31
CONV_HALO = 32
SUBLANES = 8
CONV_ROWS = 64
IN_W = 8192
GLU_COL_BLK = QKV_W // (2 * CONV_CH)
GATE_COL_BLK = (QKV_W + 2 * CONV_CH) // (2 * D_MODEL)
CROSS_HEADS = 4
CROSS_HD = D_MODEL // CROSS_HEADS
D_FF = 4096
EPS = 1e-6
NEG = -1e30
QB = 4
ROW_BLK = QB * BAND
SUB_ROWS = 256
ADAM_LR = 0.001
ADAM_B1 = 0.9
ADAM_B2 = 0.999
ADAM_EPS = 1e-08
ADAM_WD = 0.01
ADAM_STEP = 10

VMEM_LIMIT = 56 * 1024 * 1024
MESH = pl.DeviceIdType.MESH


def _params(**kw):
    return pltpu.CompilerParams(vmem_limit_bytes=VMEM_LIMIT, **kw)


def _sigmoid(x):
    return 1.0 / (1.0 + jnp.exp(-x))


def _dot(a, b, kind):
    dims = {"nn": (((1,), (0,)), ((), ())), "nt": (((1,), (1,)), ((), ())), "tn": (((0,), (0,)), ((), ()))}[kind]
    if a.dtype != BF16:
        a = a.astype(BF16)
    if b.dtype != BF16:
        b = b.astype(BF16)
    return lax.dot_general(a, b, dims, preferred_element_type=F32)


def _peers():
    x, y, c = lax.axis_index("x"), lax.axis_index("y"), lax.axis_index("c")
    me = 4 * x + 2 * y + c
    peers = [(x, y, 1 - c), (1 - x, y, c), (x, 1 - y, c), (1 - x, 1 - y, c),
             (1 - x, y, 1 - c), (x, 1 - y, 1 - c), (1 - x, 1 - y, 1 - c)]
    return me, peers


class _Comm:
    def __init__(self, chunked=(), replicated=()):
        self.arrays = list(chunked) + list(replicated)
        self.n_c = len(chunked)
        self.n = len(self.arrays)
        self.out_shape = [jax.ShapeDtypeStruct(a.shape, a.dtype) for a in chunked]
        self.out_shape += [jax.ShapeDtypeStruct((N_DEV,) + a.shape, a.dtype) for a in replicated]
        self.in_specs = [pl.BlockSpec(memory_space=pl.ANY)] * self.n
        self.out_specs = [pl.BlockSpec(memory_space=pl.ANY)] * self.n
        self.scratch = [pltpu.SemaphoreType.DMA((self.n,))] * 5 if self.n else []

    @staticmethod
    def _where():
        x, y, c = lax.axis_index("x"), lax.axis_index("y"), lax.axis_index("c")
        chips = [(1 - x, y), (x, 1 - y), (1 - x, 1 - y)]
        return (x, y, c), 4 * x + 2 * y + c, chips

    def _local(self, ins, outs, sems, a, me):
        src = ins[a].at[me] if a < self.n_c else ins[a]
        return pltpu.make_async_copy(src, outs[a].at[me], sems[2].at[a])

    @staticmethod
    def _remote(src, dst, send, recv, to):
        return pltpu.make_async_remote_copy(src_ref=src, dst_ref=dst, send_sem=send, recv_sem=recv, device_id=to,
                                            device_id_type=MESH)

    def start(self, ins, outs, sems):
        (x, y, c), me, chips = self._where()
        for a in range(self.n):
            self._local(ins, outs, sems, a, me).start()
            if a < self.n_c:
                for (px, py, pc) in _peers()[1]:
                    self._remote(ins[a].at[4 * px + 2 * py + pc], outs[a].at[me], sems[0].at[a], sems[1].at[a], (px, py, pc)).start()
            else:
                self._remote(ins[a], outs[a].at[me], sems[3].at[a], sems[4].at[a], (x, y, 1 - c)).start()
                for (px, py) in chips:
                    self._remote(ins[a], outs[a].at[me], sems[0].at[a], sems[1].at[a], (px, py, c)).start()

    def wait(self, ins, outs, sems):
        (x, y, c), me, chips = self._where()
        sibling = (x, y, 1 - c)

        def drain(a, pair, count):
            blocks = outs[a].at[pl.ds(0, count)]
            cp = self._remote(blocks, blocks, sems[pair].at[a], sems[pair + 1].at[a], sibling)
            cp.wait_send()
            cp.wait_recv()

        for a in range(self.n):
            if a < self.n_c:
                drain(a, 0, N_DEV - 1)
            else:
                drain(a, 0, len(chips))
                for (px, py) in chips:
                    blk = outs[a].at[4 * px + 2 * py + c]
                    self._remote(blk, blk, sems[3].at[a], sems[4].at[a], sibling).start()
        for a in range(self.n):
            if a >= self.n_c:
                drain(a, 3, len(chips) + 1)
            self._local(ins, outs, sems, a, me).wait()


_NO_COMM = _Comm()


def _pcall(body, name, grid, operands, in_specs, out_shape, out_specs, scratch=(), aliases=None, comm=_NO_COMM, **params):
    n_in, n_out, n_scr = len(operands), len(out_shape), len(scratch)
    grid = tuple(grid)

    def carried(*refs):
        ins, c_in = refs[:n_in], refs[n_in:n_in + comm.n]
        o0 = n_in + comm.n
        outs, c_out = refs[o0:o0 + n_out], refs[o0 + n_out:o0 + n_out + comm.n]
        s0 = o0 + n_out + comm.n
        scr, sems = refs[s0:s0 + n_scr], refs[s0 + n_scr:]
        ids = [pl.program_id(ax) for ax in range(len(grid))]

        @pl.when(functools.reduce(jnp.logical_and, [p == 0 for p in ids]))
        def _():
            comm.start(c_in, c_out, sems)

        body(*ins, *outs, *scr)

        @pl.when(functools.reduce(jnp.logical_and, [p == g - 1 for p, g in zip(ids, grid)]))
        def _():
            comm.wait(c_in, c_out, sems)

    return pl.pallas_call(
        carried if comm.n else body, name=name, grid=grid, in_specs=list(in_specs) + comm.in_specs,
        out_shape=list(out_shape) + comm.out_shape, out_specs=list(out_specs) + comm.out_specs,
        scratch_shapes=list(scratch) + comm.scratch, input_output_aliases=aliases or {},
        compiler_params=_params(dimension_semantics=("arbitrary",) * len(grid), **params),
    )(*operands, *comm.arrays)


def _mm(name, a, b, kind, grid, a_blk, b_blk, outs, extras=(), epi=None, acc_outs=(), j_outer=False, comm=_NO_COMM,
        split=None, b_resident=False, out_chunks=0):
    gi, gj, gk = grid
    n_ex = len(extras)
    n_out = len(outs)
    mode, n_chunks = split if split is not None else (None, 1)

    def spec(blk, fn, **kw):
        return pl.BlockSpec(blk, (lambda j, i, k: fn(i, j, k)) if j_outer else fn, **kw)

    def b_chunk(b_ref, c):
        if len(b_ref.shape) == 3:
            return b_ref[c]
        rows, cols = b_ref.shape
        if (kind == "nn") == (mode == "cols"):
            return b_ref[:, c * (cols // n_chunks):(c + 1) * (cols // n_chunks)]
        return b_ref[c * (rows // n_chunks):(c + 1) * (rows // n_chunks), :]

    def col_chunk(ref, c):
        width = ref.shape[-1] // n_chunks
        return slice(c * width, (c + 1) * width)

    def body(*refs):
        a_ref, b_ref = refs[0], refs[1]
        ex = refs[2:2 + n_ex]
        out_refs = refs[2 + n_ex:2 + n_ex + n_out]
        acc_ref = refs[2 + n_ex + n_out] if gk > 1 else None
        i = pl.program_id(1 if j_outer else 0)
        k = pl.program_id(2)
        if mode == "cols":
            a_val = a_ref[...]
            for c in range(n_chunks):
                acc = _dot(a_val, b_chunk(b_ref, c), kind)
                vals = epi(acc, *[e[:, col_chunk(e, c)] for e in ex]) if epi is not None else (acc,)
                for o, v in zip(out_refs, vals):
                    o[:, col_chunk(o, c)] = v.astype(o.dtype)
            return
        if mode == "sum":
            part = _dot(a_ref[:, col_chunk(a_ref, 0)], b_chunk(b_ref, 0), kind)
            for c in range(1, n_chunks):
                part = part + _dot(a_ref[:, col_chunk(a_ref, c)], b_chunk(b_ref, c), kind)
        else:
            part = _dot(a_ref[...], b_ref[...], kind)

        def finish(acc):
            if out_chunks:
                width = acc.shape[-1] // out_chunks
                for c in range(out_chunks):
                    out_refs[0][c] = acc[:, c * width:(c + 1) * width].astype(out_refs[0].dtype)
                return
            vals = epi(acc, *[e[...] for e in ex]) if epi is not None else (acc,)
            for idx, (o, v) in enumerate(zip(out_refs, vals)):
                if idx in acc_outs:
                    @pl.when(i == 0)
                    def _():
                        o[...] = v.astype(o.dtype)

                    @pl.when(i != 0)
                    def _():
                        o[...] += v.astype(o.dtype)
                else:
                    o[...] = v.astype(o.dtype)

        if gk == 1:
            finish(part)
        else:
            @pl.when(k == 0)
            def _():
                acc_ref[...] = part

            @pl.when(k != 0)
            def _():
                acc_ref[...] += part

            @pl.when(k == gk - 1)
            def _():
                finish(acc_ref[...])

    scratch = []
    if gk > 1:
        tm = a_blk[0][-1] if kind == "tn" else a_blk[0][-2]
        tn = b_blk[0][-2] if kind == "nt" else b_blk[0][-1]
        scratch = [pltpu.VMEM((tm, tn), F32)]
    b_kw = dict(pipeline_mode=pl.Buffered(1)) if b_resident else {}
    return _pcall(body, name, (gj, gi, gk) if j_outer else (gi, gj, gk), [a, b] + [e for e, _, _ in extras],
                  [spec(*a_blk), spec(*b_blk, **b_kw)] + [spec(blk, fn) for _, blk, fn in extras],
                  [s for s, _, _ in outs], [spec(blk, fn) for _, blk, fn in outs], scratch, comm=comm)


def _rms_fwd_vals(x, g):
    r = lax.rsqrt(jnp.mean(x * x, axis=-1, keepdims=True) + EPS)
    return x * r * g


def _rms_bwd_vals(x, g, du):
    r = lax.rsqrt(jnp.mean(x * x, axis=-1, keepdims=True) + EPS)
    xh = x * r
    dxh = du * g
    dx = r * (dxh - xh * jnp.mean(dxh * xh, axis=-1, keepdims=True))
    return dx, jnp.sum(du * xh, axis=0, keepdims=True)


def _rms_fwd(name, x, g, rows, comm=_NO_COMM):
    n = x.shape[0]

    def body(x_ref, g_ref, o_ref):
        o_ref[...] = _rms_fwd_vals(x_ref[...], g_ref[...]).astype(BF16)

    return _pcall(body, name, (n // rows,), [x, g],
                  [pl.BlockSpec((rows, D_MODEL), lambda i: (i, 0)), pl.BlockSpec((1, D_MODEL), lambda i: (0, 0))],
                  [jax.ShapeDtypeStruct(x.shape, BF16)], [pl.BlockSpec((rows, D_MODEL), lambda i: (i, 0))], comm=comm)


def _rope_tables(seq):
    half = ROT_DIM // 2
    pos = np.arange(seq, dtype=np.float32)
    inv_freq = np.float32(ROPE_THETA) ** (-np.arange(0, ROT_DIM, 2, dtype=np.float32) / np.float32(ROT_DIM))
    ang = (pos[:, None] * inv_freq[None, :]).astype(np.float32)
    cos, sin = np.cos(ang), np.sin(ang)
    rest = HEAD_DIM - ROT_DIM
    c = np.concatenate([cos, cos, np.ones((seq, rest), np.float32)], axis=1)
    s1 = np.concatenate([np.zeros((seq, half), np.float32), sin, np.zeros((seq, rest), np.float32)], axis=1)
    s2 = np.concatenate([-sin, np.zeros((seq, half + rest), np.float32)], axis=1)
    return jnp.asarray(c), jnp.asarray(s1), jnp.asarray(s2)


def _group_shapes(seq, width, dtype):
    return [jax.ShapeDtypeStruct((d, seq // d, width), dtype) for d in DILATIONS]


def _group_specs(width):
    return [pl.BlockSpec((d, ROW_BLK // d, width), lambda i: (0, i, 0)) for d in DILATIONS]


def _qkv_prep(z, tabs, comm=_NO_COMM):
    seq = z.shape[0]

    def body(z_ref, c_ref, s1_ref, s2_ref, a0, a1, a2, sc):
        outs = (a0, a1, a2)
        c, s1, s2 = c_ref[...], s1_ref[...], s2_ref[...]
        for part in range(3):
            for hh in range(N_GROUPS * HEADS_PER_GROUP):
                g, hl = divmod(hh, HEADS_PER_GROUP)
                col = part * ATTN_W + hh * HEAD_DIM
                ocol = part * GROUP_W + hl * HEAD_DIM
                x = z_ref[:, col:col + HEAD_DIM].astype(F32)
                if part < 2:
                    x = x * c + pltpu.roll(x, ROT_DIM // 2, 1) * s1 + pltpu.roll(x, HEAD_DIM - ROT_DIM // 2, 1) * s2
                d = DILATIONS[g]
                if d == 1:
                    outs[g][0, :, ocol:ocol + HEAD_DIM] = x.astype(BF16)
                else:
                    sc[...] = x
                    for r in range(d):
                        outs[g][r, :, ocol:ocol + HEAD_DIM] = sc[pl.ds(r, ROW_BLK // d, stride=d), :].astype(BF16)

    tab_spec = pl.BlockSpec((ROW_BLK, HEAD_DIM), lambda i: (i, 0))
    return _pcall(body, "qkv_prep", (seq // ROW_BLK,), [z, *tabs],
                  [pl.BlockSpec((ROW_BLK, QKV_W), lambda i: (i, 0)), tab_spec, tab_spec, tab_spec],
                  _group_shapes(seq, ATTN_W, BF16), _group_specs(ATTN_W), [pltpu.VMEM((ROW_BLK, HEAD_DIM), F32)], comm=comm)


def _band_masks_2(t):
    qi = lax.broadcasted_iota(jnp.int32, (BAND, 2 * BAND), 0)
    kj = lax.broadcasted_iota(jnp.int32, (BAND, 2 * BAND), 1)
    band = jnp.logical_and(kj >= qi, kj <= qi + BAND)
    return band, jnp.logical_and(band, jnp.logical_or(kj >= BAND, t > 0))


def _attn_fwd(name, a_g, comm=_NO_COMM):
    dil, m_len, _ = a_g.shape
    qb = min(QB, m_len // BAND)
    rows = qb * BAND
    steps = m_len // rows
    scale = HEAD_DIM ** -0.5

    tiles = [(sb, h) for sb in range(qb) for h in range(HEADS_PER_GROUP)]

    def body(q_ref, kc_ref, vc_ref, kp_ref, vp_ref, o_ref, l_ref, k_all, v_all, s_scr, p_scr, r_scr):
        t = pl.program_id(1)
        k_all[0:BAND, :] = kp_ref[...]
        k_all[BAND:, :] = kc_ref[...]
        v_all[0:BAND, :] = vp_ref[...]
        v_all[BAND:, :] = vc_ref[...]
        band, band_first = _band_masks_2(t)
        for idx, (sb, h) in enumerate(tiles):
            cs = slice(h * HEAD_DIM, (h + 1) * HEAD_DIM)
            s = _dot(q_ref[sb * BAND:(sb + 1) * BAND, cs], k_all[sb * BAND:(sb + 2) * BAND, cs], "nt") * scale
            s_scr[idx] = jnp.where(band_first if sb == 0 else band, s, NEG)
        lane = lax.broadcasted_iota(jnp.int32, (BAND, HEAD_DIM), 1)
        lse_rows = [jnp.zeros((BAND, HEAD_DIM), F32)] * qb
        for idx, (sb, h) in enumerate(tiles):
            s = s_scr[idx]
            mx = jnp.max(s, axis=-1, keepdims=True)
            p = jnp.exp(s - mx)
            den = jnp.sum(p, axis=-1, keepdims=True)
            p_scr[idx] = p.astype(BF16)
            r_scr[idx] = jnp.broadcast_to(1.0 / den, (BAND, HEAD_DIM))
            lse_rows[sb] = jnp.where(lane == h, jnp.broadcast_to(mx + jnp.log(den), (BAND, HEAD_DIM)), lse_rows[sb])
        for sb in range(qb):
            l_ref[sb * BAND:(sb + 1) * BAND, :] = lse_rows[sb]
        for idx, (sb, h) in enumerate(tiles):
            cs = slice(h * HEAD_DIM, (h + 1) * HEAD_DIM)
            o_ref[sb * BAND:(sb + 1) * BAND, cs] = _dot(p_scr[idx], v_all[sb * BAND:(sb + 2) * BAND, cs], "nn") * r_scr[idx]

    def prev(r, t):
        return jnp.maximum(qb * t - 1, 0)

    cur = lambda c: pl.BlockSpec((None, rows, GROUP_W), lambda r, t, c=c: (r, t, c))
    prv = lambda c: pl.BlockSpec((None, BAND, GROUP_W), lambda r, t, c=c: (r, prev(r, t), c))
    out_spec = lambda width: pl.BlockSpec((None, rows, width), lambda r, t: (r, t, 0))
    shp = lambda width: jax.ShapeDtypeStruct((dil, m_len, width), F32)
    n_t = len(tiles)
    return _pcall(body, name, (dil, steps), [a_g] * 5, [cur(0), cur(1), cur(2), prv(1), prv(2)],
                  [shp(GROUP_W), shp(HEAD_DIM)], [out_spec(GROUP_W), out_spec(HEAD_DIM)],
                  [pltpu.VMEM((rows + BAND, GROUP_W), BF16), pltpu.VMEM((rows + BAND, GROUP_W), BF16),
                   pltpu.VMEM((n_t, BAND, 2 * BAND), F32), pltpu.VMEM((n_t, BAND, 2 * BAND), BF16),
                   pltpu.VMEM((n_t, BAND, HEAD_DIM), F32)], comm=comm)


def _attn_merge(os_, ls_, seq):
    def body(o0, l0, o1, l1, o2, l2, at_ref, lt_ref, sc, lsc):
        for gi, l_r in enumerate((l1, l2)):
            d = DILATIONS[gi + 1]
            for r in range(d):
                lsc.at[gi][pl.ds(r, ROW_BLK // d, stride=d), :] = l_r[r]
        lse = (l0.at[0], lsc.at[0], lsc.at[1])
        lane = lax.broadcasted_iota(jnp.int32, (ROW_BLK, HEAD_DIM), 1)
        lt_rows = jnp.zeros((ROW_BLK, HEAD_DIM), F32)
        for h in range(HEADS_PER_GROUP):
            cs = slice(h * HEAD_DIM, (h + 1) * HEAD_DIM)
            for gi, o_r in enumerate((o1, o2)):
                d = DILATIONS[gi + 1]
                for r in range(d):
                    sc.at[gi][pl.ds(r, ROW_BLK // d, stride=d), :] = o_r[r, :, cs]
            l_h = [v[:, h:h + 1] for v in lse]
            mx = jnp.maximum(jnp.maximum(l_h[0], l_h[1]), l_h[2])
            e = [jnp.exp(v - mx) for v in l_h]
            tot = e[0] + e[1] + e[2]
            inv = 1.0 / tot
            at_ref[:, cs] = ((e[0] * inv) * o0[0, :, cs] + (e[1] * inv) * sc[0] + (e[2] * inv) * sc[1]).astype(BF16)
            lt_rows = jnp.where(lane == h, jnp.broadcast_to(mx + jnp.log(tot), (ROW_BLK, HEAD_DIM)), lt_rows)
        lt_ref[...] = lt_rows

    go, gl = _group_specs(GROUP_W), _group_specs(HEAD_DIM)
    return pl.pallas_call(
        body, name="attn_merge",
        out_shape=[jax.ShapeDtypeStruct((seq, GROUP_W), BF16), jax.ShapeDtypeStruct((seq, HEAD_DIM), F32)],
        grid=(seq // ROW_BLK,), in_specs=[go[0], gl[0], go[1], gl[1], go[2], gl[2]],
        out_specs=[pl.BlockSpec((ROW_BLK, GROUP_W), lambda i: (i, 0)), pl.BlockSpec((ROW_BLK, HEAD_DIM), lambda i: (i, 0))],
        scratch_shapes=[pltpu.VMEM((2, ROW_BLK, HEAD_DIM), F32), pltpu.VMEM((2, ROW_BLK, HEAD_DIM), F32)],
        compiler_params=_params(dimension_semantics=("arbitrary",)),
    )(os_[0], ls_[0], os_[1], ls_[1], os_[2], ls_[2])


def _attn_bwd_prep(dattn, attn, lt):
    seq = dattn.shape[0]

    def body(da_ref, at_ref, lt_ref, cl0, d1, cl1, d2, cl2, sc, csc):
        lane = lax.broadcasted_iota(jnp.int32, (ROW_BLK, HEAD_DIM), 1)
        cl = pltpu.roll(lt_ref[...], HEADS_PER_GROUP, 1)
        for h in range(HEADS_PER_GROUP):
            cs = slice(h * HEAD_DIM, (h + 1) * HEAD_DIM)
            da = da_ref[:, cs].astype(F32)
            cc = jnp.sum(da * at_ref[:, cs].astype(F32), axis=-1, keepdims=True)
            cl = jnp.where(lane == h, jnp.broadcast_to(cc, (ROW_BLK, HEAD_DIM)), cl)
            sc[...] = da
            for g, d_ref in ((1, d1), (2, d2)):
                d = DILATIONS[g]
                for r in range(d):
                    d_ref[r, :, cs] = sc[pl.ds(r, ROW_BLK // d, stride=d), :].astype(BF16)
        cl0[0] = cl
        csc[...] = cl
        for g, c_ref in ((1, cl1), (2, cl2)):
            d = DILATIONS[g]
            for r in range(d):
                c_ref[r] = csc[pl.ds(r, ROW_BLK // d, stride=d), :]

    go, gl = _group_specs(GROUP_W), _group_specs(HEAD_DIM)
    row = lambda width: pl.BlockSpec((ROW_BLK, width), lambda i: (i, 0))
    shape = lambda g, width, dt: jax.ShapeDtypeStruct((DILATIONS[g], seq // DILATIONS[g], width), dt)
    cl0, d1, cl1, d2, cl2 = pl.pallas_call(
        body, name="attn_bwd_prep",
        out_shape=[shape(0, HEAD_DIM, F32), shape(1, GROUP_W, BF16), shape(1, HEAD_DIM, F32), shape(2, GROUP_W, BF16),
                   shape(2, HEAD_DIM, F32)],
        grid=(seq // ROW_BLK,), in_specs=[row(GROUP_W), row(GROUP_W), row(HEAD_DIM)],
        out_specs=[gl[0], go[1], gl[1], go[2], gl[2]],
        scratch_shapes=[pltpu.VMEM((ROW_BLK, HEAD_DIM), F32), pltpu.VMEM((ROW_BLK, HEAD_DIM), F32)],
        compiler_params=_params(dimension_semantics=("arbitrary",)),
    )(dattn, attn, lt)
    return [(dattn[None], cl0), (d1, cl1), (d2, cl2)]


def _attn_bwd(name, a_g, da_g, cl_g, comm=_NO_COMM):
    dil, m_len, _ = a_g.shape
    qb = min(QB, m_len // BAND)
    rows = qb * BAND
    steps = m_len // rows
    scale = HEAD_DIM ** -0.5

    tiles = [(sb, h) for sb in range(qb) for h in range(HEADS_PER_GROUP)]

    def body(q_ref, kc_ref, vc_ref, kp_ref, vp_ref, da_ref, cl_ref, d_ref, dk_acc, dv_acc, car_k, car_v,
             k_all, v_all, s_scr, dp_scr, p_scr, ds_scr):
        tg = pl.program_id(1)
        t = steps - 1 - tg

        @pl.when(tg == 0)
        def _():
            car_k[...] = jnp.zeros_like(car_k)
            car_v[...] = jnp.zeros_like(car_v)

        k_all[0:BAND, :] = kp_ref[...]
        k_all[BAND:, :] = kc_ref[...]
        v_all[0:BAND, :] = vp_ref[...]
        v_all[BAND:, :] = vc_ref[...]
        zero = jnp.zeros((rows, GROUP_W), F32)
        dk_acc[0:rows, :] = zero
        dv_acc[0:rows, :] = zero
        dk_acc[rows:rows + BAND, :] = car_k[...]
        dv_acc[rows:rows + BAND, :] = car_v[...]
        band, band_first = _band_masks_2(t)
        for idx, (sb, h) in enumerate(tiles):
            cs = slice(h * HEAD_DIM, (h + 1) * HEAD_DIM)
            rs, ks = slice(sb * BAND, (sb + 1) * BAND), slice(sb * BAND, (sb + 2) * BAND)
            s_scr[idx] = _dot(q_ref[rs, cs], k_all[ks, cs], "nt")
            dp_scr[idx] = _dot(da_ref[rs, cs], v_all[ks, cs], "nt")
        for idx, (sb, h) in enumerate(tiles):
            cs = slice(h * HEAD_DIM, (h + 1) * HEAD_DIM)
            rs = slice(sb * BAND, (sb + 1) * BAND)
            cc = jnp.broadcast_to(cl_ref[rs, h:h + 1], (BAND, 2 * BAND))
            ltv = jnp.broadcast_to(cl_ref[rs, HEADS_PER_GROUP + h:HEADS_PER_GROUP + h + 1], (BAND, 2 * BAND))
            p = jnp.exp(jnp.where(band_first if sb == 0 else band, s_scr[idx] * scale - ltv, NEG))
            p_scr[idx] = p.astype(BF16)
            ds_scr[idx] = (p * (dp_scr[idx] - cc) * scale).astype(BF16)
        for idx, (sb, h) in enumerate(tiles):
            cs = slice(h * HEAD_DIM, (h + 1) * HEAD_DIM)
            rs, ks = slice(sb * BAND, (sb + 1) * BAND), slice(sb * BAND, (sb + 2) * BAND)
            d_ref[rs, cs] = _dot(ds_scr[idx], k_all[ks, cs], "nn").astype(BF16)
            dk_acc[ks, cs] += _dot(ds_scr[idx], q_ref[rs, cs], "tn")
            dv_acc[ks, cs] += _dot(p_scr[idx], da_ref[rs, cs], "tn")
        d_ref[:, GROUP_W:2 * GROUP_W] = dk_acc[BAND:rows + BAND, :].astype(BF16)
        d_ref[:, 2 * GROUP_W:3 * GROUP_W] = dv_acc[BAND:rows + BAND, :].astype(BF16)
        car_k[...] = dk_acc[0:BAND, :]
        car_v[...] = dv_acc[0:BAND, :]

    def rev(tg):
        return steps - 1 - tg

    def prev(tg):
        return jnp.maximum(qb * rev(tg) - 1, 0)

    cur = lambda c: pl.BlockSpec((None, rows, GROUP_W), lambda r, tg, c=c: (r, rev(tg), c))
    prv = lambda c: pl.BlockSpec((None, BAND, GROUP_W), lambda r, tg, c=c: (r, prev(tg), c))
    return _pcall(
        body, name, (dil, steps), [a_g, a_g, a_g, a_g, a_g, da_g, cl_g],
        [cur(0), cur(1), cur(2), prv(1), prv(2), cur(0), pl.BlockSpec((None, rows, HEAD_DIM), lambda r, tg: (r, rev(tg), 0))],
        [jax.ShapeDtypeStruct((dil, m_len, ATTN_W), BF16)], [pl.BlockSpec((None, rows, ATTN_W), lambda r, tg: (r, rev(tg), 0))],
        [pltpu.VMEM((rows + BAND, GROUP_W), F32), pltpu.VMEM((rows + BAND, GROUP_W), F32),
         pltpu.VMEM((BAND, GROUP_W), F32), pltpu.VMEM((BAND, GROUP_W), F32),
         pltpu.VMEM((rows + BAND, GROUP_W), BF16), pltpu.VMEM((rows + BAND, GROUP_W), BF16),
         pltpu.VMEM((len(tiles), BAND, 2 * BAND), F32), pltpu.VMEM((len(tiles), BAND, 2 * BAND), F32),
         pltpu.VMEM((len(tiles), BAND, 2 * BAND), BF16), pltpu.VMEM((len(tiles), BAND, 2 * BAND), BF16)], comm=comm)


def _dqkv_post(d_gs, tabs, dz):
    seq = dz.shape[0]

    def body(g0, g1, g2, c_ref, s1_ref, s2_ref, dz_any, o_ref, sc):
        del dz_any
        ins = (g0, g1, g2)
        c, s1, s2 = c_ref[...], s1_ref[...], s2_ref[...]
        for part in range(3):
            for hh in range(N_GROUPS * HEADS_PER_GROUP):
                g, hl = divmod(hh, HEADS_PER_GROUP)
                icol = part * GROUP_W + hl * HEAD_DIM
                ocol = part * ATTN_W + hh * HEAD_DIM
                d = DILATIONS[g]
                if d == 1:
                    x = ins[g][0, :, icol:icol + HEAD_DIM].astype(F32)
                else:
                    for r in range(d):
                        sc[pl.ds(r, ROW_BLK // d, stride=d), :] = ins[g][r, :, icol:icol + HEAD_DIM].astype(F32)
                    x = sc[...]
                if part < 2:
                    x = x * c + pltpu.roll(x * s1, HEAD_DIM - ROT_DIM // 2, 1) + pltpu.roll(x * s2, ROT_DIM // 2, 1)
                o_ref[:, ocol:ocol + HEAD_DIM] = x.astype(BF16)

    tab_spec = pl.BlockSpec((ROW_BLK, HEAD_DIM), lambda i: (i, 0))
    return pl.pallas_call(
        body, name="dqkv_post", out_shape=jax.ShapeDtypeStruct(dz.shape, BF16), grid=(seq // ROW_BLK,),
        in_specs=_group_specs(ATTN_W) + [tab_spec, tab_spec, tab_spec, pl.BlockSpec(memory_space=pl.ANY)],
        out_specs=pl.BlockSpec((ROW_BLK, QKV_W), lambda i: (i, 0)),
        scratch_shapes=[pltpu.VMEM((ROW_BLK, HEAD_DIM), F32)], input_output_aliases={6: 0},
        compiler_params=_params(dimension_semantics=("arbitrary",)),
    )(*d_gs, *tabs, dz)


def _glu(zg):
    a = zg[:, :CONV_CH].astype(F32)
    s = _sigmoid(zg[:, CONV_CH:].astype(F32))
    return a, s, a * s


def _shifted_copies(xs):
    n = xs.shape[1] - SUBLANES
    for b in range(1, SUBLANES):
        xs[b, 0:n, :] = xs[0, pl.ds(b, n), :]


def _shifted(xs, offset, r0, cs):
    a, b = divmod(offset, SUBLANES)
    return xs[b, pl.ds(SUBLANES * a + r0, CONV_ROWS), cs]


def _conv_fwd(z, cw, cb, lg, lb, comm=_NO_COMM):
    seq = z.shape[0]
    halo_per_blk = ROW_BLK // CONV_HALO

    def body(zg_ref, zh_ref, cw_ref, cb_ref, lg_ref, lb_ref, c2_ref, c4_ref, xs):
        i = pl.program_id(0)
        _, _, c1 = _glu(zg_ref[...])
        _, _, c1h = _glu(zh_ref[...])
        xs[0, 0:CONV_HALO, :] = jnp.where(i > 0, c1h, 0.0)
        xs[0, CONV_HALO:, :] = c1
        _shifted_copies(xs)
        for s in range(CONV_CH // HEAD_DIM):
            cs = slice(s * HEAD_DIM, (s + 1) * HEAD_DIM)
            taps = [cw_ref[j:j + 1, cs] for j in range(CONV_K)]
            bias = cb_ref[:, cs]

            def chunk(rc, carry, cs=cs, taps=taps, bias=bias):
                r0 = pl.multiple_of(rc * CONV_ROWS, CONV_ROWS)
                acc = [jnp.zeros((CONV_ROWS, HEAD_DIM), F32)] * 2
                for j in range(CONV_K):
                    acc[j % 2] = acc[j % 2] + taps[j] * _shifted(xs, CONV_HALO - (CONV_K - 1) + j, r0, cs)
                c2_ref[pl.ds(r0, CONV_ROWS), cs] = acc[0] + acc[1] + bias
                return carry

            lax.fori_loop(0, ROW_BLK // CONV_ROWS, chunk, 0)
        c2 = c2_ref[...]
        mu = jnp.mean(c2, axis=-1, keepdims=True)
        xc = c2 - mu
        rstd = lax.rsqrt(jnp.mean(xc * xc, axis=-1, keepdims=True) + EPS)
        c3 = xc * rstd * lg_ref[...] + lb_ref[...]
        c4_ref[...] = (c3 * _sigmoid(c3)).astype(BF16)

    vec = pl.BlockSpec((1, CONV_CH), lambda i: (0, 0))
    return _pcall(
        body, "conv_fwd", (seq // ROW_BLK,), [z, z, cw, cb, lg, lb],
        [pl.BlockSpec((ROW_BLK, 2 * CONV_CH), lambda i: (i, GLU_COL_BLK)),
         pl.BlockSpec((CONV_HALO, 2 * CONV_CH), lambda i: (jnp.maximum(i * halo_per_blk - 1, 0), GLU_COL_BLK)),
         pl.BlockSpec((CONV_HALO, CONV_CH), lambda i: (0, 0)), vec, vec, vec],
        [jax.ShapeDtypeStruct((seq, CONV_CH), F32), jax.ShapeDtypeStruct((seq, CONV_CH), BF16)],
        [pl.BlockSpec((ROW_BLK, CONV_CH), lambda i: (i, 0)), pl.BlockSpec((ROW_BLK, CONV_CH), lambda i: (i, 0))],
        [pltpu.VMEM((SUBLANES, ROW_BLK + CONV_HALO, CONV_CH), F32)], comm=comm)


def _conv_bwd(dc2, z, cw, dz, comm=_NO_COMM):
    seq = z.shape[0]
    halo_per_blk = ROW_BLK // CONV_HALO
    n_blk = seq // ROW_BLK
    last_halo = seq // CONV_HALO - 1

    def body(dc_ref, dn_ref, zg_ref, zh_ref, cw_ref, dz_any, o_ref, dcw_ref, xs, ys, dc1_ref, dcw_acc):
        del dz_any
        i = pl.program_id(0)
        a, s, c1 = _glu(zg_ref[...])
        _, _, c1h = _glu(zh_ref[...])
        xs[0, 0:CONV_HALO, :] = jnp.where(i > 0, c1h, 0.0)
        xs[0, CONV_HALO:, :] = c1
        ys[0, 0:ROW_BLK, :] = dc_ref[...]
        ys[0, ROW_BLK:, :] = jnp.where(i < n_blk - 1, dn_ref[...], 0.0)
        _shifted_copies(xs)
        _shifted_copies(ys)

        @pl.when(i == 0)
        def _():
            dcw_acc[...] = jnp.zeros_like(dcw_acc)

        for sl in range(CONV_CH // HEAD_DIM):
            cs = slice(sl * HEAD_DIM, (sl + 1) * HEAD_DIM)
            taps = [cw_ref[j:j + 1, cs] for j in range(CONV_K)]

            def chunk(rc, carry, cs=cs, taps=taps):
                r0 = pl.multiple_of(rc * CONV_ROWS, CONV_ROWS)
                dc = ys[0, pl.ds(r0, CONV_ROWS), cs]
                acc = [jnp.zeros((CONV_ROWS, HEAD_DIM), F32)] * 2
                for j in range(CONV_K):
                    prod = dc * _shifted(xs, CONV_HALO - (CONV_K - 1) + j, r0, cs)
                    dcw_acc[j, :, cs] += jnp.sum(prod.reshape(CONV_ROWS // SUBLANES, SUBLANES, HEAD_DIM), axis=0)
                    acc[j % 2] = acc[j % 2] + taps[j] * _shifted(ys, CONV_K - 1 - j, r0, cs)
                dc1_ref[pl.ds(r0, CONV_ROWS), cs] = acc[0] + acc[1]
                return carry

            lax.fori_loop(0, ROW_BLK // CONV_ROWS, chunk, 0)
        dc1 = dc1_ref[...]
        o_ref[:, :CONV_CH] = (dc1 * s).astype(BF16)
        o_ref[:, CONV_CH:] = (dc1 * a * s * (1.0 - s)).astype(BF16)

        @pl.when(i == n_blk - 1)
        def _():
            dcw_ref[...] = jnp.sum(dcw_acc[...], axis=1)

    return _pcall(
        body, "conv_bwd", (n_blk,), [dc2, dc2, z, z, cw, dz],
        [pl.BlockSpec((ROW_BLK, CONV_CH), lambda i: (i, 0)),
         pl.BlockSpec((CONV_HALO, CONV_CH), lambda i: (jnp.minimum((i + 1) * halo_per_blk, last_halo), 0)),
         pl.BlockSpec((ROW_BLK, 2 * CONV_CH), lambda i: (i, GLU_COL_BLK)),
         pl.BlockSpec((CONV_HALO, 2 * CONV_CH), lambda i: (jnp.maximum(i * halo_per_blk - 1, 0), GLU_COL_BLK)),
         pl.BlockSpec((CONV_HALO, CONV_CH), lambda i: (0, 0)),
         pl.BlockSpec(memory_space=pl.ANY)],
        [jax.ShapeDtypeStruct(dz.shape, BF16), jax.ShapeDtypeStruct((CONV_HALO, CONV_CH), F32)],
        [pl.BlockSpec((ROW_BLK, 2 * CONV_CH), lambda i: (i, GLU_COL_BLK)), pl.BlockSpec((CONV_HALO, CONV_CH), lambda i: (0, 0))],
        [pltpu.VMEM((SUBLANES, ROW_BLK + CONV_HALO, CONV_CH), F32), pltpu.VMEM((SUBLANES, ROW_BLK + CONV_HALO, CONV_CH), F32),
         pltpu.VMEM((ROW_BLK, CONV_CH), F32), pltpu.VMEM((CONV_HALO, SUBLANES, CONV_CH), F32)],
        aliases={5: 0}, comm=comm)


def _epi_mix(ya, c4, wcp, gates, bg):
    yc = _dot(c4, wcp, "nn")
    gv = _sigmoid(gates.astype(F32) + bg)
    merged = gv[:, :D_MODEL] * ya + gv[:, D_MODEL:] * yc
    return merged, ya, yc


def _epi_residual_rms(acc, xres, g):
    x = xres + acc
    return x, _rms_fwd_vals(x, g)


def _cross_scores(cq, ck):
    out = []
    for h in range(CROSS_HEADS):
        cs = slice(h * CROSS_HD, (h + 1) * CROSS_HD)
        s = _dot(cq[:, cs], ck[:, cs], "nt") * (CROSS_HD ** -0.5)
        e = jnp.exp(s - jnp.max(s, axis=-1, keepdims=True))
        out.append((cs, e, jnp.sum(e, axis=-1, keepdims=True)))
    return out


def _epi_cross_fwd(acc, ck, cv):
    cq = acc.astype(BF16)
    co = [_dot(e, cv[:, cs], "nn") / den for cs, e, den in _cross_scores(cq, ck)]
    return cq, jnp.concatenate(co, axis=1)


def _epi_cross_bwd(dco, cq, ck, cv):
    dco = dco.astype(BF16)
    dcq, dck, dcv = [], [], []
    for cs, e, den in _cross_scores(cq, ck):
        p = e / den
        dp = _dot(dco[:, cs], cv[:, cs], "nt")
        ds = (p * (dp - jnp.sum(dp * p, axis=-1, keepdims=True)) * (CROSS_HD ** -0.5)).astype(BF16)
        dcq.append(_dot(ds, ck[:, cs], "nn"))
        dck.append(_dot(ds, cq[:, cs], "tn"))
        dcv.append(_dot(p, dco[:, cs], "tn"))
    return jnp.concatenate(dcq, axis=1), jnp.concatenate(dck, axis=1), jnp.concatenate(dcv, axis=1)


def _epi_mlp_up(acc):
    return acc, jnp.square(jnp.maximum(acc, 0.0))


def _epi_final(acc, x2, tgt, g):
    x3 = x2 + acc
    err = _rms_fwd_vals(x3, g) - tgt
    loss = (0.5 / D_MODEL) * jnp.sum(err * err)
    dx3, dg = _rms_bwd_vals(x3, g, err * (1.0 / D_MODEL))
    return dx3, jnp.full((1, HEAD_DIM), loss, F32), dg


def _epi_mlp_down_bwd(dh, hpre):
    return (dh * 2.0 * jnp.maximum(hpre.astype(F32), 0.0),)


def _epi_rms_bwd(du, x, g, dres):
    dx, dg = _rms_bwd_vals(x, g, du)
    return dres.astype(F32) + dx, dg


def _epi_rms_bwd_g(du, x, g):
    return (_rms_bwd_vals(x, g, du)[1],)


def _epi_mix_bwd(dm, ya, yc, gates, bg):
    gv = _sigmoid(gates.astype(F32) + bg)
    ga, gb = gv[:, :D_MODEL], gv[:, D_MODEL:]
    ya, yc = ya.astype(F32), yc.astype(F32)
    dgate = jnp.concatenate([dm * ya * ga * (1.0 - ga), dm * yc * gb * (1.0 - gb)], axis=1)
    return dm * ga, dm * gb, dgate, jnp.sum(dgate, axis=0, keepdims=True)


def _epi_ln_bwd(dc4, c2, lg, lb):
    mu = jnp.mean(c2, axis=-1, keepdims=True)
    xc = c2 - mu
    rstd = lax.rsqrt(jnp.mean(xc * xc, axis=-1, keepdims=True) + EPS)
    xh = xc * rstd
    c3 = xh * lg + lb
    sg = _sigmoid(c3)
    dc3 = dc4 * sg * (1.0 + c3 * (1.0 - sg))
    dxh = dc3 * lg
    dc2 = rstd * (dxh - jnp.mean(dxh, axis=-1, keepdims=True) - xh * jnp.mean(dxh * xh, axis=-1, keepdims=True))
    return (dc2, jnp.sum(dc3 * xh, axis=0, keepdims=True), jnp.sum(dc3, axis=0, keepdims=True),
            jnp.sum(dc2, axis=0, keepdims=True))


def _sds(shape, dtype):
    return jax.ShapeDtypeStruct(shape, dtype)


class _Lazy:
    def __init__(self, fn):
        self.fn = fn

    def __getitem__(self, key):
        return self.fn(key)


def _local_step(x, mem, tgt, sm, plan):
    w = _Lazy(plan.w)
    dw = {}

    def carry(name, n_own, fn, *args, **kw):
        c = plan.comm(name, dw)
        res = fn(*args, comm=c, **kw)
        plan.done(name, res[n_own:])
        return res[:n_own]

    def mm(name, *args, **kw):
        return carry(name, len(args[6]), _mm, name, *args, **kw)

    seq = x.shape[0]
    nr = seq // ROW_BLK
    big = min(1024, seq)
    nb = seq // big
    row = lambda n: ((ROW_BLK, n), lambda i, j, k: (i, 0))
    vec = lambda n: ((1, n), lambda i, j, k: (0, 0))
    full = lambda r, c: ((r, c), lambda i, j, k: (0, 0))
    gates_blk = ((ROW_BLK, 2 * D_MODEL), lambda i, j, k: (i, GATE_COL_BLK))
    tabs = _rope_tables(seq)

    u = carry("rms_mix", 1, _rms_fwd, "rms_mix", x, sm["g_mix"], ROW_BLK)[0]
    whole3 = lambda a: (a.shape, lambda i, j, k: (0, 0, 0))
    z = plan.project_in(u)
    a_gs = carry("qkv_prep", 3, _qkv_prep, z, tabs)
    os_, ls_ = [], []
    for g in range(N_GROUPS):
        name = "attn_fwd_%d" % g
        o_g, l_g = carry(name, 2, _attn_fwd, name, a_gs[g])
        os_.append(o_g)
        ls_.append(l_g)
    attn, lt = _attn_merge(os_, ls_, seq)
    c2, c4 = carry("conv_fwd", 2, _conv_fwd, z, w["taps"], sm["conv_b"], sm["conv_ln_g"], sm["conv_ln_b"])
    merged, ya, yc = mm(
        "mix", attn, w["w_attn_proj"], "nn", (nr, 1, 1), row(GROUP_W), full(GROUP_W, D_MODEL),
        [(_sds((seq, D_MODEL), BF16), *row(D_MODEL))] * 3,
        extras=[(c4, *row(CONV_CH)), (w["w_conv_proj"], *full(CONV_CH, D_MODEL)), (z, *gates_blk), (sm["b_gate"], *vec(2 * D_MODEL))],
        epi=_epi_mix)
    x1, uq = mm("out_proj", merged, w["w_out"], "nn", (nr, 1, 1), row(D_MODEL), full(D_MODEL, D_MODEL),
                 [(_sds((seq, D_MODEL), F32), *row(D_MODEL)), (_sds((seq, D_MODEL), BF16), *row(D_MODEL))],
                 extras=[(x, *row(D_MODEL)), (sm["g_cross"], *vec(D_MODEL))], epi=_epi_residual_rms)

    mn = _rms_fwd("rms_mem", mem, sm["g_mem"], N_MEM)[0]
    ckv = mm("ckv_proj", mn, w["w_ckv"], "nn", (1, N_DEV, 1), full(N_MEM, D_MODEL),
              ((None, D_MODEL, 2 * D_MODEL // N_DEV), lambda i, j, k: (j, 0, 0)),
              [(_sds((N_MEM, 2 * D_MODEL), BF16), (N_MEM, 2 * D_MODEL // N_DEV), lambda i, j, k: (0, j))])[0]
    ck, cv = ckv[:, :D_MODEL], ckv[:, D_MODEL:]
    kv_blk = full(N_MEM, D_MODEL)
    cq, co = mm("cq_proj_cross", uq, w["w_cq"], "nn", (nr, 1, 1), row(D_MODEL), full(D_MODEL, D_MODEL),
                 [(_sds((seq, D_MODEL), BF16), *row(D_MODEL))] * 2,
                 extras=[(ck, *kv_blk), (cv, *kv_blk)], epi=_epi_cross_fwd)
    x2, um = mm("co_proj", co, w["w_co"], "nn", (nr, 1, 1), row(D_MODEL), full(D_MODEL, D_MODEL),
                 [(_sds((seq, D_MODEL), F32), *row(D_MODEL)), (_sds((seq, D_MODEL), BF16), *row(D_MODEL))],
                 extras=[(x1, *row(D_MODEL)), (sm["g_mlp"], *vec(D_MODEL))], epi=_epi_residual_rms)

    ff_blk = D_FF // N_DEV
    row_f32 = (_sds((seq, D_MODEL), F32), *row(D_MODEL))
    row_bf16 = (_sds((seq, D_MODEL), BF16), *row(D_MODEL))
    col_sum = (_sds((1, D_MODEL), F32), *vec(D_MODEL))
    hpre, h = mm("mlp_up", um, w["w_up"], "nn", (nr, 1, 1), row(D_MODEL), whole3(w["w_up"]),
                 [(_sds((seq, D_FF), BF16), *row(D_FF))] * 2, epi=_epi_mlp_up, split=("cols", N_DEV), b_resident=True)
    kt = D_FF // D_MODEL
    dx3, loss, dg_final = mm(
        "mlp_down_loss", h, w["w_down"], "nn", (nr, 1, 1), row(D_FF), full(D_FF, D_MODEL),
        [row_bf16, (_sds((1, HEAD_DIM), F32), *vec(HEAD_DIM)), col_sum],
        extras=[(x2, *row(D_MODEL)), (tgt, *row(D_MODEL)), (sm["g_final"], *vec(D_MODEL))], epi=_epi_final, acc_outs=(1, 2),
        b_resident=True)

    dhpre = mm("mlp_down_bwd", dx3, w["w_down"], "nt", (nr, 1, 1), row(D_MODEL), full(D_FF, D_MODEL),
               [(_sds((seq, D_FF), BF16), *row(D_FF))], extras=[(hpre, *row(D_FF))], epi=_epi_mlp_down_bwd,
               split=("cols", kt), b_resident=True)[0]
    big2 = min(2 * big, seq)
    nb2 = seq // big2
    dw["w_down"] = mm("dw_down", h, dx3, "tn", (kt, 1, nb2), ((big2, D_MODEL), lambda i, j, k: (k, i)),
                      ((big2, D_MODEL), lambda i, j, k: (k, 0)),
                      [(_sds((D_FF, D_MODEL), BF16), (D_MODEL, D_MODEL), lambda i, j, k: (i, 0))])[0]
    dx2, dg_mlp = mm("mlp_up_bwd", dhpre, w["w_up"], "nt", (nr, 1, 1), row(D_FF), whole3(w["w_up"]),
                     [row_bf16, col_sum],
                     extras=[(x2, *row(D_MODEL)), (sm["g_mlp"], *vec(D_MODEL)), (dx3, *row(D_MODEL))],
                     epi=_epi_rms_bwd, acc_outs=(1,), split=("sum", N_DEV), b_resident=True)
    dw["w_up"] = mm("dw_up", um, dhpre, "tn", (1, N_DEV, nb2), ((big2, D_MODEL), lambda i, j, k: (k, 0)),
                    ((big2, ff_blk), lambda i, j, k: (k, j)),
                    [(_sds((N_DEV, D_MODEL, ff_blk), BF16), (None, D_MODEL, ff_blk), lambda i, j, k: (j, 0, 0))])[0]

    acc_kv = (_sds((N_MEM, D_MODEL), F32), *kv_blk)
    dcq, dck, dcv = mm("co_proj_bwd_cross", dx2, w["w_co"], "nt", (nr, 1, 1), row(D_MODEL), full(D_MODEL, D_MODEL),
                       [row_bf16, acc_kv, acc_kv],
                       extras=[(cq, *row(D_MODEL)), (ck, *kv_blk), (cv, *kv_blk)], epi=_epi_cross_bwd, acc_outs=(1, 2))

    def dw_square(name, act, grad):
        return mm(name, act, grad, "tn", (1, 1, nb2), ((big2, D_MODEL), lambda i, j, k: (k, 0)),
                  ((big2, D_MODEL), lambda i, j, k: (k, 0)), [(_sds((D_MODEL, D_MODEL), BF16), *full(D_MODEL, D_MODEL))])[0]

    dw["w_co"] = dw_square("dw_co", co, dx2)
    dx1, dg_cross = mm("cq_proj_bwd", dcq, w["w_cq"], "nt", (nr, 1, 1), row(D_MODEL), full(D_MODEL, D_MODEL),
                       [row_bf16, col_sum],
                       extras=[(x1, *row(D_MODEL)), (sm["g_cross"], *vec(D_MODEL)), (dx2, *row(D_MODEL))],
                       epi=_epi_rms_bwd, acc_outs=(1,))
    dw["w_cq"] = dw_square("dw_cq", uq, dcq)
    dckv = jnp.concatenate([dck, dcv], axis=1)
    kv_chunk = 2 * D_MODEL // N_DEV
    dw["w_ckv"] = mm("dw_ckv", mn, dckv, "tn", (1, N_DEV, 1), full(N_MEM, D_MODEL), ((N_MEM, kv_chunk), lambda i, j, k: (0, j)),
                      [(_sds((N_DEV, D_MODEL, kv_chunk), BF16), (None, D_MODEL, kv_chunk), lambda i, j, k: (j, 0, 0))])[0]
    dg_mem = mm("ckv_proj_bwd", dckv, w["w_ckv"], "nt", (1, 1, N_DEV), ((N_MEM, kv_chunk), lambda i, j, k: (0, k)),
                 ((None, D_MODEL, kv_chunk), lambda i, j, k: (k, 0, 0)), [(_sds((1, D_MODEL), F32), *vec(D_MODEL))],
                 extras=[(mem, *full(N_MEM, D_MODEL)), (sm["g_mem"], *vec(D_MODEL))], epi=_epi_rms_bwd_g, acc_outs=(0,))[0]

    dya, dyc, dz, db_gate = mm(
        "out_proj_bwd_mix", dx1, w["w_out"], "nt", (nr, 1, 1), row(D_MODEL), full(D_MODEL, D_MODEL),
        [(_sds((seq, D_MODEL), BF16), *row(D_MODEL)), (_sds((seq, D_MODEL), BF16), *row(D_MODEL)),
         (_sds((seq, IN_W), BF16), *gates_blk), (_sds((1, 2 * D_MODEL), F32), *vec(2 * D_MODEL))],
        extras=[(ya, *row(D_MODEL)), (yc, *row(D_MODEL)), (z, *gates_blk), (sm["b_gate"], *vec(2 * D_MODEL))],
        epi=_epi_mix_bwd, acc_outs=(3,))
    dw["w_out"] = dw_square("dw_out", merged, dx1)
    dattn = mm("attn_proj_bwd", dya, w["w_attn_proj"], "nt", (nr, 1, 1), row(D_MODEL), full(GROUP_W, D_MODEL),
                [(_sds((seq, GROUP_W), BF16), *row(GROUP_W))])[0]
    pc = D_MODEL // N_DEV
    dw["w_attn_proj"] = mm("dw_attn_proj", attn, dya, "tn", (1, 1, nb2), ((big2, GROUP_W), lambda i, j, k: (k, 0)),
                           ((big2, D_MODEL), lambda i, j, k: (k, 0)),
                           [(_sds((N_DEV, GROUP_W, pc), BF16), (N_DEV, GROUP_W, pc), lambda i, j, k: (0, 0, 0))],
                           out_chunks=N_DEV)[0]
    cvec = (_sds((1, CONV_CH), F32), *vec(CONV_CH))
    dc2, dg_ln_g, dg_ln_b, dg_conv_b = mm(
        "conv_proj_bwd_ln", dyc, w["w_conv_proj"], "nt", (nr, 1, 1), row(D_MODEL), full(CONV_CH, D_MODEL),
        [(_sds((seq, CONV_CH), F32), *row(CONV_CH)), cvec, cvec, cvec],
        extras=[(c2, *row(CONV_CH)), (sm["conv_ln_g"], *vec(CONV_CH)), (sm["conv_ln_b"], *vec(CONV_CH))],
        epi=_epi_ln_bwd, acc_outs=(1, 2, 3))
    dw["w_conv_proj"] = mm("dw_conv_proj", c4, dyc, "tn", (1, 1, nb2), ((big2, CONV_CH), lambda i, j, k: (k, 0)),
                           ((big2, D_MODEL), lambda i, j, k: (k, 0)),
                           [(_sds((N_DEV, CONV_CH, pc), BF16), (N_DEV, CONV_CH, pc), lambda i, j, k: (0, 0, 0))],
                           out_chunks=N_DEV)[0]
    dz, dg_conv_w = carry("conv_bwd", 2, _conv_bwd, dc2, z, w["taps"], dz)
    preps = _attn_bwd_prep(dattn, attn, lt)
    d_gs = []
    for g in range(N_GROUPS):
        name = "attn_bwd_%d" % g
        d_gs.append(carry(name, 1, _attn_bwd, name, a_gs[g], *preps[g])[0])
    dz = _dqkv_post(d_gs, tabs, dz)
    dw["w_in"] = mm("dw_in", u, dz, "tn", (1, N_DEV, nb2), ((big2, D_MODEL), lambda i, j, k: (k, 0)),
                    ((big2, D_MODEL), lambda i, j, k: (k, j)),
                    [(_sds((N_DEV, D_MODEL, D_MODEL), BF16), (None, D_MODEL, D_MODEL), lambda i, j, k: (j, 0, 0))])[0]
    grad_x, dg_mix = mm("in_proj_bwd", dz, w["w_in"], "nt", (nr, 1, 1), row(IN_W), whole3(w["w_in"]), [row_f32, col_sum],
                        extras=[(x, *row(D_MODEL)), (sm["g_mix"], *vec(D_MODEL)), (dx1, *row(D_MODEL))],
                        epi=_epi_rms_bwd, acc_outs=(1,), split=("sum", N_DEV), b_resident=True)
    small = dict(g_mix=dg_mix, b_gate=db_gate, conv_b=dg_conv_b, conv_ln_g=dg_ln_g, conv_ln_b=dg_ln_b, g_cross=dg_cross,
                 g_mem=dg_mem, g_mlp=dg_mlp, g_final=dg_final, loss=loss, conv_w=dg_conv_w)
    return grad_x, dw, small


SHARD_SHAPE = dict(w_in=(1024, 1024), w_attn_proj=(512, 128), w_conv_proj=(768, 128), w_out=(128, 1024),
                   w_cq=(128, 1024), w_ckv=(1024, 256), w_co=(128, 1024), w_up=(1024, 512), w_down=(512, 1024))
FWD_CARRY = {"in_proj": ("taps",), "qkv_prep": ("w_attn_proj", "w_conv_proj", "w_out", "w_cq", "w_ckv", "w_co"),
             "conv_fwd": ("w_up", "w_down")}
BWD_CARRY = {"dw_up": ("w_down",), "out_proj_bwd_mix": ("w_co", "w_cq"), "conv_bwd": ("w_up", "w_ckv"),
             "attn_bwd_0": ("w_out",), "attn_bwd_1": ("w_attn_proj", "w_conv_proj"), "in_proj_bwd": ("w_in",)}


def _cols_to_2d(a):
    return a.transpose(1, 0, 2).reshape(a.shape[1], -1)


def _in_proj_gather(u, w_shard, comm):
    seq = u.shape[0]
    tm = min(1024, seq)
    x, y, c = lax.axis_index("x"), lax.axis_index("y"), lax.axis_index("c")
    ident = lambda px, py, pc: 4 * px + 2 * py + pc
    far = [(1 - x, y), (x, 1 - y), (1 - x, 1 - y)]
    order = jnp.stack([ident(x, y, c), ident(x, y, 1 - c), ident(*far[0], c), ident(*far[1], c), ident(*far[0], 1 - c),
                       ident(*far[1], 1 - c), ident(*far[2], c), ident(*far[2], 1 - c)]).astype(jnp.int32)
    forward_at = {2: 0, 3: 1, 6: 2}
    n_far = len(far)

    def body(order_ref, u_ref, wsh_ref, *rest):
        c_in, z_ref, wg_ref = rest[:comm.n], rest[comm.n], rest[comm.n + 1]
        c_out = rest[comm.n + 2:2 * comm.n + 2]
        wbuf, load_sem, local_sem, recv_sems, ici_send, d2d_send = rest[2 * comm.n + 2:2 * comm.n + 8]
        sems = rest[2 * comm.n + 8:]
        jj, i = pl.program_id(0), pl.program_id(1)
        (kx, ky, kc), me, chips = _Comm._where()
        sibling = (kx, ky, 1 - kc)
        n = order_ref[jj]

        def push(src, blk, send, to):
            return pltpu.make_async_remote_copy(src_ref=src, dst_ref=wg_ref.at[blk], send_sem=send,
                                                recv_sem=recv_sems.at[blk], device_id=to, device_id_type=MESH)

        def load(src):
            cp = pltpu.make_async_copy(src, wbuf, load_sem)
            cp.start()
            cp.wait()

        @pl.when(jnp.logical_and(jj == 0, i == 0))
        def _():
            push(wsh_ref, me, d2d_send, sibling).start()
            for (px, py) in chips[:2]:
                push(wsh_ref, me, ici_send, (px, py, kc)).start()
            pltpu.make_async_copy(wsh_ref, wg_ref.at[me], local_sem).start()
            load(wsh_ref)

        @pl.when(jnp.logical_and(jj > 0, i == 0))
        def _():
            push(wg_ref.at[n], n, d2d_send, sibling).wait_recv()
            for step, k in forward_at.items():
                @pl.when(jj == step)
                def _(k=k):
                    blk = 4 * chips[k][0] + 2 * chips[k][1] + kc
                    push(wg_ref.at[blk], blk, d2d_send, sibling).start()

            @pl.when(jj == 2)
            def _():
                push(wsh_ref, me, ici_send, (*chips[2], kc)).start()

            if comm.n:
                @pl.when(jj == 3)
                def _():
                    comm.start(c_in, c_out, sems)

            load(wg_ref.at[n])

        for r in range(0, tm, SUB_ROWS):
            z_ref[r:r + SUB_ROWS, :] = _dot(u_ref[r:r + SUB_ROWS, :], wbuf[...], "nn").astype(BF16)

        @pl.when(jnp.logical_and(jj == N_DEV - 1, i == pl.num_programs(1) - 1))
        def _():
            def drain_sends(send, count):
                blocks = wg_ref.at[pl.ds(0, count)]
                pltpu.make_async_remote_copy(src_ref=blocks, dst_ref=blocks, send_sem=send, recv_sem=recv_sems.at[0],
                                             device_id=sibling, device_id_type=MESH).wait_send()

            drain_sends(ici_send, n_far)
            drain_sends(d2d_send, n_far + 1)
            pltpu.make_async_copy(wsh_ref, wg_ref.at[me], local_sem).wait()
            if comm.n:
                comm.wait(c_in, c_out, sems)

    any_spec = pl.BlockSpec(memory_space=pl.ANY)
    grid_spec = pltpu.PrefetchScalarGridSpec(
        num_scalar_prefetch=1, grid=(N_DEV, seq // tm),
        in_specs=[pl.BlockSpec((tm, D_MODEL), lambda jj, i, order_ref: (i, 0)), any_spec] + comm.in_specs,
        out_specs=[pl.BlockSpec((tm, D_MODEL), lambda jj, i, order_ref: (i, order_ref[jj])), any_spec] + comm.out_specs,
        scratch_shapes=[pltpu.VMEM((D_MODEL, D_MODEL), BF16), pltpu.SemaphoreType.DMA, pltpu.SemaphoreType.DMA,
                        pltpu.SemaphoreType.DMA((N_DEV,)), pltpu.SemaphoreType.DMA, pltpu.SemaphoreType.DMA] + comm.scratch)
    return pl.pallas_call(
        body, name="in_proj_gather", grid_spec=grid_spec,
        out_shape=[jax.ShapeDtypeStruct((seq, IN_W), BF16), jax.ShapeDtypeStruct((N_DEV, D_MODEL, D_MODEL), BF16)] + comm.out_shape,
        compiler_params=_params(dimension_semantics=("arbitrary", "arbitrary")),
    )(order, u, w_shard, *comm.arrays)


class _Plan:
    def __init__(self, shards, n_tap_cols):
        self.shards = shards
        self.gathered = {}
        self.parts = {}
        self.n_tap_cols = n_tap_cols

    def project_in(self, u):
        comm = self.comm("in_proj", None)
        res = _in_proj_gather(u, self.shards["w_in"], comm)
        self.gathered["w_in"] = res[1]
        self.done("in_proj", res[2:])
        return res[0]

    def comm(self, name, dw):
        if name in FWD_CARRY:
            return _Comm(replicated=[self.shards[k] for k in FWD_CARRY[name]])
        if name in BWD_CARRY:
            return _Comm(chunked=[dw[k].reshape((N_DEV,) + SHARD_SHAPE[k]) for k in BWD_CARRY[name]])
        return _NO_COMM

    def done(self, name, got):
        if name in FWD_CARRY:
            self.gathered.update(zip(FWD_CARRY[name], got))
        elif name in BWD_CARRY:
            self.parts.update(zip(BWD_CARRY[name], got))

    def w(self, key):
        g = self.gathered[key]
        if key in ("w_in", "w_up", "w_ckv"):
            return g
        if key in ("w_attn_proj", "w_conv_proj"):
            return _cols_to_2d(g)
        if key == "taps":
            return jnp.pad(_cols_to_2d(g[:, :CONV_K, :self.n_tap_cols]), ((0, 1), (0, 0)))
        return g.reshape(-1, g.shape[-1])


def _adamw(name, w, m, v, parts, comm=_NO_COMM):
    rows, cols = w.shape
    n_parts = parts.shape[0]
    rb = rows if rows <= 256 or rows % 256 else 256

    def body(w_ref, m_ref, v_ref, p_ref, g_ref, d_ref, nm_ref, nv_ref):
        g = p_ref[0].astype(F32)
        for q in range(1, n_parts):
            g = g + p_ref[q].astype(F32)
        wv = w_ref[...]
        nm = ADAM_B1 * m_ref[...] + (1.0 - ADAM_B1) * g
        nv = ADAM_B2 * v_ref[...] + (1.0 - ADAM_B2) * jnp.square(g)
        m_hat = nm / (1.0 - ADAM_B1 ** ADAM_STEP)
        v_hat = nv / (1.0 - ADAM_B2 ** ADAM_STEP)
        g_ref[...] = g
        d_ref[...] = -ADAM_LR * (m_hat / (jnp.sqrt(v_hat) + ADAM_EPS) + ADAM_WD * wv)
        nm_ref[...] = nm
        nv_ref[...] = nv

    blk = pl.BlockSpec((rb, cols), lambda i: (i, 0))
    return _pcall(body, name, (rows // rb,), [w, m, v, parts],
                  [blk, blk, blk, pl.BlockSpec((n_parts, rb, cols), lambda i: (0, i, 0))],
                  [jax.ShapeDtypeStruct((rows, cols), F32)] * 4, [blk] * 4, comm=comm)


def _sum_parts(name, parts):
    def body(p_ref, o_ref):
        acc = p_ref[0]
        for q in range(1, parts.shape[0]):
            acc = acc + p_ref[q]
        o_ref[...] = acc

    return _pcall(body, name, (1,), [parts], [pl.BlockSpec(parts.shape, lambda i: (0, 0, 0))],
                  [jax.ShapeDtypeStruct(parts.shape[1:], F32)], [pl.BlockSpec(parts.shape[1:], lambda i: (0, 0))])[0]


BIG = ("w_in", "w_attn_proj", "w_conv_proj", "w_out", "w_cq", "w_ckv", "w_co", "w_up", "w_down")
SMALL = ("g_mix", "b_gate", "conv_b", "conv_ln_g", "conv_ln_b", "g_cross", "g_mem", "g_mlp", "g_final")
SMALL_ORDER = SMALL + ("loss", "conv_w")
WEIGHTS = ("g_mix", "w_in", "b_gate", "conv_w", "conv_b", "conv_ln_g", "conv_ln_b", "w_attn_proj", "w_conv_proj", "w_out",
           "g_cross", "g_mem", "w_cq", "w_ckv", "w_co", "g_mlp", "w_up", "w_down", "g_final")


def kernel(x, mem, g_mix, w_in, b_gate, conv_w, conv_b, conv_ln_g, conv_ln_b, w_attn_proj, w_conv_proj, w_out, g_cross, g_mem, w_cq, w_ckv, w_co, g_mlp, w_up, w_down, g_final, loss_target, m_g_mix, m_w_in, m_b_gate, m_conv_w, m_conv_b, m_conv_ln_g, m_conv_ln_b, m_w_attn_proj, m_w_conv_proj, m_w_out, m_g_cross, m_g_mem, m_w_cq, m_w_ckv, m_w_co, m_g_mlp, m_w_up, m_w_down, m_g_final, v_g_mix, v_w_in, v_b_gate, v_conv_w, v_conv_b, v_conv_ln_g, v_conv_ln_b, v_w_attn_proj, v_w_conv_proj, v_w_out, v_g_cross, v_g_mem, v_w_cq, v_w_ckv, v_w_co, v_g_mlp, v_w_up, v_w_down, v_g_final):
    args = dict(locals())
    wts = {k: args[k] for k in WEIGHTS}
    mom = {k: args["m_" + k] for k in WEIGHTS}
    var = {k: args["v_" + k] for k in WEIGHTS}
    two_d = lambda a: a.reshape(a.shape[-2:]) if a.ndim == 3 else a.reshape(1, -1)

    shards = {k: two_d(wts[k]).astype(BF16) for k in BIG}
    shards["taps"] = jnp.pad(two_d(conv_w), ((0, 1), (0, HEAD_DIM - conv_w.shape[-1])))
    plan = _Plan(shards, conv_w.shape[-1])
    sm = {k: two_d(wts[k]) for k in SMALL}

    grad_x, _, small = _local_step(x[0], mem[0], loss_target[0], sm, plan)
    parts = plan.parts

    out = {}
    small_comm = _Comm(replicated=[small[k] for k in SMALL_ORDER])
    for k in BIG:
        res = _adamw("adamw_" + k, two_d(wts[k]), two_d(mom[k]), two_d(var[k]), parts[k],
                     comm=small_comm if k == BIG[0] else _NO_COMM)
        out[k] = [r.reshape(wts[k].shape) for r in res[:4]]
        if k == BIG[0]:
            small_parts = dict(zip(SMALL_ORDER, res[4:]))
    for k in SMALL:
        res = _adamw("adamw_" + k, two_d(wts[k]), two_d(mom[k]), two_d(var[k]), small_parts[k])
        out[k] = [r.reshape(wts[k].shape) for r in res]
    loss = _sum_parts("loss_sum", small_parts["loss"])[0, 0]
    me = 4 * lax.axis_index("x") + 2 * lax.axis_index("y") + lax.axis_index("c")
    n_tap_cols = conv_w.shape[-1]
    tap_parts = lax.dynamic_slice(small_parts["conv_w"], (0, 0, me * n_tap_cols), (N_DEV, CONV_K, n_tap_cols))
    res = _adamw("adamw_conv_w", two_d(conv_w), two_d(m_conv_w), two_d(v_conv_w), tap_parts)
    out["conv_w"] = [r.reshape(conv_w.shape) for r in res]

    return (loss, grad_x[None], *[out[k][0] for k in WEIGHTS], *[out[k][1] for k in WEIGHTS],
            *[out[k][2] for k in WEIGHTS], *[out[k][3] for k in WEIGHTS])
```

```python
import functools

import jax
import jax.numpy as jnp
import numpy as np
from jax import lax
from jax.experimental import pallas as pl
from jax.experimental.pallas import tpu as pltpu

F32 = jnp.float32
BF16 = jnp.bfloat16

N_DEV = 8
D_MODEL = 1024
N_MEM = 256
HEAD_DIM = 128
HEADS_PER_GROUP = 4
GROUP_W = HEADS_PER_GROUP * HEAD_DIM
DILATIONS = (1, 4, 16)
BAND = 128
N_GROUPS = 3
ATTN_W = N_GROUPS * GROUP_W
QKV_W = 3 * ATTN_W
ROT_DIM = HEAD_DIM // 4
ROPE_THETA = 500000.0
CONV_CH = 768
CONV_K = 31
CONV_HALO = 32
SUBLANES = 8
CONV_ROWS = 64
IN_W = 8192
GLU_COL_BLK = QKV_W // (2 * CONV_CH)
GATE_COL_BLK = (QKV_W + 2 * CONV_CH) // (2 * D_MODEL)
CROSS_HEADS = 4
CROSS_HD = D_MODEL // CROSS_HEADS
D_FF = 4096
EPS = 1e-6
NEG = -1e30
QB = 4
ROW_BLK = QB * BAND
SUB_ROWS = 256
ADAM_LR = 0.001
ADAM_B1 = 0.9
ADAM_B2 = 0.999
ADAM_EPS = 1e-08
ADAM_WD = 0.01
ADAM_STEP = 10

VMEM_LIMIT = 56 * 1024 * 1024
MESH = pl.DeviceIdType.MESH


def _params(**kw):
    return pltpu.CompilerParams(vmem_limit_bytes=VMEM_LIMIT, **kw)


def _sigmoid(x):
    return 1.0 / (1.0 + jnp.exp(-x))


def _dot(a, b, kind):
    dims = {"nn": (((1,), (0,)), ((), ())), "nt": (((1,), (1,)), ((), ())), "tn": (((0,), (0,)), ((), ()))}[kind]
    if a.dtype != BF16:
        a = a.astype(BF16)
    if b.dtype != BF16:
        b = b.astype(BF16)
    return lax.dot_general(a, b, dims, preferred_element_type=F32)


def _peers():
    x, y, c = lax.axis_index("x"), lax.axis_index("y"), lax.axis_index("c")
    me = 4 * x + 2 * y + c
    peers = [(x, y, 1 - c), (1 - x, y, c), (x, 1 - y, c), (1 - x, 1 - y, c),
             (1 - x, y, 1 - c), (x, 1 - y, 1 - c), (1 - x, 1 - y, 1 - c)]
    return me, peers


class _Comm:
    def __init__(self, chunked=(), replicated=()):
        self.arrays = list(chunked) + list(replicated)
        self.n_c = len(chunked)
        self.n = len(self.arrays)
        self.out_shape = [jax.ShapeDtypeStruct(a.shape, a.dtype) for a in chunked]
        self.out_shape += [jax.ShapeDtypeStruct((N_DEV,) + a.shape, a.dtype) for a in replicated]
        self.in_specs = [pl.BlockSpec(memory_space=pl.ANY)] * self.n
        self.out_specs = [pl.BlockSpec(memory_space=pl.ANY)] * self.n
        self.scratch = [pltpu.SemaphoreType.DMA((self.n,))] * 5 if self.n else []

    @staticmethod
    def _where():
        x, y, c = lax.axis_index("x"), lax.axis_index("y"), lax.axis_index("c")
        chips = [(1 - x, y), (x, 1 - y), (1 - x, 1 - y)]
        return (x, y, c), 4 * x + 2 * y + c, chips

    def _local(self, ins, outs, sems, a, me):
        src = ins[a].at[me] if a < self.n_c else ins[a]
        return pltpu.make_async_copy(src, outs[a].at[me], sems[2].at[a])

    @staticmethod
    def _remote(src, dst, send, recv, to):
        return pltpu.make_async_remote_copy(src_ref=src, dst_ref=dst, send_sem=send, recv_sem=recv, device_id=to,
                                            device_id_type=MESH)

    def start(self, ins, outs, sems):
        (x, y, c), me, chips = self._where()
        for a in range(self.n):
            self._local(ins, outs, sems, a, me).start()
            if a < self.n_c:
                for (px, py, pc) in _peers()[1]:
                    self._remote(ins[a].at[4 * px + 2 * py + pc], outs[a].at[me], sems[0].at[a], sems[1].at[a], (px, py, pc)).start()
            else:
                self._remote(ins[a], outs[a].at[me], sems[3].at[a], sems[4].at[a], (x, y, 1 - c)).start()
                for (px, py) in chips:
                    self._remote(ins[a], outs[a].at[me], sems[0].at[a], sems[1].at[a], (px, py, c)).start()

    def wait(self, ins, outs, sems):
        (x, y, c), me, chips = self._where()
        sibling = (x, y, 1 - c)

        def drain(a, pair, count):
            blocks = outs[a].at[pl.ds(0, count)]
            cp = self._remote(blocks, blocks, sems[pair].at[a], sems[pair + 1].at[a], sibling)
            cp.wait_send()
            cp.wait_recv()

        for a in range(self.n):
            if a < self.n_c:
                drain(a, 0, N_DEV - 1)
            else:
                drain(a, 0, len(chips))
                for (px, py) in chips:
                    blk = outs[a].at[4 * px + 2 * py + c]
                    self._remote(blk, blk, sems[3].at[a], sems[4].at[a], sibling).start()
        for a in range(self.n):
            if a >= self.n_c:
                drain(a, 3, len(chips) + 1)
            self._local(ins, outs, sems, a, me).wait()


_NO_COMM = _Comm()


def _pcall(body, name, grid, operands, in_specs, out_shape, out_specs, scratch=(), aliases=None, comm=_NO_COMM, **params):
    n_in, n_out, n_scr = len(operands), len(out_shape), len(scratch)
    grid = tuple(grid)

    def carried(*refs):
        ins, c_in = refs[:n_in], refs[n_in:n_in + comm.n]
        o0 = n_in + comm.n
        outs, c_out = refs[o0:o0 + n_out], refs[o0 + n_out:o0 + n_out + comm.n]
        s0 = o0 + n_out + comm.n
        scr, sems = refs[s0:s0 + n_scr], refs[s0 + n_scr:]
        ids = [pl.program_id(ax) for ax in range(len(grid))]

        @pl.when(functools.reduce(jnp.logical_and, [p == 0 for p in ids]))
        def _():
            comm.start(c_in, c_out, sems)

        body(*ins, *outs, *scr)

        @pl.when(functools.reduce(jnp.logical_and, [p == g - 1 for p, g in zip(ids, grid)]))
        def _():
            comm.wait(c_in, c_out, sems)

    return pl.pallas_call(
        carried if comm.n else body, name=name, grid=grid, in_specs=list(in_specs) + comm.in_specs,
        out_shape=list(out_shape) + comm.out_shape, out_specs=list(out_specs) + comm.out_specs,
        scratch_shapes=list(scratch) + comm.scratch, input_output_aliases=aliases or {},
        compiler_params=_params(dimension_semantics=("arbitrary",) * len(grid), **params),
    )(*operands, *comm.arrays)


def _mm(name, a, b, kind, grid, a_blk, b_blk, outs, extras=(), epi=None, acc_outs=(), j_outer=False, comm=_NO_COMM,
        split=None, b_resident=False, out_chunks=0, into=None):
    gi, gj, gk = grid
    n_ex = len(extras)
    n_out = len(outs)
    n_into = 0 if into is None else 1
    mode, n_chunks = split if split is not None else (None, 1)

    def spec(blk, fn, **kw):
        return pl.BlockSpec(blk, (lambda j, i, k: fn(i, j, k)) if j_outer else fn, **kw)

    def b_chunk(b_ref, c):
        if len(b_ref.shape) == 3:
            return b_ref[c]
        rows, cols = b_ref.shape
        if (kind == "nn") == (mode == "cols"):
            return b_ref[:, c * (cols // n_chunks):(c + 1) * (cols // n_chunks)]
        return b_ref[c * (rows // n_chunks):(c + 1) * (rows // n_chunks), :]

    def col_chunk(ref, c):
        width = ref.shape[-1] // n_chunks
        return slice(c * width, (c + 1) * width)

    def body(*refs):
        a_ref, b_ref = refs[0], refs[1]
        ex = refs[2:2 + n_ex]
        out_refs = refs[2 + n_ex + n_into:2 + n_ex + n_into + n_out]
        acc_ref = refs[2 + n_ex + n_into + n_out] if gk > 1 else None
        i = pl.program_id(1 if j_outer else 0)
        k = pl.program_id(2)
        if mode == "cols":
            a_val = a_ref[...]
            for c in range(n_chunks):
                acc = _dot(a_val, b_chunk(b_ref, c), kind)
                vals = epi(acc, *[e[:, col_chunk(e, c)] for e in ex]) if epi is not None else (acc,)
                for o, v in zip(out_refs, vals):
                    o[:, col_chunk(o, c)] = v.astype(o.dtype)
            return
        if mode == "sum":
            part = _dot(a_ref[:, col_chunk(a_ref, 0)], b_chunk(b_ref, 0), kind)
            for c in range(1, n_chunks):
                part = part + _dot(a_ref[:, col_chunk(a_ref, c)], b_chunk(b_ref, c), kind)
        else:
            part = _dot(a_ref[...], b_ref[...], kind)

        def finish(acc):
            if out_chunks:
                width = acc.shape[-1] // out_chunks
                for c in range(out_chunks):
                    out_refs[0][c] = acc[:, c * width:(c + 1) * width].astype(out_refs[0].dtype)
                return
            vals = epi(acc, *[e[...] for e in ex]) if epi is not None else (acc,)
            for idx, (o, v) in enumerate(zip(out_refs, vals)):
                if idx in acc_outs:
                    @pl.when(i == 0)
                    def _():
                        o[...] = v.astype(o.dtype)

                    @pl.when(i != 0)
                    def _():
                        o[...] += v.astype(o.dtype)
                else:
                    o[...] = v.astype(o.dtype)

        if gk == 1:
            finish(part)
        else:
            @pl.when(k == 0)
            def _():
                acc_ref[...] = part

            @pl.when(k != 0)
            def _():
                acc_ref[...] += part

            @pl.when(k == gk - 1)
            def _():
                finish(acc_ref[...])

    scratch = []
    if gk > 1:
        tm = a_blk[0][-1] if kind == "tn" else a_blk[0][-2]
        tn = b_blk[0][-2] if kind == "nt" else b_blk[0][-1]
        scratch = [pltpu.VMEM((tm, tn), F32)]
    b_kw = dict(pipeline_mode=pl.Buffered(1)) if b_resident else {}
    filled = [] if into is None else [into]
    return _pcall(body, name, (gj, gi, gk) if j_outer else (gi, gj, gk), [a, b] + [e for e, _, _ in extras] + filled,
                  [spec(*a_blk), spec(*b_blk, **b_kw)] + [spec(blk, fn) for _, blk, fn in extras]
                  + [pl.BlockSpec(memory_space=pl.ANY)] * n_into,
                  [s for s, _, _ in outs], [spec(blk, fn) for _, blk, fn in outs], scratch,
                  aliases={2 + n_ex: 0} if n_into else None, comm=comm)


def _rms_fwd_vals(x, g):
    r = lax.rsqrt(jnp.mean(x * x, axis=-1, keepdims=True) + EPS)
    return x * r * g


def _rms_bwd_vals(x, g, du):
    r = lax.rsqrt(jnp.mean(x * x, axis=-1, keepdims=True) + EPS)
    xh = x * r
    dxh = du * g
    dx = r * (dxh - xh * jnp.mean(dxh * xh, axis=-1, keepdims=True))
    return dx, jnp.sum(du * xh, axis=0, keepdims=True)


def _rms_fwd(name, x, g, rows, comm=_NO_COMM):
    n = x.shape[0]

    def body(x_ref, g_ref, o_ref):
        o_ref[...] = _rms_fwd_vals(x_ref[...], g_ref[...]).astype(BF16)

    return _pcall(body, name, (n // rows,), [x, g],
                  [pl.BlockSpec((rows, D_MODEL), lambda i: (i, 0)), pl.BlockSpec((1, D_MODEL), lambda i: (0, 0))],
                  [jax.ShapeDtypeStruct(x.shape, BF16)], [pl.BlockSpec((rows, D_MODEL), lambda i: (i, 0))], comm=comm)


def _rope_tables(seq):
    half = ROT_DIM // 2
    pos = np.arange(seq, dtype=np.float32)
    inv_freq = np.float32(ROPE_THETA) ** (-np.arange(0, ROT_DIM, 2, dtype=np.float32) / np.float32(ROT_DIM))
    ang = (pos[:, None] * inv_freq[None, :]).astype(np.float32)
    cos, sin = np.cos(ang), np.sin(ang)
    rest = HEAD_DIM - ROT_DIM
    c = np.concatenate([cos, cos, np.ones((seq, rest), np.float32)], axis=1)
    s1 = np.concatenate([np.zeros((seq, half), np.float32), sin, np.zeros((seq, rest), np.float32)], axis=1)
    s2 = np.concatenate([-sin, np.zeros((seq, half + rest), np.float32)], axis=1)
    return jnp.asarray(c), jnp.asarray(s1), jnp.asarray(s2)


def _group_shapes(seq, width, dtype):
    return [jax.ShapeDtypeStruct((d, seq // d, width), dtype) for d in DILATIONS]


def _group_specs(width):
    return [pl.BlockSpec((d, ROW_BLK // d, width), lambda i: (0, i, 0)) for d in DILATIONS]


def _qkv_prep(z, tabs, comm=_NO_COMM):
    seq = z.shape[0]

    def body(z_ref, c_ref, s1_ref, s2_ref, a0, a1, a2, sc):
        outs = (a0, a1, a2)
        c, s1, s2 = c_ref[...], s1_ref[...], s2_ref[...]
        for part in range(3):
            for hh in range(N_GROUPS * HEADS_PER_GROUP):
                g, hl = divmod(hh, HEADS_PER_GROUP)
                col = part * ATTN_W + hh * HEAD_DIM
                ocol = part * GROUP_W + hl * HEAD_DIM
                x = z_ref[:, col:col + HEAD_DIM].astype(F32)
                if part < 2:
                    x = x * c + pltpu.roll(x, ROT_DIM // 2, 1) * s1 + pltpu.roll(x, HEAD_DIM - ROT_DIM // 2, 1) * s2
                d = DILATIONS[g]
                if d == 1:
                    outs[g][0, :, ocol:ocol + HEAD_DIM] = x.astype(BF16)
                else:
                    sc[...] = x
                    for r in range(d):
                        outs[g][r, :, ocol:ocol + HEAD_DIM] = sc[pl.ds(r, ROW_BLK // d, stride=d), :].astype(BF16)

    tab_spec = pl.BlockSpec((ROW_BLK, HEAD_DIM), lambda i: (i, 0))
    return _pcall(body, "qkv_prep", (seq // ROW_BLK,), [z, *tabs],
                  [pl.BlockSpec((ROW_BLK, QKV_W), lambda i: (i, 0)), tab_spec, tab_spec, tab_spec],
                  _group_shapes(seq, ATTN_W, BF16), _group_specs(ATTN_W), [pltpu.VMEM((ROW_BLK, HEAD_DIM), F32)], comm=comm)


def _band_masks_2(t):
    qi = lax.broadcasted_iota(jnp.int32, (BAND, 2 * BAND), 0)
    kj = lax.broadcasted_iota(jnp.int32, (BAND, 2 * BAND), 1)
    band = jnp.logical_and(kj >= qi, kj <= qi + BAND)
    return band, jnp.logical_and(band, jnp.logical_or(kj >= BAND, t > 0))


def _attn_fwd(name, a_g, comm=_NO_COMM):
    dil, m_len, _ = a_g.shape
    qb = min(QB, m_len // BAND)
    rows = qb * BAND
    steps = m_len // rows
    scale = HEAD_DIM ** -0.5

    tiles = [(sb, h) for sb in range(qb) for h in range(HEADS_PER_GROUP)]

    def body(q_ref, kc_ref, vc_ref, kp_ref, vp_ref, o_ref, l_ref, k_all, v_all, s_scr, p_scr, r_scr):
        t = pl.program_id(1)
        k_all[0:BAND, :] = kp_ref[...]
        k_all[BAND:, :] = kc_ref[...]
        v_all[0:BAND, :] = vp_ref[...]
        v_all[BAND:, :] = vc_ref[...]
        band, band_first = _band_masks_2(t)
        for idx, (sb, h) in enumerate(tiles):
            cs = slice(h * HEAD_DIM, (h + 1) * HEAD_DIM)
            s = _dot(q_ref[sb * BAND:(sb + 1) * BAND, cs], k_all[sb * BAND:(sb + 2) * BAND, cs], "nt") * scale
            s_scr[idx] = jnp.where(band_first if sb == 0 else band, s, NEG)
        lane = lax.broadcasted_iota(jnp.int32, (BAND, HEAD_DIM), 1)
        lse_rows = [jnp.zeros((BAND, HEAD_DIM), F32)] * qb
        for idx, (sb, h) in enumerate(tiles):
            s = s_scr[idx]
            mx = jnp.max(s, axis=-1, keepdims=True)
            p = jnp.exp(s - mx)
            den = jnp.sum(p, axis=-1, keepdims=True)
            p_scr[idx] = p.astype(BF16)
            r_scr[idx] = jnp.broadcast_to(1.0 / den, (BAND, HEAD_DIM))
            lse_rows[sb] = jnp.where(lane == h, jnp.broadcast_to(mx + jnp.log(den), (BAND, HEAD_DIM)), lse_rows[sb])
        for sb in range(qb):
            l_ref[sb * BAND:(sb + 1) * BAND, :] = lse_rows[sb]
        for idx, (sb, h) in enumerate(tiles):
            cs = slice(h * HEAD_DIM, (h + 1) * HEAD_DIM)
            o_ref[sb * BAND:(sb + 1) * BAND, cs] = _dot(p_scr[idx], v_all[sb * BAND:(sb + 2) * BAND, cs], "nn") * r_scr[idx]

    def prev(r, t):
        return jnp.maximum(qb * t - 1, 0)

    cur = lambda c: pl.BlockSpec((None, rows, GROUP_W), lambda r, t, c=c: (r, t, c))
    prv = lambda c: pl.BlockSpec((None, BAND, GROUP_W), lambda r, t, c=c: (r, prev(r, t), c))
    out_spec = lambda width: pl.BlockSpec((None, rows, width), lambda r, t: (r, t, 0))
    shp = lambda width: jax.ShapeDtypeStruct((dil, m_len, width), F32)
    n_t = len(tiles)
    return _pcall(body, name, (dil, steps), [a_g] * 5, [cur(0), cur(1), cur(2), prv(1), prv(2)],
                  [shp(GROUP_W), shp(HEAD_DIM)], [out_spec(GROUP_W), out_spec(HEAD_DIM)],
                  [pltpu.VMEM((rows + BAND, GROUP_W), BF16), pltpu.VMEM((rows + BAND, GROUP_W), BF16),
                   pltpu.VMEM((n_t, BAND, 2 * BAND), F32), pltpu.VMEM((n_t, BAND, 2 * BAND), BF16),
                   pltpu.VMEM((n_t, BAND, HEAD_DIM), F32)], comm=comm)


def _attn_merge(os_, ls_, seq):
    def body(o0, l0, o1, l1, o2, l2, at_ref, lt_ref, sc, lsc):
        for gi, l_r in enumerate((l1, l2)):
            d = DILATIONS[gi + 1]
            for r in range(d):
                lsc.at[gi][pl.ds(r, ROW_BLK // d, stride=d), :] = l_r[r]
        lse = (l0.at[0], lsc.at[0], lsc.at[1])
        lane = lax.broadcasted_iota(jnp.int32, (ROW_BLK, HEAD_DIM), 1)
        lt_rows = jnp.zeros((ROW_BLK, HEAD_DIM), F32)
        for h in range(HEADS_PER_GROUP):
            cs = slice(h * HEAD_DIM, (h + 1) * HEAD_DIM)
            for gi, o_r in enumerate((o1, o2)):
                d = DILATIONS[gi + 1]
                for r in range(d):
                    sc.at[gi][pl.ds(r, ROW_BLK // d, stride=d), :] = o_r[r, :, cs]
            l_h = [v[:, h:h + 1] for v in lse]
            mx = jnp.maximum(jnp.maximum(l_h[0], l_h[1]), l_h[2])
            e = [jnp.exp(v - mx) for v in l_h]
            tot = e[0] + e[1] + e[2]
            inv = 1.0 / tot
            at_ref[:, cs] = ((e[0] * inv) * o0[0, :, cs] + (e[1] * inv) * sc[0] + (e[2] * inv) * sc[1]).astype(BF16)
            lt_rows = jnp.where(lane == h, jnp.broadcast_to(mx + jnp.log(tot), (ROW_BLK, HEAD_DIM)), lt_rows)
        lt_ref[...] = lt_rows

    go, gl = _group_specs(GROUP_W), _group_specs(HEAD_DIM)
    return pl.pallas_call(
        body, name="attn_merge",
        out_shape=[jax.ShapeDtypeStruct((seq, GROUP_W), BF16), jax.ShapeDtypeStruct((seq, HEAD_DIM), F32)],
        grid=(seq // ROW_BLK,), in_specs=[go[0], gl[0], go[1], gl[1], go[2], gl[2]],
        out_specs=[pl.BlockSpec((ROW_BLK, GROUP_W), lambda i: (i, 0)), pl.BlockSpec((ROW_BLK, HEAD_DIM), lambda i: (i, 0))],
        scratch_shapes=[pltpu.VMEM((2, ROW_BLK, HEAD_DIM), F32), pltpu.VMEM((2, ROW_BLK, HEAD_DIM), F32)],
        compiler_params=_params(dimension_semantics=("arbitrary",)),
    )(os_[0], ls_[0], os_[1], ls_[1], os_[2], ls_[2])


def _attn_bwd_prep(dattn, attn, lt):
    seq = dattn.shape[0]

    def body(da_ref, at_ref, lt_ref, cl0, d1, cl1, d2, cl2, sc, csc):
        lane = lax.broadcasted_iota(jnp.int32, (ROW_BLK, HEAD_DIM), 1)
        cl = pltpu.roll(lt_ref[...], HEADS_PER_GROUP, 1)
        for h in range(HEADS_PER_GROUP):
            cs = slice(h * HEAD_DIM, (h + 1) * HEAD_DIM)
            da = da_ref[:, cs].astype(F32)
            cc = jnp.sum(da * at_ref[:, cs].astype(F32), axis=-1, keepdims=True)
            cl = jnp.where(lane == h, jnp.broadcast_to(cc, (ROW_BLK, HEAD_DIM)), cl)
            sc[...] = da
            for g, d_ref in ((1, d1), (2, d2)):
                d = DILATIONS[g]
                for r in range(d):
                    d_ref[r, :, cs] = sc[pl.ds(r, ROW_BLK // d, stride=d), :].astype(BF16)
        cl0[0] = cl
        csc[...] = cl
        for g, c_ref in ((1, cl1), (2, cl2)):
            d = DILATIONS[g]
            for r in range(d):
                c_ref[r] = csc[pl.ds(r, ROW_BLK // d, stride=d), :]

    go, gl = _group_specs(GROUP_W), _group_specs(HEAD_DIM)
    row = lambda width: pl.BlockSpec((ROW_BLK, width), lambda i: (i, 0))
    shape = lambda g, width, dt: jax.ShapeDtypeStruct((DILATIONS[g], seq // DILATIONS[g], width), dt)
    cl0, d1, cl1, d2, cl2 = pl.pallas_call(
        body, name="attn_bwd_prep",
        out_shape=[shape(0, HEAD_DIM, F32), shape(1, GROUP_W, BF16), shape(1, HEAD_DIM, F32), shape(2, GROUP_W, BF16),
                   shape(2, HEAD_DIM, F32)],
        grid=(seq // ROW_BLK,), in_specs=[row(GROUP_W), row(GROUP_W), row(HEAD_DIM)],
        out_specs=[gl[0], go[1], gl[1], go[2], gl[2]],
        scratch_shapes=[pltpu.VMEM((ROW_BLK, HEAD_DIM), F32), pltpu.VMEM((ROW_BLK, HEAD_DIM), F32)],
        compiler_params=_params(dimension_semantics=("arbitrary",)),
    )(dattn, attn, lt)
    return [(dattn[None], cl0), (d1, cl1), (d2, cl2)]


def _attn_bwd(name, a_g, da_g, cl_g, comm=_NO_COMM):
    dil, m_len, _ = a_g.shape
    qb = min(QB, m_len // BAND)
    rows = qb * BAND
    steps = m_len // rows
    scale = HEAD_DIM ** -0.5

    tiles = [(sb, h) for sb in range(qb) for h in range(HEADS_PER_GROUP)]

    def body(q_ref, kc_ref, vc_ref, kp_ref, vp_ref, da_ref, cl_ref, d_ref, dk_acc, dv_acc, car_k, car_v,
             k_all, v_all, s_scr, dp_scr, p_scr, ds_scr):
        tg = pl.program_id(1)
        t = steps - 1 - tg

        @pl.when(tg == 0)
        def _():
            car_k[...] = jnp.zeros_like(car_k)
            car_v[...] = jnp.zeros_like(car_v)

        k_all[0:BAND, :] = kp_ref[...]
        k_all[BAND:, :] = kc_ref[...]
        v_all[0:BAND, :] = vp_ref[...]
        v_all[BAND:, :] = vc_ref[...]
        zero = jnp.zeros((rows, GROUP_W), F32)
        dk_acc[0:rows, :] = zero
        dv_acc[0:rows, :] = zero
        dk_acc[rows:rows + BAND, :] = car_k[...]
        dv_acc[rows:rows + BAND, :] = car_v[...]
        band, band_first = _band_masks_2(t)
        for idx, (sb, h) in enumerate(tiles):
            cs = slice(h * HEAD_DIM, (h + 1) * HEAD_DIM)
            rs, ks = slice(sb * BAND, (sb + 1) * BAND), slice(sb * BAND, (sb + 2) * BAND)
            s_scr[idx] = _dot(q_ref[rs, cs], k_all[ks, cs], "nt")
            dp_scr[idx] = _dot(da_ref[rs, cs], v_all[ks, cs], "nt")
        for idx, (sb, h) in enumerate(tiles):
            cs = slice(h * HEAD_DIM, (h + 1) * HEAD_DIM)
            rs = slice(sb * BAND, (sb + 1) * BAND)
            cc = jnp.broadcast_to(cl_ref[rs, h:h + 1], (BAND, 2 * BAND))
            ltv = jnp.broadcast_to(cl_ref[rs, HEADS_PER_GROUP + h:HEADS_PER_GROUP + h + 1], (BAND, 2 * BAND))
            p = jnp.exp(jnp.where(band_first if sb == 0 else band, s_scr[idx] * scale - ltv, NEG))
            p_scr[idx] = p.astype(BF16)
            ds_scr[idx] = (p * (dp_scr[idx] - cc) * scale).astype(BF16)
        for idx, (sb, h) in enumerate(tiles):
            cs = slice(h * HEAD_DIM, (h + 1) * HEAD_DIM)
            rs, ks = slice(sb * BAND, (sb + 1) * BAND), slice(sb * BAND, (sb + 2) * BAND)
            d_ref[rs, cs] = _dot(ds_scr[idx], k_all[ks, cs], "nn").astype(BF16)
            dk_acc[ks, cs] += _dot(ds_scr[idx], q_ref[rs, cs], "tn")
            dv_acc[ks, cs] += _dot(p_scr[idx], da_ref[rs, cs], "tn")
        d_ref[:, GROUP_W:2 * GROUP_W] = dk_acc[BAND:rows + BAND, :].astype(BF16)
        d_ref[:, 2 * GROUP_W:3 * GROUP_W] = dv_acc[BAND:rows + BAND, :].astype(BF16)
        car_k[...] = dk_acc[0:BAND, :]
        car_v[...] = dv_acc[0:BAND, :]

    def rev(tg):
        return steps - 1 - tg

    def prev(tg):
        return jnp.maximum(qb * rev(tg) - 1, 0)

    cur = lambda c: pl.BlockSpec((None, rows, GROUP_W), lambda r, tg, c=c: (r, rev(tg), c))
    prv = lambda c: pl.BlockSpec((None, BAND, GROUP_W), lambda r, tg, c=c: (r, prev(tg), c))
    return _pcall(
        body, name, (dil, steps), [a_g, a_g, a_g, a_g, a_g, da_g, cl_g],
        [cur(0), cur(1), cur(2), prv(1), prv(2), cur(0), pl.BlockSpec((None, rows, HEAD_DIM), lambda r, tg: (r, rev(tg), 0))],
        [jax.ShapeDtypeStruct((dil, m_len, ATTN_W), BF16)], [pl.BlockSpec((None, rows, ATTN_W), lambda r, tg: (r, rev(tg), 0))],
        [pltpu.VMEM((rows + BAND, GROUP_W), F32), pltpu.VMEM((rows + BAND, GROUP_W), F32),
         pltpu.VMEM((BAND, GROUP_W), F32), pltpu.VMEM((BAND, GROUP_W), F32),
         pltpu.VMEM((rows + BAND, GROUP_W), BF16), pltpu.VMEM((rows + BAND, GROUP_W), BF16),
         pltpu.VMEM((len(tiles), BAND, 2 * BAND), F32), pltpu.VMEM((len(tiles), BAND, 2 * BAND), F32),
         pltpu.VMEM((len(tiles), BAND, 2 * BAND), BF16), pltpu.VMEM((len(tiles), BAND, 2 * BAND), BF16)], comm=comm)


def _dqkv_post(d_gs, tabs, dz):
    seq = dz.shape[0]

    def body(g0, g1, g2, c_ref, s1_ref, s2_ref, dz_any, o_ref, sc):
        del dz_any
        ins = (g0, g1, g2)
        c, s1, s2 = c_ref[...], s1_ref[...], s2_ref[...]
        for part in range(3):
            for hh in range(N_GROUPS * HEADS_PER_GROUP):
                g, hl = divmod(hh, HEADS_PER_GROUP)
                icol = part * GROUP_W + hl * HEAD_DIM
                ocol = part * ATTN_W + hh * HEAD_DIM
                d = DILATIONS[g]
                if d == 1:
                    x = ins[g][0, :, icol:icol + HEAD_DIM].astype(F32)
                else:
                    for r in range(d):
                        sc[pl.ds(r, ROW_BLK // d, stride=d), :] = ins[g][r, :, icol:icol + HEAD_DIM].astype(F32)
                    x = sc[...]
                if part < 2:
                    x = x * c + pltpu.roll(x * s1, HEAD_DIM - ROT_DIM // 2, 1) + pltpu.roll(x * s2, ROT_DIM // 2, 1)
                o_ref[:, ocol:ocol + HEAD_DIM] = x.astype(BF16)

    tab_spec = pl.BlockSpec((ROW_BLK, HEAD_DIM), lambda i: (i, 0))
    return pl.pallas_call(
        body, name="dqkv_post", out_shape=jax.ShapeDtypeStruct(dz.shape, BF16), grid=(seq // ROW_BLK,),
        in_specs=_group_specs(ATTN_W) + [tab_spec, tab_spec, tab_spec, pl.BlockSpec(memory_space=pl.ANY)],
        out_specs=pl.BlockSpec((ROW_BLK, QKV_W), lambda i: (i, 0)),
        scratch_shapes=[pltpu.VMEM((ROW_BLK, HEAD_DIM), F32)], input_output_aliases={6: 0},
        compiler_params=_params(dimension_semantics=("arbitrary",)),
    )(*d_gs, *tabs, dz)


def _glu(zg):
    a = zg[:, :CONV_CH].astype(F32)
    s = _sigmoid(zg[:, CONV_CH:].astype(F32))
    return a, s, a * s


def _shifted_copies(xs):
    n = xs.shape[1] - SUBLANES
    for b in range(1, SUBLANES):
        xs[b, 0:n, :] = xs[0, pl.ds(b, n), :]


def _shifted(xs, offset, r0, cs):
    a, b = divmod(offset, SUBLANES)
    return xs[b, pl.ds(SUBLANES * a + r0, CONV_ROWS), cs]


def _conv_fwd(z, cw, cb, lg, lb, comm=_NO_COMM):
    seq = z.shape[0]
    halo_per_blk = ROW_BLK // CONV_HALO

    def body(zg_ref, zh_ref, cw_ref, cb_ref, lg_ref, lb_ref, c2_ref, c4_ref, xs):
        i = pl.program_id(0)
        _, _, c1 = _glu(zg_ref[...])
        _, _, c1h = _glu(zh_ref[...])
        xs[0, 0:CONV_HALO, :] = jnp.where(i > 0, c1h, 0.0)
        xs[0, CONV_HALO:, :] = c1
        _shifted_copies(xs)
        for s in range(CONV_CH // HEAD_DIM):
            cs = slice(s * HEAD_DIM, (s + 1) * HEAD_DIM)
            taps = [cw_ref[j:j + 1, cs] for j in range(CONV_K)]
            bias = cb_ref[:, cs]

            def chunk(rc, carry, cs=cs, taps=taps, bias=bias):
                r0 = pl.multiple_of(rc * CONV_ROWS, CONV_ROWS)
                acc = [jnp.zeros((CONV_ROWS, HEAD_DIM), F32)] * 2
                for j in range(CONV_K):
                    acc[j % 2] = acc[j % 2] + taps[j] * _shifted(xs, CONV_HALO - (CONV_K - 1) + j, r0, cs)
                c2_ref[pl.ds(r0, CONV_ROWS), cs] = acc[0] + acc[1] + bias
                return carry

            lax.fori_loop(0, ROW_BLK // CONV_ROWS, chunk, 0)
        c2 = c2_ref[...]
        mu = jnp.mean(c2, axis=-1, keepdims=True)
        xc = c2 - mu
        rstd = lax.rsqrt(jnp.mean(xc * xc, axis=-1, keepdims=True) + EPS)
        c3 = xc * rstd * lg_ref[...] + lb_ref[...]
        c4_ref[...] = (c3 * _sigmoid(c3)).astype(BF16)

    vec = pl.BlockSpec((1, CONV_CH), lambda i: (0, 0))
    return _pcall(
        body, "conv_fwd", (seq // ROW_BLK,), [z, z, cw, cb, lg, lb],
        [pl.BlockSpec((ROW_BLK, 2 * CONV_CH), lambda i: (i, GLU_COL_BLK)),
         pl.BlockSpec((CONV_HALO, 2 * CONV_CH), lambda i: (jnp.maximum(i * halo_per_blk - 1, 0), GLU_COL_BLK)),
         pl.BlockSpec((CONV_HALO, CONV_CH), lambda i: (0, 0)), vec, vec, vec],
        [jax.ShapeDtypeStruct((seq, CONV_CH), F32), jax.ShapeDtypeStruct((seq, CONV_CH), BF16)],
        [pl.BlockSpec((ROW_BLK, CONV_CH), lambda i: (i, 0)), pl.BlockSpec((ROW_BLK, CONV_CH), lambda i: (i, 0))],
        [pltpu.VMEM((SUBLANES, ROW_BLK + CONV_HALO, CONV_CH), F32)], comm=comm)


def _conv_bwd(dc2, z, cw, dz, comm=_NO_COMM):
    seq = z.shape[0]
    halo_per_blk = ROW_BLK // CONV_HALO
    n_blk = seq // ROW_BLK
    last_halo = seq // CONV_HALO - 1

    def body(dc_ref, dn_ref, zg_ref, zh_ref, cw_ref, dz_any, o_ref, dcw_ref, xs, ys, dc1_ref, dcw_acc):
        del dz_any
        i = pl.program_id(0)
        a, s, c1 = _glu(zg_ref[...])
        _, _, c1h = _glu(zh_ref[...])
        xs[0, 0:CONV_HALO, :] = jnp.where(i > 0, c1h, 0.0)
        xs[0, CONV_HALO:, :] = c1
        ys[0, 0:ROW_BLK, :] = dc_ref[...]
        ys[0, ROW_BLK:, :] = jnp.where(i < n_blk - 1, dn_ref[...], 0.0)
        _shifted_copies(xs)
        _shifted_copies(ys)

        @pl.when(i == 0)
        def _():
            dcw_acc[...] = jnp.zeros_like(dcw_acc)

        for sl in range(CONV_CH // HEAD_DIM):
            cs = slice(sl * HEAD_DIM, (sl + 1) * HEAD_DIM)
            taps = [cw_ref[j:j + 1, cs] for j in range(CONV_K)]

            def chunk(rc, carry, cs=cs, taps=taps):
                r0 = pl.multiple_of(rc * CONV_ROWS, CONV_ROWS)
                dc = ys[0, pl.ds(r0, CONV_ROWS), cs]
                acc = [jnp.zeros((CONV_ROWS, HEAD_DIM), F32)] * 2
                for j in range(CONV_K):
                    prod = dc * _shifted(xs, CONV_HALO - (CONV_K - 1) + j, r0, cs)
                    dcw_acc[j, :, cs] += jnp.sum(prod.reshape(CONV_ROWS // SUBLANES, SUBLANES, HEAD_DIM), axis=0)
                    acc[j % 2] = acc[j % 2] + taps[j] * _shifted(ys, CONV_K - 1 - j, r0, cs)
                dc1_ref[pl.ds(r0, CONV_ROWS), cs] = acc[0] + acc[1]
                return carry

            lax.fori_loop(0, ROW_BLK // CONV_ROWS, chunk, 0)
        dc1 = dc1_ref[...]
        o_ref[:, :CONV_CH] = (dc1 * s).astype(BF16)
        o_ref[:, CONV_CH:] = (dc1 * a * s * (1.0 - s)).astype(BF16)

        @pl.when(i == n_blk - 1)
        def _():
            dcw_ref[...] = jnp.sum(dcw_acc[...], axis=1)

    return _pcall(
        body, "conv_bwd", (n_blk,), [dc2, dc2, z, z, cw, dz],
        [pl.BlockSpec((ROW_BLK, CONV_CH), lambda i: (i, 0)),
         pl.BlockSpec((CONV_HALO, CONV_CH), lambda i: (jnp.minimum((i + 1) * halo_per_blk, last_halo), 0)),
         pl.BlockSpec((ROW_BLK, 2 * CONV_CH), lambda i: (i, GLU_COL_BLK)),
         pl.BlockSpec((CONV_HALO, 2 * CONV_CH), lambda i: (jnp.maximum(i * halo_per_blk - 1, 0), GLU_COL_BLK)),
         pl.BlockSpec((CONV_HALO, CONV_CH), lambda i: (0, 0)),
         pl.BlockSpec(memory_space=pl.ANY)],
        [jax.ShapeDtypeStruct(dz.shape, BF16), jax.ShapeDtypeStruct((CONV_HALO, CONV_CH), F32)],
        [pl.BlockSpec((ROW_BLK, 2 * CONV_CH), lambda i: (i, GLU_COL_BLK)), pl.BlockSpec((CONV_HALO, CONV_CH), lambda i: (0, 0))],
        [pltpu.VMEM((SUBLANES, ROW_BLK + CONV_HALO, CONV_CH), F32), pltpu.VMEM((SUBLANES, ROW_BLK + CONV_HALO, CONV_CH), F32),
         pltpu.VMEM((ROW_BLK, CONV_CH), F32), pltpu.VMEM((CONV_HALO, SUBLANES, CONV_CH), F32)],
        aliases={5: 0}, comm=comm)


def _epi_mix(ya, c4, wcp, gates, bg):
    yc = _dot(c4, wcp, "nn")
    gv = _sigmoid(gates.astype(F32) + bg)
    merged = gv[:, :D_MODEL] * ya + gv[:, D_MODEL:] * yc
    return merged, ya, yc


def _epi_residual_rms(acc, xres, g):
    x = xres + acc
    return x, _rms_fwd_vals(x, g)


def _cross_scores(cq, ck):
    out = []
    for h in range(CROSS_HEADS):
        cs = slice(h * CROSS_HD, (h + 1) * CROSS_HD)
        s = _dot(cq[:, cs], ck[:, cs], "nt") * (CROSS_HD ** -0.5)
        e = jnp.exp(s - jnp.max(s, axis=-1, keepdims=True))
        out.append((cs, e, jnp.sum(e, axis=-1, keepdims=True)))
    return out


def _epi_cross_fwd(acc, ck, cv):
    cq = acc.astype(BF16)
    co = [_dot(e, cv[:, cs], "nn") / den for cs, e, den in _cross_scores(cq, ck)]
    return cq, jnp.concatenate(co, axis=1)


def _epi_cross_bwd(dco, cq, ck, cv):
    dco = dco.astype(BF16)
    dcq, dck, dcv = [], [], []
    for cs, e, den in _cross_scores(cq, ck):
        p = e / den
        dp = _dot(dco[:, cs], cv[:, cs], "nt")
        ds = (p * (dp - jnp.sum(dp * p, axis=-1, keepdims=True)) * (CROSS_HD ** -0.5)).astype(BF16)
        dcq.append(_dot(ds, ck[:, cs], "nn"))
        dck.append(_dot(ds, cq[:, cs], "tn"))
        dcv.append(_dot(p, dco[:, cs], "tn"))
    return jnp.concatenate(dcq, axis=1), jnp.concatenate(dck, axis=1), jnp.concatenate(dcv, axis=1)


def _epi_mlp_up(acc):
    return acc, jnp.square(jnp.maximum(acc, 0.0))


def _epi_final(acc, x2, tgt, g):
    x3 = x2 + acc
    err = _rms_fwd_vals(x3, g) - tgt
    loss = (0.5 / D_MODEL) * jnp.sum(err * err)
    dx3, dg = _rms_bwd_vals(x3, g, err * (1.0 / D_MODEL))
    return dx3, jnp.full((1, HEAD_DIM), loss, F32), dg


def _epi_mlp_down_bwd(dh, hpre):
    return (dh * 2.0 * jnp.maximum(hpre.astype(F32), 0.0),)


def _epi_rms_bwd(du, x, g, dres):
    dx, dg = _rms_bwd_vals(x, g, du)
    return dres.astype(F32) + dx, dg


def _epi_rms_bwd_g(du, x, g):
    return (_rms_bwd_vals(x, g, du)[1],)


def _epi_mix_bwd(dm, ya, yc, gates, bg):
    gv = _sigmoid(gates.astype(F32) + bg)
    ga, gb = gv[:, :D_MODEL], gv[:, D_MODEL:]
    ya, yc = ya.astype(F32), yc.astype(F32)
    dgate = jnp.concatenate([dm * ya * ga * (1.0 - ga), dm * yc * gb * (1.0 - gb)], axis=1)
    return dm * ga, dm * gb, dgate, jnp.sum(dgate, axis=0, keepdims=True)


def _epi_ln_bwd(dc4, c2, lg, lb):
    mu = jnp.mean(c2, axis=-1, keepdims=True)
    xc = c2 - mu
    rstd = lax.rsqrt(jnp.mean(xc * xc, axis=-1, keepdims=True) + EPS)
    xh = xc * rstd
    c3 = xh * lg + lb
    sg = _sigmoid(c3)
    dc3 = dc4 * sg * (1.0 + c3 * (1.0 - sg))
    dxh = dc3 * lg
    dc2 = rstd * (dxh - jnp.mean(dxh, axis=-1, keepdims=True) - xh * jnp.mean(dxh * xh, axis=-1, keepdims=True))
    return (dc2, jnp.sum(dc3 * xh, axis=0, keepdims=True), jnp.sum(dc3, axis=0, keepdims=True),
            jnp.sum(dc2, axis=0, keepdims=True))


def _sds(shape, dtype):
    return jax.ShapeDtypeStruct(shape, dtype)


class _Lazy:
    def __init__(self, fn):
        self.fn = fn

    def __getitem__(self, key):
        return self.fn(key)


def _local_step(x, mem, tgt, sm, plan):
    w = _Lazy(plan.w)
    dw = {}

    def carry(name, n_own, fn, *args, **kw):
        c = plan.comm(name, dw)
        res = fn(*args, comm=c, **kw)
        plan.done(name, res[n_own:])
        return res[:n_own]

    def mm(name, *args, **kw):
        return carry(name, len(args[6]), _mm, name, *args, **kw)

    seq = x.shape[0]
    nr = seq // ROW_BLK
    big = min(1024, seq)
    nb = seq // big
    row = lambda n: ((ROW_BLK, n), lambda i, j, k: (i, 0))
    vec = lambda n: ((1, n), lambda i, j, k: (0, 0))
    full = lambda r, c: ((r, c), lambda i, j, k: (0, 0))
    gates_blk = ((ROW_BLK, 2 * D_MODEL), lambda i, j, k: (i, GATE_COL_BLK))
    tabs = _rope_tables(seq)

    u = carry("rms_mix", 1, _rms_fwd, "rms_mix", x, sm["g_mix"], ROW_BLK)[0]
    whole3 = lambda a: (a.shape, lambda i, j, k: (0, 0, 0))
    z = plan.project_in(u)
    a_gs = carry("qkv_prep", 3, _qkv_prep, z, tabs)
    os_, ls_ = [], []
    for g in range(N_GROUPS):
        name = "attn_fwd_%d" % g
        o_g, l_g = carry(name, 2, _attn_fwd, name, a_gs[g])
        os_.append(o_g)
        ls_.append(l_g)
    attn, lt = _attn_merge(os_, ls_, seq)
    c2, c4 = carry("conv_fwd", 2, _conv_fwd, z, w["taps"], sm["conv_b"], sm["conv_ln_g"], sm["conv_ln_b"])
    merged, ya, yc = mm(
        "mix", attn, w["w_attn_proj"], "nn", (nr, 1, 1), row(GROUP_W), full(GROUP_W, D_MODEL),
        [(_sds((seq, D_MODEL), BF16), *row(D_MODEL))] * 3,
        extras=[(c4, *row(CONV_CH)), (w["w_conv_proj"], *full(CONV_CH, D_MODEL)), (z, *gates_blk), (sm["b_gate"], *vec(2 * D_MODEL))],
        epi=_epi_mix)
    x1, uq = mm("out_proj", merged, w["w_out"], "nn", (nr, 1, 1), row(D_MODEL), full(D_MODEL, D_MODEL),
                 [(_sds((seq, D_MODEL), F32), *row(D_MODEL)), (_sds((seq, D_MODEL), BF16), *row(D_MODEL))],
                 extras=[(x, *row(D_MODEL)), (sm["g_cross"], *vec(D_MODEL))], epi=_epi_residual_rms)

    mn = _rms_fwd("rms_mem", mem, sm["g_mem"], N_MEM)[0]
    ckv = mm("ckv_proj", mn, w["w_ckv"], "nn", (1, N_DEV, 1), full(N_MEM, D_MODEL),
              ((None, D_MODEL, 2 * D_MODEL // N_DEV), lambda i, j, k: (j, 0, 0)),
              [(_sds((N_MEM, 2 * D_MODEL), BF16), (N_MEM, 2 * D_MODEL // N_DEV), lambda i, j, k: (0, j))])[0]
    ck, cv = ckv[:, :D_MODEL], ckv[:, D_MODEL:]
    kv_blk = full(N_MEM, D_MODEL)
    cq, co = mm("cq_proj_cross", uq, w["w_cq"], "nn", (nr, 1, 1), row(D_MODEL), full(D_MODEL, D_MODEL),
                 [(_sds((seq, D_MODEL), BF16), *row(D_MODEL))] * 2,
                 extras=[(ck, *kv_blk), (cv, *kv_blk)], epi=_epi_cross_fwd)
    x2, um = mm("co_proj", co, w["w_co"], "nn", (nr, 1, 1), row(D_MODEL), full(D_MODEL, D_MODEL),
                 [(_sds((seq, D_MODEL), F32), *row(D_MODEL)), (_sds((seq, D_MODEL), BF16), *row(D_MODEL))],
                 extras=[(x1, *row(D_MODEL)), (sm["g_mlp"], *vec(D_MODEL))], epi=_epi_residual_rms)

    ff_blk = D_FF // N_DEV
    row_f32 = (_sds((seq, D_MODEL), F32), *row(D_MODEL))
    row_bf16 = (_sds((seq, D_MODEL), BF16), *row(D_MODEL))
    col_sum = (_sds((1, D_MODEL), F32), *vec(D_MODEL))
    hpre, h = mm("mlp_up", um, w["w_up"], "nn", (nr, 1, 1), row(D_MODEL), whole3(w["w_up"]),
                 [(_sds((seq, D_FF), BF16), *row(D_FF))] * 2, epi=_epi_mlp_up, split=("cols", N_DEV), b_resident=True)
    kt = D_FF // D_MODEL
    dx3, loss, dg_final = mm(
        "mlp_down_loss", h, w["w_down"], "nn", (nr, 1, 1), row(D_FF), full(D_FF, D_MODEL),
        [row_bf16, (_sds((1, HEAD_DIM), F32), *vec(HEAD_DIM)), col_sum],
        extras=[(x2, *row(D_MODEL)), (tgt, *row(D_MODEL)), (sm["g_final"], *vec(D_MODEL))], epi=_epi_final, acc_outs=(1, 2),
        b_resident=True)

    dhpre = mm("mlp_down_bwd", dx3, w["w_down"], "nt", (nr, 1, 1), row(D_MODEL), full(D_FF, D_MODEL),
               [(_sds((seq, D_FF), BF16), *row(D_FF))], extras=[(hpre, *row(D_FF))], epi=_epi_mlp_down_bwd,
               split=("cols", kt), b_resident=True)[0]
    big2 = min(2 * big, seq)
    nb2 = seq // big2
    dw["w_down"] = mm("dw_down", h, dx3, "tn", (kt, 1, nb2), ((big2, D_MODEL), lambda i, j, k: (k, i)),
                      ((big2, D_MODEL), lambda i, j, k: (k, 0)),
                      [(_sds((D_FF, D_MODEL), BF16), (D_MODEL, D_MODEL), lambda i, j, k: (i, 0))])[0]
    dx2, dg_mlp = mm("mlp_up_bwd", dhpre, w["w_up"], "nt", (nr, 1, 1), row(D_FF), whole3(w["w_up"]),
                     [row_bf16, col_sum],
                     extras=[(x2, *row(D_MODEL)), (sm["g_mlp"], *vec(D_MODEL)), (dx3, *row(D_MODEL))],
                     epi=_epi_rms_bwd, acc_outs=(1,), split=("sum", N_DEV), b_resident=True)
    dw["w_up"] = mm("dw_up", um, dhpre, "tn", (1, N_DEV, nb2), ((big2, D_MODEL), lambda i, j, k: (k, 0)),
                    ((big2, ff_blk), lambda i, j, k: (k, j)),
                    [(_sds((N_DEV, D_MODEL, ff_blk), BF16), (None, D_MODEL, ff_blk), lambda i, j, k: (j, 0, 0))])[0]

    acc_kv = (_sds((N_MEM, D_MODEL), F32), *kv_blk)
    dcq, dck, dcv = mm("co_proj_bwd_cross", dx2, w["w_co"], "nt", (nr, 1, 1), row(D_MODEL), full(D_MODEL, D_MODEL),
                       [row_bf16, acc_kv, acc_kv],
                       extras=[(cq, *row(D_MODEL)), (ck, *kv_blk), (cv, *kv_blk)], epi=_epi_cross_bwd, acc_outs=(1, 2))

    def dw_square(name, act, grad):
        return mm(name, act, grad, "tn", (1, 1, nb2), ((big2, D_MODEL), lambda i, j, k: (k, 0)),
                  ((big2, D_MODEL), lambda i, j, k: (k, 0)), [(_sds((D_MODEL, D_MODEL), BF16), *full(D_MODEL, D_MODEL))])[0]

    dw["w_co"] = dw_square("dw_co", co, dx2)
    dx1, dg_cross = mm("cq_proj_bwd", dcq, w["w_cq"], "nt", (nr, 1, 1), row(D_MODEL), full(D_MODEL, D_MODEL),
                       [row_bf16, col_sum],
                       extras=[(x1, *row(D_MODEL)), (sm["g_cross"], *vec(D_MODEL)), (dx2, *row(D_MODEL))],
                       epi=_epi_rms_bwd, acc_outs=(1,))
    dw["w_cq"] = dw_square("dw_cq", uq, dcq)
    dckv = jnp.concatenate([dck, dcv], axis=1)
    kv_chunk = 2 * D_MODEL // N_DEV
    dw["w_ckv"] = mm("dw_ckv", mn, dckv, "tn", (1, N_DEV, 1), full(N_MEM, D_MODEL), ((N_MEM, kv_chunk), lambda i, j, k: (0, j)),
                      [(_sds((N_DEV, D_MODEL, kv_chunk), BF16), (None, D_MODEL, kv_chunk), lambda i, j, k: (j, 0, 0))])[0]
    dg_mem = mm("ckv_proj_bwd", dckv, w["w_ckv"], "nt", (1, 1, N_DEV), ((N_MEM, kv_chunk), lambda i, j, k: (0, k)),
                 ((None, D_MODEL, kv_chunk), lambda i, j, k: (k, 0, 0)), [(_sds((1, D_MODEL), F32), *vec(D_MODEL))],
                 extras=[(mem, *full(N_MEM, D_MODEL)), (sm["g_mem"], *vec(D_MODEL))], epi=_epi_rms_bwd_g, acc_outs=(0,))[0]

    dya, dyc, dz, db_gate = mm(
        "out_proj_bwd_mix", dx1, w["w_out"], "nt", (nr, 1, 1), row(D_MODEL), full(D_MODEL, D_MODEL),
        [(_sds((seq, D_MODEL), BF16), *row(D_MODEL)), (_sds((seq, D_MODEL), BF16), *row(D_MODEL)),
         (_sds((seq, IN_W), BF16), *gates_blk), (_sds((1, 2 * D_MODEL), F32), *vec(2 * D_MODEL))],
        extras=[(ya, *row(D_MODEL)), (yc, *row(D_MODEL)), (z, *gates_blk), (sm["b_gate"], *vec(2 * D_MODEL))],
        epi=_epi_mix_bwd, acc_outs=(3,))
    dw["w_out"] = dw_square("dw_out", merged, dx1)
    dattn = mm("attn_proj_bwd", dya, w["w_attn_proj"], "nt", (nr, 1, 1), row(D_MODEL), full(GROUP_W, D_MODEL),
                [(_sds((seq, GROUP_W), BF16), *row(GROUP_W))])[0]
    pc = D_MODEL // N_DEV
    dw["w_attn_proj"] = mm("dw_attn_proj", attn, dya, "tn", (1, 1, nb2), ((big2, GROUP_W), lambda i, j, k: (k, 0)),
                           ((big2, D_MODEL), lambda i, j, k: (k, 0)),
                           [(_sds((N_DEV, GROUP_W, pc), BF16), (N_DEV, GROUP_W, pc), lambda i, j, k: (0, 0, 0))],
                           out_chunks=N_DEV)[0]
    cvec = (_sds((1, CONV_CH), F32), *vec(CONV_CH))
    dc2, dg_ln_g, dg_ln_b, dg_conv_b = mm(
        "conv_proj_bwd_ln", dyc, w["w_conv_proj"], "nt", (nr, 1, 1), row(D_MODEL), full(CONV_CH, D_MODEL),
        [(_sds((seq, CONV_CH), F32), *row(CONV_CH)), cvec, cvec, cvec],
        extras=[(c2, *row(CONV_CH)), (sm["conv_ln_g"], *vec(CONV_CH)), (sm["conv_ln_b"], *vec(CONV_CH))],
        epi=_epi_ln_bwd, acc_outs=(1, 2, 3))
    dw["w_conv_proj"] = mm("dw_conv_proj", c4, dyc, "tn", (1, 1, nb2), ((big2, CONV_CH), lambda i, j, k: (k, 0)),
                           ((big2, D_MODEL), lambda i, j, k: (k, 0)),
                           [(_sds((N_DEV, CONV_CH, pc), BF16), (N_DEV, CONV_CH, pc), lambda i, j, k: (0, 0, 0))],
                           out_chunks=N_DEV)[0]
    dz, dg_conv_w = carry("conv_bwd", 2, _conv_bwd, dc2, z, w["taps"], dz)
    preps = _attn_bwd_prep(dattn, attn, lt)
    d_gs = []
    for g in range(N_GROUPS):
        name = "attn_bwd_%d" % g
        d_gs.append(carry(name, 1, _attn_bwd, name, a_gs[g], *preps[g])[0])
    dz = _dqkv_post(d_gs, tabs, dz)
    small = dict(b_gate=db_gate, conv_b=dg_conv_b, conv_ln_g=dg_ln_g, conv_ln_b=dg_ln_b, g_cross=dg_cross,
                 g_mem=dg_mem, g_mlp=dg_mlp, g_final=dg_final, loss=loss, conv_w=dg_conv_w)
    dw["small"] = small
    dw["w_in"] = mm("dw_in", u, dz, "tn", (1, N_DEV, nb2), ((big2, D_MODEL), lambda i, j, k: (k, 0)),
                    ((big2, D_MODEL), lambda i, j, k: (k, j)),
                    [(_sds((N_DEV, D_MODEL, D_MODEL), BF16), (None, D_MODEL, D_MODEL), lambda i, j, k: (j, 0, 0))])[0]
    row_at = lambda n, first: ((ROW_BLK, n), lambda i, j, k: (i + first, 0))
    tile_kw = dict(epi=_epi_rms_bwd, acc_outs=(1,), split=("sum", N_DEV), b_resident=True)
    gx, dg_head = mm("in_proj_bwd", dz, w["w_in"], "nt", (nr - 1, 1, 1), row(IN_W), whole3(w["w_in"]), [row_f32, col_sum],
                     extras=[(x, *row(D_MODEL)), (sm["g_mix"], *vec(D_MODEL)), (dx1, *row(D_MODEL))], **tile_kw)
    grad_x, dg_tail = mm("in_proj_bwd_tail", dz, w["w_in"], "nt", (1, 1, 1), row_at(IN_W, nr - 1), whole3(w["w_in"]),
                         [(_sds((seq, D_MODEL), F32), *row_at(D_MODEL, nr - 1)), col_sum],
                         extras=[(x, *row_at(D_MODEL, nr - 1)), (sm["g_mix"], *vec(D_MODEL)), (dx1, *row_at(D_MODEL, nr - 1))],
                         into=gx, **tile_kw)
    small["g_mix"] = dg_head + dg_tail
    return grad_x, dw, small


SHARD_SHAPE = dict(w_in=(1024, 1024), w_attn_proj=(512, 128), w_conv_proj=(768, 128), w_out=(128, 1024),
                   w_cq=(128, 1024), w_ckv=(1024, 256), w_co=(128, 1024), w_up=(1024, 512), w_down=(512, 1024))
FWD_CARRY = {"in_proj": ("taps",), "qkv_prep": ("w_attn_proj", "w_conv_proj", "w_out", "w_cq", "w_ckv", "w_co"),
             "conv_fwd": ("w_up", "w_down")}
BWD_CARRY = {"dw_up": ("w_down",), "out_proj_bwd_mix": ("w_co", "w_cq"), "conv_bwd": ("w_up", "w_ckv"),
             "attn_bwd_0": ("w_out",), "attn_bwd_1": ("w_attn_proj", "w_conv_proj"), "in_proj_bwd": ("w_in",)}


SMALL_CARRIER = "dw_in"
SMALL_EARLY = ("b_gate", "conv_b", "conv_ln_g", "conv_ln_b", "g_cross", "g_mem", "g_mlp", "g_final", "loss", "conv_w")


def _cols_to_2d(a):
    return a.transpose(1, 0, 2).reshape(a.shape[1], -1)


def _in_proj_gather(u, w_shard, comm):
    seq = u.shape[0]
    tm = min(1024, seq)
    x, y, c = lax.axis_index("x"), lax.axis_index("y"), lax.axis_index("c")
    ident = lambda px, py, pc: 4 * px + 2 * py + pc
    far = [(1 - x, y), (x, 1 - y), (1 - x, 1 - y)]
    order = jnp.stack([ident(x, y, c), ident(x, y, 1 - c), ident(*far[0], c), ident(*far[1], c), ident(*far[0], 1 - c),
                       ident(*far[1], 1 - c), ident(*far[2], c), ident(*far[2], 1 - c)]).astype(jnp.int32)
    forward_at = {2: 0, 3: 1, 6: 2}
    n_far = len(far)

    def body(order_ref, u_ref, wsh_ref, *rest):
        c_in, z_ref, wg_ref = rest[:comm.n], rest[comm.n], rest[comm.n + 1]
        c_out = rest[comm.n + 2:2 * comm.n + 2]
        wbuf, load_sem, local_sem, recv_sems, ici_send, d2d_send = rest[2 * comm.n + 2:2 * comm.n + 8]
        sems = rest[2 * comm.n + 8:]
        jj, i = pl.program_id(0), pl.program_id(1)
        (kx, ky, kc), me, chips = _Comm._where()
        sibling = (kx, ky, 1 - kc)
        n = order_ref[jj]

        def push(src, blk, send, to):
            return pltpu.make_async_remote_copy(src_ref=src, dst_ref=wg_ref.at[blk], send_sem=send,
                                                recv_sem=recv_sems.at[blk], device_id=to, device_id_type=MESH)

        def load(src):
            cp = pltpu.make_async_copy(src, wbuf, load_sem)
            cp.start()
            cp.wait()

        @pl.when(jnp.logical_and(jj == 0, i == 0))
        def _():
            push(wsh_ref, me, d2d_send, sibling).start()
            for (px, py) in chips[:2]:
                push(wsh_ref, me, ici_send, (px, py, kc)).start()
            pltpu.make_async_copy(wsh_ref, wg_ref.at[me], local_sem).start()
            load(wsh_ref)

        @pl.when(jnp.logical_and(jj > 0, i == 0))
        def _():
            push(wg_ref.at[n], n, d2d_send, sibling).wait_recv()
            for step, k in forward_at.items():
                @pl.when(jj == step)
                def _(k=k):
                    blk = 4 * chips[k][0] + 2 * chips[k][1] + kc
                    push(wg_ref.at[blk], blk, d2d_send, sibling).start()

            @pl.when(jj == 2)
            def _():
                push(wsh_ref, me, ici_send, (*chips[2], kc)).start()

            if comm.n:
                @pl.when(jj == 3)
                def _():
                    comm.start(c_in, c_out, sems)

            load(wg_ref.at[n])

        for r in range(0, tm, SUB_ROWS):
            z_ref[r:r + SUB_ROWS, :] = _dot(u_ref[r:r + SUB_ROWS, :], wbuf[...], "nn").astype(BF16)

        @pl.when(jnp.logical_and(jj == N_DEV - 1, i == pl.num_programs(1) - 1))
        def _():
            def drain_sends(send, count):
                blocks = wg_ref.at[pl.ds(0, count)]
                pltpu.make_async_remote_copy(src_ref=blocks, dst_ref=blocks, send_sem=send, recv_sem=recv_sems.at[0],
                                             device_id=sibling, device_id_type=MESH).wait_send()

            drain_sends(ici_send, n_far)
            drain_sends(d2d_send, n_far + 1)
            pltpu.make_async_copy(wsh_ref, wg_ref.at[me], local_sem).wait()
            if comm.n:
                comm.wait(c_in, c_out, sems)

    any_spec = pl.BlockSpec(memory_space=pl.ANY)
    grid_spec = pltpu.PrefetchScalarGridSpec(
        num_scalar_prefetch=1, grid=(N_DEV, seq // tm),
        in_specs=[pl.BlockSpec((tm, D_MODEL), lambda jj, i, order_ref: (i, 0)), any_spec] + comm.in_specs,
        out_specs=[pl.BlockSpec((tm, D_MODEL), lambda jj, i, order_ref: (i, order_ref[jj])), any_spec] + comm.out_specs,
        scratch_shapes=[pltpu.VMEM((D_MODEL, D_MODEL), BF16), pltpu.SemaphoreType.DMA, pltpu.SemaphoreType.DMA,
                        pltpu.SemaphoreType.DMA((N_DEV,)), pltpu.SemaphoreType.DMA, pltpu.SemaphoreType.DMA] + comm.scratch)
    return pl.pallas_call(
        body, name="in_proj_gather", grid_spec=grid_spec,
        out_shape=[jax.ShapeDtypeStruct((seq, IN_W), BF16), jax.ShapeDtypeStruct((N_DEV, D_MODEL, D_MODEL), BF16)] + comm.out_shape,
        compiler_params=_params(dimension_semantics=("arbitrary", "arbitrary")),
    )(order, u, w_shard, *comm.arrays)


class _Plan:
    def __init__(self, shards, n_tap_cols):
        self.shards = shards
        self.gathered = {}
        self.parts = {}
        self.small_parts = {}
        self.n_tap_cols = n_tap_cols

    def project_in(self, u):
        comm = self.comm("in_proj", None)
        res = _in_proj_gather(u, self.shards["w_in"], comm)
        self.gathered["w_in"] = res[1]
        self.done("in_proj", res[2:])
        return res[0]

    def comm(self, name, dw):
        if name in FWD_CARRY:
            return _Comm(replicated=[self.shards[k] for k in FWD_CARRY[name]])
        if name in BWD_CARRY:
            return _Comm(chunked=[dw[k].reshape((N_DEV,) + SHARD_SHAPE[k]) for k in BWD_CARRY[name]])
        if name == SMALL_CARRIER:
            return _Comm(replicated=[dw["small"][k] for k in SMALL_EARLY])
        return _NO_COMM

    def done(self, name, got):
        if name in FWD_CARRY:
            self.gathered.update(zip(FWD_CARRY[name], got))
        elif name in BWD_CARRY:
            self.parts.update(zip(BWD_CARRY[name], got))
        elif name == SMALL_CARRIER:
            self.small_parts.update(zip(SMALL_EARLY, got))

    def w(self, key):
        g = self.gathered[key]
        if key in ("w_in", "w_up", "w_ckv"):
            return g
        if key in ("w_attn_proj", "w_conv_proj"):
            return _cols_to_2d(g)
        if key == "taps":
            return jnp.pad(_cols_to_2d(g[:, :CONV_K, :self.n_tap_cols]), ((0, 1), (0, 0)))
        return g.reshape(-1, g.shape[-1])


def _adamw(name, w, m, v, parts, comm=_NO_COMM):
    rows, cols = w.shape
    n_parts = parts.shape[0]
    rb = rows if rows <= 256 or rows % 256 else 256

    def body(w_ref, m_ref, v_ref, p_ref, g_ref, d_ref, nm_ref, nv_ref):
        g = p_ref[0].astype(F32)
        for q in range(1, n_parts):
            g = g + p_ref[q].astype(F32)
        wv = w_ref[...]
        nm = ADAM_B1 * m_ref[...] + (1.0 - ADAM_B1) * g
        nv = ADAM_B2 * v_ref[...] + (1.0 - ADAM_B2) * jnp.square(g)
        m_hat = nm / (1.0 - ADAM_B1 ** ADAM_STEP)
        v_hat = nv / (1.0 - ADAM_B2 ** ADAM_STEP)
        g_ref[...] = g
        d_ref[...] = -ADAM_LR * (m_hat / (jnp.sqrt(v_hat) + ADAM_EPS) + ADAM_WD * wv)
        nm_ref[...] = nm
        nv_ref[...] = nv

    blk = pl.BlockSpec((rb, cols), lambda i: (i, 0))
    return _pcall(body, name, (rows // rb,), [w, m, v, parts],
                  [blk, blk, blk, pl.BlockSpec((n_parts, rb, cols), lambda i: (0, i, 0))],
                  [jax.ShapeDtypeStruct((rows, cols), F32)] * 4, [blk] * 4, comm=comm)


def _sum_parts(name, parts):
    def body(p_ref, o_ref):
        acc = p_ref[0]
        for q in range(1, parts.shape[0]):
            acc = acc + p_ref[q]
        o_ref[...] = acc

    return _pcall(body, name, (1,), [parts], [pl.BlockSpec(parts.shape, lambda i: (0, 0, 0))],
                  [jax.ShapeDtypeStruct(parts.shape[1:], F32)], [pl.BlockSpec(parts.shape[1:], lambda i: (0, 0))])[0]


BIG = ("w_in", "w_attn_proj", "w_conv_proj", "w_out", "w_cq", "w_ckv", "w_co", "w_up", "w_down")
SMALL = ("g_mix", "b_gate", "conv_b", "conv_ln_g", "conv_ln_b", "g_cross", "g_mem", "g_mlp", "g_final")
WEIGHTS = ("g_mix", "w_in", "b_gate", "conv_w", "conv_b", "conv_ln_g", "conv_ln_b", "w_attn_proj", "w_conv_proj", "w_out",
           "g_cross", "g_mem", "w_cq", "w_ckv", "w_co", "g_mlp", "w_up", "w_down", "g_final")


def kernel(x, mem, g_mix, w_in, b_gate, conv_w, conv_b, conv_ln_g, conv_ln_b, w_attn_proj, w_conv_proj, w_out, g_cross, g_mem, w_cq, w_ckv, w_co, g_mlp, w_up, w_down, g_final, loss_target, m_g_mix, m_w_in, m_b_gate, m_conv_w, m_conv_b, m_conv_ln_g, m_conv_ln_b, m_w_attn_proj, m_w_conv_proj, m_w_out, m_g_cross, m_g_mem, m_w_cq, m_w_ckv, m_w_co, m_g_mlp, m_w_up, m_w_down, m_g_final, v_g_mix, v_w_in, v_b_gate, v_conv_w, v_conv_b, v_conv_ln_g, v_conv_ln_b, v_w_attn_proj, v_w_conv_proj, v_w_out, v_g_cross, v_g_mem, v_w_cq, v_w_ckv, v_w_co, v_g_mlp, v_w_up, v_w_down, v_g_final):
    args = dict(locals())
    wts = {k: args[k] for k in WEIGHTS}
    mom = {k: args["m_" + k] for k in WEIGHTS}
    var = {k: args["v_" + k] for k in WEIGHTS}
    two_d = lambda a: a.reshape(a.shape[-2:]) if a.ndim == 3 else a.reshape(1, -1)

    shards = {k: two_d(wts[k]).astype(BF16) for k in BIG}
    shards["taps"] = jnp.pad(two_d(conv_w), ((0, 1), (0, HEAD_DIM - conv_w.shape[-1])))
    plan = _Plan(shards, conv_w.shape[-1])
    sm = {k: two_d(wts[k]) for k in SMALL}

    grad_x, _, small = _local_step(x[0], mem[0], loss_target[0], sm, plan)
    parts = plan.parts

    out = {}
    small_parts = plan.small_parts
    late = "w_attn_proj"
    for k in BIG:
        res = _adamw("adamw_" + k, two_d(wts[k]), two_d(mom[k]), two_d(var[k]), parts[k],
                     comm=_Comm(replicated=[small["g_mix"]]) if k == late else _NO_COMM)
        out[k] = [r.reshape(wts[k].shape) for r in res[:4]]
        if k == late:
            small_parts["g_mix"] = res[4]
    for k in SMALL:
        res = _adamw("adamw_" + k, two_d(wts[k]), two_d(mom[k]), two_d(var[k]), small_parts[k])
        out[k] = [r.reshape(wts[k].shape) for r in res]
    loss = _sum_parts("loss_sum", small_parts["loss"])[0, 0]
    me = 4 * lax.axis_index("x") + 2 * lax.axis_index("y") + lax.axis_index("c")
    n_tap_cols = conv_w.shape[-1]
    tap_parts = lax.dynamic_slice(small_parts["conv_w"], (0, 0, me * n_tap_cols), (N_DEV, CONV_K, n_tap_cols))
    res = _adamw("adamw_conv_w", two_d(conv_w), two_d(m_conv_w), two_d(v_conv_w), tap_parts)
    out["conv_w"] = [r.reshape(conv_w.shape) for r in res]

    return (loss, grad_x[None], *[out[k][0] for k in WEIGHTS], *[out[k][1] for k in WEIGHTS],
            *[out[k][2] for k in WEIGHTS], *[out[k][3] for k in WEIGHTS])
```

```python
import functools

import jax
import jax.numpy as jnp
import numpy as np
from jax import lax
from jax.experimental import pallas as pl
from jax.experimental.pallas import tpu as pltpu

F32 = jnp.float32
BF16 = jnp.bfloat16

N_DEV = 8
D_MODEL = 1024
N_MEM = 256
HEAD_DIM = 128
HEADS_PER_GROUP = 4
GROUP_W = HEADS_PER_GROUP * HEAD_DIM
DILATIONS = (1, 4, 16)
BAND = 128
N_GROUPS = 3
ATTN_W = N_GROUPS * GROUP_W
QKV_W = 3 * ATTN_W
ROT_DIM = HEAD_DIM // 4
ROPE_THETA = 500000.0
CONV_CH = 768
CONV_K = 31
CONV_HALO = 32
SUBLANES = 8
CONV_ROWS = 64
IN_W = 8192
GLU_COL_BLK = QKV_W // (2 * CONV_CH)
GATE_COL_BLK = (QKV_W + 2 * CONV_CH) // (2 * D_MODEL)
CROSS_HEADS = 4
CROSS_HD = D_MODEL // CROSS_HEADS
D_FF = 4096
EPS = 1e-6
NEG = -1e30
QB = 4
ROW_BLK = QB * BAND

ADAM_LR = 0.001
ADAM_B1 = 0.9
ADAM_B2 = 0.999
ADAM_EPS = 1e-08
ADAM_WD = 0.01
ADAM_STEP = 10

VMEM_LIMIT = 56 * 1024 * 1024
MESH = pl.DeviceIdType.MESH


def _params(**kw):
    return pltpu.CompilerParams(vmem_limit_bytes=VMEM_LIMIT, **kw)


def _sigmoid(x):
    return 1.0 / (1.0 + jnp.exp(-x))


def _dot(a, b, kind):
    dims = {"nn": (((1,), (0,)), ((), ())), "nt": (((1,), (1,)), ((), ())), "tn": (((0,), (0,)), ((), ()))}[kind]
    if a.dtype != BF16:
        a = a.astype(BF16)
    if b.dtype != BF16:
        b = b.astype(BF16)
    return lax.dot_general(a, b, dims, preferred_element_type=F32)


def _peers():
    x, y, c = lax.axis_index("x"), lax.axis_index("y"), lax.axis_index("c")
    me = 4 * x + 2 * y + c
    peers = [(x, y, 1 - c), (1 - x, y, c), (x, 1 - y, c), (1 - x, 1 - y, c),
             (1 - x, y, 1 - c), (x, 1 - y, 1 - c), (1 - x, 1 - y, 1 - c)]
    return me, peers


class _Comm:
    def __init__(self, chunked=(), replicated=()):
        self.arrays = list(chunked) + list(replicated)
        self.n_c = len(chunked)
        self.n = len(self.arrays)
        self.out_shape = [jax.ShapeDtypeStruct(a.shape, a.dtype) for a in chunked]
        self.out_shape += [jax.ShapeDtypeStruct((N_DEV,) + a.shape, a.dtype) for a in replicated]
        self.in_specs = [pl.BlockSpec(memory_space=pl.ANY)] * self.n
        self.out_specs = [pl.BlockSpec(memory_space=pl.ANY)] * self.n
        self.scratch = [pltpu.SemaphoreType.DMA((self.n,))] * 5 if self.n else []

    @staticmethod
    def _where():
        x, y, c = lax.axis_index("x"), lax.axis_index("y"), lax.axis_index("c")
        chips = [(1 - x, y), (x, 1 - y), (1 - x, 1 - y)]
        return (x, y, c), 4 * x + 2 * y + c, chips

    def _local(self, ins, outs, sems, a, me):
        src = ins[a].at[me] if a < self.n_c else ins[a]
        return pltpu.make_async_copy(src, outs[a].at[me], sems[2].at[a])

    @staticmethod
    def _remote(src, dst, send, recv, to):
        return pltpu.make_async_remote_copy(src_ref=src, dst_ref=dst, send_sem=send, recv_sem=recv, device_id=to,
                                            device_id_type=MESH)

    def start(self, ins, outs, sems):
        (x, y, c), me, chips = self._where()
        for a in range(self.n):
            self._local(ins, outs, sems, a, me).start()
            if a < self.n_c:
                for (px, py, pc) in _peers()[1]:
                    self._remote(ins[a].at[4 * px + 2 * py + pc], outs[a].at[me], sems[0].at[a], sems[1].at[a], (px, py, pc)).start()
            else:
                self._remote(ins[a], outs[a].at[me], sems[3].at[a], sems[4].at[a], (x, y, 1 - c)).start()
                for (px, py) in chips:
                    self._remote(ins[a], outs[a].at[me], sems[0].at[a], sems[1].at[a], (px, py, c)).start()

    def wait(self, ins, outs, sems):
        (x, y, c), me, chips = self._where()
        sibling = (x, y, 1 - c)

        def drain(a, pair, count):
            blocks = outs[a].at[pl.ds(0, count)]
            cp = self._remote(blocks, blocks, sems[pair].at[a], sems[pair + 1].at[a], sibling)
            cp.wait_send()
            cp.wait_recv()

        for a in range(self.n):
            if a < self.n_c:
                drain(a, 0, N_DEV - 1)
            else:
                drain(a, 0, len(chips))
                for (px, py) in chips:
                    blk = outs[a].at[4 * px + 2 * py + c]
                    self._remote(blk, blk, sems[3].at[a], sems[4].at[a], sibling).start()
        for a in range(self.n):
            if a >= self.n_c:
                drain(a, 3, len(chips) + 1)
            self._local(ins, outs, sems, a, me).wait()


_NO_COMM = _Comm()


def _pcall(body, name, grid, operands, in_specs, out_shape, out_specs, scratch=(), aliases=None, comm=_NO_COMM, **params):
    n_in, n_out, n_scr = len(operands), len(out_shape), len(scratch)
    grid = tuple(grid)

    def carried(*refs):
        ins, c_in = refs[:n_in], refs[n_in:n_in + comm.n]
        o0 = n_in + comm.n
        outs, c_out = refs[o0:o0 + n_out], refs[o0 + n_out:o0 + n_out + comm.n]
        s0 = o0 + n_out + comm.n
        scr, sems = refs[s0:s0 + n_scr], refs[s0 + n_scr:]
        ids = [pl.program_id(ax) for ax in range(len(grid))]

        @pl.when(functools.reduce(jnp.logical_and, [p == 0 for p in ids]))
        def _():
            comm.start(c_in, c_out, sems)

        body(*ins, *outs, *scr)

        @pl.when(functools.reduce(jnp.logical_and, [p == g - 1 for p, g in zip(ids, grid)]))
        def _():
            comm.wait(c_in, c_out, sems)

    return pl.pallas_call(
        carried if comm.n else body, name=name, grid=grid, in_specs=list(in_specs) + comm.in_specs,
        out_shape=list(out_shape) + comm.out_shape, out_specs=list(out_specs) + comm.out_specs,
        scratch_shapes=list(scratch) + comm.scratch, input_output_aliases=aliases or {},
        compiler_params=_params(dimension_semantics=("arbitrary",) * len(grid), **params),
    )(*operands, *comm.arrays)


def _mm(name, a, b, kind, grid, a_blk, b_blk, outs, extras=(), epi=None, acc_outs=(), j_outer=False, comm=_NO_COMM,
        split=None, b_resident=False, out_chunks=0):
    gi, gj, gk = grid
    n_ex = len(extras)
    n_out = len(outs)
    mode, n_chunks = split if split is not None else (None, 1)

    def spec(blk, fn, **kw):
        return pl.BlockSpec(blk, (lambda j, i, k: fn(i, j, k)) if j_outer else fn, **kw)

    def b_chunk(b_ref, c):
        if len(b_ref.shape) == 3:
            return b_ref[c]
        rows, cols = b_ref.shape
        if (kind == "nn") == (mode == "cols"):
            return b_ref[:, c * (cols // n_chunks):(c + 1) * (cols // n_chunks)]
        return b_ref[c * (rows // n_chunks):(c + 1) * (rows // n_chunks), :]

    def col_chunk(ref, c):
        width = ref.shape[-1] // n_chunks
        return slice(c * width, (c + 1) * width)

    def body(*refs):
        a_ref, b_ref = refs[0], refs[1]
        ex = refs[2:2 + n_ex]
        out_refs = refs[2 + n_ex:2 + n_ex + n_out]
        acc_ref = refs[2 + n_ex + n_out] if gk > 1 else None
        i = pl.program_id(1 if j_outer else 0)
        k = pl.program_id(2)
        if mode == "cols":
            a_val = a_ref[...]
            for c in range(n_chunks):
                acc = _dot(a_val, b_chunk(b_ref, c), kind)
                vals = epi(acc, *[e[:, col_chunk(e, c)] for e in ex]) if epi is not None else (acc,)
                for o, v in zip(out_refs, vals):
                    o[:, col_chunk(o, c)] = v.astype(o.dtype)
            return
        if mode == "sum":
            part = _dot(a_ref[:, col_chunk(a_ref, 0)], b_chunk(b_ref, 0), kind)
            for c in range(1, n_chunks):
                part = part + _dot(a_ref[:, col_chunk(a_ref, c)], b_chunk(b_ref, c), kind)
        else:
            part = _dot(a_ref[...], b_ref[...], kind)

        def finish(acc):
            if out_chunks:
                width = acc.shape[-1] // out_chunks
                for c in range(out_chunks):
                    out_refs[0][c] = acc[:, c * width:(c + 1) * width].astype(out_refs[0].dtype)
                return
            vals = epi(acc, *[e[...] for e in ex]) if epi is not None else (acc,)
            for idx, (o, v) in enumerate(zip(out_refs, vals)):
                if idx in acc_outs:
                    @pl.when(i == 0)
                    def _():
                        o[...] = v.astype(o.dtype)

                    @pl.when(i != 0)
                    def _():
                        o[...] += v.astype(o.dtype)
                else:
                    o[...] = v.astype(o.dtype)

        if gk == 1:
            finish(part)
        else:
            @pl.when(k == 0)
            def _():
                acc_ref[...] = part

            @pl.when(k != 0)
            def _():
                acc_ref[...] += part

            @pl.when(k == gk - 1)
            def _():
                finish(acc_ref[...])

    scratch = []
    if gk > 1:
        tm = a_blk[0][-1] if kind == "tn" else a_blk[0][-2]
        tn = b_blk[0][-2] if kind == "nt" else b_blk[0][-1]
        scratch = [pltpu.VMEM((tm, tn), F32)]
    b_kw = dict(pipeline_mode=pl.Buffered(1)) if b_resident else {}
    return _pcall(body, name, (gj, gi, gk) if j_outer else (gi, gj, gk), [a, b] + [e for e, _, _ in extras],
                  [spec(*a_blk), spec(*b_blk, **b_kw)] + [spec(blk, fn) for _, blk, fn in extras],
                  [s for s, _, _ in outs], [spec(blk, fn) for _, blk, fn in outs], scratch, comm=comm)


def _rms_fwd_vals(x, g):
    r = lax.rsqrt(jnp.mean(x * x, axis=-1, keepdims=True) + EPS)
    return x * r * g


def _rms_bwd_vals(x, g, du):
    r = lax.rsqrt(jnp.mean(x * x, axis=-1, keepdims=True) + EPS)
    xh = x * r
    dxh = du * g
    dx = r * (dxh - xh * jnp.mean(dxh * xh, axis=-1, keepdims=True))
    return dx, jnp.sum(du * xh, axis=0, keepdims=True)


def _rms_fwd(name, x, g, rows, comm=_NO_COMM):
    n = x.shape[0]

    def body(x_ref, g_ref, o_ref):
        o_ref[...] = _rms_fwd_vals(x_ref[...], g_ref[...]).astype(BF16)

    return _pcall(body, name, (n // rows,), [x, g],
                  [pl.BlockSpec((rows, D_MODEL), lambda i: (i, 0)), pl.BlockSpec((1, D_MODEL), lambda i: (0, 0))],
                  [jax.ShapeDtypeStruct(x.shape, BF16)], [pl.BlockSpec((rows, D_MODEL), lambda i: (i, 0))], comm=comm)


def _rope_tables(seq):
    half = ROT_DIM // 2
    pos = np.arange(seq, dtype=np.float32)
    inv_freq = np.float32(ROPE_THETA) ** (-np.arange(0, ROT_DIM, 2, dtype=np.float32) / np.float32(ROT_DIM))
    ang = (pos[:, None] * inv_freq[None, :]).astype(np.float32)
    cos, sin = np.cos(ang), np.sin(ang)
    rest = HEAD_DIM - ROT_DIM
    c = np.concatenate([cos, cos, np.ones((seq, rest), np.float32)], axis=1)
    s1 = np.concatenate([np.zeros((seq, half), np.float32), sin, np.zeros((seq, rest), np.float32)], axis=1)
    s2 = np.concatenate([-sin, np.zeros((seq, half + rest), np.float32)], axis=1)
    return jnp.asarray(c), jnp.asarray(s1), jnp.asarray(s2)


def _group_shapes(seq, width, dtype):
    return [jax.ShapeDtypeStruct((d, seq // d, width), dtype) for d in DILATIONS]


def _group_specs(width):
    return [pl.BlockSpec((d, ROW_BLK // d, width), lambda i: (0, i, 0)) for d in DILATIONS]


def _qkv_prep(z, tabs, comm=_NO_COMM):
    seq = z.shape[0]

    def body(z_ref, c_ref, s1_ref, s2_ref, a0, a1, a2, sc):
        outs = (a0, a1, a2)
        c, s1, s2 = c_ref[...], s1_ref[...], s2_ref[...]
        for part in range(3):
            for hh in range(N_GROUPS * HEADS_PER_GROUP):
                g, hl = divmod(hh, HEADS_PER_GROUP)
                col = part * ATTN_W + hh * HEAD_DIM
                ocol = part * GROUP_W + hl * HEAD_DIM
                x = z_ref[:, col:col + HEAD_DIM].astype(F32)
                if part < 2:
                    x = x * c + pltpu.roll(x, ROT_DIM // 2, 1) * s1 + pltpu.roll(x, HEAD_DIM - ROT_DIM // 2, 1) * s2
                d = DILATIONS[g]
                if d == 1:
                    outs[g][0, :, ocol:ocol + HEAD_DIM] = x.astype(BF16)
                else:
                    sc[...] = x
                    for r in range(d):
                        outs[g][r, :, ocol:ocol + HEAD_DIM] = sc[pl.ds(r, ROW_BLK // d, stride=d), :].astype(BF16)

    tab_spec = pl.BlockSpec((ROW_BLK, HEAD_DIM), lambda i: (i, 0))
    return _pcall(body, "qkv_prep", (seq // ROW_BLK,), [z, *tabs],
                  [pl.BlockSpec((ROW_BLK, QKV_W), lambda i: (i, 0)), tab_spec, tab_spec, tab_spec],
                  _group_shapes(seq, ATTN_W, BF16), _group_specs(ATTN_W), [pltpu.VMEM((ROW_BLK, HEAD_DIM), F32)], comm=comm)


def _band_masks_2(t):
    qi = lax.broadcasted_iota(jnp.int32, (BAND, 2 * BAND), 0)
    kj = lax.broadcasted_iota(jnp.int32, (BAND, 2 * BAND), 1)
    band = jnp.logical_and(kj >= qi, kj <= qi + BAND)
    return band, jnp.logical_and(band, jnp.logical_or(kj >= BAND, t > 0))


def _attn_fwd(name, a_g, comm=_NO_COMM):
    dil, m_len, _ = a_g.shape
    qb = min(QB, m_len // BAND)
    rows = qb * BAND
    steps = m_len // rows
    scale = HEAD_DIM ** -0.5

    tiles = [(sb, h) for sb in range(qb) for h in range(HEADS_PER_GROUP)]

    def body(q_ref, kc_ref, vc_ref, kp_ref, vp_ref, o_ref, l_ref, k_all, v_all, s_scr, p_scr, r_scr):
        t = pl.program_id(1)
        k_all[0:BAND, :] = kp_ref[...]
        k_all[BAND:, :] = kc_ref[...]
        v_all[0:BAND, :] = vp_ref[...]
        v_all[BAND:, :] = vc_ref[...]
        band, band_first = _band_masks_2(t)
        for idx, (sb, h) in enumerate(tiles):
            cs = slice(h * HEAD_DIM, (h + 1) * HEAD_DIM)
            s = _dot(q_ref[sb * BAND:(sb + 1) * BAND, cs], k_all[sb * BAND:(sb + 2) * BAND, cs], "nt") * scale
            s_scr[idx] = jnp.where(band_first if sb == 0 else band, s, NEG)
        lane = lax.broadcasted_iota(jnp.int32, (BAND, HEAD_DIM), 1)
        lse_rows = [jnp.zeros((BAND, HEAD_DIM), F32)] * qb
        for idx, (sb, h) in enumerate(tiles):
            s = s_scr[idx]
            mx = jnp.max(s, axis=-1, keepdims=True)
            p = jnp.exp(s - mx)
            den = jnp.sum(p, axis=-1, keepdims=True)
            p_scr[idx] = p.astype(BF16)
            r_scr[idx] = jnp.broadcast_to(1.0 / den, (BAND, HEAD_DIM))
            lse_rows[sb] = jnp.where(lane == h, jnp.broadcast_to(mx + jnp.log(den), (BAND, HEAD_DIM)), lse_rows[sb])
        for sb in range(qb):
            l_ref[sb * BAND:(sb + 1) * BAND, :] = lse_rows[sb]
        for idx, (sb, h) in enumerate(tiles):
            cs = slice(h * HEAD_DIM, (h + 1) * HEAD_DIM)
            o_ref[sb * BAND:(sb + 1) * BAND, cs] = _dot(p_scr[idx], v_all[sb * BAND:(sb + 2) * BAND, cs], "nn") * r_scr[idx]

    def prev(r, t):
        return jnp.maximum(qb * t - 1, 0)

    cur = lambda c: pl.BlockSpec((None, rows, GROUP_W), lambda r, t, c=c: (r, t, c))
    prv = lambda c: pl.BlockSpec((None, BAND, GROUP_W), lambda r, t, c=c: (r, prev(r, t), c))
    out_spec = lambda width: pl.BlockSpec((None, rows, width), lambda r, t: (r, t, 0))
    shp = lambda width: jax.ShapeDtypeStruct((dil, m_len, width), F32)
    n_t = len(tiles)
    return _pcall(body, name, (dil, steps), [a_g] * 5, [cur(0), cur(1), cur(2), prv(1), prv(2)],
                  [shp(GROUP_W), shp(HEAD_DIM)], [out_spec(GROUP_W), out_spec(HEAD_DIM)],
                  [pltpu.VMEM((rows + BAND, GROUP_W), BF16), pltpu.VMEM((rows + BAND, GROUP_W), BF16),
                   pltpu.VMEM((n_t, BAND, 2 * BAND), F32), pltpu.VMEM((n_t, BAND, 2 * BAND), BF16),
                   pltpu.VMEM((n_t, BAND, HEAD_DIM), F32)], comm=comm)


def _attn_merge(os_, ls_, seq):
    def body(o0, l0, o1, l1, o2, l2, at_ref, lt_ref, sc, lsc):
        for gi, l_r in enumerate((l1, l2)):
            d = DILATIONS[gi + 1]
            for r in range(d):
                lsc.at[gi][pl.ds(r, ROW_BLK // d, stride=d), :] = l_r[r]
        lse = (l0.at[0], lsc.at[0], lsc.at[1])
        lane = lax.broadcasted_iota(jnp.int32, (ROW_BLK, HEAD_DIM), 1)
        lt_rows = jnp.zeros((ROW_BLK, HEAD_DIM), F32)
        for h in range(HEADS_PER_GROUP):
            cs = slice(h * HEAD_DIM, (h + 1) * HEAD_DIM)
            for gi, o_r in enumerate((o1, o2)):
                d = DILATIONS[gi + 1]
                for r in range(d):
                    sc.at[gi][pl.ds(r, ROW_BLK // d, stride=d), :] = o_r[r, :, cs]
            l_h = [v[:, h:h + 1] for v in lse]
            mx = jnp.maximum(jnp.maximum(l_h[0], l_h[1]), l_h[2])
            e = [jnp.exp(v - mx) for v in l_h]
            tot = e[0] + e[1] + e[2]
            inv = 1.0 / tot
            at_ref[:, cs] = ((e[0] * inv) * o0[0, :, cs] + (e[1] * inv) * sc[0] + (e[2] * inv) * sc[1]).astype(BF16)
            lt_rows = jnp.where(lane == h, jnp.broadcast_to(mx + jnp.log(tot), (ROW_BLK, HEAD_DIM)), lt_rows)
        lt_ref[...] = lt_rows

    go, gl = _group_specs(GROUP_W), _group_specs(HEAD_DIM)
    return pl.pallas_call(
        body, name="attn_merge",
        out_shape=[jax.ShapeDtypeStruct((seq, GROUP_W), BF16), jax.ShapeDtypeStruct((seq, HEAD_DIM), F32)],
        grid=(seq // ROW_BLK,), in_specs=[go[0], gl[0], go[1], gl[1], go[2], gl[2]],
        out_specs=[pl.BlockSpec((ROW_BLK, GROUP_W), lambda i: (i, 0)), pl.BlockSpec((ROW_BLK, HEAD_DIM), lambda i: (i, 0))],
        scratch_shapes=[pltpu.VMEM((2, ROW_BLK, HEAD_DIM), F32), pltpu.VMEM((2, ROW_BLK, HEAD_DIM), F32)],
        compiler_params=_params(dimension_semantics=("arbitrary",)),
    )(os_[0], ls_[0], os_[1], ls_[1], os_[2], ls_[2])


def _attn_bwd_prep(dattn, attn, lt):
    seq = dattn.shape[0]

    def body(da_ref, at_ref, lt_ref, cl0, d1, cl1, d2, cl2, sc, csc):
        lane = lax.broadcasted_iota(jnp.int32, (ROW_BLK, HEAD_DIM), 1)
        cl = pltpu.roll(lt_ref[...], HEADS_PER_GROUP, 1)
        for h in range(HEADS_PER_GROUP):
            cs = slice(h * HEAD_DIM, (h + 1) * HEAD_DIM)
            da = da_ref[:, cs].astype(F32)
            cc = jnp.sum(da * at_ref[:, cs].astype(F32), axis=-1, keepdims=True)
            cl = jnp.where(lane == h, jnp.broadcast_to(cc, (ROW_BLK, HEAD_DIM)), cl)
            sc[...] = da
            for g, d_ref in ((1, d1), (2, d2)):
                d = DILATIONS[g]
                for r in range(d):
                    d_ref[r, :, cs] = sc[pl.ds(r, ROW_BLK // d, stride=d), :].astype(BF16)
        cl0[0] = cl
        csc[...] = cl
        for g, c_ref in ((1, cl1), (2, cl2)):
            d = DILATIONS[g]
            for r in range(d):
                c_ref[r] = csc[pl.ds(r, ROW_BLK // d, stride=d), :]

    go, gl = _group_specs(GROUP_W), _group_specs(HEAD_DIM)
    row = lambda width: pl.BlockSpec((ROW_BLK, width), lambda i: (i, 0))
    shape = lambda g, width, dt: jax.ShapeDtypeStruct((DILATIONS[g], seq // DILATIONS[g], width), dt)
    cl0, d1, cl1, d2, cl2 = pl.pallas_call(
        body, name="attn_bwd_prep",
        out_shape=[shape(0, HEAD_DIM, F32), shape(1, GROUP_W, BF16), shape(1, HEAD_DIM, F32), shape(2, GROUP_W, BF16),
                   shape(2, HEAD_DIM, F32)],
        grid=(seq // ROW_BLK,), in_specs=[row(GROUP_W), row(GROUP_W), row(HEAD_DIM)],
        out_specs=[gl[0], go[1], gl[1], go[2], gl[2]],
        scratch_shapes=[pltpu.VMEM((ROW_BLK, HEAD_DIM), F32), pltpu.VMEM((ROW_BLK, HEAD_DIM), F32)],
        compiler_params=_params(dimension_semantics=("arbitrary",)),
    )(dattn, attn, lt)
    return [(dattn[None], cl0), (d1, cl1), (d2, cl2)]


def _attn_bwd(name, a_g, da_g, cl_g, comm=_NO_COMM):
    dil, m_len, _ = a_g.shape
    qb = min(QB, m_len // BAND)
    rows = qb * BAND
    steps = m_len // rows
    scale = HEAD_DIM ** -0.5

    tiles = [(sb, h) for sb in range(qb) for h in range(HEADS_PER_GROUP)]

    def body(q_ref, kc_ref, vc_ref, kp_ref, vp_ref, da_ref, cl_ref, d_ref, dk_acc, dv_acc, car_k, car_v,
             k_all, v_all, s_scr, dp_scr, p_scr, ds_scr):
        tg = pl.program_id(1)
        t = steps - 1 - tg

        @pl.when(tg == 0)
        def _():
            car_k[...] = jnp.zeros_like(car_k)
            car_v[...] = jnp.zeros_like(car_v)

        k_all[0:BAND, :] = kp_ref[...]
        k_all[BAND:, :] = kc_ref[...]
        v_all[0:BAND, :] = vp_ref[...]
        v_all[BAND:, :] = vc_ref[...]
        zero = jnp.zeros((rows, GROUP_W), F32)
        dk_acc[0:rows, :] = zero
        dv_acc[0:rows, :] = zero
        dk_acc[rows:rows + BAND, :] = car_k[...]
        dv_acc[rows:rows + BAND, :] = car_v[...]
        band, band_first = _band_masks_2(t)
        for idx, (sb, h) in enumerate(tiles):
            cs = slice(h * HEAD_DIM, (h + 1) * HEAD_DIM)
            rs, ks = slice(sb * BAND, (sb + 1) * BAND), slice(sb * BAND, (sb + 2) * BAND)
            s_scr[idx] = _dot(q_ref[rs, cs], k_all[ks, cs], "nt")
            dp_scr[idx] = _dot(da_ref[rs, cs], v_all[ks, cs], "nt")
        for idx, (sb, h) in enumerate(tiles):
            cs = slice(h * HEAD_DIM, (h + 1) * HEAD_DIM)
            rs = slice(sb * BAND, (sb + 1) * BAND)
            cc = jnp.broadcast_to(cl_ref[rs, h:h + 1], (BAND, 2 * BAND))
            ltv = jnp.broadcast_to(cl_ref[rs, HEADS_PER_GROUP + h:HEADS_PER_GROUP + h + 1], (BAND, 2 * BAND))
            p = jnp.exp(jnp.where(band_first if sb == 0 else band, s_scr[idx] * scale - ltv, NEG))
            p_scr[idx] = p.astype(BF16)
            ds_scr[idx] = (p * (dp_scr[idx] - cc) * scale).astype(BF16)
        for idx, (sb, h) in enumerate(tiles):
            cs = slice(h * HEAD_DIM, (h + 1) * HEAD_DIM)
            rs, ks = slice(sb * BAND, (sb + 1) * BAND), slice(sb * BAND, (sb + 2) * BAND)
            d_ref[rs, cs] = _dot(ds_scr[idx], k_all[ks, cs], "nn").astype(BF16)
            dk_acc[ks, cs] += _dot(ds_scr[idx], q_ref[rs, cs], "tn")
            dv_acc[ks, cs] += _dot(p_scr[idx], da_ref[rs, cs], "tn")
        d_ref[:, GROUP_W:2 * GROUP_W] = dk_acc[BAND:rows + BAND, :].astype(BF16)
        d_ref[:, 2 * GROUP_W:3 * GROUP_W] = dv_acc[BAND:rows + BAND, :].astype(BF16)
        car_k[...] = dk_acc[0:BAND, :]
        car_v[...] = dv_acc[0:BAND, :]

    def rev(tg):
        return steps - 1 - tg

    def prev(tg):
        return jnp.maximum(qb * rev(tg) - 1, 0)

    cur = lambda c: pl.BlockSpec((None, rows, GROUP_W), lambda r, tg, c=c: (r, rev(tg), c))
    prv = lambda c: pl.BlockSpec((None, BAND, GROUP_W), lambda r, tg, c=c: (r, prev(tg), c))
    return _pcall(
        body, name, (dil, steps), [a_g, a_g, a_g, a_g, a_g, da_g, cl_g],
        [cur(0), cur(1), cur(2), prv(1), prv(2), cur(0), pl.BlockSpec((None, rows, HEAD_DIM), lambda r, tg: (r, rev(tg), 0))],
        [jax.ShapeDtypeStruct((dil, m_len, ATTN_W), BF16)], [pl.BlockSpec((None, rows, ATTN_W), lambda r, tg: (r, rev(tg), 0))],
        [pltpu.VMEM((rows + BAND, GROUP_W), F32), pltpu.VMEM((rows + BAND, GROUP_W), F32),
         pltpu.VMEM((BAND, GROUP_W), F32), pltpu.VMEM((BAND, GROUP_W), F32),
         pltpu.VMEM((rows + BAND, GROUP_W), BF16), pltpu.VMEM((rows + BAND, GROUP_W), BF16),
         pltpu.VMEM((len(tiles), BAND, 2 * BAND), F32), pltpu.VMEM((len(tiles), BAND, 2 * BAND), F32),
         pltpu.VMEM((len(tiles), BAND, 2 * BAND), BF16), pltpu.VMEM((len(tiles), BAND, 2 * BAND), BF16)], comm=comm)


def _dqkv_post(d_gs, tabs, dz):
    seq = dz.shape[0]

    def body(g0, g1, g2, c_ref, s1_ref, s2_ref, dz_any, o_ref, sc):
        del dz_any
        ins = (g0, g1, g2)
        c, s1, s2 = c_ref[...], s1_ref[...], s2_ref[...]
        for part in range(3):
            for hh in range(N_GROUPS * HEADS_PER_GROUP):
                g, hl = divmod(hh, HEADS_PER_GROUP)
                icol = part * GROUP_W + hl * HEAD_DIM
                ocol = part * ATTN_W + hh * HEAD_DIM
                d = DILATIONS[g]
                if d == 1:
                    x = ins[g][0, :, icol:icol + HEAD_DIM].astype(F32)
                else:
                    for r in range(d):
                        sc[pl.ds(r, ROW_BLK // d, stride=d), :] = ins[g][r, :, icol:icol + HEAD_DIM].astype(F32)
                    x = sc[...]
                if part < 2:
                    x = x * c + pltpu.roll(x * s1, HEAD_DIM - ROT_DIM // 2, 1) + pltpu.roll(x * s2, ROT_DIM // 2, 1)
                o_ref[:, ocol:ocol + HEAD_DIM] = x.astype(BF16)

    tab_spec = pl.BlockSpec((ROW_BLK, HEAD_DIM), lambda i: (i, 0))
    return pl.pallas_call(
        body, name="dqkv_post", out_shape=jax.ShapeDtypeStruct(dz.shape, BF16), grid=(seq // ROW_BLK,),
        in_specs=_group_specs(ATTN_W) + [tab_spec, tab_spec, tab_spec, pl.BlockSpec(memory_space=pl.ANY)],
        out_specs=pl.BlockSpec((ROW_BLK, QKV_W), lambda i: (i, 0)),
        scratch_shapes=[pltpu.VMEM((ROW_BLK, HEAD_DIM), F32)], input_output_aliases={6: 0},
        compiler_params=_params(dimension_semantics=("arbitrary",)),
    )(*d_gs, *tabs, dz)


def _glu(zg):
    a = zg[:, :CONV_CH].astype(F32)
    s = _sigmoid(zg[:, CONV_CH:].astype(F32))
    return a, s, a * s


def _shifted_copies(xs):
    n = xs.shape[1] - SUBLANES
    for b in range(1, SUBLANES):
        xs[b, 0:n, :] = xs[0, pl.ds(b, n), :]


def _shifted(xs, offset, r0, cs):
    a, b = divmod(offset, SUBLANES)
    return xs[b, pl.ds(SUBLANES * a + r0, CONV_ROWS), cs]


def _conv_fwd(z, cw, cb, lg, lb, comm=_NO_COMM):
    seq = z.shape[0]
    halo_per_blk = ROW_BLK // CONV_HALO

    def body(zg_ref, zh_ref, cw_ref, cb_ref, lg_ref, lb_ref, c2_ref, c4_ref, xs):
        i = pl.program_id(0)
        _, _, c1 = _glu(zg_ref[...])
        _, _, c1h = _glu(zh_ref[...])
        xs[0, 0:CONV_HALO, :] = jnp.where(i > 0, c1h, 0.0)
        xs[0, CONV_HALO:, :] = c1
        _shifted_copies(xs)
        for s in range(CONV_CH // HEAD_DIM):
            cs = slice(s * HEAD_DIM, (s + 1) * HEAD_DIM)
            taps = [cw_ref[j:j + 1, cs] for j in range(CONV_K)]
            bias = cb_ref[:, cs]

            def chunk(rc, carry, cs=cs, taps=taps, bias=bias):
                r0 = pl.multiple_of(rc * CONV_ROWS, CONV_ROWS)
                acc = [jnp.zeros((CONV_ROWS, HEAD_DIM), F32)] * 2
                for j in range(CONV_K):
                    acc[j % 2] = acc[j % 2] + taps[j] * _shifted(xs, CONV_HALO - (CONV_K - 1) + j, r0, cs)
                c2_ref[pl.ds(r0, CONV_ROWS), cs] = acc[0] + acc[1] + bias
                return carry

            lax.fori_loop(0, ROW_BLK // CONV_ROWS, chunk, 0)
        c2 = c2_ref[...]
        mu = jnp.mean(c2, axis=-1, keepdims=True)
        xc = c2 - mu
        rstd = lax.rsqrt(jnp.mean(xc * xc, axis=-1, keepdims=True) + EPS)
        c3 = xc * rstd * lg_ref[...] + lb_ref[...]
        c4_ref[...] = (c3 * _sigmoid(c3)).astype(BF16)

    vec = pl.BlockSpec((1, CONV_CH), lambda i: (0, 0))
    return _pcall(
        body, "conv_fwd", (seq // ROW_BLK,), [z, z, cw, cb, lg, lb],
        [pl.BlockSpec((ROW_BLK, 2 * CONV_CH), lambda i: (i, GLU_COL_BLK)),
         pl.BlockSpec((CONV_HALO, 2 * CONV_CH), lambda i: (jnp.maximum(i * halo_per_blk - 1, 0), GLU_COL_BLK)),
         pl.BlockSpec((CONV_HALO, CONV_CH), lambda i: (0, 0)), vec, vec, vec],
        [jax.ShapeDtypeStruct((seq, CONV_CH), F32), jax.ShapeDtypeStruct((seq, CONV_CH), BF16)],
        [pl.BlockSpec((ROW_BLK, CONV_CH), lambda i: (i, 0)), pl.BlockSpec((ROW_BLK, CONV_CH), lambda i: (i, 0))],
        [pltpu.VMEM((SUBLANES, ROW_BLK + CONV_HALO, CONV_CH), F32)], comm=comm)


def _conv_bwd(dc2, z, cw, dz, comm=_NO_COMM):
    seq = z.shape[0]
    halo_per_blk = ROW_BLK // CONV_HALO
    n_blk = seq // ROW_BLK
    last_halo = seq // CONV_HALO - 1

    def body(dc_ref, dn_ref, zg_ref, zh_ref, cw_ref, dz_any, o_ref, dcw_ref, xs, ys, dc1_ref, dcw_acc):
        del dz_any
        i = pl.program_id(0)
        a, s, c1 = _glu(zg_ref[...])
        _, _, c1h = _glu(zh_ref[...])
        xs[0, 0:CONV_HALO, :] = jnp.where(i > 0, c1h, 0.0)
        xs[0, CONV_HALO:, :] = c1
        ys[0, 0:ROW_BLK, :] = dc_ref[...]
        ys[0, ROW_BLK:, :] = jnp.where(i < n_blk - 1, dn_ref[...], 0.0)
        _shifted_copies(xs)
        _shifted_copies(ys)

        @pl.when(i == 0)
        def _():
            dcw_acc[...] = jnp.zeros_like(dcw_acc)

        for sl in range(CONV_CH // HEAD_DIM):
            cs = slice(sl * HEAD_DIM, (sl + 1) * HEAD_DIM)
            taps = [cw_ref[j:j + 1, cs] for j in range(CONV_K)]

            def chunk(rc, carry, cs=cs, taps=taps):
                r0 = pl.multiple_of(rc * CONV_ROWS, CONV_ROWS)
                dc = ys[0, pl.ds(r0, CONV_ROWS), cs]
                acc = [jnp.zeros((CONV_ROWS, HEAD_DIM), F32)] * 2
                for j in range(CONV_K):
                    prod = dc * _shifted(xs, CONV_HALO - (CONV_K - 1) + j, r0, cs)
                    dcw_acc[j, :, cs] += jnp.sum(prod.reshape(CONV_ROWS // SUBLANES, SUBLANES, HEAD_DIM), axis=0)
                    acc[j % 2] = acc[j % 2] + taps[j] * _shifted(ys, CONV_K - 1 - j, r0, cs)
                dc1_ref[pl.ds(r0, CONV_ROWS), cs] = acc[0] + acc[1]
                return carry

            lax.fori_loop(0, ROW_BLK // CONV_ROWS, chunk, 0)
        dc1 = dc1_ref[...]
        o_ref[:, :CONV_CH] = (dc1 * s).astype(BF16)
        o_ref[:, CONV_CH:] = (dc1 * a * s * (1.0 - s)).astype(BF16)

        @pl.when(i == n_blk - 1)
        def _():
            dcw_ref[...] = jnp.sum(dcw_acc[...], axis=1)

    return _pcall(
        body, "conv_bwd", (n_blk,), [dc2, dc2, z, z, cw, dz],
        [pl.BlockSpec((ROW_BLK, CONV_CH), lambda i: (i, 0)),
         pl.BlockSpec((CONV_HALO, CONV_CH), lambda i: (jnp.minimum((i + 1) * halo_per_blk, last_halo), 0)),
         pl.BlockSpec((ROW_BLK, 2 * CONV_CH), lambda i: (i, GLU_COL_BLK)),
         pl.BlockSpec((CONV_HALO, 2 * CONV_CH), lambda i: (jnp.maximum(i * halo_per_blk - 1, 0), GLU_COL_BLK)),
         pl.BlockSpec((CONV_HALO, CONV_CH), lambda i: (0, 0)),
         pl.BlockSpec(memory_space=pl.ANY)],
        [jax.ShapeDtypeStruct(dz.shape, BF16), jax.ShapeDtypeStruct((CONV_HALO, CONV_CH), F32)],
        [pl.BlockSpec((ROW_BLK, 2 * CONV_CH), lambda i: (i, GLU_COL_BLK)), pl.BlockSpec((CONV_HALO, CONV_CH), lambda i: (0, 0))],
        [pltpu.VMEM((SUBLANES, ROW_BLK + CONV_HALO, CONV_CH), F32), pltpu.VMEM((SUBLANES, ROW_BLK + CONV_HALO, CONV_CH), F32),
         pltpu.VMEM((ROW_BLK, CONV_CH), F32), pltpu.VMEM((CONV_HALO, SUBLANES, CONV_CH), F32)],
        aliases={5: 0}, comm=comm)


def _epi_mix(ya, c4, wcp, gates, bg):
    yc = _dot(c4, wcp, "nn")
    gv = _sigmoid(gates.astype(F32) + bg)
    merged = gv[:, :D_MODEL] * ya + gv[:, D_MODEL:] * yc
    return merged, ya, yc


def _epi_residual_rms(acc, xres, g):
    x = xres + acc
    return x, _rms_fwd_vals(x, g)


def _cross_scores(cq, ck):
    out = []
    for h in range(CROSS_HEADS):
        cs = slice(h * CROSS_HD, (h + 1) * CROSS_HD)
        s = _dot(cq[:, cs], ck[:, cs], "nt") * (CROSS_HD ** -0.5)
        e = jnp.exp(s - jnp.max(s, axis=-1, keepdims=True))
        out.append((cs, e, jnp.sum(e, axis=-1, keepdims=True)))
    return out


def _epi_cross_fwd(acc, ck, cv):
    cq = acc.astype(BF16)
    co = [_dot(e, cv[:, cs], "nn") / den for cs, e, den in _cross_scores(cq, ck)]
    return cq, jnp.concatenate(co, axis=1)


def _epi_cross_bwd(dco, cq, ck, cv):
    dco = dco.astype(BF16)
    dcq, dck, dcv = [], [], []
    for cs, e, den in _cross_scores(cq, ck):
        p = e / den
        dp = _dot(dco[:, cs], cv[:, cs], "nt")
        ds = (p * (dp - jnp.sum(dp * p, axis=-1, keepdims=True)) * (CROSS_HD ** -0.5)).astype(BF16)
        dcq.append(_dot(ds, ck[:, cs], "nn"))
        dck.append(_dot(ds, cq[:, cs], "tn"))
        dcv.append(_dot(p, dco[:, cs], "tn"))
    return jnp.concatenate(dcq, axis=1), jnp.concatenate(dck, axis=1), jnp.concatenate(dcv, axis=1)


def _epi_mlp_up(acc):
    return acc, jnp.square(jnp.maximum(acc, 0.0))


def _epi_final(acc, x2, tgt, g):
    x3 = x2 + acc
    err = _rms_fwd_vals(x3, g) - tgt
    loss = (0.5 / D_MODEL) * jnp.sum(err * err)
    dx3, dg = _rms_bwd_vals(x3, g, err * (1.0 / D_MODEL))
    return dx3, jnp.full((1, HEAD_DIM), loss, F32), dg


def _epi_mlp_down_bwd(dh, hpre):
    return (dh * 2.0 * jnp.maximum(hpre.astype(F32), 0.0),)


def _epi_rms_bwd(du, x, g, dres):
    dx, dg = _rms_bwd_vals(x, g, du)
    return dres.astype(F32) + dx, dg


def _epi_rms_bwd_g(du, x, g):
    return (_rms_bwd_vals(x, g, du)[1],)


def _epi_mix_bwd(dm, ya, yc, gates, bg):
    gv = _sigmoid(gates.astype(F32) + bg)
    ga, gb = gv[:, :D_MODEL], gv[:, D_MODEL:]
    ya, yc = ya.astype(F32), yc.astype(F32)
    dgate = jnp.concatenate([dm * ya * ga * (1.0 - ga), dm * yc * gb * (1.0 - gb)], axis=1)
    return dm * ga, dm * gb, dgate, jnp.sum(dgate, axis=0, keepdims=True)


def _epi_ln_bwd(dc4, c2, lg, lb):
    mu = jnp.mean(c2, axis=-1, keepdims=True)
    xc = c2 - mu
    rstd = lax.rsqrt(jnp.mean(xc * xc, axis=-1, keepdims=True) + EPS)
    xh = xc * rstd
    c3 = xh * lg + lb
    sg = _sigmoid(c3)
    dc3 = dc4 * sg * (1.0 + c3 * (1.0 - sg))
    dxh = dc3 * lg
    dc2 = rstd * (dxh - jnp.mean(dxh, axis=-1, keepdims=True) - xh * jnp.mean(dxh * xh, axis=-1, keepdims=True))
    return (dc2, jnp.sum(dc3 * xh, axis=0, keepdims=True), jnp.sum(dc3, axis=0, keepdims=True),
            jnp.sum(dc2, axis=0, keepdims=True))


def _sds(shape, dtype):
    return jax.ShapeDtypeStruct(shape, dtype)


class _Lazy:
    def __init__(self, fn):
        self.fn = fn

    def __getitem__(self, key):
        return self.fn(key)


def _local_step(x, mem, tgt, sm, plan):
    w = _Lazy(plan.w)
    dw = {}

    def carry(name, n_own, fn, *args, **kw):
        c = plan.comm(name, dw)
        res = fn(*args, comm=c, **kw)
        plan.done(name, res[n_own:])
        return res[:n_own]

    def mm(name, *args, **kw):
        return carry(name, len(args[6]), _mm, name, *args, **kw)

    seq = x.shape[0]
    nr = seq // ROW_BLK
    big = min(1024, seq)
    nb = seq // big
    row = lambda n: ((ROW_BLK, n), lambda i, j, k: (i, 0))
    vec = lambda n: ((1, n), lambda i, j, k: (0, 0))
    full = lambda r, c: ((r, c), lambda i, j, k: (0, 0))
    gates_blk = ((ROW_BLK, 2 * D_MODEL), lambda i, j, k: (i, GATE_COL_BLK))
    tabs = _rope_tables(seq)

    u = carry("rms_mix", 1, _rms_fwd, "rms_mix", x, sm["g_mix"], ROW_BLK)[0]
    whole3 = lambda a: (a.shape, lambda i, j, k: (0, 0, 0))
    z = plan.project_in(u)
    a_gs = carry("qkv_prep", 3, _qkv_prep, z, tabs)
    os_, ls_ = [], []
    for g in range(N_GROUPS):
        name = "attn_fwd_%d" % g
        o_g, l_g = carry(name, 2, _attn_fwd, name, a_gs[g])
        os_.append(o_g)
        ls_.append(l_g)
    attn, lt = _attn_merge(os_, ls_, seq)
    c2, c4 = carry("conv_fwd", 2, _conv_fwd, z, w["taps"], sm["conv_b"], sm["conv_ln_g"], sm["conv_ln_b"])
    merged, ya, yc = mm(
        "mix", attn, w["w_attn_proj"], "nn", (nr, 1, 1), row(GROUP_W), full(GROUP_W, D_MODEL),
        [(_sds((seq, D_MODEL), BF16), *row(D_MODEL))] * 3,
        extras=[(c4, *row(CONV_CH)), (w["w_conv_proj"], *full(CONV_CH, D_MODEL)), (z, *gates_blk), (sm["b_gate"], *vec(2 * D_MODEL))],
        epi=_epi_mix)
    x1, uq = mm("out_proj", merged, w["w_out"], "nn", (nr, 1, 1), row(D_MODEL), full(D_MODEL, D_MODEL),
                 [(_sds((seq, D_MODEL), F32), *row(D_MODEL)), (_sds((seq, D_MODEL), BF16), *row(D_MODEL))],
                 extras=[(x, *row(D_MODEL)), (sm["g_cross"], *vec(D_MODEL))], epi=_epi_residual_rms)

    mn = _rms_fwd("rms_mem", mem, sm["g_mem"], N_MEM)[0]
    ckv = mm("ckv_proj", mn, w["w_ckv"], "nn", (1, N_DEV, 1), full(N_MEM, D_MODEL),
              ((None, D_MODEL, 2 * D_MODEL // N_DEV), lambda i, j, k: (j, 0, 0)),
              [(_sds((N_MEM, 2 * D_MODEL), BF16), (N_MEM, 2 * D_MODEL // N_DEV), lambda i, j, k: (0, j))])[0]
    ck, cv = ckv[:, :D_MODEL], ckv[:, D_MODEL:]
    kv_blk = full(N_MEM, D_MODEL)
    cq, co = mm("cq_proj_cross", uq, w["w_cq"], "nn", (nr, 1, 1), row(D_MODEL), full(D_MODEL, D_MODEL),
                 [(_sds((seq, D_MODEL), BF16), *row(D_MODEL))] * 2,
                 extras=[(ck, *kv_blk), (cv, *kv_blk)], epi=_epi_cross_fwd)
    x2, um = mm("co_proj", co, w["w_co"], "nn", (nr, 1, 1), row(D_MODEL), full(D_MODEL, D_MODEL),
                 [(_sds((seq, D_MODEL), F32), *row(D_MODEL)), (_sds((seq, D_MODEL), BF16), *row(D_MODEL))],
                 extras=[(x1, *row(D_MODEL)), (sm["g_mlp"], *vec(D_MODEL))], epi=_epi_residual_rms)

    ff_blk = D_FF // N_DEV
    row_f32 = (_sds((seq, D_MODEL), F32), *row(D_MODEL))
    row_bf16 = (_sds((seq, D_MODEL), BF16), *row(D_MODEL))
    col_sum = (_sds((1, D_MODEL), F32), *vec(D_MODEL))
    hpre, h = mm("mlp_up", um, w["w_up"], "nn", (nr, 1, 1), row(D_MODEL), whole3(w["w_up"]),
                 [(_sds((seq, D_FF), BF16), *row(D_FF))] * 2, epi=_epi_mlp_up, split=("cols", N_DEV), b_resident=True)
    kt = D_FF // D_MODEL
    dx3, loss, dg_final = mm(
        "mlp_down_loss", h, w["w_down"], "nn", (nr, 1, 1), row(D_FF), full(D_FF, D_MODEL),
        [row_bf16, (_sds((1, HEAD_DIM), F32), *vec(HEAD_DIM)), col_sum],
        extras=[(x2, *row(D_MODEL)), (tgt, *row(D_MODEL)), (sm["g_final"], *vec(D_MODEL))], epi=_epi_final, acc_outs=(1, 2),
        b_resident=True)

    dhpre = mm("mlp_down_bwd", dx3, w["w_down"], "nt", (nr, 1, 1), row(D_MODEL), full(D_FF, D_MODEL),
               [(_sds((seq, D_FF), BF16), *row(D_FF))], extras=[(hpre, *row(D_FF))], epi=_epi_mlp_down_bwd,
               split=("cols", kt), b_resident=True)[0]
    big2 = min(2 * big, seq)
    nb2 = seq // big2
    dw["w_down"] = mm("dw_down", h, dx3, "tn", (kt, 1, nb2), ((big2, D_MODEL), lambda i, j, k: (k, i)),
                      ((big2, D_MODEL), lambda i, j, k: (k, 0)),
                      [(_sds((D_FF, D_MODEL), BF16), (D_MODEL, D_MODEL), lambda i, j, k: (i, 0))])[0]
    dx2, dg_mlp = mm("mlp_up_bwd", dhpre, w["w_up"], "nt", (nr, 1, 1), row(D_FF), whole3(w["w_up"]),
                     [row_bf16, col_sum],
                     extras=[(x2, *row(D_MODEL)), (sm["g_mlp"], *vec(D_MODEL)), (dx3, *row(D_MODEL))],
                     epi=_epi_rms_bwd, acc_outs=(1,), split=("sum", N_DEV), b_resident=True)
    dw["w_up"] = mm("dw_up", um, dhpre, "tn", (1, N_DEV, nb2), ((big2, D_MODEL), lambda i, j, k: (k, 0)),
                    ((big2, ff_blk), lambda i, j, k: (k, j)),
                    [(_sds((N_DEV, D_MODEL, ff_blk), BF16), (None, D_MODEL, ff_blk), lambda i, j, k: (j, 0, 0))])[0]

    acc_kv = (_sds((N_MEM, D_MODEL), F32), *kv_blk)
    dcq, dck, dcv = mm("co_proj_bwd_cross", dx2, w["w_co"], "nt", (nr, 1, 1), row(D_MODEL), full(D_MODEL, D_MODEL),
                       [row_bf16, acc_kv, acc_kv],
                       extras=[(cq, *row(D_MODEL)), (ck, *kv_blk), (cv, *kv_blk)], epi=_epi_cross_bwd, acc_outs=(1, 2))

    def dw_square(name, act, grad):
        return mm(name, act, grad, "tn", (1, 1, nb2), ((big2, D_MODEL), lambda i, j, k: (k, 0)),
                  ((big2, D_MODEL), lambda i, j, k: (k, 0)), [(_sds((D_MODEL, D_MODEL), BF16), *full(D_MODEL, D_MODEL))])[0]

    dw["w_co"] = dw_square("dw_co", co, dx2)
    dx1, dg_cross = mm("cq_proj_bwd", dcq, w["w_cq"], "nt", (nr, 1, 1), row(D_MODEL), full(D_MODEL, D_MODEL),
                       [row_bf16, col_sum],
                       extras=[(x1, *row(D_MODEL)), (sm["g_cross"], *vec(D_MODEL)), (dx2, *row(D_MODEL))],
                       epi=_epi_rms_bwd, acc_outs=(1,))
    dw["w_cq"] = dw_square("dw_cq", uq, dcq)
    dckv = jnp.concatenate([dck, dcv], axis=1)
    kv_chunk = 2 * D_MODEL // N_DEV
    dw["w_ckv"] = mm("dw_ckv", mn, dckv, "tn", (1, N_DEV, 1), full(N_MEM, D_MODEL), ((N_MEM, kv_chunk), lambda i, j, k: (0, j)),
                      [(_sds((N_DEV, D_MODEL, kv_chunk), BF16), (None, D_MODEL, kv_chunk), lambda i, j, k: (j, 0, 0))])[0]
    dg_mem = mm("ckv_proj_bwd", dckv, w["w_ckv"], "nt", (1, 1, N_DEV), ((N_MEM, kv_chunk), lambda i, j, k: (0, k)),
                 ((None, D_MODEL, kv_chunk), lambda i, j, k: (k, 0, 0)), [(_sds((1, D_MODEL), F32), *vec(D_MODEL))],
                 extras=[(mem, *full(N_MEM, D_MODEL)), (sm["g_mem"], *vec(D_MODEL))], epi=_epi_rms_bwd_g, acc_outs=(0,))[0]

    dya, dyc, dz, db_gate = mm(
        "out_proj_bwd_mix", dx1, w["w_out"], "nt", (nr, 1, 1), row(D_MODEL), full(D_MODEL, D_MODEL),
        [(_sds((seq, D_MODEL), BF16), *row(D_MODEL)), (_sds((seq, D_MODEL), BF16), *row(D_MODEL)),
         (_sds((seq, IN_W), BF16), *gates_blk), (_sds((1, 2 * D_MODEL), F32), *vec(2 * D_MODEL))],
        extras=[(ya, *row(D_MODEL)), (yc, *row(D_MODEL)), (z, *gates_blk), (sm["b_gate"], *vec(2 * D_MODEL))],
        epi=_epi_mix_bwd, acc_outs=(3,))
    dw["w_out"] = dw_square("dw_out", merged, dx1)
    dattn = mm("attn_proj_bwd", dya, w["w_attn_proj"], "nt", (nr, 1, 1), row(D_MODEL), full(GROUP_W, D_MODEL),
                [(_sds((seq, GROUP_W), BF16), *row(GROUP_W))])[0]
    pc = D_MODEL // N_DEV
    dw["w_attn_proj"] = mm("dw_attn_proj", attn, dya, "tn", (1, 1, nb2), ((big2, GROUP_W), lambda i, j, k: (k, 0)),
                           ((big2, D_MODEL), lambda i, j, k: (k, 0)),
                           [(_sds((N_DEV, GROUP_W, pc), BF16), (N_DEV, GROUP_W, pc), lambda i, j, k: (0, 0, 0))],
                           out_chunks=N_DEV)[0]
    cvec = (_sds((1, CONV_CH), F32), *vec(CONV_CH))
    dc2, dg_ln_g, dg_ln_b, dg_conv_b = mm(
        "conv_proj_bwd_ln", dyc, w["w_conv_proj"], "nt", (nr, 1, 1), row(D_MODEL), full(CONV_CH, D_MODEL),
        [(_sds((seq, CONV_CH), F32), *row(CONV_CH)), cvec, cvec, cvec],
        extras=[(c2, *row(CONV_CH)), (sm["conv_ln_g"], *vec(CONV_CH)), (sm["conv_ln_b"], *vec(CONV_CH))],
        epi=_epi_ln_bwd, acc_outs=(1, 2, 3))
    dw["w_conv_proj"] = mm("dw_conv_proj", c4, dyc, "tn", (1, 1, nb2), ((big2, CONV_CH), lambda i, j, k: (k, 0)),
                           ((big2, D_MODEL), lambda i, j, k: (k, 0)),
                           [(_sds((N_DEV, CONV_CH, pc), BF16), (N_DEV, CONV_CH, pc), lambda i, j, k: (0, 0, 0))],
                           out_chunks=N_DEV)[0]
    dz, dg_conv_w = carry("conv_bwd", 2, _conv_bwd, dc2, z, w["taps"], dz)
    preps = _attn_bwd_prep(dattn, attn, lt)
    d_gs = []
    for g in range(N_GROUPS):
        name = "attn_bwd_%d" % g
        d_gs.append(carry(name, 1, _attn_bwd, name, a_gs[g], *preps[g])[0])
    dz = _dqkv_post(d_gs, tabs, dz)
    dw["w_in"] = mm("dw_in", u, dz, "tn", (1, N_DEV, nb2), ((big2, D_MODEL), lambda i, j, k: (k, 0)),
                    ((big2, D_MODEL), lambda i, j, k: (k, j)),
                    [(_sds((N_DEV, D_MODEL, D_MODEL), BF16), (None, D_MODEL, D_MODEL), lambda i, j, k: (j, 0, 0))])[0]
    token = plan.start_w_in(dw["w_in"])
    grad_x, dg_mix = mm("in_proj_bwd", dz, w["w_in"], "nt", (nr, 1, 1), row(IN_W), whole3(w["w_in"]), [row_f32, col_sum],
                        extras=[(x, *row(D_MODEL)), (sm["g_mix"] + token, *vec(D_MODEL)), (dx1, *row(D_MODEL))],
                        epi=_epi_rms_bwd, acc_outs=(1,), split=("sum", N_DEV), b_resident=True)
    small = dict(g_mix=dg_mix, b_gate=db_gate, conv_b=dg_conv_b, conv_ln_g=dg_ln_g, conv_ln_b=dg_ln_b, g_cross=dg_cross,
                 g_mem=dg_mem, g_mlp=dg_mlp, g_final=dg_final, loss=loss, conv_w=dg_conv_w)
    return grad_x, dw, small


SHARD_SHAPE = dict(w_in=(1024, 1024), w_attn_proj=(512, 128), w_conv_proj=(768, 128), w_out=(128, 1024), w_cq=(128, 1024),
                   w_ckv=(1024, 256), w_co=(128, 1024), w_up=(1024, 512), w_down=(512, 1024))
FWD_CARRY = {"in_proj":("w_attn_proj", "w_conv_proj", "w_out", "w_cq", "w_ckv", "w_co", "taps"),
             "qkv_prep": ("w_up",), "conv_fwd": ("w_down",)}
BWD_CARRY = {"dw_up": ("w_down",), "out_proj_bwd_mix": ("w_co", "w_cq"), "conv_bwd": ("w_up", "w_ckv"),
             "attn_bwd_0": ("w_out",), "attn_bwd_1": ("w_attn_proj", "w_conv_proj")}


def _cols_to_2d(a):
    return a.transpose(1, 0, 2).reshape(a.shape[1], -1)


def _in_proj_gather(u, w_shard, comm):
    seq = u.shape[0]
    tm = min(1024, seq)
    x, y, c = lax.axis_index("x"), lax.axis_index("y"), lax.axis_index("c")
    ident = lambda px, py, pc: 4 * px + 2 * py + pc
    far = [(1 - x, y), (x, 1 - y), (1 - x, 1 - y)]
    order = jnp.stack([ident(x, y, c), ident(x, y, 1 - c), ident(*far[0], c), ident(*far[1], c), ident(*far[0], 1 - c),
                       ident(*far[1], 1 - c), ident(*far[2], c), ident(*far[2], 1 - c)]).astype(jnp.int32)
    forward_at = {2: 0, 3: 1, 6: 2}
    n_far = len(far)

    def body(order_ref, u_ref, wsh_ref, *rest):
        c_in, z_ref, wg_ref = rest[:comm.n], rest[comm.n], rest[comm.n + 1]
        c_out = rest[comm.n + 2:2 * comm.n + 2]
        wbuf, load_sem, local_sem, recv_sems, ici_send, d2d_send = rest[2 * comm.n + 2:2 * comm.n + 8]
        sems = rest[2 * comm.n + 8:]
        jj, i = pl.program_id(0), pl.program_id(1)
        (kx, ky, kc), me, chips = _Comm._where()
        sibling = (kx, ky, 1 - kc)
        n = order_ref[jj]

        def push(src, blk, send, to):
            return pltpu.make_async_remote_copy(src_ref=src, dst_ref=wg_ref.at[blk], send_sem=send,
                                                recv_sem=recv_sems.at[blk], device_id=to, device_id_type=MESH)

        def load(src):
            cp = pltpu.make_async_copy(src, wbuf, load_sem)
            cp.start()
            cp.wait()

        @pl.when(jnp.logical_and(jj == 0, i == 0))
        def _():
            push(wsh_ref, me, d2d_send, sibling).start()
            for (px, py) in chips[:2]:
                push(wsh_ref, me, ici_send, (px, py, kc)).start()
            pltpu.make_async_copy(wsh_ref, wg_ref.at[me], local_sem).start()
            load(wsh_ref)

        @pl.when(jnp.logical_and(jj > 0, i == 0))
        def _():
            push(wg_ref.at[n], n, d2d_send, sibling).wait_recv()
            for step, k in forward_at.items():
                @pl.when(jj == step)
                def _(k=k):
                    blk = 4 * chips[k][0] + 2 * chips[k][1] + kc
                    push(wg_ref.at[blk], blk, d2d_send, sibling).start()

            @pl.when(jj == 2)
            def _():
                push(wsh_ref, me, ici_send, (*chips[2], kc)).start()

            if comm.n:
                @pl.when(jj == 3)
                def _():
                    comm.start(c_in, c_out, sems)

            load(wg_ref.at[n])

        z_ref[...] = _dot(u_ref[...], wbuf[...], "nn").astype(BF16)

        @pl.when(jnp.logical_and(jj == N_DEV - 1, i == pl.num_programs(1) - 1))
        def _():
            def drain_sends(send, count):
                blocks = wg_ref.at[pl.ds(0, count)]
                pltpu.make_async_remote_copy(src_ref=blocks, dst_ref=blocks, send_sem=send, recv_sem=recv_sems.at[0],
                                             device_id=sibling, device_id_type=MESH).wait_send()

            drain_sends(ici_send, n_far)
            drain_sends(d2d_send, n_far + 1)
            pltpu.make_async_copy(wsh_ref, wg_ref.at[me], local_sem).wait()
            if comm.n:
                comm.wait(c_in, c_out, sems)

    any_spec = pl.BlockSpec(memory_space=pl.ANY)
    grid_spec = pltpu.PrefetchScalarGridSpec(
        num_scalar_prefetch=1, grid=(N_DEV, seq // tm),
        in_specs=[pl.BlockSpec((tm, D_MODEL), lambda jj, i, order_ref: (i, 0)), any_spec] + comm.in_specs,
        out_specs=[pl.BlockSpec((tm, D_MODEL), lambda jj, i, order_ref: (i, order_ref[jj])), any_spec] + comm.out_specs,
        scratch_shapes=[pltpu.VMEM((D_MODEL, D_MODEL), BF16), pltpu.SemaphoreType.DMA, pltpu.SemaphoreType.DMA,
                        pltpu.SemaphoreType.DMA((N_DEV,)), pltpu.SemaphoreType.DMA, pltpu.SemaphoreType.DMA] + comm.scratch)
    return pl.pallas_call(
        body, name="in_proj_gather", grid_spec=grid_spec,
        out_shape=[jax.ShapeDtypeStruct((seq, IN_W), BF16), jax.ShapeDtypeStruct((N_DEV, D_MODEL, D_MODEL), BF16)] + comm.out_shape,
        compiler_params=_params(dimension_semantics=("arbitrary", "arbitrary")),
    )(order, u, w_shard, *comm.arrays)


_HBM = pl.BlockSpec(memory_space=pltpu.HBM)
_SEM = pl.BlockSpec(memory_space=pltpu.SEMAPHORE)


def _chunks_start(dw):
    def body(src_ref, land_ref, send_sem, recv_sem, src_thru, land_thru, token):
        del src_thru, land_thru
        me, peers = _peers()
        for (px, py, pc) in peers:
            pltpu.make_async_remote_copy(src_ref=src_ref.at[4 * px + 2 * py + pc], dst_ref=land_ref.at[me], send_sem=send_sem,
                                         recv_sem=recv_sem, device_id=(px, py, pc), device_id_type=MESH).start()
        token[...] = jnp.zeros_like(token)

    return pl.pallas_call(
        body, name="w_in_grad_start",
        out_shape=(pltpu.SemaphoreType.DMA(()), pltpu.SemaphoreType.DMA(()), pltpu.HBM(dw.shape, dw.dtype),
                   pltpu.HBM(dw.shape, dw.dtype), jax.ShapeDtypeStruct((SUBLANES, HEAD_DIM), F32)),
        in_specs=(_HBM, _HBM), out_specs=(_SEM, _SEM, _HBM, _HBM, pl.BlockSpec(memory_space=pltpu.VMEM)),
        input_output_aliases={0: 2, 1: 3},
        compiler_params=pltpu.CompilerParams(has_side_effects=pltpu.SideEffectType.DATAFLOW_SIDE_EFFECTING),
    )(pltpu.with_memory_space_constraint(dw, pltpu.HBM),
      pltpu.with_memory_space_constraint(lax.empty(dw.shape, dw.dtype), pltpu.HBM))


def _chunks_wait(send_sem, recv_sem, src_thru, land_thru, after):
    def body(src_ref, land_ref, send_sem, recv_sem, after_ref, src_dead, land_out):
        del after_ref, src_dead, land_out
        seven = land_ref.at[pl.ds(0, N_DEV - 1)]
        cp = pltpu.make_async_remote_copy(src_ref=seven, dst_ref=seven, send_sem=send_sem, recv_sem=recv_sem,
                                          device_id=_peers()[1][0], device_id_type=MESH)
        cp.wait_send()
        cp.wait_recv()

    return pl.pallas_call(
        body, name="w_in_grad_wait",
        out_shape=(pltpu.HBM(src_thru.shape, src_thru.dtype), pltpu.HBM(land_thru.shape, land_thru.dtype)),
        in_specs=(_HBM, _HBM, _SEM, _SEM, pl.BlockSpec(memory_space=pl.ANY)), out_specs=(_HBM, _HBM),
        input_output_aliases={0: 0, 1: 1},
        compiler_params=pltpu.CompilerParams(has_side_effects=pltpu.SideEffectType.DATAFLOW_SIDE_EFFECTING),
    )(src_thru, land_thru, send_sem, recv_sem, after)


class _Plan:
    def __init__(self, shards, n_tap_cols):
        self.shards = shards
        self.gathered = {}
        self.parts = {}
        self.n_tap_cols = n_tap_cols

    def project_in(self, u):
        comm = self.comm("in_proj", None)
        res = _in_proj_gather(u, self.shards["w_in"], comm)
        self.gathered["w_in"] = res[1]
        self.done("in_proj", res[2:])
        return res[0]

    def start_w_in(self, dw_in):
        *self.in_flight, token = _chunks_start(dw_in)
        return token[0, 0]

    def finish_w_in(self, after):
        src, land = _chunks_wait(*self.in_flight, after)
        me = _peers()[0]
        own = lax.dynamic_slice(src, (me, 0, 0), (1,) + src.shape[1:])
        return lax.dynamic_update_slice(land, own, (me, 0, 0))

    def comm(self, name, dw):
        if name in FWD_CARRY:
            return _Comm(replicated=[self.shards[k] for k in FWD_CARRY[name]])
        if name in BWD_CARRY:
            return _Comm(chunked=[dw[k].reshape((N_DEV,) + SHARD_SHAPE[k]) for k in BWD_CARRY[name]])
        return _NO_COMM

    def done(self, name, got):
        if name in FWD_CARRY:
            self.gathered.update(zip(FWD_CARRY[name], got))
        elif name in BWD_CARRY:
            self.parts.update(zip(BWD_CARRY[name], got))

    def w(self, key):
        g = self.gathered[key]
        if key in ("w_in", "w_up", "w_ckv"):
            return g
        if key in ("w_attn_proj", "w_conv_proj"):
            return _cols_to_2d(g)
        if key == "taps":
            return jnp.pad(_cols_to_2d(g[:, :CONV_K, :self.n_tap_cols]), ((0, 1), (0, 0)))
        return g.reshape(-1, g.shape[-1])


def _adamw(name, w, m, v, parts, comm=_NO_COMM):
    rows, cols = w.shape
    n_parts = parts.shape[0]
    rb = rows if rows <= 256 or rows % 256 else 256

    def body(w_ref, m_ref, v_ref, p_ref, g_ref, d_ref, nm_ref, nv_ref):
        g = p_ref[0].astype(F32)
        for q in range(1, n_parts):
            g = g + p_ref[q].astype(F32)
        wv = w_ref[...]
        nm = ADAM_B1 * m_ref[...] + (1.0 - ADAM_B1) * g
        nv = ADAM_B2 * v_ref[...] + (1.0 - ADAM_B2) * jnp.square(g)
        m_hat = nm / (1.0 - ADAM_B1 ** ADAM_STEP)
        v_hat = nv / (1.0 - ADAM_B2 ** ADAM_STEP)
        g_ref[...] = g
        d_ref[...] = -ADAM_LR * (m_hat / (jnp.sqrt(v_hat) + ADAM_EPS) + ADAM_WD * wv)
        nm_ref[...] = nm
        nv_ref[...] = nv

    blk = pl.BlockSpec((rb, cols), lambda i: (i, 0))
    return _pcall(body, name, (rows // rb,), [w, m, v, parts],
                  [blk, blk, blk, pl.BlockSpec((n_parts, rb, cols), lambda i: (0, i, 0))],
                  [jax.ShapeDtypeStruct((rows, cols), F32)] * 4, [blk] * 4, comm=comm)


def _sum_parts(name, parts):
    def body(p_ref, o_ref):
        acc = p_ref[0]
        for q in range(1, parts.shape[0]):
            acc = acc + p_ref[q]
        o_ref[...] = acc

    return _pcall(body, name, (1,), [parts], [pl.BlockSpec(parts.shape, lambda i: (0, 0, 0))],
                  [jax.ShapeDtypeStruct(parts.shape[1:], F32)], [pl.BlockSpec(parts.shape[1:], lambda i: (0, 0))])[0]


BIG = ("w_in", "w_attn_proj", "w_conv_proj", "w_out", "w_cq", "w_ckv", "w_co", "w_up", "w_down")
SMALL = ("g_mix", "b_gate", "conv_b", "conv_ln_g", "conv_ln_b", "g_cross", "g_mem", "g_mlp", "g_final")
SMALL_ORDER = SMALL + ("loss", "conv_w")
WEIGHTS = ("g_mix", "w_in", "b_gate", "conv_w", "conv_b", "conv_ln_g", "conv_ln_b", "w_attn_proj", "w_conv_proj", "w_out",
           "g_cross", "g_mem", "w_cq", "w_ckv", "w_co", "g_mlp", "w_up", "w_down", "g_final")


def kernel(x, mem, g_mix, w_in, b_gate, conv_w, conv_b, conv_ln_g, conv_ln_b, w_attn_proj, w_conv_proj, w_out, g_cross, g_mem, w_cq, w_ckv, w_co, g_mlp, w_up, w_down, g_final, loss_target, m_g_mix, m_w_in, m_b_gate, m_conv_w, m_conv_b, m_conv_ln_g, m_conv_ln_b, m_w_attn_proj, m_w_conv_proj, m_w_out, m_g_cross, m_g_mem, m_w_cq, m_w_ckv, m_w_co, m_g_mlp, m_w_up, m_w_down, m_g_final, v_g_mix, v_w_in, v_b_gate, v_conv_w, v_conv_b, v_conv_ln_g, v_conv_ln_b, v_w_attn_proj, v_w_conv_proj, v_w_out, v_g_cross, v_g_mem, v_w_cq, v_w_ckv, v_w_co, v_g_mlp, v_w_up, v_w_down, v_g_final):
    args = dict(locals())
    wts = {k: args[k] for k in WEIGHTS}
    mom = {k: args["m_" + k] for k in WEIGHTS}
    var = {k: args["v_" + k] for k in WEIGHTS}
    two_d = lambda a: a.reshape(a.shape[-2:]) if a.ndim == 3 else a.reshape(1, -1)

    shards = {k: two_d(wts[k]).astype(BF16) for k in BIG}
    shards["taps"] = jnp.pad(two_d(conv_w), ((0, 1), (0, HEAD_DIM - conv_w.shape[-1])))
    plan = _Plan(shards, conv_w.shape[-1])
    sm = {k: two_d(wts[k]) for k in SMALL}

    grad_x, _, small = _local_step(x[0], mem[0], loss_target[0], sm, plan)
    parts = plan.parts

    out = {}
    small_comm = _Comm(replicated=[small[k] for k in SMALL_ORDER])
    for k in BIG[1:]:
        res = _adamw("adamw_" + k, two_d(wts[k]), two_d(mom[k]), two_d(var[k]), parts[k],
                     comm=small_comm if k == BIG[1] else _NO_COMM)
        out[k] = [r.reshape(wts[k].shape) for r in res[:4]]
        if k == BIG[1]:
            small_parts = dict(zip(SMALL_ORDER, res[4:]))
    for k in SMALL:
        res = _adamw("adamw_" + k, two_d(wts[k]), two_d(mom[k]), two_d(var[k]), small_parts[k])
        out[k] = [r.reshape(wts[k].shape) for r in res]
    res = _adamw("adamw_w_in", two_d(w_in), two_d(m_w_in), two_d(v_w_in), plan.finish_w_in(after=out["g_final"][3]))
    out["w_in"] = [r.reshape(w_in.shape) for r in res]
    loss = _sum_parts("loss_sum", small_parts["loss"])[0, 0]
    me = 4 * lax.axis_index("x") + 2 * lax.axis_index("y") + lax.axis_index("c")
    n_tap_cols = conv_w.shape[-1]
    tap_parts = lax.dynamic_slice(small_parts["conv_w"], (0, 0, me * n_tap_cols), (N_DEV, CONV_K, n_tap_cols))
    res = _adamw("adamw_conv_w", two_d(conv_w), two_d(m_conv_w), two_d(v_conv_w), tap_parts)
    out["conv_w"] = [r.reshape(conv_w.shape) for r in res]

    return (loss, grad_x[None], *[out[k][0] for k in WEIGHTS], *[out[k][1] for k in WEIGHTS],
            *[out[k][2] for k in WEIGHTS], *[out[k][3] for k in WEIGHTS])
```

```python
import functools

import jax
import jax.numpy as jnp
import numpy as np
from jax import lax
from jax.experimental import pallas as pl
from jax.experimental.pallas import tpu as pltpu

F32 = jnp.float32
BF16 = jnp.bfloat16

N_DEV = 8
D_MODEL = 1024
N_MEM = 256
HEAD_DIM = 128
HEADS_PER_GROUP = 4
GROUP_W = HEADS_PER_GROUP * HEAD_DIM
DILATIONS = (1, 4, 16)
BAND = 128
N_GROUPS = 3
ATTN_W = N_GROUPS * GROUP_W
QKV_W = 3 * ATTN_W
ROT_DIM = HEAD_DIM // 4
ROPE_THETA = 500000.0
CONV_CH = 768
CONV_K = 31
CONV_HALO = 32
SUBLANES = 8
CONV_ROWS = 64
IN_W = 8192
GLU_COL_BLK = QKV_W // (2 * CONV_CH)
GATE_COL_BLK = (QKV_W + 2 * CONV_CH) // (2 * D_MODEL)
CROSS_HEADS = 4
CROSS_HD = D_MODEL // CROSS_HEADS
D_FF = 4096
EPS = 1e-6
NEG = -1e30
QB = 4
ROW_BLK = QB * BAND

ADAM_LR = 0.001
ADAM_B1 = 0.9
ADAM_B2 = 0.999
ADAM_EPS = 1e-08
ADAM_WD = 0.01
ADAM_STEP = 10

VMEM_LIMIT = 56 * 1024 * 1024
MESH = pl.DeviceIdType.MESH


def _params(**kw):
    return pltpu.CompilerParams(vmem_limit_bytes=VMEM_LIMIT, **kw)


def _sigmoid(x):
    return 1.0 / (1.0 + jnp.exp(-x))


def _dot(a, b, kind):
    dims = {"nn": (((1,), (0,)), ((), ())), "nt": (((1,), (1,)), ((), ())), "tn": (((0,), (0,)), ((), ()))}[kind]
    if a.dtype != BF16:
        a = a.astype(BF16)
    if b.dtype != BF16:
        b = b.astype(BF16)
    return lax.dot_general(a, b, dims, preferred_element_type=F32)


def _peers():
    x, y, c = lax.axis_index("x"), lax.axis_index("y"), lax.axis_index("c")
    me = 4 * x + 2 * y + c
    peers = [(x, y, 1 - c), (1 - x, y, c), (x, 1 - y, c), (1 - x, 1 - y, c),
             (1 - x, y, 1 - c), (x, 1 - y, 1 - c), (1 - x, 1 - y, 1 - c)]
    return me, peers


class _Comm:
    def __init__(self, chunked=(), replicated=()):
        self.arrays = list(chunked) + list(replicated)
        self.n_c = len(chunked)
        self.n = len(self.arrays)
        self.out_shape = [jax.ShapeDtypeStruct(a.shape, a.dtype) for a in chunked]
        self.out_shape += [jax.ShapeDtypeStruct((N_DEV,) + a.shape, a.dtype) for a in replicated]
        self.in_specs = [pl.BlockSpec(memory_space=pl.ANY)] * self.n
        self.out_specs = [pl.BlockSpec(memory_space=pl.ANY)] * self.n
        self.scratch = [pltpu.SemaphoreType.DMA((self.n,))] * 5 if self.n else []

    @staticmethod
    def _where():
        x, y, c = lax.axis_index("x"), lax.axis_index("y"), lax.axis_index("c")
        chips = [(1 - x, y), (x, 1 - y), (1 - x, 1 - y)]
        return (x, y, c), 4 * x + 2 * y + c, chips

    def _local(self, ins, outs, sems, a, me):
        src = ins[a].at[me] if a < self.n_c else ins[a]
        return pltpu.make_async_copy(src, outs[a].at[me], sems[2].at[a])

    @staticmethod
    def _remote(src, dst, send, recv, to):
        return pltpu.make_async_remote_copy(src_ref=src, dst_ref=dst, send_sem=send, recv_sem=recv, device_id=to,
                                            device_id_type=MESH)

    def start(self, ins, outs, sems):
        (x, y, c), me, chips = self._where()
        for a in range(self.n):
            self._local(ins, outs, sems, a, me).start()
            if a < self.n_c:
                for (px, py, pc) in _peers()[1]:
                    self._remote(ins[a].at[4 * px + 2 * py + pc], outs[a].at[me], sems[0].at[a], sems[1].at[a], (px, py, pc)).start()
            else:
                self._remote(ins[a], outs[a].at[me], sems[3].at[a], sems[4].at[a], (x, y, 1 - c)).start()
                for (px, py) in chips:
                    self._remote(ins[a], outs[a].at[me], sems[0].at[a], sems[1].at[a], (px, py, c)).start()

    def wait(self, ins, outs, sems):
        (x, y, c), me, chips = self._where()
        sibling = (x, y, 1 - c)

        def drain(a, pair, count):
            blocks = outs[a].at[pl.ds(0, count)]
            cp = self._remote(blocks, blocks, sems[pair].at[a], sems[pair + 1].at[a], sibling)
            cp.wait_send()
            cp.wait_recv()

        for a in range(self.n):
            if a < self.n_c:
                drain(a, 0, N_DEV - 1)
            else:
                drain(a, 0, len(chips))
                for (px, py) in chips:
                    blk = outs[a].at[4 * px + 2 * py + c]
                    self._remote(blk, blk, sems[3].at[a], sems[4].at[a], sibling).start()
        for a in range(self.n):
            if a >= self.n_c:
                drain(a, 3, len(chips) + 1)
            self._local(ins, outs, sems, a, me).wait()


_NO_COMM = _Comm()


def _pcall(body, name, grid, operands, in_specs, out_shape, out_specs, scratch=(), aliases=None, comm=_NO_COMM, **params):
    n_in, n_out, n_scr = len(operands), len(out_shape), len(scratch)
    grid = tuple(grid)

    def carried(*refs):
        ins, c_in = refs[:n_in], refs[n_in:n_in + comm.n]
        o0 = n_in + comm.n
        outs, c_out = refs[o0:o0 + n_out], refs[o0 + n_out:o0 + n_out + comm.n]
        s0 = o0 + n_out + comm.n
        scr, sems = refs[s0:s0 + n_scr], refs[s0 + n_scr:]
        ids = [pl.program_id(ax) for ax in range(len(grid))]

        @pl.when(functools.reduce(jnp.logical_and, [p == 0 for p in ids]))
        def _():
            comm.start(c_in, c_out, sems)

        body(*ins, *outs, *scr)

        @pl.when(functools.reduce(jnp.logical_and, [p == g - 1 for p, g in zip(ids, grid)]))
        def _():
            comm.wait(c_in, c_out, sems)

    return pl.pallas_call(
        carried if comm.n else body, name=name, grid=grid, in_specs=list(in_specs) + comm.in_specs,
        out_shape=list(out_shape) + comm.out_shape, out_specs=list(out_specs) + comm.out_specs,
        scratch_shapes=list(scratch) + comm.scratch, input_output_aliases=aliases or {},
        compiler_params=_params(dimension_semantics=("arbitrary",) * len(grid), **params),
    )(*operands, *comm.arrays)


def _mm(name, a, b, kind, grid, a_blk, b_blk, outs, extras=(), epi=None, acc_outs=(), j_outer=False, comm=_NO_COMM,
        split=None, b_resident=False, out_chunks=0):
    gi, gj, gk = grid
    n_ex = len(extras)
    n_out = len(outs)
    mode, n_chunks = split if split is not None else (None, 1)

    def spec(blk, fn, **kw):
        return pl.BlockSpec(blk, (lambda j, i, k: fn(i, j, k)) if j_outer else fn, **kw)

    def b_chunk(b_ref, c):
        if len(b_ref.shape) == 3:
            return b_ref[c]
        rows, cols = b_ref.shape
        if (kind == "nn") == (mode == "cols"):
            return b_ref[:, c * (cols // n_chunks):(c + 1) * (cols // n_chunks)]
        return b_ref[c * (rows // n_chunks):(c + 1) * (rows // n_chunks), :]

    def col_chunk(ref, c):
        width = ref.shape[-1] // n_chunks
        return slice(c * width, (c + 1) * width)

    def body(*refs):
        a_ref, b_ref = refs[0], refs[1]
        ex = refs[2:2 + n_ex]
        out_refs = refs[2 + n_ex:2 + n_ex + n_out]
        acc_ref = refs[2 + n_ex + n_out] if gk > 1 else None
        i = pl.program_id(1 if j_outer else 0)
        k = pl.program_id(2)
        if mode == "cols":
            a_val = a_ref[...]
            for c in range(n_chunks):
                acc = _dot(a_val, b_chunk(b_ref, c), kind)
                vals = epi(acc, *[e[:, col_chunk(e, c)] for e in ex]) if epi is not None else (acc,)
                for o, v in zip(out_refs, vals):
                    o[:, col_chunk(o, c)] = v.astype(o.dtype)
            return
        if mode == "sum":
            part = _dot(a_ref[:, col_chunk(a_ref, 0)], b_chunk(b_ref, 0), kind)
            for c in range(1, n_chunks):
                part = part + _dot(a_ref[:, col_chunk(a_ref, c)], b_chunk(b_ref, c), kind)
        else:
            part = _dot(a_ref[...], b_ref[...], kind)

        def finish(acc):
            if out_chunks:
                width = acc.shape[-1] // out_chunks
                for c in range(out_chunks):
                    out_refs[0][c] = acc[:, c * width:(c + 1) * width].astype(out_refs[0].dtype)
                return
            vals = epi(acc, *[e[...] for e in ex]) if epi is not None else (acc,)
            for idx, (o, v) in enumerate(zip(out_refs, vals)):
                if idx in acc_outs:
                    @pl.when(i == 0)
                    def _():
                        o[...] = v.astype(o.dtype)

                    @pl.when(i != 0)
                    def _():
                        o[...] += v.astype(o.dtype)
                else:
                    o[...] = v.astype(o.dtype)

        if gk == 1:
            finish(part)
        else:
            @pl.when(k == 0)
            def _():
                acc_ref[...] = part

            @pl.when(k != 0)
            def _():
                acc_ref[...] += part

            @pl.when(k == gk - 1)
            def _():
                finish(acc_ref[...])

    scratch = []
    if gk > 1:
        tm = a_blk[0][-1] if kind == "tn" else a_blk[0][-2]
        tn = b_blk[0][-2] if kind == "nt" else b_blk[0][-1]
        scratch = [pltpu.VMEM((tm, tn), F32)]
    b_kw = dict(pipeline_mode=pl.Buffered(1)) if b_resident else {}
    return _pcall(body, name, (gj, gi, gk) if j_outer else (gi, gj, gk), [a, b] + [e for e, _, _ in extras],
                  [spec(*a_blk), spec(*b_blk, **b_kw)] + [spec(blk, fn) for _, blk, fn in extras],
                  [s for s, _, _ in outs], [spec(blk, fn) for _, blk, fn in outs], scratch, comm=comm)


def _rms_fwd_vals(x, g):
    r = lax.rsqrt(jnp.mean(x * x, axis=-1, keepdims=True) + EPS)
    return x * r * g


def _rms_bwd_vals(x, g, du):
    r = lax.rsqrt(jnp.mean(x * x, axis=-1, keepdims=True) + EPS)
    xh = x * r
    dxh = du * g
    dx = r * (dxh - xh * jnp.mean(dxh * xh, axis=-1, keepdims=True))
    return dx, jnp.sum(du * xh, axis=0, keepdims=True)


def _rms_fwd(name, x, g, rows, comm=_NO_COMM):
    n = x.shape[0]

    def body(x_ref, g_ref, o_ref):
        o_ref[...] = _rms_fwd_vals(x_ref[...], g_ref[...]).astype(BF16)

    return _pcall(body, name, (n // rows,), [x, g],
                  [pl.BlockSpec((rows, D_MODEL), lambda i: (i, 0)), pl.BlockSpec((1, D_MODEL), lambda i: (0, 0))],
                  [jax.ShapeDtypeStruct(x.shape, BF16)], [pl.BlockSpec((rows, D_MODEL), lambda i: (i, 0))], comm=comm)


def _rope_tables(seq):
    half = ROT_DIM // 2
    pos = np.arange(seq, dtype=np.float32)
    inv_freq = np.float32(ROPE_THETA) ** (-np.arange(0, ROT_DIM, 2, dtype=np.float32) / np.float32(ROT_DIM))
    ang = (pos[:, None] * inv_freq[None, :]).astype(np.float32)
    cos, sin = np.cos(ang), np.sin(ang)
    rest = HEAD_DIM - ROT_DIM
    c = np.concatenate([cos, cos, np.ones((seq, rest), np.float32)], axis=1)
    s1 = np.concatenate([np.zeros((seq, half), np.float32), sin, np.zeros((seq, rest), np.float32)], axis=1)
    s2 = np.concatenate([-sin, np.zeros((seq, half + rest), np.float32)], axis=1)
    return jnp.asarray(c), jnp.asarray(s1), jnp.asarray(s2)


def _group_shapes(seq, width, dtype):
    return [jax.ShapeDtypeStruct((d, seq // d, width), dtype) for d in DILATIONS]


def _group_specs(width):
    return [pl.BlockSpec((d, ROW_BLK // d, width), lambda i: (0, i, 0)) for d in DILATIONS]


def _qkv_prep(z, tabs, comm=_NO_COMM):
    seq = z.shape[0]

    def body(z_ref, c_ref, s1_ref, s2_ref, a0, a1, a2, sc):
        outs = (a0, a1, a2)
        c, s1, s2 = c_ref[...], s1_ref[...], s2_ref[...]
        for part in range(3):
            for hh in range(N_GROUPS * HEADS_PER_GROUP):
                g, hl = divmod(hh, HEADS_PER_GROUP)
                col = part * ATTN_W + hh * HEAD_DIM
                ocol = part * GROUP_W + hl * HEAD_DIM
                x = z_ref[:, col:col + HEAD_DIM].astype(F32)
                if part < 2:
                    x = x * c + pltpu.roll(x, ROT_DIM // 2, 1) * s1 + pltpu.roll(x, HEAD_DIM - ROT_DIM // 2, 1) * s2
                d = DILATIONS[g]
                if d == 1:
                    outs[g][0, :, ocol:ocol + HEAD_DIM] = x.astype(BF16)
                else:
                    sc[...] = x
                    for r in range(d):
                        outs[g][r, :, ocol:ocol + HEAD_DIM] = sc[pl.ds(r, ROW_BLK // d, stride=d), :].astype(BF16)

    tab_spec = pl.BlockSpec((ROW_BLK, HEAD_DIM), lambda i: (i, 0))
    return _pcall(body, "qkv_prep", (seq // ROW_BLK,), [z, *tabs],
                  [pl.BlockSpec((ROW_BLK, QKV_W), lambda i: (i, 0)), tab_spec, tab_spec, tab_spec],
                  _group_shapes(seq, ATTN_W, BF16), _group_specs(ATTN_W), [pltpu.VMEM((ROW_BLK, HEAD_DIM), F32)], comm=comm)


def _band_masks_2(t):
    qi = lax.broadcasted_iota(jnp.int32, (BAND, 2 * BAND), 0)
    kj = lax.broadcasted_iota(jnp.int32, (BAND, 2 * BAND), 1)
    band = jnp.logical_and(kj >= qi, kj <= qi + BAND)
    return band, jnp.logical_and(band, jnp.logical_or(kj >= BAND, t > 0))


def _attn_fwd(name, a_g, comm=_NO_COMM):
    dil, m_len, _ = a_g.shape
    qb = min(QB, m_len // BAND)
    rows = qb * BAND
    steps = m_len // rows
    scale = HEAD_DIM ** -0.5

    tiles = [(sb, h) for sb in range(qb) for h in range(HEADS_PER_GROUP)]

    def body(q_ref, kc_ref, vc_ref, kp_ref, vp_ref, o_ref, l_ref, k_all, v_all, s_scr, p_scr, r_scr):
        t = pl.program_id(1)
        k_all[0:BAND, :] = kp_ref[...]
        k_all[BAND:, :] = kc_ref[...]
        v_all[0:BAND, :] = vp_ref[...]
        v_all[BAND:, :] = vc_ref[...]
        band, band_first = _band_masks_2(t)
        for idx, (sb, h) in enumerate(tiles):
            cs = slice(h * HEAD_DIM, (h + 1) * HEAD_DIM)
            s = _dot(q_ref[sb * BAND:(sb + 1) * BAND, cs], k_all[sb * BAND:(sb + 2) * BAND, cs], "nt") * scale
            s_scr[idx] = jnp.where(band_first if sb == 0 else band, s, NEG)
        lane = lax.broadcasted_iota(jnp.int32, (BAND, HEAD_DIM), 1)
        lse_rows = [jnp.zeros((BAND, HEAD_DIM), F32)] * qb
        for idx, (sb, h) in enumerate(tiles):
            s = s_scr[idx]
            mx = jnp.max(s, axis=-1, keepdims=True)
            p = jnp.exp(s - mx)
            den = jnp.sum(p, axis=-1, keepdims=True)
            p_scr[idx] = p.astype(BF16)
            r_scr[idx] = jnp.broadcast_to(1.0 / den, (BAND, HEAD_DIM))
            lse_rows[sb] = jnp.where(lane == h, jnp.broadcast_to(mx + jnp.log(den), (BAND, HEAD_DIM)), lse_rows[sb])
        for sb in range(qb):
            l_ref[sb * BAND:(sb + 1) * BAND, :] = lse_rows[sb]
        for idx, (sb, h) in enumerate(tiles):
            cs = slice(h * HEAD_DIM, (h + 1) * HEAD_DIM)
            o_ref[sb * BAND:(sb + 1) * BAND, cs] = _dot(p_scr[idx], v_all[sb * BAND:(sb + 2) * BAND, cs], "nn") * r_scr[idx]

    def prev(r, t):
        return jnp.maximum(qb * t - 1, 0)

    cur = lambda c: pl.BlockSpec((None, rows, GROUP_W), lambda r, t, c=c: (r, t, c))
    prv = lambda c: pl.BlockSpec((None, BAND, GROUP_W), lambda r, t, c=c: (r, prev(r, t), c))
    out_spec = lambda width: pl.BlockSpec((None, rows, width), lambda r, t: (r, t, 0))
    shp = lambda width: jax.ShapeDtypeStruct((dil, m_len, width), F32)
    n_t = len(tiles)
    return _pcall(body, name, (dil, steps), [a_g] * 5, [cur(0), cur(1), cur(2), prv(1), prv(2)],
                  [shp(GROUP_W), shp(HEAD_DIM)], [out_spec(GROUP_W), out_spec(HEAD_DIM)],
                  [pltpu.VMEM((rows + BAND, GROUP_W), BF16), pltpu.VMEM((rows + BAND, GROUP_W), BF16),
                   pltpu.VMEM((n_t, BAND, 2 * BAND), F32), pltpu.VMEM((n_t, BAND, 2 * BAND), BF16),
                   pltpu.VMEM((n_t, BAND, HEAD_DIM), F32)], comm=comm)


def _attn_merge(os_, ls_, seq):
    def body(o0, l0, o1, l1, o2, l2, at_ref, lt_ref, sc, lsc):
        for gi, l_r in enumerate((l1, l2)):
            d = DILATIONS[gi + 1]
            for r in range(d):
                lsc.at[gi][pl.ds(r, ROW_BLK // d, stride=d), :] = l_r[r]
        lse = (l0.at[0], lsc.at[0], lsc.at[1])
        lane = lax.broadcasted_iota(jnp.int32, (ROW_BLK, HEAD_DIM), 1)
        lt_rows = jnp.zeros((ROW_BLK, HEAD_DIM), F32)
        for h in range(HEADS_PER_GROUP):
            cs = slice(h * HEAD_DIM, (h + 1) * HEAD_DIM)
            for gi, o_r in enumerate((o1, o2)):
                d = DILATIONS[gi + 1]
                for r in range(d):
                    sc.at[gi][pl.ds(r, ROW_BLK // d, stride=d), :] = o_r[r, :, cs]
            l_h = [v[:, h:h + 1] for v in lse]
            mx = jnp.maximum(jnp.maximum(l_h[0], l_h[1]), l_h[2])
            e = [jnp.exp(v - mx) for v in l_h]
            tot = e[0] + e[1] + e[2]
            inv = 1.0 / tot
            at_ref[:, cs] = ((e[0] * inv) * o0[0, :, cs] + (e[1] * inv) * sc[0] + (e[2] * inv) * sc[1]).astype(BF16)
            lt_rows = jnp.where(lane == h, jnp.broadcast_to(mx + jnp.log(tot), (ROW_BLK, HEAD_DIM)), lt_rows)
        lt_ref[...] = lt_rows

    go, gl = _group_specs(GROUP_W), _group_specs(HEAD_DIM)
    return pl.pallas_call(
        body, name="attn_merge",
        out_shape=[jax.ShapeDtypeStruct((seq, GROUP_W), BF16), jax.ShapeDtypeStruct((seq, HEAD_DIM), F32)],
        grid=(seq // ROW_BLK,), in_specs=[go[0], gl[0], go[1], gl[1], go[2], gl[2]],
        out_specs=[pl.BlockSpec((ROW_BLK, GROUP_W), lambda i: (i, 0)), pl.BlockSpec((ROW_BLK, HEAD_DIM), lambda i: (i, 0))],
        scratch_shapes=[pltpu.VMEM((2, ROW_BLK, HEAD_DIM), F32), pltpu.VMEM((2, ROW_BLK, HEAD_DIM), F32)],
        compiler_params=_params(dimension_semantics=("arbitrary",)),
    )(os_[0], ls_[0], os_[1], ls_[1], os_[2], ls_[2])


def _attn_bwd_prep(dattn, attn, lt):
    seq = dattn.shape[0]

    def body(da_ref, at_ref, lt_ref, cl0, d1, cl1, d2, cl2, sc, csc):
        lane = lax.broadcasted_iota(jnp.int32, (ROW_BLK, HEAD_DIM), 1)
        cl = pltpu.roll(lt_ref[...], HEADS_PER_GROUP, 1)
        for h in range(HEADS_PER_GROUP):
            cs = slice(h * HEAD_DIM, (h + 1) * HEAD_DIM)
            da = da_ref[:, cs].astype(F32)
            cc = jnp.sum(da * at_ref[:, cs].astype(F32), axis=-1, keepdims=True)
            cl = jnp.where(lane == h, jnp.broadcast_to(cc, (ROW_BLK, HEAD_DIM)), cl)
            sc[...] = da
            for g, d_ref in ((1, d1), (2, d2)):
                d = DILATIONS[g]
                for r in range(d):
                    d_ref[r, :, cs] = sc[pl.ds(r, ROW_BLK // d, stride=d), :].astype(BF16)
        cl0[0] = cl
        csc[...] = cl
        for g, c_ref in ((1, cl1), (2, cl2)):
            d = DILATIONS[g]
            for r in range(d):
                c_ref[r] = csc[pl.ds(r, ROW_BLK // d, stride=d), :]

    go, gl = _group_specs(GROUP_W), _group_specs(HEAD_DIM)
    row = lambda width: pl.BlockSpec((ROW_BLK, width), lambda i: (i, 0))
    shape = lambda g, width, dt: jax.ShapeDtypeStruct((DILATIONS[g], seq // DILATIONS[g], width), dt)
    cl0, d1, cl1, d2, cl2 = pl.pallas_call(
        body, name="attn_bwd_prep",
        out_shape=[shape(0, HEAD_DIM, F32), shape(1, GROUP_W, BF16), shape(1, HEAD_DIM, F32), shape(2, GROUP_W, BF16),
                   shape(2, HEAD_DIM, F32)],
        grid=(seq // ROW_BLK,), in_specs=[row(GROUP_W), row(GROUP_W), row(HEAD_DIM)],
        out_specs=[gl[0], go[1], gl[1], go[2], gl[2]],
        scratch_shapes=[pltpu.VMEM((ROW_BLK, HEAD_DIM), F32), pltpu.VMEM((ROW_BLK, HEAD_DIM), F32)],
        compiler_params=_params(dimension_semantics=("arbitrary",)),
    )(dattn, attn, lt)
    return [(dattn[None], cl0), (d1, cl1), (d2, cl2)]


def _attn_bwd(name, a_g, da_g, cl_g, comm=_NO_COMM):
    dil, m_len, _ = a_g.shape
    qb = min(QB, m_len // BAND)
    rows = qb * BAND
    steps = m_len // rows
    scale = HEAD_DIM ** -0.5

    tiles = [(sb, h) for sb in range(qb) for h in range(HEADS_PER_GROUP)]

    def body(q_ref, kc_ref, vc_ref, kp_ref, vp_ref, da_ref, cl_ref, d_ref, dk_acc, dv_acc, car_k, car_v,
             k_all, v_all, s_scr, dp_scr, p_scr, ds_scr):
        tg = pl.program_id(1)
        t = steps - 1 - tg

        @pl.when(tg == 0)
        def _():
            car_k[...] = jnp.zeros_like(car_k)
            car_v[...] = jnp.zeros_like(car_v)

        k_all[0:BAND, :] = kp_ref[...]
        k_all[BAND:, :] = kc_ref[...]
        v_all[0:BAND, :] = vp_ref[...]
        v_all[BAND:, :] = vc_ref[...]
        zero = jnp.zeros((rows, GROUP_W), F32)
        dk_acc[0:rows, :] = zero
        dv_acc[0:rows, :] = zero
        dk_acc[rows:rows + BAND, :] = car_k[...]
        dv_acc[rows:rows + BAND, :] = car_v[...]
        band, band_first = _band_masks_2(t)
        for idx, (sb, h) in enumerate(tiles):
            cs = slice(h * HEAD_DIM, (h + 1) * HEAD_DIM)
            rs, ks = slice(sb * BAND, (sb + 1) * BAND), slice(sb * BAND, (sb + 2) * BAND)
            s_scr[idx] = _dot(q_ref[rs, cs], k_all[ks, cs], "nt")
            dp_scr[idx] = _dot(da_ref[rs, cs], v_all[ks, cs], "nt")
        for idx, (sb, h) in enumerate(tiles):
            cs = slice(h * HEAD_DIM, (h + 1) * HEAD_DIM)
            rs = slice(sb * BAND, (sb + 1) * BAND)
            cc = jnp.broadcast_to(cl_ref[rs, h:h + 1], (BAND, 2 * BAND))
            ltv = jnp.broadcast_to(cl_ref[rs, HEADS_PER_GROUP + h:HEADS_PER_GROUP + h + 1], (BAND, 2 * BAND))
            p = jnp.exp(jnp.where(band_first if sb == 0 else band, s_scr[idx] * scale - ltv, NEG))
            p_scr[idx] = p.astype(BF16)
            ds_scr[idx] = (p * (dp_scr[idx] - cc) * scale).astype(BF16)
        for idx, (sb, h) in enumerate(tiles):
            cs = slice(h * HEAD_DIM, (h + 1) * HEAD_DIM)
            rs, ks = slice(sb * BAND, (sb + 1) * BAND), slice(sb * BAND, (sb + 2) * BAND)
            d_ref[rs, cs] = _dot(ds_scr[idx], k_all[ks, cs], "nn").astype(BF16)
            dk_acc[ks, cs] += _dot(ds_scr[idx], q_ref[rs, cs], "tn")
            dv_acc[ks, cs] += _dot(p_scr[idx], da_ref[rs, cs], "tn")
        d_ref[:, GROUP_W:2 * GROUP_W] = dk_acc[BAND:rows + BAND, :].astype(BF16)
        d_ref[:, 2 * GROUP_W:3 * GROUP_W] = dv_acc[BAND:rows + BAND, :].astype(BF16)
        car_k[...] = dk_acc[0:BAND, :]
        car_v[...] = dv_acc[0:BAND, :]

    def rev(tg):
        return steps - 1 - tg

    def prev(tg):
        return jnp.maximum(qb * rev(tg) - 1, 0)

    cur = lambda c: pl.BlockSpec((None, rows, GROUP_W), lambda r, tg, c=c: (r, rev(tg), c))
    prv = lambda c: pl.BlockSpec((None, BAND, GROUP_W), lambda r, tg, c=c: (r, prev(tg), c))
    return _pcall(
        body, name, (dil, steps), [a_g, a_g, a_g, a_g, a_g, da_g, cl_g],
        [cur(0), cur(1), cur(2), prv(1), prv(2), cur(0), pl.BlockSpec((None, rows, HEAD_DIM), lambda r, tg: (r, rev(tg), 0))],
        [jax.ShapeDtypeStruct((dil, m_len, ATTN_W), BF16)], [pl.BlockSpec((None, rows, ATTN_W), lambda r, tg: (r, rev(tg), 0))],
        [pltpu.VMEM((rows + BAND, GROUP_W), F32), pltpu.VMEM((rows + BAND, GROUP_W), F32),
         pltpu.VMEM((BAND, GROUP_W), F32), pltpu.VMEM((BAND, GROUP_W), F32),
         pltpu.VMEM((rows + BAND, GROUP_W), BF16), pltpu.VMEM((rows + BAND, GROUP_W), BF16),
         pltpu.VMEM((len(tiles), BAND, 2 * BAND), F32), pltpu.VMEM((len(tiles), BAND, 2 * BAND), F32),
         pltpu.VMEM((len(tiles), BAND, 2 * BAND), BF16), pltpu.VMEM((len(tiles), BAND, 2 * BAND), BF16)], comm=comm)


def _dqkv_post(d_gs, tabs, dz):
    seq = dz.shape[0]

    def body(g0, g1, g2, c_ref, s1_ref, s2_ref, dz_any, o_ref, sc):
        del dz_any
        ins = (g0, g1, g2)
        c, s1, s2 = c_ref[...], s1_ref[...], s2_ref[...]
        for part in range(3):
            for hh in range(N_GROUPS * HEADS_PER_GROUP):
                g, hl = divmod(hh, HEADS_PER_GROUP)
                icol = part * GROUP_W + hl * HEAD_DIM
                ocol = part * ATTN_W + hh * HEAD_DIM
                d = DILATIONS[g]
                if d == 1:
                    x = ins[g][0, :, icol:icol + HEAD_DIM].astype(F32)
                else:
                    for r in range(d):
                        sc[pl.ds(r, ROW_BLK // d, stride=d), :] = ins[g][r, :, icol:icol + HEAD_DIM].astype(F32)
                    x = sc[...]
                if part < 2:
                    x = x * c + pltpu.roll(x * s1, HEAD_DIM - ROT_DIM // 2, 1) + pltpu.roll(x * s2, ROT_DIM // 2, 1)
                o_ref[:, ocol:ocol + HEAD_DIM] = x.astype(BF16)

    tab_spec = pl.BlockSpec((ROW_BLK, HEAD_DIM), lambda i: (i, 0))
    return pl.pallas_call(
        body, name="dqkv_post", out_shape=jax.ShapeDtypeStruct(dz.shape, BF16), grid=(seq // ROW_BLK,),
        in_specs=_group_specs(ATTN_W) + [tab_spec, tab_spec, tab_spec, pl.BlockSpec(memory_space=pl.ANY)],
        out_specs=pl.BlockSpec((ROW_BLK, QKV_W), lambda i: (i, 0)),
        scratch_shapes=[pltpu.VMEM((ROW_BLK, HEAD_DIM), F32)], input_output_aliases={6: 0},
        compiler_params=_params(dimension_semantics=("arbitrary",)),
    )(*d_gs, *tabs, dz)


def _glu(zg):
    a = zg[:, :CONV_CH].astype(F32)
    s = _sigmoid(zg[:, CONV_CH:].astype(F32))
    return a, s, a * s


def _shifted_copies(xs):
    n = xs.shape[1] - SUBLANES
    for b in range(1, SUBLANES):
        xs[b, 0:n, :] = xs[0, pl.ds(b, n), :]


def _shifted(xs, offset, r0, cs):
    a, b = divmod(offset, SUBLANES)
    return xs[b, pl.ds(SUBLANES * a + r0, CONV_ROWS), cs]


def _conv_fwd(z, cw, cb, lg, lb, comm=_NO_COMM):
    seq = z.shape[0]
    halo_per_blk = ROW_BLK // CONV_HALO

    def body(zg_ref, zh_ref, cw_ref, cb_ref, lg_ref, lb_ref, c2_ref, c4_ref, xs):
        i = pl.program_id(0)
        _, _, c1 = _glu(zg_ref[...])
        _, _, c1h = _glu(zh_ref[...])
        xs[0, 0:CONV_HALO, :] = jnp.where(i > 0, c1h, 0.0)
        xs[0, CONV_HALO:, :] = c1
        _shifted_copies(xs)
        for s in range(CONV_CH // HEAD_DIM):
            cs = slice(s * HEAD_DIM, (s + 1) * HEAD_DIM)
            taps = [cw_ref[j:j + 1, cs] for j in range(CONV_K)]
            bias = cb_ref[:, cs]

            def chunk(rc, carry, cs=cs, taps=taps, bias=bias):
                r0 = pl.multiple_of(rc * CONV_ROWS, CONV_ROWS)
                acc = [jnp.zeros((CONV_ROWS, HEAD_DIM), F32)] * 2
                for j in range(CONV_K):
                    acc[j % 2] = acc[j % 2] + taps[j] * _shifted(xs, CONV_HALO - (CONV_K - 1) + j, r0, cs)
                c2_ref[pl.ds(r0, CONV_ROWS), cs] = acc[0] + acc[1] + bias
                return carry

            lax.fori_loop(0, ROW_BLK // CONV_ROWS, chunk, 0)
        c2 = c2_ref[...]
        mu = jnp.mean(c2, axis=-1, keepdims=True)
        xc = c2 - mu
        rstd = lax.rsqrt(jnp.mean(xc * xc, axis=-1, keepdims=True) + EPS)
        c3 = xc * rstd * lg_ref[...] + lb_ref[...]
        c4_ref[...] = (c3 * _sigmoid(c3)).astype(BF16)

    vec = pl.BlockSpec((1, CONV_CH), lambda i: (0, 0))
    return _pcall(
        body, "conv_fwd", (seq // ROW_BLK,), [z, z, cw, cb, lg, lb],
        [pl.BlockSpec((ROW_BLK, 2 * CONV_CH), lambda i: (i, GLU_COL_BLK)),
         pl.BlockSpec((CONV_HALO, 2 * CONV_CH), lambda i: (jnp.maximum(i * halo_per_blk - 1, 0), GLU_COL_BLK)),
         pl.BlockSpec((CONV_HALO, CONV_CH), lambda i: (0, 0)), vec, vec, vec],
        [jax.ShapeDtypeStruct((seq, CONV_CH), F32), jax.ShapeDtypeStruct((seq, CONV_CH), BF16)],
        [pl.BlockSpec((ROW_BLK, CONV_CH), lambda i: (i, 0)), pl.BlockSpec((ROW_BLK, CONV_CH), lambda i: (i, 0))],
        [pltpu.VMEM((SUBLANES, ROW_BLK + CONV_HALO, CONV_CH), F32)], comm=comm)


def _conv_bwd(dc2, z, cw, dz, comm=_NO_COMM):
    seq = z.shape[0]
    halo_per_blk = ROW_BLK // CONV_HALO
    n_blk = seq // ROW_BLK
    last_halo = seq // CONV_HALO - 1

    def body(dc_ref, dn_ref, zg_ref, zh_ref, cw_ref, dz_any, o_ref, dcw_ref, xs, ys, dc1_ref, dcw_acc):
        del dz_any
        i = pl.program_id(0)
        a, s, c1 = _glu(zg_ref[...])
        _, _, c1h = _glu(zh_ref[...])
        xs[0, 0:CONV_HALO, :] = jnp.where(i > 0, c1h, 0.0)
        xs[0, CONV_HALO:, :] = c1
        ys[0, 0:ROW_BLK, :] = dc_ref[...]
        ys[0, ROW_BLK:, :] = jnp.where(i < n_blk - 1, dn_ref[...], 0.0)
        _shifted_copies(xs)
        _shifted_copies(ys)

        @pl.when(i == 0)
        def _():
            dcw_acc[...] = jnp.zeros_like(dcw_acc)

        for sl in range(CONV_CH // HEAD_DIM):
            cs = slice(sl * HEAD_DIM, (sl + 1) * HEAD_DIM)
            taps = [cw_ref[j:j + 1, cs] for j in range(CONV_K)]

            def chunk(rc, carry, cs=cs, taps=taps):
                r0 = pl.multiple_of(rc * CONV_ROWS, CONV_ROWS)
                dc = ys[0, pl.ds(r0, CONV_ROWS), cs]
                acc = [jnp.zeros((CONV_ROWS, HEAD_DIM), F32)] * 2
                for j in range(CONV_K):
                    prod = dc * _shifted(xs, CONV_HALO - (CONV_K - 1) + j, r0, cs)
                    dcw_acc[j, :, cs] += jnp.sum(prod.reshape(CONV_ROWS // SUBLANES, SUBLANES, HEAD_DIM), axis=0)
                    acc[j % 2] = acc[j % 2] + taps[j] * _shifted(ys, CONV_K - 1 - j, r0, cs)
                dc1_ref[pl.ds(r0, CONV_ROWS), cs] = acc[0] + acc[1]
                return carry

            lax.fori_loop(0, ROW_BLK // CONV_ROWS, chunk, 0)
        dc1 = dc1_ref[...]
        o_ref[:, :CONV_CH] = (dc1 * s).astype(BF16)
        o_ref[:, CONV_CH:] = (dc1 * a * s * (1.0 - s)).astype(BF16)

        @pl.when(i == n_blk - 1)
        def _():
            dcw_ref[...] = jnp.sum(dcw_acc[...], axis=1)

    return _pcall(
        body, "conv_bwd", (n_blk,), [dc2, dc2, z, z, cw, dz],
        [pl.BlockSpec((ROW_BLK, CONV_CH), lambda i: (i, 0)),
         pl.BlockSpec((CONV_HALO, CONV_CH), lambda i: (jnp.minimum((i + 1) * halo_per_blk, last_halo), 0)),
         pl.BlockSpec((ROW_BLK, 2 * CONV_CH), lambda i: (i, GLU_COL_BLK)),
         pl.BlockSpec((CONV_HALO, 2 * CONV_CH), lambda i: (jnp.maximum(i * halo_per_blk - 1, 0), GLU_COL_BLK)),
         pl.BlockSpec((CONV_HALO, CONV_CH), lambda i: (0, 0)),
         pl.BlockSpec(memory_space=pl.ANY)],
        [jax.ShapeDtypeStruct(dz.shape, BF16), jax.ShapeDtypeStruct((CONV_HALO, CONV_CH), F32)],
        [pl.BlockSpec((ROW_BLK, 2 * CONV_CH), lambda i: (i, GLU_COL_BLK)), pl.BlockSpec((CONV_HALO, CONV_CH), lambda i: (0, 0))],
        [pltpu.VMEM((SUBLANES, ROW_BLK + CONV_HALO, CONV_CH), F32), pltpu.VMEM((SUBLANES, ROW_BLK + CONV_HALO, CONV_CH), F32),
         pltpu.VMEM((ROW_BLK, CONV_CH), F32), pltpu.VMEM((CONV_HALO, SUBLANES, CONV_CH), F32)],
        aliases={5: 0}, comm=comm)


def _epi_mix(ya, c4, wcp, gates, bg):
    yc = _dot(c4, wcp, "nn")
    gv = _sigmoid(gates.astype(F32) + bg)
    merged = gv[:, :D_MODEL] * ya + gv[:, D_MODEL:] * yc
    return merged, ya, yc


def _epi_residual_rms(acc, xres, g):
    x = xres + acc
    return x, _rms_fwd_vals(x, g)


def _cross_scores(cq, ck):
    out = []
    for h in range(CROSS_HEADS):
        cs = slice(h * CROSS_HD, (h + 1) * CROSS_HD)
        s = _dot(cq[:, cs], ck[:, cs], "nt") * (CROSS_HD ** -0.5)
        e = jnp.exp(s - jnp.max(s, axis=-1, keepdims=True))
        out.append((cs, e, jnp.sum(e, axis=-1, keepdims=True)))
    return out


def _epi_cross_fwd(acc, ck, cv):
    cq = acc.astype(BF16)
    co = [_dot(e, cv[:, cs], "nn") / den for cs, e, den in _cross_scores(cq, ck)]
    return cq, jnp.concatenate(co, axis=1)


def _epi_cross_bwd(dco, cq, ck, cv):
    dco = dco.astype(BF16)
    dcq, dck, dcv = [], [], []
    for cs, e, den in _cross_scores(cq, ck):
        p = e / den
        dp = _dot(dco[:, cs], cv[:, cs], "nt")
        ds = (p * (dp - jnp.sum(dp * p, axis=-1, keepdims=True)) * (CROSS_HD ** -0.5)).astype(BF16)
        dcq.append(_dot(ds, ck[:, cs], "nn"))
        dck.append(_dot(ds, cq[:, cs], "tn"))
        dcv.append(_dot(p, dco[:, cs], "tn"))
    return jnp.concatenate(dcq, axis=1), jnp.concatenate(dck, axis=1), jnp.concatenate(dcv, axis=1)


def _epi_mlp_up(acc):
    return acc, jnp.square(jnp.maximum(acc, 0.0))


def _epi_final(acc, x2, tgt, g):
    x3 = x2 + acc
    err = _rms_fwd_vals(x3, g) - tgt
    loss = (0.5 / D_MODEL) * jnp.sum(err * err)
    dx3, dg = _rms_bwd_vals(x3, g, err * (1.0 / D_MODEL))
    return dx3, jnp.full((1, HEAD_DIM), loss, F32), dg


def _epi_mlp_down_bwd(dh, hpre):
    return (dh * 2.0 * jnp.maximum(hpre.astype(F32), 0.0),)


def _epi_rms_bwd(du, x, g, dres):
    dx, dg = _rms_bwd_vals(x, g, du)
    return dres.astype(F32) + dx, dg


def _epi_rms_bwd_g(du, x, g):
    return (_rms_bwd_vals(x, g, du)[1],)


def _epi_mix_bwd(dm, ya, yc, gates, bg):
    gv = _sigmoid(gates.astype(F32) + bg)
    ga, gb = gv[:, :D_MODEL], gv[:, D_MODEL:]
    ya, yc = ya.astype(F32), yc.astype(F32)
    dgate = jnp.concatenate([dm * ya * ga * (1.0 - ga), dm * yc * gb * (1.0 - gb)], axis=1)
    return dm * ga, dm * gb, dgate, jnp.sum(dgate, axis=0, keepdims=True)


def _epi_ln_bwd(dc4, c2, lg, lb):
    mu = jnp.mean(c2, axis=-1, keepdims=True)
    xc = c2 - mu
    rstd = lax.rsqrt(jnp.mean(xc * xc, axis=-1, keepdims=True) + EPS)
    xh = xc * rstd
    c3 = xh * lg + lb
    sg = _sigmoid(c3)
    dc3 = dc4 * sg * (1.0 + c3 * (1.0 - sg))
    dxh = dc3 * lg
    dc2 = rstd * (dxh - jnp.mean(dxh, axis=-1, keepdims=True) - xh * jnp.mean(dxh * xh, axis=-1, keepdims=True))
    return (dc2, jnp.sum(dc3 * xh, axis=0, keepdims=True), jnp.sum(dc3, axis=0, keepdims=True),
            jnp.sum(dc2, axis=0, keepdims=True))


def _sds(shape, dtype):
    return jax.ShapeDtypeStruct(shape, dtype)


class _Lazy:
    def __init__(self, fn):
        self.fn = fn

    def __getitem__(self, key):
        return self.fn(key)


def _local_step(x, mem, tgt, sm, plan):
    w = _Lazy(plan.w)
    dw = {}

    def carry(name, n_own, fn, *args, **kw):
        c = plan.comm(name, dw)
        res = fn(*args, comm=c, **kw)
        plan.done(name, res[n_own:])
        return res[:n_own]

    def mm(name, *args, **kw):
        return carry(name, len(args[6]), _mm, name, *args, **kw)

    seq = x.shape[0]
    nr = seq // ROW_BLK
    big = min(1024, seq)
    nb = seq // big
    row = lambda n: ((ROW_BLK, n), lambda i, j, k: (i, 0))
    vec = lambda n: ((1, n), lambda i, j, k: (0, 0))
    full = lambda r, c: ((r, c), lambda i, j, k: (0, 0))
    gates_blk = ((ROW_BLK, 2 * D_MODEL), lambda i, j, k: (i, GATE_COL_BLK))
    tabs = _rope_tables(seq)

    u = carry("rms_mix", 1, _rms_fwd, "rms_mix", x, sm["g_mix"], ROW_BLK)[0]
    whole3 = lambda a: (a.shape, lambda i, j, k: (0, 0, 0))
    z = plan.project_in(u)
    a_gs = carry("qkv_prep", 3, _qkv_prep, z, tabs)
    os_, ls_ = [], []
    for g in range(N_GROUPS):
        name = "attn_fwd_%d" % g
        o_g, l_g = carry(name, 2, _attn_fwd, name, a_gs[g])
        os_.append(o_g)
        ls_.append(l_g)
    attn, lt = _attn_merge(os_, ls_, seq)
    c2, c4 = carry("conv_fwd", 2, _conv_fwd, z, w["taps"], sm["conv_b"], sm["conv_ln_g"], sm["conv_ln_b"])
    merged, ya, yc = mm(
        "mix", attn, w["w_attn_proj"], "nn", (nr, 1, 1), row(GROUP_W), full(GROUP_W, D_MODEL),
        [(_sds((seq, D_MODEL), BF16), *row(D_MODEL))] * 3,
        extras=[(c4, *row(CONV_CH)), (w["w_conv_proj"], *full(CONV_CH, D_MODEL)), (z, *gates_blk), (sm["b_gate"], *vec(2 * D_MODEL))],
        epi=_epi_mix)
    x1, uq = mm("out_proj", merged, w["w_out"], "nn", (nr, 1, 1), row(D_MODEL), full(D_MODEL, D_MODEL),
                 [(_sds((seq, D_MODEL), F32), *row(D_MODEL)), (_sds((seq, D_MODEL), BF16), *row(D_MODEL))],
                 extras=[(x, *row(D_MODEL)), (sm["g_cross"], *vec(D_MODEL))], epi=_epi_residual_rms)

    mn = _rms_fwd("rms_mem", mem, sm["g_mem"], N_MEM)[0]
    ckv = mm("ckv_proj", mn, w["w_ckv"], "nn", (1, N_DEV, 1), full(N_MEM, D_MODEL),
              ((None, D_MODEL, 2 * D_MODEL // N_DEV), lambda i, j, k: (j, 0, 0)),
              [(_sds((N_MEM, 2 * D_MODEL), BF16), (N_MEM, 2 * D_MODEL // N_DEV), lambda i, j, k: (0, j))])[0]
    ck, cv = ckv[:, :D_MODEL], ckv[:, D_MODEL:]
    kv_blk = full(N_MEM, D_MODEL)
    cq, co = mm("cq_proj_cross", uq, w["w_cq"], "nn", (nr, 1, 1), row(D_MODEL), full(D_MODEL, D_MODEL),
                 [(_sds((seq, D_MODEL), BF16), *row(D_MODEL))] * 2,
                 extras=[(ck, *kv_blk), (cv, *kv_blk)], epi=_epi_cross_fwd)
    x2, um = mm("co_proj", co, w["w_co"], "nn", (nr, 1, 1), row(D_MODEL), full(D_MODEL, D_MODEL),
                 [(_sds((seq, D_MODEL), F32), *row(D_MODEL)), (_sds((seq, D_MODEL), BF16), *row(D_MODEL))],
                 extras=[(x1, *row(D_MODEL)), (sm["g_mlp"], *vec(D_MODEL))], epi=_epi_residual_rms)

    ff_blk = D_FF // N_DEV
    row_f32 = (_sds((seq, D_MODEL), F32), *row(D_MODEL))
    row_bf16 = (_sds((seq, D_MODEL), BF16), *row(D_MODEL))
    col_sum = (_sds((1, D_MODEL), F32), *vec(D_MODEL))
    hpre, h = mm("mlp_up", um, w["w_up"], "nn", (nr, 1, 1), row(D_MODEL), whole3(w["w_up"]),
                 [(_sds((seq, D_FF), BF16), *row(D_FF))] * 2, epi=_epi_mlp_up, split=("cols", N_DEV), b_resident=True)
    kt = D_FF // D_MODEL
    dx3, loss, dg_final = mm(
        "mlp_down_loss", h, w["w_down"], "nn", (nr, 1, 1), row(D_FF), full(D_FF, D_MODEL),
        [row_bf16, (_sds((1, HEAD_DIM), F32), *vec(HEAD_DIM)), col_sum],
        extras=[(x2, *row(D_MODEL)), (tgt, *row(D_MODEL)), (sm["g_final"], *vec(D_MODEL))], epi=_epi_final, acc_outs=(1, 2),
        b_resident=True)

    dhpre = mm("mlp_down_bwd", dx3, w["w_down"], "nt", (nr, 1, 1), row(D_MODEL), full(D_FF, D_MODEL),
               [(_sds((seq, D_FF), BF16), *row(D_FF))], extras=[(hpre, *row(D_FF))], epi=_epi_mlp_down_bwd,
               split=("cols", kt), b_resident=True)[0]
    big2 = min(2 * big, seq)
    nb2 = seq // big2
    dw["w_down"] = mm("dw_down", h, dx3, "tn", (kt, 1, nb2), ((big2, D_MODEL), lambda i, j, k: (k, i)),
                      ((big2, D_MODEL), lambda i, j, k: (k, 0)),
                      [(_sds((D_FF, D_MODEL), BF16), (D_MODEL, D_MODEL), lambda i, j, k: (i, 0))])[0]
    dx2, dg_mlp = mm("mlp_up_bwd", dhpre, w["w_up"], "nt", (nr, 1, 1), row(D_FF), whole3(w["w_up"]),
                     [row_bf16, col_sum],
                     extras=[(x2, *row(D_MODEL)), (sm["g_mlp"], *vec(D_MODEL)), (dx3, *row(D_MODEL))],
                     epi=_epi_rms_bwd, acc_outs=(1,), split=("sum", N_DEV), b_resident=True)
    dw["w_up"] = mm("dw_up", um, dhpre, "tn", (1, N_DEV, nb2), ((big2, D_MODEL), lambda i, j, k: (k, 0)),
                    ((big2, ff_blk), lambda i, j, k: (k, j)),
                    [(_sds((N_DEV, D_MODEL, ff_blk), BF16), (None, D_MODEL, ff_blk), lambda i, j, k: (j, 0, 0))])[0]

    acc_kv = (_sds((N_MEM, D_MODEL), F32), *kv_blk)
    dcq, dck, dcv = mm("co_proj_bwd_cross", dx2, w["w_co"], "nt", (nr, 1, 1), row(D_MODEL), full(D_MODEL, D_MODEL),
                       [row_bf16, acc_kv, acc_kv],
                       extras=[(cq, *row(D_MODEL)), (ck, *kv_blk), (cv, *kv_blk)], epi=_epi_cross_bwd, acc_outs=(1, 2))

    def dw_square(name, act, grad):
        return mm(name, act, grad, "tn", (1, 1, nb2), ((big2, D_MODEL), lambda i, j, k: (k, 0)),
                  ((big2, D_MODEL), lambda i, j, k: (k, 0)), [(_sds((D_MODEL, D_MODEL), BF16), *full(D_MODEL, D_MODEL))])[0]

    dw["w_co"] = dw_square("dw_co", co, dx2)
    token = plan.start("mlp", dw)
    dx1, dg_cross = mm("cq_proj_bwd", dcq, w["w_cq"], "nt", (nr, 1, 1), row(D_MODEL), full(D_MODEL, D_MODEL),
                       [row_bf16, col_sum],
                       extras=[(x1, *row(D_MODEL)), (sm["g_cross"] + token, *vec(D_MODEL)), (dx2, *row(D_MODEL))],
                       epi=_epi_rms_bwd, acc_outs=(1,))
    dw["w_cq"] = dw_square("dw_cq", uq, dcq)
    dckv = jnp.concatenate([dck, dcv], axis=1)
    kv_chunk = 2 * D_MODEL // N_DEV
    dw["w_ckv"] = mm("dw_ckv", mn, dckv, "tn", (1, N_DEV, 1), full(N_MEM, D_MODEL), ((N_MEM, kv_chunk), lambda i, j, k: (0, j)),
                      [(_sds((N_DEV, D_MODEL, kv_chunk), BF16), (None, D_MODEL, kv_chunk), lambda i, j, k: (j, 0, 0))])[0]
    dg_mem = mm("ckv_proj_bwd", dckv, w["w_ckv"], "nt", (1, 1, N_DEV), ((N_MEM, kv_chunk), lambda i, j, k: (0, k)),
                 ((None, D_MODEL, kv_chunk), lambda i, j, k: (k, 0, 0)), [(_sds((1, D_MODEL), F32), *vec(D_MODEL))],
                 extras=[(mem, *full(N_MEM, D_MODEL)), (sm["g_mem"], *vec(D_MODEL))], epi=_epi_rms_bwd_g, acc_outs=(0,))[0]

    token = plan.start("cross", dw)
    dya, dyc, dz, db_gate = mm(
        "out_proj_bwd_mix", dx1, w["w_out"], "nt", (nr, 1, 1), row(D_MODEL), full(D_MODEL, D_MODEL),
        [(_sds((seq, D_MODEL), BF16), *row(D_MODEL)), (_sds((seq, D_MODEL), BF16), *row(D_MODEL)),
         (_sds((seq, IN_W), BF16), *gates_blk), (_sds((1, 2 * D_MODEL), F32), *vec(2 * D_MODEL))],
        extras=[(ya, *row(D_MODEL)), (yc, *row(D_MODEL)), (z, *gates_blk), (sm["b_gate"] + token, *vec(2 * D_MODEL))],
        epi=_epi_mix_bwd, acc_outs=(3,))
    dw["w_out"] = dw_square("dw_out", merged, dx1)
    dattn = mm("attn_proj_bwd", dya, w["w_attn_proj"], "nt", (nr, 1, 1), row(D_MODEL), full(GROUP_W, D_MODEL),
                [(_sds((seq, GROUP_W), BF16), *row(GROUP_W))])[0]
    pc = D_MODEL // N_DEV
    dw["w_attn_proj"] = mm("dw_attn_proj", attn, dya, "tn", (1, 1, nb2), ((big2, GROUP_W), lambda i, j, k: (k, 0)),
                           ((big2, D_MODEL), lambda i, j, k: (k, 0)),
                           [(_sds((N_DEV, GROUP_W, pc), BF16), (N_DEV, GROUP_W, pc), lambda i, j, k: (0, 0, 0))],
                           out_chunks=N_DEV)[0]
    cvec = (_sds((1, CONV_CH), F32), *vec(CONV_CH))
    dc2, dg_ln_g, dg_ln_b, dg_conv_b = mm(
        "conv_proj_bwd_ln", dyc, w["w_conv_proj"], "nt", (nr, 1, 1), row(D_MODEL), full(CONV_CH, D_MODEL),
        [(_sds((seq, CONV_CH), F32), *row(CONV_CH)), cvec, cvec, cvec],
        extras=[(c2, *row(CONV_CH)), (sm["conv_ln_g"], *vec(CONV_CH)), (sm["conv_ln_b"], *vec(CONV_CH))],
        epi=_epi_ln_bwd, acc_outs=(1, 2, 3))
    dw["w_conv_proj"] = mm("dw_conv_proj", c4, dyc, "tn", (1, 1, nb2), ((big2, CONV_CH), lambda i, j, k: (k, 0)),
                           ((big2, D_MODEL), lambda i, j, k: (k, 0)),
                           [(_sds((N_DEV, CONV_CH, pc), BF16), (N_DEV, CONV_CH, pc), lambda i, j, k: (0, 0, 0))],
                           out_chunks=N_DEV)[0]
    token = plan.start("mixers", dw)
    dz, dg_conv_w = carry("conv_bwd", 2, _conv_bwd, dc2, z, w["taps"] + token, dz)
    preps = _attn_bwd_prep(dattn, attn, lt)
    d_gs = []
    for g in range(N_GROUPS):
        name = "attn_bwd_%d" % g
        d_gs.append(carry(name, 1, _attn_bwd, name, a_gs[g], *preps[g])[0])
    dz = _dqkv_post(d_gs, tabs, dz)
    dw["w_in"] = mm("dw_in", u, dz, "tn", (1, N_DEV, nb2), ((big2, D_MODEL), lambda i, j, k: (k, 0)),
                    ((big2, D_MODEL), lambda i, j, k: (k, j)),
                    [(_sds((N_DEV, D_MODEL, D_MODEL), BF16), (None, D_MODEL, D_MODEL), lambda i, j, k: (j, 0, 0))])[0]
    token = plan.start("in", dw)
    grad_x, dg_mix = mm("in_proj_bwd", dz, w["w_in"], "nt", (nr, 1, 1), row(IN_W), whole3(w["w_in"]), [row_f32, col_sum],
                        extras=[(x, *row(D_MODEL)), (sm["g_mix"] + token, *vec(D_MODEL)), (dx1, *row(D_MODEL))],
                        epi=_epi_rms_bwd, acc_outs=(1,), split=("sum", N_DEV), b_resident=True)
    small = dict(g_mix=dg_mix, b_gate=db_gate, conv_b=dg_conv_b, conv_ln_g=dg_ln_g, conv_ln_b=dg_ln_b, g_cross=dg_cross,
                 g_mem=dg_mem, g_mlp=dg_mlp, g_final=dg_final, loss=loss, conv_w=dg_conv_w)
    return grad_x, dw, small


SHARD_SHAPE = dict(w_in=(1024, 1024), w_attn_proj=(512, 128), w_conv_proj=(768, 128), w_out=(128, 1024), w_cq=(128, 1024),
                   w_ckv=(1024, 256), w_co=(128, 1024), w_up=(1024, 512), w_down=(512, 1024))
FWD_CARRY = {"in_proj":("w_attn_proj", "w_conv_proj", "w_out", "w_cq", "w_ckv", "w_co", "taps"),
             "qkv_prep": ("w_up",), "conv_fwd": ("w_down",)}
GRAD_SENDS = {"mlp": ("w_down", "w_up", "w_co"), "cross": ("w_cq", "w_ckv"),
              "mixers": ("w_out", "w_attn_proj", "w_conv_proj"), "in": ("w_in",)}


def _cols_to_2d(a):
    return a.transpose(1, 0, 2).reshape(a.shape[1], -1)


def _in_proj_gather(u, w_shard, comm):
    seq = u.shape[0]
    tm = min(1024, seq)
    x, y, c = lax.axis_index("x"), lax.axis_index("y"), lax.axis_index("c")
    ident = lambda px, py, pc: 4 * px + 2 * py + pc
    far = [(1 - x, y), (x, 1 - y), (1 - x, 1 - y)]
    order = jnp.stack([ident(x, y, c), ident(x, y, 1 - c), ident(*far[0], c), ident(*far[1], c), ident(*far[0], 1 - c),
                       ident(*far[1], 1 - c), ident(*far[2], c), ident(*far[2], 1 - c)]).astype(jnp.int32)
    forward_at = {2: 0, 3: 1, 6: 2}
    n_far = len(far)

    def body(order_ref, u_ref, wsh_ref, *rest):
        c_in, z_ref, wg_ref = rest[:comm.n], rest[comm.n], rest[comm.n + 1]
        c_out = rest[comm.n + 2:2 * comm.n + 2]
        wbuf, load_sem, local_sem, recv_sems, ici_send, d2d_send = rest[2 * comm.n + 2:2 * comm.n + 8]
        sems = rest[2 * comm.n + 8:]
        jj, i = pl.program_id(0), pl.program_id(1)
        (kx, ky, kc), me, chips = _Comm._where()
        sibling = (kx, ky, 1 - kc)
        n = order_ref[jj]

        def push(src, blk, send, to):
            return pltpu.make_async_remote_copy(src_ref=src, dst_ref=wg_ref.at[blk], send_sem=send,
                                                recv_sem=recv_sems.at[blk], device_id=to, device_id_type=MESH)

        def load(src):
            cp = pltpu.make_async_copy(src, wbuf, load_sem)
            cp.start()
            cp.wait()

        @pl.when(jnp.logical_and(jj == 0, i == 0))
        def _():
            push(wsh_ref, me, d2d_send, sibling).start()
            for (px, py) in chips[:2]:
                push(wsh_ref, me, ici_send, (px, py, kc)).start()
            pltpu.make_async_copy(wsh_ref, wg_ref.at[me], local_sem).start()
            load(wsh_ref)

        @pl.when(jnp.logical_and(jj > 0, i == 0))
        def _():
            push(wg_ref.at[n], n, d2d_send, sibling).wait_recv()
            for step, k in forward_at.items():
                @pl.when(jj == step)
                def _(k=k):
                    blk = 4 * chips[k][0] + 2 * chips[k][1] + kc
                    push(wg_ref.at[blk], blk, d2d_send, sibling).start()

            @pl.when(jj == 2)
            def _():
                push(wsh_ref, me, ici_send, (*chips[2], kc)).start()

            if comm.n:
                @pl.when(jj == 3)
                def _():
                    comm.start(c_in, c_out, sems)

            load(wg_ref.at[n])

        z_ref[...] = _dot(u_ref[...], wbuf[...], "nn").astype(BF16)

        @pl.when(jnp.logical_and(jj == N_DEV - 1, i == pl.num_programs(1) - 1))
        def _():
            def drain_sends(send, count):
                blocks = wg_ref.at[pl.ds(0, count)]
                pltpu.make_async_remote_copy(src_ref=blocks, dst_ref=blocks, send_sem=send, recv_sem=recv_sems.at[0],
                                             device_id=sibling, device_id_type=MESH).wait_send()

            drain_sends(ici_send, n_far)
            drain_sends(d2d_send, n_far + 1)
            pltpu.make_async_copy(wsh_ref, wg_ref.at[me], local_sem).wait()
            if comm.n:
                comm.wait(c_in, c_out, sems)

    any_spec = pl.BlockSpec(memory_space=pl.ANY)
    grid_spec = pltpu.PrefetchScalarGridSpec(
        num_scalar_prefetch=1, grid=(N_DEV, seq // tm),
        in_specs=[pl.BlockSpec((tm, D_MODEL), lambda jj, i, order_ref: (i, 0)), any_spec] + comm.in_specs,
        out_specs=[pl.BlockSpec((tm, D_MODEL), lambda jj, i, order_ref: (i, order_ref[jj])), any_spec] + comm.out_specs,
        scratch_shapes=[pltpu.VMEM((D_MODEL, D_MODEL), BF16), pltpu.SemaphoreType.DMA, pltpu.SemaphoreType.DMA,
                        pltpu.SemaphoreType.DMA((N_DEV,)), pltpu.SemaphoreType.DMA, pltpu.SemaphoreType.DMA] + comm.scratch)
    return pl.pallas_call(
        body, name="in_proj_gather", grid_spec=grid_spec,
        out_shape=[jax.ShapeDtypeStruct((seq, IN_W), BF16), jax.ShapeDtypeStruct((N_DEV, D_MODEL, D_MODEL), BF16)] + comm.out_shape,
        compiler_params=_params(dimension_semantics=("arbitrary", "arbitrary")),
    )(order, u, w_shard, *comm.arrays)


_HBM = pl.BlockSpec(memory_space=pltpu.HBM)
_SEM = pl.BlockSpec(memory_space=pltpu.SEMAPHORE)


def _chunks_start(name, dws):
    n = len(dws)

    def body(*refs):
        srcs, lands, sends, recvs, token = refs[:n], refs[n:2 * n], refs[2 * n:3 * n], refs[3 * n:4 * n], refs[-1]
        me, peers = _peers()
        for a in range(n):
            for (px, py, pc) in peers:
                pltpu.make_async_remote_copy(src_ref=srcs[a].at[4 * px + 2 * py + pc], dst_ref=lands[a].at[me],
                                             send_sem=sends[a], recv_sem=recvs[a], device_id=(px, py, pc),
                                             device_id_type=MESH).start()
        token[...] = jnp.zeros_like(token)

    hbm = [pltpu.HBM(d.shape, d.dtype) for d in dws]
    res = pl.pallas_call(
        body, name=name,
        out_shape=[pltpu.SemaphoreType.DMA(())] * (2 * n) + hbm + hbm + [jax.ShapeDtypeStruct((SUBLANES, HEAD_DIM), F32)],
        in_specs=[_HBM] * (2 * n), out_specs=[_SEM] * (2 * n) + [_HBM] * (2 * n) + [pl.BlockSpec(memory_space=pltpu.VMEM)],
        input_output_aliases={a: 2 * n + a for a in range(2 * n)},
        compiler_params=pltpu.CompilerParams(has_side_effects=pltpu.SideEffectType.DATAFLOW_SIDE_EFFECTING),
    )(*[pltpu.with_memory_space_constraint(d, pltpu.HBM) for d in dws],
      *[pltpu.with_memory_space_constraint(lax.empty(d.shape, d.dtype), pltpu.HBM) for d in dws])
    flights = [(res[a], res[n + a], res[2 * n + a], res[3 * n + a]) for a in range(n)]
    return flights, res[-1][0, 0]


def _chunks_wait(name, flights, after):
    n = len(flights)

    def body(*refs):
        lands, sends, recvs = refs[n:2 * n], refs[2 * n:3 * n], refs[3 * n:4 * n]
        for a in range(n):
            seven = lands[a].at[pl.ds(0, N_DEV - 1)]
            cp = pltpu.make_async_remote_copy(src_ref=seven, dst_ref=seven, send_sem=sends[a], recv_sem=recvs[a],
                                              device_id=_peers()[1][0], device_id_type=MESH)
            cp.wait_send()
            cp.wait_recv()

    srcs, lands = [f[2] for f in flights], [f[3] for f in flights]
    res = pl.pallas_call(
        body, name=name, out_shape=[pltpu.HBM(t.shape, t.dtype) for t in srcs + lands],
        in_specs=[_HBM] * (2 * n) + [_SEM] * (2 * n) + [pl.BlockSpec(memory_space=pl.ANY)], out_specs=[_HBM] * (2 * n),
        input_output_aliases={a: a for a in range(2 * n)},
        compiler_params=pltpu.CompilerParams(has_side_effects=pltpu.SideEffectType.DATAFLOW_SIDE_EFFECTING),
    )(*srcs, *lands, *[f[0] for f in flights], *[f[1] for f in flights], after)
    return [(res[a], res[n + a]) for a in range(n)]


def _with_own_chunk(src, land):
    me = _peers()[0]
    own = lax.dynamic_slice(src, (me,) + (0,) * (src.ndim - 1), (1,) + src.shape[1:])
    return lax.dynamic_update_slice(land, own, (me,) + (0,) * (src.ndim - 1))


class _Plan:
    def __init__(self, shards, n_tap_cols):
        self.shards = shards
        self.gathered = {}
        self.parts = {}
        self.in_flight = {}
        self.n_tap_cols = n_tap_cols

    def project_in(self, u):
        comm = self.comm("in_proj", None)
        res = _in_proj_gather(u, self.shards["w_in"], comm)
        self.gathered["w_in"] = res[1]
        self.done("in_proj", res[2:])
        return res[0]

    def start(self, tag, dw):
        keys = GRAD_SENDS[tag]
        flights, token = _chunks_start("grads_start_" + tag, [dw[k].reshape((N_DEV,) + SHARD_SHAPE[k]) for k in keys])
        self.in_flight.update(zip(keys, flights))
        return token

    def finish(self, tag, keys, after):
        done = _chunks_wait("grads_wait_" + tag, [self.in_flight[k] for k in keys], after)
        self.parts.update({k: _with_own_chunk(src, land) for k, (src, land) in zip(keys, done)})

    def comm(self, name, dw):
        if name in FWD_CARRY:
            return _Comm(replicated=[self.shards[k] for k in FWD_CARRY[name]])
        return _NO_COMM

    def done(self, name, got):
        if name in FWD_CARRY:
            self.gathered.update(zip(FWD_CARRY[name], got))

    def w(self, key):
        g = self.gathered[key]
        if key in ("w_in", "w_up", "w_ckv"):
            return g
        if key in ("w_attn_proj", "w_conv_proj"):
            return _cols_to_2d(g)
        if key == "taps":
            return jnp.pad(_cols_to_2d(g[:, :CONV_K, :self.n_tap_cols]), ((0, 1), (0, 0)))
        return g.reshape(-1, g.shape[-1])


def _adamw(name, w, m, v, parts, comm=_NO_COMM):
    rows, cols = w.shape
    n_parts = parts.shape[0]
    rb = rows if rows <= 256 or rows % 256 else 256

    def body(w_ref, m_ref, v_ref, p_ref, g_ref, d_ref, nm_ref, nv_ref):
        g = p_ref[0].astype(F32)
        for q in range(1, n_parts):
            g = g + p_ref[q].astype(F32)
        wv = w_ref[...]
        nm = ADAM_B1 * m_ref[...] + (1.0 - ADAM_B1) * g
        nv = ADAM_B2 * v_ref[...] + (1.0 - ADAM_B2) * jnp.square(g)
        m_hat = nm / (1.0 - ADAM_B1 ** ADAM_STEP)
        v_hat = nv / (1.0 - ADAM_B2 ** ADAM_STEP)
        g_ref[...] = g
        d_ref[...] = -ADAM_LR * (m_hat / (jnp.sqrt(v_hat) + ADAM_EPS) + ADAM_WD * wv)
        nm_ref[...] = nm
        nv_ref[...] = nv

    blk = pl.BlockSpec((rb, cols), lambda i: (i, 0))
    return _pcall(body, name, (rows // rb,), [w, m, v, parts],
                  [blk, blk, blk, pl.BlockSpec((n_parts, rb, cols), lambda i: (0, i, 0))],
                  [jax.ShapeDtypeStruct((rows, cols), F32)] * 4, [blk] * 4, comm=comm)


def _sum_parts(name, parts):
    def body(p_ref, o_ref):
        acc = p_ref[0]
        for q in range(1, parts.shape[0]):
            acc = acc + p_ref[q]
        o_ref[...] = acc

    return _pcall(body, name, (1,), [parts], [pl.BlockSpec(parts.shape, lambda i: (0, 0, 0))],
                  [jax.ShapeDtypeStruct(parts.shape[1:], F32)], [pl.BlockSpec(parts.shape[1:], lambda i: (0, 0))])[0]


BIG = ("w_in", "w_attn_proj", "w_conv_proj", "w_out", "w_cq", "w_ckv", "w_co", "w_up", "w_down")
SMALL = ("g_mix", "b_gate", "conv_b", "conv_ln_g", "conv_ln_b", "g_cross", "g_mem", "g_mlp", "g_final")
SMALL_ORDER = SMALL + ("loss", "conv_w")
WEIGHTS = ("g_mix", "w_in", "b_gate", "conv_w", "conv_b", "conv_ln_g", "conv_ln_b", "w_attn_proj", "w_conv_proj", "w_out",
           "g_cross", "g_mem", "w_cq", "w_ckv", "w_co", "g_mlp", "w_up", "w_down", "g_final")


def kernel(x, mem, g_mix, w_in, b_gate, conv_w, conv_b, conv_ln_g, conv_ln_b, w_attn_proj, w_conv_proj, w_out, g_cross, g_mem, w_cq, w_ckv, w_co, g_mlp, w_up, w_down, g_final, loss_target, m_g_mix, m_w_in, m_b_gate, m_conv_w, m_conv_b, m_conv_ln_g, m_conv_ln_b, m_w_attn_proj, m_w_conv_proj, m_w_out, m_g_cross, m_g_mem, m_w_cq, m_w_ckv, m_w_co, m_g_mlp, m_w_up, m_w_down, m_g_final, v_g_mix, v_w_in, v_b_gate, v_conv_w, v_conv_b, v_conv_ln_g, v_conv_ln_b, v_w_attn_proj, v_w_conv_proj, v_w_out, v_g_cross, v_g_mem, v_w_cq, v_w_ckv, v_w_co, v_g_mlp, v_w_up, v_w_down, v_g_final):
    args = dict(locals())
    wts = {k: args[k] for k in WEIGHTS}
    mom = {k: args["m_" + k] for k in WEIGHTS}
    var = {k: args["v_" + k] for k in WEIGHTS}
    two_d = lambda a: a.reshape(a.shape[-2:]) if a.ndim == 3 else a.reshape(1, -1)

    shards = {k: two_d(wts[k]).astype(BF16) for k in BIG}
    shards["taps"] = jnp.pad(two_d(conv_w), ((0, 1), (0, HEAD_DIM - conv_w.shape[-1])))
    plan = _Plan(shards, conv_w.shape[-1])
    sm = {k: two_d(wts[k]) for k in SMALL}

    grad_x, _, small = _local_step(x[0], mem[0], loss_target[0], sm, plan)
    plan.finish("early", GRAD_SENDS["mlp"] + GRAD_SENDS["cross"] + GRAD_SENDS["mixers"], after=small["g_mix"])
    parts = plan.parts

    out = {}
    small_comm = _Comm(replicated=[small[k] for k in SMALL_ORDER])
    for k in BIG[1:]:
        res = _adamw("adamw_" + k, two_d(wts[k]), two_d(mom[k]), two_d(var[k]), parts[k],
                     comm=small_comm if k == BIG[1] else _NO_COMM)
        out[k] = [r.reshape(wts[k].shape) for r in res[:4]]
        if k == BIG[1]:
            small_parts = dict(zip(SMALL_ORDER, res[4:]))
    for k in SMALL:
        res = _adamw("adamw_" + k, two_d(wts[k]), two_d(mom[k]), two_d(var[k]), small_parts[k])
        out[k] = [r.reshape(wts[k].shape) for r in res]
    plan.finish("in", GRAD_SENDS["in"], after=out["g_final"][3])
    res = _adamw("adamw_w_in", two_d(w_in), two_d(m_w_in), two_d(v_w_in), parts["w_in"])
    out["w_in"] = [r.reshape(w_in.shape) for r in res]
    loss = _sum_parts("loss_sum", small_parts["loss"])[0, 0]
    me = 4 * lax.axis_index("x") + 2 * lax.axis_index("y") + lax.axis_index("c")
    n_tap_cols = conv_w.shape[-1]
    tap_parts = lax.dynamic_slice(small_parts["conv_w"], (0, 0, me * n_tap_cols), (N_DEV, CONV_K, n_tap_cols))
    res = _adamw("adamw_conv_w", two_d(conv_w), two_d(m_conv_w), two_d(v_conv_w), tap_parts)
    out["conv_w"] = [r.reshape(conv_w.shape) for r in res]

    return (loss, grad_x[None], *[out[k][0] for k in WEIGHTS], *[out[k][1] for k in WEIGHTS],
            *[out[k][2] for k in WEIGHTS], *[out[k][3] for k in WEIGHTS])
```

```python
import functools

import jax
import jax.numpy as jnp
import numpy as np
from jax import lax
from jax.experimental import pallas as pl
from jax.experimental.pallas import tpu as pltpu

F32 = jnp.float32
BF16 = jnp.bfloat16

N_DEV = 8
D_MODEL = 1024
N_MEM = 256
HEAD_DIM = 128
HEADS_PER_GROUP = 4
GROUP_W = HEADS_PER_GROUP * HEAD_DIM
DILATIONS = (1, 4, 16)
BAND = 128
N_GROUPS = 3
ATTN_W = N_GROUPS * GROUP_W
QKV_W = 3 * ATTN_W
ROT_DIM = HEAD_DIM // 4
ROPE_THETA = 500000.0
CONV_CH = 768
CONV_K = 31
CONV_HALO = 32
SUBLANES = 8
CONV_ROWS = 64
IN_W = 8192
GLU_COL_BLK = QKV_W // (2 * CONV_CH)
GATE_COL_BLK = (QKV_W + 2 * CONV_CH) // (2 * D_MODEL)
CROSS_HEADS = 4
CROSS_HD = D_MODEL // CROSS_HEADS
D_FF = 4096
EPS = 1e-6
NEG = -1e30
QB = 4
ROW_BLK = QB * BAND

ADAM_LR = 0.001
ADAM_B1 = 0.9
ADAM_B2 = 0.999
ADAM_EPS = 1e-08
ADAM_WD = 0.01
ADAM_STEP = 10

VMEM_LIMIT = 56 * 1024 * 1024
MESH = pl.DeviceIdType.MESH


def _params(**kw):
    return pltpu.CompilerParams(vmem_limit_bytes=VMEM_LIMIT, **kw)


def _sigmoid(x):
    return 1.0 / (1.0 + jnp.exp(-x))


def _dot(a, b, kind):
    dims = {"nn": (((1,), (0,)), ((), ())), "nt": (((1,), (1,)), ((), ())), "tn": (((0,), (0,)), ((), ()))}[kind]
    if a.dtype != BF16:
        a = a.astype(BF16)
    if b.dtype != BF16:
        b = b.astype(BF16)
    return lax.dot_general(a, b, dims, preferred_element_type=F32)


def _peers():
    x, y, c = lax.axis_index("x"), lax.axis_index("y"), lax.axis_index("c")
    me = 4 * x + 2 * y + c
    peers = [(x, y, 1 - c), (1 - x, y, c), (x, 1 - y, c), (1 - x, 1 - y, c),
             (1 - x, y, 1 - c), (x, 1 - y, 1 - c), (1 - x, 1 - y, 1 - c)]
    return me, peers


class _Comm:
    def __init__(self, chunked=(), replicated=()):
        self.arrays = list(chunked) + list(replicated)
        self.n_c = len(chunked)
        self.n = len(self.arrays)
        self.out_shape = [jax.ShapeDtypeStruct(a.shape, a.dtype) for a in chunked]
        self.out_shape += [jax.ShapeDtypeStruct((N_DEV,) + a.shape, a.dtype) for a in replicated]
        self.in_specs = [pl.BlockSpec(memory_space=pl.ANY)] * self.n
        self.out_specs = [pl.BlockSpec(memory_space=pl.ANY)] * self.n
        self.scratch = [pltpu.SemaphoreType.DMA((self.n,))] * 5 if self.n else []

    @staticmethod
    def _where():
        x, y, c = lax.axis_index("x"), lax.axis_index("y"), lax.axis_index("c")
        chips = [(1 - x, y), (x, 1 - y), (1 - x, 1 - y)]
        return (x, y, c), 4 * x + 2 * y + c, chips

    def _local(self, ins, outs, sems, a, me):
        src = ins[a].at[me] if a < self.n_c else ins[a]
        return pltpu.make_async_copy(src, outs[a].at[me], sems[2].at[a])

    @staticmethod
    def _remote(src, dst, send, recv, to):
        return pltpu.make_async_remote_copy(src_ref=src, dst_ref=dst, send_sem=send, recv_sem=recv, device_id=to,
                                            device_id_type=MESH)

    def start(self, ins, outs, sems):
        (x, y, c), me, chips = self._where()
        for a in range(self.n):
            self._local(ins, outs, sems, a, me).start()
            if a < self.n_c:
                for (px, py, pc) in _peers()[1]:
                    self._remote(ins[a].at[4 * px + 2 * py + pc], outs[a].at[me], sems[0].at[a], sems[1].at[a], (px, py, pc)).start()
            else:
                self._remote(ins[a], outs[a].at[me], sems[3].at[a], sems[4].at[a], (x, y, 1 - c)).start()
                for (px, py) in chips:
                    self._remote(ins[a], outs[a].at[me], sems[0].at[a], sems[1].at[a], (px, py, c)).start()

    def wait(self, ins, outs, sems):
        (x, y, c), me, chips = self._where()
        sibling = (x, y, 1 - c)

        def drain(a, pair, count):
            blocks = outs[a].at[pl.ds(0, count)]
            cp = self._remote(blocks, blocks, sems[pair].at[a], sems[pair + 1].at[a], sibling)
            cp.wait_send()
            cp.wait_recv()

        for a in range(self.n):
            if a < self.n_c:
                drain(a, 0, N_DEV - 1)
            else:
                drain(a, 0, len(chips))
                for (px, py) in chips:
                    blk = outs[a].at[4 * px + 2 * py + c]
                    self._remote(blk, blk, sems[3].at[a], sems[4].at[a], sibling).start()
        for a in range(self.n):
            if a >= self.n_c:
                drain(a, 3, len(chips) + 1)
            self._local(ins, outs, sems, a, me).wait()


_NO_COMM = _Comm()


def _pcall(body, name, grid, operands, in_specs, out_shape, out_specs, scratch=(), aliases=None, comm=_NO_COMM, **params):
    n_in, n_out, n_scr = len(operands), len(out_shape), len(scratch)
    grid = tuple(grid)

    def carried(*refs):
        ins, c_in = refs[:n_in], refs[n_in:n_in + comm.n]
        o0 = n_in + comm.n
        outs, c_out = refs[o0:o0 + n_out], refs[o0 + n_out:o0 + n_out + comm.n]
        s0 = o0 + n_out + comm.n
        scr, sems = refs[s0:s0 + n_scr], refs[s0 + n_scr:]
        ids = [pl.program_id(ax) for ax in range(len(grid))]

        @pl.when(functools.reduce(jnp.logical_and, [p == 0 for p in ids]))
        def _():
            comm.start(c_in, c_out, sems)

        body(*ins, *outs, *scr)

        @pl.when(functools.reduce(jnp.logical_and, [p == g - 1 for p, g in zip(ids, grid)]))
        def _():
            comm.wait(c_in, c_out, sems)

    return pl.pallas_call(
        carried if comm.n else body, name=name, grid=grid, in_specs=list(in_specs) + comm.in_specs,
        out_shape=list(out_shape) + comm.out_shape, out_specs=list(out_specs) + comm.out_specs,
        scratch_shapes=list(scratch) + comm.scratch, input_output_aliases=aliases or {},
        compiler_params=_params(dimension_semantics=("arbitrary",) * len(grid), **params),
    )(*operands, *comm.arrays)


def _mm(name, a, b, kind, grid, a_blk, b_blk, outs, extras=(), epi=None, acc_outs=(), j_outer=False, comm=_NO_COMM,
        split=None, b_resident=False, out_chunks=0):
    gi, gj, gk = grid
    n_ex = len(extras)
    n_out = len(outs)
    mode, n_chunks = split if split is not None else (None, 1)

    def spec(blk, fn, **kw):
        return pl.BlockSpec(blk, (lambda j, i, k: fn(i, j, k)) if j_outer else fn, **kw)

    def b_chunk(b_ref, c):
        if len(b_ref.shape) == 3:
            return b_ref[c]
        rows, cols = b_ref.shape
        if (kind == "nn") == (mode == "cols"):
            return b_ref[:, c * (cols // n_chunks):(c + 1) * (cols // n_chunks)]
        return b_ref[c * (rows // n_chunks):(c + 1) * (rows // n_chunks), :]

    def col_chunk(ref, c):
        width = ref.shape[-1] // n_chunks
        return slice(c * width, (c + 1) * width)

    def body(*refs):
        a_ref, b_ref = refs[0], refs[1]
        ex = refs[2:2 + n_ex]
        out_refs = refs[2 + n_ex:2 + n_ex + n_out]
        acc_ref = refs[2 + n_ex + n_out] if gk > 1 else None
        i = pl.program_id(1 if j_outer else 0)
        k = pl.program_id(2)
        if mode == "cols":
            a_val = a_ref[...]
            for c in range(n_chunks):
                acc = _dot(a_val, b_chunk(b_ref, c), kind)
                vals = epi(acc, *[e[:, col_chunk(e, c)] for e in ex]) if epi is not None else (acc,)
                for o, v in zip(out_refs, vals):
                    o[:, col_chunk(o, c)] = v.astype(o.dtype)
            return
        if mode == "sum":
            part = _dot(a_ref[:, col_chunk(a_ref, 0)], b_chunk(b_ref, 0), kind)
            for c in range(1, n_chunks):
                part = part + _dot(a_ref[:, col_chunk(a_ref, c)], b_chunk(b_ref, c), kind)
        else:
            part = _dot(a_ref[...], b_ref[...], kind)

        def finish(acc):
            if out_chunks:
                width = acc.shape[-1] // out_chunks
                for c in range(out_chunks):
                    out_refs[0][c] = acc[:, c * width:(c + 1) * width].astype(out_refs[0].dtype)
                return
            vals = epi(acc, *[e[...] for e in ex]) if epi is not None else (acc,)
            for idx, (o, v) in enumerate(zip(out_refs, vals)):
                if idx in acc_outs:
                    @pl.when(i == 0)
                    def _():
                        o[...] = v.astype(o.dtype)

                    @pl.when(i != 0)
                    def _():
                        o[...] += v.astype(o.dtype)
                else:
                    o[...] = v.astype(o.dtype)

        if gk == 1:
            finish(part)
        else:
            @pl.when(k == 0)
            def _():
                acc_ref[...] = part

            @pl.when(k != 0)
            def _():
                acc_ref[...] += part

            @pl.when(k == gk - 1)
            def _():
                finish(acc_ref[...])

    scratch = []
    if gk > 1:
        tm = a_blk[0][-1] if kind == "tn" else a_blk[0][-2]
        tn = b_blk[0][-2] if kind == "nt" else b_blk[0][-1]
        scratch = [pltpu.VMEM((tm, tn), F32)]
    b_kw = dict(pipeline_mode=pl.Buffered(1)) if b_resident else {}
    return _pcall(body, name, (gj, gi, gk) if j_outer else (gi, gj, gk), [a, b] + [e for e, _, _ in extras],
                  [spec(*a_blk), spec(*b_blk, **b_kw)] + [spec(blk, fn) for _, blk, fn in extras],
                  [s for s, _, _ in outs], [spec(blk, fn) for _, blk, fn in outs], scratch, comm=comm)


def _rms_fwd_vals(x, g):
    r = lax.rsqrt(jnp.mean(x * x, axis=-1, keepdims=True) + EPS)
    return x * r * g


def _rms_bwd_vals(x, g, du):
    r = lax.rsqrt(jnp.mean(x * x, axis=-1, keepdims=True) + EPS)
    xh = x * r
    dxh = du * g
    dx = r * (dxh - xh * jnp.mean(dxh * xh, axis=-1, keepdims=True))
    return dx, jnp.sum(du * xh, axis=0, keepdims=True)


def _rms_fwd(name, x, g, rows, comm=_NO_COMM):
    n = x.shape[0]

    def body(x_ref, g_ref, o_ref):
        o_ref[...] = _rms_fwd_vals(x_ref[...], g_ref[...]).astype(BF16)

    return _pcall(body, name, (n // rows,), [x, g],
                  [pl.BlockSpec((rows, D_MODEL), lambda i: (i, 0)), pl.BlockSpec((1, D_MODEL), lambda i: (0, 0))],
                  [jax.ShapeDtypeStruct(x.shape, BF16)], [pl.BlockSpec((rows, D_MODEL), lambda i: (i, 0))], comm=comm)


def _rope_tables(seq):
    half = ROT_DIM // 2
    pos = np.arange(seq, dtype=np.float32)
    inv_freq = np.float32(ROPE_THETA) ** (-np.arange(0, ROT_DIM, 2, dtype=np.float32) / np.float32(ROT_DIM))
    ang = (pos[:, None] * inv_freq[None, :]).astype(np.float32)
    cos, sin = np.cos(ang), np.sin(ang)
    rest = HEAD_DIM - ROT_DIM
    c = np.concatenate([cos, cos, np.ones((seq, rest), np.float32)], axis=1)
    s1 = np.concatenate([np.zeros((seq, half), np.float32), sin, np.zeros((seq, rest), np.float32)], axis=1)
    s2 = np.concatenate([-sin, np.zeros((seq, half + rest), np.float32)], axis=1)
    return jnp.asarray(c), jnp.asarray(s1), jnp.asarray(s2)


def _group_shapes(seq, width, dtype):
    return [jax.ShapeDtypeStruct((d, seq // d, width), dtype) for d in DILATIONS]


def _group_specs(width):
    return [pl.BlockSpec((d, ROW_BLK // d, width), lambda i: (0, i, 0)) for d in DILATIONS]


def _qkv_prep(z, tabs, comm=_NO_COMM):
    seq = z.shape[0]

    def body(z_ref, c_ref, s1_ref, s2_ref, a0, a1, a2, sc):
        outs = (a0, a1, a2)
        c, s1, s2 = c_ref[...], s1_ref[...], s2_ref[...]
        for part in range(3):
            for hh in range(N_GROUPS * HEADS_PER_GROUP):
                g, hl = divmod(hh, HEADS_PER_GROUP)
                col = part * ATTN_W + hh * HEAD_DIM
                ocol = part * GROUP_W + hl * HEAD_DIM
                x = z_ref[:, col:col + HEAD_DIM].astype(F32)
                if part < 2:
                    x = x * c + pltpu.roll(x, ROT_DIM // 2, 1) * s1 + pltpu.roll(x, HEAD_DIM - ROT_DIM // 2, 1) * s2
                d = DILATIONS[g]
                if d == 1:
                    outs[g][0, :, ocol:ocol + HEAD_DIM] = x.astype(BF16)
                else:
                    sc[...] = x
                    for r in range(d):
                        outs[g][r, :, ocol:ocol + HEAD_DIM] = sc[pl.ds(r, ROW_BLK // d, stride=d), :].astype(BF16)

    tab_spec = pl.BlockSpec((ROW_BLK, HEAD_DIM), lambda i: (i, 0))
    return _pcall(body, "qkv_prep", (seq // ROW_BLK,), [z, *tabs],
                  [pl.BlockSpec((ROW_BLK, QKV_W), lambda i: (i, 0)), tab_spec, tab_spec, tab_spec],
                  _group_shapes(seq, ATTN_W, BF16), _group_specs(ATTN_W), [pltpu.VMEM((ROW_BLK, HEAD_DIM), F32)], comm=comm)


def _band_masks_2(t):
    qi = lax.broadcasted_iota(jnp.int32, (BAND, 2 * BAND), 0)
    kj = lax.broadcasted_iota(jnp.int32, (BAND, 2 * BAND), 1)
    band = jnp.logical_and(kj >= qi, kj <= qi + BAND)
    return band, jnp.logical_and(band, jnp.logical_or(kj >= BAND, t > 0))


def _attn_fwd(name, a_g, comm=_NO_COMM):
    dil, m_len, _ = a_g.shape
    qb = min(QB, m_len // BAND)
    rows = qb * BAND
    steps = m_len // rows
    scale = HEAD_DIM ** -0.5

    tiles = [(sb, h) for sb in range(qb) for h in range(HEADS_PER_GROUP)]

    def body(q_ref, kc_ref, vc_ref, kp_ref, vp_ref, o_ref, l_ref, k_all, v_all, s_scr, p_scr, r_scr):
        t = pl.program_id(1)
        k_all[0:BAND, :] = kp_ref[...]
        k_all[BAND:, :] = kc_ref[...]
        v_all[0:BAND, :] = vp_ref[...]
        v_all[BAND:, :] = vc_ref[...]
        band, band_first = _band_masks_2(t)
        for idx, (sb, h) in enumerate(tiles):
            cs = slice(h * HEAD_DIM, (h + 1) * HEAD_DIM)
            s = _dot(q_ref[sb * BAND:(sb + 1) * BAND, cs], k_all[sb * BAND:(sb + 2) * BAND, cs], "nt") * scale
            s_scr[idx] = jnp.where(band_first if sb == 0 else band, s, NEG)
        lane = lax.broadcasted_iota(jnp.int32, (BAND, HEAD_DIM), 1)
        lse_rows = [jnp.zeros((BAND, HEAD_DIM), F32)] * qb
        for idx, (sb, h) in enumerate(tiles):
            s = s_scr[idx]
            mx = jnp.max(s, axis=-1, keepdims=True)
            p = jnp.exp(s - mx)
            den = jnp.sum(p, axis=-1, keepdims=True)
            p_scr[idx] = p.astype(BF16)
            r_scr[idx] = jnp.broadcast_to(1.0 / den, (BAND, HEAD_DIM))
            lse_rows[sb] = jnp.where(lane == h, jnp.broadcast_to(mx + jnp.log(den), (BAND, HEAD_DIM)), lse_rows[sb])
        for sb in range(qb):
            l_ref[sb * BAND:(sb + 1) * BAND, :] = lse_rows[sb]
        for idx, (sb, h) in enumerate(tiles):
            cs = slice(h * HEAD_DIM, (h + 1) * HEAD_DIM)
            o_ref[sb * BAND:(sb + 1) * BAND, cs] = _dot(p_scr[idx], v_all[sb * BAND:(sb + 2) * BAND, cs], "nn") * r_scr[idx]

    def prev(r, t):
        return jnp.maximum(qb * t - 1, 0)

    cur = lambda c: pl.BlockSpec((None, rows, GROUP_W), lambda r, t, c=c: (r, t, c))
    prv = lambda c: pl.BlockSpec((None, BAND, GROUP_W), lambda r, t, c=c: (r, prev(r, t), c))
    out_spec = lambda width: pl.BlockSpec((None, rows, width), lambda r, t: (r, t, 0))
    shp = lambda width: jax.ShapeDtypeStruct((dil, m_len, width), F32)
    n_t = len(tiles)
    return _pcall(body, name, (dil, steps), [a_g] * 5, [cur(0), cur(1), cur(2), prv(1), prv(2)],
                  [shp(GROUP_W), shp(HEAD_DIM)], [out_spec(GROUP_W), out_spec(HEAD_DIM)],
                  [pltpu.VMEM((rows + BAND, GROUP_W), BF16), pltpu.VMEM((rows + BAND, GROUP_W), BF16),
                   pltpu.VMEM((n_t, BAND, 2 * BAND), F32), pltpu.VMEM((n_t, BAND, 2 * BAND), BF16),
                   pltpu.VMEM((n_t, BAND, HEAD_DIM), F32)], comm=comm)


def _attn_merge(os_, ls_, seq):
    def body(o0, l0, o1, l1, o2, l2, at_ref, lt_ref, sc, lsc):
        for gi, l_r in enumerate((l1, l2)):
            d = DILATIONS[gi + 1]
            for r in range(d):
                lsc.at[gi][pl.ds(r, ROW_BLK // d, stride=d), :] = l_r[r]
        lse = (l0.at[0], lsc.at[0], lsc.at[1])
        lane = lax.broadcasted_iota(jnp.int32, (ROW_BLK, HEAD_DIM), 1)
        lt_rows = jnp.zeros((ROW_BLK, HEAD_DIM), F32)
        for h in range(HEADS_PER_GROUP):
            cs = slice(h * HEAD_DIM, (h + 1) * HEAD_DIM)
            for gi, o_r in enumerate((o1, o2)):
                d = DILATIONS[gi + 1]
                for r in range(d):
                    sc.at[gi][pl.ds(r, ROW_BLK // d, stride=d), :] = o_r[r, :, cs]
            l_h = [v[:, h:h + 1] for v in lse]
            mx = jnp.maximum(jnp.maximum(l_h[0], l_h[1]), l_h[2])
            e = [jnp.exp(v - mx) for v in l_h]
            tot = e[0] + e[1] + e[2]
            inv = 1.0 / tot
            at_ref[:, cs] = ((e[0] * inv) * o0[0, :, cs] + (e[1] * inv) * sc[0] + (e[2] * inv) * sc[1]).astype(BF16)
            lt_rows = jnp.where(lane == h, jnp.broadcast_to(mx + jnp.log(tot), (ROW_BLK, HEAD_DIM)), lt_rows)
        lt_ref[...] = lt_rows

    go, gl = _group_specs(GROUP_W), _group_specs(HEAD_DIM)
    return pl.pallas_call(
        body, name="attn_merge",
        out_shape=[jax.ShapeDtypeStruct((seq, GROUP_W), BF16), jax.ShapeDtypeStruct((seq, HEAD_DIM), F32)],
        grid=(seq // ROW_BLK,), in_specs=[go[0], gl[0], go[1], gl[1], go[2], gl[2]],
        out_specs=[pl.BlockSpec((ROW_BLK, GROUP_W), lambda i: (i, 0)), pl.BlockSpec((ROW_BLK, HEAD_DIM), lambda i: (i, 0))],
        scratch_shapes=[pltpu.VMEM((2, ROW_BLK, HEAD_DIM), F32), pltpu.VMEM((2, ROW_BLK, HEAD_DIM), F32)],
        compiler_params=_params(dimension_semantics=("arbitrary",)),
    )(os_[0], ls_[0], os_[1], ls_[1], os_[2], ls_[2])


def _attn_bwd_prep(dattn, attn, lt):
    seq = dattn.shape[0]

    def body(da_ref, at_ref, lt_ref, cl0, d1, cl1, d2, cl2, sc, csc):
        lane = lax.broadcasted_iota(jnp.int32, (ROW_BLK, HEAD_DIM), 1)
        cl = pltpu.roll(lt_ref[...], HEADS_PER_GROUP, 1)
        for h in range(HEADS_PER_GROUP):
            cs = slice(h * HEAD_DIM, (h + 1) * HEAD_DIM)
            da = da_ref[:, cs].astype(F32)
            cc = jnp.sum(da * at_ref[:, cs].astype(F32), axis=-1, keepdims=True)
            cl = jnp.where(lane == h, jnp.broadcast_to(cc, (ROW_BLK, HEAD_DIM)), cl)
            sc[...] = da
            for g, d_ref in ((1, d1), (2, d2)):
                d = DILATIONS[g]
                for r in range(d):
                    d_ref[r, :, cs] = sc[pl.ds(r, ROW_BLK // d, stride=d), :].astype(BF16)
        cl0[0] = cl
        csc[...] = cl
        for g, c_ref in ((1, cl1), (2, cl2)):
            d = DILATIONS[g]
            for r in range(d):
                c_ref[r] = csc[pl.ds(r, ROW_BLK // d, stride=d), :]

    go, gl = _group_specs(GROUP_W), _group_specs(HEAD_DIM)
    row = lambda width: pl.BlockSpec((ROW_BLK, width), lambda i: (i, 0))
    shape = lambda g, width, dt: jax.ShapeDtypeStruct((DILATIONS[g], seq // DILATIONS[g], width), dt)
    cl0, d1, cl1, d2, cl2 = pl.pallas_call(
        body, name="attn_bwd_prep",
        out_shape=[shape(0, HEAD_DIM, F32), shape(1, GROUP_W, BF16), shape(1, HEAD_DIM, F32), shape(2, GROUP_W, BF16),
                   shape(2, HEAD_DIM, F32)],
        grid=(seq // ROW_BLK,), in_specs=[row(GROUP_W), row(GROUP_W), row(HEAD_DIM)],
        out_specs=[gl[0], go[1], gl[1], go[2], gl[2]],
        scratch_shapes=[pltpu.VMEM((ROW_BLK, HEAD_DIM), F32), pltpu.VMEM((ROW_BLK, HEAD_DIM), F32)],
        compiler_params=_params(dimension_semantics=("arbitrary",)),
    )(dattn, attn, lt)
    return [(dattn[None], cl0), (d1, cl1), (d2, cl2)]


def _attn_bwd(name, a_g, da_g, cl_g, comm=_NO_COMM):
    dil, m_len, _ = a_g.shape
    qb = min(QB, m_len // BAND)
    rows = qb * BAND
    steps = m_len // rows
    scale = HEAD_DIM ** -0.5

    tiles = [(sb, h) for sb in range(qb) for h in range(HEADS_PER_GROUP)]

    def body(q_ref, kc_ref, vc_ref, kp_ref, vp_ref, da_ref, cl_ref, d_ref, dk_acc, dv_acc, car_k, car_v,
             k_all, v_all, s_scr, dp_scr, p_scr, ds_scr):
        tg = pl.program_id(1)
        t = steps - 1 - tg

        @pl.when(tg == 0)
        def _():
            car_k[...] = jnp.zeros_like(car_k)
            car_v[...] = jnp.zeros_like(car_v)

        k_all[0:BAND, :] = kp_ref[...]
        k_all[BAND:, :] = kc_ref[...]
        v_all[0:BAND, :] = vp_ref[...]
        v_all[BAND:, :] = vc_ref[...]
        zero = jnp.zeros((rows, GROUP_W), F32)
        dk_acc[0:rows, :] = zero
        dv_acc[0:rows, :] = zero
        dk_acc[rows:rows + BAND, :] = car_k[...]
        dv_acc[rows:rows + BAND, :] = car_v[...]
        band, band_first = _band_masks_2(t)
        for idx, (sb, h) in enumerate(tiles):
            cs = slice(h * HEAD_DIM, (h + 1) * HEAD_DIM)
            rs, ks = slice(sb * BAND, (sb + 1) * BAND), slice(sb * BAND, (sb + 2) * BAND)
            s_scr[idx] = _dot(q_ref[rs, cs], k_all[ks, cs], "nt")
            dp_scr[idx] = _dot(da_ref[rs, cs], v_all[ks, cs], "nt")
        for idx, (sb, h) in enumerate(tiles):
            cs = slice(h * HEAD_DIM, (h + 1) * HEAD_DIM)
            rs = slice(sb * BAND, (sb + 1) * BAND)
            cc = jnp.broadcast_to(cl_ref[rs, h:h + 1], (BAND, 2 * BAND))
            ltv = jnp.broadcast_to(cl_ref[rs, HEADS_PER_GROUP + h:HEADS_PER_GROUP + h + 1], (BAND, 2 * BAND))
            p = jnp.exp(jnp.where(band_first if sb == 0 else band, s_scr[idx] * scale - ltv, NEG))
            p_scr[idx] = p.astype(BF16)
            ds_scr[idx] = (p * (dp_scr[idx] - cc) * scale).astype(BF16)
        for idx, (sb, h) in enumerate(tiles):
            cs = slice(h * HEAD_DIM, (h + 1) * HEAD_DIM)
            rs, ks = slice(sb * BAND, (sb + 1) * BAND), slice(sb * BAND, (sb + 2) * BAND)
            d_ref[rs, cs] = _dot(ds_scr[idx], k_all[ks, cs], "nn").astype(BF16)
            dk_acc[ks, cs] += _dot(ds_scr[idx], q_ref[rs, cs], "tn")
            dv_acc[ks, cs] += _dot(p_scr[idx], da_ref[rs, cs], "tn")
        d_ref[:, GROUP_W:2 * GROUP_W] = dk_acc[BAND:rows + BAND, :].astype(BF16)
        d_ref[:, 2 * GROUP_W:3 * GROUP_W] = dv_acc[BAND:rows + BAND, :].astype(BF16)
        car_k[...] = dk_acc[0:BAND, :]
        car_v[...] = dv_acc[0:BAND, :]

    def rev(tg):
        return steps - 1 - tg

    def prev(tg):
        return jnp.maximum(qb * rev(tg) - 1, 0)

    cur = lambda c: pl.BlockSpec((None, rows, GROUP_W), lambda r, tg, c=c: (r, rev(tg), c))
    prv = lambda c: pl.BlockSpec((None, BAND, GROUP_W), lambda r, tg, c=c: (r, prev(tg), c))
    return _pcall(
        body, name, (dil, steps), [a_g, a_g, a_g, a_g, a_g, da_g, cl_g],
        [cur(0), cur(1), cur(2), prv(1), prv(2), cur(0), pl.BlockSpec((None, rows, HEAD_DIM), lambda r, tg: (r, rev(tg), 0))],
        [jax.ShapeDtypeStruct((dil, m_len, ATTN_W), BF16)], [pl.BlockSpec((None, rows, ATTN_W), lambda r, tg: (r, rev(tg), 0))],
        [pltpu.VMEM((rows + BAND, GROUP_W), F32), pltpu.VMEM((rows + BAND, GROUP_W), F32),
         pltpu.VMEM((BAND, GROUP_W), F32), pltpu.VMEM((BAND, GROUP_W), F32),
         pltpu.VMEM((rows + BAND, GROUP_W), BF16), pltpu.VMEM((rows + BAND, GROUP_W), BF16),
         pltpu.VMEM((len(tiles), BAND, 2 * BAND), F32), pltpu.VMEM((len(tiles), BAND, 2 * BAND), F32),
         pltpu.VMEM((len(tiles), BAND, 2 * BAND), BF16), pltpu.VMEM((len(tiles), BAND, 2 * BAND), BF16)], comm=comm)


def _dqkv_post(d_gs, tabs, dz):
    seq = dz.shape[0]

    def body(g0, g1, g2, c_ref, s1_ref, s2_ref, dz_any, o_ref, sc):
        del dz_any
        ins = (g0, g1, g2)
        c, s1, s2 = c_ref[...], s1_ref[...], s2_ref[...]
        for part in range(3):
            for hh in range(N_GROUPS * HEADS_PER_GROUP):
                g, hl = divmod(hh, HEADS_PER_GROUP)
                icol = part * GROUP_W + hl * HEAD_DIM
                ocol = part * ATTN_W + hh * HEAD_DIM
                d = DILATIONS[g]
                if d == 1:
                    x = ins[g][0, :, icol:icol + HEAD_DIM].astype(F32)
                else:
                    for r in range(d):
                        sc[pl.ds(r, ROW_BLK // d, stride=d), :] = ins[g][r, :, icol:icol + HEAD_DIM].astype(F32)
                    x = sc[...]
                if part < 2:
                    x = x * c + pltpu.roll(x * s1, HEAD_DIM - ROT_DIM // 2, 1) + pltpu.roll(x * s2, ROT_DIM // 2, 1)
                o_ref[:, ocol:ocol + HEAD_DIM] = x.astype(BF16)

    tab_spec = pl.BlockSpec((ROW_BLK, HEAD_DIM), lambda i: (i, 0))
    return pl.pallas_call(
        body, name="dqkv_post", out_shape=jax.ShapeDtypeStruct(dz.shape, BF16), grid=(seq // ROW_BLK,),
        in_specs=_group_specs(ATTN_W) + [tab_spec, tab_spec, tab_spec, pl.BlockSpec(memory_space=pl.ANY)],
        out_specs=pl.BlockSpec((ROW_BLK, QKV_W), lambda i: (i, 0)),
        scratch_shapes=[pltpu.VMEM((ROW_BLK, HEAD_DIM), F32)], input_output_aliases={6: 0},
        compiler_params=_params(dimension_semantics=("arbitrary",)),
    )(*d_gs, *tabs, dz)


def _glu(zg):
    a = zg[:, :CONV_CH].astype(F32)
    s = _sigmoid(zg[:, CONV_CH:].astype(F32))
    return a, s, a * s


def _shifted_copies(xs):
    n = xs.shape[1] - SUBLANES
    for b in range(1, SUBLANES):
        xs[b, 0:n, :] = xs[0, pl.ds(b, n), :]


def _shifted(xs, offset, r0, cs):
    a, b = divmod(offset, SUBLANES)
    return xs[b, pl.ds(SUBLANES * a + r0, CONV_ROWS), cs]


def _conv_fwd(z, cw, cb, lg, lb, comm=_NO_COMM):
    seq = z.shape[0]
    halo_per_blk = ROW_BLK // CONV_HALO

    def body(zg_ref, zh_ref, cw_ref, cb_ref, lg_ref, lb_ref, c2_ref, c4_ref, xs):
        i = pl.program_id(0)
        _, _, c1 = _glu(zg_ref[...])
        _, _, c1h = _glu(zh_ref[...])
        xs[0, 0:CONV_HALO, :] = jnp.where(i > 0, c1h, 0.0)
        xs[0, CONV_HALO:, :] = c1
        _shifted_copies(xs)
        for s in range(CONV_CH // HEAD_DIM):
            cs = slice(s * HEAD_DIM, (s + 1) * HEAD_DIM)
            taps = [cw_ref[j:j + 1, cs] for j in range(CONV_K)]
            bias = cb_ref[:, cs]

            def chunk(rc, carry, cs=cs, taps=taps, bias=bias):
                r0 = pl.multiple_of(rc * CONV_ROWS, CONV_ROWS)
                acc = [jnp.zeros((CONV_ROWS, HEAD_DIM), F32)] * 2
                for j in range(CONV_K):
                    acc[j % 2] = acc[j % 2] + taps[j] * _shifted(xs, CONV_HALO - (CONV_K - 1) + j, r0, cs)
                c2_ref[pl.ds(r0, CONV_ROWS), cs] = acc[0] + acc[1] + bias
                return carry

            lax.fori_loop(0, ROW_BLK // CONV_ROWS, chunk, 0)
        c2 = c2_ref[...]
        mu = jnp.mean(c2, axis=-1, keepdims=True)
        xc = c2 - mu
        rstd = lax.rsqrt(jnp.mean(xc * xc, axis=-1, keepdims=True) + EPS)
        c3 = xc * rstd * lg_ref[...] + lb_ref[...]
        c4_ref[...] = (c3 * _sigmoid(c3)).astype(BF16)

    vec = pl.BlockSpec((1, CONV_CH), lambda i: (0, 0))
    return _pcall(
        body, "conv_fwd", (seq // ROW_BLK,), [z, z, cw, cb, lg, lb],
        [pl.BlockSpec((ROW_BLK, 2 * CONV_CH), lambda i: (i, GLU_COL_BLK)),
         pl.BlockSpec((CONV_HALO, 2 * CONV_CH), lambda i: (jnp.maximum(i * halo_per_blk - 1, 0), GLU_COL_BLK)),
         pl.BlockSpec((CONV_HALO, CONV_CH), lambda i: (0, 0)), vec, vec, vec],
        [jax.ShapeDtypeStruct((seq, CONV_CH), F32), jax.ShapeDtypeStruct((seq, CONV_CH), BF16)],
        [pl.BlockSpec((ROW_BLK, CONV_CH), lambda i: (i, 0)), pl.BlockSpec((ROW_BLK, CONV_CH), lambda i: (i, 0))],
        [pltpu.VMEM((SUBLANES, ROW_BLK + CONV_HALO, CONV_CH), F32)], comm=comm)


def _conv_bwd(dc2, z, cw, dz, comm=_NO_COMM):
    seq = z.shape[0]
    halo_per_blk = ROW_BLK // CONV_HALO
    n_blk = seq // ROW_BLK
    last_halo = seq // CONV_HALO - 1

    def body(dc_ref, dn_ref, zg_ref, zh_ref, cw_ref, dz_any, o_ref, dcw_ref, xs, ys, dc1_ref, dcw_acc):
        del dz_any
        i = pl.program_id(0)
        a, s, c1 = _glu(zg_ref[...])
        _, _, c1h = _glu(zh_ref[...])
        xs[0, 0:CONV_HALO, :] = jnp.where(i > 0, c1h, 0.0)
        xs[0, CONV_HALO:, :] = c1
        ys[0, 0:ROW_BLK, :] = dc_ref[...]
        ys[0, ROW_BLK:, :] = jnp.where(i < n_blk - 1, dn_ref[...], 0.0)
        _shifted_copies(xs)
        _shifted_copies(ys)

        @pl.when(i == 0)
        def _():
            dcw_acc[...] = jnp.zeros_like(dcw_acc)

        for sl in range(CONV_CH // HEAD_DIM):
            cs = slice(sl * HEAD_DIM, (sl + 1) * HEAD_DIM)
            taps = [cw_ref[j:j + 1, cs] for j in range(CONV_K)]

            def chunk(rc, carry, cs=cs, taps=taps):
                r0 = pl.multiple_of(rc * CONV_ROWS, CONV_ROWS)
                dc = ys[0, pl.ds(r0, CONV_ROWS), cs]
                acc = [jnp.zeros((CONV_ROWS, HEAD_DIM), F32)] * 2
                for j in range(CONV_K):
                    prod = dc * _shifted(xs, CONV_HALO - (CONV_K - 1) + j, r0, cs)
                    dcw_acc[j, :, cs] += jnp.sum(prod.reshape(CONV_ROWS // SUBLANES, SUBLANES, HEAD_DIM), axis=0)
                    acc[j % 2] = acc[j % 2] + taps[j] * _shifted(ys, CONV_K - 1 - j, r0, cs)
                dc1_ref[pl.ds(r0, CONV_ROWS), cs] = acc[0] + acc[1]
                return carry

            lax.fori_loop(0, ROW_BLK // CONV_ROWS, chunk, 0)
        dc1 = dc1_ref[...]
        o_ref[:, :CONV_CH] = (dc1 * s).astype(BF16)
        o_ref[:, CONV_CH:] = (dc1 * a * s * (1.0 - s)).astype(BF16)

        @pl.when(i == n_blk - 1)
        def _():
            dcw_ref[...] = jnp.sum(dcw_acc[...], axis=1)

    return _pcall(
        body, "conv_bwd", (n_blk,), [dc2, dc2, z, z, cw, dz],
        [pl.BlockSpec((ROW_BLK, CONV_CH), lambda i: (i, 0)),
         pl.BlockSpec((CONV_HALO, CONV_CH), lambda i: (jnp.minimum((i + 1) * halo_per_blk, last_halo), 0)),
         pl.BlockSpec((ROW_BLK, 2 * CONV_CH), lambda i: (i, GLU_COL_BLK)),
         pl.BlockSpec((CONV_HALO, 2 * CONV_CH), lambda i: (jnp.maximum(i * halo_per_blk - 1, 0), GLU_COL_BLK)),
         pl.BlockSpec((CONV_HALO, CONV_CH), lambda i: (0, 0)),
         pl.BlockSpec(memory_space=pl.ANY)],
        [jax.ShapeDtypeStruct(dz.shape, BF16), jax.ShapeDtypeStruct((CONV_HALO, CONV_CH), F32)],
        [pl.BlockSpec((ROW_BLK, 2 * CONV_CH), lambda i: (i, GLU_COL_BLK)), pl.BlockSpec((CONV_HALO, CONV_CH), lambda i: (0, 0))],
        [pltpu.VMEM((SUBLANES, ROW_BLK + CONV_HALO, CONV_CH), F32), pltpu.VMEM((SUBLANES, ROW_BLK + CONV_HALO, CONV_CH), F32),
         pltpu.VMEM((ROW_BLK, CONV_CH), F32), pltpu.VMEM((CONV_HALO, SUBLANES, CONV_CH), F32)],
        aliases={5: 0}, comm=comm)


def _epi_mix(ya, c4, wcp, gates, bg):
    yc = _dot(c4, wcp, "nn")
    gv = _sigmoid(gates.astype(F32) + bg)
    merged = gv[:, :D_MODEL] * ya + gv[:, D_MODEL:] * yc
    return merged, ya, yc


def _epi_residual_rms(acc, xres, g):
    x = xres + acc
    return x, _rms_fwd_vals(x, g)


def _cross_scores(cq, ck):
    out = []
    for h in range(CROSS_HEADS):
        cs = slice(h * CROSS_HD, (h + 1) * CROSS_HD)
        s = _dot(cq[:, cs], ck[:, cs], "nt") * (CROSS_HD ** -0.5)
        e = jnp.exp(s - jnp.max(s, axis=-1, keepdims=True))
        out.append((cs, e, jnp.sum(e, axis=-1, keepdims=True)))
    return out


def _epi_cross_fwd(acc, ck, cv):
    cq = acc.astype(BF16)
    co = [_dot(e, cv[:, cs], "nn") / den for cs, e, den in _cross_scores(cq, ck)]
    return cq, jnp.concatenate(co, axis=1)


def _epi_cross_bwd(dco, cq, ck, cv):
    dco = dco.astype(BF16)
    dcq, dck, dcv = [], [], []
    for cs, e, den in _cross_scores(cq, ck):
        p = e / den
        dp = _dot(dco[:, cs], cv[:, cs], "nt")
        ds = (p * (dp - jnp.sum(dp * p, axis=-1, keepdims=True)) * (CROSS_HD ** -0.5)).astype(BF16)
        dcq.append(_dot(ds, ck[:, cs], "nn"))
        dck.append(_dot(ds, cq[:, cs], "tn"))
        dcv.append(_dot(p, dco[:, cs], "tn"))
    return jnp.concatenate(dcq, axis=1), jnp.concatenate(dck, axis=1), jnp.concatenate(dcv, axis=1)


def _epi_mlp_up(acc):
    return acc, jnp.square(jnp.maximum(acc, 0.0))


def _epi_final(acc, x2, tgt, g):
    x3 = x2 + acc
    err = _rms_fwd_vals(x3, g) - tgt
    loss = (0.5 / D_MODEL) * jnp.sum(err * err)
    dx3, dg = _rms_bwd_vals(x3, g, err * (1.0 / D_MODEL))
    return dx3, jnp.full((1, HEAD_DIM), loss, F32), dg


def _epi_mlp_down_bwd(dh, hpre):
    return (dh * 2.0 * jnp.maximum(hpre.astype(F32), 0.0),)


def _epi_rms_bwd(du, x, g, dres):
    dx, dg = _rms_bwd_vals(x, g, du)
    return dres.astype(F32) + dx, dg


def _epi_rms_bwd_g(du, x, g):
    return (_rms_bwd_vals(x, g, du)[1],)


def _epi_mix_bwd(dm, ya, yc, gates, bg):
    gv = _sigmoid(gates.astype(F32) + bg)
    ga, gb = gv[:, :D_MODEL], gv[:, D_MODEL:]
    ya, yc = ya.astype(F32), yc.astype(F32)
    dgate = jnp.concatenate([dm * ya * ga * (1.0 - ga), dm * yc * gb * (1.0 - gb)], axis=1)
    return dm * ga, dm * gb, dgate, jnp.sum(dgate, axis=0, keepdims=True)


def _epi_ln_bwd(dc4, c2, lg, lb):
    mu = jnp.mean(c2, axis=-1, keepdims=True)
    xc = c2 - mu
    rstd = lax.rsqrt(jnp.mean(xc * xc, axis=-1, keepdims=True) + EPS)
    xh = xc * rstd
    c3 = xh * lg + lb
    sg = _sigmoid(c3)
    dc3 = dc4 * sg * (1.0 + c3 * (1.0 - sg))
    dxh = dc3 * lg
    dc2 = rstd * (dxh - jnp.mean(dxh, axis=-1, keepdims=True) - xh * jnp.mean(dxh * xh, axis=-1, keepdims=True))
    return (dc2, jnp.sum(dc3 * xh, axis=0, keepdims=True), jnp.sum(dc3, axis=0, keepdims=True),
            jnp.sum(dc2, axis=0, keepdims=True))


def _sds(shape, dtype):
    return jax.ShapeDtypeStruct(shape, dtype)


class _Lazy:
    def __init__(self, fn):
        self.fn = fn

    def __getitem__(self, key):
        return self.fn(key)


def _local_step(x, mem, tgt, sm, plan):
    w = _Lazy(plan.w)
    dw = {}

    def carry(name, n_own, fn, *args, **kw):
        c = plan.comm(name, dw)
        res = fn(*args, comm=c, **kw)
        plan.done(name, res[n_own:])
        return res[:n_own]

    def mm(name, *args, **kw):
        return carry(name, len(args[6]), _mm, name, *args, **kw)

    seq = x.shape[0]
    nr = seq // ROW_BLK
    big = min(1024, seq)
    nb = seq // big
    row = lambda n: ((ROW_BLK, n), lambda i, j, k: (i, 0))
    vec = lambda n: ((1, n), lambda i, j, k: (0, 0))
    full = lambda r, c: ((r, c), lambda i, j, k: (0, 0))
    gates_blk = ((ROW_BLK, 2 * D_MODEL), lambda i, j, k: (i, GATE_COL_BLK))
    tabs = _rope_tables(seq)

    whole3 = lambda a: (a.shape, lambda i, j, k: (0, 0, 0))
    u, z = plan.project_in(x, sm["g_mix"])
    a_gs = carry("qkv_prep", 3, _qkv_prep, z, tabs)
    os_, ls_ = [], []
    for g in range(N_GROUPS):
        name = "attn_fwd_%d" % g
        o_g, l_g = carry(name, 2, _attn_fwd, name, a_gs[g])
        os_.append(o_g)
        ls_.append(l_g)
    attn, lt = _attn_merge(os_, ls_, seq)
    c2, c4 = carry("conv_fwd", 2, _conv_fwd, z, w["taps"], sm["conv_b"], sm["conv_ln_g"], sm["conv_ln_b"])
    merged, ya, yc = mm(
        "mix", attn, w["w_attn_proj"], "nn", (nr, 1, 1), row(GROUP_W), full(GROUP_W, D_MODEL),
        [(_sds((seq, D_MODEL), BF16), *row(D_MODEL))] * 3,
        extras=[(c4, *row(CONV_CH)), (w["w_conv_proj"], *full(CONV_CH, D_MODEL)), (z, *gates_blk), (sm["b_gate"], *vec(2 * D_MODEL))],
        epi=_epi_mix)
    x1, uq = mm("out_proj", merged, w["w_out"], "nn", (nr, 1, 1), row(D_MODEL), full(D_MODEL, D_MODEL),
                 [(_sds((seq, D_MODEL), F32), *row(D_MODEL)), (_sds((seq, D_MODEL), BF16), *row(D_MODEL))],
                 extras=[(x, *row(D_MODEL)), (sm["g_cross"], *vec(D_MODEL))], epi=_epi_residual_rms)

    mn = _rms_fwd("rms_mem", mem, sm["g_mem"], N_MEM)[0]
    ckv = mm("ckv_proj", mn, w["w_ckv"], "nn", (1, N_DEV, 1), full(N_MEM, D_MODEL),
              ((None, D_MODEL, 2 * D_MODEL // N_DEV), lambda i, j, k: (j, 0, 0)),
              [(_sds((N_MEM, 2 * D_MODEL), BF16), (N_MEM, 2 * D_MODEL // N_DEV), lambda i, j, k: (0, j))])[0]
    ck, cv = ckv[:, :D_MODEL], ckv[:, D_MODEL:]
    kv_blk = full(N_MEM, D_MODEL)
    cq, co = mm("cq_proj_cross", uq, w["w_cq"], "nn", (nr, 1, 1), row(D_MODEL), full(D_MODEL, D_MODEL),
                 [(_sds((seq, D_MODEL), BF16), *row(D_MODEL))] * 2,
                 extras=[(ck, *kv_blk), (cv, *kv_blk)], epi=_epi_cross_fwd)
    x2, um = mm("co_proj", co, w["w_co"], "nn", (nr, 1, 1), row(D_MODEL), full(D_MODEL, D_MODEL),
                 [(_sds((seq, D_MODEL), F32), *row(D_MODEL)), (_sds((seq, D_MODEL), BF16), *row(D_MODEL))],
                 extras=[(x1, *row(D_MODEL)), (sm["g_mlp"], *vec(D_MODEL))], epi=_epi_residual_rms)

    ff_blk = D_FF // N_DEV
    row_f32 = (_sds((seq, D_MODEL), F32), *row(D_MODEL))
    row_bf16 = (_sds((seq, D_MODEL), BF16), *row(D_MODEL))
    col_sum = (_sds((1, D_MODEL), F32), *vec(D_MODEL))
    hpre, h = mm("mlp_up", um, w["w_up"], "nn", (nr, 1, 1), row(D_MODEL), whole3(w["w_up"]),
                 [(_sds((seq, D_FF), BF16), *row(D_FF))] * 2, epi=_epi_mlp_up, split=("cols", N_DEV), b_resident=True)
    kt = D_FF // D_MODEL
    dx3, loss, dg_final = mm(
        "mlp_down_loss", h, w["w_down"], "nn", (nr, 1, 1), row(D_FF), full(D_FF, D_MODEL),
        [row_bf16, (_sds((1, HEAD_DIM), F32), *vec(HEAD_DIM)), col_sum],
        extras=[(x2, *row(D_MODEL)), (tgt, *row(D_MODEL)), (sm["g_final"], *vec(D_MODEL))], epi=_epi_final, acc_outs=(1, 2),
        b_resident=True)

    dhpre = mm("mlp_down_bwd", dx3, w["w_down"], "nt", (nr, 1, 1), row(D_MODEL), full(D_FF, D_MODEL),
               [(_sds((seq, D_FF), BF16), *row(D_FF))], extras=[(hpre, *row(D_FF))], epi=_epi_mlp_down_bwd,
               split=("cols", kt), b_resident=True)[0]
    big2 = min(2 * big, seq)
    nb2 = seq // big2
    dw["w_down"] = mm("dw_down", h, dx3, "tn", (kt, 1, nb2), ((big2, D_MODEL), lambda i, j, k: (k, i)),
                      ((big2, D_MODEL), lambda i, j, k: (k, 0)),
                      [(_sds((D_FF, D_MODEL), BF16), (D_MODEL, D_MODEL), lambda i, j, k: (i, 0))])[0]
    dx2, dg_mlp = mm("mlp_up_bwd", dhpre, w["w_up"], "nt", (nr, 1, 1), row(D_FF), whole3(w["w_up"]),
                     [row_bf16, col_sum],
                     extras=[(x2, *row(D_MODEL)), (sm["g_mlp"], *vec(D_MODEL)), (dx3, *row(D_MODEL))],
                     epi=_epi_rms_bwd, acc_outs=(1,), split=("sum", N_DEV), b_resident=True)
    dw["w_up"] = mm("dw_up", um, dhpre, "tn", (1, N_DEV, nb2), ((big2, D_MODEL), lambda i, j, k: (k, 0)),
                    ((big2, ff_blk), lambda i, j, k: (k, j)),
                    [(_sds((N_DEV, D_MODEL, ff_blk), BF16), (None, D_MODEL, ff_blk), lambda i, j, k: (j, 0, 0))])[0]

    acc_kv = (_sds((N_MEM, D_MODEL), F32), *kv_blk)
    dcq, dck, dcv = mm("co_proj_bwd_cross", dx2, w["w_co"], "nt", (nr, 1, 1), row(D_MODEL), full(D_MODEL, D_MODEL),
                       [row_bf16, acc_kv, acc_kv],
                       extras=[(cq, *row(D_MODEL)), (ck, *kv_blk), (cv, *kv_blk)], epi=_epi_cross_bwd, acc_outs=(1, 2))

    def dw_square(name, act, grad):
        return mm(name, act, grad, "tn", (1, 1, nb2), ((big2, D_MODEL), lambda i, j, k: (k, 0)),
                  ((big2, D_MODEL), lambda i, j, k: (k, 0)), [(_sds((D_MODEL, D_MODEL), BF16), *full(D_MODEL, D_MODEL))])[0]

    dw["w_co"] = dw_square("dw_co", co, dx2)
    dx1, dg_cross = mm("cq_proj_bwd", dcq, w["w_cq"], "nt", (nr, 1, 1), row(D_MODEL), full(D_MODEL, D_MODEL),
                       [row_bf16, col_sum],
                       extras=[(x1, *row(D_MODEL)), (sm["g_cross"], *vec(D_MODEL)), (dx2, *row(D_MODEL))],
                       epi=_epi_rms_bwd, acc_outs=(1,))
    dw["w_cq"] = dw_square("dw_cq", uq, dcq)
    dckv = jnp.concatenate([dck, dcv], axis=1)
    kv_chunk = 2 * D_MODEL // N_DEV
    dw["w_ckv"] = mm("dw_ckv", mn, dckv, "tn", (1, N_DEV, 1), full(N_MEM, D_MODEL), ((N_MEM, kv_chunk), lambda i, j, k: (0, j)),
                      [(_sds((N_DEV, D_MODEL, kv_chunk), BF16), (None, D_MODEL, kv_chunk), lambda i, j, k: (j, 0, 0))])[0]
    dg_mem = mm("ckv_proj_bwd", dckv, w["w_ckv"], "nt", (1, 1, N_DEV), ((N_MEM, kv_chunk), lambda i, j, k: (0, k)),
                 ((None, D_MODEL, kv_chunk), lambda i, j, k: (k, 0, 0)), [(_sds((1, D_MODEL), F32), *vec(D_MODEL))],
                 extras=[(mem, *full(N_MEM, D_MODEL)), (sm["g_mem"], *vec(D_MODEL))], epi=_epi_rms_bwd_g, acc_outs=(0,))[0]

    dya, dyc, dz, db_gate = mm(
        "out_proj_bwd_mix", dx1, w["w_out"], "nt", (nr, 1, 1), row(D_MODEL), full(D_MODEL, D_MODEL),
        [(_sds((seq, D_MODEL), BF16), *row(D_MODEL)), (_sds((seq, D_MODEL), BF16), *row(D_MODEL)),
         (_sds((seq, IN_W), BF16), *gates_blk), (_sds((1, 2 * D_MODEL), F32), *vec(2 * D_MODEL))],
        extras=[(ya, *row(D_MODEL)), (yc, *row(D_MODEL)), (z, *gates_blk), (sm["b_gate"], *vec(2 * D_MODEL))],
        epi=_epi_mix_bwd, acc_outs=(3,))
    dw["w_out"] = dw_square("dw_out", merged, dx1)
    dattn = mm("attn_proj_bwd", dya, w["w_attn_proj"], "nt", (nr, 1, 1), row(D_MODEL), full(GROUP_W, D_MODEL),
                [(_sds((seq, GROUP_W), BF16), *row(GROUP_W))])[0]
    pc = D_MODEL // N_DEV
    dw["w_attn_proj"] = mm("dw_attn_proj", attn, dya, "tn", (1, 1, nb2), ((big2, GROUP_W), lambda i, j, k: (k, 0)),
                           ((big2, D_MODEL), lambda i, j, k: (k, 0)),
                           [(_sds((N_DEV, GROUP_W, pc), BF16), (N_DEV, GROUP_W, pc), lambda i, j, k: (0, 0, 0))],
                           out_chunks=N_DEV)[0]
    cvec = (_sds((1, CONV_CH), F32), *vec(CONV_CH))
    dc2, dg_ln_g, dg_ln_b, dg_conv_b = mm(
        "conv_proj_bwd_ln", dyc, w["w_conv_proj"], "nt", (nr, 1, 1), row(D_MODEL), full(CONV_CH, D_MODEL),
        [(_sds((seq, CONV_CH), F32), *row(CONV_CH)), cvec, cvec, cvec],
        extras=[(c2, *row(CONV_CH)), (sm["conv_ln_g"], *vec(CONV_CH)), (sm["conv_ln_b"], *vec(CONV_CH))],
        epi=_epi_ln_bwd, acc_outs=(1, 2, 3))
    dw["w_conv_proj"] = mm("dw_conv_proj", c4, dyc, "tn", (1, 1, nb2), ((big2, CONV_CH), lambda i, j, k: (k, 0)),
                           ((big2, D_MODEL), lambda i, j, k: (k, 0)),
                           [(_sds((N_DEV, CONV_CH, pc), BF16), (N_DEV, CONV_CH, pc), lambda i, j, k: (0, 0, 0))],
                           out_chunks=N_DEV)[0]
    dz, dg_conv_w = carry("conv_bwd", 2, _conv_bwd, dc2, z, w["taps"], dz)
    preps = _attn_bwd_prep(dattn, attn, lt)
    d_gs = []
    for g in range(N_GROUPS):
        name = "attn_bwd_%d" % g
        d_gs.append(carry(name, 1, _attn_bwd, name, a_gs[g], *preps[g])[0])
    dz = _dqkv_post(d_gs, tabs, dz)
    dw["w_in"] = mm("dw_in", u, dz, "tn", (1, N_DEV, nb2), ((big2, D_MODEL), lambda i, j, k: (k, 0)),
                    ((big2, D_MODEL), lambda i, j, k: (k, j)),
                    [(_sds((N_DEV, D_MODEL, D_MODEL), BF16), (None, D_MODEL, D_MODEL), lambda i, j, k: (j, 0, 0))])[0]
    token = plan.start_w_in(dw["w_in"])
    grad_x, dg_mix = mm("in_proj_bwd", dz, w["w_in"], "nt", (nr, 1, 1), row(IN_W), whole3(w["w_in"]), [row_f32, col_sum],
                        extras=[(x, *row(D_MODEL)), (sm["g_mix"] + token, *vec(D_MODEL)), (dx1, *row(D_MODEL))],
                        epi=_epi_rms_bwd, acc_outs=(1,), split=("sum", N_DEV), b_resident=True)
    small = dict(g_mix=dg_mix, b_gate=db_gate, conv_b=dg_conv_b, conv_ln_g=dg_ln_g, conv_ln_b=dg_ln_b, g_cross=dg_cross,
                 g_mem=dg_mem, g_mlp=dg_mlp, g_final=dg_final, loss=loss, conv_w=dg_conv_w)
    return grad_x, dw, small


SHARD_SHAPE = dict(w_in=(1024, 1024), w_attn_proj=(512, 128), w_conv_proj=(768, 128), w_out=(128, 1024), w_cq=(128, 1024),
                   w_ckv=(1024, 256), w_co=(128, 1024), w_up=(1024, 512), w_down=(512, 1024))
FWD_CARRY = {"in_proj":("w_attn_proj", "w_conv_proj", "w_out", "w_cq", "w_ckv", "w_co", "taps"),
             "qkv_prep": ("w_up",), "conv_fwd": ("w_down",)}
BWD_CARRY = {"dw_up": ("w_down",), "out_proj_bwd_mix": ("w_co", "w_cq"), "conv_bwd": ("w_up", "w_ckv"),
             "attn_bwd_0": ("w_out",), "attn_bwd_1": ("w_attn_proj", "w_conv_proj")}


def _cols_to_2d(a):
    return a.transpose(1, 0, 2).reshape(a.shape[1], -1)


def _in_proj_gather(resid, g, w_shard, comm):
    seq = resid.shape[0]
    tm = min(1024, seq)
    x, y, c = lax.axis_index("x"), lax.axis_index("y"), lax.axis_index("c")
    ident = lambda px, py, pc: 4 * px + 2 * py + pc
    far = [(1 - x, y), (x, 1 - y), (1 - x, 1 - y)]
    order = jnp.stack([ident(x, y, c), ident(x, y, 1 - c), ident(*far[0], c), ident(*far[1], c), ident(*far[0], 1 - c),
                       ident(*far[1], 1 - c), ident(*far[2], c), ident(*far[2], 1 - c)]).astype(jnp.int32)
    forward_at = {2: 0, 3: 1, 6: 2}
    n_far = len(far)

    def body(order_ref, x_ref, g_ref, wsh_ref, *rest):
        c_in, z_ref, u_ref, wg_ref = rest[:comm.n], rest[comm.n], rest[comm.n + 1], rest[comm.n + 2]
        c_out = rest[comm.n + 3:2 * comm.n + 3]
        wbuf, u_all, load_sem, local_sem, recv_sems, ici_send, d2d_send = rest[2 * comm.n + 3:2 * comm.n + 10]
        sems = rest[2 * comm.n + 10:]
        jj, i = pl.program_id(0), pl.program_id(1)
        (kx, ky, kc), me, chips = _Comm._where()
        sibling = (kx, ky, 1 - kc)
        n = order_ref[jj]

        def push(src, blk, send, to):
            return pltpu.make_async_remote_copy(src_ref=src, dst_ref=wg_ref.at[blk], send_sem=send,
                                                recv_sem=recv_sems.at[blk], device_id=to, device_id_type=MESH)

        def load(src):
            cp = pltpu.make_async_copy(src, wbuf, load_sem)
            cp.start()
            cp.wait()

        @pl.when(jnp.logical_and(jj == 0, i == 0))
        def _():
            push(wsh_ref, me, d2d_send, sibling).start()
            for (px, py) in chips[:2]:
                push(wsh_ref, me, ici_send, (px, py, kc)).start()
            pltpu.make_async_copy(wsh_ref, wg_ref.at[me], local_sem).start()
            load(wsh_ref)

        @pl.when(jnp.logical_and(jj > 0, i == 0))
        def _():
            push(wg_ref.at[n], n, d2d_send, sibling).wait_recv()
            for step, k in forward_at.items():
                @pl.when(jj == step)
                def _(k=k):
                    blk = 4 * chips[k][0] + 2 * chips[k][1] + kc
                    push(wg_ref.at[blk], blk, d2d_send, sibling).start()

            @pl.when(jj == 2)
            def _():
                push(wsh_ref, me, ici_send, (*chips[2], kc)).start()

            if comm.n:
                @pl.when(jj == 3)
                def _():
                    comm.start(c_in, c_out, sems)

            load(wg_ref.at[n])

        rows = pl.ds(pl.multiple_of(i * tm, tm), tm)

        @pl.when(jj == 0)
        def _():
            u_val = _rms_fwd_vals(x_ref[...], g_ref[...]).astype(BF16)
            u_all[rows, :] = u_val
            u_ref[...] = u_val
            z_ref[...] = _dot(u_val, wbuf[...], "nn").astype(BF16)

        @pl.when(jj > 0)
        def _():
            z_ref[...] = _dot(u_all[rows, :], wbuf[...], "nn").astype(BF16)

        @pl.when(jnp.logical_and(jj == N_DEV - 1, i == pl.num_programs(1) - 1))
        def _():
            def drain_sends(send, count):
                blocks = wg_ref.at[pl.ds(0, count)]
                pltpu.make_async_remote_copy(src_ref=blocks, dst_ref=blocks, send_sem=send, recv_sem=recv_sems.at[0],
                                             device_id=sibling, device_id_type=MESH).wait_send()

            drain_sends(ici_send, n_far)
            drain_sends(d2d_send, n_far + 1)
            pltpu.make_async_copy(wsh_ref, wg_ref.at[me], local_sem).wait()
            if comm.n:
                comm.wait(c_in, c_out, sems)

    any_spec = pl.BlockSpec(memory_space=pl.ANY)
    n_i = seq // tm
    first_pass = lambda jj, i, order_ref: (jnp.where(jj == 0, i, n_i - 1), 0)
    grid_spec = pltpu.PrefetchScalarGridSpec(
        num_scalar_prefetch=1, grid=(N_DEV, n_i),
        in_specs=[pl.BlockSpec((tm, D_MODEL), first_pass), pl.BlockSpec((1, D_MODEL), lambda jj, i, order_ref: (0, 0)),
                  any_spec] + comm.in_specs,
        out_specs=[pl.BlockSpec((tm, D_MODEL), lambda jj, i, order_ref: (i, order_ref[jj])),
                   pl.BlockSpec((tm, D_MODEL), first_pass), any_spec] + comm.out_specs,
        scratch_shapes=[pltpu.VMEM((D_MODEL, D_MODEL), BF16), pltpu.VMEM((seq, D_MODEL), BF16), pltpu.SemaphoreType.DMA,
                        pltpu.SemaphoreType.DMA, pltpu.SemaphoreType.DMA((N_DEV,)), pltpu.SemaphoreType.DMA,
                        pltpu.SemaphoreType.DMA] + comm.scratch)
    return pl.pallas_call(
        body, name="in_proj_gather", grid_spec=grid_spec,
        out_shape=[jax.ShapeDtypeStruct((seq, IN_W), BF16), jax.ShapeDtypeStruct((seq, D_MODEL), BF16),
                   jax.ShapeDtypeStruct((N_DEV, D_MODEL, D_MODEL), BF16)] + comm.out_shape,
        compiler_params=_params(dimension_semantics=("arbitrary", "arbitrary")),
    )(order, resid, g, w_shard, *comm.arrays)


_HBM = pl.BlockSpec(memory_space=pltpu.HBM)
_SEM = pl.BlockSpec(memory_space=pltpu.SEMAPHORE)


def _chunks_start(dw):
    def body(src_ref, land_ref, send_sem, recv_sem, src_thru, land_thru, token):
        del src_thru, land_thru
        me, peers = _peers()
        for (px, py, pc) in peers:
            pltpu.make_async_remote_copy(src_ref=src_ref.at[4 * px + 2 * py + pc], dst_ref=land_ref.at[me], send_sem=send_sem,
                                         recv_sem=recv_sem, device_id=(px, py, pc), device_id_type=MESH).start()
        token[...] = jnp.zeros_like(token)

    return pl.pallas_call(
        body, name="w_in_grad_start",
        out_shape=(pltpu.SemaphoreType.DMA(()), pltpu.SemaphoreType.DMA(()), pltpu.HBM(dw.shape, dw.dtype),
                   pltpu.HBM(dw.shape, dw.dtype), jax.ShapeDtypeStruct((SUBLANES, HEAD_DIM), F32)),
        in_specs=(_HBM, _HBM), out_specs=(_SEM, _SEM, _HBM, _HBM, pl.BlockSpec(memory_space=pltpu.VMEM)),
        input_output_aliases={0: 2, 1: 3},
        compiler_params=pltpu.CompilerParams(has_side_effects=pltpu.SideEffectType.DATAFLOW_SIDE_EFFECTING),
    )(pltpu.with_memory_space_constraint(dw, pltpu.HBM),
      pltpu.with_memory_space_constraint(lax.empty(dw.shape, dw.dtype), pltpu.HBM))


def _chunks_wait(send_sem, recv_sem, src_thru, land_thru, after):
    def body(src_ref, land_ref, send_sem, recv_sem, after_ref, src_dead, land_out):
        del after_ref, src_dead, land_out
        seven = land_ref.at[pl.ds(0, N_DEV - 1)]
        cp = pltpu.make_async_remote_copy(src_ref=seven, dst_ref=seven, send_sem=send_sem, recv_sem=recv_sem,
                                          device_id=_peers()[1][0], device_id_type=MESH)
        cp.wait_send()
        cp.wait_recv()

    return pl.pallas_call(
        body, name="w_in_grad_wait",
        out_shape=(pltpu.HBM(src_thru.shape, src_thru.dtype), pltpu.HBM(land_thru.shape, land_thru.dtype)),
        in_specs=(_HBM, _HBM, _SEM, _SEM, pl.BlockSpec(memory_space=pl.ANY)), out_specs=(_HBM, _HBM),
        input_output_aliases={0: 0, 1: 1},
        compiler_params=pltpu.CompilerParams(has_side_effects=pltpu.SideEffectType.DATAFLOW_SIDE_EFFECTING),
    )(src_thru, land_thru, send_sem, recv_sem, after)


class _Plan:
    def __init__(self, shards, n_tap_cols):
        self.shards = shards
        self.gathered = {}
        self.parts = {}
        self.n_tap_cols = n_tap_cols

    def project_in(self, x, g):
        comm = self.comm("in_proj", None)
        res = _in_proj_gather(x, g, self.shards["w_in"], comm)
        self.gathered["w_in"] = res[2]
        self.done("in_proj", res[3:])
        return res[1], res[0]

    def start_w_in(self, dw_in):
        *self.in_flight, token = _chunks_start(dw_in)
        return token[0, 0]

    def finish_w_in(self, after):
        src, land = _chunks_wait(*self.in_flight, after)
        me = _peers()[0]
        own = lax.dynamic_slice(src, (me, 0, 0), (1,) + src.shape[1:])
        return lax.dynamic_update_slice(land, own, (me, 0, 0))

    def comm(self, name, dw):
        if name in FWD_CARRY:
            return _Comm(replicated=[self.shards[k] for k in FWD_CARRY[name]])
        if name in BWD_CARRY:
            return _Comm(chunked=[dw[k].reshape((N_DEV,) + SHARD_SHAPE[k]) for k in BWD_CARRY[name]])
        return _NO_COMM

    def done(self, name, got):
        if name in FWD_CARRY:
            self.gathered.update(zip(FWD_CARRY[name], got))
        elif name in BWD_CARRY:
            self.parts.update(zip(BWD_CARRY[name], got))

    def w(self, key):
        g = self.gathered[key]
        if key in ("w_in", "w_up", "w_ckv"):
            return g
        if key in ("w_attn_proj", "w_conv_proj"):
            return _cols_to_2d(g)
        if key == "taps":
            return jnp.pad(_cols_to_2d(g[:, :CONV_K, :self.n_tap_cols]), ((0, 1), (0, 0)))
        return g.reshape(-1, g.shape[-1])


def _adamw(name, w, m, v, parts, comm=_NO_COMM):
    rows, cols = w.shape
    n_parts = parts.shape[0]
    rb = rows if rows <= 256 or rows % 256 else 256

    def body(w_ref, m_ref, v_ref, p_ref, g_ref, d_ref, nm_ref, nv_ref):
        g = p_ref[0].astype(F32)
        for q in range(1, n_parts):
            g = g + p_ref[q].astype(F32)
        wv = w_ref[...]
        nm = ADAM_B1 * m_ref[...] + (1.0 - ADAM_B1) * g
        nv = ADAM_B2 * v_ref[...] + (1.0 - ADAM_B2) * jnp.square(g)
        m_hat = nm / (1.0 - ADAM_B1 ** ADAM_STEP)
        v_hat = nv / (1.0 - ADAM_B2 ** ADAM_STEP)
        g_ref[...] = g
        d_ref[...] = -ADAM_LR * (m_hat / (jnp.sqrt(v_hat) + ADAM_EPS) + ADAM_WD * wv)
        nm_ref[...] = nm
        nv_ref[...] = nv

    blk = pl.BlockSpec((rb, cols), lambda i: (i, 0))
    return _pcall(body, name, (rows // rb,), [w, m, v, parts],
                  [blk, blk, blk, pl.BlockSpec((n_parts, rb, cols), lambda i: (0, i, 0))],
                  [jax.ShapeDtypeStruct((rows, cols), F32)] * 4, [blk] * 4, comm=comm)


def _sum_parts(name, parts):
    def body(p_ref, o_ref):
        acc = p_ref[0]
        for q in range(1, parts.shape[0]):
            acc = acc + p_ref[q]
        o_ref[...] = acc

    return _pcall(body, name, (1,), [parts], [pl.BlockSpec(parts.shape, lambda i: (0, 0, 0))],
                  [jax.ShapeDtypeStruct(parts.shape[1:], F32)], [pl.BlockSpec(parts.shape[1:], lambda i: (0, 0))])[0]


BIG = ("w_in", "w_attn_proj", "w_conv_proj", "w_out", "w_cq", "w_ckv", "w_co", "w_up", "w_down")
SMALL = ("g_mix", "b_gate", "conv_b", "conv_ln_g", "conv_ln_b", "g_cross", "g_mem", "g_mlp", "g_final")
SMALL_ORDER = SMALL + ("loss", "conv_w")
WEIGHTS = ("g_mix", "w_in", "b_gate", "conv_w", "conv_b", "conv_ln_g", "conv_ln_b", "w_attn_proj", "w_conv_proj", "w_out",
           "g_cross", "g_mem", "w_cq", "w_ckv", "w_co", "g_mlp", "w_up", "w_down", "g_final")


def kernel(x, mem, g_mix, w_in, b_gate, conv_w, conv_b, conv_ln_g, conv_ln_b, w_attn_proj, w_conv_proj, w_out, g_cross, g_mem, w_cq, w_ckv, w_co, g_mlp, w_up, w_down, g_final, loss_target, m_g_mix, m_w_in, m_b_gate, m_conv_w, m_conv_b, m_conv_ln_g, m_conv_ln_b, m_w_attn_proj, m_w_conv_proj, m_w_out, m_g_cross, m_g_mem, m_w_cq, m_w_ckv, m_w_co, m_g_mlp, m_w_up, m_w_down, m_g_final, v_g_mix, v_w_in, v_b_gate, v_conv_w, v_conv_b, v_conv_ln_g, v_conv_ln_b, v_w_attn_proj, v_w_conv_proj, v_w_out, v_g_cross, v_g_mem, v_w_cq, v_w_ckv, v_w_co, v_g_mlp, v_w_up, v_w_down, v_g_final):
    args = dict(locals())
    wts = {k: args[k] for k in WEIGHTS}
    mom = {k: args["m_" + k] for k in WEIGHTS}
    var = {k: args["v_" + k] for k in WEIGHTS}
    two_d = lambda a: a.reshape(a.shape[-2:]) if a.ndim == 3 else a.reshape(1, -1)

    shards = {k: two_d(wts[k]).astype(BF16) for k in BIG}
    shards["taps"] = jnp.pad(two_d(conv_w), ((0, 1), (0, HEAD_DIM - conv_w.shape[-1])))
    plan = _Plan(shards, conv_w.shape[-1])
    sm = {k: two_d(wts[k]) for k in SMALL}

    grad_x, _, small = _local_step(x[0], mem[0], loss_target[0], sm, plan)
    parts = plan.parts

    out = {}
    small_comm = _Comm(replicated=[small[k] for k in SMALL_ORDER])
    for k in BIG[1:]:
        res = _adamw("adamw_" + k, two_d(wts[k]), two_d(mom[k]), two_d(var[k]), parts[k],
                     comm=small_comm if k == BIG[1] else _NO_COMM)
        out[k] = [r.reshape(wts[k].shape) for r in res[:4]]
        if k == BIG[1]:
            small_parts = dict(zip(SMALL_ORDER, res[4:]))
    for k in SMALL:
        res = _adamw("adamw_" + k, two_d(wts[k]), two_d(mom[k]), two_d(var[k]), small_parts[k])
        out[k] = [r.reshape(wts[k].shape) for r in res]
    res = _adamw("adamw_w_in", two_d(w_in), two_d(m_w_in), two_d(v_w_in), plan.finish_w_in(after=out["g_final"][3]))
    out["w_in"] = [r.reshape(w_in.shape) for r in res]
    loss = _sum_parts("loss_sum", small_parts["loss"])[0, 0]
    me = 4 * lax.axis_index("x") + 2 * lax.axis_index("y") + lax.axis_index("c")
    n_tap_cols = conv_w.shape[-1]
    tap_parts = lax.dynamic_slice(small_parts["conv_w"], (0, 0, me * n_tap_cols), (N_DEV, CONV_K, n_tap_cols))
    res = _adamw("adamw_conv_w", two_d(conv_w), two_d(m_conv_w), two_d(v_conv_w), tap_parts)
    out["conv_w"] = [r.reshape(conv_w.shape) for r in res]

    return (loss, grad_x[None], *[out[k][0] for k in WEIGHTS], *[out[k][1] for k in WEIGHTS],
            *[out[k][2] for k in WEIGHTS], *[out[k][3] for k in WEIGHTS])
```

```python
import functools

import jax
import jax.numpy as jnp
import numpy as np
from jax import lax
from jax.experimental import pallas as pl
from jax.experimental.pallas import tpu as pltpu

F32 = jnp.float32
BF16 = jnp.bfloat16

N_DEV = 8
D_MODEL = 1024
N_MEM = 256
HEAD_DIM = 128
HEADS_PER_GROUP = 4
GROUP_W = HEADS_PER_GROUP * HEAD_DIM
DILATIONS = (1, 4, 16)
BAND = 128
N_GROUPS = 3
ATTN_W = N_GROUPS * GROUP_W
QKV_W = 3 * ATTN_W
ROT_DIM = HEAD_DIM // 4
ROPE_THETA = 500000.0
CONV_CH = 768
CONV_K = 31
CONV_HALO = 32
SUBLANES = 8
CONV_ROWS = 64
IN_W = 8192
GLU_COL_BLK = QKV_W // (2 * CONV_CH)
GATE_COL_BLK = (QKV_W + 2 * CONV_CH) // (2 * D_MODEL)
CROSS_HEADS = 4
CROSS_HD = D_MODEL // CROSS_HEADS
D_FF = 4096
EPS = 1e-6
NEG = -1e30
QB = 4
ROW_BLK = QB * BAND

ADAM_LR = 0.001
ADAM_B1 = 0.9
ADAM_B2 = 0.999
ADAM_EPS = 1e-08
ADAM_WD = 0.01
ADAM_STEP = 10

VMEM_LIMIT = 56 * 1024 * 1024
MESH = pl.DeviceIdType.MESH


def _params(**kw):
    return pltpu.CompilerParams(vmem_limit_bytes=VMEM_LIMIT, **kw)


def _sigmoid(x):
    return 1.0 / (1.0 + jnp.exp(-x))


def _dot(a, b, kind):
    dims = {"nn": (((1,), (0,)), ((), ())), "nt": (((1,), (1,)), ((), ())), "tn": (((0,), (0,)), ((), ()))}[kind]
    if a.dtype != BF16:
        a = a.astype(BF16)
    if b.dtype != BF16:
        b = b.astype(BF16)
    return lax.dot_general(a, b, dims, preferred_element_type=F32)


def _peers():
    x, y, c = lax.axis_index("x"), lax.axis_index("y"), lax.axis_index("c")
    me = 4 * x + 2 * y + c
    peers = [(x, y, 1 - c), (1 - x, y, c), (x, 1 - y, c), (1 - x, 1 - y, c),
             (1 - x, y, 1 - c), (x, 1 - y, 1 - c), (1 - x, 1 - y, 1 - c)]
    return me, peers


class _Comm:
    def __init__(self, chunked=(), replicated=()):
        self.arrays = list(chunked) + list(replicated)
        self.n_c = len(chunked)
        self.n = len(self.arrays)
        self.out_shape = [jax.ShapeDtypeStruct(a.shape, a.dtype) for a in chunked]
        self.out_shape += [jax.ShapeDtypeStruct((N_DEV,) + a.shape, a.dtype) for a in replicated]
        self.in_specs = [pl.BlockSpec(memory_space=pl.ANY)] * self.n
        self.out_specs = [pl.BlockSpec(memory_space=pl.ANY)] * self.n
        self.scratch = [pltpu.SemaphoreType.DMA((self.n,))] * 5 if self.n else []

    @staticmethod
    def _where():
        x, y, c = lax.axis_index("x"), lax.axis_index("y"), lax.axis_index("c")
        chips = [(1 - x, y), (x, 1 - y), (1 - x, 1 - y)]
        return (x, y, c), 4 * x + 2 * y + c, chips

    def _local(self, ins, outs, sems, a, me):
        src = ins[a].at[me] if a < self.n_c else ins[a]
        return pltpu.make_async_copy(src, outs[a].at[me], sems[2].at[a])

    @staticmethod
    def _remote(src, dst, send, recv, to):
        return pltpu.make_async_remote_copy(src_ref=src, dst_ref=dst, send_sem=send, recv_sem=recv, device_id=to,
                                            device_id_type=MESH)

    def start(self, ins, outs, sems):
        (x, y, c), me, chips = self._where()
        for a in range(self.n):
            self._local(ins, outs, sems, a, me).start()
            if a < self.n_c:
                for (px, py, pc) in _peers()[1]:
                    self._remote(ins[a].at[4 * px + 2 * py + pc], outs[a].at[me], sems[0].at[a], sems[1].at[a], (px, py, pc)).start()
            else:
                self._remote(ins[a], outs[a].at[me], sems[3].at[a], sems[4].at[a], (x, y, 1 - c)).start()
                for (px, py) in chips:
                    self._remote(ins[a], outs[a].at[me], sems[0].at[a], sems[1].at[a], (px, py, c)).start()

    def wait(self, ins, outs, sems):
        (x, y, c), me, chips = self._where()
        sibling = (x, y, 1 - c)

        def drain(a, pair, count):
            blocks = outs[a].at[pl.ds(0, count)]
            cp = self._remote(blocks, blocks, sems[pair].at[a], sems[pair + 1].at[a], sibling)
            cp.wait_send()
            cp.wait_recv()

        for a in range(self.n):
            if a < self.n_c:
                drain(a, 0, N_DEV - 1)
            else:
                drain(a, 0, len(chips))
                for (px, py) in chips:
                    blk = outs[a].at[4 * px + 2 * py + c]
                    self._remote(blk, blk, sems[3].at[a], sems[4].at[a], sibling).start()
        for a in range(self.n):
            if a >= self.n_c:
                drain(a, 3, len(chips) + 1)
            self._local(ins, outs, sems, a, me).wait()


_NO_COMM = _Comm()


def _pcall(body, name, grid, operands, in_specs, out_shape, out_specs, scratch=(), aliases=None, comm=_NO_COMM, **params):
    n_in, n_out, n_scr = len(operands), len(out_shape), len(scratch)
    grid = tuple(grid)

    def carried(*refs):
        ins, c_in = refs[:n_in], refs[n_in:n_in + comm.n]
        o0 = n_in + comm.n
        outs, c_out = refs[o0:o0 + n_out], refs[o0 + n_out:o0 + n_out + comm.n]
        s0 = o0 + n_out + comm.n
        scr, sems = refs[s0:s0 + n_scr], refs[s0 + n_scr:]
        ids = [pl.program_id(ax) for ax in range(len(grid))]

        @pl.when(functools.reduce(jnp.logical_and, [p == 0 for p in ids]))
        def _():
            comm.start(c_in, c_out, sems)

        body(*ins, *outs, *scr)

        @pl.when(functools.reduce(jnp.logical_and, [p == g - 1 for p, g in zip(ids, grid)]))
        def _():
            comm.wait(c_in, c_out, sems)

    return pl.pallas_call(
        carried if comm.n else body, name=name, grid=grid, in_specs=list(in_specs) + comm.in_specs,
        out_shape=list(out_shape) + comm.out_shape, out_specs=list(out_specs) + comm.out_specs,
        scratch_shapes=list(scratch) + comm.scratch, input_output_aliases=aliases or {},
        compiler_params=_params(dimension_semantics=("arbitrary",) * len(grid), **params),
    )(*operands, *comm.arrays)


def _mm(name, a, b, kind, grid, a_blk, b_blk, outs, extras=(), epi=None, acc_outs=(), j_outer=False, comm=_NO_COMM,
        split=None, b_resident=False, out_chunks=0):
    gi, gj, gk = grid
    n_ex = len(extras)
    n_out = len(outs)
    mode, n_chunks = split if split is not None else (None, 1)

    def spec(blk, fn, **kw):
        return pl.BlockSpec(blk, (lambda j, i, k: fn(i, j, k)) if j_outer else fn, **kw)

    def b_chunk(b_ref, c):
        if len(b_ref.shape) == 3:
            return b_ref[c]
        rows, cols = b_ref.shape
        if (kind == "nn") == (mode == "cols"):
            return b_ref[:, c * (cols // n_chunks):(c + 1) * (cols // n_chunks)]
        return b_ref[c * (rows // n_chunks):(c + 1) * (rows // n_chunks), :]

    def col_chunk(ref, c):
        width = ref.shape[-1] // n_chunks
        return slice(c * width, (c + 1) * width)

    def body(*refs):
        a_ref, b_ref = refs[0], refs[1]
        ex = refs[2:2 + n_ex]
        out_refs = refs[2 + n_ex:2 + n_ex + n_out]
        acc_ref = refs[2 + n_ex + n_out] if gk > 1 else None
        i = pl.program_id(1 if j_outer else 0)
        k = pl.program_id(2)
        if mode == "cols":
            a_val = a_ref[...]
            for c in range(n_chunks):
                acc = _dot(a_val, b_chunk(b_ref, c), kind)
                vals = epi(acc, *[e[:, col_chunk(e, c)] for e in ex]) if epi is not None else (acc,)
                for o, v in zip(out_refs, vals):
                    o[:, col_chunk(o, c)] = v.astype(o.dtype)
            return
        if mode == "sum":
            part = _dot(a_ref[:, col_chunk(a_ref, 0)], b_chunk(b_ref, 0), kind)
            for c in range(1, n_chunks):
                part = part + _dot(a_ref[:, col_chunk(a_ref, c)], b_chunk(b_ref, c), kind)
        else:
            part = _dot(a_ref[...], b_ref[...], kind)

        def finish(acc):
            if out_chunks:
                width = acc.shape[-1] // out_chunks
                for c in range(out_chunks):
                    out_refs[0][c] = acc[:, c * width:(c + 1) * width].astype(out_refs[0].dtype)
                return
            vals = epi(acc, *[e[...] for e in ex]) if epi is not None else (acc,)
            for idx, (o, v) in enumerate(zip(out_refs, vals)):
                if idx in acc_outs:
                    @pl.when(i == 0)
                    def _():
                        o[...] = v.astype(o.dtype)

                    @pl.when(i != 0)
                    def _():
                        o[...] += v.astype(o.dtype)
                else:
                    o[...] = v.astype(o.dtype)

        if gk == 1:
            finish(part)
        else:
            @pl.when(k == 0)
            def _():
                acc_ref[...] = part

            @pl.when(k != 0)
            def _():
                acc_ref[...] += part

            @pl.when(k == gk - 1)
            def _():
                finish(acc_ref[...])

    scratch = []
    if gk > 1:
        tm = a_blk[0][-1] if kind == "tn" else a_blk[0][-2]
        tn = b_blk[0][-2] if kind == "nt" else b_blk[0][-1]
        scratch = [pltpu.VMEM((tm, tn), F32)]
    b_kw = dict(pipeline_mode=pl.Buffered(1)) if b_resident else {}
    return _pcall(body, name, (gj, gi, gk) if j_outer else (gi, gj, gk), [a, b] + [e for e, _, _ in extras],
                  [spec(*a_blk), spec(*b_blk, **b_kw)] + [spec(blk, fn) for _, blk, fn in extras],
                  [s for s, _, _ in outs], [spec(blk, fn) for _, blk, fn in outs], scratch, comm=comm)


def _rms_fwd_vals(x, g):
    r = lax.rsqrt(jnp.mean(x * x, axis=-1, keepdims=True) + EPS)
    return x * r * g


def _rms_bwd_vals(x, g, du):
    r = lax.rsqrt(jnp.mean(x * x, axis=-1, keepdims=True) + EPS)
    xh = x * r
    dxh = du * g
    dx = r * (dxh - xh * jnp.mean(dxh * xh, axis=-1, keepdims=True))
    return dx, jnp.sum(du * xh, axis=0, keepdims=True)


def _rms_fwd(name, x, g, rows, comm=_NO_COMM):
    n = x.shape[0]

    def body(x_ref, g_ref, o_ref):
        o_ref[...] = _rms_fwd_vals(x_ref[...], g_ref[...]).astype(BF16)

    return _pcall(body, name, (n // rows,), [x, g],
                  [pl.BlockSpec((rows, D_MODEL), lambda i: (i, 0)), pl.BlockSpec((1, D_MODEL), lambda i: (0, 0))],
                  [jax.ShapeDtypeStruct(x.shape, BF16)], [pl.BlockSpec((rows, D_MODEL), lambda i: (i, 0))], comm=comm)


def _rope_tables(seq):
    half = ROT_DIM // 2
    pos = np.arange(seq, dtype=np.float32)
    inv_freq = np.float32(ROPE_THETA) ** (-np.arange(0, ROT_DIM, 2, dtype=np.float32) / np.float32(ROT_DIM))
    ang = (pos[:, None] * inv_freq[None, :]).astype(np.float32)
    cos, sin = np.cos(ang), np.sin(ang)
    rest = HEAD_DIM - ROT_DIM
    c = np.concatenate([cos, cos, np.ones((seq, rest), np.float32)], axis=1)
    s1 = np.concatenate([np.zeros((seq, half), np.float32), sin, np.zeros((seq, rest), np.float32)], axis=1)
    s2 = np.concatenate([-sin, np.zeros((seq, half + rest), np.float32)], axis=1)
    return jnp.asarray(c), jnp.asarray(s1), jnp.asarray(s2)


def _group_shapes(seq, width, dtype):
    return [jax.ShapeDtypeStruct((d, seq // d, width), dtype) for d in DILATIONS]


def _group_specs(width):
    return [pl.BlockSpec((d, ROW_BLK // d, width), lambda i: (0, i, 0)) for d in DILATIONS]


def _qkv_prep(z, tabs, comm=_NO_COMM):
    seq = z.shape[0]

    def body(z_ref, c_ref, s1_ref, s2_ref, a0, a1, a2, sc):
        outs = (a0, a1, a2)
        c, s1, s2 = c_ref[...], s1_ref[...], s2_ref[...]
        for part in range(3):
            for hh in range(N_GROUPS * HEADS_PER_GROUP):
                g, hl = divmod(hh, HEADS_PER_GROUP)
                col = part * ATTN_W + hh * HEAD_DIM
                ocol = part * GROUP_W + hl * HEAD_DIM
                x = z_ref[:, col:col + HEAD_DIM].astype(F32)
                if part < 2:
                    x = x * c + pltpu.roll(x, ROT_DIM // 2, 1) * s1 + pltpu.roll(x, HEAD_DIM - ROT_DIM // 2, 1) * s2
                d = DILATIONS[g]
                if d == 1:
                    outs[g][0, :, ocol:ocol + HEAD_DIM] = x.astype(BF16)
                else:
                    sc[...] = x
                    for r in range(d):
                        outs[g][r, :, ocol:ocol + HEAD_DIM] = sc[pl.ds(r, ROW_BLK // d, stride=d), :].astype(BF16)

    tab_spec = pl.BlockSpec((ROW_BLK, HEAD_DIM), lambda i: (i, 0))
    return _pcall(body, "qkv_prep", (seq // ROW_BLK,), [z, *tabs],
                  [pl.BlockSpec((ROW_BLK, QKV_W), lambda i: (i, 0)), tab_spec, tab_spec, tab_spec],
                  _group_shapes(seq, ATTN_W, BF16), _group_specs(ATTN_W), [pltpu.VMEM((ROW_BLK, HEAD_DIM), F32)], comm=comm)


def _band_masks_2(t):
    qi = lax.broadcasted_iota(jnp.int32, (BAND, 2 * BAND), 0)
    kj = lax.broadcasted_iota(jnp.int32, (BAND, 2 * BAND), 1)
    band = jnp.logical_and(kj >= qi, kj <= qi + BAND)
    return band, jnp.logical_and(band, jnp.logical_or(kj >= BAND, t > 0))


def _attn_fwd(name, a_g, comm=_NO_COMM):
    dil, m_len, _ = a_g.shape
    qb = min(QB, m_len // BAND)
    rows = qb * BAND
    steps = m_len // rows
    scale = HEAD_DIM ** -0.5

    tiles = [(sb, h) for sb in range(qb) for h in range(HEADS_PER_GROUP)]

    def body(q_ref, kc_ref, vc_ref, kp_ref, vp_ref, o_ref, l_ref, k_all, v_all, s_scr, p_scr, r_scr):
        t = pl.program_id(1)
        k_all[0:BAND, :] = kp_ref[...]
        k_all[BAND:, :] = kc_ref[...]
        v_all[0:BAND, :] = vp_ref[...]
        v_all[BAND:, :] = vc_ref[...]
        band, band_first = _band_masks_2(t)
        for idx, (sb, h) in enumerate(tiles):
            cs = slice(h * HEAD_DIM, (h + 1) * HEAD_DIM)
            s = _dot(q_ref[sb * BAND:(sb + 1) * BAND, cs], k_all[sb * BAND:(sb + 2) * BAND, cs], "nt") * scale
            s_scr[idx] = jnp.where(band_first if sb == 0 else band, s, NEG)
        lane = lax.broadcasted_iota(jnp.int32, (BAND, HEAD_DIM), 1)
        lse_rows = [jnp.zeros((BAND, HEAD_DIM), F32)] * qb
        for idx, (sb, h) in enumerate(tiles):
            s = s_scr[idx]
            mx = jnp.max(s, axis=-1, keepdims=True)
            p = jnp.exp(s - mx)
            den = jnp.sum(p, axis=-1, keepdims=True)
            p_scr[idx] = p.astype(BF16)
            r_scr[idx] = jnp.broadcast_to(1.0 / den, (BAND, HEAD_DIM))
            lse_rows[sb] = jnp.where(lane == h, jnp.broadcast_to(mx + jnp.log(den), (BAND, HEAD_DIM)), lse_rows[sb])
        for sb in range(qb):
            l_ref[sb * BAND:(sb + 1) * BAND, :] = lse_rows[sb]
        for idx, (sb, h) in enumerate(tiles):
            cs = slice(h * HEAD_DIM, (h + 1) * HEAD_DIM)
            o = _dot(p_scr[idx], v_all[sb * BAND:(sb + 2) * BAND, cs], "nn") * r_scr[idx]
            o_ref[sb * BAND:(sb + 1) * BAND, cs] = o.astype(BF16)

    def prev(r, t):
        return jnp.maximum(qb * t - 1, 0)

    cur = lambda c: pl.BlockSpec((None, rows, GROUP_W), lambda r, t, c=c: (r, t, c))
    prv = lambda c: pl.BlockSpec((None, BAND, GROUP_W), lambda r, t, c=c: (r, prev(r, t), c))
    out_spec = lambda width: pl.BlockSpec((None, rows, width), lambda r, t: (r, t, 0))
    shp = lambda width, dt: jax.ShapeDtypeStruct((dil, m_len, width), dt)
    n_t = len(tiles)
    return _pcall(body, name, (dil, steps), [a_g] * 5, [cur(0), cur(1), cur(2), prv(1), prv(2)],
                  [shp(GROUP_W, BF16), shp(HEAD_DIM, F32)], [out_spec(GROUP_W), out_spec(HEAD_DIM)],
                  [pltpu.VMEM((rows + BAND, GROUP_W), BF16), pltpu.VMEM((rows + BAND, GROUP_W), BF16),
                   pltpu.VMEM((n_t, BAND, 2 * BAND), F32), pltpu.VMEM((n_t, BAND, 2 * BAND), BF16),
                   pltpu.VMEM((n_t, BAND, HEAD_DIM), F32)], comm=comm)


def _attn_merge(os_, ls_, seq):
    def body(o0, l0, o1, l1, o2, l2, at_ref, lt_ref, sc, lsc):
        for gi, l_r in enumerate((l1, l2)):
            d = DILATIONS[gi + 1]
            for r in range(d):
                lsc.at[gi][pl.ds(r, ROW_BLK // d, stride=d), :] = l_r[r]
        lse = (l0.at[0], lsc.at[0], lsc.at[1])
        lane = lax.broadcasted_iota(jnp.int32, (ROW_BLK, HEAD_DIM), 1)
        lt_rows = jnp.zeros((ROW_BLK, HEAD_DIM), F32)
        for h in range(HEADS_PER_GROUP):
            cs = slice(h * HEAD_DIM, (h + 1) * HEAD_DIM)
            for gi, o_r in enumerate((o1, o2)):
                d = DILATIONS[gi + 1]
                for r in range(d):
                    sc.at[gi][pl.ds(r, ROW_BLK // d, stride=d), :] = o_r[r, :, cs].astype(F32)
            l_h = [v[:, h:h + 1] for v in lse]
            mx = jnp.maximum(jnp.maximum(l_h[0], l_h[1]), l_h[2])
            e = [jnp.exp(v - mx) for v in l_h]
            tot = e[0] + e[1] + e[2]
            inv = 1.0 / tot
            at_ref[:, cs] = ((e[0] * inv) * o0[0, :, cs].astype(F32) + (e[1] * inv) * sc[0] + (e[2] * inv) * sc[1]).astype(BF16)
            lt_rows = jnp.where(lane == h, jnp.broadcast_to(mx + jnp.log(tot), (ROW_BLK, HEAD_DIM)), lt_rows)
        lt_ref[...] = lt_rows

    go, gl = _group_specs(GROUP_W), _group_specs(HEAD_DIM)
    return pl.pallas_call(
        body, name="attn_merge",
        out_shape=[jax.ShapeDtypeStruct((seq, GROUP_W), BF16), jax.ShapeDtypeStruct((seq, HEAD_DIM), F32)],
        grid=(seq // ROW_BLK,), in_specs=[go[0], gl[0], go[1], gl[1], go[2], gl[2]],
        out_specs=[pl.BlockSpec((ROW_BLK, GROUP_W), lambda i: (i, 0)), pl.BlockSpec((ROW_BLK, HEAD_DIM), lambda i: (i, 0))],
        scratch_shapes=[pltpu.VMEM((2, ROW_BLK, HEAD_DIM), F32), pltpu.VMEM((2, ROW_BLK, HEAD_DIM), F32)],
        compiler_params=_params(dimension_semantics=("arbitrary",)),
    )(os_[0], ls_[0], os_[1], ls_[1], os_[2], ls_[2])


def _attn_bwd_prep(dattn, attn, lt):
    seq = dattn.shape[0]

    def body(da_ref, at_ref, lt_ref, cl0, d1, cl1, d2, cl2, sc, csc):
        lane = lax.broadcasted_iota(jnp.int32, (ROW_BLK, HEAD_DIM), 1)
        cl = pltpu.roll(lt_ref[...], HEADS_PER_GROUP, 1)
        for h in range(HEADS_PER_GROUP):
            cs = slice(h * HEAD_DIM, (h + 1) * HEAD_DIM)
            da = da_ref[:, cs].astype(F32)
            cc = jnp.sum(da * at_ref[:, cs].astype(F32), axis=-1, keepdims=True)
            cl = jnp.where(lane == h, jnp.broadcast_to(cc, (ROW_BLK, HEAD_DIM)), cl)
            sc[...] = da
            for g, d_ref in ((1, d1), (2, d2)):
                d = DILATIONS[g]
                for r in range(d):
                    d_ref[r, :, cs] = sc[pl.ds(r, ROW_BLK // d, stride=d), :].astype(BF16)
        cl0[0] = cl
        csc[...] = cl
        for g, c_ref in ((1, cl1), (2, cl2)):
            d = DILATIONS[g]
            for r in range(d):
                c_ref[r] = csc[pl.ds(r, ROW_BLK // d, stride=d), :]

    go, gl = _group_specs(GROUP_W), _group_specs(HEAD_DIM)
    row = lambda width: pl.BlockSpec((ROW_BLK, width), lambda i: (i, 0))
    shape = lambda g, width, dt: jax.ShapeDtypeStruct((DILATIONS[g], seq // DILATIONS[g], width), dt)
    cl0, d1, cl1, d2, cl2 = pl.pallas_call(
        body, name="attn_bwd_prep",
        out_shape=[shape(0, HEAD_DIM, F32), shape(1, GROUP_W, BF16), shape(1, HEAD_DIM, F32), shape(2, GROUP_W, BF16),
                   shape(2, HEAD_DIM, F32)],
        grid=(seq // ROW_BLK,), in_specs=[row(GROUP_W), row(GROUP_W), row(HEAD_DIM)],
        out_specs=[gl[0], go[1], gl[1], go[2], gl[2]],
        scratch_shapes=[pltpu.VMEM((ROW_BLK, HEAD_DIM), F32), pltpu.VMEM((ROW_BLK, HEAD_DIM), F32)],
        compiler_params=_params(dimension_semantics=("arbitrary",)),
    )(dattn, attn, lt)
    return [(dattn[None], cl0), (d1, cl1), (d2, cl2)]


def _attn_bwd(name, a_g, da_g, cl_g, comm=_NO_COMM):
    dil, m_len, _ = a_g.shape
    qb = min(QB, m_len // BAND)
    rows = qb * BAND
    steps = m_len // rows
    scale = HEAD_DIM ** -0.5

    tiles = [(sb, h) for sb in range(qb) for h in range(HEADS_PER_GROUP)]

    def body(q_ref, kc_ref, vc_ref, kp_ref, vp_ref, da_ref, cl_ref, d_ref, dk_acc, dv_acc, car_k, car_v,
             k_all, v_all, s_scr, dp_scr, p_scr, ds_scr):
        tg = pl.program_id(1)
        t = steps - 1 - tg

        @pl.when(tg == 0)
        def _():
            car_k[...] = jnp.zeros_like(car_k)
            car_v[...] = jnp.zeros_like(car_v)

        k_all[0:BAND, :] = kp_ref[...]
        k_all[BAND:, :] = kc_ref[...]
        v_all[0:BAND, :] = vp_ref[...]
        v_all[BAND:, :] = vc_ref[...]
        zero = jnp.zeros((rows, GROUP_W), F32)
        dk_acc[0:rows, :] = zero
        dv_acc[0:rows, :] = zero
        dk_acc[rows:rows + BAND, :] = car_k[...]
        dv_acc[rows:rows + BAND, :] = car_v[...]
        band, band_first = _band_masks_2(t)
        for idx, (sb, h) in enumerate(tiles):
            cs = slice(h * HEAD_DIM, (h + 1) * HEAD_DIM)
            rs, ks = slice(sb * BAND, (sb + 1) * BAND), slice(sb * BAND, (sb + 2) * BAND)
            s_scr[idx] = _dot(q_ref[rs, cs], k_all[ks, cs], "nt")
            dp_scr[idx] = _dot(da_ref[rs, cs], v_all[ks, cs], "nt")
        for idx, (sb, h) in enumerate(tiles):
            cs = slice(h * HEAD_DIM, (h + 1) * HEAD_DIM)
            rs = slice(sb * BAND, (sb + 1) * BAND)
            cc = jnp.broadcast_to(cl_ref[rs, h:h + 1], (BAND, 2 * BAND))
            ltv = jnp.broadcast_to(cl_ref[rs, HEADS_PER_GROUP + h:HEADS_PER_GROUP + h + 1], (BAND, 2 * BAND))
            p = jnp.exp(jnp.where(band_first if sb == 0 else band, s_scr[idx] * scale - ltv, NEG))
            p_scr[idx] = p.astype(BF16)
            ds_scr[idx] = (p * (dp_scr[idx] - cc) * scale).astype(BF16)
        for idx, (sb, h) in enumerate(tiles):
            cs = slice(h * HEAD_DIM, (h + 1) * HEAD_DIM)
            rs, ks = slice(sb * BAND, (sb + 1) * BAND), slice(sb * BAND, (sb + 2) * BAND)
            d_ref[rs, cs] = _dot(ds_scr[idx], k_all[ks, cs], "nn").astype(BF16)
            dk_acc[ks, cs] += _dot(ds_scr[idx], q_ref[rs, cs], "tn")
            dv_acc[ks, cs] += _dot(p_scr[idx], da_ref[rs, cs], "tn")
        d_ref[:, GROUP_W:2 * GROUP_W] = dk_acc[BAND:rows + BAND, :].astype(BF16)
        d_ref[:, 2 * GROUP_W:3 * GROUP_W] = dv_acc[BAND:rows + BAND, :].astype(BF16)
        car_k[...] = dk_acc[0:BAND, :]
        car_v[...] = dv_acc[0:BAND, :]

    def rev(tg):
        return steps - 1 - tg

    def prev(tg):
        return jnp.maximum(qb * rev(tg) - 1, 0)

    cur = lambda c: pl.BlockSpec((None, rows, GROUP_W), lambda r, tg, c=c: (r, rev(tg), c))
    prv = lambda c: pl.BlockSpec((None, BAND, GROUP_W), lambda r, tg, c=c: (r, prev(tg), c))
    return _pcall(
        body, name, (dil, steps), [a_g, a_g, a_g, a_g, a_g, da_g, cl_g],
        [cur(0), cur(1), cur(2), prv(1), prv(2), cur(0), pl.BlockSpec((None, rows, HEAD_DIM), lambda r, tg: (r, rev(tg), 0))],
        [jax.ShapeDtypeStruct((dil, m_len, ATTN_W), BF16)], [pl.BlockSpec((None, rows, ATTN_W), lambda r, tg: (r, rev(tg), 0))],
        [pltpu.VMEM((rows + BAND, GROUP_W), F32), pltpu.VMEM((rows + BAND, GROUP_W), F32),
         pltpu.VMEM((BAND, GROUP_W), F32), pltpu.VMEM((BAND, GROUP_W), F32),
         pltpu.VMEM((rows + BAND, GROUP_W), BF16), pltpu.VMEM((rows + BAND, GROUP_W), BF16),
         pltpu.VMEM((len(tiles), BAND, 2 * BAND), F32), pltpu.VMEM((len(tiles), BAND, 2 * BAND), F32),
         pltpu.VMEM((len(tiles), BAND, 2 * BAND), BF16), pltpu.VMEM((len(tiles), BAND, 2 * BAND), BF16)], comm=comm)


def _dqkv_post(d_gs, tabs, dz):
    seq = dz.shape[0]

    def body(g0, g1, g2, c_ref, s1_ref, s2_ref, dz_any, o_ref, sc):
        del dz_any
        ins = (g0, g1, g2)
        c, s1, s2 = c_ref[...], s1_ref[...], s2_ref[...]
        for part in range(3):
            for hh in range(N_GROUPS * HEADS_PER_GROUP):
                g, hl = divmod(hh, HEADS_PER_GROUP)
                icol = part * GROUP_W + hl * HEAD_DIM
                ocol = part * ATTN_W + hh * HEAD_DIM
                d = DILATIONS[g]
                if d == 1:
                    x = ins[g][0, :, icol:icol + HEAD_DIM].astype(F32)
                else:
                    for r in range(d):
                        sc[pl.ds(r, ROW_BLK // d, stride=d), :] = ins[g][r, :, icol:icol + HEAD_DIM].astype(F32)
                    x = sc[...]
                if part < 2:
                    x = x * c + pltpu.roll(x * s1, HEAD_DIM - ROT_DIM // 2, 1) + pltpu.roll(x * s2, ROT_DIM // 2, 1)
                o_ref[:, ocol:ocol + HEAD_DIM] = x.astype(BF16)

    tab_spec = pl.BlockSpec((ROW_BLK, HEAD_DIM), lambda i: (i, 0))
    return pl.pallas_call(
        body, name="dqkv_post", out_shape=jax.ShapeDtypeStruct(dz.shape, BF16), grid=(seq // ROW_BLK,),
        in_specs=_group_specs(ATTN_W) + [tab_spec, tab_spec, tab_spec, pl.BlockSpec(memory_space=pl.ANY)],
        out_specs=pl.BlockSpec((ROW_BLK, QKV_W), lambda i: (i, 0)),
        scratch_shapes=[pltpu.VMEM((ROW_BLK, HEAD_DIM), F32)], input_output_aliases={6: 0},
        compiler_params=_params(dimension_semantics=("arbitrary",)),
    )(*d_gs, *tabs, dz)


def _glu(zg):
    a = zg[:, :CONV_CH].astype(F32)
    s = _sigmoid(zg[:, CONV_CH:].astype(F32))
    return a, s, a * s


def _shifted_copies(xs):
    n = xs.shape[1] - SUBLANES
    for b in range(1, SUBLANES):
        xs[b, 0:n, :] = xs[0, pl.ds(b, n), :]


def _shifted(xs, offset, r0, cs):
    a, b = divmod(offset, SUBLANES)
    return xs[b, pl.ds(SUBLANES * a + r0, CONV_ROWS), cs]


def _conv_fwd(z, cw, cb, lg, lb, comm=_NO_COMM):
    seq = z.shape[0]
    halo_per_blk = ROW_BLK // CONV_HALO

    def body(zg_ref, zh_ref, cw_ref, cb_ref, lg_ref, lb_ref, c2_ref, c4_ref, xs):
        i = pl.program_id(0)
        _, _, c1 = _glu(zg_ref[...])
        _, _, c1h = _glu(zh_ref[...])
        xs[0, 0:CONV_HALO, :] = jnp.where(i > 0, c1h, 0.0)
        xs[0, CONV_HALO:, :] = c1
        _shifted_copies(xs)
        for s in range(CONV_CH // HEAD_DIM):
            cs = slice(s * HEAD_DIM, (s + 1) * HEAD_DIM)
            taps = [cw_ref[j:j + 1, cs] for j in range(CONV_K)]
            bias = cb_ref[:, cs]

            def chunk(rc, carry, cs=cs, taps=taps, bias=bias):
                r0 = pl.multiple_of(rc * CONV_ROWS, CONV_ROWS)
                acc = [jnp.zeros((CONV_ROWS, HEAD_DIM), F32)] * 2
                for j in range(CONV_K):
                    acc[j % 2] = acc[j % 2] + taps[j] * _shifted(xs, CONV_HALO - (CONV_K - 1) + j, r0, cs)
                c2_ref[pl.ds(r0, CONV_ROWS), cs] = acc[0] + acc[1] + bias
                return carry

            lax.fori_loop(0, ROW_BLK // CONV_ROWS, chunk, 0)
        c2 = c2_ref[...]
        mu = jnp.mean(c2, axis=-1, keepdims=True)
        xc = c2 - mu
        rstd = lax.rsqrt(jnp.mean(xc * xc, axis=-1, keepdims=True) + EPS)
        c3 = xc * rstd * lg_ref[...] + lb_ref[...]
        c4_ref[...] = (c3 * _sigmoid(c3)).astype(BF16)

    vec = pl.BlockSpec((1, CONV_CH), lambda i: (0, 0))
    return _pcall(
        body, "conv_fwd", (seq // ROW_BLK,), [z, z, cw, cb, lg, lb],
        [pl.BlockSpec((ROW_BLK, 2 * CONV_CH), lambda i: (i, GLU_COL_BLK)),
         pl.BlockSpec((CONV_HALO, 2 * CONV_CH), lambda i: (jnp.maximum(i * halo_per_blk - 1, 0), GLU_COL_BLK)),
         pl.BlockSpec((CONV_HALO, CONV_CH), lambda i: (0, 0)), vec, vec, vec],
        [jax.ShapeDtypeStruct((seq, CONV_CH), F32), jax.ShapeDtypeStruct((seq, CONV_CH), BF16)],
        [pl.BlockSpec((ROW_BLK, CONV_CH), lambda i: (i, 0)), pl.BlockSpec((ROW_BLK, CONV_CH), lambda i: (i, 0))],
        [pltpu.VMEM((SUBLANES, ROW_BLK + CONV_HALO, CONV_CH), F32)], comm=comm)


def _conv_bwd(dc2, z, cw, dz, comm=_NO_COMM):
    seq = z.shape[0]
    halo_per_blk = ROW_BLK // CONV_HALO
    n_blk = seq // ROW_BLK
    last_halo = seq // CONV_HALO - 1

    def body(dc_ref, dn_ref, zg_ref, zh_ref, cw_ref, dz_any, o_ref, dcw_ref, xs, ys, dc1_ref, dcw_acc):
        del dz_any
        i = pl.program_id(0)
        a, s, c1 = _glu(zg_ref[...])
        _, _, c1h = _glu(zh_ref[...])
        xs[0, 0:CONV_HALO, :] = jnp.where(i > 0, c1h, 0.0)
        xs[0, CONV_HALO:, :] = c1
        ys[0, 0:ROW_BLK, :] = dc_ref[...]
        ys[0, ROW_BLK:, :] = jnp.where(i < n_blk - 1, dn_ref[...], 0.0)
        _shifted_copies(xs)
        _shifted_copies(ys)

        @pl.when(i == 0)
        def _():
            dcw_acc[...] = jnp.zeros_like(dcw_acc)

        for sl in range(CONV_CH // HEAD_DIM):
            cs = slice(sl * HEAD_DIM, (sl + 1) * HEAD_DIM)
            taps = [cw_ref[j:j + 1, cs] for j in range(CONV_K)]

            def chunk(rc, carry, cs=cs, taps=taps):
                r0 = pl.multiple_of(rc * CONV_ROWS, CONV_ROWS)
                dc = ys[0, pl.ds(r0, CONV_ROWS), cs]
                acc = [jnp.zeros((CONV_ROWS, HEAD_DIM), F32)] * 2
                for j in range(CONV_K):
                    prod = dc * _shifted(xs, CONV_HALO - (CONV_K - 1) + j, r0, cs)
                    dcw_acc[j, :, cs] += jnp.sum(prod.reshape(CONV_ROWS // SUBLANES, SUBLANES, HEAD_DIM), axis=0)
                    acc[j % 2] = acc[j % 2] + taps[j] * _shifted(ys, CONV_K - 1 - j, r0, cs)
                dc1_ref[pl.ds(r0, CONV_ROWS), cs] = acc[0] + acc[1]
                return carry

            lax.fori_loop(0, ROW_BLK // CONV_ROWS, chunk, 0)
        dc1 = dc1_ref[...]
        o_ref[:, :CONV_CH] = (dc1 * s).astype(BF16)
        o_ref[:, CONV_CH:] = (dc1 * a * s * (1.0 - s)).astype(BF16)

        @pl.when(i == n_blk - 1)
        def _():
            dcw_ref[...] = jnp.sum(dcw_acc[...], axis=1)

    return _pcall(
        body, "conv_bwd", (n_blk,), [dc2, dc2, z, z, cw, dz],
        [pl.BlockSpec((ROW_BLK, CONV_CH), lambda i: (i, 0)),
         pl.BlockSpec((CONV_HALO, CONV_CH), lambda i: (jnp.minimum((i + 1) * halo_per_blk, last_halo), 0)),
         pl.BlockSpec((ROW_BLK, 2 * CONV_CH), lambda i: (i, GLU_COL_BLK)),
         pl.BlockSpec((CONV_HALO, 2 * CONV_CH), lambda i: (jnp.maximum(i * halo_per_blk - 1, 0), GLU_COL_BLK)),
         pl.BlockSpec((CONV_HALO, CONV_CH), lambda i: (0, 0)),
         pl.BlockSpec(memory_space=pl.ANY)],
        [jax.ShapeDtypeStruct(dz.shape, BF16), jax.ShapeDtypeStruct((CONV_HALO, CONV_CH), F32)],
        [pl.BlockSpec((ROW_BLK, 2 * CONV_CH), lambda i: (i, GLU_COL_BLK)), pl.BlockSpec((CONV_HALO, CONV_CH), lambda i: (0, 0))],
        [pltpu.VMEM((SUBLANES, ROW_BLK + CONV_HALO, CONV_CH), F32), pltpu.VMEM((SUBLANES, ROW_BLK + CONV_HALO, CONV_CH), F32),
         pltpu.VMEM((ROW_BLK, CONV_CH), F32), pltpu.VMEM((CONV_HALO, SUBLANES, CONV_CH), F32)],
        aliases={5: 0}, comm=comm)


def _epi_mix(ya, c4, wcp, gates, bg):
    yc = _dot(c4, wcp, "nn")
    gv = _sigmoid(gates.astype(F32) + bg)
    merged = gv[:, :D_MODEL] * ya + gv[:, D_MODEL:] * yc
    return merged, ya, yc


def _epi_residual_rms(acc, xres, g):
    x = xres + acc
    return x, _rms_fwd_vals(x, g)


def _cross_scores(cq, ck):
    out = []
    for h in range(CROSS_HEADS):
        cs = slice(h * CROSS_HD, (h + 1) * CROSS_HD)
        s = _dot(cq[:, cs], ck[:, cs], "nt") * (CROSS_HD ** -0.5)
        e = jnp.exp(s - jnp.max(s, axis=-1, keepdims=True))
        out.append((cs, e, jnp.sum(e, axis=-1, keepdims=True)))
    return out


def _epi_cross_fwd(acc, ck, cv):
    cq = acc.astype(BF16)
    co = [_dot(e, cv[:, cs], "nn") / den for cs, e, den in _cross_scores(cq, ck)]
    return cq, jnp.concatenate(co, axis=1)


def _epi_cross_bwd(dco, cq, ck, cv):
    dco = dco.astype(BF16)
    dcq, dck, dcv = [], [], []
    for cs, e, den in _cross_scores(cq, ck):
        p = e / den
        dp = _dot(dco[:, cs], cv[:, cs], "nt")
        ds = (p * (dp - jnp.sum(dp * p, axis=-1, keepdims=True)) * (CROSS_HD ** -0.5)).astype(BF16)
        dcq.append(_dot(ds, ck[:, cs], "nn"))
        dck.append(_dot(ds, cq[:, cs], "tn"))
        dcv.append(_dot(p, dco[:, cs], "tn"))
    return jnp.concatenate(dcq, axis=1), jnp.concatenate(dck, axis=1), jnp.concatenate(dcv, axis=1)


def _epi_mlp_up(acc):
    return acc, jnp.square(jnp.maximum(acc, 0.0))


def _epi_final(acc, x2, tgt, g):
    x3 = x2 + acc
    err = _rms_fwd_vals(x3, g) - tgt
    loss = (0.5 / D_MODEL) * jnp.sum(err * err)
    dx3, dg = _rms_bwd_vals(x3, g, err * (1.0 / D_MODEL))
    return dx3, jnp.full((1, HEAD_DIM), loss, F32), dg


def _epi_mlp_down_bwd(dh, hpre):
    return (dh * 2.0 * jnp.maximum(hpre.astype(F32), 0.0),)


def _epi_rms_bwd(du, x, g, dres):
    dx, dg = _rms_bwd_vals(x, g, du)
    return dres.astype(F32) + dx, dg


def _epi_rms_bwd_g(du, x, g):
    return (_rms_bwd_vals(x, g, du)[1],)


def _epi_mix_bwd(dm, ya, yc, gates, bg):
    gv = _sigmoid(gates.astype(F32) + bg)
    ga, gb = gv[:, :D_MODEL], gv[:, D_MODEL:]
    ya, yc = ya.astype(F32), yc.astype(F32)
    dgate = jnp.concatenate([dm * ya * ga * (1.0 - ga), dm * yc * gb * (1.0 - gb)], axis=1)
    return dm * ga, dm * gb, dgate, jnp.sum(dgate, axis=0, keepdims=True)


def _epi_ln_bwd(dc4, c2, lg, lb):
    mu = jnp.mean(c2, axis=-1, keepdims=True)
    xc = c2 - mu
    rstd = lax.rsqrt(jnp.mean(xc * xc, axis=-1, keepdims=True) + EPS)
    xh = xc * rstd
    c3 = xh * lg + lb
    sg = _sigmoid(c3)
    dc3 = dc4 * sg * (1.0 + c3 * (1.0 - sg))
    dxh = dc3 * lg
    dc2 = rstd * (dxh - jnp.mean(dxh, axis=-1, keepdims=True) - xh * jnp.mean(dxh * xh, axis=-1, keepdims=True))
    return (dc2, jnp.sum(dc3 * xh, axis=0, keepdims=True), jnp.sum(dc3, axis=0, keepdims=True),
            jnp.sum(dc2, axis=0, keepdims=True))


def _sds(shape, dtype):
    return jax.ShapeDtypeStruct(shape, dtype)


class _Lazy:
    def __init__(self, fn):
        self.fn = fn

    def __getitem__(self, key):
        return self.fn(key)


def _local_step(x, mem, tgt, sm, plan):
    w = _Lazy(plan.w)
    dw = {}

    def carry(name, n_own, fn, *args, **kw):
        c = plan.comm(name, dw)
        res = fn(*args, comm=c, **kw)
        plan.done(name, res[n_own:])
        return res[:n_own]

    def mm(name, *args, **kw):
        return carry(name, len(args[6]), _mm, name, *args, **kw)

    seq = x.shape[0]
    nr = seq // ROW_BLK
    big = min(1024, seq)
    nb = seq // big
    row = lambda n: ((ROW_BLK, n), lambda i, j, k: (i, 0))
    vec = lambda n: ((1, n), lambda i, j, k: (0, 0))
    full = lambda r, c: ((r, c), lambda i, j, k: (0, 0))
    gates_blk = ((ROW_BLK, 2 * D_MODEL), lambda i, j, k: (i, GATE_COL_BLK))
    tabs = _rope_tables(seq)

    whole3 = lambda a: (a.shape, lambda i, j, k: (0, 0, 0))
    u, z = plan.project_in(x, sm["g_mix"])
    a_gs = carry("qkv_prep", 3, _qkv_prep, z, tabs)
    os_, ls_ = [], []
    for g in range(N_GROUPS):
        name = "attn_fwd_%d" % g
        o_g, l_g = carry(name, 2, _attn_fwd, name, a_gs[g])
        os_.append(o_g)
        ls_.append(l_g)
    attn, lt = _attn_merge(os_, ls_, seq)
    c2, c4 = carry("conv_fwd", 2, _conv_fwd, z, w["taps"], sm["conv_b"], sm["conv_ln_g"], sm["conv_ln_b"])
    merged, ya, yc = mm(
        "mix", attn, w["w_attn_proj"], "nn", (nr, 1, 1), row(GROUP_W), full(GROUP_W, D_MODEL),
        [(_sds((seq, D_MODEL), BF16), *row(D_MODEL))] * 3,
        extras=[(c4, *row(CONV_CH)), (w["w_conv_proj"], *full(CONV_CH, D_MODEL)), (z, *gates_blk), (sm["b_gate"], *vec(2 * D_MODEL))],
        epi=_epi_mix)
    x1, uq = mm("out_proj", merged, w["w_out"], "nn", (nr, 1, 1), row(D_MODEL), full(D_MODEL, D_MODEL),
                 [(_sds((seq, D_MODEL), F32), *row(D_MODEL)), (_sds((seq, D_MODEL), BF16), *row(D_MODEL))],
                 extras=[(x, *row(D_MODEL)), (sm["g_cross"], *vec(D_MODEL))], epi=_epi_residual_rms)

    mn = _rms_fwd("rms_mem", mem, sm["g_mem"], N_MEM)[0]
    ckv = mm("ckv_proj", mn, w["w_ckv"], "nn", (1, N_DEV, 1), full(N_MEM, D_MODEL),
              ((None, D_MODEL, 2 * D_MODEL // N_DEV), lambda i, j, k: (j, 0, 0)),
              [(_sds((N_MEM, 2 * D_MODEL), BF16), (N_MEM, 2 * D_MODEL // N_DEV), lambda i, j, k: (0, j))])[0]
    ck, cv = ckv[:, :D_MODEL], ckv[:, D_MODEL:]
    kv_blk = full(N_MEM, D_MODEL)
    cq, co = mm("cq_proj_cross", uq, w["w_cq"], "nn", (nr, 1, 1), row(D_MODEL), full(D_MODEL, D_MODEL),
                 [(_sds((seq, D_MODEL), BF16), *row(D_MODEL))] * 2,
                 extras=[(ck, *kv_blk), (cv, *kv_blk)], epi=_epi_cross_fwd)
    x2, um = mm("co_proj", co, w["w_co"], "nn", (nr, 1, 1), row(D_MODEL), full(D_MODEL, D_MODEL),
                 [(_sds((seq, D_MODEL), F32), *row(D_MODEL)), (_sds((seq, D_MODEL), BF16), *row(D_MODEL))],
                 extras=[(x1, *row(D_MODEL)), (sm["g_mlp"], *vec(D_MODEL))], epi=_epi_residual_rms)

    ff_blk = D_FF // N_DEV
    row_f32 = (_sds((seq, D_MODEL), F32), *row(D_MODEL))
    row_bf16 = (_sds((seq, D_MODEL), BF16), *row(D_MODEL))
    col_sum = (_sds((1, D_MODEL), F32), *vec(D_MODEL))
    hpre, h = mm("mlp_up", um, w["w_up"], "nn", (nr, 1, 1), row(D_MODEL), whole3(w["w_up"]),
                 [(_sds((seq, D_FF), BF16), *row(D_FF))] * 2, epi=_epi_mlp_up, split=("cols", N_DEV), b_resident=True)
    kt = D_FF // D_MODEL
    dx3, loss, dg_final = mm(
        "mlp_down_loss", h, w["w_down"], "nn", (nr, 1, 1), row(D_FF), full(D_FF, D_MODEL),
        [row_bf16, (_sds((1, HEAD_DIM), F32), *vec(HEAD_DIM)), col_sum],
        extras=[(x2, *row(D_MODEL)), (tgt, *row(D_MODEL)), (sm["g_final"], *vec(D_MODEL))], epi=_epi_final, acc_outs=(1, 2),
        b_resident=True)

    dhpre = mm("mlp_down_bwd", dx3, w["w_down"], "nt", (nr, 1, 1), row(D_MODEL), full(D_FF, D_MODEL),
               [(_sds((seq, D_FF), BF16), *row(D_FF))], extras=[(hpre, *row(D_FF))], epi=_epi_mlp_down_bwd,
               split=("cols", kt), b_resident=True)[0]
    big2 = min(2 * big, seq)
    nb2 = seq // big2
    dw["w_down"] = mm("dw_down", h, dx3, "tn", (kt, 1, nb2), ((big2, D_MODEL), lambda i, j, k: (k, i)),
                      ((big2, D_MODEL), lambda i, j, k: (k, 0)),
                      [(_sds((D_FF, D_MODEL), BF16), (D_MODEL, D_MODEL), lambda i, j, k: (i, 0))])[0]
    dx2, dg_mlp = mm("mlp_up_bwd", dhpre, w["w_up"], "nt", (nr, 1, 1), row(D_FF), whole3(w["w_up"]),
                     [row_bf16, col_sum],
                     extras=[(x2, *row(D_MODEL)), (sm["g_mlp"], *vec(D_MODEL)), (dx3, *row(D_MODEL))],
                     epi=_epi_rms_bwd, acc_outs=(1,), split=("sum", N_DEV), b_resident=True)
    dw["w_up"] = mm("dw_up", um, dhpre, "tn", (1, N_DEV, nb2), ((big2, D_MODEL), lambda i, j, k: (k, 0)),
                    ((big2, ff_blk), lambda i, j, k: (k, j)),
                    [(_sds((N_DEV, D_MODEL, ff_blk), BF16), (None, D_MODEL, ff_blk), lambda i, j, k: (j, 0, 0))])[0]

    acc_kv = (_sds((N_MEM, D_MODEL), F32), *kv_blk)
    dcq, dck, dcv = mm("co_proj_bwd_cross", dx2, w["w_co"], "nt", (nr, 1, 1), row(D_MODEL), full(D_MODEL, D_MODEL),
                       [row_bf16, acc_kv, acc_kv],
                       extras=[(cq, *row(D_MODEL)), (ck, *kv_blk), (cv, *kv_blk)], epi=_epi_cross_bwd, acc_outs=(1, 2))

    def dw_square(name, act, grad):
        return mm(name, act, grad, "tn", (1, 1, nb2), ((big2, D_MODEL), lambda i, j, k: (k, 0)),
                  ((big2, D_MODEL), lambda i, j, k: (k, 0)), [(_sds((D_MODEL, D_MODEL), BF16), *full(D_MODEL, D_MODEL))])[0]

    dw["w_co"] = dw_square("dw_co", co, dx2)
    dx1, dg_cross = mm("cq_proj_bwd", dcq, w["w_cq"], "nt", (nr, 1, 1), row(D_MODEL), full(D_MODEL, D_MODEL),
                       [row_bf16, col_sum],
                       extras=[(x1, *row(D_MODEL)), (sm["g_cross"], *vec(D_MODEL)), (dx2, *row(D_MODEL))],
                       epi=_epi_rms_bwd, acc_outs=(1,))
    dw["w_cq"] = dw_square("dw_cq", uq, dcq)
    dckv = jnp.concatenate([dck, dcv], axis=1)
    kv_chunk = 2 * D_MODEL // N_DEV
    dw["w_ckv"] = mm("dw_ckv", mn, dckv, "tn", (1, N_DEV, 1), full(N_MEM, D_MODEL), ((N_MEM, kv_chunk), lambda i, j, k: (0, j)),
                      [(_sds((N_DEV, D_MODEL, kv_chunk), BF16), (None, D_MODEL, kv_chunk), lambda i, j, k: (j, 0, 0))])[0]
    dg_mem = mm("ckv_proj_bwd", dckv, w["w_ckv"], "nt", (1, 1, N_DEV), ((N_MEM, kv_chunk), lambda i, j, k: (0, k)),
                 ((None, D_MODEL, kv_chunk), lambda i, j, k: (k, 0, 0)), [(_sds((1, D_MODEL), F32), *vec(D_MODEL))],
                 extras=[(mem, *full(N_MEM, D_MODEL)), (sm["g_mem"], *vec(D_MODEL))], epi=_epi_rms_bwd_g, acc_outs=(0,))[0]

    dya, dyc, dz, db_gate = mm(
        "out_proj_bwd_mix", dx1, w["w_out"], "nt", (nr, 1, 1), row(D_MODEL), full(D_MODEL, D_MODEL),
        [(_sds((seq, D_MODEL), BF16), *row(D_MODEL)), (_sds((seq, D_MODEL), BF16), *row(D_MODEL)),
         (_sds((seq, IN_W), BF16), *gates_blk), (_sds((1, 2 * D_MODEL), F32), *vec(2 * D_MODEL))],
        extras=[(ya, *row(D_MODEL)), (yc, *row(D_MODEL)), (z, *gates_blk), (sm["b_gate"], *vec(2 * D_MODEL))],
        epi=_epi_mix_bwd, acc_outs=(3,))
    dw["w_out"] = dw_square("dw_out", merged, dx1)
    dattn = mm("attn_proj_bwd", dya, w["w_attn_proj"], "nt", (nr, 1, 1), row(D_MODEL), full(GROUP_W, D_MODEL),
                [(_sds((seq, GROUP_W), BF16), *row(GROUP_W))])[0]
    pc = D_MODEL // N_DEV
    dw["w_attn_proj"] = mm("dw_attn_proj", attn, dya, "tn", (1, 1, nb2), ((big2, GROUP_W), lambda i, j, k: (k, 0)),
                           ((big2, D_MODEL), lambda i, j, k: (k, 0)),
                           [(_sds((N_DEV, GROUP_W, pc), BF16), (N_DEV, GROUP_W, pc), lambda i, j, k: (0, 0, 0))],
                           out_chunks=N_DEV)[0]
    cvec = (_sds((1, CONV_CH), F32), *vec(CONV_CH))
    dc2, dg_ln_g, dg_ln_b, dg_conv_b = mm(
        "conv_proj_bwd_ln", dyc, w["w_conv_proj"], "nt", (nr, 1, 1), row(D_MODEL), full(CONV_CH, D_MODEL),
        [(_sds((seq, CONV_CH), F32), *row(CONV_CH)), cvec, cvec, cvec],
        extras=[(c2, *row(CONV_CH)), (sm["conv_ln_g"], *vec(CONV_CH)), (sm["conv_ln_b"], *vec(CONV_CH))],
        epi=_epi_ln_bwd, acc_outs=(1, 2, 3))
    dw["w_conv_proj"] = mm("dw_conv_proj", c4, dyc, "tn", (1, 1, nb2), ((big2, CONV_CH), lambda i, j, k: (k, 0)),
                           ((big2, D_MODEL), lambda i, j, k: (k, 0)),
                           [(_sds((N_DEV, CONV_CH, pc), BF16), (N_DEV, CONV_CH, pc), lambda i, j, k: (0, 0, 0))],
                           out_chunks=N_DEV)[0]
    dz, dg_conv_w = carry("conv_bwd", 2, _conv_bwd, dc2, z, w["taps"], dz)
    preps = _attn_bwd_prep(dattn, attn, lt)
    d_gs = []
    for g in range(N_GROUPS):
        name = "attn_bwd_%d" % g
        d_gs.append(carry(name, 1, _attn_bwd, name, a_gs[g], *preps[g])[0])
    dz = _dqkv_post(d_gs, tabs, dz)
    dw["w_in"] = mm("dw_in", u, dz, "tn", (1, N_DEV, nb2), ((big2, D_MODEL), lambda i, j, k: (k, 0)),
                    ((big2, D_MODEL), lambda i, j, k: (k, j)),
                    [(_sds((N_DEV, D_MODEL, D_MODEL), BF16), (None, D_MODEL, D_MODEL), lambda i, j, k: (j, 0, 0))])[0]
    token = plan.start_w_in(dw["w_in"])
    grad_x, dg_mix = mm("in_proj_bwd", dz, w["w_in"], "nt", (nr, 1, 1), row(IN_W), whole3(w["w_in"]), [row_f32, col_sum],
                        extras=[(x, *row(D_MODEL)), (sm["g_mix"] + token, *vec(D_MODEL)), (dx1, *row(D_MODEL))],
                        epi=_epi_rms_bwd, acc_outs=(1,), split=("sum", N_DEV), b_resident=True)
    small = dict(g_mix=dg_mix, b_gate=db_gate, conv_b=dg_conv_b, conv_ln_g=dg_ln_g, conv_ln_b=dg_ln_b, g_cross=dg_cross,
                 g_mem=dg_mem, g_mlp=dg_mlp, g_final=dg_final, loss=loss, conv_w=dg_conv_w)
    return grad_x, dw, small


SHARD_SHAPE = dict(w_in=(1024, 1024), w_attn_proj=(512, 128), w_conv_proj=(768, 128), w_out=(128, 1024), w_cq=(128, 1024),
                   w_ckv=(1024, 256), w_co=(128, 1024), w_up=(1024, 512), w_down=(512, 1024))
FWD_CARRY = {"in_proj":("w_attn_proj", "w_conv_proj", "w_out", "w_cq", "w_ckv", "w_co", "taps"),
             "qkv_prep": ("w_up",), "conv_fwd": ("w_down",)}
BWD_CARRY = {"dw_up": ("w_down",), "out_proj_bwd_mix": ("w_co", "w_cq"), "conv_bwd": ("w_up", "w_ckv"),
             "attn_bwd_0": ("w_out",), "attn_bwd_1": ("w_attn_proj", "w_conv_proj")}


def _cols_to_2d(a):
    return a.transpose(1, 0, 2).reshape(a.shape[1], -1)


def _in_proj_gather(resid, g, w_shard, comm):
    seq = resid.shape[0]
    tm = min(1024, seq)
    x, y, c = lax.axis_index("x"), lax.axis_index("y"), lax.axis_index("c")
    ident = lambda px, py, pc: 4 * px + 2 * py + pc
    far = [(1 - x, y), (x, 1 - y), (1 - x, 1 - y)]
    order = jnp.stack([ident(x, y, c), ident(x, y, 1 - c), ident(*far[0], c), ident(*far[1], c), ident(*far[0], 1 - c),
                       ident(*far[1], 1 - c), ident(*far[2], c), ident(*far[2], 1 - c)]).astype(jnp.int32)
    forward_at = {2: 0, 3: 1, 6: 2}
    n_far = len(far)

    def body(order_ref, x_ref, g_ref, wsh_ref, *rest):
        c_in, z_ref, u_ref, wg_ref = rest[:comm.n], rest[comm.n], rest[comm.n + 1], rest[comm.n + 2]
        c_out = rest[comm.n + 3:2 * comm.n + 3]
        wbuf, u_all, load_sem, local_sem, recv_sems, ici_send, d2d_send = rest[2 * comm.n + 3:2 * comm.n + 10]
        sems = rest[2 * comm.n + 10:]
        jj, i = pl.program_id(0), pl.program_id(1)
        (kx, ky, kc), me, chips = _Comm._where()
        sibling = (kx, ky, 1 - kc)
        n = order_ref[jj]

        def push(src, blk, send, to):
            return pltpu.make_async_remote_copy(src_ref=src, dst_ref=wg_ref.at[blk], send_sem=send,
                                                recv_sem=recv_sems.at[blk], device_id=to, device_id_type=MESH)

        def load(src):
            cp = pltpu.make_async_copy(src, wbuf, load_sem)
            cp.start()
            cp.wait()

        @pl.when(jnp.logical_and(jj == 0, i == 0))
        def _():
            push(wsh_ref, me, d2d_send, sibling).start()
            for (px, py) in chips[:2]:
                push(wsh_ref, me, ici_send, (px, py, kc)).start()
            pltpu.make_async_copy(wsh_ref, wg_ref.at[me], local_sem).start()
            load(wsh_ref)

        @pl.when(jnp.logical_and(jj > 0, i == 0))
        def _():
            push(wg_ref.at[n], n, d2d_send, sibling).wait_recv()
            for step, k in forward_at.items():
                @pl.when(jj == step)
                def _(k=k):
                    blk = 4 * chips[k][0] + 2 * chips[k][1] + kc
                    push(wg_ref.at[blk], blk, d2d_send, sibling).start()

            @pl.when(jj == 2)
            def _():
                push(wsh_ref, me, ici_send, (*chips[2], kc)).start()

            if comm.n:
                @pl.when(jj == 3)
                def _():
                    comm.start(c_in, c_out, sems)

            load(wg_ref.at[n])

        rows = pl.ds(pl.multiple_of(i * tm, tm), tm)

        @pl.when(jj == 0)
        def _():
            u_val = _rms_fwd_vals(x_ref[...], g_ref[...]).astype(BF16)
            u_all[rows, :] = u_val
            u_ref[...] = u_val
            z_ref[...] = _dot(u_val, wbuf[...], "nn").astype(BF16)

        @pl.when(jj > 0)
        def _():
            z_ref[...] = _dot(u_all[rows, :], wbuf[...], "nn").astype(BF16)

        @pl.when(jnp.logical_and(jj == N_DEV - 1, i == pl.num_programs(1) - 1))
        def _():
            def drain_sends(send, count):
                blocks = wg_ref.at[pl.ds(0, count)]
                pltpu.make_async_remote_copy(src_ref=blocks, dst_ref=blocks, send_sem=send, recv_sem=recv_sems.at[0],
                                             device_id=sibling, device_id_type=MESH).wait_send()

            drain_sends(ici_send, n_far)
            drain_sends(d2d_send, n_far + 1)
            pltpu.make_async_copy(wsh_ref, wg_ref.at[me], local_sem).wait()
            if comm.n:
                comm.wait(c_in, c_out, sems)

    any_spec = pl.BlockSpec(memory_space=pl.ANY)
    n_i = seq // tm
    first_pass = lambda jj, i, order_ref: (jnp.where(jj == 0, i, n_i - 1), 0)
    grid_spec = pltpu.PrefetchScalarGridSpec(
        num_scalar_prefetch=1, grid=(N_DEV, n_i),
        in_specs=[pl.BlockSpec((tm, D_MODEL), first_pass), pl.BlockSpec((1, D_MODEL), lambda jj, i, order_ref: (0, 0)),
                  any_spec] + comm.in_specs,
        out_specs=[pl.BlockSpec((tm, D_MODEL), lambda jj, i, order_ref: (i, order_ref[jj])),
                   pl.BlockSpec((tm, D_MODEL), first_pass), any_spec] + comm.out_specs,
        scratch_shapes=[pltpu.VMEM((D_MODEL, D_MODEL), BF16), pltpu.VMEM((seq, D_MODEL), BF16), pltpu.SemaphoreType.DMA,
                        pltpu.SemaphoreType.DMA, pltpu.SemaphoreType.DMA((N_DEV,)), pltpu.SemaphoreType.DMA,
                        pltpu.SemaphoreType.DMA] + comm.scratch)
    return pl.pallas_call(
        body, name="in_proj_gather", grid_spec=grid_spec,
        out_shape=[jax.ShapeDtypeStruct((seq, IN_W), BF16), jax.ShapeDtypeStruct((seq, D_MODEL), BF16),
                   jax.ShapeDtypeStruct((N_DEV, D_MODEL, D_MODEL), BF16)] + comm.out_shape,
        compiler_params=_params(dimension_semantics=("arbitrary", "arbitrary")),
    )(order, resid, g, w_shard, *comm.arrays)


_HBM = pl.BlockSpec(memory_space=pltpu.HBM)
_SEM = pl.BlockSpec(memory_space=pltpu.SEMAPHORE)


def _chunks_start(dw):
    def body(src_ref, land_ref, send_sem, recv_sem, src_thru, land_thru, token):
        del src_thru, land_thru
        me, peers = _peers()
        for (px, py, pc) in peers:
            pltpu.make_async_remote_copy(src_ref=src_ref.at[4 * px + 2 * py + pc], dst_ref=land_ref.at[me], send_sem=send_sem,
                                         recv_sem=recv_sem, device_id=(px, py, pc), device_id_type=MESH).start()
        token[...] = jnp.zeros_like(token)

    return pl.pallas_call(
        body, name="w_in_grad_start",
        out_shape=(pltpu.SemaphoreType.DMA(()), pltpu.SemaphoreType.DMA(()), pltpu.HBM(dw.shape, dw.dtype),
                   pltpu.HBM(dw.shape, dw.dtype), jax.ShapeDtypeStruct((SUBLANES, HEAD_DIM), F32)),
        in_specs=(_HBM, _HBM), out_specs=(_SEM, _SEM, _HBM, _HBM, pl.BlockSpec(memory_space=pltpu.VMEM)),
        input_output_aliases={0: 2, 1: 3},
        compiler_params=pltpu.CompilerParams(has_side_effects=pltpu.SideEffectType.DATAFLOW_SIDE_EFFECTING),
    )(pltpu.with_memory_space_constraint(dw, pltpu.HBM),
      pltpu.with_memory_space_constraint(lax.empty(dw.shape, dw.dtype), pltpu.HBM))


def _chunks_wait(send_sem, recv_sem, src_thru, land_thru, after):
    def body(src_ref, land_ref, send_sem, recv_sem, after_ref, src_dead, land_out):
        del after_ref, src_dead, land_out
        seven = land_ref.at[pl.ds(0, N_DEV - 1)]
        cp = pltpu.make_async_remote_copy(src_ref=seven, dst_ref=seven, send_sem=send_sem, recv_sem=recv_sem,
                                          device_id=_peers()[1][0], device_id_type=MESH)
        cp.wait_send()
        cp.wait_recv()

    return pl.pallas_call(
        body, name="w_in_grad_wait",
        out_shape=(pltpu.HBM(src_thru.shape, src_thru.dtype), pltpu.HBM(land_thru.shape, land_thru.dtype)),
        in_specs=(_HBM, _HBM, _SEM, _SEM, pl.BlockSpec(memory_space=pl.ANY)), out_specs=(_HBM, _HBM),
        input_output_aliases={0: 0, 1: 1},
        compiler_params=pltpu.CompilerParams(has_side_effects=pltpu.SideEffectType.DATAFLOW_SIDE_EFFECTING),
    )(src_thru, land_thru, send_sem, recv_sem, after)


class _Plan:
    def __init__(self, shards, n_tap_cols):
        self.shards = shards
        self.gathered = {}
        self.parts = {}
        self.n_tap_cols = n_tap_cols

    def project_in(self, x, g):
        comm = self.comm("in_proj", None)
        res = _in_proj_gather(x, g, self.shards["w_in"], comm)
        self.gathered["w_in"] = res[2]
        self.done("in_proj", res[3:])
        return res[1], res[0]

    def start_w_in(self, dw_in):
        *self.in_flight, token = _chunks_start(dw_in)
        return token[0, 0]

    def finish_w_in(self, after):
        src, land = _chunks_wait(*self.in_flight, after)
        me = _peers()[0]
        own = lax.dynamic_slice(src, (me, 0, 0), (1,) + src.shape[1:])
        return lax.dynamic_update_slice(land, own, (me, 0, 0))

    def comm(self, name, dw):
        if name in FWD_CARRY:
            return _Comm(replicated=[self.shards[k] for k in FWD_CARRY[name]])
        if name in BWD_CARRY:
            return _Comm(chunked=[dw[k].reshape((N_DEV,) + SHARD_SHAPE[k]) for k in BWD_CARRY[name]])
        return _NO_COMM

    def done(self, name, got):
        if name in FWD_CARRY:
            self.gathered.update(zip(FWD_CARRY[name], got))
        elif name in BWD_CARRY:
            self.parts.update(zip(BWD_CARRY[name], got))

    def w(self, key):
        g = self.gathered[key]
        if key in ("w_in", "w_up", "w_ckv"):
            return g
        if key in ("w_attn_proj", "w_conv_proj"):
            return _cols_to_2d(g)
        if key == "taps":
            return jnp.pad(_cols_to_2d(g[:, :CONV_K, :self.n_tap_cols]), ((0, 1), (0, 0)))
        return g.reshape(-1, g.shape[-1])


def _adamw(name, w, m, v, parts, comm=_NO_COMM):
    rows, cols = w.shape
    n_parts = parts.shape[0]
    rb = rows if rows <= 256 or rows % 256 else 256

    def body(w_ref, m_ref, v_ref, p_ref, g_ref, d_ref, nm_ref, nv_ref):
        g = p_ref[0].astype(F32)
        for q in range(1, n_parts):
            g = g + p_ref[q].astype(F32)
        wv = w_ref[...]
        nm = ADAM_B1 * m_ref[...] + (1.0 - ADAM_B1) * g
        nv = ADAM_B2 * v_ref[...] + (1.0 - ADAM_B2) * jnp.square(g)
        m_hat = nm / (1.0 - ADAM_B1 ** ADAM_STEP)
        v_hat = nv / (1.0 - ADAM_B2 ** ADAM_STEP)
        g_ref[...] = g
        d_ref[...] = -ADAM_LR * (m_hat / (jnp.sqrt(v_hat) + ADAM_EPS) + ADAM_WD * wv)
        nm_ref[...] = nm
        nv_ref[...] = nv

    blk = pl.BlockSpec((rb, cols), lambda i: (i, 0))
    return _pcall(body, name, (rows // rb,), [w, m, v, parts],
                  [blk, blk, blk, pl.BlockSpec((n_parts, rb, cols), lambda i: (0, i, 0))],
                  [jax.ShapeDtypeStruct((rows, cols), F32)] * 4, [blk] * 4, comm=comm)


def _sum_parts(name, parts):
    def body(p_ref, o_ref):
        acc = p_ref[0]
        for q in range(1, parts.shape[0]):
            acc = acc + p_ref[q]
        o_ref[...] = acc

    return _pcall(body, name, (1,), [parts], [pl.BlockSpec(parts.shape, lambda i: (0, 0, 0))],
                  [jax.ShapeDtypeStruct(parts.shape[1:], F32)], [pl.BlockSpec(parts.shape[1:], lambda i: (0, 0))])[0]


BIG = ("w_in", "w_attn_proj", "w_conv_proj", "w_out", "w_cq", "w_ckv", "w_co", "w_up", "w_down")
SMALL = ("g_mix", "b_gate", "conv_b", "conv_ln_g", "conv_ln_b", "g_cross", "g_mem", "g_mlp", "g_final")
SMALL_ORDER = SMALL + ("loss", "conv_w")
WEIGHTS = ("g_mix", "w_in", "b_gate", "conv_w", "conv_b", "conv_ln_g", "conv_ln_b", "w_attn_proj", "w_conv_proj", "w_out",
           "g_cross", "g_mem", "w_cq", "w_ckv", "w_co", "g_mlp", "w_up", "w_down", "g_final")


def kernel(x, mem, g_mix, w_in, b_gate, conv_w, conv_b, conv_ln_g, conv_ln_b, w_attn_proj, w_conv_proj, w_out, g_cross, g_mem, w_cq, w_ckv, w_co, g_mlp, w_up, w_down, g_final, loss_target, m_g_mix, m_w_in, m_b_gate, m_conv_w, m_conv_b, m_conv_ln_g, m_conv_ln_b, m_w_attn_proj, m_w_conv_proj, m_w_out, m_g_cross, m_g_mem, m_w_cq, m_w_ckv, m_w_co, m_g_mlp, m_w_up, m_w_down, m_g_final, v_g_mix, v_w_in, v_b_gate, v_conv_w, v_conv_b, v_conv_ln_g, v_conv_ln_b, v_w_attn_proj, v_w_conv_proj, v_w_out, v_g_cross, v_g_mem, v_w_cq, v_w_ckv, v_w_co, v_g_mlp, v_w_up, v_w_down, v_g_final):
    args = dict(locals())
    wts = {k: args[k] for k in WEIGHTS}
    mom = {k: args["m_" + k] for k in WEIGHTS}
    var = {k: args["v_" + k] for k in WEIGHTS}
    two_d = lambda a: a.reshape(a.shape[-2:]) if a.ndim == 3 else a.reshape(1, -1)

    shards = {k: two_d(wts[k]).astype(BF16) for k in BIG}
    shards["taps"] = jnp.pad(two_d(conv_w), ((0, 1), (0, HEAD_DIM - conv_w.shape[-1])))
    plan = _Plan(shards, conv_w.shape[-1])
    sm = {k: two_d(wts[k]) for k in SMALL}

    grad_x, _, small = _local_step(x[0], mem[0], loss_target[0], sm, plan)
    parts = plan.parts

    out = {}
    small_comm = _Comm(replicated=[small[k] for k in SMALL_ORDER])
    for k in BIG[1:]:
        res = _adamw("adamw_" + k, two_d(wts[k]), two_d(mom[k]), two_d(var[k]), parts[k],
                     comm=small_comm if k == BIG[1] else _NO_COMM)
        out[k] = [r.reshape(wts[k].shape) for r in res[:4]]
        if k == BIG[1]:
            small_parts = dict(zip(SMALL_ORDER, res[4:]))
    for k in SMALL:
        res = _adamw("adamw_" + k, two_d(wts[k]), two_d(mom[k]), two_d(var[k]), small_parts[k])
        out[k] = [r.reshape(wts[k].shape) for r in res]
    res = _adamw("adamw_w_in", two_d(w_in), two_d(m_w_in), two_d(v_w_in), plan.finish_w_in(after=out["g_final"][3]))
    out["w_in"] = [r.reshape(w_in.shape) for r in res]
    loss = _sum_parts("loss_sum", small_parts["loss"])[0, 0]
    me = 4 * lax.axis_index("x") + 2 * lax.axis_index("y") + lax.axis_index("c")
    n_tap_cols = conv_w.shape[-1]
    tap_parts = lax.dynamic_slice(small_parts["conv_w"], (0, 0, me * n_tap_cols), (N_DEV, CONV_K, n_tap_cols))
    res = _adamw("adamw_conv_w", two_d(conv_w), two_d(m_conv_w), two_d(v_conv_w), tap_parts)
    out["conv_w"] = [r.reshape(conv_w.shape) for r in res]

    return (loss, grad_x[None], *[out[k][0] for k in WEIGHTS], *[out[k][1] for k in WEIGHTS],
            *[out[k][2] for k in WEIGHTS], *[out[k][3] for k in WEIGHTS])
```

```python
import functools

import jax
import jax.numpy as jnp
import numpy as np
from jax import lax
from jax.experimental import pallas as pl
from jax.experimental.pallas import tpu as pltpu

F32 = jnp.float32
BF16 = jnp.bfloat16

N_DEV = 8
D_MODEL = 1024
N_MEM = 256
HEAD_DIM = 128
HEADS_PER_GROUP = 4
GROUP_W = HEADS_PER_GROUP * HEAD_DIM
DILATIONS = (1, 4, 16)
BAND = 128
N_GROUPS = 3
ATTN_W = N_GROUPS * GROUP_W
QKV_W = 3 * ATTN_W
ROT_DIM = HEAD_DIM // 4
ROPE_THETA = 500000.0
CONV_CH = 768
CONV_K = 31
CONV_HALO = 32
SUBLANES = 8
CONV_ROWS = 64
IN_W = 8192
GLU_COL_BLK = QKV_W // (2 * CONV_CH)
GATE_COL_BLK = (QKV_W + 2 * CONV_CH) // (2 * D_MODEL)
CROSS_HEADS = 4
CROSS_HD = D_MODEL // CROSS_HEADS
D_FF = 4096
EPS = 1e-6
NEG = -1e30
QB = 4
ROW_BLK = QB * BAND

ADAM_LR = 0.001
ADAM_B1 = 0.9
ADAM_B2 = 0.999
ADAM_EPS = 1e-08
ADAM_WD = 0.01
ADAM_STEP = 10

VMEM_LIMIT = 56 * 1024 * 1024
MESH = pl.DeviceIdType.MESH


def _params(**kw):
    return pltpu.CompilerParams(vmem_limit_bytes=VMEM_LIMIT, **kw)


def _sigmoid(x):
    return 1.0 / (1.0 + jnp.exp(-x))


def _dot(a, b, kind):
    dims = {"nn": (((1,), (0,)), ((), ())), "nt": (((1,), (1,)), ((), ())), "tn": (((0,), (0,)), ((), ()))}[kind]
    if a.dtype != BF16:
        a = a.astype(BF16)
    if b.dtype != BF16:
        b = b.astype(BF16)
    return lax.dot_general(a, b, dims, preferred_element_type=F32)


def _peers():
    x, y, c = lax.axis_index("x"), lax.axis_index("y"), lax.axis_index("c")
    me = 4 * x + 2 * y + c
    peers = [(x, y, 1 - c), (1 - x, y, c), (x, 1 - y, c), (1 - x, 1 - y, c),
             (1 - x, y, 1 - c), (x, 1 - y, 1 - c), (1 - x, 1 - y, 1 - c)]
    return me, peers


class _Comm:
    def __init__(self, chunked=(), replicated=()):
        self.arrays = list(chunked) + list(replicated)
        self.n_c = len(chunked)
        self.n = len(self.arrays)
        self.out_shape = [jax.ShapeDtypeStruct(a.shape, a.dtype) for a in chunked]
        self.out_shape += [jax.ShapeDtypeStruct((N_DEV,) + a.shape, a.dtype) for a in replicated]
        self.in_specs = [pl.BlockSpec(memory_space=pl.ANY)] * self.n
        self.out_specs = [pl.BlockSpec(memory_space=pl.ANY)] * self.n
        self.scratch = [pltpu.SemaphoreType.DMA((self.n,))] * 5 if self.n else []

    @staticmethod
    def _where():
        x, y, c = lax.axis_index("x"), lax.axis_index("y"), lax.axis_index("c")
        chips = [(1 - x, y), (x, 1 - y), (1 - x, 1 - y)]
        return (x, y, c), 4 * x + 2 * y + c, chips

    def _local(self, ins, outs, sems, a, me):
        src = ins[a].at[me] if a < self.n_c else ins[a]
        return pltpu.make_async_copy(src, outs[a].at[me], sems[2].at[a])

    @staticmethod
    def _remote(src, dst, send, recv, to):
        return pltpu.make_async_remote_copy(src_ref=src, dst_ref=dst, send_sem=send, recv_sem=recv, device_id=to,
                                            device_id_type=MESH)

    def start(self, ins, outs, sems):
        (x, y, c), me, chips = self._where()
        for a in range(self.n):
            self._local(ins, outs, sems, a, me).start()
            if a < self.n_c:
                for (px, py, pc) in _peers()[1]:
                    self._remote(ins[a].at[4 * px + 2 * py + pc], outs[a].at[me], sems[0].at[a], sems[1].at[a], (px, py, pc)).start()
            else:
                self._remote(ins[a], outs[a].at[me], sems[3].at[a], sems[4].at[a], (x, y, 1 - c)).start()
                for (px, py) in chips:
                    self._remote(ins[a], outs[a].at[me], sems[0].at[a], sems[1].at[a], (px, py, c)).start()

    def wait(self, ins, outs, sems):
        (x, y, c), me, chips = self._where()
        sibling = (x, y, 1 - c)

        def drain(a, pair, count):
            blocks = outs[a].at[pl.ds(0, count)]
            cp = self._remote(blocks, blocks, sems[pair].at[a], sems[pair + 1].at[a], sibling)
            cp.wait_send()
            cp.wait_recv()

        for a in range(self.n):
            if a < self.n_c:
                drain(a, 0, N_DEV - 1)
            else:
                drain(a, 0, len(chips))
                for (px, py) in chips:
                    blk = outs[a].at[4 * px + 2 * py + c]
                    self._remote(blk, blk, sems[3].at[a], sems[4].at[a], sibling).start()
        for a in range(self.n):
            if a >= self.n_c:
                drain(a, 3, len(chips) + 1)
            self._local(ins, outs, sems, a, me).wait()


_NO_COMM = _Comm()


def _pcall(body, name, grid, operands, in_specs, out_shape, out_specs, scratch=(), aliases=None, comm=_NO_COMM, **params):
    n_in, n_out, n_scr = len(operands), len(out_shape), len(scratch)
    grid = tuple(grid)

    def carried(*refs):
        ins, c_in = refs[:n_in], refs[n_in:n_in + comm.n]
        o0 = n_in + comm.n
        outs, c_out = refs[o0:o0 + n_out], refs[o0 + n_out:o0 + n_out + comm.n]
        s0 = o0 + n_out + comm.n
        scr, sems = refs[s0:s0 + n_scr], refs[s0 + n_scr:]
        ids = [pl.program_id(ax) for ax in range(len(grid))]

        @pl.when(functools.reduce(jnp.logical_and, [p == 0 for p in ids]))
        def _():
            comm.start(c_in, c_out, sems)

        body(*ins, *outs, *scr)

        @pl.when(functools.reduce(jnp.logical_and, [p == g - 1 for p, g in zip(ids, grid)]))
        def _():
            comm.wait(c_in, c_out, sems)

    return pl.pallas_call(
        carried if comm.n else body, name=name, grid=grid, in_specs=list(in_specs) + comm.in_specs,
        out_shape=list(out_shape) + comm.out_shape, out_specs=list(out_specs) + comm.out_specs,
        scratch_shapes=list(scratch) + comm.scratch, input_output_aliases=aliases or {},
        compiler_params=_params(dimension_semantics=("arbitrary",) * len(grid), **params),
    )(*operands, *comm.arrays)


def _mm(name, a, b, kind, grid, a_blk, b_blk, outs, extras=(), epi=None, acc_outs=(), j_outer=False, comm=_NO_COMM,
        split=None, b_resident=False, out_chunks=0):
    gi, gj, gk = grid
    n_ex = len(extras)
    n_out = len(outs)
    mode, n_chunks = split if split is not None else (None, 1)

    def spec(blk, fn, **kw):
        return pl.BlockSpec(blk, (lambda j, i, k: fn(i, j, k)) if j_outer else fn, **kw)

    def b_chunk(b_ref, c):
        if len(b_ref.shape) == 3:
            return b_ref[c]
        rows, cols = b_ref.shape
        if (kind == "nn") == (mode == "cols"):
            return b_ref[:, c * (cols // n_chunks):(c + 1) * (cols // n_chunks)]
        return b_ref[c * (rows // n_chunks):(c + 1) * (rows // n_chunks), :]

    def col_chunk(ref, c):
        width = ref.shape[-1] // n_chunks
        return slice(c * width, (c + 1) * width)

    def body(*refs):
        a_ref, b_ref = refs[0], refs[1]
        ex = refs[2:2 + n_ex]
        out_refs = refs[2 + n_ex:2 + n_ex + n_out]
        acc_ref = refs[2 + n_ex + n_out] if gk > 1 else None
        i = pl.program_id(1 if j_outer else 0)
        k = pl.program_id(2)
        if mode == "cols":
            a_val = a_ref[...]
            for c in range(n_chunks):
                acc = _dot(a_val, b_chunk(b_ref, c), kind)
                vals = epi(acc, *[e[:, col_chunk(e, c)] for e in ex]) if epi is not None else (acc,)
                for o, v in zip(out_refs, vals):
                    o[:, col_chunk(o, c)] = v.astype(o.dtype)
            return
        if mode == "sum":
            part = _dot(a_ref[:, col_chunk(a_ref, 0)], b_chunk(b_ref, 0), kind)
            for c in range(1, n_chunks):
                part = part + _dot(a_ref[:, col_chunk(a_ref, c)], b_chunk(b_ref, c), kind)
        else:
            part = _dot(a_ref[...], b_ref[...], kind)

        def finish(acc):
            if out_chunks:
                width = acc.shape[-1] // out_chunks
                for c in range(out_chunks):
                    out_refs[0][c] = acc[:, c * width:(c + 1) * width].astype(out_refs[0].dtype)
                return
            vals = epi(acc, *[e[...] for e in ex]) if epi is not None else (acc,)
            for idx, (o, v) in enumerate(zip(out_refs, vals)):
                if idx in acc_outs:
                    @pl.when(i == 0)
                    def _():
                        o[...] = v.astype(o.dtype)

                    @pl.when(i != 0)
                    def _():
                        o[...] += v.astype(o.dtype)
                else:
                    o[...] = v.astype(o.dtype)

        if gk == 1:
            finish(part)
        else:
            @pl.when(k == 0)
            def _():
                acc_ref[...] = part

            @pl.when(k != 0)
            def _():
                acc_ref[...] += part

            @pl.when(k == gk - 1)
            def _():
                finish(acc_ref[...])

    scratch = []
    if gk > 1:
        tm = a_blk[0][-1] if kind == "tn" else a_blk[0][-2]
        tn = b_blk[0][-2] if kind == "nt" else b_blk[0][-1]
        scratch = [pltpu.VMEM((tm, tn), F32)]
    b_kw = dict(pipeline_mode=pl.Buffered(1)) if b_resident else {}
    return _pcall(body, name, (gj, gi, gk) if j_outer else (gi, gj, gk), [a, b] + [e for e, _, _ in extras],
                  [spec(*a_blk), spec(*b_blk, **b_kw)] + [spec(blk, fn) for _, blk, fn in extras],
                  [s for s, _, _ in outs], [spec(blk, fn) for _, blk, fn in outs], scratch, comm=comm)


def _rms_fwd_vals(x, g):
    r = lax.rsqrt(jnp.mean(x * x, axis=-1, keepdims=True) + EPS)
    return x * r * g


def _rms_bwd_vals(x, g, du):
    r = lax.rsqrt(jnp.mean(x * x, axis=-1, keepdims=True) + EPS)
    xh = x * r
    dxh = du * g
    dx = r * (dxh - xh * jnp.mean(dxh * xh, axis=-1, keepdims=True))
    return dx, jnp.sum(du * xh, axis=0, keepdims=True)


def _rms_fwd(name, x, g, rows, comm=_NO_COMM):
    n = x.shape[0]

    def body(x_ref, g_ref, o_ref):
        o_ref[...] = _rms_fwd_vals(x_ref[...], g_ref[...]).astype(BF16)

    return _pcall(body, name, (n // rows,), [x, g],
                  [pl.BlockSpec((rows, D_MODEL), lambda i: (i, 0)), pl.BlockSpec((1, D_MODEL), lambda i: (0, 0))],
                  [jax.ShapeDtypeStruct(x.shape, BF16)], [pl.BlockSpec((rows, D_MODEL), lambda i: (i, 0))], comm=comm)


def _rope_tables(seq):
    half = ROT_DIM // 2
    pos = np.arange(seq, dtype=np.float32)
    inv_freq = np.float32(ROPE_THETA) ** (-np.arange(0, ROT_DIM, 2, dtype=np.float32) / np.float32(ROT_DIM))
    ang = (pos[:, None] * inv_freq[None, :]).astype(np.float32)
    cos, sin = np.cos(ang), np.sin(ang)
    rest = HEAD_DIM - ROT_DIM
    c = np.concatenate([cos, cos, np.ones((seq, rest), np.float32)], axis=1)
    s1 = np.concatenate([np.zeros((seq, half), np.float32), sin, np.zeros((seq, rest), np.float32)], axis=1)
    s2 = np.concatenate([-sin, np.zeros((seq, half + rest), np.float32)], axis=1)
    return jnp.asarray(c), jnp.asarray(s1), jnp.asarray(s2)


def _group_shapes(seq, width, dtype):
    return [jax.ShapeDtypeStruct((d, seq // d, width), dtype) for d in DILATIONS]


def _group_specs(width):
    return [pl.BlockSpec((d, ROW_BLK // d, width), lambda i: (0, i, 0)) for d in DILATIONS]


def _qkv_prep(z, tabs, comm=_NO_COMM):
    seq = z.shape[0]

    def body(z_ref, c_ref, s1_ref, s2_ref, a0, a1, a2, sc):
        outs = (a0, a1, a2)
        c, s1, s2 = c_ref[...], s1_ref[...], s2_ref[...]
        for part in range(3):
            for hh in range(N_GROUPS * HEADS_PER_GROUP):
                g, hl = divmod(hh, HEADS_PER_GROUP)
                col = part * ATTN_W + hh * HEAD_DIM
                ocol = part * GROUP_W + hl * HEAD_DIM
                x = z_ref[:, col:col + HEAD_DIM].astype(F32)
                if part < 2:
                    x = x * c + pltpu.roll(x, ROT_DIM // 2, 1) * s1 + pltpu.roll(x, HEAD_DIM - ROT_DIM // 2, 1) * s2
                d = DILATIONS[g]
                if d == 1:
                    outs[g][0, :, ocol:ocol + HEAD_DIM] = x.astype(BF16)
                else:
                    sc[...] = x
                    for r in range(d):
                        outs[g][r, :, ocol:ocol + HEAD_DIM] = sc[pl.ds(r, ROW_BLK // d, stride=d), :].astype(BF16)

    tab_spec = pl.BlockSpec((ROW_BLK, HEAD_DIM), lambda i: (i, 0))
    return _pcall(body, "qkv_prep", (seq // ROW_BLK,), [z, *tabs],
                  [pl.BlockSpec((ROW_BLK, QKV_W), lambda i: (i, 0)), tab_spec, tab_spec, tab_spec],
                  _group_shapes(seq, ATTN_W, BF16), _group_specs(ATTN_W), [pltpu.VMEM((ROW_BLK, HEAD_DIM), F32)], comm=comm)


def _band_masks_2(t):
    qi = lax.broadcasted_iota(jnp.int32, (BAND, 2 * BAND), 0)
    kj = lax.broadcasted_iota(jnp.int32, (BAND, 2 * BAND), 1)
    band = jnp.logical_and(kj >= qi, kj <= qi + BAND)
    return band, jnp.logical_and(band, jnp.logical_or(kj >= BAND, t > 0))


def _attn_fwd(name, a_g, comm=_NO_COMM):
    dil, m_len, _ = a_g.shape
    qb = min(QB, m_len // BAND)
    rows = qb * BAND
    steps = m_len // rows
    scale = HEAD_DIM ** -0.5

    tiles = [(sb, h) for sb in range(qb) for h in range(HEADS_PER_GROUP)]

    def body(q_ref, kc_ref, vc_ref, kp_ref, vp_ref, o_ref, l_ref, k_all, v_all, s_scr, p_scr, r_scr):
        t = pl.program_id(1)
        k_all[0:BAND, :] = kp_ref[...]
        k_all[BAND:, :] = kc_ref[...]
        v_all[0:BAND, :] = vp_ref[...]
        v_all[BAND:, :] = vc_ref[...]
        band, band_first = _band_masks_2(t)
        for idx, (sb, h) in enumerate(tiles):
            cs = slice(h * HEAD_DIM, (h + 1) * HEAD_DIM)
            s = _dot(q_ref[sb * BAND:(sb + 1) * BAND, cs], k_all[sb * BAND:(sb + 2) * BAND, cs], "nt") * scale
            s_scr[idx] = jnp.where(band_first if sb == 0 else band, s, NEG)
        lane = lax.broadcasted_iota(jnp.int32, (BAND, HEAD_DIM), 1)
        lse_rows = [jnp.zeros((BAND, HEAD_DIM), F32)] * qb
        for idx, (sb, h) in enumerate(tiles):
            s = s_scr[idx]
            mx = jnp.max(s, axis=-1, keepdims=True)
            p = jnp.exp(s - mx)
            den = jnp.sum(p, axis=-1, keepdims=True)
            p_scr[idx] = p.astype(BF16)
            r_scr[idx] = jnp.broadcast_to(1.0 / den, (BAND, HEAD_DIM))
            lse_rows[sb] = jnp.where(lane == h, jnp.broadcast_to(mx + jnp.log(den), (BAND, HEAD_DIM)), lse_rows[sb])
        for sb in range(qb):
            l_ref[sb * BAND:(sb + 1) * BAND, :] = lse_rows[sb]
        for idx, (sb, h) in enumerate(tiles):
            cs = slice(h * HEAD_DIM, (h + 1) * HEAD_DIM)
            o = _dot(p_scr[idx], v_all[sb * BAND:(sb + 2) * BAND, cs], "nn") * r_scr[idx]
            o_ref[sb * BAND:(sb + 1) * BAND, cs] = o.astype(BF16)

    def prev(r, t):
        return jnp.maximum(qb * t - 1, 0)

    cur = lambda c: pl.BlockSpec((None, rows, GROUP_W), lambda r, t, c=c: (r, t, c))
    prv = lambda c: pl.BlockSpec((None, BAND, GROUP_W), lambda r, t, c=c: (r, prev(r, t), c))
    out_spec = lambda width: pl.BlockSpec((None, rows, width), lambda r, t: (r, t, 0))
    shp = lambda width, dt: jax.ShapeDtypeStruct((dil, m_len, width), dt)
    n_t = len(tiles)
    return _pcall(body, name, (dil, steps), [a_g] * 5, [cur(0), cur(1), cur(2), prv(1), prv(2)],
                  [shp(GROUP_W, BF16), shp(HEAD_DIM, F32)], [out_spec(GROUP_W), out_spec(HEAD_DIM)],
                  [pltpu.VMEM((rows + BAND, GROUP_W), BF16), pltpu.VMEM((rows + BAND, GROUP_W), BF16),
                   pltpu.VMEM((n_t, BAND, 2 * BAND), F32), pltpu.VMEM((n_t, BAND, 2 * BAND), BF16),
                   pltpu.VMEM((n_t, BAND, HEAD_DIM), F32)], comm=comm)


def _attn_merge(os_, ls_, seq):
    def body(o0, l0, o1, l1, o2, l2, at_ref, lt_ref, sc, lsc):
        for gi, l_r in enumerate((l1, l2)):
            d = DILATIONS[gi + 1]
            for r in range(d):
                lsc.at[gi][pl.ds(r, ROW_BLK // d, stride=d), :] = l_r[r]
        lse = (l0.at[0], lsc.at[0], lsc.at[1])
        lane = lax.broadcasted_iota(jnp.int32, (ROW_BLK, HEAD_DIM), 1)
        lt_rows = jnp.zeros((ROW_BLK, HEAD_DIM), F32)
        for h in range(HEADS_PER_GROUP):
            cs = slice(h * HEAD_DIM, (h + 1) * HEAD_DIM)
            for gi, o_r in enumerate((o1, o2)):
                d = DILATIONS[gi + 1]
                for r in range(d):
                    sc.at[gi][pl.ds(r, ROW_BLK // d, stride=d), :] = o_r[r, :, cs].astype(F32)
            l_h = [v[:, h:h + 1] for v in lse]
            mx = jnp.maximum(jnp.maximum(l_h[0], l_h[1]), l_h[2])
            e = [jnp.exp(v - mx) for v in l_h]
            tot = e[0] + e[1] + e[2]
            inv = 1.0 / tot
            at_ref[:, cs] = ((e[0] * inv) * o0[0, :, cs].astype(F32) + (e[1] * inv) * sc[0] + (e[2] * inv) * sc[1]).astype(BF16)
            lt_rows = jnp.where(lane == h, jnp.broadcast_to(mx + jnp.log(tot), (ROW_BLK, HEAD_DIM)), lt_rows)
        lt_ref[...] = lt_rows

    go, gl = _group_specs(GROUP_W), _group_specs(HEAD_DIM)
    return pl.pallas_call(
        body, name="attn_merge",
        out_shape=[jax.ShapeDtypeStruct((seq, GROUP_W), BF16), jax.ShapeDtypeStruct((seq, HEAD_DIM), F32)],
        grid=(seq // ROW_BLK,), in_specs=[go[0], gl[0], go[1], gl[1], go[2], gl[2]],
        out_specs=[pl.BlockSpec((ROW_BLK, GROUP_W), lambda i: (i, 0)), pl.BlockSpec((ROW_BLK, HEAD_DIM), lambda i: (i, 0))],
        scratch_shapes=[pltpu.VMEM((2, ROW_BLK, HEAD_DIM), F32), pltpu.VMEM((2, ROW_BLK, HEAD_DIM), F32)],
        compiler_params=_params(dimension_semantics=("arbitrary",)),
    )(os_[0], ls_[0], os_[1], ls_[1], os_[2], ls_[2])


def _attn_bwd_prep(dattn, attn, lt):
    seq = dattn.shape[0]

    def body(da_ref, at_ref, lt_ref, cl0, d1, cl1, d2, cl2, sc, csc):
        lane = lax.broadcasted_iota(jnp.int32, (ROW_BLK, HEAD_DIM), 1)
        cl = pltpu.roll(lt_ref[...], HEADS_PER_GROUP, 1)
        for h in range(HEADS_PER_GROUP):
            cs = slice(h * HEAD_DIM, (h + 1) * HEAD_DIM)
            da = da_ref[:, cs].astype(F32)
            cc = jnp.sum(da * at_ref[:, cs].astype(F32), axis=-1, keepdims=True)
            cl = jnp.where(lane == h, jnp.broadcast_to(cc, (ROW_BLK, HEAD_DIM)), cl)
            sc[...] = da
            for g, d_ref in ((1, d1), (2, d2)):
                d = DILATIONS[g]
                for r in range(d):
                    d_ref[r, :, cs] = sc[pl.ds(r, ROW_BLK // d, stride=d), :].astype(BF16)
        cl0[0] = cl
        csc[...] = cl
        for g, c_ref in ((1, cl1), (2, cl2)):
            d = DILATIONS[g]
            for r in range(d):
                c_ref[r] = csc[pl.ds(r, ROW_BLK // d, stride=d), :]

    go, gl = _group_specs(GROUP_W), _group_specs(HEAD_DIM)
    row = lambda width: pl.BlockSpec((ROW_BLK, width), lambda i: (i, 0))
    shape = lambda g, width, dt: jax.ShapeDtypeStruct((DILATIONS[g], seq // DILATIONS[g], width), dt)
    cl0, d1, cl1, d2, cl2 = pl.pallas_call(
        body, name="attn_bwd_prep",
        out_shape=[shape(0, HEAD_DIM, F32), shape(1, GROUP_W, BF16), shape(1, HEAD_DIM, F32), shape(2, GROUP_W, BF16),
                   shape(2, HEAD_DIM, F32)],
        grid=(seq // ROW_BLK,), in_specs=[row(GROUP_W), row(GROUP_W), row(HEAD_DIM)],
        out_specs=[gl[0], go[1], gl[1], go[2], gl[2]],
        scratch_shapes=[pltpu.VMEM((ROW_BLK, HEAD_DIM), F32), pltpu.VMEM((ROW_BLK, HEAD_DIM), F32)],
        compiler_params=_params(dimension_semantics=("arbitrary",)),
    )(dattn, attn, lt)
    return [(dattn[None], cl0), (d1, cl1), (d2, cl2)]


def _attn_bwd(name, a_g, da_g, cl_g, comm=_NO_COMM):
    dil, m_len, _ = a_g.shape
    qb = min(QB, m_len // BAND)
    rows = qb * BAND
    steps = m_len // rows
    scale = HEAD_DIM ** -0.5

    tiles = [(sb, h) for sb in range(qb) for h in range(HEADS_PER_GROUP)]

    def body(q_ref, kc_ref, vc_ref, kp_ref, vp_ref, da_ref, cl_ref, d_ref, dk_acc, dv_acc, car_k, car_v,
             k_all, v_all, s_scr, dp_scr, p_scr, ds_scr):
        tg = pl.program_id(1)
        t = steps - 1 - tg

        @pl.when(tg == 0)
        def _():
            car_k[...] = jnp.zeros_like(car_k)
            car_v[...] = jnp.zeros_like(car_v)

        k_all[0:BAND, :] = kp_ref[...]
        k_all[BAND:, :] = kc_ref[...]
        v_all[0:BAND, :] = vp_ref[...]
        v_all[BAND:, :] = vc_ref[...]
        zero = jnp.zeros((rows, GROUP_W), F32)
        dk_acc[0:rows, :] = zero
        dv_acc[0:rows, :] = zero
        dk_acc[rows:rows + BAND, :] = car_k[...]
        dv_acc[rows:rows + BAND, :] = car_v[...]
        band, band_first = _band_masks_2(t)
        for idx, (sb, h) in enumerate(tiles):
            cs = slice(h * HEAD_DIM, (h + 1) * HEAD_DIM)
            rs, ks = slice(sb * BAND, (sb + 1) * BAND), slice(sb * BAND, (sb + 2) * BAND)
            s_scr[idx] = _dot(q_ref[rs, cs], k_all[ks, cs], "nt")
            dp_scr[idx] = _dot(da_ref[rs, cs], v_all[ks, cs], "nt")
        for idx, (sb, h) in enumerate(tiles):
            cs = slice(h * HEAD_DIM, (h + 1) * HEAD_DIM)
            rs = slice(sb * BAND, (sb + 1) * BAND)
            cc = jnp.broadcast_to(cl_ref[rs, h:h + 1], (BAND, 2 * BAND))
            ltv = jnp.broadcast_to(cl_ref[rs, HEADS_PER_GROUP + h:HEADS_PER_GROUP + h + 1], (BAND, 2 * BAND))
            p = jnp.exp(jnp.where(band_first if sb == 0 else band, s_scr[idx] * scale - ltv, NEG))
            p_scr[idx] = p.astype(BF16)
            ds_scr[idx] = (p * (dp_scr[idx] - cc) * scale).astype(BF16)
        for idx, (sb, h) in enumerate(tiles):
            cs = slice(h * HEAD_DIM, (h + 1) * HEAD_DIM)
            rs, ks = slice(sb * BAND, (sb + 1) * BAND), slice(sb * BAND, (sb + 2) * BAND)
            d_ref[rs, cs] = _dot(ds_scr[idx], k_all[ks, cs], "nn").astype(BF16)
            dk_acc[ks, cs] += _dot(ds_scr[idx], q_ref[rs, cs], "tn")
            dv_acc[ks, cs] += _dot(p_scr[idx], da_ref[rs, cs], "tn")
        d_ref[:, GROUP_W:2 * GROUP_W] = dk_acc[BAND:rows + BAND, :].astype(BF16)
        d_ref[:, 2 * GROUP_W:3 * GROUP_W] = dv_acc[BAND:rows + BAND, :].astype(BF16)
        car_k[...] = dk_acc[0:BAND, :]
        car_v[...] = dv_acc[0:BAND, :]

    def rev(tg):
        return steps - 1 - tg

    def prev(tg):
        return jnp.maximum(qb * rev(tg) - 1, 0)

    cur = lambda c: pl.BlockSpec((None, rows, GROUP_W), lambda r, tg, c=c: (r, rev(tg), c))
    prv = lambda c: pl.BlockSpec((None, BAND, GROUP_W), lambda r, tg, c=c: (r, prev(tg), c))
    return _pcall(
        body, name, (dil, steps), [a_g, a_g, a_g, a_g, a_g, da_g, cl_g],
        [cur(0), cur(1), cur(2), prv(1), prv(2), cur(0), pl.BlockSpec((None, rows, HEAD_DIM), lambda r, tg: (r, rev(tg), 0))],
        [jax.ShapeDtypeStruct((dil, m_len, ATTN_W), BF16)], [pl.BlockSpec((None, rows, ATTN_W), lambda r, tg: (r, rev(tg), 0))],
        [pltpu.VMEM((rows + BAND, GROUP_W), F32), pltpu.VMEM((rows + BAND, GROUP_W), F32),
         pltpu.VMEM((BAND, GROUP_W), F32), pltpu.VMEM((BAND, GROUP_W), F32),
         pltpu.VMEM((rows + BAND, GROUP_W), BF16), pltpu.VMEM((rows + BAND, GROUP_W), BF16),
         pltpu.VMEM((len(tiles), BAND, 2 * BAND), F32), pltpu.VMEM((len(tiles), BAND, 2 * BAND), F32),
         pltpu.VMEM((len(tiles), BAND, 2 * BAND), BF16), pltpu.VMEM((len(tiles), BAND, 2 * BAND), BF16)], comm=comm)


def _dqkv_post(d_gs, tabs, dz):
    seq = dz.shape[0]

    def body(g0, g1, g2, c_ref, s1_ref, s2_ref, dz_any, o_ref, sc):
        del dz_any
        ins = (g0, g1, g2)
        c, s1, s2 = c_ref[...], s1_ref[...], s2_ref[...]
        for part in range(3):
            for hh in range(N_GROUPS * HEADS_PER_GROUP):
                g, hl = divmod(hh, HEADS_PER_GROUP)
                icol = part * GROUP_W + hl * HEAD_DIM
                ocol = part * ATTN_W + hh * HEAD_DIM
                d = DILATIONS[g]
                if d == 1:
                    x = ins[g][0, :, icol:icol + HEAD_DIM].astype(F32)
                else:
                    for r in range(d):
                        sc[pl.ds(r, ROW_BLK // d, stride=d), :] = ins[g][r, :, icol:icol + HEAD_DIM].astype(F32)
                    x = sc[...]
                if part < 2:
                    x = x * c + pltpu.roll(x * s1, HEAD_DIM - ROT_DIM // 2, 1) + pltpu.roll(x * s2, ROT_DIM // 2, 1)
                o_ref[:, ocol:ocol + HEAD_DIM] = x.astype(BF16)

    tab_spec = pl.BlockSpec((ROW_BLK, HEAD_DIM), lambda i: (i, 0))
    return pl.pallas_call(
        body, name="dqkv_post", out_shape=jax.ShapeDtypeStruct(dz.shape, BF16), grid=(seq // ROW_BLK,),
        in_specs=_group_specs(ATTN_W) + [tab_spec, tab_spec, tab_spec, pl.BlockSpec(memory_space=pl.ANY)],
        out_specs=pl.BlockSpec((ROW_BLK, QKV_W), lambda i: (i, 0)),
        scratch_shapes=[pltpu.VMEM((ROW_BLK, HEAD_DIM), F32)], input_output_aliases={6: 0},
        compiler_params=_params(dimension_semantics=("arbitrary",)),
    )(*d_gs, *tabs, dz)


def _glu(zg):
    a = zg[:, :CONV_CH].astype(F32)
    s = _sigmoid(zg[:, CONV_CH:].astype(F32))
    return a, s, a * s


def _shifted_copies(xs):
    n = xs.shape[1] - SUBLANES
    for b in range(1, SUBLANES):
        xs[b, 0:n, :] = xs[0, pl.ds(b, n), :]


def _shifted(xs, offset, r0, cs):
    a, b = divmod(offset, SUBLANES)
    return xs[b, pl.ds(SUBLANES * a + r0, CONV_ROWS), cs]


def _conv_fwd(z, cw, cb, lg, lb, comm=_NO_COMM):
    seq = z.shape[0]
    halo_per_blk = ROW_BLK // CONV_HALO

    def body(zg_ref, zh_ref, cw_ref, cb_ref, lg_ref, lb_ref, c2_ref, c4_ref, xs):
        i = pl.program_id(0)
        _, _, c1 = _glu(zg_ref[...])
        _, _, c1h = _glu(zh_ref[...])
        xs[0, 0:CONV_HALO, :] = jnp.where(i > 0, c1h, 0.0)
        xs[0, CONV_HALO:, :] = c1
        _shifted_copies(xs)
        for s in range(CONV_CH // HEAD_DIM):
            cs = slice(s * HEAD_DIM, (s + 1) * HEAD_DIM)
            taps = [cw_ref[j:j + 1, cs] for j in range(CONV_K)]
            bias = cb_ref[:, cs]

            def chunk(rc, carry, cs=cs, taps=taps, bias=bias):
                r0 = pl.multiple_of(rc * CONV_ROWS, CONV_ROWS)
                acc = [jnp.zeros((CONV_ROWS, HEAD_DIM), F32)] * 2
                for j in range(CONV_K):
                    acc[j % 2] = acc[j % 2] + taps[j] * _shifted(xs, CONV_HALO - (CONV_K - 1) + j, r0, cs)
                c2_ref[pl.ds(r0, CONV_ROWS), cs] = acc[0] + acc[1] + bias
                return carry

            lax.fori_loop(0, ROW_BLK // CONV_ROWS, chunk, 0)
        c2 = c2_ref[...]
        mu = jnp.mean(c2, axis=-1, keepdims=True)
        xc = c2 - mu
        rstd = lax.rsqrt(jnp.mean(xc * xc, axis=-1, keepdims=True) + EPS)
        c3 = xc * rstd * lg_ref[...] + lb_ref[...]
        c4_ref[...] = (c3 * _sigmoid(c3)).astype(BF16)

    vec = pl.BlockSpec((1, CONV_CH), lambda i: (0, 0))
    return _pcall(
        body, "conv_fwd", (seq // ROW_BLK,), [z, z, cw, cb, lg, lb],
        [pl.BlockSpec((ROW_BLK, 2 * CONV_CH), lambda i: (i, GLU_COL_BLK)),
         pl.BlockSpec((CONV_HALO, 2 * CONV_CH), lambda i: (jnp.maximum(i * halo_per_blk - 1, 0), GLU_COL_BLK)),
         pl.BlockSpec((CONV_HALO, CONV_CH), lambda i: (0, 0)), vec, vec, vec],
        [jax.ShapeDtypeStruct((seq, CONV_CH), F32), jax.ShapeDtypeStruct((seq, CONV_CH), BF16)],
        [pl.BlockSpec((ROW_BLK, CONV_CH), lambda i: (i, 0)), pl.BlockSpec((ROW_BLK, CONV_CH), lambda i: (i, 0))],
        [pltpu.VMEM((SUBLANES, ROW_BLK + CONV_HALO, CONV_CH), F32)], comm=comm)


def _conv_bwd(dc2, z, cw, dz, comm=_NO_COMM):
    seq = z.shape[0]
    halo_per_blk = ROW_BLK // CONV_HALO
    n_blk = seq // ROW_BLK
    last_halo = seq // CONV_HALO - 1

    def body(dc_ref, dn_ref, zg_ref, zh_ref, cw_ref, dz_any, o_ref, dcw_ref, xs, ys, dc1_ref, dcw_acc):
        del dz_any
        i = pl.program_id(0)
        a, s, c1 = _glu(zg_ref[...])
        _, _, c1h = _glu(zh_ref[...])
        xs[0, 0:CONV_HALO, :] = jnp.where(i > 0, c1h, 0.0)
        xs[0, CONV_HALO:, :] = c1
        ys[0, 0:ROW_BLK, :] = dc_ref[...]
        ys[0, ROW_BLK:, :] = jnp.where(i < n_blk - 1, dn_ref[...], 0.0)
        _shifted_copies(xs)
        _shifted_copies(ys)

        @pl.when(i == 0)
        def _():
            dcw_acc[...] = jnp.zeros_like(dcw_acc)

        for sl in range(CONV_CH // HEAD_DIM):
            cs = slice(sl * HEAD_DIM, (sl + 1) * HEAD_DIM)
            taps = [cw_ref[j:j + 1, cs] for j in range(CONV_K)]

            def chunk(rc, carry, cs=cs, taps=taps):
                r0 = pl.multiple_of(rc * CONV_ROWS, CONV_ROWS)
                dc = ys[0, pl.ds(r0, CONV_ROWS), cs]
                acc = [jnp.zeros((CONV_ROWS, HEAD_DIM), F32)] * 2
                for j in range(CONV_K):
                    prod = dc * _shifted(xs, CONV_HALO - (CONV_K - 1) + j, r0, cs)
                    dcw_acc[j, :, cs] += jnp.sum(prod.reshape(CONV_ROWS // SUBLANES, SUBLANES, HEAD_DIM), axis=0)
                    acc[j % 2] = acc[j % 2] + taps[j] * _shifted(ys, CONV_K - 1 - j, r0, cs)
                dc1_ref[pl.ds(r0, CONV_ROWS), cs] = acc[0] + acc[1]
                return carry

            lax.fori_loop(0, ROW_BLK // CONV_ROWS, chunk, 0)
        dc1 = dc1_ref[...]
        o_ref[:, :CONV_CH] = (dc1 * s).astype(BF16)
        o_ref[:, CONV_CH:] = (dc1 * a * s * (1.0 - s)).astype(BF16)

        @pl.when(i == n_blk - 1)
        def _():
            dcw_ref[...] = jnp.sum(dcw_acc[...], axis=1)

    return _pcall(
        body, "conv_bwd", (n_blk,), [dc2, dc2, z, z, cw, dz],
        [pl.BlockSpec((ROW_BLK, CONV_CH), lambda i: (i, 0)),
         pl.BlockSpec((CONV_HALO, CONV_CH), lambda i: (jnp.minimum((i + 1) * halo_per_blk, last_halo), 0)),
         pl.BlockSpec((ROW_BLK, 2 * CONV_CH), lambda i: (i, GLU_COL_BLK)),
         pl.BlockSpec((CONV_HALO, 2 * CONV_CH), lambda i: (jnp.maximum(i * halo_per_blk - 1, 0), GLU_COL_BLK)),
         pl.BlockSpec((CONV_HALO, CONV_CH), lambda i: (0, 0)),
         pl.BlockSpec(memory_space=pl.ANY)],
        [jax.ShapeDtypeStruct(dz.shape, BF16), jax.ShapeDtypeStruct((CONV_HALO, CONV_CH), F32)],
        [pl.BlockSpec((ROW_BLK, 2 * CONV_CH), lambda i: (i, GLU_COL_BLK)), pl.BlockSpec((CONV_HALO, CONV_CH), lambda i: (0, 0))],
        [pltpu.VMEM((SUBLANES, ROW_BLK + CONV_HALO, CONV_CH), F32), pltpu.VMEM((SUBLANES, ROW_BLK + CONV_HALO, CONV_CH), F32),
         pltpu.VMEM((ROW_BLK, CONV_CH), F32), pltpu.VMEM((CONV_HALO, SUBLANES, CONV_CH), F32)],
        aliases={5: 0}, comm=comm)


def _epi_mix(ya, c4, wcp, gates, bg):
    yc = _dot(c4, wcp, "nn")
    gv = _sigmoid(gates.astype(F32) + bg)
    merged = gv[:, :D_MODEL] * ya + gv[:, D_MODEL:] * yc
    return merged, ya, yc


def _epi_residual_rms(acc, xres, g):
    x = xres + acc
    return x, _rms_fwd_vals(x, g)


def _cross_scores(cq, ck):
    out = []
    for h in range(CROSS_HEADS):
        cs = slice(h * CROSS_HD, (h + 1) * CROSS_HD)
        s = _dot(cq[:, cs], ck[:, cs], "nt") * (CROSS_HD ** -0.5)
        e = jnp.exp(s - jnp.max(s, axis=-1, keepdims=True))
        out.append((cs, e, jnp.sum(e, axis=-1, keepdims=True)))
    return out


def _epi_cross_fwd(acc, ck, cv):
    cq = acc.astype(BF16)
    co = [_dot(e, cv[:, cs], "nn") / den for cs, e, den in _cross_scores(cq, ck)]
    return cq, jnp.concatenate(co, axis=1)


def _epi_cross_bwd(dco, cq, ck, cv):
    dco = dco.astype(BF16)
    dcq, dck, dcv = [], [], []
    for cs, e, den in _cross_scores(cq, ck):
        p = e / den
        dp = _dot(dco[:, cs], cv[:, cs], "nt")
        ds = (p * (dp - jnp.sum(dp * p, axis=-1, keepdims=True)) * (CROSS_HD ** -0.5)).astype(BF16)
        dcq.append(_dot(ds, ck[:, cs], "nn"))
        dck.append(_dot(ds, cq[:, cs], "tn"))
        dcv.append(_dot(p, dco[:, cs], "tn"))
    return jnp.concatenate(dcq, axis=1), jnp.concatenate(dck, axis=1), jnp.concatenate(dcv, axis=1)


def _epi_mlp_up(acc):
    return acc, jnp.square(jnp.maximum(acc, 0.0))


def _epi_final(acc, x2, tgt, g):
    x3 = x2 + acc
    err = _rms_fwd_vals(x3, g) - tgt
    loss = (0.5 / D_MODEL) * jnp.sum(err * err)
    dx3, dg = _rms_bwd_vals(x3, g, err * (1.0 / D_MODEL))
    return dx3, jnp.full((1, HEAD_DIM), loss, F32), dg


def _epi_mlp_down_bwd(dh, hpre):
    return (dh * 2.0 * jnp.maximum(hpre.astype(F32), 0.0),)


def _epi_rms_bwd(du, x, g, dres):
    dx, dg = _rms_bwd_vals(x, g, du)
    return dres.astype(F32) + dx, dg


def _epi_rms_bwd_g(du, x, g):
    return (_rms_bwd_vals(x, g, du)[1],)


def _epi_mix_bwd(dm, ya, yc, gates, bg):
    gv = _sigmoid(gates.astype(F32) + bg)
    ga, gb = gv[:, :D_MODEL], gv[:, D_MODEL:]
    ya, yc = ya.astype(F32), yc.astype(F32)
    dgate = jnp.concatenate([dm * ya * ga * (1.0 - ga), dm * yc * gb * (1.0 - gb)], axis=1)
    return dm * ga, dm * gb, dgate, jnp.sum(dgate, axis=0, keepdims=True)


def _epi_ln_bwd(dc4, c2, lg, lb):
    mu = jnp.mean(c2, axis=-1, keepdims=True)
    xc = c2 - mu
    rstd = lax.rsqrt(jnp.mean(xc * xc, axis=-1, keepdims=True) + EPS)
    xh = xc * rstd
    c3 = xh * lg + lb
    sg = _sigmoid(c3)
    dc3 = dc4 * sg * (1.0 + c3 * (1.0 - sg))
    dxh = dc3 * lg
    dc2 = rstd * (dxh - jnp.mean(dxh, axis=-1, keepdims=True) - xh * jnp.mean(dxh * xh, axis=-1, keepdims=True))
    return (dc2, jnp.sum(dc3 * xh, axis=0, keepdims=True), jnp.sum(dc3, axis=0, keepdims=True),
            jnp.sum(dc2, axis=0, keepdims=True))


def _sds(shape, dtype):
    return jax.ShapeDtypeStruct(shape, dtype)


class _Lazy:
    def __init__(self, fn):
        self.fn = fn

    def __getitem__(self, key):
        return self.fn(key)


def _local_step(x, mem, tgt, sm, plan):
    w = _Lazy(plan.w)
    dw = {}

    def carry(name, n_own, fn, *args, **kw):
        c = plan.comm(name, dw)
        res = fn(*args, comm=c, **kw)
        plan.done(name, res[n_own:])
        return res[:n_own]

    def mm(name, *args, **kw):
        return carry(name, len(args[6]), _mm, name, *args, **kw)

    seq = x.shape[0]
    nr = seq // ROW_BLK
    big = min(1024, seq)
    nb = seq // big
    row = lambda n: ((ROW_BLK, n), lambda i, j, k: (i, 0))
    vec = lambda n: ((1, n), lambda i, j, k: (0, 0))
    full = lambda r, c: ((r, c), lambda i, j, k: (0, 0))
    gates_blk = ((ROW_BLK, 2 * D_MODEL), lambda i, j, k: (i, GATE_COL_BLK))
    tabs = _rope_tables(seq)

    whole3 = lambda a: (a.shape, lambda i, j, k: (0, 0, 0))
    u, z = plan.project_in(x, sm["g_mix"])
    a_gs = carry("qkv_prep", 3, _qkv_prep, z, tabs)
    os_, ls_ = [], []
    for g in range(N_GROUPS):
        name = "attn_fwd_%d" % g
        o_g, l_g = carry(name, 2, _attn_fwd, name, a_gs[g])
        os_.append(o_g)
        ls_.append(l_g)
    attn, lt = _attn_merge(os_, ls_, seq)
    c2, c4 = carry("conv_fwd", 2, _conv_fwd, z, w["taps"], sm["conv_b"], sm["conv_ln_g"], sm["conv_ln_b"])
    merged, ya, yc = mm(
        "mix", attn, w["w_attn_proj"], "nn", (nr, 1, 1), row(GROUP_W), full(GROUP_W, D_MODEL),
        [(_sds((seq, D_MODEL), BF16), *row(D_MODEL))] * 3,
        extras=[(c4, *row(CONV_CH)), (w["w_conv_proj"], *full(CONV_CH, D_MODEL)), (z, *gates_blk), (sm["b_gate"], *vec(2 * D_MODEL))],
        epi=_epi_mix)
    x1, uq = mm("out_proj", merged, w["w_out"], "nn", (nr, 1, 1), row(D_MODEL), full(D_MODEL, D_MODEL),
                 [(_sds((seq, D_MODEL), F32), *row(D_MODEL)), (_sds((seq, D_MODEL), BF16), *row(D_MODEL))],
                 extras=[(x, *row(D_MODEL)), (sm["g_cross"], *vec(D_MODEL))], epi=_epi_residual_rms)

    mn = _rms_fwd("rms_mem", mem, sm["g_mem"], N_MEM)[0]
    ckv = mm("ckv_proj", mn, w["w_ckv"], "nn", (1, N_DEV, 1), full(N_MEM, D_MODEL),
              ((None, D_MODEL, 2 * D_MODEL // N_DEV), lambda i, j, k: (j, 0, 0)),
              [(_sds((N_MEM, 2 * D_MODEL), BF16), (N_MEM, 2 * D_MODEL // N_DEV), lambda i, j, k: (0, j))])[0]
    ck, cv = ckv[:, :D_MODEL], ckv[:, D_MODEL:]
    kv_blk = full(N_MEM, D_MODEL)
    cq, co = mm("cq_proj_cross", uq, w["w_cq"], "nn", (nr, 1, 1), row(D_MODEL), full(D_MODEL, D_MODEL),
                 [(_sds((seq, D_MODEL), BF16), *row(D_MODEL))] * 2,
                 extras=[(ck, *kv_blk), (cv, *kv_blk)], epi=_epi_cross_fwd)
    x2, um = mm("co_proj", co, w["w_co"], "nn", (nr, 1, 1), row(D_MODEL), full(D_MODEL, D_MODEL),
                 [(_sds((seq, D_MODEL), F32), *row(D_MODEL)), (_sds((seq, D_MODEL), BF16), *row(D_MODEL))],
                 extras=[(x1, *row(D_MODEL)), (sm["g_mlp"], *vec(D_MODEL))], epi=_epi_residual_rms)

    ff_blk = D_FF // N_DEV
    row_f32 = (_sds((seq, D_MODEL), F32), *row(D_MODEL))
    row_bf16 = (_sds((seq, D_MODEL), BF16), *row(D_MODEL))
    col_sum = (_sds((1, D_MODEL), F32), *vec(D_MODEL))
    hpre, h = mm("mlp_up", um, w["w_up"], "nn", (nr, 1, 1), row(D_MODEL), whole3(w["w_up"]),
                 [(_sds((seq, D_FF), BF16), *row(D_FF))] * 2, epi=_epi_mlp_up, split=("cols", N_DEV), b_resident=True)
    kt = D_FF // D_MODEL
    dx3, loss, dg_final = mm(
        "mlp_down_loss", h, w["w_down"], "nn", (nr, 1, 1), row(D_FF), full(D_FF, D_MODEL),
        [row_bf16, (_sds((1, HEAD_DIM), F32), *vec(HEAD_DIM)), col_sum],
        extras=[(x2, *row(D_MODEL)), (tgt, *row(D_MODEL)), (sm["g_final"], *vec(D_MODEL))], epi=_epi_final, acc_outs=(1, 2),
        b_resident=True)

    dhpre = mm("mlp_down_bwd", dx3, w["w_down"], "nt", (nr, 1, 1), row(D_MODEL), full(D_FF, D_MODEL),
               [(_sds((seq, D_FF), BF16), *row(D_FF))], extras=[(hpre, *row(D_FF))], epi=_epi_mlp_down_bwd,
               split=("cols", kt), b_resident=True)[0]
    big2 = min(2 * big, seq)
    nb2 = seq // big2
    dw["w_down"] = mm("dw_down", h, dx3, "tn", (kt, 1, nb2), ((big2, D_MODEL), lambda i, j, k: (k, i)),
                      ((big2, D_MODEL), lambda i, j, k: (k, 0)),
                      [(_sds((D_FF, D_MODEL), BF16), (D_MODEL, D_MODEL), lambda i, j, k: (i, 0))])[0]
    dx2, dg_mlp = mm("mlp_up_bwd", dhpre, w["w_up"], "nt", (nr, 1, 1), row(D_FF), whole3(w["w_up"]),
                     [row_bf16, col_sum],
                     extras=[(x2, *row(D_MODEL)), (sm["g_mlp"], *vec(D_MODEL)), (dx3, *row(D_MODEL))],
                     epi=_epi_rms_bwd, acc_outs=(1,), split=("sum", N_DEV), b_resident=True)
    dw["w_up"] = mm("dw_up", um, dhpre, "tn", (1, N_DEV, nb2), ((big2, D_MODEL), lambda i, j, k: (k, 0)),
                    ((big2, ff_blk), lambda i, j, k: (k, j)),
                    [(_sds((N_DEV, D_MODEL, ff_blk), BF16), (None, D_MODEL, ff_blk), lambda i, j, k: (j, 0, 0))])[0]

    acc_kv = (_sds((N_MEM, D_MODEL), F32), *kv_blk)
    dcq, dck, dcv = mm("co_proj_bwd_cross", dx2, w["w_co"], "nt", (nr, 1, 1), row(D_MODEL), full(D_MODEL, D_MODEL),
                       [row_bf16, acc_kv, acc_kv],
                       extras=[(cq, *row(D_MODEL)), (ck, *kv_blk), (cv, *kv_blk)], epi=_epi_cross_bwd, acc_outs=(1, 2))

    def dw_square(name, act, grad):
        return mm(name, act, grad, "tn", (1, 1, nb2), ((big2, D_MODEL), lambda i, j, k: (k, 0)),
                  ((big2, D_MODEL), lambda i, j, k: (k, 0)), [(_sds((D_MODEL, D_MODEL), BF16), *full(D_MODEL, D_MODEL))])[0]

    dw["w_co"] = dw_square("dw_co", co, dx2)
    dx1, dg_cross = mm("cq_proj_bwd", dcq, w["w_cq"], "nt", (nr, 1, 1), row(D_MODEL), full(D_MODEL, D_MODEL),
                       [row_bf16, col_sum],
                       extras=[(x1, *row(D_MODEL)), (sm["g_cross"], *vec(D_MODEL)), (dx2, *row(D_MODEL))],
                       epi=_epi_rms_bwd, acc_outs=(1,))
    dw["w_cq"] = dw_square("dw_cq", uq, dcq)
    dckv = jnp.concatenate([dck, dcv], axis=1)
    kv_chunk = 2 * D_MODEL // N_DEV
    dw["w_ckv"] = mm("dw_ckv", mn, dckv, "tn", (1, N_DEV, 1), full(N_MEM, D_MODEL), ((N_MEM, kv_chunk), lambda i, j, k: (0, j)),
                      [(_sds((N_DEV, D_MODEL, kv_chunk), BF16), (None, D_MODEL, kv_chunk), lambda i, j, k: (j, 0, 0))])[0]
    dg_mem = mm("ckv_proj_bwd", dckv, w["w_ckv"], "nt", (1, 1, N_DEV), ((N_MEM, kv_chunk), lambda i, j, k: (0, k)),
                 ((None, D_MODEL, kv_chunk), lambda i, j, k: (k, 0, 0)), [(_sds((1, D_MODEL), F32), *vec(D_MODEL))],
                 extras=[(mem, *full(N_MEM, D_MODEL)), (sm["g_mem"], *vec(D_MODEL))], epi=_epi_rms_bwd_g, acc_outs=(0,))[0]

    dya, dyc, dz, db_gate = mm(
        "out_proj_bwd_mix", dx1, w["w_out"], "nt", (nr, 1, 1), row(D_MODEL), full(D_MODEL, D_MODEL),
        [(_sds((seq, D_MODEL), BF16), *row(D_MODEL)), (_sds((seq, D_MODEL), BF16), *row(D_MODEL)),
         (_sds((seq, IN_W), BF16), *gates_blk), (_sds((1, 2 * D_MODEL), F32), *vec(2 * D_MODEL))],
        extras=[(ya, *row(D_MODEL)), (yc, *row(D_MODEL)), (z, *gates_blk), (sm["b_gate"], *vec(2 * D_MODEL))],
        epi=_epi_mix_bwd, acc_outs=(3,))
    dw["w_out"] = dw_square("dw_out", merged, dx1)
    dattn = mm("attn_proj_bwd", dya, w["w_attn_proj"], "nt", (nr, 1, 1), row(D_MODEL), full(GROUP_W, D_MODEL),
                [(_sds((seq, GROUP_W), BF16), *row(GROUP_W))])[0]
    pc = D_MODEL // N_DEV
    dw["w_attn_proj"] = mm("dw_attn_proj", attn, dya, "tn", (1, 1, nb2), ((big2, GROUP_W), lambda i, j, k: (k, 0)),
                           ((big2, D_MODEL), lambda i, j, k: (k, 0)),
                           [(_sds((N_DEV, GROUP_W, pc), BF16), (N_DEV, GROUP_W, pc), lambda i, j, k: (0, 0, 0))],
                           out_chunks=N_DEV)[0]
    cvec = (_sds((1, CONV_CH), F32), *vec(CONV_CH))
    dc2, dg_ln_g, dg_ln_b, dg_conv_b = mm(
        "conv_proj_bwd_ln", dyc, w["w_conv_proj"], "nt", (nr, 1, 1), row(D_MODEL), full(CONV_CH, D_MODEL),
        [(_sds((seq, CONV_CH), F32), *row(CONV_CH)), cvec, cvec, cvec],
        extras=[(c2, *row(CONV_CH)), (sm["conv_ln_g"], *vec(CONV_CH)), (sm["conv_ln_b"], *vec(CONV_CH))],
        epi=_epi_ln_bwd, acc_outs=(1, 2, 3))
    dw["w_conv_proj"] = mm("dw_conv_proj", c4, dyc, "tn", (1, 1, nb2), ((big2, CONV_CH), lambda i, j, k: (k, 0)),
                           ((big2, D_MODEL), lambda i, j, k: (k, 0)),
                           [(_sds((N_DEV, CONV_CH, pc), BF16), (N_DEV, CONV_CH, pc), lambda i, j, k: (0, 0, 0))],
                           out_chunks=N_DEV)[0]
    dz, dg_conv_w = carry("conv_bwd", 2, _conv_bwd, dc2, z, w["taps"], dz)
    preps = _attn_bwd_prep(dattn, attn, lt)
    d_gs = []
    for g in range(N_GROUPS):
        name = "attn_bwd_%d" % g
        d_gs.append(carry(name, 1, _attn_bwd, name, a_gs[g], *preps[g])[0])
    dz = _dqkv_post(d_gs, tabs, dz)
    dw["w_in"] = mm("dw_in", u, dz, "tn", (1, N_DEV, nb2), ((big2, D_MODEL), lambda i, j, k: (k, 0)),
                    ((big2, D_MODEL), lambda i, j, k: (k, j)),
                    [(_sds((N_DEV, D_MODEL, D_MODEL), BF16), (None, D_MODEL, D_MODEL), lambda i, j, k: (j, 0, 0))])[0]
    token = plan.start_w_in(dw["w_in"])
    grad_x, dg_mix = mm("in_proj_bwd", dz, w["w_in"], "nt", (nr, 1, 1), row(IN_W), whole3(w["w_in"]), [row_f32, col_sum],
                        extras=[(x, *row(D_MODEL)), (sm["g_mix"] + token, *vec(D_MODEL)), (dx1, *row(D_MODEL))],
                        epi=_epi_rms_bwd, acc_outs=(1,), split=("sum", N_DEV), b_resident=True)
    small = dict(g_mix=dg_mix, b_gate=db_gate, conv_b=dg_conv_b, conv_ln_g=dg_ln_g, conv_ln_b=dg_ln_b, g_cross=dg_cross,
                 g_mem=dg_mem, g_mlp=dg_mlp, g_final=dg_final, loss=loss, conv_w=dg_conv_w)
    return grad_x, dw, small


SHARD_SHAPE = dict(w_in=(1024, 1024), w_attn_proj=(512, 128), w_conv_proj=(768, 128), w_out=(128, 1024), w_cq=(128, 1024),
                   w_ckv=(1024, 256), w_co=(128, 1024), w_up=(1024, 512), w_down=(512, 1024))
FWD_CARRY = {"in_proj":("w_attn_proj", "w_conv_proj", "w_out", "w_cq", "w_ckv", "w_co", "taps"),
             "conv_fwd": ("w_up", "w_down")}
BWD_CARRY = {"dw_up": ("w_down",), "out_proj_bwd_mix": ("w_co", "w_cq"), "conv_bwd": ("w_up", "w_ckv"),
             "attn_bwd_0": ("w_out",), "attn_bwd_1": ("w_attn_proj", "w_conv_proj")}


def _cols_to_2d(a):
    return a.transpose(1, 0, 2).reshape(a.shape[1], -1)


def _in_proj_gather(resid, g, w_shard, comm):
    seq = resid.shape[0]
    tm = min(1024, seq)
    x, y, c = lax.axis_index("x"), lax.axis_index("y"), lax.axis_index("c")
    ident = lambda px, py, pc: 4 * px + 2 * py + pc
    far = [(1 - x, y), (x, 1 - y), (1 - x, 1 - y)]
    order = jnp.stack([ident(x, y, c), ident(x, y, 1 - c), ident(*far[0], c), ident(*far[1], c), ident(*far[0], 1 - c),
                       ident(*far[1], 1 - c), ident(*far[2], c), ident(*far[2], 1 - c)]).astype(jnp.int32)
    forward_at = {2: 0, 3: 1, 6: 2}
    n_far = len(far)

    def body(order_ref, x_ref, g_ref, wsh_ref, *rest):
        c_in, z_ref, u_ref, wg_ref = rest[:comm.n], rest[comm.n], rest[comm.n + 1], rest[comm.n + 2]
        c_out = rest[comm.n + 3:2 * comm.n + 3]
        wbuf, u_all, load_sem, local_sem, recv_sems, ici_send, d2d_send = rest[2 * comm.n + 3:2 * comm.n + 10]
        sems = rest[2 * comm.n + 10:]
        jj, i = pl.program_id(0), pl.program_id(1)
        (kx, ky, kc), me, chips = _Comm._where()
        sibling = (kx, ky, 1 - kc)
        n = order_ref[jj]

        def push(src, blk, send, to):
            return pltpu.make_async_remote_copy(src_ref=src, dst_ref=wg_ref.at[blk], send_sem=send,
                                                recv_sem=recv_sems.at[blk], device_id=to, device_id_type=MESH)

        def load(src):
            cp = pltpu.make_async_copy(src, wbuf, load_sem)
            cp.start()
            cp.wait()

        @pl.when(jnp.logical_and(jj == 0, i == 0))
        def _():
            push(wsh_ref, me, d2d_send, sibling).start()
            for (px, py) in chips[:2]:
                push(wsh_ref, me, ici_send, (px, py, kc)).start()
            pltpu.make_async_copy(wsh_ref, wg_ref.at[me], local_sem).start()
            load(wsh_ref)

        @pl.when(jnp.logical_and(jj > 0, i == 0))
        def _():
            push(wg_ref.at[n], n, d2d_send, sibling).wait_recv()
            for step, k in forward_at.items():
                @pl.when(jj == step)
                def _(k=k):
                    blk = 4 * chips[k][0] + 2 * chips[k][1] + kc
                    push(wg_ref.at[blk], blk, d2d_send, sibling).start()

            @pl.when(jj == 2)
            def _():
                push(wsh_ref, me, ici_send, (*chips[2], kc)).start()

            if comm.n:
                @pl.when(jj == 3)
                def _():
                    comm.start(c_in, c_out, sems)

            load(wg_ref.at[n])

        rows = pl.ds(pl.multiple_of(i * tm, tm), tm)

        @pl.when(jj == 0)
        def _():
            u_val = _rms_fwd_vals(x_ref[...], g_ref[...]).astype(BF16)
            u_all[rows, :] = u_val
            u_ref[...] = u_val
            z_ref[...] = _dot(u_val, wbuf[...], "nn").astype(BF16)

        @pl.when(jj > 0)
        def _():
            z_ref[...] = _dot(u_all[rows, :], wbuf[...], "nn").astype(BF16)

        @pl.when(jnp.logical_and(jj == N_DEV - 1, i == pl.num_programs(1) - 1))
        def _():
            def drain_sends(send, count):
                blocks = wg_ref.at[pl.ds(0, count)]
                pltpu.make_async_remote_copy(src_ref=blocks, dst_ref=blocks, send_sem=send, recv_sem=recv_sems.at[0],
                                             device_id=sibling, device_id_type=MESH).wait_send()

            drain_sends(ici_send, n_far)
            drain_sends(d2d_send, n_far + 1)
            pltpu.make_async_copy(wsh_ref, wg_ref.at[me], local_sem).wait()
            if comm.n:
                comm.wait(c_in, c_out, sems)

    any_spec = pl.BlockSpec(memory_space=pl.ANY)
    n_i = seq // tm
    first_pass = lambda jj, i, order_ref: (jnp.where(jj == 0, i, n_i - 1), 0)
    grid_spec = pltpu.PrefetchScalarGridSpec(
        num_scalar_prefetch=1, grid=(N_DEV, n_i),
        in_specs=[pl.BlockSpec((tm, D_MODEL), first_pass), pl.BlockSpec((1, D_MODEL), lambda jj, i, order_ref: (0, 0)),
                  any_spec] + comm.in_specs,
        out_specs=[pl.BlockSpec((tm, D_MODEL), lambda jj, i, order_ref: (i, order_ref[jj])),
                   pl.BlockSpec((tm, D_MODEL), first_pass), any_spec] + comm.out_specs,
        scratch_shapes=[pltpu.VMEM((D_MODEL, D_MODEL), BF16), pltpu.VMEM((seq, D_MODEL), BF16), pltpu.SemaphoreType.DMA,
                        pltpu.SemaphoreType.DMA, pltpu.SemaphoreType.DMA((N_DEV,)), pltpu.SemaphoreType.DMA,
                        pltpu.SemaphoreType.DMA] + comm.scratch)
    return pl.pallas_call(
        body, name="in_proj_gather", grid_spec=grid_spec,
        out_shape=[jax.ShapeDtypeStruct((seq, IN_W), BF16), jax.ShapeDtypeStruct((seq, D_MODEL), BF16),
                   jax.ShapeDtypeStruct((N_DEV, D_MODEL, D_MODEL), BF16)] + comm.out_shape,
        compiler_params=_params(dimension_semantics=("arbitrary", "arbitrary")),
    )(order, resid, g, w_shard, *comm.arrays)


_HBM = pl.BlockSpec(memory_space=pltpu.HBM)
_SEM = pl.BlockSpec(memory_space=pltpu.SEMAPHORE)


def _chunks_start(dw):
    def body(src_ref, land_ref, send_sem, recv_sem, src_thru, land_thru, token):
        del src_thru, land_thru
        me, peers = _peers()
        for (px, py, pc) in peers:
            pltpu.make_async_remote_copy(src_ref=src_ref.at[4 * px + 2 * py + pc], dst_ref=land_ref.at[me], send_sem=send_sem,
                                         recv_sem=recv_sem, device_id=(px, py, pc), device_id_type=MESH).start()
        token[...] = jnp.zeros_like(token)

    return pl.pallas_call(
        body, name="w_in_grad_start",
        out_shape=(pltpu.SemaphoreType.DMA(()), pltpu.SemaphoreType.DMA(()), pltpu.HBM(dw.shape, dw.dtype),
                   pltpu.HBM(dw.shape, dw.dtype), jax.ShapeDtypeStruct((SUBLANES, HEAD_DIM), F32)),
        in_specs=(_HBM, _HBM), out_specs=(_SEM, _SEM, _HBM, _HBM, pl.BlockSpec(memory_space=pltpu.VMEM)),
        input_output_aliases={0: 2, 1: 3},
        compiler_params=pltpu.CompilerParams(has_side_effects=pltpu.SideEffectType.DATAFLOW_SIDE_EFFECTING),
    )(pltpu.with_memory_space_constraint(dw, pltpu.HBM),
      pltpu.with_memory_space_constraint(lax.empty(dw.shape, dw.dtype), pltpu.HBM))


def _chunks_wait(send_sem, recv_sem, src_thru, land_thru, after):
    def body(src_ref, land_ref, send_sem, recv_sem, after_ref, src_dead, land_out):
        del after_ref, src_dead, land_out
        seven = land_ref.at[pl.ds(0, N_DEV - 1)]
        cp = pltpu.make_async_remote_copy(src_ref=seven, dst_ref=seven, send_sem=send_sem, recv_sem=recv_sem,
                                          device_id=_peers()[1][0], device_id_type=MESH)
        cp.wait_send()
        cp.wait_recv()

    return pl.pallas_call(
        body, name="w_in_grad_wait",
        out_shape=(pltpu.HBM(src_thru.shape, src_thru.dtype), pltpu.HBM(land_thru.shape, land_thru.dtype)),
        in_specs=(_HBM, _HBM, _SEM, _SEM, pl.BlockSpec(memory_space=pl.ANY)), out_specs=(_HBM, _HBM),
        input_output_aliases={0: 0, 1: 1},
        compiler_params=pltpu.CompilerParams(has_side_effects=pltpu.SideEffectType.DATAFLOW_SIDE_EFFECTING),
    )(src_thru, land_thru, send_sem, recv_sem, after)


class _Plan:
    def __init__(self, shards, n_tap_cols):
        self.shards = shards
        self.gathered = {}
        self.parts = {}
        self.n_tap_cols = n_tap_cols

    def project_in(self, x, g):
        comm = self.comm("in_proj", None)
        res = _in_proj_gather(x, g, self.shards["w_in"], comm)
        self.gathered["w_in"] = res[2]
        self.done("in_proj", res[3:])
        return res[1], res[0]

    def start_w_in(self, dw_in):
        *self.in_flight, token = _chunks_start(dw_in)
        return token[0, 0]

    def finish_w_in(self, after):
        src, land = _chunks_wait(*self.in_flight, after)
        me = _peers()[0]
        own = lax.dynamic_slice(src, (me, 0, 0), (1,) + src.shape[1:])
        return lax.dynamic_update_slice(land, own, (me, 0, 0))

    def comm(self, name, dw):
        if name in FWD_CARRY:
            return _Comm(replicated=[self.shards[k] for k in FWD_CARRY[name]])
        if name in BWD_CARRY:
            return _Comm(chunked=[dw[k].reshape((N_DEV,) + SHARD_SHAPE[k]) for k in BWD_CARRY[name]])
        return _NO_COMM

    def done(self, name, got):
        if name in FWD_CARRY:
            self.gathered.update(zip(FWD_CARRY[name], got))
        elif name in BWD_CARRY:
            self.parts.update(zip(BWD_CARRY[name], got))

    def w(self, key):
        g = self.gathered[key]
        if key in ("w_in", "w_up", "w_ckv"):
            return g
        if key in ("w_attn_proj", "w_conv_proj"):
            return _cols_to_2d(g)
        if key == "taps":
            return jnp.pad(_cols_to_2d(g[:, :CONV_K, :self.n_tap_cols]), ((0, 1), (0, 0)))
        return g.reshape(-1, g.shape[-1])


def _adamw(name, w, m, v, parts, comm=_NO_COMM):
    rows, cols = w.shape
    n_parts = parts.shape[0]
    rb = rows if rows <= 256 or rows % 256 else 256

    def body(w_ref, m_ref, v_ref, p_ref, g_ref, d_ref, nm_ref, nv_ref):
        g = p_ref[0].astype(F32)
        for q in range(1, n_parts):
            g = g + p_ref[q].astype(F32)
        wv = w_ref[...]
        nm = ADAM_B1 * m_ref[...] + (1.0 - ADAM_B1) * g
        nv = ADAM_B2 * v_ref[...] + (1.0 - ADAM_B2) * jnp.square(g)
        m_hat = nm / (1.0 - ADAM_B1 ** ADAM_STEP)
        v_hat = nv / (1.0 - ADAM_B2 ** ADAM_STEP)
        g_ref[...] = g
        d_ref[...] = -ADAM_LR * (m_hat / (jnp.sqrt(v_hat) + ADAM_EPS) + ADAM_WD * wv)
        nm_ref[...] = nm
        nv_ref[...] = nv

    blk = pl.BlockSpec((rb, cols), lambda i: (i, 0))
    return _pcall(body, name, (rows // rb,), [w, m, v, parts],
                  [blk, blk, blk, pl.BlockSpec((n_parts, rb, cols), lambda i: (0, i, 0))],
                  [jax.ShapeDtypeStruct((rows, cols), F32)] * 4, [blk] * 4, comm=comm)


def _sum_parts(name, parts):
    def body(p_ref, o_ref):
        acc = p_ref[0]
        for q in range(1, parts.shape[0]):
            acc = acc + p_ref[q]
        o_ref[...] = acc

    return _pcall(body, name, (1,), [parts], [pl.BlockSpec(parts.shape, lambda i: (0, 0, 0))],
                  [jax.ShapeDtypeStruct(parts.shape[1:], F32)], [pl.BlockSpec(parts.shape[1:], lambda i: (0, 0))])[0]


BIG = ("w_in", "w_attn_proj", "w_conv_proj", "w_out", "w_cq", "w_ckv", "w_co", "w_up", "w_down")
SMALL = ("g_mix", "b_gate", "conv_b", "conv_ln_g", "conv_ln_b", "g_cross", "g_mem", "g_mlp", "g_final")
SMALL_ORDER = SMALL + ("loss", "conv_w")
WEIGHTS = ("g_mix", "w_in", "b_gate", "conv_w", "conv_b", "conv_ln_g", "conv_ln_b", "w_attn_proj", "w_conv_proj", "w_out",
           "g_cross", "g_mem", "w_cq", "w_ckv", "w_co", "g_mlp", "w_up", "w_down", "g_final")


def kernel(x, mem, g_mix, w_in, b_gate, conv_w, conv_b, conv_ln_g, conv_ln_b, w_attn_proj, w_conv_proj, w_out, g_cross, g_mem, w_cq, w_ckv, w_co, g_mlp, w_up, w_down, g_final, loss_target, m_g_mix, m_w_in, m_b_gate, m_conv_w, m_conv_b, m_conv_ln_g, m_conv_ln_b, m_w_attn_proj, m_w_conv_proj, m_w_out, m_g_cross, m_g_mem, m_w_cq, m_w_ckv, m_w_co, m_g_mlp, m_w_up, m_w_down, m_g_final, v_g_mix, v_w_in, v_b_gate, v_conv_w, v_conv_b, v_conv_ln_g, v_conv_ln_b, v_w_attn_proj, v_w_conv_proj, v_w_out, v_g_cross, v_g_mem, v_w_cq, v_w_ckv, v_w_co, v_g_mlp, v_w_up, v_w_down, v_g_final):
    args = dict(locals())
    wts = {k: args[k] for k in WEIGHTS}
    mom = {k: args["m_" + k] for k in WEIGHTS}
    var = {k: args["v_" + k] for k in WEIGHTS}
    two_d = lambda a: a.reshape(a.shape[-2:]) if a.ndim == 3 else a.reshape(1, -1)

    shards = {k: two_d(wts[k]).astype(BF16) for k in BIG}
    shards["taps"] = jnp.pad(two_d(conv_w), ((0, 1), (0, HEAD_DIM - conv_w.shape[-1])))
    plan = _Plan(shards, conv_w.shape[-1])
    sm = {k: two_d(wts[k]) for k in SMALL}

    grad_x, _, small = _local_step(x[0], mem[0], loss_target[0], sm, plan)
    parts = plan.parts

    out = {}
    small_comm = _Comm(replicated=[small[k] for k in SMALL_ORDER])
    for k in BIG[1:]:
        res = _adamw("adamw_" + k, two_d(wts[k]), two_d(mom[k]), two_d(var[k]), parts[k],
                     comm=small_comm if k == BIG[1] else _NO_COMM)
        out[k] = [r.reshape(wts[k].shape) for r in res[:4]]
        if k == BIG[1]:
            small_parts = dict(zip(SMALL_ORDER, res[4:]))
    for k in SMALL:
        res = _adamw("adamw_" + k, two_d(wts[k]), two_d(mom[k]), two_d(var[k]), small_parts[k])
        out[k] = [r.reshape(wts[k].shape) for r in res]
    res = _adamw("adamw_w_in", two_d(w_in), two_d(m_w_in), two_d(v_w_in), plan.finish_w_in(after=out["g_final"][3]))
    out["w_in"] = [r.reshape(w_in.shape) for r in res]
    loss = _sum_parts("loss_sum", small_parts["loss"])[0, 0]
    me = 4 * lax.axis_index("x") + 2 * lax.axis_index("y") + lax.axis_index("c")
    n_tap_cols = conv_w.shape[-1]
    tap_parts = lax.dynamic_slice(small_parts["conv_w"], (0, 0, me * n_tap_cols), (N_DEV, CONV_K, n_tap_cols))
    res = _adamw("adamw_conv_w", two_d(conv_w), two_d(m_conv_w), two_d(v_conv_w), tap_parts)
    out["conv_w"] = [r.reshape(conv_w.shape) for r in res]

    return (loss, grad_x[None], *[out[k][0] for k in WEIGHTS], *[out[k][1] for k in WEIGHTS],
            *[out[k][2] for k in WEIGHTS], *[out[k][3] for k in WEIGHTS])
```

```python
import functools

import jax
import jax.numpy as jnp
import numpy as np
from jax import lax
from jax.experimental import pallas as pl
from jax.experimental.pallas import tpu as pltpu

F32 = jnp.float32
BF16 = jnp.bfloat16

N_DEV = 8
D_MODEL = 1024
N_MEM = 256
HEAD_DIM = 128
HEADS_PER_GROUP = 4
GROUP_W = HEADS_PER_GROUP * HEAD_DIM
DILATIONS = (1, 4, 16)
BAND = 128
N_GROUPS = 3
ATTN_W = N_GROUPS * GROUP_W
QKV_W = 3 * ATTN_W
ROT_DIM = HEAD_DIM // 4
ROPE_THETA = 500000.0
CONV_CH = 768
CONV_K = 31
CONV_HALO = 32
SUBLANES = 8
CONV_ROWS = 64
IN_W = 8192
GLU_COL_BLK = QKV_W // (2 * CONV_CH)
GATE_COL_BLK = (QKV_W + 2 * CONV_CH) // (2 * D_MODEL)
CROSS_HEADS = 4
CROSS_HD = D_MODEL // CROSS_HEADS
D_FF = 4096
EPS = 1e-6
NEG = -1e30
QB = 4
ROW_BLK = QB * BAND

ADAM_LR = 0.001
ADAM_B1 = 0.9
ADAM_B2 = 0.999
ADAM_EPS = 1e-08
ADAM_WD = 0.01
ADAM_STEP = 10

VMEM_LIMIT = 56 * 1024 * 1024
MESH = pl.DeviceIdType.MESH


def _params(**kw):
    return pltpu.CompilerParams(vmem_limit_bytes=VMEM_LIMIT, **kw)


def _sigmoid(x):
    return 1.0 / (1.0 + jnp.exp(-x))


def _dot(a, b, kind):
    dims = {"nn": (((1,), (0,)), ((), ())), "nt": (((1,), (1,)), ((), ())), "tn": (((0,), (0,)), ((), ()))}[kind]
    if a.dtype != BF16:
        a = a.astype(BF16)
    if b.dtype != BF16:
        b = b.astype(BF16)
    return lax.dot_general(a, b, dims, preferred_element_type=F32)


def _peers():
    x, y, c = lax.axis_index("x"), lax.axis_index("y"), lax.axis_index("c")
    me = 4 * x + 2 * y + c
    peers = [(x, y, 1 - c), (1 - x, y, c), (x, 1 - y, c), (1 - x, 1 - y, c),
             (1 - x, y, 1 - c), (x, 1 - y, 1 - c), (1 - x, 1 - y, 1 - c)]
    return me, peers


class _Comm:
    def __init__(self, chunked=(), replicated=()):
        self.arrays = list(chunked) + list(replicated)
        self.n_c = len(chunked)
        self.n = len(self.arrays)
        self.out_shape = [jax.ShapeDtypeStruct(a.shape, a.dtype) for a in chunked]
        self.out_shape += [jax.ShapeDtypeStruct((N_DEV,) + a.shape, a.dtype) for a in replicated]
        self.in_specs = [pl.BlockSpec(memory_space=pl.ANY)] * self.n
        self.out_specs = [pl.BlockSpec(memory_space=pl.ANY)] * self.n
        self.scratch = [pltpu.SemaphoreType.DMA((self.n,))] * 5 if self.n else []

    @staticmethod
    def _where():
        x, y, c = lax.axis_index("x"), lax.axis_index("y"), lax.axis_index("c")
        chips = [(1 - x, y), (x, 1 - y), (1 - x, 1 - y)]
        return (x, y, c), 4 * x + 2 * y + c, chips

    def _local(self, ins, outs, sems, a, me):
        src = ins[a].at[me] if a < self.n_c else ins[a]
        return pltpu.make_async_copy(src, outs[a].at[me], sems[2].at[a])

    @staticmethod
    def _remote(src, dst, send, recv, to):
        return pltpu.make_async_remote_copy(src_ref=src, dst_ref=dst, send_sem=send, recv_sem=recv, device_id=to,
                                            device_id_type=MESH)

    def start(self, ins, outs, sems):
        (x, y, c), me, chips = self._where()
        for a in range(self.n):
            self._local(ins, outs, sems, a, me).start()
            if a < self.n_c:
                for (px, py, pc) in _peers()[1]:
                    self._remote(ins[a].at[4 * px + 2 * py + pc], outs[a].at[me], sems[0].at[a], sems[1].at[a], (px, py, pc)).start()
            else:
                self._remote(ins[a], outs[a].at[me], sems[3].at[a], sems[4].at[a], (x, y, 1 - c)).start()
                for (px, py) in chips:
                    self._remote(ins[a], outs[a].at[me], sems[0].at[a], sems[1].at[a], (px, py, c)).start()

    def wait(self, ins, outs, sems):
        (x, y, c), me, chips = self._where()
        sibling = (x, y, 1 - c)

        def drain(a, pair, count):
            blocks = outs[a].at[pl.ds(0, count)]
            cp = self._remote(blocks, blocks, sems[pair].at[a], sems[pair + 1].at[a], sibling)
            cp.wait_send()
            cp.wait_recv()

        for a in range(self.n):
            if a < self.n_c:
                drain(a, 0, N_DEV - 1)
            else:
                drain(a, 0, len(chips))
                for (px, py) in chips:
                    blk = outs[a].at[4 * px + 2 * py + c]
                    self._remote(blk, blk, sems[3].at[a], sems[4].at[a], sibling).start()
        for a in range(self.n):
            if a >= self.n_c:
                drain(a, 3, len(chips) + 1)
            self._local(ins, outs, sems, a, me).wait()


_NO_COMM = _Comm()


def _pcall(body, name, grid, operands, in_specs, out_shape, out_specs, scratch=(), aliases=None, comm=_NO_COMM, **params):
    n_in, n_out, n_scr = len(operands), len(out_shape), len(scratch)
    grid = tuple(grid)

    def carried(*refs):
        ins, c_in = refs[:n_in], refs[n_in:n_in + comm.n]
        o0 = n_in + comm.n
        outs, c_out = refs[o0:o0 + n_out], refs[o0 + n_out:o0 + n_out + comm.n]
        s0 = o0 + n_out + comm.n
        scr, sems = refs[s0:s0 + n_scr], refs[s0 + n_scr:]
        ids = [pl.program_id(ax) for ax in range(len(grid))]

        @pl.when(functools.reduce(jnp.logical_and, [p == 0 for p in ids]))
        def _():
            comm.start(c_in, c_out, sems)

        body(*ins, *outs, *scr)

        @pl.when(functools.reduce(jnp.logical_and, [p == g - 1 for p, g in zip(ids, grid)]))
        def _():
            comm.wait(c_in, c_out, sems)

    return pl.pallas_call(
        carried if comm.n else body, name=name, grid=grid, in_specs=list(in_specs) + comm.in_specs,
        out_shape=list(out_shape) + comm.out_shape, out_specs=list(out_specs) + comm.out_specs,
        scratch_shapes=list(scratch) + comm.scratch, input_output_aliases=aliases or {},
        compiler_params=_params(dimension_semantics=("arbitrary",) * len(grid), **params),
    )(*operands, *comm.arrays)


def _mm(name, a, b, kind, grid, a_blk, b_blk, outs, extras=(), epi=None, acc_outs=(), j_outer=False, comm=_NO_COMM,
        split=None, b_resident=False, out_chunks=0):
    gi, gj, gk = grid
    n_ex = len(extras)
    n_out = len(outs)
    mode, n_chunks = split if split is not None else (None, 1)

    def spec(blk, fn, **kw):
        return pl.BlockSpec(blk, (lambda j, i, k: fn(i, j, k)) if j_outer else fn, **kw)

    def b_chunk(b_ref, c):
        if len(b_ref.shape) == 3:
            return b_ref[c]
        rows, cols = b_ref.shape
        if (kind == "nn") == (mode == "cols"):
            return b_ref[:, c * (cols // n_chunks):(c + 1) * (cols // n_chunks)]
        return b_ref[c * (rows // n_chunks):(c + 1) * (rows // n_chunks), :]

    def col_chunk(ref, c):
        width = ref.shape[-1] // n_chunks
        return slice(c * width, (c + 1) * width)

    def body(*refs):
        a_ref, b_ref = refs[0], refs[1]
        ex = refs[2:2 + n_ex]
        out_refs = refs[2 + n_ex:2 + n_ex + n_out]
        acc_ref = refs[2 + n_ex + n_out] if gk > 1 else None
        i = pl.program_id(1 if j_outer else 0)
        k = pl.program_id(2)
        if mode == "cols":
            a_val = a_ref[...]
            for c in range(n_chunks):
                acc = _dot(a_val, b_chunk(b_ref, c), kind)
                vals = epi(acc, *[e[:, col_chunk(e, c)] for e in ex]) if epi is not None else (acc,)
                for o, v in zip(out_refs, vals):
                    o[:, col_chunk(o, c)] = v.astype(o.dtype)
            return
        if mode == "sum":
            part = _dot(a_ref[:, col_chunk(a_ref, 0)], b_chunk(b_ref, 0), kind)
            for c in range(1, n_chunks):
                part = part + _dot(a_ref[:, col_chunk(a_ref, c)], b_chunk(b_ref, c), kind)
        else:
            part = _dot(a_ref[...], b_ref[...], kind)

        def finish(acc):
            if out_chunks:
                width = acc.shape[-1] // out_chunks
                for c in range(out_chunks):
                    out_refs[0][c] = acc[:, c * width:(c + 1) * width].astype(out_refs[0].dtype)
                return
            vals = epi(acc, *[e[...] for e in ex]) if epi is not None else (acc,)
            for idx, (o, v) in enumerate(zip(out_refs, vals)):
                if idx in acc_outs:
                    @pl.when(i == 0)
                    def _():
                        o[...] = v.astype(o.dtype)

                    @pl.when(i != 0)
                    def _():
                        o[...] += v.astype(o.dtype)
                else:
                    o[...] = v.astype(o.dtype)

        if gk == 1:
            finish(part)
        else:
            @pl.when(k == 0)
            def _():
                acc_ref[...] = part

            @pl.when(k != 0)
            def _():
                acc_ref[...] += part

            @pl.when(k == gk - 1)
            def _():
                finish(acc_ref[...])

    scratch = []
    if gk > 1:
        tm = a_blk[0][-1] if kind == "tn" else a_blk[0][-2]
        tn = b_blk[0][-2] if kind == "nt" else b_blk[0][-1]
        scratch = [pltpu.VMEM((tm, tn), F32)]
    b_kw = dict(pipeline_mode=pl.Buffered(1)) if b_resident else {}
    return _pcall(body, name, (gj, gi, gk) if j_outer else (gi, gj, gk), [a, b] + [e for e, _, _ in extras],
                  [spec(*a_blk), spec(*b_blk, **b_kw)] + [spec(blk, fn) for _, blk, fn in extras],
                  [s for s, _, _ in outs], [spec(blk, fn) for _, blk, fn in outs], scratch, comm=comm)


def _rms_fwd_vals(x, g):
    r = lax.rsqrt(jnp.mean(x * x, axis=-1, keepdims=True) + EPS)
    return x * r * g


def _rms_bwd_vals(x, g, du):
    r = lax.rsqrt(jnp.mean(x * x, axis=-1, keepdims=True) + EPS)
    xh = x * r
    dxh = du * g
    dx = r * (dxh - xh * jnp.mean(dxh * xh, axis=-1, keepdims=True))
    return dx, jnp.sum(du * xh, axis=0, keepdims=True)


def _rms_fwd(name, x, g, rows, comm=_NO_COMM):
    n = x.shape[0]

    def body(x_ref, g_ref, o_ref):
        o_ref[...] = _rms_fwd_vals(x_ref[...], g_ref[...]).astype(BF16)

    return _pcall(body, name, (n // rows,), [x, g],
                  [pl.BlockSpec((rows, D_MODEL), lambda i: (i, 0)), pl.BlockSpec((1, D_MODEL), lambda i: (0, 0))],
                  [jax.ShapeDtypeStruct(x.shape, BF16)], [pl.BlockSpec((rows, D_MODEL), lambda i: (i, 0))], comm=comm)


def _rope_tables(seq):
    half = ROT_DIM // 2
    pos = np.arange(seq, dtype=np.float32)
    inv_freq = np.float32(ROPE_THETA) ** (-np.arange(0, ROT_DIM, 2, dtype=np.float32) / np.float32(ROT_DIM))
    ang = (pos[:, None] * inv_freq[None, :]).astype(np.float32)
    cos, sin = np.cos(ang), np.sin(ang)
    rest = HEAD_DIM - ROT_DIM
    c = np.concatenate([cos, cos, np.ones((seq, rest), np.float32)], axis=1)
    s1 = np.concatenate([np.zeros((seq, half), np.float32), sin, np.zeros((seq, rest), np.float32)], axis=1)
    s2 = np.concatenate([-sin, np.zeros((seq, half + rest), np.float32)], axis=1)
    return jnp.asarray(c), jnp.asarray(s1), jnp.asarray(s2)


def _group_shapes(seq, width, dtype):
    return [jax.ShapeDtypeStruct((d, seq // d, width), dtype) for d in DILATIONS]


def _group_specs(width):
    return [pl.BlockSpec((d, ROW_BLK // d, width), lambda i: (0, i, 0)) for d in DILATIONS]


def _qkv_prep(z, tabs, comm=_NO_COMM):
    seq = z.shape[0]

    def body(z_ref, c_ref, s1_ref, s2_ref, a0, a1, a2, sc):
        outs = (a0, a1, a2)
        c, s1, s2 = c_ref[...], s1_ref[...], s2_ref[...]
        for part in range(3):
            for hh in range(N_GROUPS * HEADS_PER_GROUP):
                g, hl = divmod(hh, HEADS_PER_GROUP)
                col = part * ATTN_W + hh * HEAD_DIM
                ocol = part * GROUP_W + hl * HEAD_DIM
                x = z_ref[:, col:col + HEAD_DIM].astype(F32)
                if part < 2:
                    x = x * c + pltpu.roll(x, ROT_DIM // 2, 1) * s1 + pltpu.roll(x, HEAD_DIM - ROT_DIM // 2, 1) * s2
                d = DILATIONS[g]
                if d == 1:
                    outs[g][0, :, ocol:ocol + HEAD_DIM] = x.astype(BF16)
                else:
                    sc[...] = x
                    for r in range(d):
                        outs[g][r, :, ocol:ocol + HEAD_DIM] = sc[pl.ds(r, ROW_BLK // d, stride=d), :].astype(BF16)

    tab_spec = pl.BlockSpec((ROW_BLK, HEAD_DIM), lambda i: (i, 0))
    return _pcall(body, "qkv_prep", (seq // ROW_BLK,), [z, *tabs],
                  [pl.BlockSpec((ROW_BLK, QKV_W), lambda i: (i, 0)), tab_spec, tab_spec, tab_spec],
                  _group_shapes(seq, ATTN_W, BF16), _group_specs(ATTN_W), [pltpu.VMEM((ROW_BLK, HEAD_DIM), F32)], comm=comm)


def _band_masks_2(t):
    qi = lax.broadcasted_iota(jnp.int32, (BAND, 2 * BAND), 0)
    kj = lax.broadcasted_iota(jnp.int32, (BAND, 2 * BAND), 1)
    band = jnp.logical_and(kj >= qi, kj <= qi + BAND)
    return band, jnp.logical_and(band, jnp.logical_or(kj >= BAND, t > 0))


def _attn_fwd(name, a_g, comm=_NO_COMM):
    dil, m_len, _ = a_g.shape
    qb = min(QB, m_len // BAND)
    rows = qb * BAND
    steps = m_len // rows
    scale = HEAD_DIM ** -0.5

    tiles = [(sb, h) for sb in range(qb) for h in range(HEADS_PER_GROUP)]

    def body(q_ref, kc_ref, vc_ref, kp_ref, vp_ref, o_ref, l_ref, k_all, v_all, s_scr, p_scr, r_scr):
        t = pl.program_id(1)
        k_all[0:BAND, :] = kp_ref[...]
        k_all[BAND:, :] = kc_ref[...]
        v_all[0:BAND, :] = vp_ref[...]
        v_all[BAND:, :] = vc_ref[...]
        band, band_first = _band_masks_2(t)
        for idx, (sb, h) in enumerate(tiles):
            cs = slice(h * HEAD_DIM, (h + 1) * HEAD_DIM)
            s = _dot(q_ref[sb * BAND:(sb + 1) * BAND, cs], k_all[sb * BAND:(sb + 2) * BAND, cs], "nt") * scale
            s_scr[idx] = jnp.where(band_first if sb == 0 else band, s, NEG)
        lane = lax.broadcasted_iota(jnp.int32, (BAND, HEAD_DIM), 1)
        lse_rows = [jnp.zeros((BAND, HEAD_DIM), F32)] * qb
        for idx, (sb, h) in enumerate(tiles):
            s = s_scr[idx]
            mx = jnp.max(s, axis=-1, keepdims=True)
            p = jnp.exp(s - mx)
            den = jnp.sum(p, axis=-1, keepdims=True)
            p_scr[idx] = p.astype(BF16)
            r_scr[idx] = jnp.broadcast_to(1.0 / den, (BAND, HEAD_DIM))
            lse_rows[sb] = jnp.where(lane == h, jnp.broadcast_to(mx + jnp.log(den), (BAND, HEAD_DIM)), lse_rows[sb])
        for sb in range(qb):
            l_ref[sb * BAND:(sb + 1) * BAND, :] = lse_rows[sb]
        for idx, (sb, h) in enumerate(tiles):
            cs = slice(h * HEAD_DIM, (h + 1) * HEAD_DIM)
            o = _dot(p_scr[idx], v_all[sb * BAND:(sb + 2) * BAND, cs], "nn") * r_scr[idx]
            o_ref[sb * BAND:(sb + 1) * BAND, cs] = o.astype(BF16)

    def prev(r, t):
        return jnp.maximum(qb * t - 1, 0)

    cur = lambda c: pl.BlockSpec((None, rows, GROUP_W), lambda r, t, c=c: (r, t, c))
    prv = lambda c: pl.BlockSpec((None, BAND, GROUP_W), lambda r, t, c=c: (r, prev(r, t), c))
    out_spec = lambda width: pl.BlockSpec((None, rows, width), lambda r, t: (r, t, 0))
    shp = lambda width, dt: jax.ShapeDtypeStruct((dil, m_len, width), dt)
    n_t = len(tiles)
    return _pcall(body, name, (dil, steps), [a_g] * 5, [cur(0), cur(1), cur(2), prv(1), prv(2)],
                  [shp(GROUP_W, BF16), shp(HEAD_DIM, F32)], [out_spec(GROUP_W), out_spec(HEAD_DIM)],
                  [pltpu.VMEM((rows + BAND, GROUP_W), BF16), pltpu.VMEM((rows + BAND, GROUP_W), BF16),
                   pltpu.VMEM((n_t, BAND, 2 * BAND), F32), pltpu.VMEM((n_t, BAND, 2 * BAND), BF16),
                   pltpu.VMEM((n_t, BAND, HEAD_DIM), F32)], comm=comm)


def _attn_merge(os_, ls_, seq):
    def body(o0, l0, o1, l1, o2, l2, at_ref, lt_ref, sc, lsc):
        for gi, l_r in enumerate((l1, l2)):
            d = DILATIONS[gi + 1]
            for r in range(d):
                lsc.at[gi][pl.ds(r, ROW_BLK // d, stride=d), :] = l_r[r]
        lse = (l0.at[0], lsc.at[0], lsc.at[1])
        lane = lax.broadcasted_iota(jnp.int32, (ROW_BLK, HEAD_DIM), 1)
        lt_rows = jnp.zeros((ROW_BLK, HEAD_DIM), F32)
        for h in range(HEADS_PER_GROUP):
            cs = slice(h * HEAD_DIM, (h + 1) * HEAD_DIM)
            for gi, o_r in enumerate((o1, o2)):
                d = DILATIONS[gi + 1]
                for r in range(d):
                    sc.at[gi][pl.ds(r, ROW_BLK // d, stride=d), :] = o_r[r, :, cs].astype(F32)
            l_h = [v[:, h:h + 1] for v in lse]
            mx = jnp.maximum(jnp.maximum(l_h[0], l_h[1]), l_h[2])
            e = [jnp.exp(v - mx) for v in l_h]
            tot = e[0] + e[1] + e[2]
            inv = 1.0 / tot
            at_ref[:, cs] = ((e[0] * inv) * o0[0, :, cs].astype(F32) + (e[1] * inv) * sc[0] + (e[2] * inv) * sc[1]).astype(BF16)
            lt_rows = jnp.where(lane == h, jnp.broadcast_to(mx + jnp.log(tot), (ROW_BLK, HEAD_DIM)), lt_rows)
        lt_ref[...] = lt_rows

    go, gl = _group_specs(GROUP_W), _group_specs(HEAD_DIM)
    return pl.pallas_call(
        body, name="attn_merge",
        out_shape=[jax.ShapeDtypeStruct((seq, GROUP_W), BF16), jax.ShapeDtypeStruct((seq, HEAD_DIM), F32)],
        grid=(seq // ROW_BLK,), in_specs=[go[0], gl[0], go[1], gl[1], go[2], gl[2]],
        out_specs=[pl.BlockSpec((ROW_BLK, GROUP_W), lambda i: (i, 0)), pl.BlockSpec((ROW_BLK, HEAD_DIM), lambda i: (i, 0))],
        scratch_shapes=[pltpu.VMEM((2, ROW_BLK, HEAD_DIM), F32), pltpu.VMEM((2, ROW_BLK, HEAD_DIM), F32)],
        compiler_params=_params(dimension_semantics=("arbitrary",)),
    )(os_[0], ls_[0], os_[1], ls_[1], os_[2], ls_[2])


def _attn_bwd_prep(dattn, attn, lt):
    seq = dattn.shape[0]

    def body(da_ref, at_ref, lt_ref, cl0, d1, cl1, d2, cl2, sc, csc):
        lane = lax.broadcasted_iota(jnp.int32, (ROW_BLK, HEAD_DIM), 1)
        cl = pltpu.roll(lt_ref[...], HEADS_PER_GROUP, 1)
        for h in range(HEADS_PER_GROUP):
            cs = slice(h * HEAD_DIM, (h + 1) * HEAD_DIM)
            da = da_ref[:, cs].astype(F32)
            cc = jnp.sum(da * at_ref[:, cs].astype(F32), axis=-1, keepdims=True)
            cl = jnp.where(lane == h, jnp.broadcast_to(cc, (ROW_BLK, HEAD_DIM)), cl)
            sc[...] = da
            for g, d_ref in ((1, d1), (2, d2)):
                d = DILATIONS[g]
                for r in range(d):
                    d_ref[r, :, cs] = sc[pl.ds(r, ROW_BLK // d, stride=d), :].astype(BF16)
        cl0[0] = cl
        csc[...] = cl
        for g, c_ref in ((1, cl1), (2, cl2)):
            d = DILATIONS[g]
            for r in range(d):
                c_ref[r] = csc[pl.ds(r, ROW_BLK // d, stride=d), :]

    go, gl = _group_specs(GROUP_W), _group_specs(HEAD_DIM)
    row = lambda width: pl.BlockSpec((ROW_BLK, width), lambda i: (i, 0))
    shape = lambda g, width, dt: jax.ShapeDtypeStruct((DILATIONS[g], seq // DILATIONS[g], width), dt)
    cl0, d1, cl1, d2, cl2 = pl.pallas_call(
        body, name="attn_bwd_prep",
        out_shape=[shape(0, HEAD_DIM, F32), shape(1, GROUP_W, BF16), shape(1, HEAD_DIM, F32), shape(2, GROUP_W, BF16),
                   shape(2, HEAD_DIM, F32)],
        grid=(seq // ROW_BLK,), in_specs=[row(GROUP_W), row(GROUP_W), row(HEAD_DIM)],
        out_specs=[gl[0], go[1], gl[1], go[2], gl[2]],
        scratch_shapes=[pltpu.VMEM((ROW_BLK, HEAD_DIM), F32), pltpu.VMEM((ROW_BLK, HEAD_DIM), F32)],
        compiler_params=_params(dimension_semantics=("arbitrary",)),
    )(dattn, attn, lt)
    return [(dattn[None], cl0), (d1, cl1), (d2, cl2)]


def _attn_bwd(name, a_g, da_g, cl_g, comm=_NO_COMM):
    dil, m_len, _ = a_g.shape
    qb = min(QB, m_len // BAND)
    rows = qb * BAND
    steps = m_len // rows
    scale = HEAD_DIM ** -0.5

    tiles = [(sb, h) for sb in range(qb) for h in range(HEADS_PER_GROUP)]

    def body(q_ref, kc_ref, vc_ref, kp_ref, vp_ref, da_ref, cl_ref, d_ref, dk_acc, dv_acc, car_k, car_v,
             k_all, v_all, s_scr, dp_scr, p_scr, ds_scr):
        tg = pl.program_id(1)
        t = steps - 1 - tg

        @pl.when(tg == 0)
        def _():
            car_k[...] = jnp.zeros_like(car_k)
            car_v[...] = jnp.zeros_like(car_v)

        k_all[0:BAND, :] = kp_ref[...]
        k_all[BAND:, :] = kc_ref[...]
        v_all[0:BAND, :] = vp_ref[...]
        v_all[BAND:, :] = vc_ref[...]
        zero = jnp.zeros((rows, GROUP_W), F32)
        dk_acc[0:rows, :] = zero
        dv_acc[0:rows, :] = zero
        dk_acc[rows:rows + BAND, :] = car_k[...]
        dv_acc[rows:rows + BAND, :] = car_v[...]
        band, band_first = _band_masks_2(t)
        for idx, (sb, h) in enumerate(tiles):
            cs = slice(h * HEAD_DIM, (h + 1) * HEAD_DIM)
            rs, ks = slice(sb * BAND, (sb + 1) * BAND), slice(sb * BAND, (sb + 2) * BAND)
            s_scr[idx] = _dot(q_ref[rs, cs], k_all[ks, cs], "nt")
            dp_scr[idx] = _dot(da_ref[rs, cs], v_all[ks, cs], "nt")
        for idx, (sb, h) in enumerate(tiles):
            cs = slice(h * HEAD_DIM, (h + 1) * HEAD_DIM)
            rs = slice(sb * BAND, (sb + 1) * BAND)
            cc = jnp.broadcast_to(cl_ref[rs, h:h + 1], (BAND, 2 * BAND))
            ltv = jnp.broadcast_to(cl_ref[rs, HEADS_PER_GROUP + h:HEADS_PER_GROUP + h + 1], (BAND, 2 * BAND))
            p = jnp.exp(jnp.where(band_first if sb == 0 else band, s_scr[idx] * scale - ltv, NEG))
            p_scr[idx] = p.astype(BF16)
            ds_scr[idx] = (p * (dp_scr[idx] - cc) * scale).astype(BF16)
        for idx, (sb, h) in enumerate(tiles):
            cs = slice(h * HEAD_DIM, (h + 1) * HEAD_DIM)
            rs, ks = slice(sb * BAND, (sb + 1) * BAND), slice(sb * BAND, (sb + 2) * BAND)
            d_ref[rs, cs] = _dot(ds_scr[idx], k_all[ks, cs], "nn").astype(BF16)
            dk_acc[ks, cs] += _dot(ds_scr[idx], q_ref[rs, cs], "tn")
            dv_acc[ks, cs] += _dot(p_scr[idx], da_ref[rs, cs], "tn")
        d_ref[:, GROUP_W:2 * GROUP_W] = dk_acc[BAND:rows + BAND, :].astype(BF16)
        d_ref[:, 2 * GROUP_W:3 * GROUP_W] = dv_acc[BAND:rows + BAND, :].astype(BF16)
        car_k[...] = dk_acc[0:BAND, :]
        car_v[...] = dv_acc[0:BAND, :]

    def rev(tg):
        return steps - 1 - tg

    def prev(tg):
        return jnp.maximum(qb * rev(tg) - 1, 0)

    cur = lambda c: pl.BlockSpec((None, rows, GROUP_W), lambda r, tg, c=c: (r, rev(tg), c))
    prv = lambda c: pl.BlockSpec((None, BAND, GROUP_W), lambda r, tg, c=c: (r, prev(tg), c))
    return _pcall(
        body, name, (dil, steps), [a_g, a_g, a_g, a_g, a_g, da_g, cl_g],
        [cur(0), cur(1), cur(2), prv(1), prv(2), cur(0), pl.BlockSpec((None, rows, HEAD_DIM), lambda r, tg: (r, rev(tg), 0))],
        [jax.ShapeDtypeStruct((dil, m_len, ATTN_W), BF16)], [pl.BlockSpec((None, rows, ATTN_W), lambda r, tg: (r, rev(tg), 0))],
        [pltpu.VMEM((rows + BAND, GROUP_W), F32), pltpu.VMEM((rows + BAND, GROUP_W), F32),
         pltpu.VMEM((BAND, GROUP_W), F32), pltpu.VMEM((BAND, GROUP_W), F32),
         pltpu.VMEM((rows + BAND, GROUP_W), BF16), pltpu.VMEM((rows + BAND, GROUP_W), BF16),
         pltpu.VMEM((len(tiles), BAND, 2 * BAND), F32), pltpu.VMEM((len(tiles), BAND, 2 * BAND), F32),
         pltpu.VMEM((len(tiles), BAND, 2 * BAND), BF16), pltpu.VMEM((len(tiles), BAND, 2 * BAND), BF16)], comm=comm)


def _dqkv_post(d_gs, tabs, dz, comm=_NO_COMM):
    seq = dz.shape[0]

    def body(g0, g1, g2, c_ref, s1_ref, s2_ref, dz_any, o_ref, sc):
        del dz_any
        ins = (g0, g1, g2)
        c, s1, s2 = c_ref[...], s1_ref[...], s2_ref[...]
        for part in range(3):
            for hh in range(N_GROUPS * HEADS_PER_GROUP):
                g, hl = divmod(hh, HEADS_PER_GROUP)
                icol = part * GROUP_W + hl * HEAD_DIM
                ocol = part * ATTN_W + hh * HEAD_DIM
                d = DILATIONS[g]
                if d == 1:
                    x = ins[g][0, :, icol:icol + HEAD_DIM].astype(F32)
                else:
                    for r in range(d):
                        sc[pl.ds(r, ROW_BLK // d, stride=d), :] = ins[g][r, :, icol:icol + HEAD_DIM].astype(F32)
                    x = sc[...]
                if part < 2:
                    x = x * c + pltpu.roll(x * s1, HEAD_DIM - ROT_DIM // 2, 1) + pltpu.roll(x * s2, ROT_DIM // 2, 1)
                o_ref[:, ocol:ocol + HEAD_DIM] = x.astype(BF16)

    tab_spec = pl.BlockSpec((ROW_BLK, HEAD_DIM), lambda i: (i, 0))
    return _pcall(body, "dqkv_post", (seq // ROW_BLK,), [*d_gs, *tabs, dz],
                  _group_specs(ATTN_W) + [tab_spec, tab_spec, tab_spec, pl.BlockSpec(memory_space=pl.ANY)],
                  [jax.ShapeDtypeStruct(dz.shape, BF16)], [pl.BlockSpec((ROW_BLK, QKV_W), lambda i: (i, 0))],
                  [pltpu.VMEM((ROW_BLK, HEAD_DIM), F32)], aliases={6: 0}, comm=comm)


def _glu(zg):
    a = zg[:, :CONV_CH].astype(F32)
    s = _sigmoid(zg[:, CONV_CH:].astype(F32))
    return a, s, a * s


def _shifted_copies(xs):
    n = xs.shape[1] - SUBLANES
    for b in range(1, SUBLANES):
        xs[b, 0:n, :] = xs[0, pl.ds(b, n), :]


def _shifted(xs, offset, r0, cs):
    a, b = divmod(offset, SUBLANES)
    return xs[b, pl.ds(SUBLANES * a + r0, CONV_ROWS), cs]


def _conv_fwd(z, cw, cb, lg, lb, comm=_NO_COMM):
    seq = z.shape[0]
    halo_per_blk = ROW_BLK // CONV_HALO

    def body(zg_ref, zh_ref, cw_ref, cb_ref, lg_ref, lb_ref, c2_ref, c4_ref, xs):
        i = pl.program_id(0)
        _, _, c1 = _glu(zg_ref[...])
        _, _, c1h = _glu(zh_ref[...])
        xs[0, 0:CONV_HALO, :] = jnp.where(i > 0, c1h, 0.0)
        xs[0, CONV_HALO:, :] = c1
        _shifted_copies(xs)
        for s in range(CONV_CH // HEAD_DIM):
            cs = slice(s * HEAD_DIM, (s + 1) * HEAD_DIM)
            taps = [cw_ref[j:j + 1, cs] for j in range(CONV_K)]
            bias = cb_ref[:, cs]

            def chunk(rc, carry, cs=cs, taps=taps, bias=bias):
                r0 = pl.multiple_of(rc * CONV_ROWS, CONV_ROWS)
                acc = [jnp.zeros((CONV_ROWS, HEAD_DIM), F32)] * 2
                for j in range(CONV_K):
                    acc[j % 2] = acc[j % 2] + taps[j] * _shifted(xs, CONV_HALO - (CONV_K - 1) + j, r0, cs)
                c2_ref[pl.ds(r0, CONV_ROWS), cs] = acc[0] + acc[1] + bias
                return carry

            lax.fori_loop(0, ROW_BLK // CONV_ROWS, chunk, 0)
        c2 = c2_ref[...]
        mu = jnp.mean(c2, axis=-1, keepdims=True)
        xc = c2 - mu
        rstd = lax.rsqrt(jnp.mean(xc * xc, axis=-1, keepdims=True) + EPS)
        c3 = xc * rstd * lg_ref[...] + lb_ref[...]
        c4_ref[...] = (c3 * _sigmoid(c3)).astype(BF16)

    vec = pl.BlockSpec((1, CONV_CH), lambda i: (0, 0))
    return _pcall(
        body, "conv_fwd", (seq // ROW_BLK,), [z, z, cw, cb, lg, lb],
        [pl.BlockSpec((ROW_BLK, 2 * CONV_CH), lambda i: (i, GLU_COL_BLK)),
         pl.BlockSpec((CONV_HALO, 2 * CONV_CH), lambda i: (jnp.maximum(i * halo_per_blk - 1, 0), GLU_COL_BLK)),
         pl.BlockSpec((CONV_HALO, CONV_CH), lambda i: (0, 0)), vec, vec, vec],
        [jax.ShapeDtypeStruct((seq, CONV_CH), F32), jax.ShapeDtypeStruct((seq, CONV_CH), BF16)],
        [pl.BlockSpec((ROW_BLK, CONV_CH), lambda i: (i, 0)), pl.BlockSpec((ROW_BLK, CONV_CH), lambda i: (i, 0))],
        [pltpu.VMEM((SUBLANES, ROW_BLK + CONV_HALO, CONV_CH), F32)], comm=comm)


def _conv_bwd(dc2, z, cw, dz, comm=_NO_COMM):
    seq = z.shape[0]
    halo_per_blk = ROW_BLK // CONV_HALO
    n_blk = seq // ROW_BLK
    last_halo = seq // CONV_HALO - 1

    def body(dc_ref, dn_ref, zg_ref, zh_ref, cw_ref, dz_any, o_ref, dcw_ref, xs, ys, dc1_ref, dcw_acc):
        del dz_any
        i = pl.program_id(0)
        a, s, c1 = _glu(zg_ref[...])
        _, _, c1h = _glu(zh_ref[...])
        xs[0, 0:CONV_HALO, :] = jnp.where(i > 0, c1h, 0.0)
        xs[0, CONV_HALO:, :] = c1
        ys[0, 0:ROW_BLK, :] = dc_ref[...]
        ys[0, ROW_BLK:, :] = jnp.where(i < n_blk - 1, dn_ref[...], 0.0)
        _shifted_copies(xs)
        _shifted_copies(ys)

        @pl.when(i == 0)
        def _():
            dcw_acc[...] = jnp.zeros_like(dcw_acc)

        for sl in range(CONV_CH // HEAD_DIM):
            cs = slice(sl * HEAD_DIM, (sl + 1) * HEAD_DIM)
            taps = [cw_ref[j:j + 1, cs] for j in range(CONV_K)]

            def chunk(rc, carry, cs=cs, taps=taps):
                r0 = pl.multiple_of(rc * CONV_ROWS, CONV_ROWS)
                dc = ys[0, pl.ds(r0, CONV_ROWS), cs]
                acc = [jnp.zeros((CONV_ROWS, HEAD_DIM), F32)] * 2
                for j in range(CONV_K):
                    prod = dc * _shifted(xs, CONV_HALO - (CONV_K - 1) + j, r0, cs)
                    dcw_acc[j, :, cs] += jnp.sum(prod.reshape(CONV_ROWS // SUBLANES, SUBLANES, HEAD_DIM), axis=0)
                    acc[j % 2] = acc[j % 2] + taps[j] * _shifted(ys, CONV_K - 1 - j, r0, cs)
                dc1_ref[pl.ds(r0, CONV_ROWS), cs] = acc[0] + acc[1]
                return carry

            lax.fori_loop(0, ROW_BLK // CONV_ROWS, chunk, 0)
        dc1 = dc1_ref[...]
        o_ref[:, :CONV_CH] = (dc1 * s).astype(BF16)
        o_ref[:, CONV_CH:] = (dc1 * a * s * (1.0 - s)).astype(BF16)

        @pl.when(i == n_blk - 1)
        def _():
            dcw_ref[...] = jnp.sum(dcw_acc[...], axis=1)

    return _pcall(
        body, "conv_bwd", (n_blk,), [dc2, dc2, z, z, cw, dz],
        [pl.BlockSpec((ROW_BLK, CONV_CH), lambda i: (i, 0)),
         pl.BlockSpec((CONV_HALO, CONV_CH), lambda i: (jnp.minimum((i + 1) * halo_per_blk, last_halo), 0)),
         pl.BlockSpec((ROW_BLK, 2 * CONV_CH), lambda i: (i, GLU_COL_BLK)),
         pl.BlockSpec((CONV_HALO, 2 * CONV_CH), lambda i: (jnp.maximum(i * halo_per_blk - 1, 0), GLU_COL_BLK)),
         pl.BlockSpec((CONV_HALO, CONV_CH), lambda i: (0, 0)),
         pl.BlockSpec(memory_space=pl.ANY)],
        [jax.ShapeDtypeStruct(dz.shape, BF16), jax.ShapeDtypeStruct((CONV_HALO, CONV_CH), F32)],
        [pl.BlockSpec((ROW_BLK, 2 * CONV_CH), lambda i: (i, GLU_COL_BLK)), pl.BlockSpec((CONV_HALO, CONV_CH), lambda i: (0, 0))],
        [pltpu.VMEM((SUBLANES, ROW_BLK + CONV_HALO, CONV_CH), F32), pltpu.VMEM((SUBLANES, ROW_BLK + CONV_HALO, CONV_CH), F32),
         pltpu.VMEM((ROW_BLK, CONV_CH), F32), pltpu.VMEM((CONV_HALO, SUBLANES, CONV_CH), F32)],
        aliases={5: 0}, comm=comm)


def _epi_mix(ya, c4, wcp, gates, bg):
    yc = _dot(c4, wcp, "nn")
    gv = _sigmoid(gates.astype(F32) + bg)
    merged = gv[:, :D_MODEL] * ya + gv[:, D_MODEL:] * yc
    return merged, ya, yc


def _epi_residual_rms(acc, xres, g):
    x = xres + acc
    return x, _rms_fwd_vals(x, g)


def _cross_scores(cq, ck):
    out = []
    for h in range(CROSS_HEADS):
        cs = slice(h * CROSS_HD, (h + 1) * CROSS_HD)
        s = _dot(cq[:, cs], ck[:, cs], "nt") * (CROSS_HD ** -0.5)
        e = jnp.exp(s - jnp.max(s, axis=-1, keepdims=True))
        out.append((cs, e, jnp.sum(e, axis=-1, keepdims=True)))
    return out


def _epi_cross_fwd(acc, ck, cv):
    cq = acc.astype(BF16)
    co = [_dot(e, cv[:, cs], "nn") / den for cs, e, den in _cross_scores(cq, ck)]
    return cq, jnp.concatenate(co, axis=1)


def _epi_cross_bwd(dco, cq, ck, cv):
    dco = dco.astype(BF16)
    dcq, dck, dcv = [], [], []
    for cs, e, den in _cross_scores(cq, ck):
        p = e / den
        dp = _dot(dco[:, cs], cv[:, cs], "nt")
        ds = (p * (dp - jnp.sum(dp * p, axis=-1, keepdims=True)) * (CROSS_HD ** -0.5)).astype(BF16)
        dcq.append(_dot(ds, ck[:, cs], "nn"))
        dck.append(_dot(ds, cq[:, cs], "tn"))
        dcv.append(_dot(p, dco[:, cs], "tn"))
    return jnp.concatenate(dcq, axis=1), jnp.concatenate(dck, axis=1), jnp.concatenate(dcv, axis=1)


def _epi_mlp_up(acc):
    return acc, jnp.square(jnp.maximum(acc, 0.0))


def _epi_final(acc, x2, tgt, g):
    x3 = x2 + acc
    err = _rms_fwd_vals(x3, g) - tgt
    loss = (0.5 / D_MODEL) * jnp.sum(err * err)
    dx3, dg = _rms_bwd_vals(x3, g, err * (1.0 / D_MODEL))
    return dx3, jnp.full((1, HEAD_DIM), loss, F32), dg


def _epi_mlp_down_bwd(dh, hpre):
    return (dh * 2.0 * jnp.maximum(hpre.astype(F32), 0.0),)


def _epi_rms_bwd(du, x, g, dres):
    dx, dg = _rms_bwd_vals(x, g, du)
    return dres.astype(F32) + dx, dg


def _epi_rms_bwd_g(du, x, g):
    return (_rms_bwd_vals(x, g, du)[1],)


def _epi_mix_bwd(dm, ya, yc, gates, bg):
    gv = _sigmoid(gates.astype(F32) + bg)
    ga, gb = gv[:, :D_MODEL], gv[:, D_MODEL:]
    ya, yc = ya.astype(F32), yc.astype(F32)
    dgate = jnp.concatenate([dm * ya * ga * (1.0 - ga), dm * yc * gb * (1.0 - gb)], axis=1)
    return dm * ga, dm * gb, dgate, jnp.sum(dgate, axis=0, keepdims=True)


def _epi_ln_bwd(dc4, c2, lg, lb):
    mu = jnp.mean(c2, axis=-1, keepdims=True)
    xc = c2 - mu
    rstd = lax.rsqrt(jnp.mean(xc * xc, axis=-1, keepdims=True) + EPS)
    xh = xc * rstd
    c3 = xh * lg + lb
    sg = _sigmoid(c3)
    dc3 = dc4 * sg * (1.0 + c3 * (1.0 - sg))
    dxh = dc3 * lg
    dc2 = rstd * (dxh - jnp.mean(dxh, axis=-1, keepdims=True) - xh * jnp.mean(dxh * xh, axis=-1, keepdims=True))
    return (dc2, jnp.sum(dc3 * xh, axis=0, keepdims=True), jnp.sum(dc3, axis=0, keepdims=True),
            jnp.sum(dc2, axis=0, keepdims=True))


def _sds(shape, dtype):
    return jax.ShapeDtypeStruct(shape, dtype)


class _Lazy:
    def __init__(self, fn):
        self.fn = fn

    def __getitem__(self, key):
        return self.fn(key)


def _local_step(x, mem, tgt, sm, plan):
    w = _Lazy(plan.w)
    dw = {}

    def carry(name, n_own, fn, *args, **kw):
        c = plan.comm(name, dw)
        res = fn(*args, comm=c, **kw)
        plan.done(name, res[n_own:])
        return res[:n_own]

    def mm(name, *args, **kw):
        return carry(name, len(args[6]), _mm, name, *args, **kw)

    seq = x.shape[0]
    nr = seq // ROW_BLK
    big = min(1024, seq)
    nb = seq // big
    row = lambda n: ((ROW_BLK, n), lambda i, j, k: (i, 0))
    vec = lambda n: ((1, n), lambda i, j, k: (0, 0))
    full = lambda r, c: ((r, c), lambda i, j, k: (0, 0))
    gates_blk = ((ROW_BLK, 2 * D_MODEL), lambda i, j, k: (i, GATE_COL_BLK))
    tabs = _rope_tables(seq)

    whole3 = lambda a: (a.shape, lambda i, j, k: (0, 0, 0))
    u, z = plan.project_in(x, sm["g_mix"])
    a_gs = carry("qkv_prep", 3, _qkv_prep, z, tabs)
    os_, ls_ = [], []
    for g in range(N_GROUPS):
        name = "attn_fwd_%d" % g
        o_g, l_g = carry(name, 2, _attn_fwd, name, a_gs[g])
        os_.append(o_g)
        ls_.append(l_g)
    attn, lt = _attn_merge(os_, ls_, seq)
    c2, c4 = carry("conv_fwd", 2, _conv_fwd, z, w["taps"], sm["conv_b"], sm["conv_ln_g"], sm["conv_ln_b"])
    merged, ya, yc = mm(
        "mix", attn, w["w_attn_proj"], "nn", (nr, 1, 1), row(GROUP_W), full(GROUP_W, D_MODEL),
        [(_sds((seq, D_MODEL), BF16), *row(D_MODEL))] * 3,
        extras=[(c4, *row(CONV_CH)), (w["w_conv_proj"], *full(CONV_CH, D_MODEL)), (z, *gates_blk), (sm["b_gate"], *vec(2 * D_MODEL))],
        epi=_epi_mix)
    x1, uq = mm("out_proj", merged, w["w_out"], "nn", (nr, 1, 1), row(D_MODEL), full(D_MODEL, D_MODEL),
                 [(_sds((seq, D_MODEL), F32), *row(D_MODEL)), (_sds((seq, D_MODEL), BF16), *row(D_MODEL))],
                 extras=[(x, *row(D_MODEL)), (sm["g_cross"], *vec(D_MODEL))], epi=_epi_residual_rms)

    mn = _rms_fwd("rms_mem", mem, sm["g_mem"], N_MEM)[0]
    ckv = mm("ckv_proj", mn, w["w_ckv"], "nn", (1, N_DEV, 1), full(N_MEM, D_MODEL),
              ((None, D_MODEL, 2 * D_MODEL // N_DEV), lambda i, j, k: (j, 0, 0)),
              [(_sds((N_MEM, 2 * D_MODEL), BF16), (N_MEM, 2 * D_MODEL // N_DEV), lambda i, j, k: (0, j))])[0]
    ck, cv = ckv[:, :D_MODEL], ckv[:, D_MODEL:]
    kv_blk = full(N_MEM, D_MODEL)
    cq, co = mm("cq_proj_cross", uq, w["w_cq"], "nn", (nr, 1, 1), row(D_MODEL), full(D_MODEL, D_MODEL),
                 [(_sds((seq, D_MODEL), BF16), *row(D_MODEL))] * 2,
                 extras=[(ck, *kv_blk), (cv, *kv_blk)], epi=_epi_cross_fwd)
    x2, um = mm("co_proj", co, w["w_co"], "nn", (nr, 1, 1), row(D_MODEL), full(D_MODEL, D_MODEL),
                 [(_sds((seq, D_MODEL), F32), *row(D_MODEL)), (_sds((seq, D_MODEL), BF16), *row(D_MODEL))],
                 extras=[(x1, *row(D_MODEL)), (sm["g_mlp"], *vec(D_MODEL))], epi=_epi_residual_rms)

    ff_blk = D_FF // N_DEV
    row_f32 = (_sds((seq, D_MODEL), F32), *row(D_MODEL))
    row_bf16 = (_sds((seq, D_MODEL), BF16), *row(D_MODEL))
    col_sum = (_sds((1, D_MODEL), F32), *vec(D_MODEL))
    hpre, h = mm("mlp_up", um, w["w_up"], "nn", (nr, 1, 1), row(D_MODEL), whole3(w["w_up"]),
                 [(_sds((seq, D_FF), BF16), *row(D_FF))] * 2, epi=_epi_mlp_up, split=("cols", N_DEV), b_resident=True)
    kt = D_FF // D_MODEL
    dx3, loss, dg_final = mm(
        "mlp_down_loss", h, w["w_down"], "nn", (nr, 1, 1), row(D_FF), full(D_FF, D_MODEL),
        [row_bf16, (_sds((1, HEAD_DIM), F32), *vec(HEAD_DIM)), col_sum],
        extras=[(x2, *row(D_MODEL)), (tgt, *row(D_MODEL)), (sm["g_final"], *vec(D_MODEL))], epi=_epi_final, acc_outs=(1, 2),
        b_resident=True)

    dhpre = mm("mlp_down_bwd", dx3, w["w_down"], "nt", (nr, 1, 1), row(D_MODEL), full(D_FF, D_MODEL),
               [(_sds((seq, D_FF), BF16), *row(D_FF))], extras=[(hpre, *row(D_FF))], epi=_epi_mlp_down_bwd,
               split=("cols", kt), b_resident=True)[0]
    big2 = min(2 * big, seq)
    nb2 = seq // big2
    dw["w_down"] = mm("dw_down", h, dx3, "tn", (kt, 1, nb2), ((big2, D_MODEL), lambda i, j, k: (k, i)),
                      ((big2, D_MODEL), lambda i, j, k: (k, 0)),
                      [(_sds((D_FF, D_MODEL), BF16), (D_MODEL, D_MODEL), lambda i, j, k: (i, 0))])[0]
    dx2, dg_mlp = mm("mlp_up_bwd", dhpre, w["w_up"], "nt", (nr, 1, 1), row(D_FF), whole3(w["w_up"]),
                     [row_bf16, col_sum],
                     extras=[(x2, *row(D_MODEL)), (sm["g_mlp"], *vec(D_MODEL)), (dx3, *row(D_MODEL))],
                     epi=_epi_rms_bwd, acc_outs=(1,), split=("sum", N_DEV), b_resident=True)
    dw["w_up"] = mm("dw_up", um, dhpre, "tn", (1, N_DEV, nb2), ((big2, D_MODEL), lambda i, j, k: (k, 0)),
                    ((big2, ff_blk), lambda i, j, k: (k, j)),
                    [(_sds((N_DEV, D_MODEL, ff_blk), BF16), (None, D_MODEL, ff_blk), lambda i, j, k: (j, 0, 0))])[0]

    acc_kv = (_sds((N_MEM, D_MODEL), F32), *kv_blk)
    dcq, dck, dcv = mm("co_proj_bwd_cross", dx2, w["w_co"], "nt", (nr, 1, 1), row(D_MODEL), full(D_MODEL, D_MODEL),
                       [row_bf16, acc_kv, acc_kv],
                       extras=[(cq, *row(D_MODEL)), (ck, *kv_blk), (cv, *kv_blk)], epi=_epi_cross_bwd, acc_outs=(1, 2))

    def dw_square(name, act, grad):
        return mm(name, act, grad, "tn", (1, 1, nb2), ((big2, D_MODEL), lambda i, j, k: (k, 0)),
                  ((big2, D_MODEL), lambda i, j, k: (k, 0)), [(_sds((D_MODEL, D_MODEL), BF16), *full(D_MODEL, D_MODEL))])[0]

    dw["w_co"] = dw_square("dw_co", co, dx2)
    dx1, dg_cross = mm("cq_proj_bwd", dcq, w["w_cq"], "nt", (nr, 1, 1), row(D_MODEL), full(D_MODEL, D_MODEL),
                       [row_bf16, col_sum],
                       extras=[(x1, *row(D_MODEL)), (sm["g_cross"], *vec(D_MODEL)), (dx2, *row(D_MODEL))],
                       epi=_epi_rms_bwd, acc_outs=(1,))
    dw["w_cq"] = dw_square("dw_cq", uq, dcq)
    dckv = jnp.concatenate([dck, dcv], axis=1)
    kv_chunk = 2 * D_MODEL // N_DEV
    dw["w_ckv"] = mm("dw_ckv", mn, dckv, "tn", (1, N_DEV, 1), full(N_MEM, D_MODEL), ((N_MEM, kv_chunk), lambda i, j, k: (0, j)),
                      [(_sds((N_DEV, D_MODEL, kv_chunk), BF16), (None, D_MODEL, kv_chunk), lambda i, j, k: (j, 0, 0))])[0]
    dg_mem = mm("ckv_proj_bwd", dckv, w["w_ckv"], "nt", (1, 1, N_DEV), ((N_MEM, kv_chunk), lambda i, j, k: (0, k)),
                 ((None, D_MODEL, kv_chunk), lambda i, j, k: (k, 0, 0)), [(_sds((1, D_MODEL), F32), *vec(D_MODEL))],
                 extras=[(mem, *full(N_MEM, D_MODEL)), (sm["g_mem"], *vec(D_MODEL))], epi=_epi_rms_bwd_g, acc_outs=(0,))[0]

    dya, dyc, dz, db_gate = mm(
        "out_proj_bwd_mix", dx1, w["w_out"], "nt", (nr, 1, 1), row(D_MODEL), full(D_MODEL, D_MODEL),
        [(_sds((seq, D_MODEL), BF16), *row(D_MODEL)), (_sds((seq, D_MODEL), BF16), *row(D_MODEL)),
         (_sds((seq, IN_W), BF16), *gates_blk), (_sds((1, 2 * D_MODEL), F32), *vec(2 * D_MODEL))],
        extras=[(ya, *row(D_MODEL)), (yc, *row(D_MODEL)), (z, *gates_blk), (sm["b_gate"], *vec(2 * D_MODEL))],
        epi=_epi_mix_bwd, acc_outs=(3,))
    dw["w_out"] = dw_square("dw_out", merged, dx1)
    dattn = mm("attn_proj_bwd", dya, w["w_attn_proj"], "nt", (nr, 1, 1), row(D_MODEL), full(GROUP_W, D_MODEL),
                [(_sds((seq, GROUP_W), BF16), *row(GROUP_W))])[0]
    pc = D_MODEL // N_DEV
    dw["w_attn_proj"] = mm("dw_attn_proj", attn, dya, "tn", (1, 1, nb2), ((big2, GROUP_W), lambda i, j, k: (k, 0)),
                           ((big2, D_MODEL), lambda i, j, k: (k, 0)),
                           [(_sds((N_DEV, GROUP_W, pc), BF16), (N_DEV, GROUP_W, pc), lambda i, j, k: (0, 0, 0))],
                           out_chunks=N_DEV)[0]
    cvec = (_sds((1, CONV_CH), F32), *vec(CONV_CH))
    dc2, dg_ln_g, dg_ln_b, dg_conv_b = mm(
        "conv_proj_bwd_ln", dyc, w["w_conv_proj"], "nt", (nr, 1, 1), row(D_MODEL), full(CONV_CH, D_MODEL),
        [(_sds((seq, CONV_CH), F32), *row(CONV_CH)), cvec, cvec, cvec],
        extras=[(c2, *row(CONV_CH)), (sm["conv_ln_g"], *vec(CONV_CH)), (sm["conv_ln_b"], *vec(CONV_CH))],
        epi=_epi_ln_bwd, acc_outs=(1, 2, 3))
    dw["w_conv_proj"] = mm("dw_conv_proj", c4, dyc, "tn", (1, 1, nb2), ((big2, CONV_CH), lambda i, j, k: (k, 0)),
                           ((big2, D_MODEL), lambda i, j, k: (k, 0)),
                           [(_sds((N_DEV, CONV_CH, pc), BF16), (N_DEV, CONV_CH, pc), lambda i, j, k: (0, 0, 0))],
                           out_chunks=N_DEV)[0]
    dz, dg_conv_w = carry("conv_bwd", 2, _conv_bwd, dc2, z, w["taps"], dz)
    preps = _attn_bwd_prep(dattn, attn, lt)
    d_gs = []
    for g in range(N_GROUPS):
        name = "attn_bwd_%d" % g
        d_gs.append(carry(name, 1, _attn_bwd, name, a_gs[g], *preps[g])[0])
    dz = carry("dqkv_post", 1, _dqkv_post, d_gs, tabs, dz)[0]
    dw["w_in"] = mm("dw_in", u, dz, "tn", (1, N_DEV, nb2), ((big2, D_MODEL), lambda i, j, k: (k, 0)),
                    ((big2, D_MODEL), lambda i, j, k: (k, j)),
                    [(_sds((N_DEV, D_MODEL, D_MODEL), BF16), (None, D_MODEL, D_MODEL), lambda i, j, k: (j, 0, 0))])[0]
    token = plan.start_w_in(dw["w_in"])
    grad_x, dg_mix = mm("in_proj_bwd", dz, w["w_in"], "nt", (nr, 1, 1), row(IN_W), whole3(w["w_in"]), [row_f32, col_sum],
                        extras=[(x, *row(D_MODEL)), (sm["g_mix"] + token, *vec(D_MODEL)), (dx1, *row(D_MODEL))],
                        epi=_epi_rms_bwd, acc_outs=(1,), split=("sum", N_DEV), b_resident=True)
    small = dict(g_mix=dg_mix, b_gate=db_gate, conv_b=dg_conv_b, conv_ln_g=dg_ln_g, conv_ln_b=dg_ln_b, g_cross=dg_cross,
                 g_mem=dg_mem, g_mlp=dg_mlp, g_final=dg_final, loss=loss, conv_w=dg_conv_w)
    return grad_x, dw, small


SHARD_SHAPE = dict(w_in=(1024, 1024), w_attn_proj=(512, 128), w_conv_proj=(768, 128), w_out=(128, 1024), w_cq=(128, 1024),
                   w_ckv=(1024, 256), w_co=(128, 1024), w_up=(1024, 512), w_down=(512, 1024))
FWD_CARRY = {"in_proj":("w_attn_proj", "w_conv_proj", "w_out", "w_cq", "w_ckv", "w_co", "taps"),
             "conv_fwd": ("w_up", "w_down")}
BWD_CARRY = {"dw_up": ("w_down",), "out_proj_bwd_mix": ("w_co", "w_cq"), "conv_bwd": ("w_up", "w_ckv"),
             "dqkv_post": ("w_out", "w_attn_proj", "w_conv_proj")}


def _cols_to_2d(a):
    return a.transpose(1, 0, 2).reshape(a.shape[1], -1)


def _in_proj_gather(resid, g, w_shard, comm):
    seq = resid.shape[0]
    tm = min(1024, seq)
    x, y, c = lax.axis_index("x"), lax.axis_index("y"), lax.axis_index("c")
    ident = lambda px, py, pc: 4 * px + 2 * py + pc
    far = [(1 - x, y), (x, 1 - y), (1 - x, 1 - y)]
    order = jnp.stack([ident(x, y, c), ident(x, y, 1 - c), ident(*far[0], c), ident(*far[1], c), ident(*far[0], 1 - c),
                       ident(*far[1], 1 - c), ident(*far[2], c), ident(*far[2], 1 - c)]).astype(jnp.int32)
    forward_at = {2: 0, 3: 1, 6: 2}
    n_far = len(far)

    def body(order_ref, x_ref, g_ref, wsh_ref, *rest):
        c_in, z_ref, u_ref, wg_ref = rest[:comm.n], rest[comm.n], rest[comm.n + 1], rest[comm.n + 2]
        c_out = rest[comm.n + 3:2 * comm.n + 3]
        wbuf, u_all, load_sem, local_sem, recv_sems, ici_send, d2d_send = rest[2 * comm.n + 3:2 * comm.n + 10]
        sems = rest[2 * comm.n + 10:]
        jj, i = pl.program_id(0), pl.program_id(1)
        (kx, ky, kc), me, chips = _Comm._where()
        sibling = (kx, ky, 1 - kc)
        n = order_ref[jj]

        def push(src, blk, send, to):
            return pltpu.make_async_remote_copy(src_ref=src, dst_ref=wg_ref.at[blk], send_sem=send,
                                                recv_sem=recv_sems.at[blk], device_id=to, device_id_type=MESH)

        def load(src):
            cp = pltpu.make_async_copy(src, wbuf, load_sem)
            cp.start()
            cp.wait()

        @pl.when(jnp.logical_and(jj == 0, i == 0))
        def _():
            push(wsh_ref, me, d2d_send, sibling).start()
            for (px, py) in chips[:2]:
                push(wsh_ref, me, ici_send, (px, py, kc)).start()
            pltpu.make_async_copy(wsh_ref, wg_ref.at[me], local_sem).start()
            load(wsh_ref)

        @pl.when(jnp.logical_and(jj > 0, i == 0))
        def _():
            push(wg_ref.at[n], n, d2d_send, sibling).wait_recv()
            for step, k in forward_at.items():
                @pl.when(jj == step)
                def _(k=k):
                    blk = 4 * chips[k][0] + 2 * chips[k][1] + kc
                    push(wg_ref.at[blk], blk, d2d_send, sibling).start()

            @pl.when(jj == 2)
            def _():
                push(wsh_ref, me, ici_send, (*chips[2], kc)).start()

            if comm.n:
                @pl.when(jj == 3)
                def _():
                    comm.start(c_in, c_out, sems)

            load(wg_ref.at[n])

        rows = pl.ds(pl.multiple_of(i * tm, tm), tm)

        @pl.when(jj == 0)
        def _():
            u_val = _rms_fwd_vals(x_ref[...], g_ref[...]).astype(BF16)
            u_all[rows, :] = u_val
            u_ref[...] = u_val
            z_ref[...] = _dot(u_val, wbuf[...], "nn").astype(BF16)

        @pl.when(jj > 0)
        def _():
            z_ref[...] = _dot(u_all[rows, :], wbuf[...], "nn").astype(BF16)

        @pl.when(jnp.logical_and(jj == N_DEV - 1, i == pl.num_programs(1) - 1))
        def _():
            def drain_sends(send, count):
                blocks = wg_ref.at[pl.ds(0, count)]
                pltpu.make_async_remote_copy(src_ref=blocks, dst_ref=blocks, send_sem=send, recv_sem=recv_sems.at[0],
                                             device_id=sibling, device_id_type=MESH).wait_send()

            drain_sends(ici_send, n_far)
            drain_sends(d2d_send, n_far + 1)
            pltpu.make_async_copy(wsh_ref, wg_ref.at[me], local_sem).wait()
            if comm.n:
                comm.wait(c_in, c_out, sems)

    any_spec = pl.BlockSpec(memory_space=pl.ANY)
    n_i = seq // tm
    first_pass = lambda jj, i, order_ref: (jnp.where(jj == 0, i, n_i - 1), 0)
    grid_spec = pltpu.PrefetchScalarGridSpec(
        num_scalar_prefetch=1, grid=(N_DEV, n_i),
        in_specs=[pl.BlockSpec((tm, D_MODEL), first_pass), pl.BlockSpec((1, D_MODEL), lambda jj, i, order_ref: (0, 0)),
                  any_spec] + comm.in_specs,
        out_specs=[pl.BlockSpec((tm, D_MODEL), lambda jj, i, order_ref: (i, order_ref[jj])),
                   pl.BlockSpec((tm, D_MODEL), first_pass), any_spec] + comm.out_specs,
        scratch_shapes=[pltpu.VMEM((D_MODEL, D_MODEL), BF16), pltpu.VMEM((seq, D_MODEL), BF16), pltpu.SemaphoreType.DMA,
                        pltpu.SemaphoreType.DMA, pltpu.SemaphoreType.DMA((N_DEV,)), pltpu.SemaphoreType.DMA,
                        pltpu.SemaphoreType.DMA] + comm.scratch)
    return pl.pallas_call(
        body, name="in_proj_gather", grid_spec=grid_spec,
        out_shape=[jax.ShapeDtypeStruct((seq, IN_W), BF16), jax.ShapeDtypeStruct((seq, D_MODEL), BF16),
                   jax.ShapeDtypeStruct((N_DEV, D_MODEL, D_MODEL), BF16)] + comm.out_shape,
        compiler_params=_params(dimension_semantics=("arbitrary", "arbitrary")),
    )(order, resid, g, w_shard, *comm.arrays)


_HBM = pl.BlockSpec(memory_space=pltpu.HBM)
_SEM = pl.BlockSpec(memory_space=pltpu.SEMAPHORE)


def _chunks_start(dw):
    def body(src_ref, land_ref, send_sem, recv_sem, src_thru, land_thru, token):
        del src_thru, land_thru
        me, peers = _peers()
        for (px, py, pc) in peers:
            pltpu.make_async_remote_copy(src_ref=src_ref.at[4 * px + 2 * py + pc], dst_ref=land_ref.at[me], send_sem=send_sem,
                                         recv_sem=recv_sem, device_id=(px, py, pc), device_id_type=MESH).start()
        token[...] = jnp.zeros_like(token)

    return pl.pallas_call(
        body, name="w_in_grad_start",
        out_shape=(pltpu.SemaphoreType.DMA(()), pltpu.SemaphoreType.DMA(()), pltpu.HBM(dw.shape, dw.dtype),
                   pltpu.HBM(dw.shape, dw.dtype), jax.ShapeDtypeStruct((SUBLANES, HEAD_DIM), F32)),
        in_specs=(_HBM, _HBM), out_specs=(_SEM, _SEM, _HBM, _HBM, pl.BlockSpec(memory_space=pltpu.VMEM)),
        input_output_aliases={0: 2, 1: 3},
        compiler_params=pltpu.CompilerParams(has_side_effects=pltpu.SideEffectType.DATAFLOW_SIDE_EFFECTING),
    )(pltpu.with_memory_space_constraint(dw, pltpu.HBM),
      pltpu.with_memory_space_constraint(lax.empty(dw.shape, dw.dtype), pltpu.HBM))


def _chunks_wait(send_sem, recv_sem, src_thru, land_thru, after):
    def body(src_ref, land_ref, send_sem, recv_sem, after_ref, src_dead, land_out):
        del after_ref, src_dead, land_out
        seven = land_ref.at[pl.ds(0, N_DEV - 1)]
        cp = pltpu.make_async_remote_copy(src_ref=seven, dst_ref=seven, send_sem=send_sem, recv_sem=recv_sem,
                                          device_id=_peers()[1][0], device_id_type=MESH)
        cp.wait_send()
        cp.wait_recv()

    return pl.pallas_call(
        body, name="w_in_grad_wait",
        out_shape=(pltpu.HBM(src_thru.shape, src_thru.dtype), pltpu.HBM(land_thru.shape, land_thru.dtype)),
        in_specs=(_HBM, _HBM, _SEM, _SEM, pl.BlockSpec(memory_space=pl.ANY)), out_specs=(_HBM, _HBM),
        input_output_aliases={0: 0, 1: 1},
        compiler_params=pltpu.CompilerParams(has_side_effects=pltpu.SideEffectType.DATAFLOW_SIDE_EFFECTING),
    )(src_thru, land_thru, send_sem, recv_sem, after)


class _Plan:
    def __init__(self, shards, n_tap_cols):
        self.shards = shards
        self.gathered = {}
        self.parts = {}
        self.n_tap_cols = n_tap_cols

    def project_in(self, x, g):
        comm = self.comm("in_proj", None)
        res = _in_proj_gather(x, g, self.shards["w_in"], comm)
        self.gathered["w_in"] = res[2]
        self.done("in_proj", res[3:])
        return res[1], res[0]

    def start_w_in(self, dw_in):
        *self.in_flight, token = _chunks_start(dw_in)
        return token[0, 0]

    def finish_w_in(self, after):
        src, land = _chunks_wait(*self.in_flight, after)
        me = _peers()[0]
        own = lax.dynamic_slice(src, (me, 0, 0), (1,) + src.shape[1:])
        return lax.dynamic_update_slice(land, own, (me, 0, 0))

    def comm(self, name, dw):
        if name in FWD_CARRY:
            return _Comm(replicated=[self.shards[k] for k in FWD_CARRY[name]])
        if name in BWD_CARRY:
            return _Comm(chunked=[dw[k].reshape((N_DEV,) + SHARD_SHAPE[k]) for k in BWD_CARRY[name]])
        return _NO_COMM

    def done(self, name, got):
        if name in FWD_CARRY:
            self.gathered.update(zip(FWD_CARRY[name], got))
        elif name in BWD_CARRY:
            self.parts.update(zip(BWD_CARRY[name], got))

    def w(self, key):
        g = self.gathered[key]
        if key in ("w_in", "w_up", "w_ckv"):
            return g
        if key in ("w_attn_proj", "w_conv_proj"):
            return _cols_to_2d(g)
        if key == "taps":
            return jnp.pad(_cols_to_2d(g[:, :CONV_K, :self.n_tap_cols]), ((0, 1), (0, 0)))
        return g.reshape(-1, g.shape[-1])


def _adamw(name, w, m, v, parts, comm=_NO_COMM):
    rows, cols = w.shape
    n_parts = parts.shape[0]
    rb = rows if rows <= 256 or rows % 256 else 256

    def body(w_ref, m_ref, v_ref, p_ref, g_ref, d_ref, nm_ref, nv_ref):
        g = p_ref[0].astype(F32)
        for q in range(1, n_parts):
            g = g + p_ref[q].astype(F32)
        wv = w_ref[...]
        nm = ADAM_B1 * m_ref[...] + (1.0 - ADAM_B1) * g
        nv = ADAM_B2 * v_ref[...] + (1.0 - ADAM_B2) * jnp.square(g)
        m_hat = nm / (1.0 - ADAM_B1 ** ADAM_STEP)
        v_hat = nv / (1.0 - ADAM_B2 ** ADAM_STEP)
        g_ref[...] = g
        d_ref[...] = -ADAM_LR * (m_hat / (jnp.sqrt(v_hat) + ADAM_EPS) + ADAM_WD * wv)
        nm_ref[...] = nm
        nv_ref[...] = nv

    blk = pl.BlockSpec((rb, cols), lambda i: (i, 0))
    return _pcall(body, name, (rows // rb,), [w, m, v, parts],
                  [blk, blk, blk, pl.BlockSpec((n_parts, rb, cols), lambda i: (0, i, 0))],
                  [jax.ShapeDtypeStruct((rows, cols), F32)] * 4, [blk] * 4, comm=comm)


def _sum_parts(name, parts):
    def body(p_ref, o_ref):
        acc = p_ref[0]
        for q in range(1, parts.shape[0]):
            acc = acc + p_ref[q]
        o_ref[...] = acc

    return _pcall(body, name, (1,), [parts], [pl.BlockSpec(parts.shape, lambda i: (0, 0, 0))],
                  [jax.ShapeDtypeStruct(parts.shape[1:], F32)], [pl.BlockSpec(parts.shape[1:], lambda i: (0, 0))])[0]


BIG = ("w_in", "w_attn_proj", "w_conv_proj", "w_out", "w_cq", "w_ckv", "w_co", "w_up", "w_down")
SMALL = ("g_mix", "b_gate", "conv_b", "conv_ln_g", "conv_ln_b", "g_cross", "g_mem", "g_mlp", "g_final")
SMALL_ORDER = SMALL + ("loss", "conv_w")
WEIGHTS = ("g_mix", "w_in", "b_gate", "conv_w", "conv_b", "conv_ln_g", "conv_ln_b", "w_attn_proj", "w_conv_proj", "w_out",
           "g_cross", "g_mem", "w_cq", "w_ckv", "w_co", "g_mlp", "w_up", "w_down", "g_final")


def kernel(x, mem, g_mix, w_in, b_gate, conv_w, conv_b, conv_ln_g, conv_ln_b, w_attn_proj, w_conv_proj, w_out, g_cross, g_mem, w_cq, w_ckv, w_co, g_mlp, w_up, w_down, g_final, loss_target, m_g_mix, m_w_in, m_b_gate, m_conv_w, m_conv_b, m_conv_ln_g, m_conv_ln_b, m_w_attn_proj, m_w_conv_proj, m_w_out, m_g_cross, m_g_mem, m_w_cq, m_w_ckv, m_w_co, m_g_mlp, m_w_up, m_w_down, m_g_final, v_g_mix, v_w_in, v_b_gate, v_conv_w, v_conv_b, v_conv_ln_g, v_conv_ln_b, v_w_attn_proj, v_w_conv_proj, v_w_out, v_g_cross, v_g_mem, v_w_cq, v_w_ckv, v_w_co, v_g_mlp, v_w_up, v_w_down, v_g_final):
    args = dict(locals())
    wts = {k: args[k] for k in WEIGHTS}
    mom = {k: args["m_" + k] for k in WEIGHTS}
    var = {k: args["v_" + k] for k in WEIGHTS}
    two_d = lambda a: a.reshape(a.shape[-2:]) if a.ndim == 3 else a.reshape(1, -1)

    shards = {k: two_d(wts[k]).astype(BF16) for k in BIG}
    shards["taps"] = jnp.pad(two_d(conv_w), ((0, 1), (0, HEAD_DIM - conv_w.shape[-1])))
    plan = _Plan(shards, conv_w.shape[-1])
    sm = {k: two_d(wts[k]) for k in SMALL}

    grad_x, _, small = _local_step(x[0], mem[0], loss_target[0], sm, plan)
    parts = plan.parts

    out = {}
    small_comm = _Comm(replicated=[small[k] for k in SMALL_ORDER])
    for k in BIG[1:]:
        res = _adamw("adamw_" + k, two_d(wts[k]), two_d(mom[k]), two_d(var[k]), parts[k],
                     comm=small_comm if k == BIG[1] else _NO_COMM)
        out[k] = [r.reshape(wts[k].shape) for r in res[:4]]
        if k == BIG[1]:
            small_parts = dict(zip(SMALL_ORDER, res[4:]))
    for k in SMALL:
        res = _adamw("adamw_" + k, two_d(wts[k]), two_d(mom[k]), two_d(var[k]), small_parts[k])
        out[k] = [r.reshape(wts[k].shape) for r in res]
    res = _adamw("adamw_w_in", two_d(w_in), two_d(m_w_in), two_d(v_w_in), plan.finish_w_in(after=out["g_final"][3]))
    out["w_in"] = [r.reshape(w_in.shape) for r in res]
    loss = _sum_parts("loss_sum", small_parts["loss"])[0, 0]
    me = 4 * lax.axis_index("x") + 2 * lax.axis_index("y") + lax.axis_index("c")
    n_tap_cols = conv_w.shape[-1]
    tap_parts = lax.dynamic_slice(small_parts["conv_w"], (0, 0, me * n_tap_cols), (N_DEV, CONV_K, n_tap_cols))
    res = _adamw("adamw_conv_w", two_d(conv_w), two_d(m_conv_w), two_d(v_conv_w), tap_parts)
    out["conv_w"] = [r.reshape(conv_w.shape) for r in res]

    return (loss, grad_x[None], *[out[k][0] for k in WEIGHTS], *[out[k][1] for k in WEIGHTS],
            *[out[k][2] for k in WEIGHTS], *[out[k][3] for k in WEIGHTS])
```

```python
import functools

import jax
import jax.numpy as jnp
import numpy as np
from jax import lax
from jax.experimental import pallas as pl
from jax.experimental.pallas import tpu as pltpu

F32 = jnp.float32
BF16 = jnp.bfloat16

N_DEV = 8
D_MODEL = 1024
N_MEM = 256
HEAD_DIM = 128
HEADS_PER_GROUP = 4
GROUP_W = HEADS_PER_GROUP * HEAD_DIM
DILATIONS = (1, 4, 16)
BAND = 128
N_GROUPS = 3
ATTN_W = N_GROUPS * GROUP_W
QKV_W = 3 * ATTN_W
ROT_DIM = HEAD_DIM // 4
ROPE_THETA = 500000.0
CONV_CH = 768
CONV_K = 31
CONV_HALO = 32
SUBLANES = 8
CONV_ROWS = 64
IN_W = 8192
GLU_COL_BLK = QKV_W // (2 * CONV_CH)
GATE_COL_BLK = (QKV_W + 2 * CONV_CH) // (2 * D_MODEL)
CROSS_HEADS = 4
CROSS_HD = D_MODEL // CROSS_HEADS
D_FF = 4096
EPS = 1e-6
NEG = -1e30
QB = 4
ROW_BLK = QB * BAND
ROPE_ROWS = 256

ADAM_LR = 0.001
ADAM_B1 = 0.9
ADAM_B2 = 0.999
ADAM_EPS = 1e-08
ADAM_WD = 0.01
ADAM_STEP = 10

VMEM_LIMIT = 56 * 1024 * 1024
MESH = pl.DeviceIdType.MESH


def _params(**kw):
    return pltpu.CompilerParams(vmem_limit_bytes=VMEM_LIMIT, **kw)


def _sigmoid(x):
    return 1.0 / (1.0 + jnp.exp(-x))


def _dot(a, b, kind):
    dims = {"nn": (((1,), (0,)), ((), ())), "nt": (((1,), (1,)), ((), ())), "tn": (((0,), (0,)), ((), ()))}[kind]
    if a.dtype != BF16:
        a = a.astype(BF16)
    if b.dtype != BF16:
        b = b.astype(BF16)
    return lax.dot_general(a, b, dims, preferred_element_type=F32)


def _peers():
    x, y, c = lax.axis_index("x"), lax.axis_index("y"), lax.axis_index("c")
    me = 4 * x + 2 * y + c
    peers = [(x, y, 1 - c), (1 - x, y, c), (x, 1 - y, c), (1 - x, 1 - y, c),
             (1 - x, y, 1 - c), (x, 1 - y, 1 - c), (1 - x, 1 - y, 1 - c)]
    return me, peers


class _Comm:
    def __init__(self, chunked=(), replicated=()):
        self.arrays = list(chunked) + list(replicated)
        self.n_c = len(chunked)
        self.n = len(self.arrays)
        self.out_shape = [jax.ShapeDtypeStruct(a.shape, a.dtype) for a in chunked]
        self.out_shape += [jax.ShapeDtypeStruct((N_DEV,) + a.shape, a.dtype) for a in replicated]
        self.in_specs = [pl.BlockSpec(memory_space=pl.ANY)] * self.n
        self.out_specs = [pl.BlockSpec(memory_space=pl.ANY)] * self.n
        self.scratch = [pltpu.SemaphoreType.DMA((self.n,))] * 5 if self.n else []

    @staticmethod
    def _where():
        x, y, c = lax.axis_index("x"), lax.axis_index("y"), lax.axis_index("c")
        chips = [(1 - x, y), (x, 1 - y), (1 - x, 1 - y)]
        return (x, y, c), 4 * x + 2 * y + c, chips

    def _local(self, ins, outs, sems, a, me):
        src = ins[a].at[me] if a < self.n_c else ins[a]
        return pltpu.make_async_copy(src, outs[a].at[me], sems[2].at[a])

    @staticmethod
    def _remote(src, dst, send, recv, to):
        return pltpu.make_async_remote_copy(src_ref=src, dst_ref=dst, send_sem=send, recv_sem=recv, device_id=to,
                                            device_id_type=MESH)

    def start(self, ins, outs, sems):
        (x, y, c), me, chips = self._where()
        for a in range(self.n):
            self._local(ins, outs, sems, a, me).start()
            if a < self.n_c:
                for (px, py, pc) in _peers()[1]:
                    self._remote(ins[a].at[4 * px + 2 * py + pc], outs[a].at[me], sems[0].at[a], sems[1].at[a], (px, py, pc)).start()
            else:
                self._remote(ins[a], outs[a].at[me], sems[3].at[a], sems[4].at[a], (x, y, 1 - c)).start()
                for (px, py) in chips:
                    self._remote(ins[a], outs[a].at[me], sems[0].at[a], sems[1].at[a], (px, py, c)).start()

    def wait(self, ins, outs, sems):
        (x, y, c), me, chips = self._where()
        sibling = (x, y, 1 - c)

        def drain(a, pair, count):
            blocks = outs[a].at[pl.ds(0, count)]
            cp = self._remote(blocks, blocks, sems[pair].at[a], sems[pair + 1].at[a], sibling)
            cp.wait_send()
            cp.wait_recv()

        for a in range(self.n):
            if a < self.n_c:
                drain(a, 0, N_DEV - 1)
            else:
                drain(a, 0, len(chips))
                for (px, py) in chips:
                    blk = outs[a].at[4 * px + 2 * py + c]
                    self._remote(blk, blk, sems[3].at[a], sems[4].at[a], sibling).start()
        for a in range(self.n):
            if a >= self.n_c:
                drain(a, 3, len(chips) + 1)
            self._local(ins, outs, sems, a, me).wait()


_NO_COMM = _Comm()


def _pcall(body, name, grid, operands, in_specs, out_shape, out_specs, scratch=(), aliases=None, comm=_NO_COMM, **params):
    n_in, n_out, n_scr = len(operands), len(out_shape), len(scratch)
    grid = tuple(grid)

    def carried(*refs):
        ins, c_in = refs[:n_in], refs[n_in:n_in + comm.n]
        o0 = n_in + comm.n
        outs, c_out = refs[o0:o0 + n_out], refs[o0 + n_out:o0 + n_out + comm.n]
        s0 = o0 + n_out + comm.n
        scr, sems = refs[s0:s0 + n_scr], refs[s0 + n_scr:]
        ids = [pl.program_id(ax) for ax in range(len(grid))]

        @pl.when(functools.reduce(jnp.logical_and, [p == 0 for p in ids]))
        def _():
            comm.start(c_in, c_out, sems)

        body(*ins, *outs, *scr)

        @pl.when(functools.reduce(jnp.logical_and, [p == g - 1 for p, g in zip(ids, grid)]))
        def _():
            comm.wait(c_in, c_out, sems)

    return pl.pallas_call(
        carried if comm.n else body, name=name, grid=grid, in_specs=list(in_specs) + comm.in_specs,
        out_shape=list(out_shape) + comm.out_shape, out_specs=list(out_specs) + comm.out_specs,
        scratch_shapes=list(scratch) + comm.scratch, input_output_aliases=aliases or {},
        compiler_params=_params(dimension_semantics=("arbitrary",) * len(grid), **params),
    )(*operands, *comm.arrays)


def _mm(name, a, b, kind, grid, a_blk, b_blk, outs, extras=(), epi=None, acc_outs=(), j_outer=False, comm=_NO_COMM,
        split=None, b_resident=False, out_chunks=0):
    gi, gj, gk = grid
    n_ex = len(extras)
    n_out = len(outs)
    mode, n_chunks = split if split is not None else (None, 1)

    def spec(blk, fn, **kw):
        return pl.BlockSpec(blk, (lambda j, i, k: fn(i, j, k)) if j_outer else fn, **kw)

    def b_chunk(b_ref, c):
        if len(b_ref.shape) == 3:
            return b_ref[c]
        rows, cols = b_ref.shape
        if (kind == "nn") == (mode == "cols"):
            return b_ref[:, c * (cols // n_chunks):(c + 1) * (cols // n_chunks)]
        return b_ref[c * (rows // n_chunks):(c + 1) * (rows // n_chunks), :]

    def col_chunk(ref, c):
        width = ref.shape[-1] // n_chunks
        return slice(c * width, (c + 1) * width)

    def body(*refs):
        a_ref, b_ref = refs[0], refs[1]
        ex = refs[2:2 + n_ex]
        out_refs = refs[2 + n_ex:2 + n_ex + n_out]
        acc_ref = refs[2 + n_ex + n_out] if gk > 1 else None
        i = pl.program_id(1 if j_outer else 0)
        k = pl.program_id(2)
        if mode == "cols":
            a_val = a_ref[...]
            for c in range(n_chunks):
                acc = _dot(a_val, b_chunk(b_ref, c), kind)
                vals = epi(acc, *[e[:, col_chunk(e, c)] for e in ex]) if epi is not None else (acc,)
                for o, v in zip(out_refs, vals):
                    o[:, col_chunk(o, c)] = v.astype(o.dtype)
            return
        if mode == "sum":
            part = _dot(a_ref[:, col_chunk(a_ref, 0)], b_chunk(b_ref, 0), kind)
            for c in range(1, n_chunks):
                part = part + _dot(a_ref[:, col_chunk(a_ref, c)], b_chunk(b_ref, c), kind)
        else:
            part = _dot(a_ref[...], b_ref[...], kind)

        def finish(acc):
            if out_chunks:
                width = acc.shape[-1] // out_chunks
                for c in range(out_chunks):
                    out_refs[0][c] = acc[:, c * width:(c + 1) * width].astype(out_refs[0].dtype)
                return
            vals = epi(acc, *[e[...] for e in ex]) if epi is not None else (acc,)
            for idx, (o, v) in enumerate(zip(out_refs, vals)):
                if idx in acc_outs:
                    @pl.when(i == 0)
                    def _():
                        o[...] = v.astype(o.dtype)

                    @pl.when(i != 0)
                    def _():
                        o[...] += v.astype(o.dtype)
                else:
                    o[...] = v.astype(o.dtype)

        if gk == 1:
            finish(part)
        else:
            @pl.when(k == 0)
            def _():
                acc_ref[...] = part

            @pl.when(k != 0)
            def _():
                acc_ref[...] += part

            @pl.when(k == gk - 1)
            def _():
                finish(acc_ref[...])

    scratch = []
    if gk > 1:
        tm = a_blk[0][-1] if kind == "tn" else a_blk[0][-2]
        tn = b_blk[0][-2] if kind == "nt" else b_blk[0][-1]
        scratch = [pltpu.VMEM((tm, tn), F32)]
    b_kw = dict(pipeline_mode=pl.Buffered(1)) if b_resident else {}
    return _pcall(body, name, (gj, gi, gk) if j_outer else (gi, gj, gk), [a, b] + [e for e, _, _ in extras],
                  [spec(*a_blk), spec(*b_blk, **b_kw)] + [spec(blk, fn) for _, blk, fn in extras],
                  [s for s, _, _ in outs], [spec(blk, fn) for _, blk, fn in outs], scratch, comm=comm)


def _rms_fwd_vals(x, g):
    r = lax.rsqrt(jnp.mean(x * x, axis=-1, keepdims=True) + EPS)
    return x * r * g


def _rms_bwd_vals(x, g, du):
    r = lax.rsqrt(jnp.mean(x * x, axis=-1, keepdims=True) + EPS)
    xh = x * r
    dxh = du * g
    dx = r * (dxh - xh * jnp.mean(dxh * xh, axis=-1, keepdims=True))
    return dx, jnp.sum(du * xh, axis=0, keepdims=True)


def _rms_fwd(name, x, g, rows, comm=_NO_COMM):
    n = x.shape[0]

    def body(x_ref, g_ref, o_ref):
        o_ref[...] = _rms_fwd_vals(x_ref[...], g_ref[...]).astype(BF16)

    return _pcall(body, name, (n // rows,), [x, g],
                  [pl.BlockSpec((rows, D_MODEL), lambda i: (i, 0)), pl.BlockSpec((1, D_MODEL), lambda i: (0, 0))],
                  [jax.ShapeDtypeStruct(x.shape, BF16)], [pl.BlockSpec((rows, D_MODEL), lambda i: (i, 0))], comm=comm)


def _rope_tables(seq):
    half = ROT_DIM // 2
    pos = np.arange(seq, dtype=np.float32)
    inv_freq = np.float32(ROPE_THETA) ** (-np.arange(0, ROT_DIM, 2, dtype=np.float32) / np.float32(ROT_DIM))
    ang = (pos[:, None] * inv_freq[None, :]).astype(np.float32)
    cos, sin = np.cos(ang), np.sin(ang)
    rest = HEAD_DIM - ROT_DIM
    c = np.concatenate([cos, cos, np.ones((seq, rest), np.float32)], axis=1)
    s1 = np.concatenate([np.zeros((seq, half), np.float32), sin, np.zeros((seq, rest), np.float32)], axis=1)
    s2 = np.concatenate([-sin, np.zeros((seq, half + rest), np.float32)], axis=1)
    return jnp.asarray(c), jnp.asarray(s1), jnp.asarray(s2)


def _group_shapes(seq, width, dtype):
    return [jax.ShapeDtypeStruct((d, seq // d, width), dtype) for d in DILATIONS]


def _group_specs(width):
    return [pl.BlockSpec((d, ROW_BLK // d, width), lambda i: (0, i, 0)) for d in DILATIONS]


def _qkv_prep(z, tabs, comm=_NO_COMM):
    seq = z.shape[0]

    def body(z_ref, c_ref, s1_ref, s2_ref, a0, a1, a2, sc):
        outs = (a0, a1, a2)
        for rc in range(ROW_BLK // ROPE_ROWS):
            rows = slice(rc * ROPE_ROWS, (rc + 1) * ROPE_ROWS)
            c, s1, s2 = c_ref[rows, :], s1_ref[rows, :], s2_ref[rows, :]
            for part in range(3):
                for hh in range(N_GROUPS * HEADS_PER_GROUP):
                    g, hl = divmod(hh, HEADS_PER_GROUP)
                    col = part * ATTN_W + hh * HEAD_DIM
                    ocol = part * GROUP_W + hl * HEAD_DIM
                    x = z_ref[rows, col:col + HEAD_DIM].astype(F32)
                    if part < 2:
                        x = x * c + pltpu.roll(x, ROT_DIM // 2, 1) * s1 + pltpu.roll(x, HEAD_DIM - ROT_DIM // 2, 1) * s2
                    d = DILATIONS[g]
                    if d == 1:
                        outs[g][0, rows, ocol:ocol + HEAD_DIM] = x.astype(BF16)
                    else:
                        sc[0:ROPE_ROWS, :] = x
                        per = ROPE_ROWS // d
                        for r in range(d):
                            outs[g][r, rc * per:(rc + 1) * per, ocol:ocol + HEAD_DIM] = sc[pl.ds(r, per, stride=d), :].astype(BF16)

    tab_spec = pl.BlockSpec((ROW_BLK, HEAD_DIM), lambda i: (i, 0))
    return _pcall(body, "qkv_prep", (seq // ROW_BLK,), [z, *tabs],
                  [pl.BlockSpec((ROW_BLK, QKV_W), lambda i: (i, 0)), tab_spec, tab_spec, tab_spec],
                  _group_shapes(seq, ATTN_W, BF16), _group_specs(ATTN_W), [pltpu.VMEM((ROW_BLK, HEAD_DIM), F32)], comm=comm)


def _band_masks_2(t):
    qi = lax.broadcasted_iota(jnp.int32, (BAND, 2 * BAND), 0)
    kj = lax.broadcasted_iota(jnp.int32, (BAND, 2 * BAND), 1)
    band = jnp.logical_and(kj >= qi, kj <= qi + BAND)
    return band, jnp.logical_and(band, jnp.logical_or(kj >= BAND, t > 0))


def _attn_fwd(name, a_g, comm=_NO_COMM):
    dil, m_len, _ = a_g.shape
    qb = min(QB, m_len // BAND)
    rows = qb * BAND
    steps = m_len // rows
    scale = HEAD_DIM ** -0.5

    tiles = [(sb, h) for sb in range(qb) for h in range(HEADS_PER_GROUP)]

    def body(q_ref, kc_ref, vc_ref, kp_ref, vp_ref, o_ref, l_ref, k_all, v_all, s_scr, p_scr, r_scr):
        t = pl.program_id(1)
        k_all[0:BAND, :] = kp_ref[...]
        k_all[BAND:, :] = kc_ref[...]
        v_all[0:BAND, :] = vp_ref[...]
        v_all[BAND:, :] = vc_ref[...]
        band, band_first = _band_masks_2(t)
        for idx, (sb, h) in enumerate(tiles):
            cs = slice(h * HEAD_DIM, (h + 1) * HEAD_DIM)
            s = _dot(q_ref[sb * BAND:(sb + 1) * BAND, cs], k_all[sb * BAND:(sb + 2) * BAND, cs], "nt") * scale
            s_scr[idx] = jnp.where(band_first if sb == 0 else band, s, NEG)
        lane = lax.broadcasted_iota(jnp.int32, (BAND, HEAD_DIM), 1)
        lse_rows = [jnp.zeros((BAND, HEAD_DIM), F32)] * qb
        for idx, (sb, h) in enumerate(tiles):
            s = s_scr[idx]
            mx = jnp.max(s, axis=-1, keepdims=True)
            p = jnp.exp(s - mx)
            den = jnp.sum(p, axis=-1, keepdims=True)
            p_scr[idx] = p.astype(BF16)
            r_scr[idx] = jnp.broadcast_to(1.0 / den, (BAND, HEAD_DIM))
            lse_rows[sb] = jnp.where(lane == h, jnp.broadcast_to(mx + jnp.log(den), (BAND, HEAD_DIM)), lse_rows[sb])
        for sb in range(qb):
            l_ref[sb * BAND:(sb + 1) * BAND, :] = lse_rows[sb]
        for idx, (sb, h) in enumerate(tiles):
            cs = slice(h * HEAD_DIM, (h + 1) * HEAD_DIM)
            o = _dot(p_scr[idx], v_all[sb * BAND:(sb + 2) * BAND, cs], "nn") * r_scr[idx]
            o_ref[sb * BAND:(sb + 1) * BAND, cs] = o.astype(BF16)

    def prev(r, t):
        return jnp.maximum(qb * t - 1, 0)

    cur = lambda c: pl.BlockSpec((None, rows, GROUP_W), lambda r, t, c=c: (r, t, c))
    prv = lambda c: pl.BlockSpec((None, BAND, GROUP_W), lambda r, t, c=c: (r, prev(r, t), c))
    out_spec = lambda width: pl.BlockSpec((None, rows, width), lambda r, t: (r, t, 0))
    shp = lambda width, dt: jax.ShapeDtypeStruct((dil, m_len, width), dt)
    n_t = len(tiles)
    return _pcall(body, name, (dil, steps), [a_g] * 5, [cur(0), cur(1), cur(2), prv(1), prv(2)],
                  [shp(GROUP_W, BF16), shp(HEAD_DIM, F32)], [out_spec(GROUP_W), out_spec(HEAD_DIM)],
                  [pltpu.VMEM((rows + BAND, GROUP_W), BF16), pltpu.VMEM((rows + BAND, GROUP_W), BF16),
                   pltpu.VMEM((n_t, BAND, 2 * BAND), F32), pltpu.VMEM((n_t, BAND, 2 * BAND), BF16),
                   pltpu.VMEM((n_t, BAND, HEAD_DIM), F32)], comm=comm)


def _attn_merge(os_, ls_, seq):
    def body(o0, l0, o1, l1, o2, l2, at_ref, lt_ref, sc, lsc):
        for gi, l_r in enumerate((l1, l2)):
            d = DILATIONS[gi + 1]
            for r in range(d):
                lsc.at[gi][pl.ds(r, ROW_BLK // d, stride=d), :] = l_r[r]
        lse = (l0.at[0], lsc.at[0], lsc.at[1])
        lane = lax.broadcasted_iota(jnp.int32, (ROW_BLK, HEAD_DIM), 1)
        lt_rows = jnp.zeros((ROW_BLK, HEAD_DIM), F32)
        for h in range(HEADS_PER_GROUP):
            cs = slice(h * HEAD_DIM, (h + 1) * HEAD_DIM)
            for gi, o_r in enumerate((o1, o2)):
                d = DILATIONS[gi + 1]
                for r in range(d):
                    sc.at[gi][pl.ds(r, ROW_BLK // d, stride=d), :] = o_r[r, :, cs].astype(F32)
            l_h = [v[:, h:h + 1] for v in lse]
            mx = jnp.maximum(jnp.maximum(l_h[0], l_h[1]), l_h[2])
            e = [jnp.exp(v - mx) for v in l_h]
            tot = e[0] + e[1] + e[2]
            inv = 1.0 / tot
            at_ref[:, cs] = ((e[0] * inv) * o0[0, :, cs].astype(F32) + (e[1] * inv) * sc[0] + (e[2] * inv) * sc[1]).astype(BF16)
            lt_rows = jnp.where(lane == h, jnp.broadcast_to(mx + jnp.log(tot), (ROW_BLK, HEAD_DIM)), lt_rows)
        lt_ref[...] = lt_rows

    go, gl = _group_specs(GROUP_W), _group_specs(HEAD_DIM)
    return pl.pallas_call(
        body, name="attn_merge",
        out_shape=[jax.ShapeDtypeStruct((seq, GROUP_W), BF16), jax.ShapeDtypeStruct((seq, HEAD_DIM), F32)],
        grid=(seq // ROW_BLK,), in_specs=[go[0], gl[0], go[1], gl[1], go[2], gl[2]],
        out_specs=[pl.BlockSpec((ROW_BLK, GROUP_W), lambda i: (i, 0)), pl.BlockSpec((ROW_BLK, HEAD_DIM), lambda i: (i, 0))],
        scratch_shapes=[pltpu.VMEM((2, ROW_BLK, HEAD_DIM), F32), pltpu.VMEM((2, ROW_BLK, HEAD_DIM), F32)],
        compiler_params=_params(dimension_semantics=("arbitrary",)),
    )(os_[0], ls_[0], os_[1], ls_[1], os_[2], ls_[2])


def _attn_bwd_prep(dattn, attn, lt):
    seq = dattn.shape[0]

    def body(da_ref, at_ref, lt_ref, cl0, d1, cl1, d2, cl2, sc, csc):
        lane = lax.broadcasted_iota(jnp.int32, (ROW_BLK, HEAD_DIM), 1)
        cl = pltpu.roll(lt_ref[...], HEADS_PER_GROUP, 1)
        for h in range(HEADS_PER_GROUP):
            cs = slice(h * HEAD_DIM, (h + 1) * HEAD_DIM)
            da = da_ref[:, cs].astype(F32)
            cc = jnp.sum(da * at_ref[:, cs].astype(F32), axis=-1, keepdims=True)
            cl = jnp.where(lane == h, jnp.broadcast_to(cc, (ROW_BLK, HEAD_DIM)), cl)
            sc[...] = da
            for g, d_ref in ((1, d1), (2, d2)):
                d = DILATIONS[g]
                for r in range(d):
                    d_ref[r, :, cs] = sc[pl.ds(r, ROW_BLK // d, stride=d), :].astype(BF16)
        cl0[0] = cl
        csc[...] = cl
        for g, c_ref in ((1, cl1), (2, cl2)):
            d = DILATIONS[g]
            for r in range(d):
                c_ref[r] = csc[pl.ds(r, ROW_BLK // d, stride=d), :]

    go, gl = _group_specs(GROUP_W), _group_specs(HEAD_DIM)
    row = lambda width: pl.BlockSpec((ROW_BLK, width), lambda i: (i, 0))
    shape = lambda g, width, dt: jax.ShapeDtypeStruct((DILATIONS[g], seq // DILATIONS[g], width), dt)
    cl0, d1, cl1, d2, cl2 = pl.pallas_call(
        body, name="attn_bwd_prep",
        out_shape=[shape(0, HEAD_DIM, F32), shape(1, GROUP_W, BF16), shape(1, HEAD_DIM, F32), shape(2, GROUP_W, BF16),
                   shape(2, HEAD_DIM, F32)],
        grid=(seq // ROW_BLK,), in_specs=[row(GROUP_W), row(GROUP_W), row(HEAD_DIM)],
        out_specs=[gl[0], go[1], gl[1], go[2], gl[2]],
        scratch_shapes=[pltpu.VMEM((ROW_BLK, HEAD_DIM), F32), pltpu.VMEM((ROW_BLK, HEAD_DIM), F32)],
        compiler_params=_params(dimension_semantics=("arbitrary",)),
    )(dattn, attn, lt)
    return [(dattn[None], cl0), (d1, cl1), (d2, cl2)]


def _attn_bwd(name, a_g, da_g, cl_g, comm=_NO_COMM):
    dil, m_len, _ = a_g.shape
    qb = min(QB, m_len // BAND)
    rows = qb * BAND
    steps = m_len // rows
    scale = HEAD_DIM ** -0.5

    tiles = [(sb, h) for sb in range(qb) for h in range(HEADS_PER_GROUP)]

    def body(q_ref, kc_ref, vc_ref, kp_ref, vp_ref, da_ref, cl_ref, d_ref, dk_acc, dv_acc, car_k, car_v,
             k_all, v_all, s_scr, dp_scr, p_scr, ds_scr):
        tg = pl.program_id(1)
        t = steps - 1 - tg

        @pl.when(tg == 0)
        def _():
            car_k[...] = jnp.zeros_like(car_k)
            car_v[...] = jnp.zeros_like(car_v)

        k_all[0:BAND, :] = kp_ref[...]
        k_all[BAND:, :] = kc_ref[...]
        v_all[0:BAND, :] = vp_ref[...]
        v_all[BAND:, :] = vc_ref[...]
        zero = jnp.zeros((rows, GROUP_W), F32)
        dk_acc[0:rows, :] = zero
        dv_acc[0:rows, :] = zero
        dk_acc[rows:rows + BAND, :] = car_k[...]
        dv_acc[rows:rows + BAND, :] = car_v[...]
        band, band_first = _band_masks_2(t)
        for idx, (sb, h) in enumerate(tiles):
            cs = slice(h * HEAD_DIM, (h + 1) * HEAD_DIM)
            rs, ks = slice(sb * BAND, (sb + 1) * BAND), slice(sb * BAND, (sb + 2) * BAND)
            s_scr[idx] = _dot(q_ref[rs, cs], k_all[ks, cs], "nt")
            dp_scr[idx] = _dot(da_ref[rs, cs], v_all[ks, cs], "nt")
        for idx, (sb, h) in enumerate(tiles):
            cs = slice(h * HEAD_DIM, (h + 1) * HEAD_DIM)
            rs = slice(sb * BAND, (sb + 1) * BAND)
            cc = jnp.broadcast_to(cl_ref[rs, h:h + 1], (BAND, 2 * BAND))
            ltv = jnp.broadcast_to(cl_ref[rs, HEADS_PER_GROUP + h:HEADS_PER_GROUP + h + 1], (BAND, 2 * BAND))
            p = jnp.exp(jnp.where(band_first if sb == 0 else band, s_scr[idx] * scale - ltv, NEG))
            p_scr[idx] = p.astype(BF16)
            ds_scr[idx] = (p * (dp_scr[idx] - cc) * scale).astype(BF16)
        for idx, (sb, h) in enumerate(tiles):
            cs = slice(h * HEAD_DIM, (h + 1) * HEAD_DIM)
            rs, ks = slice(sb * BAND, (sb + 1) * BAND), slice(sb * BAND, (sb + 2) * BAND)
            d_ref[rs, cs] = _dot(ds_scr[idx], k_all[ks, cs], "nn").astype(BF16)
            dk_acc[ks, cs] += _dot(ds_scr[idx], q_ref[rs, cs], "tn")
            dv_acc[ks, cs] += _dot(p_scr[idx], da_ref[rs, cs], "tn")
        d_ref[:, GROUP_W:2 * GROUP_W] = dk_acc[BAND:rows + BAND, :].astype(BF16)
        d_ref[:, 2 * GROUP_W:3 * GROUP_W] = dv_acc[BAND:rows + BAND, :].astype(BF16)
        car_k[...] = dk_acc[0:BAND, :]
        car_v[...] = dv_acc[0:BAND, :]

    def rev(tg):
        return steps - 1 - tg

    def prev(tg):
        return jnp.maximum(qb * rev(tg) - 1, 0)

    cur = lambda c: pl.BlockSpec((None, rows, GROUP_W), lambda r, tg, c=c: (r, rev(tg), c))
    prv = lambda c: pl.BlockSpec((None, BAND, GROUP_W), lambda r, tg, c=c: (r, prev(tg), c))
    return _pcall(
        body, name, (dil, steps), [a_g, a_g, a_g, a_g, a_g, da_g, cl_g],
        [cur(0), cur(1), cur(2), prv(1), prv(2), cur(0), pl.BlockSpec((None, rows, HEAD_DIM), lambda r, tg: (r, rev(tg), 0))],
        [jax.ShapeDtypeStruct((dil, m_len, ATTN_W), BF16)], [pl.BlockSpec((None, rows, ATTN_W), lambda r, tg: (r, rev(tg), 0))],
        [pltpu.VMEM((rows + BAND, GROUP_W), F32), pltpu.VMEM((rows + BAND, GROUP_W), F32),
         pltpu.VMEM((BAND, GROUP_W), F32), pltpu.VMEM((BAND, GROUP_W), F32),
         pltpu.VMEM((rows + BAND, GROUP_W), BF16), pltpu.VMEM((rows + BAND, GROUP_W), BF16),
         pltpu.VMEM((len(tiles), BAND, 2 * BAND), F32), pltpu.VMEM((len(tiles), BAND, 2 * BAND), F32),
         pltpu.VMEM((len(tiles), BAND, 2 * BAND), BF16), pltpu.VMEM((len(tiles), BAND, 2 * BAND), BF16)], comm=comm)


def _dqkv_post(d_gs, tabs, dz, comm=_NO_COMM):
    seq = dz.shape[0]

    def body(g0, g1, g2, c_ref, s1_ref, s2_ref, dz_any, o_ref, sc):
        del dz_any
        ins = (g0, g1, g2)
        for rc in range(ROW_BLK // ROPE_ROWS):
            rows = slice(rc * ROPE_ROWS, (rc + 1) * ROPE_ROWS)
            c, s1, s2 = c_ref[rows, :], s1_ref[rows, :], s2_ref[rows, :]
            for part in range(3):
                for hh in range(N_GROUPS * HEADS_PER_GROUP):
                    g, hl = divmod(hh, HEADS_PER_GROUP)
                    icol = part * GROUP_W + hl * HEAD_DIM
                    ocol = part * ATTN_W + hh * HEAD_DIM
                    d = DILATIONS[g]
                    if d == 1:
                        x = ins[g][0, rows, icol:icol + HEAD_DIM].astype(F32)
                    else:
                        per = ROPE_ROWS // d
                        for r in range(d):
                            sc[pl.ds(r, per, stride=d), :] = ins[g][r, rc * per:(rc + 1) * per, icol:icol + HEAD_DIM].astype(F32)
                        x = sc[0:ROPE_ROWS, :]
                    if part < 2:
                        x = x * c + pltpu.roll(x * s1, HEAD_DIM - ROT_DIM // 2, 1) + pltpu.roll(x * s2, ROT_DIM // 2, 1)
                    o_ref[rows, ocol:ocol + HEAD_DIM] = x.astype(BF16)

    tab_spec = pl.BlockSpec((ROW_BLK, HEAD_DIM), lambda i: (i, 0))
    return _pcall(body, "dqkv_post", (seq // ROW_BLK,), [*d_gs, *tabs, dz],
                  _group_specs(ATTN_W) + [tab_spec, tab_spec, tab_spec, pl.BlockSpec(memory_space=pl.ANY)],
                  [jax.ShapeDtypeStruct(dz.shape, BF16)], [pl.BlockSpec((ROW_BLK, QKV_W), lambda i: (i, 0))],
                  [pltpu.VMEM((ROW_BLK, HEAD_DIM), F32)], aliases={6: 0}, comm=comm)


def _glu(zg):
    a = zg[:, :CONV_CH].astype(F32)
    s = _sigmoid(zg[:, CONV_CH:].astype(F32))
    return a, s, a * s


def _shifted_copies(xs):
    n = xs.shape[1] - SUBLANES
    for b in range(1, SUBLANES):
        xs[b, 0:n, :] = xs[0, pl.ds(b, n), :]


def _shifted(xs, offset, r0, cs):
    a, b = divmod(offset, SUBLANES)
    return xs[b, pl.ds(SUBLANES * a + r0, CONV_ROWS), cs]


def _conv_fwd(z, cw, cb, lg, lb, comm=_NO_COMM):
    seq = z.shape[0]
    halo_per_blk = ROW_BLK // CONV_HALO

    def body(zg_ref, zh_ref, cw_ref, cb_ref, lg_ref, lb_ref, c2_ref, c4_ref, xs):
        i = pl.program_id(0)
        _, _, c1 = _glu(zg_ref[...])
        _, _, c1h = _glu(zh_ref[...])
        xs[0, 0:CONV_HALO, :] = jnp.where(i > 0, c1h, 0.0)
        xs[0, CONV_HALO:, :] = c1
        _shifted_copies(xs)
        for s in range(CONV_CH // HEAD_DIM):
            cs = slice(s * HEAD_DIM, (s + 1) * HEAD_DIM)
            taps = [cw_ref[j:j + 1, cs] for j in range(CONV_K)]
            bias = cb_ref[:, cs]

            def chunk(rc, carry, cs=cs, taps=taps, bias=bias):
                r0 = pl.multiple_of(rc * CONV_ROWS, CONV_ROWS)
                acc = [jnp.zeros((CONV_ROWS, HEAD_DIM), F32)] * 2
                for j in range(CONV_K):
                    acc[j % 2] = acc[j % 2] + taps[j] * _shifted(xs, CONV_HALO - (CONV_K - 1) + j, r0, cs)
                c2_ref[pl.ds(r0, CONV_ROWS), cs] = acc[0] + acc[1] + bias
                return carry

            lax.fori_loop(0, ROW_BLK // CONV_ROWS, chunk, 0)
        c2 = c2_ref[...]
        mu = jnp.mean(c2, axis=-1, keepdims=True)
        xc = c2 - mu
        rstd = lax.rsqrt(jnp.mean(xc * xc, axis=-1, keepdims=True) + EPS)
        c3 = xc * rstd * lg_ref[...] + lb_ref[...]
        c4_ref[...] = (c3 * _sigmoid(c3)).astype(BF16)

    vec = pl.BlockSpec((1, CONV_CH), lambda i: (0, 0))
    return _pcall(
        body, "conv_fwd", (seq // ROW_BLK,), [z, z, cw, cb, lg, lb],
        [pl.BlockSpec((ROW_BLK, 2 * CONV_CH), lambda i: (i, GLU_COL_BLK)),
         pl.BlockSpec((CONV_HALO, 2 * CONV_CH), lambda i: (jnp.maximum(i * halo_per_blk - 1, 0), GLU_COL_BLK)),
         pl.BlockSpec((CONV_HALO, CONV_CH), lambda i: (0, 0)), vec, vec, vec],
        [jax.ShapeDtypeStruct((seq, CONV_CH), F32), jax.ShapeDtypeStruct((seq, CONV_CH), BF16)],
        [pl.BlockSpec((ROW_BLK, CONV_CH), lambda i: (i, 0)), pl.BlockSpec((ROW_BLK, CONV_CH), lambda i: (i, 0))],
        [pltpu.VMEM((SUBLANES, ROW_BLK + CONV_HALO, CONV_CH), F32)], comm=comm)


def _conv_bwd(dc2, z, cw, dz, comm=_NO_COMM):
    seq = z.shape[0]
    halo_per_blk = ROW_BLK // CONV_HALO
    n_blk = seq // ROW_BLK
    last_halo = seq // CONV_HALO - 1

    def body(dc_ref, dn_ref, zg_ref, zh_ref, cw_ref, dz_any, o_ref, dcw_ref, xs, ys, dc1_ref, dcw_acc):
        del dz_any
        i = pl.program_id(0)
        a, s, c1 = _glu(zg_ref[...])
        _, _, c1h = _glu(zh_ref[...])
        xs[0, 0:CONV_HALO, :] = jnp.where(i > 0, c1h, 0.0)
        xs[0, CONV_HALO:, :] = c1
        ys[0, 0:ROW_BLK, :] = dc_ref[...]
        ys[0, ROW_BLK:, :] = jnp.where(i < n_blk - 1, dn_ref[...], 0.0)
        _shifted_copies(xs)
        _shifted_copies(ys)

        @pl.when(i == 0)
        def _():
            dcw_acc[...] = jnp.zeros_like(dcw_acc)

        for sl in range(CONV_CH // HEAD_DIM):
            cs = slice(sl * HEAD_DIM, (sl + 1) * HEAD_DIM)
            taps = [cw_ref[j:j + 1, cs] for j in range(CONV_K)]

            def chunk(rc, carry, cs=cs, taps=taps):
                r0 = pl.multiple_of(rc * CONV_ROWS, CONV_ROWS)
                dc = ys[0, pl.ds(r0, CONV_ROWS), cs]
                acc = [jnp.zeros((CONV_ROWS, HEAD_DIM), F32)] * 2
                for j in range(CONV_K):
                    prod = dc * _shifted(xs, CONV_HALO - (CONV_K - 1) + j, r0, cs)
                    dcw_acc[j, :, cs] += jnp.sum(prod.reshape(CONV_ROWS // SUBLANES, SUBLANES, HEAD_DIM), axis=0)
                    acc[j % 2] = acc[j % 2] + taps[j] * _shifted(ys, CONV_K - 1 - j, r0, cs)
                dc1_ref[pl.ds(r0, CONV_ROWS), cs] = acc[0] + acc[1]
                return carry

            lax.fori_loop(0, ROW_BLK // CONV_ROWS, chunk, 0)
        dc1 = dc1_ref[...]
        o_ref[:, :CONV_CH] = (dc1 * s).astype(BF16)
        o_ref[:, CONV_CH:] = (dc1 * a * s * (1.0 - s)).astype(BF16)

        @pl.when(i == n_blk - 1)
        def _():
            dcw_ref[...] = jnp.sum(dcw_acc[...], axis=1)

    return _pcall(
        body, "conv_bwd", (n_blk,), [dc2, dc2, z, z, cw, dz],
        [pl.BlockSpec((ROW_BLK, CONV_CH), lambda i: (i, 0)),
         pl.BlockSpec((CONV_HALO, CONV_CH), lambda i: (jnp.minimum((i + 1) * halo_per_blk, last_halo), 0)),
         pl.BlockSpec((ROW_BLK, 2 * CONV_CH), lambda i: (i, GLU_COL_BLK)),
         pl.BlockSpec((CONV_HALO, 2 * CONV_CH), lambda i: (jnp.maximum(i * halo_per_blk - 1, 0), GLU_COL_BLK)),
         pl.BlockSpec((CONV_HALO, CONV_CH), lambda i: (0, 0)),
         pl.BlockSpec(memory_space=pl.ANY)],
        [jax.ShapeDtypeStruct(dz.shape, BF16), jax.ShapeDtypeStruct((CONV_HALO, CONV_CH), F32)],
        [pl.BlockSpec((ROW_BLK, 2 * CONV_CH), lambda i: (i, GLU_COL_BLK)), pl.BlockSpec((CONV_HALO, CONV_CH), lambda i: (0, 0))],
        [pltpu.VMEM((SUBLANES, ROW_BLK + CONV_HALO, CONV_CH), F32), pltpu.VMEM((SUBLANES, ROW_BLK + CONV_HALO, CONV_CH), F32),
         pltpu.VMEM((ROW_BLK, CONV_CH), F32), pltpu.VMEM((CONV_HALO, SUBLANES, CONV_CH), F32)],
        aliases={5: 0}, comm=comm)


def _epi_mix(ya, c4, wcp, gates, bg):
    yc = _dot(c4, wcp, "nn")
    gv = _sigmoid(gates.astype(F32) + bg)
    merged = gv[:, :D_MODEL] * ya + gv[:, D_MODEL:] * yc
    return merged, ya, yc


def _epi_residual_rms(acc, xres, g):
    x = xres + acc
    return x, _rms_fwd_vals(x, g)


def _cross_scores(cq, ck):
    out = []
    for h in range(CROSS_HEADS):
        cs = slice(h * CROSS_HD, (h + 1) * CROSS_HD)
        s = _dot(cq[:, cs], ck[:, cs], "nt") * (CROSS_HD ** -0.5)
        e = jnp.exp(s - jnp.max(s, axis=-1, keepdims=True))
        out.append((cs, e, jnp.sum(e, axis=-1, keepdims=True)))
    return out


def _epi_cross_fwd(acc, ck, cv):
    cq = acc.astype(BF16)
    co = [_dot(e, cv[:, cs], "nn") / den for cs, e, den in _cross_scores(cq, ck)]
    return cq, jnp.concatenate(co, axis=1)


def _epi_cross_bwd(dco, cq, ck, cv):
    dco = dco.astype(BF16)
    dcq, dck, dcv = [], [], []
    for cs, e, den in _cross_scores(cq, ck):
        p = e / den
        dp = _dot(dco[:, cs], cv[:, cs], "nt")
        ds = (p * (dp - jnp.sum(dp * p, axis=-1, keepdims=True)) * (CROSS_HD ** -0.5)).astype(BF16)
        dcq.append(_dot(ds, ck[:, cs], "nn"))
        dck.append(_dot(ds, cq[:, cs], "tn"))
        dcv.append(_dot(p, dco[:, cs], "tn"))
    return jnp.concatenate(dcq, axis=1), jnp.concatenate(dck, axis=1), jnp.concatenate(dcv, axis=1)


def _epi_mlp_up(acc):
    return acc, jnp.square(jnp.maximum(acc, 0.0))


def _epi_final(acc, x2, tgt, g):
    x3 = x2 + acc
    err = _rms_fwd_vals(x3, g) - tgt
    loss = (0.5 / D_MODEL) * jnp.sum(err * err)
    dx3, dg = _rms_bwd_vals(x3, g, err * (1.0 / D_MODEL))
    return dx3, jnp.full((1, HEAD_DIM), loss, F32), dg


def _epi_mlp_down_bwd(dh, hpre):
    return (dh * 2.0 * jnp.maximum(hpre.astype(F32), 0.0),)


def _epi_rms_bwd(du, x, g, dres):
    dx, dg = _rms_bwd_vals(x, g, du)
    return dres.astype(F32) + dx, dg


def _epi_rms_bwd_g(du, x, g):
    return (_rms_bwd_vals(x, g, du)[1],)


def _epi_mix_bwd(dm, ya, yc, gates, bg):
    gv = _sigmoid(gates.astype(F32) + bg)
    ga, gb = gv[:, :D_MODEL], gv[:, D_MODEL:]
    ya, yc = ya.astype(F32), yc.astype(F32)
    dgate = jnp.concatenate([dm * ya * ga * (1.0 - ga), dm * yc * gb * (1.0 - gb)], axis=1)
    return dm * ga, dm * gb, dgate, jnp.sum(dgate, axis=0, keepdims=True)


def _epi_ln_bwd(dc4, c2, lg, lb):
    mu = jnp.mean(c2, axis=-1, keepdims=True)
    xc = c2 - mu
    rstd = lax.rsqrt(jnp.mean(xc * xc, axis=-1, keepdims=True) + EPS)
    xh = xc * rstd
    c3 = xh * lg + lb
    sg = _sigmoid(c3)
    dc3 = dc4 * sg * (1.0 + c3 * (1.0 - sg))
    dxh = dc3 * lg
    dc2 = rstd * (dxh - jnp.mean(dxh, axis=-1, keepdims=True) - xh * jnp.mean(dxh * xh, axis=-1, keepdims=True))
    return (dc2, jnp.sum(dc3 * xh, axis=0, keepdims=True), jnp.sum(dc3, axis=0, keepdims=True),
            jnp.sum(dc2, axis=0, keepdims=True))


def _sds(shape, dtype):
    return jax.ShapeDtypeStruct(shape, dtype)


class _Lazy:
    def __init__(self, fn):
        self.fn = fn

    def __getitem__(self, key):
        return self.fn(key)


def _local_step(x, mem, tgt, sm, plan):
    w = _Lazy(plan.w)
    dw = {}

    def carry(name, n_own, fn, *args, **kw):
        c = plan.comm(name, dw)
        res = fn(*args, comm=c, **kw)
        plan.done(name, res[n_own:])
        return res[:n_own]

    def mm(name, *args, **kw):
        return carry(name, len(args[6]), _mm, name, *args, **kw)

    seq = x.shape[0]
    nr = seq // ROW_BLK
    big = min(1024, seq)
    nb = seq // big
    row = lambda n: ((ROW_BLK, n), lambda i, j, k: (i, 0))
    vec = lambda n: ((1, n), lambda i, j, k: (0, 0))
    full = lambda r, c: ((r, c), lambda i, j, k: (0, 0))
    gates_blk = ((ROW_BLK, 2 * D_MODEL), lambda i, j, k: (i, GATE_COL_BLK))
    tabs = _rope_tables(seq)

    whole3 = lambda a: (a.shape, lambda i, j, k: (0, 0, 0))
    u, z = plan.project_in(x, sm["g_mix"])
    a_gs = carry("qkv_prep", 3, _qkv_prep, z, tabs)
    os_, ls_ = [], []
    for g in range(N_GROUPS):
        name = "attn_fwd_%d" % g
        o_g, l_g = carry(name, 2, _attn_fwd, name, a_gs[g])
        os_.append(o_g)
        ls_.append(l_g)
    attn, lt = _attn_merge(os_, ls_, seq)
    c2, c4 = carry("conv_fwd", 2, _conv_fwd, z, w["taps"], sm["conv_b"], sm["conv_ln_g"], sm["conv_ln_b"])
    merged, ya, yc = mm(
        "mix", attn, w["w_attn_proj"], "nn", (nr, 1, 1), row(GROUP_W), full(GROUP_W, D_MODEL),
        [(_sds((seq, D_MODEL), BF16), *row(D_MODEL))] * 3,
        extras=[(c4, *row(CONV_CH)), (w["w_conv_proj"], *full(CONV_CH, D_MODEL)), (z, *gates_blk), (sm["b_gate"], *vec(2 * D_MODEL))],
        epi=_epi_mix)
    x1, uq = mm("out_proj", merged, w["w_out"], "nn", (nr, 1, 1), row(D_MODEL), full(D_MODEL, D_MODEL),
                 [(_sds((seq, D_MODEL), F32), *row(D_MODEL)), (_sds((seq, D_MODEL), BF16), *row(D_MODEL))],
                 extras=[(x, *row(D_MODEL)), (sm["g_cross"], *vec(D_MODEL))], epi=_epi_residual_rms)

    mn = _rms_fwd("rms_mem", mem, sm["g_mem"], N_MEM)[0]
    ckv = mm("ckv_proj", mn, w["w_ckv"], "nn", (1, N_DEV, 1), full(N_MEM, D_MODEL),
              ((None, D_MODEL, 2 * D_MODEL // N_DEV), lambda i, j, k: (j, 0, 0)),
              [(_sds((N_MEM, 2 * D_MODEL), BF16), (N_MEM, 2 * D_MODEL // N_DEV), lambda i, j, k: (0, j))])[0]
    ck, cv = ckv[:, :D_MODEL], ckv[:, D_MODEL:]
    kv_blk = full(N_MEM, D_MODEL)
    cq, co = mm("cq_proj_cross", uq, w["w_cq"], "nn", (nr, 1, 1), row(D_MODEL), full(D_MODEL, D_MODEL),
                 [(_sds((seq, D_MODEL), BF16), *row(D_MODEL))] * 2,
                 extras=[(ck, *kv_blk), (cv, *kv_blk)], epi=_epi_cross_fwd)
    x2, um = mm("co_proj", co, w["w_co"], "nn", (nr, 1, 1), row(D_MODEL), full(D_MODEL, D_MODEL),
                 [(_sds((seq, D_MODEL), F32), *row(D_MODEL)), (_sds((seq, D_MODEL), BF16), *row(D_MODEL))],
                 extras=[(x1, *row(D_MODEL)), (sm["g_mlp"], *vec(D_MODEL))], epi=_epi_residual_rms)

    ff_blk = D_FF // N_DEV
    row_f32 = (_sds((seq, D_MODEL), F32), *row(D_MODEL))
    row_bf16 = (_sds((seq, D_MODEL), BF16), *row(D_MODEL))
    col_sum = (_sds((1, D_MODEL), F32), *vec(D_MODEL))
    hpre, h = mm("mlp_up", um, w["w_up"], "nn", (nr, 1, 1), row(D_MODEL), whole3(w["w_up"]),
                 [(_sds((seq, D_FF), BF16), *row(D_FF))] * 2, epi=_epi_mlp_up, split=("cols", N_DEV), b_resident=True)
    kt = D_FF // D_MODEL
    dx3, loss, dg_final = mm(
        "mlp_down_loss", h, w["w_down"], "nn", (nr, 1, 1), row(D_FF), full(D_FF, D_MODEL),
        [row_bf16, (_sds((1, HEAD_DIM), F32), *vec(HEAD_DIM)), col_sum],
        extras=[(x2, *row(D_MODEL)), (tgt, *row(D_MODEL)), (sm["g_final"], *vec(D_MODEL))], epi=_epi_final, acc_outs=(1, 2),
        b_resident=True)

    dhpre = mm("mlp_down_bwd", dx3, w["w_down"], "nt", (nr, 1, 1), row(D_MODEL), full(D_FF, D_MODEL),
               [(_sds((seq, D_FF), BF16), *row(D_FF))], extras=[(hpre, *row(D_FF))], epi=_epi_mlp_down_bwd,
               split=("cols", kt), b_resident=True)[0]
    big2 = min(2 * big, seq)
    nb2 = seq // big2
    dw["w_down"] = mm("dw_down", h, dx3, "tn", (kt, 1, nb2), ((big2, D_MODEL), lambda i, j, k: (k, i)),
                      ((big2, D_MODEL), lambda i, j, k: (k, 0)),
                      [(_sds((D_FF, D_MODEL), BF16), (D_MODEL, D_MODEL), lambda i, j, k: (i, 0))])[0]
    dx2, dg_mlp = mm("mlp_up_bwd", dhpre, w["w_up"], "nt", (nr, 1, 1), row(D_FF), whole3(w["w_up"]),
                     [row_bf16, col_sum],
                     extras=[(x2, *row(D_MODEL)), (sm["g_mlp"], *vec(D_MODEL)), (dx3, *row(D_MODEL))],
                     epi=_epi_rms_bwd, acc_outs=(1,), split=("sum", N_DEV), b_resident=True)
    dw["w_up"] = mm("dw_up", um, dhpre, "tn", (1, N_DEV, nb2), ((big2, D_MODEL), lambda i, j, k: (k, 0)),
                    ((big2, ff_blk), lambda i, j, k: (k, j)),
                    [(_sds((N_DEV, D_MODEL, ff_blk), BF16), (None, D_MODEL, ff_blk), lambda i, j, k: (j, 0, 0))])[0]

    acc_kv = (_sds((N_MEM, D_MODEL), F32), *kv_blk)
    dcq, dck, dcv = mm("co_proj_bwd_cross", dx2, w["w_co"], "nt", (nr, 1, 1), row(D_MODEL), full(D_MODEL, D_MODEL),
                       [row_bf16, acc_kv, acc_kv],
                       extras=[(cq, *row(D_MODEL)), (ck, *kv_blk), (cv, *kv_blk)], epi=_epi_cross_bwd, acc_outs=(1, 2))

    def dw_square(name, act, grad):
        return mm(name, act, grad, "tn", (1, 1, nb2), ((big2, D_MODEL), lambda i, j, k: (k, 0)),
                  ((big2, D_MODEL), lambda i, j, k: (k, 0)), [(_sds((D_MODEL, D_MODEL), BF16), *full(D_MODEL, D_MODEL))])[0]

    dw["w_co"] = dw_square("dw_co", co, dx2)
    dx1, dg_cross = mm("cq_proj_bwd", dcq, w["w_cq"], "nt", (nr, 1, 1), row(D_MODEL), full(D_MODEL, D_MODEL),
                       [row_bf16, col_sum],
                       extras=[(x1, *row(D_MODEL)), (sm["g_cross"], *vec(D_MODEL)), (dx2, *row(D_MODEL))],
                       epi=_epi_rms_bwd, acc_outs=(1,))
    dw["w_cq"] = dw_square("dw_cq", uq, dcq)
    dckv = jnp.concatenate([dck, dcv], axis=1)
    kv_chunk = 2 * D_MODEL // N_DEV
    dw["w_ckv"] = mm("dw_ckv", mn, dckv, "tn", (1, N_DEV, 1), full(N_MEM, D_MODEL), ((N_MEM, kv_chunk), lambda i, j, k: (0, j)),
                      [(_sds((N_DEV, D_MODEL, kv_chunk), BF16), (None, D_MODEL, kv_chunk), lambda i, j, k: (j, 0, 0))])[0]
    dg_mem = mm("ckv_proj_bwd", dckv, w["w_ckv"], "nt", (1, 1, N_DEV), ((N_MEM, kv_chunk), lambda i, j, k: (0, k)),
                 ((None, D_MODEL, kv_chunk), lambda i, j, k: (k, 0, 0)), [(_sds((1, D_MODEL), F32), *vec(D_MODEL))],
                 extras=[(mem, *full(N_MEM, D_MODEL)), (sm["g_mem"], *vec(D_MODEL))], epi=_epi_rms_bwd_g, acc_outs=(0,))[0]

    dya, dyc, dz, db_gate = mm(
        "out_proj_bwd_mix", dx1, w["w_out"], "nt", (nr, 1, 1), row(D_MODEL), full(D_MODEL, D_MODEL),
        [(_sds((seq, D_MODEL), BF16), *row(D_MODEL)), (_sds((seq, D_MODEL), BF16), *row(D_MODEL)),
         (_sds((seq, IN_W), BF16), *gates_blk), (_sds((1, 2 * D_MODEL), F32), *vec(2 * D_MODEL))],
        extras=[(ya, *row(D_MODEL)), (yc, *row(D_MODEL)), (z, *gates_blk), (sm["b_gate"], *vec(2 * D_MODEL))],
        epi=_epi_mix_bwd, acc_outs=(3,))
    dw["w_out"] = dw_square("dw_out", merged, dx1)
    dattn = mm("attn_proj_bwd", dya, w["w_attn_proj"], "nt", (nr, 1, 1), row(D_MODEL), full(GROUP_W, D_MODEL),
                [(_sds((seq, GROUP_W), BF16), *row(GROUP_W))])[0]
    pc = D_MODEL // N_DEV
    dw["w_attn_proj"] = mm("dw_attn_proj", attn, dya, "tn", (1, 1, nb2), ((big2, GROUP_W), lambda i, j, k: (k, 0)),
                           ((big2, D_MODEL), lambda i, j, k: (k, 0)),
                           [(_sds((N_DEV, GROUP_W, pc), BF16), (N_DEV, GROUP_W, pc), lambda i, j, k: (0, 0, 0))],
                           out_chunks=N_DEV)[0]
    cvec = (_sds((1, CONV_CH), F32), *vec(CONV_CH))
    dc2, dg_ln_g, dg_ln_b, dg_conv_b = mm(
        "conv_proj_bwd_ln", dyc, w["w_conv_proj"], "nt", (nr, 1, 1), row(D_MODEL), full(CONV_CH, D_MODEL),
        [(_sds((seq, CONV_CH), F32), *row(CONV_CH)), cvec, cvec, cvec],
        extras=[(c2, *row(CONV_CH)), (sm["conv_ln_g"], *vec(CONV_CH)), (sm["conv_ln_b"], *vec(CONV_CH))],
        epi=_epi_ln_bwd, acc_outs=(1, 2, 3))
    dw["w_conv_proj"] = mm("dw_conv_proj", c4, dyc, "tn", (1, 1, nb2), ((big2, CONV_CH), lambda i, j, k: (k, 0)),
                           ((big2, D_MODEL), lambda i, j, k: (k, 0)),
                           [(_sds((N_DEV, CONV_CH, pc), BF16), (N_DEV, CONV_CH, pc), lambda i, j, k: (0, 0, 0))],
                           out_chunks=N_DEV)[0]
    dz, dg_conv_w = carry("conv_bwd", 2, _conv_bwd, dc2, z, w["taps"], dz)
    preps = _attn_bwd_prep(dattn, attn, lt)
    d_gs = []
    for g in range(N_GROUPS):
        name = "attn_bwd_%d" % g
        d_gs.append(carry(name, 1, _attn_bwd, name, a_gs[g], *preps[g])[0])
    dz = carry("dqkv_post", 1, _dqkv_post, d_gs, tabs, dz)[0]
    dw["w_in"] = mm("dw_in", u, dz, "tn", (1, N_DEV, nb2), ((big2, D_MODEL), lambda i, j, k: (k, 0)),
                    ((big2, D_MODEL), lambda i, j, k: (k, j)),
                    [(_sds((N_DEV, D_MODEL, D_MODEL), BF16), (None, D_MODEL, D_MODEL), lambda i, j, k: (j, 0, 0))])[0]
    token = plan.start_w_in(dw["w_in"])
    grad_x, dg_mix = mm("in_proj_bwd", dz, w["w_in"], "nt", (nr, 1, 1), row(IN_W), whole3(w["w_in"]), [row_f32, col_sum],
                        extras=[(x, *row(D_MODEL)), (sm["g_mix"] + token, *vec(D_MODEL)), (dx1, *row(D_MODEL))],
                        epi=_epi_rms_bwd, acc_outs=(1,), split=("sum", N_DEV), b_resident=True)
    small = dict(g_mix=dg_mix, b_gate=db_gate, conv_b=dg_conv_b, conv_ln_g=dg_ln_g, conv_ln_b=dg_ln_b, g_cross=dg_cross,
                 g_mem=dg_mem, g_mlp=dg_mlp, g_final=dg_final, loss=loss, conv_w=dg_conv_w)
    return grad_x, dw, small


SHARD_SHAPE = dict(w_in=(1024, 1024), w_attn_proj=(512, 128), w_conv_proj=(768, 128), w_out=(128, 1024), w_cq=(128, 1024),
                   w_ckv=(1024, 256), w_co=(128, 1024), w_up=(1024, 512), w_down=(512, 1024))
FWD_CARRY = {"in_proj":("w_attn_proj", "w_conv_proj", "w_out", "w_cq", "w_ckv", "w_co", "taps"),
             "conv_fwd": ("w_up", "w_down")}
BWD_CARRY = {"dw_up": ("w_down",), "out_proj_bwd_mix": ("w_co", "w_cq"), "conv_bwd": ("w_up", "w_ckv"),
             "dqkv_post": ("w_out", "w_attn_proj", "w_conv_proj")}


def _cols_to_2d(a):
    return a.transpose(1, 0, 2).reshape(a.shape[1], -1)


def _in_proj_gather(resid, g, w_shard, comm):
    seq = resid.shape[0]
    tm = min(1024, seq)
    x, y, c = lax.axis_index("x"), lax.axis_index("y"), lax.axis_index("c")
    ident = lambda px, py, pc: 4 * px + 2 * py + pc
    far = [(1 - x, y), (x, 1 - y), (1 - x, 1 - y)]
    order = jnp.stack([ident(x, y, c), ident(x, y, 1 - c), ident(*far[0], c), ident(*far[1], c), ident(*far[0], 1 - c),
                       ident(*far[1], 1 - c), ident(*far[2], c), ident(*far[2], 1 - c)]).astype(jnp.int32)
    forward_at = {2: 0, 3: 1, 6: 2}
    n_far = len(far)

    def body(order_ref, x_ref, g_ref, wsh_ref, *rest):
        c_in, z_ref, u_ref, wg_ref = rest[:comm.n], rest[comm.n], rest[comm.n + 1], rest[comm.n + 2]
        c_out = rest[comm.n + 3:2 * comm.n + 3]
        wbuf, u_all, load_sem, local_sem, recv_sems, ici_send, d2d_send = rest[2 * comm.n + 3:2 * comm.n + 10]
        sems = rest[2 * comm.n + 10:]
        jj, i = pl.program_id(0), pl.program_id(1)
        (kx, ky, kc), me, chips = _Comm._where()
        sibling = (kx, ky, 1 - kc)
        n = order_ref[jj]

        def push(src, blk, send, to):
            return pltpu.make_async_remote_copy(src_ref=src, dst_ref=wg_ref.at[blk], send_sem=send,
                                                recv_sem=recv_sems.at[blk], device_id=to, device_id_type=MESH)

        def load(src):
            cp = pltpu.make_async_copy(src, wbuf, load_sem)
            cp.start()
            cp.wait()

        @pl.when(jnp.logical_and(jj == 0, i == 0))
        def _():
            push(wsh_ref, me, d2d_send, sibling).start()
            for (px, py) in chips[:2]:
                push(wsh_ref, me, ici_send, (px, py, kc)).start()
            pltpu.make_async_copy(wsh_ref, wg_ref.at[me], local_sem).start()
            load(wsh_ref)

        @pl.when(jnp.logical_and(jj > 0, i == 0))
        def _():
            push(wg_ref.at[n], n, d2d_send, sibling).wait_recv()
            for step, k in forward_at.items():
                @pl.when(jj == step)
                def _(k=k):
                    blk = 4 * chips[k][0] + 2 * chips[k][1] + kc
                    push(wg_ref.at[blk], blk, d2d_send, sibling).start()

            @pl.when(jj == 2)
            def _():
                push(wsh_ref, me, ici_send, (*chips[2], kc)).start()

            if comm.n:
                @pl.when(jj == 3)
                def _():
                    comm.start(c_in, c_out, sems)

            load(wg_ref.at[n])

        rows = pl.ds(pl.multiple_of(i * tm, tm), tm)

        @pl.when(jj == 0)
        def _():
            u_val = _rms_fwd_vals(x_ref[...], g_ref[...]).astype(BF16)
            u_all[rows, :] = u_val
            u_ref[...] = u_val
            z_ref[...] = _dot(u_val, wbuf[...], "nn").astype(BF16)

        @pl.when(jj > 0)
        def _():
            z_ref[...] = _dot(u_all[rows, :], wbuf[...], "nn").astype(BF16)

        @pl.when(jnp.logical_and(jj == N_DEV - 1, i == pl.num_programs(1) - 1))
        def _():
            def drain_sends(send, count):
                blocks = wg_ref.at[pl.ds(0, count)]
                pltpu.make_async_remote_copy(src_ref=blocks, dst_ref=blocks, send_sem=send, recv_sem=recv_sems.at[0],
                                             device_id=sibling, device_id_type=MESH).wait_send()

            drain_sends(ici_send, n_far)
            drain_sends(d2d_send, n_far + 1)
            pltpu.make_async_copy(wsh_ref, wg_ref.at[me], local_sem).wait()
            if comm.n:
                comm.wait(c_in, c_out, sems)

    any_spec = pl.BlockSpec(memory_space=pl.ANY)
    n_i = seq // tm
    first_pass = lambda jj, i, order_ref: (jnp.where(jj == 0, i, n_i - 1), 0)
    grid_spec = pltpu.PrefetchScalarGridSpec(
        num_scalar_prefetch=1, grid=(N_DEV, n_i),
        in_specs=[pl.BlockSpec((tm, D_MODEL), first_pass), pl.BlockSpec((1, D_MODEL), lambda jj, i, order_ref: (0, 0)),
                  any_spec] + comm.in_specs,
        out_specs=[pl.BlockSpec((tm, D_MODEL), lambda jj, i, order_ref: (i, order_ref[jj])),
                   pl.BlockSpec((tm, D_MODEL), first_pass), any_spec] + comm.out_specs,
        scratch_shapes=[pltpu.VMEM((D_MODEL, D_MODEL), BF16), pltpu.VMEM((seq, D_MODEL), BF16), pltpu.SemaphoreType.DMA,
                        pltpu.SemaphoreType.DMA, pltpu.SemaphoreType.DMA((N_DEV,)), pltpu.SemaphoreType.DMA,
                        pltpu.SemaphoreType.DMA] + comm.scratch)
    return pl.pallas_call(
        body, name="in_proj_gather", grid_spec=grid_spec,
        out_shape=[jax.ShapeDtypeStruct((seq, IN_W), BF16), jax.ShapeDtypeStruct((seq, D_MODEL), BF16),
                   jax.ShapeDtypeStruct((N_DEV, D_MODEL, D_MODEL), BF16)] + comm.out_shape,
        compiler_params=_params(dimension_semantics=("arbitrary", "arbitrary")),
    )(order, resid, g, w_shard, *comm.arrays)


_HBM = pl.BlockSpec(memory_space=pltpu.HBM)
_SEM = pl.BlockSpec(memory_space=pltpu.SEMAPHORE)


def _chunks_start(dw):
    def body(src_ref, land_ref, send_sem, recv_sem, src_thru, land_thru, token):
        del src_thru, land_thru
        me, peers = _peers()
        for (px, py, pc) in peers:
            pltpu.make_async_remote_copy(src_ref=src_ref.at[4 * px + 2 * py + pc], dst_ref=land_ref.at[me], send_sem=send_sem,
                                         recv_sem=recv_sem, device_id=(px, py, pc), device_id_type=MESH).start()
        token[...] = jnp.zeros_like(token)

    return pl.pallas_call(
        body, name="w_in_grad_start",
        out_shape=(pltpu.SemaphoreType.DMA(()), pltpu.SemaphoreType.DMA(()), pltpu.HBM(dw.shape, dw.dtype),
                   pltpu.HBM(dw.shape, dw.dtype), jax.ShapeDtypeStruct((SUBLANES, HEAD_DIM), F32)),
        in_specs=(_HBM, _HBM), out_specs=(_SEM, _SEM, _HBM, _HBM, pl.BlockSpec(memory_space=pltpu.VMEM)),
        input_output_aliases={0: 2, 1: 3},
        compiler_params=pltpu.CompilerParams(has_side_effects=pltpu.SideEffectType.DATAFLOW_SIDE_EFFECTING),
    )(pltpu.with_memory_space_constraint(dw, pltpu.HBM),
      pltpu.with_memory_space_constraint(lax.empty(dw.shape, dw.dtype), pltpu.HBM))


def _chunks_wait(send_sem, recv_sem, src_thru, land_thru, after):
    def body(src_ref, land_ref, send_sem, recv_sem, after_ref, src_dead, land_out):
        del after_ref, src_dead, land_out
        seven = land_ref.at[pl.ds(0, N_DEV - 1)]
        cp = pltpu.make_async_remote_copy(src_ref=seven, dst_ref=seven, send_sem=send_sem, recv_sem=recv_sem,
                                          device_id=_peers()[1][0], device_id_type=MESH)
        cp.wait_send()
        cp.wait_recv()

    return pl.pallas_call(
        body, name="w_in_grad_wait",
        out_shape=(pltpu.HBM(src_thru.shape, src_thru.dtype), pltpu.HBM(land_thru.shape, land_thru.dtype)),
        in_specs=(_HBM, _HBM, _SEM, _SEM, pl.BlockSpec(memory_space=pl.ANY)), out_specs=(_HBM, _HBM),
        input_output_aliases={0: 0, 1: 1},
        compiler_params=pltpu.CompilerParams(has_side_effects=pltpu.SideEffectType.DATAFLOW_SIDE_EFFECTING),
    )(src_thru, land_thru, send_sem, recv_sem, after)


class _Plan:
    def __init__(self, shards, n_tap_cols):
        self.shards = shards
        self.gathered = {}
        self.parts = {}
        self.n_tap_cols = n_tap_cols

    def project_in(self, x, g):
        comm = self.comm("in_proj", None)
        res = _in_proj_gather(x, g, self.shards["w_in"], comm)
        self.gathered["w_in"] = res[2]
        self.done("in_proj", res[3:])
        return res[1], res[0]

    def start_w_in(self, dw_in):
        *self.in_flight, token = _chunks_start(dw_in)
        return token[0, 0]

    def finish_w_in(self, after):
        src, land = _chunks_wait(*self.in_flight, after)
        me = _peers()[0]
        own = lax.dynamic_slice(src, (me, 0, 0), (1,) + src.shape[1:])
        return lax.dynamic_update_slice(land, own, (me, 0, 0))

    def comm(self, name, dw):
        if name in FWD_CARRY:
            return _Comm(replicated=[self.shards[k] for k in FWD_CARRY[name]])
        if name in BWD_CARRY:
            return _Comm(chunked=[dw[k].reshape((N_DEV,) + SHARD_SHAPE[k]) for k in BWD_CARRY[name]])
        return _NO_COMM

    def done(self, name, got):
        if name in FWD_CARRY:
            self.gathered.update(zip(FWD_CARRY[name], got))
        elif name in BWD_CARRY:
            self.parts.update(zip(BWD_CARRY[name], got))

    def w(self, key):
        g = self.gathered[key]
        if key in ("w_in", "w_up", "w_ckv"):
            return g
        if key in ("w_attn_proj", "w_conv_proj"):
            return _cols_to_2d(g)
        if key == "taps":
            return jnp.pad(_cols_to_2d(g[:, :CONV_K, :self.n_tap_cols]), ((0, 1), (0, 0)))
        return g.reshape(-1, g.shape[-1])


def _adamw(name, w, m, v, parts, comm=_NO_COMM):
    rows, cols = w.shape
    n_parts = parts.shape[0]
    rb = rows if rows <= 256 or rows % 256 else 256

    def body(w_ref, m_ref, v_ref, p_ref, g_ref, d_ref, nm_ref, nv_ref):
        g = p_ref[0].astype(F32)
        for q in range(1, n_parts):
            g = g + p_ref[q].astype(F32)
        wv = w_ref[...]
        nm = ADAM_B1 * m_ref[...] + (1.0 - ADAM_B1) * g
        nv = ADAM_B2 * v_ref[...] + (1.0 - ADAM_B2) * jnp.square(g)
        m_hat = nm / (1.0 - ADAM_B1 ** ADAM_STEP)
        v_hat = nv / (1.0 - ADAM_B2 ** ADAM_STEP)
        g_ref[...] = g
        d_ref[...] = -ADAM_LR * (m_hat / (jnp.sqrt(v_hat) + ADAM_EPS) + ADAM_WD * wv)
        nm_ref[...] = nm
        nv_ref[...] = nv

    blk = pl.BlockSpec((rb, cols), lambda i: (i, 0))
    return _pcall(body, name, (rows // rb,), [w, m, v, parts],
                  [blk, blk, blk, pl.BlockSpec((n_parts, rb, cols), lambda i: (0, i, 0))],
                  [jax.ShapeDtypeStruct((rows, cols), F32)] * 4, [blk] * 4, comm=comm)


def _sum_parts(name, parts):
    def body(p_ref, o_ref):
        acc = p_ref[0]
        for q in range(1, parts.shape[0]):
            acc = acc + p_ref[q]
        o_ref[...] = acc

    return _pcall(body, name, (1,), [parts], [pl.BlockSpec(parts.shape, lambda i: (0, 0, 0))],
                  [jax.ShapeDtypeStruct(parts.shape[1:], F32)], [pl.BlockSpec(parts.shape[1:], lambda i: (0, 0))])[0]


BIG = ("w_in", "w_attn_proj", "w_conv_proj", "w_out", "w_cq", "w_ckv", "w_co", "w_up", "w_down")
SMALL = ("g_mix", "b_gate", "conv_b", "conv_ln_g", "conv_ln_b", "g_cross", "g_mem", "g_mlp", "g_final")
SMALL_ORDER = SMALL + ("loss", "conv_w")
WEIGHTS = ("g_mix", "w_in", "b_gate", "conv_w", "conv_b", "conv_ln_g", "conv_ln_b", "w_attn_proj", "w_conv_proj", "w_out",
           "g_cross", "g_mem", "w_cq", "w_ckv", "w_co", "g_mlp", "w_up", "w_down", "g_final")


def kernel(x, mem, g_mix, w_in, b_gate, conv_w, conv_b, conv_ln_g, conv_ln_b, w_attn_proj, w_conv_proj, w_out, g_cross, g_mem, w_cq, w_ckv, w_co, g_mlp, w_up, w_down, g_final, loss_target, m_g_mix, m_w_in, m_b_gate, m_conv_w, m_conv_b, m_conv_ln_g, m_conv_ln_b, m_w_attn_proj, m_w_conv_proj, m_w_out, m_g_cross, m_g_mem, m_w_cq, m_w_ckv, m_w_co, m_g_mlp, m_w_up, m_w_down, m_g_final, v_g_mix, v_w_in, v_b_gate, v_conv_w, v_conv_b, v_conv_ln_g, v_conv_ln_b, v_w_attn_proj, v_w_conv_proj, v_w_out, v_g_cross, v_g_mem, v_w_cq, v_w_ckv, v_w_co, v_g_mlp, v_w_up, v_w_down, v_g_final):
    args = dict(locals())
    wts = {k: args[k] for k in WEIGHTS}
    mom = {k: args["m_" + k] for k in WEIGHTS}
    var = {k: args["v_" + k] for k in WEIGHTS}
    two_d = lambda a: a.reshape(a.shape[-2:]) if a.ndim == 3 else a.reshape(1, -1)

    shards = {k: two_d(wts[k]).astype(BF16) for k in BIG}
    shards["taps"] = jnp.pad(two_d(conv_w), ((0, 1), (0, HEAD_DIM - conv_w.shape[-1])))
    plan = _Plan(shards, conv_w.shape[-1])
    sm = {k: two_d(wts[k]) for k in SMALL}

    grad_x, _, small = _local_step(x[0], mem[0], loss_target[0], sm, plan)
    parts = plan.parts

    out = {}
    small_comm = _Comm(replicated=[small[k] for k in SMALL_ORDER])
    for k in BIG[1:]:
        res = _adamw("adamw_" + k, two_d(wts[k]), two_d(mom[k]), two_d(var[k]), parts[k],
                     comm=small_comm if k == BIG[1] else _NO_COMM)
        out[k] = [r.reshape(wts[k].shape) for r in res[:4]]
        if k == BIG[1]:
            small_parts = dict(zip(SMALL_ORDER, res[4:]))
    for k in SMALL:
        res = _adamw("adamw_" + k, two_d(wts[k]), two_d(mom[k]), two_d(var[k]), small_parts[k])
        out[k] = [r.reshape(wts[k].shape) for r in res]
    res = _adamw("adamw_w_in", two_d(w_in), two_d(m_w_in), two_d(v_w_in), plan.finish_w_in(after=out["g_final"][3]))
    out["w_in"] = [r.reshape(w_in.shape) for r in res]
    loss = _sum_parts("loss_sum", small_parts["loss"])[0, 0]
    me = 4 * lax.axis_index("x") + 2 * lax.axis_index("y") + lax.axis_index("c")
    n_tap_cols = conv_w.shape[-1]
    tap_parts = lax.dynamic_slice(small_parts["conv_w"], (0, 0, me * n_tap_cols), (N_DEV, CONV_K, n_tap_cols))
    res = _adamw("adamw_conv_w", two_d(conv_w), two_d(m_conv_w), two_d(v_conv_w), tap_parts)
    out["conv_w"] = [r.reshape(conv_w.shape) for r in res]

    return (loss, grad_x[None], *[out[k][0] for k in WEIGHTS], *[out[k][1] for k in WEIGHTS],
            *[out[k][2] for k in WEIGHTS], *[out[k][3] for k in WEIGHTS])
```
